```python
import math
import jax, jax.numpy as jnp
from jax import lax
import numpy as np

D_MODEL = 1024
BATCH = 16
SEQ = 2048
DEPTH = 1

CTX_LEN = 256
GRID_W = 64
D_MIX = D_MODEL
D_ATTN = D_MIX // 2
D_CONV = D_MIX - D_ATTN
HEAD_DIM = 64
N_Q_HEADS = D_ATTN // HEAD_DIM
N_KV_HEADS = 2
GROUP = N_Q_HEADS // N_KV_HEADS
ROPE_AXIS_DIM = HEAD_DIM // 2
ROPE_THETA = 10000.0
Q_BLOCK = 128
CONV_WIDTH = 31
CONV_PAD = (CONV_WIDTH - 1) // 2
EPS = 1e-6

Q_LO, Q_HI = 0, D_ATTN
K_LO, K_HI = Q_HI, Q_HI + N_KV_HEADS * HEAD_DIM
V_LO, V_HI = K_HI, K_HI + N_KV_HEADS * HEAD_DIM
ZA_LO, ZA_HI = V_HI, V_HI + D_ATTN
GLU_LO, GLU_HI = ZA_HI, ZA_HI + 2 * D_CONV
ZC_LO, ZC_HI = GLU_HI, GLU_HI + D_CONV
D_IN = ZC_HI

kernel_name = "hybrid_gqa_conformer_prefix_dit_layer"


def rms_norm(x, w):
    xf = x.astype(jnp.float32)
    y = xf * lax.rsqrt(jnp.mean(xf * xf, axis=-1, keepdims=True) + EPS)
    return (y * w.astype(jnp.float32)).astype(x.dtype)


def layer_norm(x, w, b):
    xf = x.astype(jnp.float32)
    mu = jnp.mean(xf, axis=-1, keepdims=True)
    var = jnp.mean(jnp.square(xf - mu), axis=-1, keepdims=True)
    y = (xf - mu) * lax.rsqrt(var + EPS)
    return (y * w.astype(jnp.float32) + b.astype(jnp.float32)).astype(x.dtype)


def grid_angles(n_tokens, dtype):
    rows = n_tokens // GRID_W
    row = jnp.repeat(jnp.arange(rows, dtype=jnp.float32), GRID_W)
    col = jnp.tile(jnp.arange(GRID_W, dtype=jnp.float32), rows)
    freqs = ROPE_THETA ** (-jnp.arange(0, ROPE_AXIS_DIM, 2, dtype=jnp.float32) / ROPE_AXIS_DIM)
    ang_r = row[:, None] * freqs[None, :]
    ang_c = col[:, None] * freqs[None, :]
    return (jnp.cos(ang_r).astype(dtype), jnp.sin(ang_r).astype(dtype),
            jnp.cos(ang_c).astype(dtype), jnp.sin(ang_c).astype(dtype))


def rope_axis(x, cos, sin):
    cos = cos[None, :, None, :]
    sin = sin[None, :, None, :]
    x1, x2 = jnp.split(x, 2, axis=-1)
    return jnp.concatenate([x1 * cos - x2 * sin, x1 * sin + x2 * cos], axis=-1)


def rope_2d(x, angles):
    cr, sr, cc, sc = angles
    return jnp.concatenate([rope_axis(x[..., :ROPE_AXIS_DIM], cr, sr),
                            rope_axis(x[..., ROPE_AXIS_DIM:], cc, sc)], axis=-1)


def qk_heads(p, qn_w, kn_w):
    B, S, _ = p.shape
    q = p[..., Q_LO:Q_HI].reshape(B, S, N_Q_HEADS, HEAD_DIM)
    k = p[..., K_LO:K_HI].reshape(B, S, N_KV_HEADS, HEAD_DIM)
    v = p[..., V_LO:V_HI].reshape(B, S, N_KV_HEADS, HEAD_DIM)
    return rms_norm(q, qn_w), rms_norm(k, kn_w), v


def kv_heads(p_kv, kn_w):
    B, S, _ = p_kv.shape
    kv = p_kv.reshape(B, S, 2, N_KV_HEADS, HEAD_DIM)
    return rms_norm(kv[:, :, 0], kn_w), kv[:, :, 1]


def attend_blocked(q, k, v):
    B, S, _, _ = q.shape
    scale = 1.0 / math.sqrt(HEAD_DIM)
    qb = q.reshape(B, S // Q_BLOCK, Q_BLOCK, N_KV_HEADS, GROUP, HEAD_DIM).transpose(1, 0, 2, 3, 4, 5)

    def one_block(qblk):
        s = jnp.einsum('bqkgd,bskd->bkgqs', qblk, k).astype(jnp.float32) * scale
        pr = jax.nn.softmax(s, axis=-1).astype(v.dtype)
        return jnp.einsum('bkgqs,bskd->bqkgd', pr, v)

    o = lax.map(one_block, qb)
    return o.transpose(1, 0, 2, 3, 4, 5).reshape(B, S, D_ATTN)


def attend_dense(q, k, v):
    B, C, _, _ = q.shape
    qg = q.reshape(B, C, N_KV_HEADS, GROUP, HEAD_DIM)
    s = jnp.einsum('bqkgd,bskd->bkgqs', qg, k).astype(jnp.float32) / math.sqrt(HEAD_DIM)
    pr = jax.nn.softmax(s, axis=-1).astype(v.dtype)
    return jnp.einsum('bkgqs,bskd->bqkgd', pr, v).reshape(B, C, D_ATTN)


def conformer_conv(glu_in, conv_w, conv_b, ln_w, ln_b, w_pw, b_pw):
    a, g = jnp.split(glu_in, 2, axis=-1)
    u = a * jax.nn.sigmoid(g)
    y = lax.conv_general_dilated(u, conv_w[:, None, :].astype(u.dtype), window_strides=(1,),
                                 padding=[(CONV_PAD, CONV_PAD)],
                                 dimension_numbers=('NWC', 'WIO', 'NWC'),
                                 feature_group_count=D_CONV) + conv_b
    y = jax.nn.silu(layer_norm(y, ln_w, ln_b))
    return y @ w_pw + b_pw


def _fwd_setup_inputs(seed: int = 0) -> dict:
    key = jax.random.key(seed)
    ks = jax.random.split(key, 20)
    nrm = jax.random.normal
    f32 = jnp.float32
    return {
        "x": nrm(ks[0], (BATCH, SEQ, D_MODEL), f32),
        "c": nrm(ks[1], (BATCH, D_MODEL), f32),
        "ctx": nrm(ks[2], (BATCH, CTX_LEN, D_MODEL), f32),
        "c_ctx": nrm(ks[3], (D_MODEL,), f32),
        "w_mod": nrm(ks[4], (DEPTH, D_MODEL, 3 * D_MODEL), f32) * D_MODEL ** -0.5,
        "b_mod": nrm(ks[5], (DEPTH, 3 * D_MODEL), f32) * 0.02,
        "norm_w": 1.0 + 0.05 * nrm(ks[6], (DEPTH, D_MODEL), f32),
        "w_in": nrm(ks[7], (DEPTH, D_MODEL, D_IN), f32) * D_MODEL ** -0.5,
        "q_norm_w": 1.0 + 0.05 * nrm(ks[8], (DEPTH, HEAD_DIM), f32),
        "k_norm_w": 1.0 + 0.05 * nrm(ks[9], (DEPTH, HEAD_DIM), f32),
        "conv_w": nrm(ks[10], (DEPTH, CONV_WIDTH, D_CONV), f32) * CONV_WIDTH ** -0.5,
        "conv_b": nrm(ks[11], (DEPTH, D_CONV), f32) * 0.02,
        "conv_ln_w": 1.0 + 0.05 * nrm(ks[12], (DEPTH, D_CONV), f32),
        "conv_ln_b": nrm(ks[13], (DEPTH, D_CONV), f32) * 0.02,
        "w_pw": nrm(ks[14], (DEPTH, D_CONV, D_CONV), f32) * D_CONV ** -0.5,
        "b_pw": nrm(ks[15], (DEPTH, D_CONV), f32) * 0.02,
        "w_out": nrm(ks[16], (DEPTH, D_MIX, D_MODEL), f32) * D_MIX ** -0.5,
    }


def _fwd_reference(x, c, ctx, c_ctx, w_mod, b_mod, norm_w, w_in, q_norm_w, k_norm_w,
              conv_w, conv_b, conv_ln_w, conv_ln_b, w_pw, b_pw, w_out):
    S = x.shape[1]
    angles = grid_angles(S, x.dtype)
    h, hc = x, ctx
    for l in range(DEPTH):
        mod = jax.nn.silu(c) @ w_mod[l] + b_mod[l]
        shift, scale, gate = jnp.split(mod, 3, axis=-1)
        mod_c = jax.nn.silu(c_ctx) @ w_mod[l] + b_mod[l]
        shift_c, scale_c, gate_c = jnp.split(mod_c, 3, axis=-1)

        u = rms_norm(h, norm_w[l]) * (1.0 + scale[:, None, :]) + shift[:, None, :]
        uc = rms_norm(hc, norm_w[l]) * (1.0 + scale_c) + shift_c

        p = u @ w_in[l]
        q, k, v = qk_heads(p, q_norm_w[l], k_norm_w[l])
        q = rope_2d(q, angles)
        k = rope_2d(k, angles)

        if l == DEPTH - 1:
            k_c, v_c = kv_heads(uc @ w_in[l][:, K_LO:V_HI], k_norm_w[l])
        else:
            pc = uc @ w_in[l]
            q_c, k_c, v_c = qk_heads(pc, q_norm_w[l], k_norm_w[l])
            attn_c = attend_dense(q_c, k_c, v_c) * jax.nn.silu(pc[..., ZA_LO:ZA_HI])
            conv_c = conformer_conv(pc[..., GLU_LO:GLU_HI], conv_w[l], conv_b[l], conv_ln_w[l],
                                    conv_ln_b[l], w_pw[l], b_pw[l]) * jax.nn.silu(pc[..., ZC_LO:ZC_HI])
            out_c = jnp.concatenate([attn_c, conv_c], axis=-1) @ w_out[l]

        k_all = jnp.concatenate([k_c, k], axis=1)
        v_all = jnp.concatenate([v_c, v], axis=1)
        attn = attend_blocked(q, k_all, v_all) * jax.nn.silu(p[..., ZA_LO:ZA_HI])

        conv = conformer_conv(p[..., GLU_LO:GLU_HI], conv_w[l], conv_b[l], conv_ln_w[l],
                              conv_ln_b[l], w_pw[l], b_pw[l]) * jax.nn.silu(p[..., ZC_LO:ZC_HI])

        out = jnp.concatenate([attn, conv], axis=-1) @ w_out[l]
        h = h + gate[:, None, :] * out
        if l < DEPTH - 1:
            hc = hc + gate_c * out_c
    return h


import jax as _jax
import jax.numpy as _jnp

TWIN_FORMAT = 'train_step'
FWD_PARAMS = ['x', 'c', 'ctx', 'c_ctx', 'w_mod', 'b_mod', 'norm_w', 'w_in', 'q_norm_w', 'k_norm_w', 'conv_w', 'conv_b', 'conv_ln_w', 'conv_ln_b', 'w_pw', 'b_pw', 'w_out']
TWIN_WEIGHTS = ['c_ctx', 'w_mod', 'b_mod', 'norm_w', 'w_in', 'q_norm_w', 'k_norm_w', 'conv_w', 'conv_b', 'conv_ln_w', 'conv_ln_b', 'w_pw', 'b_pw', 'w_out']
TWIN_DIFF_INPUT = 'x'
TWIN_INPUTS = ['x', 'c', 'ctx', 'c_ctx', 'w_mod', 'b_mod', 'norm_w', 'w_in', 'q_norm_w', 'k_norm_w', 'conv_w', 'conv_b', 'conv_ln_w', 'conv_ln_b', 'w_pw', 'b_pw', 'w_out', 'loss_target', 'm_c_ctx', 'm_w_mod', 'm_b_mod', 'm_norm_w', 'm_w_in', 'm_q_norm_w', 'm_k_norm_w', 'm_conv_w', 'm_conv_b', 'm_conv_ln_w', 'm_conv_ln_b', 'm_w_pw', 'm_b_pw', 'm_w_out', 'v_c_ctx', 'v_w_mod', 'v_b_mod', 'v_norm_w', 'v_w_in', 'v_q_norm_w', 'v_k_norm_w', 'v_conv_w', 'v_conv_b', 'v_conv_ln_w', 'v_conv_ln_b', 'v_w_pw', 'v_b_pw', 'v_w_out']
TWIN_OUTPUTS = ['loss', 'grad_x', 'grad_c_ctx', 'grad_w_mod', 'grad_b_mod', 'grad_norm_w', 'grad_w_in', 'grad_q_norm_w', 'grad_k_norm_w', 'grad_conv_w', 'grad_conv_b', 'grad_conv_ln_w', 'grad_conv_ln_b', 'grad_w_pw', 'grad_b_pw', 'grad_w_out', 'delta_c_ctx', 'delta_w_mod', 'delta_b_mod', 'delta_norm_w', 'delta_w_in', 'delta_q_norm_w', 'delta_k_norm_w', 'delta_conv_w', 'delta_conv_b', 'delta_conv_ln_w', 'delta_conv_ln_b', 'delta_w_pw', 'delta_b_pw', 'delta_w_out', 'new_m_c_ctx', 'new_m_w_mod', 'new_m_b_mod', 'new_m_norm_w', 'new_m_w_in', 'new_m_q_norm_w', 'new_m_k_norm_w', 'new_m_conv_w', 'new_m_conv_b', 'new_m_conv_ln_w', 'new_m_conv_ln_b', 'new_m_w_pw', 'new_m_b_pw', 'new_m_w_out', 'new_v_c_ctx', 'new_v_w_mod', 'new_v_b_mod', 'new_v_norm_w', 'new_v_w_in', 'new_v_q_norm_w', 'new_v_k_norm_w', 'new_v_conv_w', 'new_v_conv_b', 'new_v_conv_ln_w', 'new_v_conv_ln_b', 'new_v_w_pw', 'new_v_b_pw', 'new_v_w_out']
TWIN_LEAF_KINDS = {'loss': 'loss', 'grad_x': 'grad_x', 'grad_c_ctx': 'grad_w', 'grad_w_mod': 'grad_w', 'grad_b_mod': 'grad_w', 'grad_norm_w': 'grad_w', 'grad_w_in': 'grad_w', 'grad_q_norm_w': 'grad_w', 'grad_k_norm_w': 'grad_w', 'grad_conv_w': 'grad_w', 'grad_conv_b': 'grad_w', 'grad_conv_ln_w': 'grad_w', 'grad_conv_ln_b': 'grad_w', 'grad_w_pw': 'grad_w', 'grad_b_pw': 'grad_w', 'grad_w_out': 'grad_w', 'delta_c_ctx': 'delta_w', 'delta_w_mod': 'delta_w', 'delta_b_mod': 'delta_w', 'delta_norm_w': 'delta_w', 'delta_w_in': 'delta_w', 'delta_q_norm_w': 'delta_w', 'delta_k_norm_w': 'delta_w', 'delta_conv_w': 'delta_w', 'delta_conv_b': 'delta_w', 'delta_conv_ln_w': 'delta_w', 'delta_conv_ln_b': 'delta_w', 'delta_w_pw': 'delta_w', 'delta_b_pw': 'delta_w', 'delta_w_out': 'delta_w', 'new_m_c_ctx': 'new_m', 'new_m_w_mod': 'new_m', 'new_m_b_mod': 'new_m', 'new_m_norm_w': 'new_m', 'new_m_w_in': 'new_m', 'new_m_q_norm_w': 'new_m', 'new_m_k_norm_w': 'new_m', 'new_m_conv_w': 'new_m', 'new_m_conv_b': 'new_m', 'new_m_conv_ln_w': 'new_m', 'new_m_conv_ln_b': 'new_m', 'new_m_w_pw': 'new_m', 'new_m_b_pw': 'new_m', 'new_m_w_out': 'new_m', 'new_v_c_ctx': 'new_v', 'new_v_w_mod': 'new_v', 'new_v_b_mod': 'new_v', 'new_v_norm_w': 'new_v', 'new_v_w_in': 'new_v', 'new_v_q_norm_w': 'new_v', 'new_v_k_norm_w': 'new_v', 'new_v_conv_w': 'new_v', 'new_v_conv_b': 'new_v', 'new_v_conv_ln_w': 'new_v', 'new_v_conv_ln_b': 'new_v', 'new_v_w_pw': 'new_v', 'new_v_b_pw': 'new_v', 'new_v_w_out': 'new_v'}


def _forward(args):
    return _fwd_reference(*[args[k] for k in FWD_PARAMS])


def _output_shape():
    out = _jax.eval_shape(lambda: _forward(_fwd_setup_inputs(0)))
    return out.shape, out.dtype

N_MICROBATCH = 1
ADAM_LR = 0.001
ADAM_B1 = 0.9
ADAM_B2 = 0.999
ADAM_EPS = 1e-08
ADAM_WD = 0.01
ADAM_STEP = 10
PER_EXAMPLE_BATCH_AXIS = {'x': 0, 'c': 0, 'ctx': 0, 'loss_target': 0}
SHARED_INPUTS = []
_WEIGHT_DTYPES = {'c_ctx': _jnp.float32, 'w_mod': _jnp.float32, 'b_mod': _jnp.float32, 'norm_w': _jnp.float32, 'w_in': _jnp.float32, 'q_norm_w': _jnp.float32, 'k_norm_w': _jnp.float32, 'conv_w': _jnp.float32, 'conv_b': _jnp.float32, 'conv_ln_w': _jnp.float32, 'conv_ln_b': _jnp.float32, 'w_pw': _jnp.float32, 'b_pw': _jnp.float32, 'w_out': _jnp.float32}
MOMENT_SCALE = {'c_ctx': 3.578262e-01, 'w_mod': 1.765546e+00, 'b_mod': 3.558916e+00, 'norm_w': 2.812392e+00, 'w_in': 1.280556e+00, 'q_norm_w': 2.555468e-01, 'k_norm_w': 2.748240e-01, 'conv_w': 6.930278e-01, 'conv_b': 2.398364e+00, 'conv_ln_w': 5.060658e+00, 'conv_ln_b': 3.100541e+00, 'w_pw': 7.047764e-01, 'b_pw': 2.730025e+00, 'w_out': 4.095244e-01}


def _to_microbatches(a, axis):
    t = _jnp.moveaxis(a, axis, 0)
    t = t.reshape((N_MICROBATCH, t.shape[0] // N_MICROBATCH) + t.shape[1:])
    return _jnp.moveaxis(t, 1, axis + 1)


def setup_inputs(seed: int = 0) -> dict:
    inp = _fwd_setup_inputs(seed)
    key = _jax.random.fold_in(_jax.random.key(seed), 7919)
    shape, _ = _output_shape()
    out = dict(inp)
    out["loss_target"] = _jax.random.normal(_jax.random.fold_in(key, 0), shape, _jnp.float32)
    for i, name in enumerate(TWIN_WEIGHTS):
        w = inp[name].astype(_jnp.float32)
        if MOMENT_SCALE is None:
            s = _jnp.sqrt(_jnp.mean(_jnp.square(w)) + 1e-30)
        else:
            s = MOMENT_SCALE[name]
        km, kv = _jax.random.split(_jax.random.fold_in(key, i + 1))
        out[name] = w
        out["m_" + name] = s * _jax.random.normal(km, w.shape, _jnp.float32)
        out["v_" + name] = (s * s) * _jax.random.uniform(kv, w.shape, _jnp.float32, 0.5, 1.5)
    if N_MICROBATCH > 1:
        for name, axis in PER_EXAMPLE_BATCH_AXIS.items():
            out[name] = _to_microbatches(out[name], axis)
    return {'x': out['x'], 'c': out['c'], 'ctx': out['ctx'], 'c_ctx': out['c_ctx'], 'w_mod': out['w_mod'], 'b_mod': out['b_mod'], 'norm_w': out['norm_w'], 'w_in': out['w_in'], 'q_norm_w': out['q_norm_w'], 'k_norm_w': out['k_norm_w'], 'conv_w': out['conv_w'], 'conv_b': out['conv_b'], 'conv_ln_w': out['conv_ln_w'], 'conv_ln_b': out['conv_ln_b'], 'w_pw': out['w_pw'], 'b_pw': out['b_pw'], 'w_out': out['w_out'], 'loss_target': out['loss_target'], 'm_c_ctx': out['m_c_ctx'], 'm_w_mod': out['m_w_mod'], 'm_b_mod': out['m_b_mod'], 'm_norm_w': out['m_norm_w'], 'm_w_in': out['m_w_in'], 'm_q_norm_w': out['m_q_norm_w'], 'm_k_norm_w': out['m_k_norm_w'], 'm_conv_w': out['m_conv_w'], 'm_conv_b': out['m_conv_b'], 'm_conv_ln_w': out['m_conv_ln_w'], 'm_conv_ln_b': out['m_conv_ln_b'], 'm_w_pw': out['m_w_pw'], 'm_b_pw': out['m_b_pw'], 'm_w_out': out['m_w_out'], 'v_c_ctx': out['v_c_ctx'], 'v_w_mod': out['v_w_mod'], 'v_b_mod': out['v_b_mod'], 'v_norm_w': out['v_norm_w'], 'v_w_in': out['v_w_in'], 'v_q_norm_w': out['v_q_norm_w'], 'v_k_norm_w': out['v_k_norm_w'], 'v_conv_w': out['v_conv_w'], 'v_conv_b': out['v_conv_b'], 'v_conv_ln_w': out['v_conv_ln_w'], 'v_conv_ln_b': out['v_conv_ln_b'], 'v_w_pw': out['v_w_pw'], 'v_b_pw': out['v_b_pw'], 'v_w_out': out['v_w_out']}


def _loss(weights, diff, rest, loss_target):
    with _jax.named_scope("forward"):
        args = {**rest, TWIN_DIFF_INPUT: diff, **{k: w.astype(_WEIGHT_DTYPES[k]) for k, w in weights.items()}}
        y = _forward(args)
    with _jax.named_scope("loss_head"):
        err = _jnp.square(y.astype(_jnp.float32) - loss_target)
        return 0.5 * _jnp.sum(_jnp.mean(err, axis=-1)) if err.ndim else 0.5 * err


def _adamw(w, g, m, v):
    m = ADAM_B1 * m + (1.0 - ADAM_B1) * g
    v = ADAM_B2 * v + (1.0 - ADAM_B2) * _jnp.square(g)
    m_hat = m / (1.0 - ADAM_B1 ** ADAM_STEP)
    v_hat = v / (1.0 - ADAM_B2 ** ADAM_STEP)
    delta = -ADAM_LR * (m_hat / (_jnp.sqrt(v_hat) + ADAM_EPS) + ADAM_WD * w)
    return delta, m, v


def reference(x, c, ctx, c_ctx, w_mod, b_mod, norm_w, w_in, q_norm_w, k_norm_w, conv_w, conv_b, conv_ln_w, conv_ln_b, w_pw, b_pw, w_out, loss_target, m_c_ctx, m_w_mod, m_b_mod, m_norm_w, m_w_in, m_q_norm_w, m_k_norm_w, m_conv_w, m_conv_b, m_conv_ln_w, m_conv_ln_b, m_w_pw, m_b_pw, m_w_out, v_c_ctx, v_w_mod, v_b_mod, v_norm_w, v_w_in, v_q_norm_w, v_k_norm_w, v_conv_w, v_conv_b, v_conv_ln_w, v_conv_ln_b, v_w_pw, v_b_pw, v_w_out):
    given = dict(x=x, c=c, ctx=ctx, c_ctx=c_ctx, w_mod=w_mod, b_mod=b_mod, norm_w=norm_w, w_in=w_in, q_norm_w=q_norm_w, k_norm_w=k_norm_w, conv_w=conv_w, conv_b=conv_b, conv_ln_w=conv_ln_w, conv_ln_b=conv_ln_b, w_pw=w_pw, b_pw=b_pw, w_out=w_out, loss_target=loss_target, m_c_ctx=m_c_ctx, m_w_mod=m_w_mod, m_b_mod=m_b_mod, m_norm_w=m_norm_w, m_w_in=m_w_in, m_q_norm_w=m_q_norm_w, m_k_norm_w=m_k_norm_w, m_conv_w=m_conv_w, m_conv_b=m_conv_b, m_conv_ln_w=m_conv_ln_w, m_conv_ln_b=m_conv_ln_b, m_w_pw=m_w_pw, m_b_pw=m_b_pw, m_w_out=m_w_out, v_c_ctx=v_c_ctx, v_w_mod=v_w_mod, v_b_mod=v_b_mod, v_norm_w=v_norm_w, v_w_in=v_w_in, v_q_norm_w=v_q_norm_w, v_k_norm_w=v_k_norm_w, v_conv_w=v_conv_w, v_conv_b=v_conv_b, v_conv_ln_w=v_conv_ln_w, v_conv_ln_b=v_conv_ln_b, v_w_pw=v_w_pw, v_b_pw=v_b_pw, v_w_out=v_w_out)
    weights = {n: given[n] for n in TWIN_WEIGHTS}
    shared = {n: given[n] for n in SHARED_INPUTS}
    per_example = {n: given[n] for n in ['x', 'c', 'ctx']}
    grad_fn = _jax.value_and_grad(_loss, argnums=(0, 1))

    def one_microbatch(ex, loss_target):
        ex = dict(ex)
        diff = ex.pop(TWIN_DIFF_INPUT)
        return grad_fn(weights, diff, {**shared, **ex}, loss_target)

    if N_MICROBATCH == 1:
        loss, (grad_w, grad_x) = one_microbatch(per_example, given["loss_target"])
    else:
        def body(carry, xs):
            loss_sum, grad_sum = carry
            l_k, (gw_k, gx_k) = one_microbatch(xs[0], xs[1])
            with _jax.named_scope("update"):
                return (loss_sum + l_k, _jax.tree.map(_jnp.add, grad_sum, gw_k)), gx_k

        init = (_jnp.zeros((), _jnp.float32), _jax.tree.map(_jnp.zeros_like, weights))
        (loss, grad_w), grad_x = _jax.lax.scan(body, init, (per_example, given["loss_target"]))
    with _jax.named_scope("update"):
        delta_w, new_m, new_v = {}, {}, {}
        for n in TWIN_WEIGHTS:
            delta_w[n], new_m[n], new_v[n] = _adamw(weights[n], grad_w[n], given["m_" + n], given["v_" + n])
    return (loss, grad_x, *[grad_w[n] for n in TWIN_WEIGHTS], *[delta_w[n] for n in TWIN_WEIGHTS],
            *[new_m[n] for n in TWIN_WEIGHTS], *[new_v[n] for n in TWIN_WEIGHTS])
```

```python
import functools
import math

import jax
import jax.numpy as jnp
from jax import lax
from jax.experimental import pallas as pl
from jax.experimental.pallas import tpu as pltpu

F32 = jnp.float32
BF16 = jnp.bfloat16
MESH = pl.DeviceIdType.MESH

D_MODEL = 1024
D_ATTN = 512
D_CONV = 512
HEAD_DIM = 64
N_KV = 2
GRID_W = 64
ROPE_AXIS_DIM = 32
ROPE_THETA = 10000.0
CONV_WIDTH = 31
CONV_PAD = 15
HALO = 16
EPS = 1e-6
SPLITS = ((0, 768), (768, 1280), (1280, 1792), (1792, 2304), (2304, 2816))
D_IN = 2816
KV_LO, KV_HI = 512, 768

ADAM_LR = 0.001
ADAM_B1 = 0.9
ADAM_B2 = 0.999
ADAM_EPS = 1e-08
ADAM_WD = 0.01
ADAM_STEP = 10

VMEM_LIMIT = 56 * 1024 * 1024

NT = (((1,), (1,)), ((), ()))
TN = (((0,), (0,)), ((), ()))


def _params(n_axes=0, **kw):
    if n_axes:
        kw["dimension_semantics"] = ("arbitrary",) * n_axes
    return pltpu.CompilerParams(vmem_limit_bytes=VMEM_LIMIT, **kw)


def _sigmoid(x):
    return 1.0 / (1.0 + jnp.exp(-x))


def _silu_and_grad(z):
    s = _sigmoid(z)
    return z * s, s * (1.0 + z * (1.0 - s))


def _seg_mean(v, ones_bd):
    hi = v.astype(BF16)
    lo = (v - hi.astype(F32)).astype(BF16)
    s = jnp.dot(hi, ones_bd, preferred_element_type=F32) + jnp.dot(lo, ones_bd, preferred_element_type=F32)
    return s * (1.0 / HEAD_DIM)


def _partner(v):
    n = v.shape[1]
    lane = lax.broadcasted_iota(jnp.int32, (v.shape[0], 128), 1)
    first = (lane % 32) < 16
    parts = []
    for k in range(n // 128):
        ch = v[:, 128 * k:128 * (k + 1)]
        parts.append(jnp.where(first, pltpu.roll(ch, 112, 1), pltpu.roll(ch, 16, 1)))
    return parts[0] if len(parts) == 1 else jnp.concatenate(parts, axis=1)


def _tile_lanes(t, reps):
    return t if reps == 1 else jnp.concatenate([t] * reps, axis=1)


def _lo_mask(rows):
    return lax.broadcasted_iota(jnp.int32, (rows, 128), 1) < HEAD_DIM


def _all_gather8(x_local, name, reduce=False):
    r, cdim = x_local.shape

    def body(x_ref, out_ref, *rest):
        if reduce:
            sum_ref, send_sems, recv_sems, local_sem = rest
        else:
            send_sems, recv_sems, local_sem = rest
        x, y, c = lax.axis_index("x"), lax.axis_index("y"), lax.axis_index("c")
        me, sibling = (x, y, c), (x, y, 1 - c)
        chips = [(1 - x, y), (x, 1 - y), (1 - x, 1 - y)]

        def slot(px, py, pc):
            return out_ref.at[4 * px + 2 * py + pc]

        def copy(k, block, to, src=None):
            return pltpu.make_async_remote_copy(
                src_ref=slot(*block) if src is None else src, dst_ref=slot(*block),
                send_sem=send_sems.at[k], recv_sem=recv_sems.at[k], device_id=to, device_id_type=MESH)

        mine = pltpu.make_async_copy(x_ref, slot(*me), local_sem)
        mine.start()
        first = [copy(0, me, sibling, src=x_ref)]
        first += [copy(1 + j, me, (*chip, c), src=x_ref) for j, chip in enumerate(chips)]
        for cp in first:
            cp.start()
        passed = [copy(4 + j, (*chip, c), sibling) for j, chip in enumerate(chips)]
        for j, chip in enumerate(chips):
            copy(1 + j, (*chip, c), me).wait_recv()
            passed[j].start()
        copy(0, sibling, me).wait_recv()
        for j, chip in enumerate(chips):
            copy(4 + j, (*chip, 1 - c), me).wait_recv()
        for cp in first + passed:
            cp.wait_send()
        mine.wait()
        if reduce:
            acc = out_ref[0]
            for d in range(1, 8):
                acc = acc + out_ref[d]
            sum_ref[...] = acc

    out_shape = [jax.ShapeDtypeStruct((8, r, cdim), x_local.dtype)]
    out_specs = [pl.BlockSpec(memory_space=pltpu.VMEM)]
    if reduce:
        out_shape.append(jax.ShapeDtypeStruct((r, cdim), x_local.dtype))
        out_specs.append(pl.BlockSpec(memory_space=pltpu.VMEM))
    res = pl.pallas_call(
        body, name=name, out_shape=out_shape,
        in_specs=[pl.BlockSpec(memory_space=pltpu.VMEM)], out_specs=out_specs,
        scratch_shapes=[pltpu.SemaphoreType.DMA((7,)), pltpu.SemaphoreType.DMA((7,)), pltpu.SemaphoreType.DMA],
        compiler_params=_params(),
    )(x_local)
    return res if reduce else res[0]


def _chip_gather(arrs, name):
    n = len(arrs)

    def body(*refs):
        ins, outs = refs[:n], refs[n:2 * n]
        send_sems, recv_sems, local_sems = refs[2 * n:]
        x, y, c = lax.axis_index("x"), lax.axis_index("y"), lax.axis_index("c")
        sibling = (x, y, 1 - c)
        chips = [(1 - x, y), (x, 1 - y), (1 - x, 1 - y)]
        mychip = 2 * x + y

        def half(a, chip_idx, cc):
            h = arrs[a].shape[0] // 2
            return outs[a].at[chip_idx, pl.ds(cc * h, h)]

        def copy(a, k, chip_idx, cc, to, src=None):
            dst = half(a, chip_idx, cc)
            return pltpu.make_async_remote_copy(
                src_ref=dst if src is None else src, dst_ref=dst,
                send_sem=send_sems.at[6 * a + k], recv_sem=recv_sems.at[6 * a + k], device_id=to, device_id_type=MESH)

        locals_, firsts, passed = [], [], []
        for a in range(n):
            h = arrs[a].shape[0] // 2
            loc = pltpu.make_async_copy(ins[a], outs[a].at[mychip], local_sems.at[a])
            loc.start()
            locals_.append(loc)
            for j, chip in enumerate(chips):
                cp = copy(a, j, mychip, c, (*chip, c), src=ins[a].at[pl.ds(c * h, h)])
                cp.start()
                firsts.append(cp)
        for a in range(n):
            for j, (cx, cy) in enumerate(chips):
                copy(a, j, 2 * cx + cy, c, (x, y, c)).wait_recv()
                fw = copy(a, 3 + j, 2 * cx + cy, c, sibling)
                fw.start()
                passed.append(fw)
        for a in range(n):
            for j, (cx, cy) in enumerate(chips):
                copy(a, 3 + j, 2 * cx + cy, 1 - c, (x, y, c)).wait_recv()
        for cp in firsts + passed:
            cp.wait_send()
        for loc in locals_:
            loc.wait()

    return pl.pallas_call(
        body, name=name,
        out_shape=[jax.ShapeDtypeStruct((4,) + a.shape, a.dtype) for a in arrs],
        in_specs=[pl.BlockSpec(memory_space=pl.ANY)] * n, out_specs=[pl.BlockSpec(memory_space=pl.ANY)] * n,
        scratch_shapes=[pltpu.SemaphoreType.DMA((6 * n,)), pltpu.SemaphoreType.DMA((6 * n,)),
                        pltpu.SemaphoreType.DMA((n,))],
        compiler_params=_params(),
    )(*arrs)


def _reduce_scatter(grads, name):
    n = len(grads)
    hs = [g.shape[0] // 8 for g in grads]
    views = [g.reshape(4, 2, h, g.shape[1]) for g, h in zip(grads, hs)]

    def body(*refs):
        gs, outs = refs[:n], refs[n:2 * n]
        scr = refs[2 * n:]
        recv_a, tsum, recv_b = scr[:n], scr[n:2 * n], scr[2 * n:3 * n]
        send_sems, recv_sems = scr[3 * n:]
        x, y, c = lax.axis_index("x"), lax.axis_index("y"), lax.axis_index("c")
        sibling = (x, y, 1 - c)
        chips = [(1 - x, y), (x, 1 - y), (1 - x, 1 - y)]
        mychip = 2 * x + y

        def rcopy(a, k, src, dst, to):
            return pltpu.make_async_remote_copy(
                src_ref=src, dst_ref=dst, send_sem=send_sems.at[8 * a + k], recv_sem=recv_sems.at[8 * a + k],
                device_id=to, device_id_type=MESH)

        stage_a, stage_b, stage_c = [], [], []
        for a in range(n):
            for k in range(4):
                cp = rcopy(a, k, gs[a].at[k, 1 - c], recv_a[a].at[k], sibling)
                cp.start()
                stage_a.append(cp)
        for a in range(n):
            for j, (cx, cy) in enumerate(chips):
                k = 2 * cx + cy
                rcopy(a, k, gs[a].at[k, 1 - c], recv_a[a].at[k], sibling).wait_recv()
                tsum[a][k] = gs[a][k, c] + recv_a[a][k]
                cp = rcopy(a, 4 + j, tsum[a].at[k], recv_b[a].at[j], (cx, cy, c))
                cp.start()
                stage_b.append(cp)
            rcopy(a, mychip, gs[a].at[mychip, 1 - c], recv_a[a].at[mychip], sibling).wait_recv()
            tsum[a][mychip] = gs[a][mychip, c] + recv_a[a][mychip]
        for a in range(n):
            acc = tsum[a][mychip]
            for j in range(3):
                rcopy(a, 4 + j, tsum[a].at[0], recv_b[a].at[j], sibling).wait_recv()
                acc = acc + recv_b[a][j]
            outs[a][c] = acc
            cp = rcopy(a, 7, outs[a].at[c], outs[a].at[c], sibling)
            cp.start()
            stage_c.append(cp)
        for a in range(n):
            rcopy(a, 7, outs[a].at[1 - c], outs[a].at[1 - c], sibling).wait_recv()
        for cp in stage_a + stage_b + stage_c:
            cp.wait_send()

    scratch = [pltpu.VMEM((4, h, g.shape[1]), F32) for g, h in zip(grads, hs)]
    scratch += [pltpu.VMEM((4, h, g.shape[1]), F32) for g, h in zip(grads, hs)]
    scratch += [pltpu.VMEM((3, h, g.shape[1]), F32) for g, h in zip(grads, hs)]
    scratch += [pltpu.SemaphoreType.DMA((8 * n,)), pltpu.SemaphoreType.DMA((8 * n,))]
    res = pl.pallas_call(
        body, name=name,
        out_shape=[jax.ShapeDtypeStruct((2, h, g.shape[1]), F32) for g, h in zip(grads, hs)],
        in_specs=[pl.BlockSpec(memory_space=pltpu.VMEM)] * n, out_specs=[pl.BlockSpec(memory_space=pltpu.VMEM)] * n,
        scratch_shapes=scratch, compiler_params=_params(),
    )(*views)
    return [o.reshape(2 * h, o.shape[2]) for o, h in zip(res, hs)]


def _mod_forward(c_rows, w_mod, b_cols):
    rows, ncol = c_rows.shape[0], w_mod.shape[1]

    def body(c_ref, w_ref, b_ref, sc_ref, mod_ref):
        cv = c_ref[...]
        sc = cv * _sigmoid(cv)
        sc_ref[...] = sc
        mod_ref[...] = jnp.dot(sc, w_ref[...], preferred_element_type=F32, precision=lax.Precision.HIGHEST) + b_ref[...]

    return pl.pallas_call(
        body, name="mod_forward",
        out_shape=[jax.ShapeDtypeStruct((rows, D_MODEL), F32), jax.ShapeDtypeStruct((rows, ncol), F32)],
        compiler_params=_params(),
    )(c_rows, w_mod, b_cols)


def _mod_backward(sc_rows, dmod_cols, w_mod):
    rows, ncol = dmod_cols.shape

    def body(sc_ref, dm_ref, w_ref, gw_ref, gcc_ref):
        gw_ref[...] = lax.dot_general(sc_ref[...], dm_ref[...], TN, preferred_element_type=F32,
                                      precision=lax.Precision.HIGHEST)
        gcc_ref[...] = lax.dot_general(dm_ref[64:72, :], w_ref[...], NT, preferred_element_type=F32,
                                       precision=lax.Precision.HIGHEST)

    return pl.pallas_call(
        body, name="mod_backward",
        out_shape=[jax.ShapeDtypeStruct((D_MODEL, ncol), F32), jax.ShapeDtypeStruct((8, D_MODEL), F32)],
        compiler_params=_params(),
    )(sc_rows, dmod_cols, w_mod)


def _bcast_spec(arr):
    if arr.shape[0] == 1:
        return pl.BlockSpec((1, 1, arr.shape[2]), lambda b, i: (0, 0, 0))
    return pl.BlockSpec((1, 1, arr.shape[2]), lambda b, i: (b, 0, 0))


def _norm_inproj(x, shift, scale1p, norm_w, w_t, splits, tm, name):
    bsz, s, d = x.shape

    def body(x_ref, sh_ref, sc_ref, nw_ref, w_ref, u_ref, *out_refs):
        xv = x_ref[0]
        rstd = lax.rsqrt(jnp.mean(xv * xv, axis=-1, keepdims=True) + EPS)
        u = (xv * rstd * nw_ref[...]) * sc_ref[0] + sh_ref[0]
        ub = u.astype(BF16)
        u_ref[0] = ub
        for (lo, hi), o_ref in zip(splits, out_refs):
            o_ref[0] = lax.dot_general(ub, w_ref[lo:hi, :], NT, preferred_element_type=F32)

    tok = lambda w: pl.BlockSpec((1, tm, w), lambda b, i: (b, i, 0))
    return pl.pallas_call(
        body, name=name, grid=(bsz, s // tm),
        in_specs=[tok(d), _bcast_spec(shift), _bcast_spec(scale1p), pl.BlockSpec((1, d), lambda b, i: (0, 0)),
                  pl.BlockSpec(w_t.shape, lambda b, i: (0, 0))],
        out_specs=[tok(d)] + [tok(hi - lo) for lo, hi in splits],
        out_shape=[jax.ShapeDtypeStruct((bsz, s, d), BF16)]
        + [jax.ShapeDtypeStruct((bsz, s, hi - lo), F32) for lo, hi in splits],
        compiler_params=_params(2),
    )(x, shift, scale1p, norm_w, w_t)


def _dup_heads(kv, lo_mask):
    r = pltpu.roll(kv, HEAD_DIM, 1)
    return jnp.where(lo_mask, kv, r), jnp.where(lo_mask, r, kv)


def _qkv_prep(qkv, cos, sin, qnw, knw, bd512, bd128, ts):
    bsz, s, _ = qkv.shape

    def body(p_ref, cos_ref, sin_ref, qnw_ref, knw_ref, bd512_ref, bd128_ref, q_ref, k_ref, v_ref):
        lo_mask = _lo_mask(ts)
        cos_t, sin_t = cos_ref[...], sin_ref[...]
        qp = p_ref[0, :, 0:512]
        qn = qp * lax.rsqrt(_seg_mean(qp * qp, bd512_ref[...]) + EPS) * qnw_ref[...]
        qr = qn * _tile_lanes(cos_t, 4) + _partner(qn) * _tile_lanes(sin_t, 4)
        q_ref[0] = (qr * (1.0 / math.sqrt(HEAD_DIM))).astype(BF16)
        kp = p_ref[0, :, 512:640]
        kn = kp * lax.rsqrt(_seg_mean(kp * kp, bd128_ref[...]) + EPS) * knw_ref[...]
        kr = kn * cos_t + _partner(kn) * sin_t
        k0, k1 = _dup_heads(kr, lo_mask)
        k_ref[0, 0] = k0.astype(BF16)
        k_ref[0, 1] = k1.astype(BF16)
        v0, v1 = _dup_heads(p_ref[0, :, 640:768], lo_mask)
        v_ref[0, 0] = v0.astype(BF16)
        v_ref[0, 1] = v1.astype(BF16)

    const = lambda a: pl.BlockSpec(a.shape, lambda b, i: (0,) * a.ndim)
    kv_spec = pl.BlockSpec((1, 2, ts, 128), lambda b, i: (b, 0, i, 0))
    return pl.pallas_call(
        body, name="qkv_prep", grid=(bsz, s // ts),
        in_specs=[pl.BlockSpec((1, ts, 768), lambda b, i: (b, i, 0)),
                  pl.BlockSpec((ts, 128), lambda b, i: (i, 0)), pl.BlockSpec((ts, 128), lambda b, i: (i, 0)),
                  const(qnw), const(knw), const(bd512), const(bd128)],
        out_specs=[pl.BlockSpec((1, ts, 512), lambda b, i: (b, i, 0)), kv_spec, kv_spec],
        out_shape=[jax.ShapeDtypeStruct((bsz, s, 512), BF16), jax.ShapeDtypeStruct((bsz, 2, s, 128), BF16),
                   jax.ShapeDtypeStruct((bsz, 2, s, 128), BF16)],
        compiler_params=_params(2),
    )(qkv, cos, sin, qnw, knw, bd512, bd128)


def _ctx_kv_prep(pc, knw, bd128):
    bsz, cl, _ = pc.shape

    def body(p_ref, knw_ref, bd128_ref, k_ref, v_ref):
        lo_mask = _lo_mask(cl)
        kp = p_ref[0, :, 0:128]
        kn = kp * lax.rsqrt(_seg_mean(kp * kp, bd128_ref[...]) + EPS) * knw_ref[...]
        k0, k1 = _dup_heads(kn, lo_mask)
        k_ref[0, 0] = k0.astype(BF16)
        k_ref[0, 1] = k1.astype(BF16)
        v0, v1 = _dup_heads(p_ref[0, :, 128:256], lo_mask)
        v_ref[0, 0] = v0.astype(BF16)
        v_ref[0, 1] = v1.astype(BF16)

    const = lambda a: pl.BlockSpec(a.shape, lambda b: (0,) * a.ndim)
    kv_spec = pl.BlockSpec((1, 2, cl, 128), lambda b: (b, 0, 0, 0))
    return pl.pallas_call(
        body, name="ctx_kv_prep", grid=(bsz,),
        in_specs=[pl.BlockSpec((1, cl, 256), lambda b: (b, 0, 0)), const(knw), const(bd128)],
        out_specs=[kv_spec, kv_spec],
        out_shape=[jax.ShapeDtypeStruct((bsz, 2, cl, 128), BF16)] * 2,
        compiler_params=_params(1),
    )(pc, knw, bd128)


def _attn_forward(q, k2, v2, tq):
    bsz, s, _ = q.shape
    sk = k2.shape[2]

    def body(q_ref, k_ref, v_ref, o_ref, lse_ref):
        kk, vv = k_ref[0, 0], v_ref[0, 0]
        lo_mask = _lo_mask(tq)
        for j in range(2):
            qp = q_ref[0, :, 128 * j:128 * (j + 1)]
            outs, lses = [], []
            for half in range(2):
                sel = lo_mask if half == 0 else jnp.logical_not(lo_mask)
                qs = jnp.where(sel, qp, jnp.zeros_like(qp))
                sc = lax.dot_general(qs, kk, NT, preferred_element_type=F32)
                m = jnp.max(sc, axis=-1, keepdims=True)
                p = jnp.exp(sc - m)
                l = jnp.sum(p, axis=-1, keepdims=True)
                o = jnp.dot(p.astype(BF16), vv, preferred_element_type=F32)
                outs.append(o / l)
                lses.append(jnp.broadcast_to(m + jnp.log(l), (tq, 128)))
            o_ref[0, :, 128 * j:128 * (j + 1)] = jnp.where(lo_mask, outs[0], outs[1])
            lse_ref[0, :, 128 * j:128 * (j + 1)] = jnp.where(lo_mask, lses[0], lses[1])

    q_spec = pl.BlockSpec((1, tq, 256), lambda b, g, i: (b, i, g))
    kv_spec = pl.BlockSpec((1, 1, sk, 128), lambda b, g, i: (b, g, 0, 0))
    return pl.pallas_call(
        body, name="attn_forward", grid=(bsz, N_KV, s // tq),
        in_specs=[q_spec, kv_spec, kv_spec], out_specs=[q_spec, q_spec],
        out_shape=[jax.ShapeDtypeStruct((bsz, s, 512), F32)] * 2,
        compiler_params=_params(3),
    )(q, k2, v2)


def _halo_specs(width, ts, s):
    r = ts // HALO
    last = s // HALO - 1
    return [pl.BlockSpec((1, ts, width), lambda b, i: (b, i, 0)),
            pl.BlockSpec((1, HALO, width), lambda b, i: (b, jnp.maximum(i * r - 1, 0), 0)),
            pl.BlockSpec((1, HALO, width), lambda b, i: (b, jnp.minimum((i + 1) * r, last), 0))]


def _fill_ext(ext_ref, cur, prev, nxt, i, n_tiles, ts):
    ext_ref[0:HALO, :] = jnp.where(i > 0, prev, jnp.zeros_like(prev))
    ext_ref[HALO:HALO + ts, :] = cur
    ext_ref[HALO + ts:2 * HALO + ts, :] = jnp.where(i < n_tiles - 1, nxt, jnp.zeros_like(nxt))


def _conv_forward(ga, gg, conv_w, conv_b, ln_w, ln_b, w_pw, b_pw, ts):
    bsz, s, dc = ga.shape
    n_tiles = s // ts

    def body(a_ref, ap_ref, an_ref, g_ref, gp_ref, gn_ref, cw_ref, cb_ref, lw_ref, lb_ref, wp_ref, bp_ref,
             y_ref, cv_ref, ext_ref):
        i = pl.program_id(1)
        glu = lambda a, g: a * _sigmoid(g)
        _fill_ext(ext_ref, glu(a_ref[0], g_ref[0]), glu(ap_ref[0], gp_ref[0]), glu(an_ref[0], gn_ref[0]), i, n_tiles, ts)
        acc = jnp.broadcast_to(cb_ref[...], (ts, dc))
        for j in range(CONV_WIDTH):
            acc = acc + cw_ref[j:j + 1, :] * ext_ref[pl.ds(HALO - CONV_PAD + j, ts), :]
        y_ref[0] = acc
        mu = jnp.mean(acc, axis=-1, keepdims=True)
        yc = acc - mu
        var = jnp.mean(yc * yc, axis=-1, keepdims=True)
        yn = yc * lax.rsqrt(var + EPS) * lw_ref[...] + lb_ref[...]
        ys = yn * _sigmoid(yn)
        cv_ref[0] = jnp.dot(ys.astype(BF16), wp_ref[...], preferred_element_type=F32) + bp_ref[...]

    const = lambda a: pl.BlockSpec(a.shape, lambda b, i: (0,) * a.ndim)
    return pl.pallas_call(
        body, name="conv_forward", grid=(bsz, n_tiles),
        in_specs=_halo_specs(dc, ts, s) + _halo_specs(dc, ts, s)
        + [const(conv_w), const(conv_b), const(ln_w), const(ln_b), const(w_pw), const(b_pw)],
        out_specs=[pl.BlockSpec((1, ts, dc), lambda b, i: (b, i, 0))] * 2,
        out_shape=[jax.ShapeDtypeStruct((bsz, s, dc), F32)] * 2,
        scratch_shapes=[pltpu.VMEM((ts + 2 * HALO, dc), F32)],
        compiler_params=_params(2),
    )(ga, ga, ga, gg, gg, gg, conv_w, conv_b, ln_w, ln_b, w_pw, b_pw)


def _outproj_loss(x, target, gate, o, za, cv, zc, w_out, tm):
    bsz, s, d = x.shape

    def body(x_ref, t_ref, gate_ref, o_ref, za_ref, cv_ref, zc_ref, w_ref,
             loss_ref, dh_ref, do_ref, dza_ref, dcv_ref, dzc_ref, dgate_ref, gw_ref):
        b, i = pl.program_id(0), pl.program_id(1)
        ov, cvv = o_ref[0], cv_ref[0]
        silu_a, dsilu_a = _silu_and_grad(za_ref[0])
        silu_c, dsilu_c = _silu_and_grad(zc_ref[0])
        mix = jnp.concatenate([ov * silu_a, cvv * silu_c], axis=1).astype(BF16)
        out = jnp.dot(mix, w_ref[...], preferred_element_type=F32)
        gate_v = gate_ref[0]
        err = x_ref[0] + gate_v * out - t_ref[0]
        dh = err * (1.0 / d)
        dh_ref[0] = dh
        dout = (dh * gate_v).astype(BF16)
        dmix = lax.dot_general(dout, w_ref[...], NT, preferred_element_type=F32)
        gw = lax.dot_general(mix, dout, TN, preferred_element_type=F32)
        dg = jnp.sum(dh * out, axis=0, keepdims=True)
        sq = jnp.sum(err * err)

        @pl.when(jnp.logical_and(b == 0, i == 0))
        def _():
            gw_ref[...] = gw

        @pl.when(jnp.logical_or(b > 0, i > 0))
        def _():
            gw_ref[...] += gw

        @pl.when(i == 0)
        def _():
            dgate_ref[0] = dg
            loss_ref[...] = jnp.zeros(loss_ref.shape, F32) + sq

        @pl.when(i > 0)
        def _():
            dgate_ref[0] += dg
            loss_ref[...] += sq

        dma, dmc = dmix[:, :D_ATTN], dmix[:, D_ATTN:]
        do_ref[0] = dma * silu_a
        dza_ref[0] = (dma * ov * dsilu_a).astype(BF16)
        dcv_ref[0] = dmc * silu_c
        dzc_ref[0] = (dmc * cvv * dsilu_c).astype(BF16)

    tok = lambda w: pl.BlockSpec((1, tm, w), lambda b, i: (b, i, 0))
    return pl.pallas_call(
        body, name="outproj_loss", grid=(bsz, s // tm),
        in_specs=[tok(d), tok(d), _bcast_spec(gate), tok(512), tok(512), tok(512), tok(512),
                  pl.BlockSpec(w_out.shape, lambda b, i: (0, 0))],
        out_specs=[pl.BlockSpec((1, 8, 128), lambda b, i: (b, 0, 0)), tok(d), tok(512), tok(512), tok(512), tok(512),
                   pl.BlockSpec((1, 1, d), lambda b, i: (b, 0, 0)), pl.BlockSpec((d, d), lambda b, i: (0, 0))],
        out_shape=[jax.ShapeDtypeStruct((bsz, 8, 128), F32), jax.ShapeDtypeStruct((bsz, s, d), F32),
                   jax.ShapeDtypeStruct((bsz, s, 512), F32), jax.ShapeDtypeStruct((bsz, s, 512), BF16),
                   jax.ShapeDtypeStruct((bsz, s, 512), F32), jax.ShapeDtypeStruct((bsz, s, 512), BF16),
                   jax.ShapeDtypeStruct((bsz, 1, d), F32), jax.ShapeDtypeStruct((d, d), F32)],
        compiler_params=_params(2),
    )(x, target, gate, o, za, cv, zc, w_out)


def _conv_token_backward(dcv, y, ln_w, ln_b, w_pw, tm):
    bsz, s, dc = dcv.shape

    def body(dcv_ref, y_ref, lw_ref, lb_ref, wp_ref, dy_ref, gwp_ref, st_ref):
        b, i = pl.program_id(0), pl.program_id(1)
        yv, dcvv = y_ref[0], dcv_ref[0]
        mu = jnp.mean(yv, axis=-1, keepdims=True)
        yc = yv - mu
        rstd = lax.rsqrt(jnp.mean(yc * yc, axis=-1, keepdims=True) + EPS)
        yhat = yc * rstd
        yn = yhat * lw_ref[...] + lb_ref[...]
        ys, dsilu = _silu_and_grad(yn)
        dcvb = dcvv.astype(BF16)
        gwp = lax.dot_general(ys.astype(BF16), dcvb, TN, preferred_element_type=F32)
        dys = lax.dot_general(dcvb, wp_ref[...], NT, preferred_element_type=F32)
        dyn = dys * dsilu
        dyhat = dyn * lw_ref[...]
        dy = rstd * (dyhat - jnp.mean(dyhat, axis=-1, keepdims=True)
                     - yhat * jnp.mean(dyhat * yhat, axis=-1, keepdims=True))
        dy_ref[0] = dy
        red = lambda v: jnp.sum(v, axis=0, keepdims=True)
        stats = jnp.concatenate([red(dcvv), red(dyn * yhat), red(dyn), red(dy), jnp.zeros((4, dc), F32)], axis=0)
        first = jnp.logical_and(b == 0, i == 0)

        @pl.when(first)
        def _():
            gwp_ref[...] = gwp
            st_ref[...] = stats

        @pl.when(jnp.logical_not(first))
        def _():
            gwp_ref[...] += gwp
            st_ref[...] += stats

    tok = pl.BlockSpec((1, tm, dc), lambda b, i: (b, i, 0))
    const = lambda a: pl.BlockSpec(a.shape, lambda b, i: (0,) * a.ndim)
    return pl.pallas_call(
        body, name="conv_token_backward", grid=(bsz, s // tm),
        in_specs=[tok, tok, const(ln_w), const(ln_b), const(w_pw)],
        out_specs=[tok, pl.BlockSpec((dc, dc), lambda b, i: (0, 0)), pl.BlockSpec((8, dc), lambda b, i: (0, 0))],
        out_shape=[jax.ShapeDtypeStruct((bsz, s, dc), F32), jax.ShapeDtypeStruct((dc, dc), F32),
                   jax.ShapeDtypeStruct((8, dc), F32)],
        compiler_params=_params(2),
    )(dcv, y, ln_w, ln_b, w_pw)


def _conv_backward(dy, ga, gg, conv_w, ts):
    bsz, s, dc = dy.shape
    n_tiles = s // ts

    def body(dy_ref, dyp_ref, dyn_ref, a_ref, ap_ref, an_ref, g_ref, gp_ref, gn_ref, cw_ref,
             da_ref, dg_ref, gcw_ref, dyext_ref, ugext_ref):
        b, i = pl.program_id(0), pl.program_id(1)
        dyv = dy_ref[0]
        av, sg = a_ref[0], _sigmoid(g_ref[0])
        glu = lambda a, g: a * _sigmoid(g)
        _fill_ext(dyext_ref, dyv, dyp_ref[0], dyn_ref[0], i, n_tiles, ts)
        _fill_ext(ugext_ref, av * sg, glu(ap_ref[0], gp_ref[0]), glu(an_ref[0], gn_ref[0]), i, n_tiles, ts)
        dug = jnp.zeros((ts, dc), F32)
        rows = []
        for j in range(CONV_WIDTH):
            dug = dug + cw_ref[j:j + 1, :] * dyext_ref[pl.ds(HALO + CONV_PAD - j, ts), :]
            rows.append(jnp.sum(dyv * ugext_ref[pl.ds(HALO - CONV_PAD + j, ts), :], axis=0, keepdims=True))
        rows.append(jnp.zeros((1, dc), F32))
        gcw = jnp.concatenate(rows, axis=0)
        first = jnp.logical_and(b == 0, i == 0)

        @pl.when(first)
        def _():
            gcw_ref[...] = gcw

        @pl.when(jnp.logical_not(first))
        def _():
            gcw_ref[...] += gcw

        da_ref[0] = (dug * sg).astype(BF16)
        dg_ref[0] = (dug * av * sg * (1.0 - sg)).astype(BF16)

    tok = pl.BlockSpec((1, ts, dc), lambda b, i: (b, i, 0))
    return pl.pallas_call(
        body, name="conv_backward", grid=(bsz, n_tiles),
        in_specs=_halo_specs(dc, ts, s) + _halo_specs(dc, ts, s) + _halo_specs(dc, ts, s)
        + [pl.BlockSpec(conv_w.shape, lambda b, i: (0, 0))],
        out_specs=[tok, tok, pl.BlockSpec((32, dc), lambda b, i: (0, 0))],
        out_shape=[jax.ShapeDtypeStruct((bsz, s, dc), BF16), jax.ShapeDtypeStruct((bsz, s, dc), BF16),
                   jax.ShapeDtypeStruct((32, dc), F32)],
        scratch_shapes=[pltpu.VMEM((ts + 2 * HALO, dc), F32), pltpu.VMEM((ts + 2 * HALO, dc), F32)],
        compiler_params=_params(2),
    )(dy, dy, dy, ga, ga, ga, gg, gg, gg, conv_w)


def _attn_backward(q, k2, v2, o, do, lse, tq):
    bsz, s, _ = q.shape
    sk = k2.shape[2]
    scale = 1.0 / math.sqrt(HEAD_DIM)

    def body(q_ref, k_ref, v_ref, o_ref, do_ref, lse_ref, dq_ref, dk_ref, dv_ref):
        i = pl.program_id(2)
        kk, vv = k_ref[0, 0], v_ref[0, 0]
        lo_mask = _lo_mask(tq)
        dk_acc = jnp.zeros((sk, 128), F32)
        dv_acc = jnp.zeros((sk, 128), F32)
        for j in range(2):
            cols = slice(128 * j, 128 * (j + 1))
            qp, dop, lsep = q_ref[0, :, cols], do_ref[0, :, cols], lse_ref[0, :, cols]
            dprod = dop * o_ref[0, :, cols]
            dqs = []
            for half in range(2):
                sel = lo_mask if half == 0 else jnp.logical_not(lo_mask)
                qs = jnp.where(sel, qp, jnp.zeros_like(qp))
                dos = jnp.where(sel, dop, 0.0).astype(BF16)
                lse_h = jnp.max(jnp.where(sel, lsep, -jnp.inf), axis=-1, keepdims=True)
                delta = jnp.sum(jnp.where(sel, dprod, 0.0), axis=-1, keepdims=True)
                sc = lax.dot_general(qs, kk, NT, preferred_element_type=F32)
                p = jnp.exp(sc - lse_h)
                dp = lax.dot_general(dos, vv, NT, preferred_element_type=F32)
                ds = (p * (dp - delta)).astype(BF16)
                dv_acc = dv_acc + lax.dot_general(p.astype(BF16), dos, TN, preferred_element_type=F32)
                dk_acc = dk_acc + lax.dot_general(ds, qs, TN, preferred_element_type=F32)
                dqs.append(jnp.dot(ds, kk, preferred_element_type=F32))
            dq_ref[0, :, cols] = jnp.where(lo_mask, dqs[0], dqs[1]) * scale

        @pl.when(i == 0)
        def _():
            dk_ref[0, 0] = dk_acc
            dv_ref[0, 0] = dv_acc

        @pl.when(i > 0)
        def _():
            dk_ref[0, 0] += dk_acc
            dv_ref[0, 0] += dv_acc

    q_spec = pl.BlockSpec((1, tq, 256), lambda b, g, i: (b, i, g))
    kv_spec = pl.BlockSpec((1, 1, sk, 128), lambda b, g, i: (b, g, 0, 0))
    return pl.pallas_call(
        body, name="attn_backward", grid=(bsz, N_KV, s // tq),
        in_specs=[q_spec, kv_spec, kv_spec, q_spec, q_spec, q_spec],
        out_specs=[q_spec, kv_spec, kv_spec],
        out_shape=[jax.ShapeDtypeStruct((bsz, s, 512), F32), jax.ShapeDtypeStruct((bsz, 2, sk, 128), F32),
                   jax.ShapeDtypeStruct((bsz, 2, sk, 128), F32)],
        compiler_params=_params(3),
    )(q, k2, v2, o, do, lse)


def _fold_heads(acc2_ref_val0, acc2_ref_val1, lo_mask):
    f0 = acc2_ref_val0 + pltpu.roll(acc2_ref_val0, HEAD_DIM, 1)
    f1 = acc2_ref_val1 + pltpu.roll(acc2_ref_val1, HEAD_DIM, 1)
    return jnp.where(lo_mask, f0, f1)


def _norm_backward(dn, pre, w, bd):
    rstd = lax.rsqrt(_seg_mean(pre * pre, bd) + EPS)
    xhat = pre * rstd
    dxhat = dn * w
    return rstd * (dxhat - xhat * _seg_mean(dxhat * xhat, bd)), dn * xhat


def _qkv_backward(qkv, dq, dk2, dv2, cos, sin, qnw, knw, bd512, bd128, ts, row0):
    bsz, s, _ = qkv.shape

    def body(p_ref, dq_ref, dk_ref, dv_ref, cos_ref, sin_ref, qnw_ref, knw_ref, bd512_ref, bd128_ref, d_ref, gw_ref):
        b, i = pl.program_id(0), pl.program_id(1)
        lo_mask = _lo_mask(ts)
        cos_t, sin_t = cos_ref[...], sin_ref[...]
        dqr = dq_ref[0]
        dqn = dqr * _tile_lanes(cos_t, 4) + _partner(dqr * _tile_lanes(sin_t, 4))
        dqp, gq = _norm_backward(dqn, p_ref[0, :, 0:512], qnw_ref[...], bd512_ref[...])
        dkr = _fold_heads(dk_ref[0, 0], dk_ref[0, 1], lo_mask)
        dkn = dkr * cos_t + _partner(dkr * sin_t)
        dkp, gk = _norm_backward(dkn, p_ref[0, :, 512:640], knw_ref[...], bd128_ref[...])
        dvp = _fold_heads(dv_ref[0, 0], dv_ref[0, 1], lo_mask)
        d_ref[0] = jnp.concatenate([dqp, dkp, dvp], axis=1).astype(BF16)
        gk512 = jnp.concatenate([jnp.sum(gk, axis=0, keepdims=True), jnp.zeros((1, 384), F32)], axis=1)
        rows = jnp.concatenate([jnp.sum(gq, axis=0, keepdims=True), gk512, jnp.zeros((6, 512), F32)], axis=0)
        first = jnp.logical_and(b == 0, i == 0)

        @pl.when(first)
        def _():
            gw_ref[...] = rows

        @pl.when(jnp.logical_not(first))
        def _():
            gw_ref[...] += rows

    const = lambda a: pl.BlockSpec(a.shape, lambda b, i: (0,) * a.ndim)
    kv_spec = pl.BlockSpec((1, 2, ts, 128), lambda b, i: (b, 0, i + row0 // ts, 0))
    return pl.pallas_call(
        body, name="qkv_backward", grid=(bsz, s // ts),
        in_specs=[pl.BlockSpec((1, ts, 768), lambda b, i: (b, i, 0)), pl.BlockSpec((1, ts, 512), lambda b, i: (b, i, 0)),
                  kv_spec, kv_spec, pl.BlockSpec((ts, 128), lambda b, i: (i, 0)),
                  pl.BlockSpec((ts, 128), lambda b, i: (i, 0)), const(qnw), const(knw), const(bd512), const(bd128)],
        out_specs=[pl.BlockSpec((1, ts, 768), lambda b, i: (b, i, 0)), pl.BlockSpec((8, 512), lambda b, i: (0, 0))],
        out_shape=[jax.ShapeDtypeStruct((bsz, s, 768), BF16), jax.ShapeDtypeStruct((8, 512), F32)],
        compiler_params=_params(2),
    )(qkv, dq, dk2, dv2, cos, sin, qnw, knw, bd512, bd128)


def _ctx_kv_backward(pc, dk2, dv2, knw, bd128):
    bsz, cl, _ = pc.shape

    def body(p_ref, dk_ref, dv_ref, knw_ref, bd128_ref, d_ref, gw_ref):
        b = pl.program_id(0)
        lo_mask = _lo_mask(cl)
        dkn = _fold_heads(dk_ref[0, 0], dk_ref[0, 1], lo_mask)
        dkp, gk = _norm_backward(dkn, p_ref[0, :, 0:128], knw_ref[...], bd128_ref[...])
        dvp = _fold_heads(dv_ref[0, 0], dv_ref[0, 1], lo_mask)
        d_ref[0] = jnp.concatenate([dkp, dvp], axis=1).astype(BF16)
        rows = jnp.concatenate([jnp.sum(gk, axis=0, keepdims=True), jnp.zeros((7, 128), F32)], axis=0)

        @pl.when(b == 0)
        def _():
            gw_ref[...] = rows

        @pl.when(b > 0)
        def _():
            gw_ref[...] += rows

    const = lambda a: pl.BlockSpec(a.shape, lambda b: (0,) * a.ndim)
    kv_spec = pl.BlockSpec((1, 2, cl, 128), lambda b: (b, 0, 0, 0))
    return pl.pallas_call(
        body, name="ctx_kv_backward", grid=(bsz,),
        in_specs=[pl.BlockSpec((1, cl, 256), lambda b: (b, 0, 0)), kv_spec, kv_spec, const(knw), const(bd128)],
        out_specs=[pl.BlockSpec((1, cl, 256), lambda b: (b, 0, 0)), pl.BlockSpec((8, 128), lambda b: (0, 0))],
        out_shape=[jax.ShapeDtypeStruct((bsz, cl, 256), BF16), jax.ShapeDtypeStruct((8, 128), F32)],
        compiler_params=_params(1),
    )(pc, dk2, dv2, knw, bd128)


def _inproj_backward(dps, u, x, dh, scale1p, norm_w, w_t, gw_init, tm, name):
    bsz, s, d = x.shape
    n_p = len(dps)
    nrows = w_t.shape[0]
    shared = scale1p.shape[0] == 1
    with_dx = dh is not None

    def body(*refs):
        dp_refs = refs[:n_p]
        u_ref, x_ref = refs[n_p], refs[n_p + 1]
        k = n_p + 2
        if with_dx:
            dh_ref = refs[k]
            k += 1
        sc_ref, nw_ref, w_ref = refs[k], refs[k + 1], refs[k + 2]
        k += 3
        if gw_init is not None:
            gi_ref = refs[k]
            k += 1
        outs = refs[k:]
        if with_dx:
            gx_ref, gw_ref, mod_ref, gnw_ref = outs
        else:
            gw_ref, mod_ref, gnw_ref = outs
        b, i = pl.program_id(0), pl.program_id(1)
        first = jnp.logical_and(b == 0, i == 0)
        dp = dp_refs[0][0] if n_p == 1 else jnp.concatenate([r[0] for r in dp_refs], axis=1)
        du = jnp.dot(dp, w_ref[...], preferred_element_type=F32)
        gw = lax.dot_general(dp, u_ref[0], TN, preferred_element_type=F32)

        @pl.when(first)
        def _():
            gw_ref[...] = gw
            if gw_init is not None:
                gw_ref[KV_LO:KV_HI, :] += gi_ref[...]

        @pl.when(jnp.logical_not(first))
        def _():
            gw_ref[...] += gw

        xv = x_ref[0]
        rstd = lax.rsqrt(jnp.mean(xv * xv, axis=-1, keepdims=True) + EPS)
        xhat = xv * rstd
        nw, sc = nw_ref[...], sc_ref[0]
        red = lambda v: jnp.sum(v, axis=0, keepdims=True)
        mod_rows = jnp.concatenate([red(du), red(du * (xhat * nw)), jnp.zeros((6, d), F32)], axis=0)
        gnw_rows = jnp.concatenate([red(du * sc * xhat), jnp.zeros((7, d), F32)], axis=0)
        mod_first = first if shared else i == 0

        @pl.when(mod_first)
        def _():
            mod_ref[0] = mod_rows

        @pl.when(jnp.logical_not(mod_first))
        def _():
            mod_ref[0] += mod_rows

        @pl.when(first)
        def _():
            gnw_ref[...] = gnw_rows

        @pl.when(jnp.logical_not(first))
        def _():
            gnw_ref[...] += gnw_rows

        if with_dx:
            dxhat = du * (nw * sc)
            gx_ref[0] = dh_ref[0] + rstd * (dxhat - xhat * jnp.mean(dxhat * xhat, axis=-1, keepdims=True))

    tok = lambda w: pl.BlockSpec((1, tm, w), lambda b, i: (b, i, 0))
    in_specs = [tok(p.shape[2]) for p in dps] + [tok(d), tok(d)]
    args = list(dps) + [u, x]
    if with_dx:
        in_specs.append(tok(d))
        args.append(dh)
    in_specs += [_bcast_spec(scale1p), pl.BlockSpec((1, d), lambda b, i: (0, 0)),
                 pl.BlockSpec(w_t.shape, lambda b, i: (0, 0))]
    args += [scale1p, norm_w, w_t]
    if gw_init is not None:
        in_specs.append(pl.BlockSpec(gw_init.shape, lambda b, i: (0, 0)))
        args.append(gw_init)
    bm = scale1p.shape[0]
    mod_spec = pl.BlockSpec((1, 8, d), (lambda b, i: (0, 0, 0)) if shared else (lambda b, i: (b, 0, 0)))
    out_specs = [pl.BlockSpec((nrows, d), lambda b, i: (0, 0)), mod_spec, pl.BlockSpec((8, d), lambda b, i: (0, 0))]
    out_shape = [jax.ShapeDtypeStruct((nrows, d), F32), jax.ShapeDtypeStruct((bm, 8, d), F32),
                 jax.ShapeDtypeStruct((8, d), F32)]
    if with_dx:
        out_specs.insert(0, tok(d))
        out_shape.insert(0, jax.ShapeDtypeStruct((bsz, s, d), F32))
    res = pl.pallas_call(
        body, name=name, grid=(bsz, s // tm), in_specs=in_specs, out_specs=out_specs, out_shape=out_shape,
        compiler_params=_params(2),
    )(*args)
    return res if with_dx else [None] + list(res)


def _adamw(w, g, m, v, name):
    r, cdim = w.shape
    tr = 256 if r % 256 == 0 and r > 256 else r

    def body(w_ref, g_ref, m_ref, v_ref, d_ref, nm_ref, nv_ref):
        gv = g_ref[...]
        mn = ADAM_B1 * m_ref[...] + (1.0 - ADAM_B1) * gv
        vn = ADAM_B2 * v_ref[...] + (1.0 - ADAM_B2) * (gv * gv)
        m_hat = mn / (1.0 - ADAM_B1 ** ADAM_STEP)
        v_hat = vn / (1.0 - ADAM_B2 ** ADAM_STEP)
        d_ref[...] = -ADAM_LR * (m_hat / (jnp.sqrt(v_hat) + ADAM_EPS) + ADAM_WD * w_ref[...])
        nm_ref[...] = mn
        nv_ref[...] = vn

    spec = pl.BlockSpec((tr, cdim), lambda i: (i, 0))
    return pl.pallas_call(
        body, name=name, grid=(r // tr,), in_specs=[spec] * 4, out_specs=[spec] * 3,
        out_shape=[jax.ShapeDtypeStruct((r, cdim), F32)] * 3, compiler_params=_params(1),
    )(w, g, m, v)


def _rope_tables(s):
    rows = s // GRID_W
    row = jnp.repeat(jnp.arange(rows, dtype=F32), GRID_W)
    col = jnp.tile(jnp.arange(GRID_W, dtype=F32), rows)
    freqs = ROPE_THETA ** (-jnp.arange(0, ROPE_AXIS_DIM, 2, dtype=F32) / ROPE_AXIS_DIM)
    ang_r, ang_c = row[:, None] * freqs[None, :], col[:, None] * freqs[None, :]
    cr, sr, cc, sc = jnp.cos(ang_r), jnp.sin(ang_r), jnp.cos(ang_c), jnp.sin(ang_c)
    cos64 = jnp.concatenate([cr, cr, cc, cc], axis=1)
    sin64 = jnp.concatenate([-sr, sr, -sc, sc], axis=1)
    return jnp.concatenate([cos64, cos64], axis=1), jnp.concatenate([sin64, sin64], axis=1)


def _pack_rows(parts, rows):
    flat = jnp.concatenate([p.reshape(-1) for p in parts])
    return jnp.pad(flat, (0, rows * D_MODEL - flat.shape[0])).reshape(rows, D_MODEL)


def kernel(x, c, ctx, c_ctx, w_mod, b_mod, norm_w, w_in, q_norm_w, k_norm_w, conv_w, conv_b, conv_ln_w, conv_ln_b, w_pw, b_pw, w_out, loss_target, m_c_ctx, m_w_mod, m_b_mod, m_norm_w, m_w_in, m_q_norm_w, m_k_norm_w, m_conv_w, m_conv_b, m_conv_ln_w, m_conv_ln_b, m_w_pw, m_b_pw, m_w_out, v_c_ctx, v_w_mod, v_b_mod, v_norm_w, v_w_in, v_q_norm_w, v_k_norm_w, v_conv_w, v_conv_b, v_conv_ln_w, v_conv_ln_b, v_w_pw, v_b_pw, v_w_out):
    bsz, s, d = x.shape
    cl = ctx.shape[1]
    xi, yi, ci = lax.axis_index("x"), lax.axis_index("y"), lax.axis_index("c")
    chip = 2 * xi + yi
    dev = 2 * chip + ci
    ncol_mod = w_mod.shape[2]

    w_in_t_loc = w_in[0].T.astype(BF16)
    conv_w_loc = jnp.pad(conv_w[0], ((0, 1), (0, 0)))
    g_in, g_out, g_pw, g_cw = _chip_gather(
        [w_in_t_loc, w_out[0].astype(BF16), w_pw[0].astype(BF16), conv_w_loc], "weight_gather")
    w_in_t = g_in.reshape(D_IN, d)
    w_out_f = g_out.reshape(d, d)
    w_pw_f = g_pw.reshape(D_CONV, D_CONV)
    conv_w_f = g_cw.transpose(1, 0, 2).reshape(32, D_CONV)

    c_all = _all_gather8(jnp.pad(c, ((0, 8 - bsz), (0, 0))), "c_gather").reshape(64, d)
    c_rows = jnp.concatenate([c_all, jnp.pad(c_ctx[None, :], ((0, 15), (0, 0)))], axis=0)
    b_cols = lax.dynamic_slice(b_mod, (0, chip * ncol_mod), (1, ncol_mod))
    sc_rows, mod_cols = _mod_forward(c_rows, w_mod[0], b_cols)
    mod_all = _chip_gather([mod_cols], "mod_gather")[0].transpose(1, 0, 2).reshape(80, 3 * d)
    mod_loc = lax.dynamic_slice(mod_all, (8 * dev, 0), (bsz, 3 * d))
    shift, scale1p, gate = mod_loc[:, None, :d], 1.0 + mod_loc[:, None, d:2 * d], mod_loc[:, None, 2 * d:]
    shift_c, scale1p_c = mod_all[64:65, :d][None], 1.0 + mod_all[64:65, d:2 * d][None]

    cos, sin = _rope_tables(s)
    qnw512 = jnp.tile(q_norm_w, (1, 8))
    knw128 = jnp.tile(k_norm_w, (1, 2))
    bd512 = jnp.kron(jnp.eye(8, dtype=F32), jnp.ones((HEAD_DIM, HEAD_DIM), F32)).astype(BF16)
    bd128 = bd512[:128, :128]

    u, p_qkv, p_za, p_ga, p_gg, p_zc = _norm_inproj(x, shift, scale1p, norm_w, w_in_t, SPLITS, 512, "norm_inproj")
    uc, pc_kv = _norm_inproj(ctx, shift_c, scale1p_c, norm_w, w_in_t[KV_LO:KV_HI], ((0, 256),), cl, "ctx_norm_inproj")
    q, k2x, v2x = _qkv_prep(p_qkv, cos, sin, qnw512, knw128, bd512, bd128, 256)
    k2c, v2c = _ctx_kv_prep(pc_kv, knw128, bd128)
    k2 = jnp.concatenate([k2c, k2x], axis=2)
    v2 = jnp.concatenate([v2c, v2x], axis=2)
    o, lse = _attn_forward(q, k2, v2, 256)
    y, cv = _conv_forward(p_ga, p_gg, conv_w_f, conv_b, conv_ln_w, conv_ln_b, w_pw_f, b_pw, 256)
    loss_part, dh, do, dza, dcv, dzc, dgate, gw_out = _outproj_loss(
        x, loss_target, gate, o, p_za, cv, p_zc, w_out_f, 256)

    dy, gw_pw, conv_stats = _conv_token_backward(dcv, y, conv_ln_w, conv_ln_b, w_pw_f, 256)
    da, dg, gcw = _conv_backward(dy, p_ga, p_gg, conv_w_f, 256)
    dq, dk2, dv2 = _attn_backward(q, k2, v2, o, do, lse, 256)
    dqkv, qk_stats = _qkv_backward(p_qkv, dq, dk2, dv2, cos, sin, qnw512, knw128, bd512, bd128, 256, cl)
    dpc, kc_stats = _ctx_kv_backward(pc_kv, dk2, dv2, knw128, bd128)
    _, gw_ctx, modc, gnw_c = _inproj_backward(
        [dpc], uc, ctx, None, scale1p_c, norm_w, w_in_t[KV_LO:KV_HI], None, cl, "ctx_inproj_backward")
    grad_x, gw_in_t, modx, gnw_x = _inproj_backward(
        [dqkv, dza, da, dg, dzc], u, x, dh, scale1p, norm_w, w_in_t, gw_ctx, 256, "inproj_backward")

    g_w_out, g_w_pw = _reduce_scatter([gw_out, gw_pw], "grad_out_pw_reduce")
    (g_w_in_t,) = _reduce_scatter([gw_in_t], "grad_in_reduce")
    dmod_loc = jnp.concatenate([modx[:, 0, :], modx[:, 1, :], dgate[:, 0, :]], axis=1)
    gq = qk_stats[0].reshape(8, HEAD_DIM).sum(axis=0)
    gk = (qk_stats[1, :128] + kc_stats[0]).reshape(2, HEAD_DIM).sum(axis=0)
    packed = _pack_rows([dmod_loc, dmod_loc.sum(axis=0), gnw_x[0] + gnw_c[0], modc[0, 0], modc[0, 1], gq, gk,
                         conv_stats[0], conv_stats[1], conv_stats[2], conv_stats[3], gcw,
                         jnp.sum(loss_part[:, 0, 0])[None]], 32)
    gathered, total = _all_gather8(packed, "small_grads", reduce=True)
    flat = total.reshape(-1)
    offs = [0]

    def take(nelem):
        lo = offs[0]
        offs[0] = lo + nelem
        return flat[lo:lo + nelem]

    take(bsz * 3 * d)
    g_b_mod_x = take(3 * d)
    g_norm_w = take(d)
    dshift_c, dscale_c = take(d), take(d)
    g_qnw, g_knw = take(HEAD_DIM), take(HEAD_DIM)
    g_b_pw, g_ln_w, g_ln_b, g_conv_b = take(D_CONV), take(D_CONV), take(D_CONV), take(D_CONV)
    g_conv_w_full = take(32 * D_CONV).reshape(32, D_CONV)
    loss = take(1)[0] * (0.5 / d)

    dmod_c = jnp.concatenate([dshift_c, dscale_c, jnp.zeros((d,), F32)])
    g_b_mod = (g_b_mod_x + dmod_c)[None, :]
    dmod_rows = jnp.pad(gathered[:, :6, :].reshape(8, bsz, 3 * d), ((0, 0), (0, 8 - bsz), (0, 0))).reshape(64, 3 * d)
    dmod_all = jnp.concatenate([dmod_rows, jnp.pad(dmod_c[None, :], ((0, 15), (0, 0)))], axis=0)
    dmod_cols = lax.dynamic_slice(dmod_all, (0, chip * ncol_mod), (80, ncol_mod))
    g_w_mod, gcc_part = _mod_backward(sc_rows, dmod_cols, w_mod[0])
    gcc_all = _all_gather8(gcc_part, "c_ctx_grad_gather")
    dsilu_ctx = gcc_all[0, 0] + gcc_all[2, 0] + gcc_all[4, 0] + gcc_all[6, 0]
    sg = _sigmoid(c_ctx)
    g_c_ctx = dsilu_ctx * (sg * (1.0 + c_ctx * (1.0 - sg)))

    g_w_in = g_w_in_t.T
    g_conv_w = lax.dynamic_slice(g_conv_w_full, (0, chip * 128), (CONV_WIDTH, 128))

    grads = {
        "c_ctx": g_c_ctx, "w_mod": g_w_mod[None], "b_mod": g_b_mod, "norm_w": g_norm_w[None], "w_in": g_w_in[None],
        "q_norm_w": g_qnw[None], "k_norm_w": g_knw[None], "conv_w": g_conv_w[None], "conv_b": g_conv_b[None],
        "conv_ln_w": g_ln_w[None], "conv_ln_b": g_ln_b[None], "w_pw": g_w_pw[None], "b_pw": g_b_pw[None],
        "w_out": g_w_out[None],
    }
    weights = {
        "c_ctx": (c_ctx, m_c_ctx, v_c_ctx), "w_mod": (w_mod, m_w_mod, v_w_mod), "b_mod": (b_mod, m_b_mod, v_b_mod),
        "norm_w": (norm_w, m_norm_w, v_norm_w), "w_in": (w_in, m_w_in, v_w_in),
        "q_norm_w": (q_norm_w, m_q_norm_w, v_q_norm_w), "k_norm_w": (k_norm_w, m_k_norm_w, v_k_norm_w),
        "conv_w": (conv_w, m_conv_w, v_conv_w), "conv_b": (conv_b, m_conv_b, v_conv_b),
        "conv_ln_w": (conv_ln_w, m_conv_ln_w, v_conv_ln_w), "conv_ln_b": (conv_ln_b, m_conv_ln_b, v_conv_ln_b),
        "w_pw": (w_pw, m_w_pw, v_w_pw), "b_pw": (b_pw, m_b_pw, v_b_pw), "w_out": (w_out, m_w_out, v_w_out),
    }
    names = list(weights)
    deltas, new_ms, new_vs = [], [], []
    for n in names:
        w, m, v = weights[n]
        shape = w.shape
        two_d = (1, shape[0]) if w.ndim == 1 else (shape[-2] if w.ndim == 3 else 1, shape[-1])
        dl, nm, nv = _adamw(w.reshape(two_d), grads[n].reshape(two_d), m.reshape(two_d), v.reshape(two_d), "adamw_" + n)
        deltas.append(dl.reshape(shape))
        new_ms.append(nm.reshape(shape))
        new_vs.append(nv.reshape(shape))
        grads[n] = grads[n].reshape(shape)

    return (loss, grad_x, *[grads[n] for n in names], *deltas, *new_ms, *new_vs)
```

```python
import functools
import math

import jax
import jax.numpy as jnp
import numpy as np
from jax import lax
from jax.experimental import pallas as pl
from jax.experimental.pallas import tpu as pltpu

F32 = jnp.float32
BF16 = jnp.bfloat16
MESH = pl.DeviceIdType.MESH

D_MODEL = 1024
D_ATTN = 512
D_CONV = 512
HEAD_DIM = 64
N_KV = 2
GRID_W = 64
ROPE_AXIS_DIM = 32
ROPE_THETA = 10000.0
CONV_WIDTH = 31
CONV_PAD = 15
HALO = 16
EPS = 1e-6
SPLITS = ((0, 768), (768, 1280), (1280, 1792), (1792, 2304), (2304, 2816))
D_IN = 2816
KV_LO, KV_HI = 512, 768

ADAM_LR = 0.001
ADAM_B1 = 0.9
ADAM_B2 = 0.999
ADAM_EPS = 1e-08
ADAM_WD = 0.01
ADAM_STEP = 10

VMEM_LIMIT = 56 * 1024 * 1024

NT = (((1,), (1,)), ((), ()))
TN = (((0,), (0,)), ((), ()))


def _params(n_axes=0, **kw):
    if n_axes:
        kw["dimension_semantics"] = ("arbitrary",) * n_axes
    return pltpu.CompilerParams(vmem_limit_bytes=VMEM_LIMIT, **kw)


def _sigmoid(x):
    return 1.0 / (1.0 + jnp.exp(-x))


def _silu_and_grad(z):
    s = _sigmoid(z)
    return z * s, s * (1.0 + z * (1.0 - s))


def _seg_mean(v, ones_bd):
    hi = v.astype(BF16)
    lo = (v - hi.astype(F32)).astype(BF16)
    s = jnp.dot(hi, ones_bd, preferred_element_type=F32) + jnp.dot(lo, ones_bd, preferred_element_type=F32)
    return s * (1.0 / HEAD_DIM)


def _partner(v):
    n = v.shape[1]
    lane = lax.broadcasted_iota(jnp.int32, (v.shape[0], 128), 1)
    first = (lane % 32) < 16
    parts = []
    for k in range(n // 128):
        ch = v[:, 128 * k:128 * (k + 1)]
        parts.append(jnp.where(first, pltpu.roll(ch, 112, 1), pltpu.roll(ch, 16, 1)))
    return parts[0] if len(parts) == 1 else jnp.concatenate(parts, axis=1)


def _tile_lanes(t, reps):
    return t if reps == 1 else jnp.concatenate([t] * reps, axis=1)


def _lo_mask(rows):
    return lax.broadcasted_iota(jnp.int32, (rows, 128), 1) < HEAD_DIM


def _all_gather8(x_local, name, reduce=False):
    r, cdim = x_local.shape

    def body(x_ref, out_ref, *rest):
        if reduce:
            sum_ref, send_sems, recv_sems, local_sem = rest
        else:
            send_sems, recv_sems, local_sem = rest
        x, y, c = lax.axis_index("x"), lax.axis_index("y"), lax.axis_index("c")
        me, sibling = (x, y, c), (x, y, 1 - c)
        chips = [(1 - x, y), (x, 1 - y), (1 - x, 1 - y)]

        def slot(px, py, pc):
            return out_ref.at[4 * px + 2 * py + pc]

        def copy(k, block, to, src=None):
            return pltpu.make_async_remote_copy(
                src_ref=slot(*block) if src is None else src, dst_ref=slot(*block),
                send_sem=send_sems.at[k], recv_sem=recv_sems.at[k], device_id=to, device_id_type=MESH)

        mine = pltpu.make_async_copy(x_ref, slot(*me), local_sem)
        mine.start()
        first = [copy(0, me, sibling, src=x_ref)]
        first += [copy(1 + j, me, (*chip, c), src=x_ref) for j, chip in enumerate(chips)]
        for cp in first:
            cp.start()
        passed = [copy(4 + j, (*chip, c), sibling) for j, chip in enumerate(chips)]
        for j, chip in enumerate(chips):
            copy(1 + j, (*chip, c), me).wait_recv()
            passed[j].start()
        copy(0, sibling, me).wait_recv()
        for j, chip in enumerate(chips):
            copy(4 + j, (*chip, 1 - c), me).wait_recv()
        for cp in first + passed:
            cp.wait_send()
        mine.wait()
        if reduce:
            acc = out_ref[0]
            for d in range(1, 8):
                acc = acc + out_ref[d]
            sum_ref[...] = acc

    out_shape = [jax.ShapeDtypeStruct((8, r, cdim), x_local.dtype)]
    out_specs = [pl.BlockSpec(memory_space=pltpu.VMEM)]
    if reduce:
        out_shape.append(jax.ShapeDtypeStruct((r, cdim), x_local.dtype))
        out_specs.append(pl.BlockSpec(memory_space=pltpu.VMEM))
    res = pl.pallas_call(
        body, name=name, out_shape=out_shape,
        in_specs=[pl.BlockSpec(memory_space=pltpu.VMEM)], out_specs=out_specs,
        scratch_shapes=[pltpu.SemaphoreType.DMA((7,)), pltpu.SemaphoreType.DMA((7,)), pltpu.SemaphoreType.DMA],
        compiler_params=_params(),
    )(x_local)
    return res if reduce else res[0]


def _chip_gather(arrs, name):
    n = len(arrs)

    def body(*refs):
        ins, outs = refs[:n], refs[n:2 * n]
        send_sems, recv_sems, local_sems = refs[2 * n:]
        x, y, c = lax.axis_index("x"), lax.axis_index("y"), lax.axis_index("c")
        sibling = (x, y, 1 - c)
        chips = [(1 - x, y), (x, 1 - y), (1 - x, 1 - y)]
        mychip = 2 * x + y

        def half(a, chip_idx, cc):
            h = arrs[a].shape[0] // 2
            return outs[a].at[chip_idx, pl.ds(cc * h, h)]

        def copy(a, k, chip_idx, cc, to, src=None):
            dst = half(a, chip_idx, cc)
            return pltpu.make_async_remote_copy(
                src_ref=dst if src is None else src, dst_ref=dst,
                send_sem=send_sems.at[6 * a + k], recv_sem=recv_sems.at[6 * a + k], device_id=to, device_id_type=MESH)

        locals_, firsts, passed = [], [], []
        for a in range(n):
            h = arrs[a].shape[0] // 2
            loc = pltpu.make_async_copy(ins[a], outs[a].at[mychip], local_sems.at[a])
            loc.start()
            locals_.append(loc)
            for j, chip in enumerate(chips):
                cp = copy(a, j, mychip, c, (*chip, c), src=ins[a].at[pl.ds(c * h, h)])
                cp.start()
                firsts.append(cp)
        for a in range(n):
            for j, (cx, cy) in enumerate(chips):
                copy(a, j, 2 * cx + cy, c, (x, y, c)).wait_recv()
                fw = copy(a, 3 + j, 2 * cx + cy, c, sibling)
                fw.start()
                passed.append(fw)
        for a in range(n):
            for j, (cx, cy) in enumerate(chips):
                copy(a, 3 + j, 2 * cx + cy, 1 - c, (x, y, c)).wait_recv()
        for cp in firsts + passed:
            cp.wait_send()
        for loc in locals_:
            loc.wait()

    return pl.pallas_call(
        body, name=name,
        out_shape=[jax.ShapeDtypeStruct((4,) + a.shape, a.dtype) for a in arrs],
        in_specs=[pl.BlockSpec(memory_space=pl.ANY)] * n, out_specs=[pl.BlockSpec(memory_space=pl.ANY)] * n,
        scratch_shapes=[pltpu.SemaphoreType.DMA((6 * n,)), pltpu.SemaphoreType.DMA((6 * n,)),
                        pltpu.SemaphoreType.DMA((n,))],
        compiler_params=_params(),
    )(*arrs)


def _reduce_scatter(grads, name):
    n = len(grads)
    hs = [g.shape[0] // 8 for g in grads]
    views = [g.reshape(4, 2, h, g.shape[1]) for g, h in zip(grads, hs)]

    def body(*refs):
        gs, outs = refs[:n], refs[n:2 * n]
        scr = refs[2 * n:]
        recv_a, tsum, recv_b = scr[:n], scr[n:2 * n], scr[2 * n:3 * n]
        send_sems, recv_sems = scr[3 * n:]
        x, y, c = lax.axis_index("x"), lax.axis_index("y"), lax.axis_index("c")
        sibling = (x, y, 1 - c)
        chips = [(1 - x, y), (x, 1 - y), (1 - x, 1 - y)]
        mychip = 2 * x + y

        def rcopy(a, k, src, dst, to):
            return pltpu.make_async_remote_copy(
                src_ref=src, dst_ref=dst, send_sem=send_sems.at[8 * a + k], recv_sem=recv_sems.at[8 * a + k],
                device_id=to, device_id_type=MESH)

        stage_a, stage_b, stage_c = [], [], []
        for a in range(n):
            for k in range(4):
                cp = rcopy(a, k, gs[a].at[k, 1 - c], recv_a[a].at[k], sibling)
                cp.start()
                stage_a.append(cp)
        for a in range(n):
            for j, (cx, cy) in enumerate(chips):
                k = 2 * cx + cy
                rcopy(a, k, gs[a].at[k, 1 - c], recv_a[a].at[k], sibling).wait_recv()
                tsum[a][j] = (gs[a][k, c] + recv_a[a][k]).astype(BF16)
                cp = rcopy(a, 4 + j, tsum[a].at[j], recv_b[a].at[j], (cx, cy, c))
                cp.start()
                stage_b.append(cp)
            rcopy(a, mychip, gs[a].at[mychip, 1 - c], recv_a[a].at[mychip], sibling).wait_recv()
        for a in range(n):
            acc = gs[a][mychip, c] + recv_a[a][mychip]
            for j in range(3):
                rcopy(a, 4 + j, tsum[a].at[j], recv_b[a].at[j], sibling).wait_recv()
                acc = acc + recv_b[a][j].astype(F32)
            outs[a][c] = acc
            cp = rcopy(a, 7, outs[a].at[c], outs[a].at[c], sibling)
            cp.start()
            stage_c.append(cp)
        for a in range(n):
            rcopy(a, 7, outs[a].at[1 - c], outs[a].at[1 - c], sibling).wait_recv()
        for cp in stage_a + stage_b + stage_c:
            cp.wait_send()

    scratch = [pltpu.VMEM((4, h, g.shape[1]), F32) for g, h in zip(grads, hs)]
    scratch += [pltpu.VMEM((3, h, g.shape[1]), BF16) for g, h in zip(grads, hs)]
    scratch += [pltpu.VMEM((3, h, g.shape[1]), BF16) for g, h in zip(grads, hs)]
    scratch += [pltpu.SemaphoreType.DMA((8 * n,)), pltpu.SemaphoreType.DMA((8 * n,))]
    res = pl.pallas_call(
        body, name=name,
        out_shape=[jax.ShapeDtypeStruct((2, h, g.shape[1]), F32) for g, h in zip(grads, hs)],
        in_specs=[pl.BlockSpec(memory_space=pltpu.VMEM)] * n, out_specs=[pl.BlockSpec(memory_space=pltpu.VMEM)] * n,
        scratch_shapes=scratch, compiler_params=_params(),
    )(*views)
    return [o.reshape(2 * h, o.shape[2]) for o, h in zip(res, hs)]


def _mod_forward(c_rows, w_mod, b_cols):
    rows, ncol = c_rows.shape[0], w_mod.shape[1]

    def body(c_ref, w_ref, b_ref, sc_ref, mod_ref):
        cv = c_ref[...]
        sc = cv * _sigmoid(cv)
        sc_ref[...] = sc
        mod_ref[...] = jnp.dot(sc, w_ref[...], preferred_element_type=F32, precision=lax.Precision.HIGHEST) + b_ref[...]

    return pl.pallas_call(
        body, name="mod_forward",
        out_shape=[jax.ShapeDtypeStruct((rows, D_MODEL), F32), jax.ShapeDtypeStruct((rows, ncol), F32)],
        compiler_params=_params(),
    )(c_rows, w_mod, b_cols)


def _mod_backward(sc_rows, dmod_cols, w_mod):
    rows, ncol = dmod_cols.shape

    def body(sc_ref, dm_ref, w_ref, gw_ref, gcc_ref):
        gw_ref[...] = lax.dot_general(sc_ref[...], dm_ref[...], TN, preferred_element_type=F32,
                                      precision=lax.Precision.HIGHEST)
        gcc_ref[...] = lax.dot_general(dm_ref[64:72, :], w_ref[...], NT, preferred_element_type=F32,
                                       precision=lax.Precision.HIGHEST)

    return pl.pallas_call(
        body, name="mod_backward",
        out_shape=[jax.ShapeDtypeStruct((D_MODEL, ncol), F32), jax.ShapeDtypeStruct((8, D_MODEL), F32)],
        compiler_params=_params(),
    )(sc_rows, dmod_cols, w_mod)


def _bcast_spec(arr):
    if arr.shape[0] == 1:
        return pl.BlockSpec((1, 1, arr.shape[2]), lambda b, i: (0, 0, 0))
    return pl.BlockSpec((1, 1, arr.shape[2]), lambda b, i: (b, 0, 0))


def _norm_inproj(x, shift, scale1p, norm_w, w_t, splits, tm, name):
    bsz, s, d = x.shape

    def body(x_ref, sh_ref, sc_ref, nw_ref, w_ref, u_ref, *out_refs):
        xv = x_ref[0]
        rstd = lax.rsqrt(jnp.mean(xv * xv, axis=-1, keepdims=True) + EPS)
        u = (xv * rstd * nw_ref[...]) * sc_ref[0] + sh_ref[0]
        ub = u.astype(BF16)
        u_ref[0] = ub
        for (lo, hi), o_ref in zip(splits, out_refs):
            o_ref[0] = lax.dot_general(ub, w_ref[lo:hi, :], NT, preferred_element_type=F32)

    tok = lambda w: pl.BlockSpec((1, tm, w), lambda b, i: (b, i, 0))
    return pl.pallas_call(
        body, name=name, grid=(bsz, s // tm),
        in_specs=[tok(d), _bcast_spec(shift), _bcast_spec(scale1p), pl.BlockSpec((1, d), lambda b, i: (0, 0)),
                  pl.BlockSpec(w_t.shape, lambda b, i: (0, 0))],
        out_specs=[tok(d)] + [tok(hi - lo) for lo, hi in splits],
        out_shape=[jax.ShapeDtypeStruct((bsz, s, d), BF16)]
        + [jax.ShapeDtypeStruct((bsz, s, hi - lo), F32) for lo, hi in splits],
        compiler_params=_params(2),
    )(x, shift, scale1p, norm_w, w_t)


def _dup_heads(kv, lo_mask):
    r = pltpu.roll(kv, HEAD_DIM, 1)
    return jnp.where(lo_mask, kv, r), jnp.where(lo_mask, r, kv)


def _qkv_prep(qkv, cos, sin, qnw, knw, bd512, bd128, ts, row0):
    bsz, s, _ = qkv.shape

    def body(p_ref, cos_ref, sin_ref, qnw_ref, knw_ref, bd512_ref, bd128_ref, q_ref, k_ref, v_ref):
        lo_mask = _lo_mask(ts)
        cos_t, sin_t = cos_ref[...], sin_ref[...]
        qp = p_ref[0, :, 0:512]
        qn = qp * lax.rsqrt(_seg_mean(qp * qp, bd512_ref[...]) + EPS) * qnw_ref[...]
        qr = qn * _tile_lanes(cos_t, 4) + _partner(qn) * _tile_lanes(sin_t, 4)
        q_ref[0] = (qr * (1.0 / math.sqrt(HEAD_DIM))).astype(BF16)
        kp = p_ref[0, :, 512:640]
        kn = kp * lax.rsqrt(_seg_mean(kp * kp, bd128_ref[...]) + EPS) * knw_ref[...]
        kr = kn * cos_t + _partner(kn) * sin_t
        k0, k1 = _dup_heads(kr, lo_mask)
        k_ref[0, 0] = k0.astype(BF16)
        k_ref[0, 1] = k1.astype(BF16)
        v0, v1 = _dup_heads(p_ref[0, :, 640:768], lo_mask)
        v_ref[0, 0] = v0.astype(BF16)
        v_ref[0, 1] = v1.astype(BF16)

    const = lambda a: pl.BlockSpec(a.shape, lambda b, i: (0,) * a.ndim)
    kv_spec = pl.BlockSpec((1, 2, ts, 128), lambda b, i: (b, 0, i + row0 // ts, 0))
    return pl.pallas_call(
        body, name="qkv_prep", grid=(bsz, s // ts),
        in_specs=[pl.BlockSpec((1, ts, 768), lambda b, i: (b, i, 0)),
                  pl.BlockSpec((ts, 128), lambda b, i: (i, 0)), pl.BlockSpec((ts, 128), lambda b, i: (i, 0)),
                  const(qnw), const(knw), const(bd512), const(bd128)],
        out_specs=[pl.BlockSpec((1, ts, 512), lambda b, i: (b, i, 0)), kv_spec, kv_spec],
        out_shape=[jax.ShapeDtypeStruct((bsz, s, 512), BF16), jax.ShapeDtypeStruct((bsz, 2, row0 + s, 128), BF16),
                   jax.ShapeDtypeStruct((bsz, 2, row0 + s, 128), BF16)],
        compiler_params=_params(2),
    )(qkv, cos, sin, qnw, knw, bd512, bd128)


def _ctx_kv_prep(pc, knw, bd128, k2, v2):
    bsz, cl, _ = pc.shape

    def body(p_ref, knw_ref, bd128_ref, k_in, v_in, k_ref, v_ref):
        lo_mask = _lo_mask(cl)
        kp = p_ref[0, :, 0:128]
        kn = kp * lax.rsqrt(_seg_mean(kp * kp, bd128_ref[...]) + EPS) * knw_ref[...]
        k0, k1 = _dup_heads(kn, lo_mask)
        k_ref[0, 0] = k0.astype(BF16)
        k_ref[0, 1] = k1.astype(BF16)
        v0, v1 = _dup_heads(p_ref[0, :, 128:256], lo_mask)
        v_ref[0, 0] = v0.astype(BF16)
        v_ref[0, 1] = v1.astype(BF16)

    const = lambda a: pl.BlockSpec(a.shape, lambda b: (0,) * a.ndim)
    kv_spec = pl.BlockSpec((1, 2, cl, 128), lambda b: (b, 0, 0, 0))
    return pl.pallas_call(
        body, name="ctx_kv_prep", grid=(bsz,),
        in_specs=[pl.BlockSpec((1, cl, 256), lambda b: (b, 0, 0)), const(knw), const(bd128),
                  pl.BlockSpec(memory_space=pl.ANY), pl.BlockSpec(memory_space=pl.ANY)],
        out_specs=[kv_spec, kv_spec],
        out_shape=[jax.ShapeDtypeStruct(k2.shape, BF16), jax.ShapeDtypeStruct(v2.shape, BF16)],
        input_output_aliases={3: 0, 4: 1},
        compiler_params=_params(1),
    )(pc, knw, bd128, k2, v2)


def _attn_forward(q, k2, v2, tq):
    bsz, s, _ = q.shape
    sk = k2.shape[2]

    def body(q_ref, k_ref, v_ref, o_ref, lse_ref):
        kk, vv = k_ref[0, 0], v_ref[0, 0]
        lo_mask = _lo_mask(tq)
        for j in range(2):
            qp = q_ref[0, :, 128 * j:128 * (j + 1)]
            outs, lses = [], []
            for half in range(2):
                sel = lo_mask if half == 0 else jnp.logical_not(lo_mask)
                qs = jnp.where(sel, qp, jnp.zeros_like(qp))
                sc = lax.dot_general(qs, kk, NT, preferred_element_type=F32)
                m = jnp.max(sc, axis=-1, keepdims=True)
                p = jnp.exp(sc - m)
                l = jnp.sum(p, axis=-1, keepdims=True)
                o = jnp.dot(p.astype(BF16), vv, preferred_element_type=F32)
                outs.append(o / l)
                lses.append(jnp.broadcast_to(m + jnp.log(l), (tq, 128)))
            o_ref[0, :, 128 * j:128 * (j + 1)] = jnp.where(lo_mask, outs[0], outs[1])
            lse_ref[0, :, 128 * j:128 * (j + 1)] = jnp.where(lo_mask, lses[0], lses[1])

    q_spec = pl.BlockSpec((1, tq, 256), lambda b, g, i: (b, i, g))
    kv_spec = pl.BlockSpec((1, 1, sk, 128), lambda b, g, i: (b, g, 0, 0))
    return pl.pallas_call(
        body, name="attn_forward", grid=(bsz, N_KV, s // tq),
        in_specs=[q_spec, kv_spec, kv_spec], out_specs=[q_spec, q_spec],
        out_shape=[jax.ShapeDtypeStruct((bsz, s, 512), F32)] * 2,
        compiler_params=_params(3),
    )(q, k2, v2)


def _halo_specs(width, ts, s):
    r = ts // HALO
    last = s // HALO - 1
    return [pl.BlockSpec((1, ts, width), lambda b, i: (b, i, 0)),
            pl.BlockSpec((1, HALO, width), lambda b, i: (b, jnp.maximum(i * r - 1, 0), 0)),
            pl.BlockSpec((1, HALO, width), lambda b, i: (b, jnp.minimum((i + 1) * r, last), 0))]


def _fill_ext(ext_ref, cur, prev, nxt, i, n_tiles, ts):
    ext_ref[0:HALO, :] = jnp.where(i > 0, prev, jnp.zeros_like(prev))
    ext_ref[HALO:HALO + ts, :] = cur
    ext_ref[HALO + ts:2 * HALO + ts, :] = jnp.where(i < n_tiles - 1, nxt, jnp.zeros_like(nxt))


def _fill_shifted(sh_ref, ext_ref, ts):
    n = ts + 2 * HALO - 8
    for r in range(1, 8):
        sh_ref[r - 1, 0:n, :] = ext_ref[pl.ds(r, n), :]


def _window(sh_ref, ext_ref, off, ts):
    q, r = divmod(off, 8)
    if r == 0:
        return ext_ref[pl.ds(off, ts), :]
    return sh_ref[r - 1, pl.ds(8 * q, ts), :]


def _conv_forward(ga, gg, conv_w, conv_b, ln_w, ln_b, w_pw, b_pw, ts):
    bsz, s, dc = ga.shape
    n_tiles = s // ts

    def body(a_ref, ap_ref, an_ref, g_ref, gp_ref, gn_ref, cw_ref, cb_ref, lw_ref, lb_ref, wp_ref, bp_ref,
             y_ref, cv_ref, ext_ref, sh_ref):
        i = pl.program_id(1)
        glu = lambda a, g: a * _sigmoid(g)
        _fill_ext(ext_ref, glu(a_ref[0], g_ref[0]), glu(ap_ref[0], gp_ref[0]), glu(an_ref[0], gn_ref[0]), i, n_tiles, ts)
        _fill_shifted(sh_ref, ext_ref, ts)
        acc = jnp.broadcast_to(cb_ref[...], (ts, dc))
        for j in range(CONV_WIDTH):
            acc = acc + cw_ref[j:j + 1, :] * _window(sh_ref, ext_ref, HALO - CONV_PAD + j, ts)
        y_ref[0] = acc
        mu = jnp.mean(acc, axis=-1, keepdims=True)
        yc = acc - mu
        var = jnp.mean(yc * yc, axis=-1, keepdims=True)
        yn = yc * lax.rsqrt(var + EPS) * lw_ref[...] + lb_ref[...]
        ys = yn * _sigmoid(yn)
        cv_ref[0] = jnp.dot(ys.astype(BF16), wp_ref[...], preferred_element_type=F32) + bp_ref[...]

    const = lambda a: pl.BlockSpec(a.shape, lambda b, i: (0,) * a.ndim)
    return pl.pallas_call(
        body, name="conv_forward", grid=(bsz, n_tiles),
        in_specs=_halo_specs(dc, ts, s) + _halo_specs(dc, ts, s)
        + [const(conv_w), const(conv_b), const(ln_w), const(ln_b), const(w_pw), const(b_pw)],
        out_specs=[pl.BlockSpec((1, ts, dc), lambda b, i: (b, i, 0))] * 2,
        out_shape=[jax.ShapeDtypeStruct((bsz, s, dc), F32)] * 2,
        scratch_shapes=[pltpu.VMEM((ts + 2 * HALO, dc), F32), pltpu.VMEM((7, ts + 2 * HALO, dc), F32)],
        compiler_params=_params(2),
    )(ga, ga, ga, gg, gg, gg, conv_w, conv_b, ln_w, ln_b, w_pw, b_pw)


def _outproj_loss(x, target, gate, o, za, cv, zc, w_out, tm):
    bsz, s, d = x.shape

    def body(x_ref, t_ref, gate_ref, o_ref, za_ref, cv_ref, zc_ref, w_ref,
             loss_ref, dh_ref, do_ref, dza_ref, dcv_ref, dzc_ref, dgate_ref, gw_ref):
        b, i = pl.program_id(0), pl.program_id(1)
        ov, cvv = o_ref[0], cv_ref[0]
        silu_a, dsilu_a = _silu_and_grad(za_ref[0])
        silu_c, dsilu_c = _silu_and_grad(zc_ref[0])
        mix = jnp.concatenate([ov * silu_a, cvv * silu_c], axis=1).astype(BF16)
        out = jnp.dot(mix, w_ref[...], preferred_element_type=F32)
        gate_v = gate_ref[0]
        err = x_ref[0] + gate_v * out - t_ref[0]
        dh = err * (1.0 / d)
        dh_ref[0] = dh
        dout = (dh * gate_v).astype(BF16)
        dmix = lax.dot_general(dout, w_ref[...], NT, preferred_element_type=F32)
        gw = lax.dot_general(mix, dout, TN, preferred_element_type=F32)
        dg = jnp.sum(dh * out, axis=0, keepdims=True)
        sq = jnp.sum(err * err)

        @pl.when(jnp.logical_and(b == 0, i == 0))
        def _():
            gw_ref[...] = gw

        @pl.when(jnp.logical_or(b > 0, i > 0))
        def _():
            gw_ref[...] += gw

        @pl.when(i == 0)
        def _():
            dgate_ref[0] = dg
            loss_ref[...] = jnp.zeros(loss_ref.shape, F32) + sq

        @pl.when(i > 0)
        def _():
            dgate_ref[0] += dg
            loss_ref[...] += sq

        dma, dmc = dmix[:, :D_ATTN], dmix[:, D_ATTN:]
        do_ref[0] = dma * silu_a
        dza_ref[0] = (dma * ov * dsilu_a).astype(BF16)
        dcv_ref[0] = dmc * silu_c
        dzc_ref[0] = (dmc * cvv * dsilu_c).astype(BF16)

    tok = lambda w: pl.BlockSpec((1, tm, w), lambda b, i: (b, i, 0))
    return pl.pallas_call(
        body, name="outproj_loss", grid=(bsz, s // tm),
        in_specs=[tok(d), tok(d), _bcast_spec(gate), tok(512), tok(512), tok(512), tok(512),
                  pl.BlockSpec(w_out.shape, lambda b, i: (0, 0))],
        out_specs=[pl.BlockSpec((1, 8, 128), lambda b, i: (b, 0, 0)), tok(d), tok(512), tok(512), tok(512), tok(512),
                   pl.BlockSpec((1, 1, d), lambda b, i: (b, 0, 0)), pl.BlockSpec((d, d), lambda b, i: (0, 0))],
        out_shape=[jax.ShapeDtypeStruct((bsz, 8, 128), F32), jax.ShapeDtypeStruct((bsz, s, d), F32),
                   jax.ShapeDtypeStruct((bsz, s, 512), F32), jax.ShapeDtypeStruct((bsz, s, 512), BF16),
                   jax.ShapeDtypeStruct((bsz, s, 512), F32), jax.ShapeDtypeStruct((bsz, s, 512), BF16),
                   jax.ShapeDtypeStruct((bsz, 1, d), F32), jax.ShapeDtypeStruct((d, d), F32)],
        compiler_params=_params(2),
    )(x, target, gate, o, za, cv, zc, w_out)


def _conv_token_backward(dcv, y, ln_w, ln_b, w_pw, tm):
    bsz, s, dc = dcv.shape

    def body(dcv_ref, y_ref, lw_ref, lb_ref, wp_ref, dy_ref, gwp_ref, st_ref):
        b, i = pl.program_id(0), pl.program_id(1)
        yv, dcvv = y_ref[0], dcv_ref[0]
        mu = jnp.mean(yv, axis=-1, keepdims=True)
        yc = yv - mu
        rstd = lax.rsqrt(jnp.mean(yc * yc, axis=-1, keepdims=True) + EPS)
        yhat = yc * rstd
        yn = yhat * lw_ref[...] + lb_ref[...]
        ys, dsilu = _silu_and_grad(yn)
        dcvb = dcvv.astype(BF16)
        gwp = lax.dot_general(ys.astype(BF16), dcvb, TN, preferred_element_type=F32)
        dys = lax.dot_general(dcvb, wp_ref[...], NT, preferred_element_type=F32)
        dyn = dys * dsilu
        dyhat = dyn * lw_ref[...]
        dy = rstd * (dyhat - jnp.mean(dyhat, axis=-1, keepdims=True)
                     - yhat * jnp.mean(dyhat * yhat, axis=-1, keepdims=True))
        dy_ref[0] = dy
        red = lambda v: jnp.sum(v, axis=0, keepdims=True)
        stats = jnp.concatenate([red(dcvv), red(dyn * yhat), red(dyn), red(dy), jnp.zeros((4, dc), F32)], axis=0)
        first = jnp.logical_and(b == 0, i == 0)

        @pl.when(first)
        def _():
            gwp_ref[...] = gwp
            st_ref[...] = stats

        @pl.when(jnp.logical_not(first))
        def _():
            gwp_ref[...] += gwp
            st_ref[...] += stats

    tok = pl.BlockSpec((1, tm, dc), lambda b, i: (b, i, 0))
    const = lambda a: pl.BlockSpec(a.shape, lambda b, i: (0,) * a.ndim)
    return pl.pallas_call(
        body, name="conv_token_backward", grid=(bsz, s // tm),
        in_specs=[tok, tok, const(ln_w), const(ln_b), const(w_pw)],
        out_specs=[tok, pl.BlockSpec((dc, dc), lambda b, i: (0, 0)), pl.BlockSpec((8, dc), lambda b, i: (0, 0))],
        out_shape=[jax.ShapeDtypeStruct((bsz, s, dc), F32), jax.ShapeDtypeStruct((dc, dc), F32),
                   jax.ShapeDtypeStruct((8, dc), F32)],
        compiler_params=_params(2),
    )(dcv, y, ln_w, ln_b, w_pw)


def _conv_backward(dy, ga, gg, conv_w, ts):
    bsz, s, dc = dy.shape
    n_tiles = s // ts

    def body(dy_ref, dyp_ref, dyn_ref, a_ref, ap_ref, an_ref, g_ref, gp_ref, gn_ref, cw_ref,
             da_ref, dg_ref, gcw_ref, dyext_ref, ugext_ref, dysh_ref, ugsh_ref):
        b, i = pl.program_id(0), pl.program_id(1)
        dyv = dy_ref[0]
        av, sg = a_ref[0], _sigmoid(g_ref[0])
        glu = lambda a, g: a * _sigmoid(g)
        _fill_ext(dyext_ref, dyv, dyp_ref[0], dyn_ref[0], i, n_tiles, ts)
        _fill_ext(ugext_ref, av * sg, glu(ap_ref[0], gp_ref[0]), glu(an_ref[0], gn_ref[0]), i, n_tiles, ts)
        _fill_shifted(dysh_ref, dyext_ref, ts)
        _fill_shifted(ugsh_ref, ugext_ref, ts)
        dug = jnp.zeros((ts, dc), F32)
        rows = []
        for j in range(CONV_WIDTH):
            dug = dug + cw_ref[j:j + 1, :] * _window(dysh_ref, dyext_ref, HALO + CONV_PAD - j, ts)
            rows.append(jnp.sum(dyv * _window(ugsh_ref, ugext_ref, HALO - CONV_PAD + j, ts), axis=0, keepdims=True))
        rows.append(jnp.zeros((1, dc), F32))
        gcw = jnp.concatenate(rows, axis=0)
        first = jnp.logical_and(b == 0, i == 0)

        @pl.when(first)
        def _():
            gcw_ref[...] = gcw

        @pl.when(jnp.logical_not(first))
        def _():
            gcw_ref[...] += gcw

        da_ref[0] = (dug * sg).astype(BF16)
        dg_ref[0] = (dug * av * sg * (1.0 - sg)).astype(BF16)

    tok = pl.BlockSpec((1, ts, dc), lambda b, i: (b, i, 0))
    return pl.pallas_call(
        body, name="conv_backward", grid=(bsz, n_tiles),
        in_specs=_halo_specs(dc, ts, s) + _halo_specs(dc, ts, s) + _halo_specs(dc, ts, s)
        + [pl.BlockSpec(conv_w.shape, lambda b, i: (0, 0))],
        out_specs=[tok, tok, pl.BlockSpec((32, dc), lambda b, i: (0, 0))],
        out_shape=[jax.ShapeDtypeStruct((bsz, s, dc), BF16), jax.ShapeDtypeStruct((bsz, s, dc), BF16),
                   jax.ShapeDtypeStruct((32, dc), F32)],
        scratch_shapes=[pltpu.VMEM((ts + 2 * HALO, dc), F32), pltpu.VMEM((ts + 2 * HALO, dc), F32),
                        pltpu.VMEM((7, ts + 2 * HALO, dc), F32), pltpu.VMEM((7, ts + 2 * HALO, dc), F32)],
        compiler_params=_params(2),
    )(dy, dy, dy, ga, ga, ga, gg, gg, gg, conv_w)


def _attn_backward(q, k2, v2, o, do, lse, tq):
    bsz, s, _ = q.shape
    sk = k2.shape[2]
    scale = 1.0 / math.sqrt(HEAD_DIM)

    def body(q_ref, k_ref, v_ref, o_ref, do_ref, lse_ref, dq_ref, dk_ref, dv_ref):
        i = pl.program_id(2)
        kk, vv = k_ref[0, 0], v_ref[0, 0]
        lo_mask = _lo_mask(tq)
        dk_acc = jnp.zeros((sk, 128), F32)
        dv_acc = jnp.zeros((sk, 128), F32)
        for j in range(2):
            cols = slice(128 * j, 128 * (j + 1))
            qp, dop, lsep = q_ref[0, :, cols], do_ref[0, :, cols], lse_ref[0, :, cols]
            dprod = dop * o_ref[0, :, cols]
            dqs = []
            for half in range(2):
                sel = lo_mask if half == 0 else jnp.logical_not(lo_mask)
                qs = jnp.where(sel, qp, jnp.zeros_like(qp))
                dos = jnp.where(sel, dop, 0.0).astype(BF16)
                lse_h = jnp.max(jnp.where(sel, lsep, -jnp.inf), axis=-1, keepdims=True)
                delta = jnp.sum(jnp.where(sel, dprod, 0.0), axis=-1, keepdims=True)
                sc = lax.dot_general(qs, kk, NT, preferred_element_type=F32)
                p = jnp.exp(sc - lse_h)
                dp = lax.dot_general(dos, vv, NT, preferred_element_type=F32)
                ds = (p * (dp - delta)).astype(BF16)
                dv_acc = dv_acc + lax.dot_general(p.astype(BF16), dos, TN, preferred_element_type=F32)
                dk_acc = dk_acc + lax.dot_general(ds, qs, TN, preferred_element_type=F32)
                dqs.append(jnp.dot(ds, kk, preferred_element_type=F32))
            dq_ref[0, :, cols] = jnp.where(lo_mask, dqs[0], dqs[1]) * scale

        @pl.when(i == 0)
        def _():
            dk_ref[0, 0] = dk_acc
            dv_ref[0, 0] = dv_acc

        @pl.when(i > 0)
        def _():
            dk_ref[0, 0] += dk_acc
            dv_ref[0, 0] += dv_acc

    q_spec = pl.BlockSpec((1, tq, 256), lambda b, g, i: (b, i, g))
    kv_spec = pl.BlockSpec((1, 1, sk, 128), lambda b, g, i: (b, g, 0, 0))
    return pl.pallas_call(
        body, name="attn_backward", grid=(bsz, N_KV, s // tq),
        in_specs=[q_spec, kv_spec, kv_spec, q_spec, q_spec, q_spec],
        out_specs=[q_spec, kv_spec, kv_spec],
        out_shape=[jax.ShapeDtypeStruct((bsz, s, 512), F32), jax.ShapeDtypeStruct((bsz, 2, sk, 128), F32),
                   jax.ShapeDtypeStruct((bsz, 2, sk, 128), F32)],
        compiler_params=_params(3),
    )(q, k2, v2, o, do, lse)


def _fold_heads(acc2_ref_val0, acc2_ref_val1, lo_mask):
    f0 = acc2_ref_val0 + pltpu.roll(acc2_ref_val0, HEAD_DIM, 1)
    f1 = acc2_ref_val1 + pltpu.roll(acc2_ref_val1, HEAD_DIM, 1)
    return jnp.where(lo_mask, f0, f1)


def _norm_backward(dn, pre, w, bd):
    rstd = lax.rsqrt(_seg_mean(pre * pre, bd) + EPS)
    xhat = pre * rstd
    dxhat = dn * w
    return rstd * (dxhat - xhat * _seg_mean(dxhat * xhat, bd)), dn * xhat


def _qkv_backward(qkv, dq, dk2, dv2, cos, sin, qnw, knw, bd512, bd128, ts, row0):
    bsz, s, _ = qkv.shape

    def body(p_ref, dq_ref, dk_ref, dv_ref, cos_ref, sin_ref, qnw_ref, knw_ref, bd512_ref, bd128_ref, d_ref, gw_ref):
        b, i = pl.program_id(0), pl.program_id(1)
        lo_mask = _lo_mask(ts)
        cos_t, sin_t = cos_ref[...], sin_ref[...]
        dqr = dq_ref[0]
        dqn = dqr * _tile_lanes(cos_t, 4) + _partner(dqr * _tile_lanes(sin_t, 4))
        dqp, gq = _norm_backward(dqn, p_ref[0, :, 0:512], qnw_ref[...], bd512_ref[...])
        dkr = _fold_heads(dk_ref[0, 0], dk_ref[0, 1], lo_mask)
        dkn = dkr * cos_t + _partner(dkr * sin_t)
        dkp, gk = _norm_backward(dkn, p_ref[0, :, 512:640], knw_ref[...], bd128_ref[...])
        dvp = _fold_heads(dv_ref[0, 0], dv_ref[0, 1], lo_mask)
        d_ref[0] = jnp.concatenate([dqp, dkp, dvp], axis=1).astype(BF16)
        gk512 = jnp.concatenate([jnp.sum(gk, axis=0, keepdims=True), jnp.zeros((1, 384), F32)], axis=1)
        rows = jnp.concatenate([jnp.sum(gq, axis=0, keepdims=True), gk512, jnp.zeros((6, 512), F32)], axis=0)
        first = jnp.logical_and(b == 0, i == 0)

        @pl.when(first)
        def _():
            gw_ref[...] = rows

        @pl.when(jnp.logical_not(first))
        def _():
            gw_ref[...] += rows

    const = lambda a: pl.BlockSpec(a.shape, lambda b, i: (0,) * a.ndim)
    kv_spec = pl.BlockSpec((1, 2, ts, 128), lambda b, i: (b, 0, i + row0 // ts, 0))
    return pl.pallas_call(
        body, name="qkv_backward", grid=(bsz, s // ts),
        in_specs=[pl.BlockSpec((1, ts, 768), lambda b, i: (b, i, 0)), pl.BlockSpec((1, ts, 512), lambda b, i: (b, i, 0)),
                  kv_spec, kv_spec, pl.BlockSpec((ts, 128), lambda b, i: (i, 0)),
                  pl.BlockSpec((ts, 128), lambda b, i: (i, 0)), const(qnw), const(knw), const(bd512), const(bd128)],
        out_specs=[pl.BlockSpec((1, ts, 768), lambda b, i: (b, i, 0)), pl.BlockSpec((8, 512), lambda b, i: (0, 0))],
        out_shape=[jax.ShapeDtypeStruct((bsz, s, 768), BF16), jax.ShapeDtypeStruct((8, 512), F32)],
        compiler_params=_params(2),
    )(qkv, dq, dk2, dv2, cos, sin, qnw, knw, bd512, bd128)


def _ctx_kv_backward(pc, dk2, dv2, knw, bd128):
    bsz, cl, _ = pc.shape

    def body(p_ref, dk_ref, dv_ref, knw_ref, bd128_ref, d_ref, gw_ref):
        b = pl.program_id(0)
        lo_mask = _lo_mask(cl)
        dkn = _fold_heads(dk_ref[0, 0], dk_ref[0, 1], lo_mask)
        dkp, gk = _norm_backward(dkn, p_ref[0, :, 0:128], knw_ref[...], bd128_ref[...])
        dvp = _fold_heads(dv_ref[0, 0], dv_ref[0, 1], lo_mask)
        d_ref[0] = jnp.concatenate([dkp, dvp], axis=1).astype(BF16)
        rows = jnp.concatenate([jnp.sum(gk, axis=0, keepdims=True), jnp.zeros((7, 128), F32)], axis=0)

        @pl.when(b == 0)
        def _():
            gw_ref[...] = rows

        @pl.when(b > 0)
        def _():
            gw_ref[...] += rows

    const = lambda a: pl.BlockSpec(a.shape, lambda b: (0,) * a.ndim)
    kv_spec = pl.BlockSpec((1, 2, cl, 128), lambda b: (b, 0, 0, 0))
    return pl.pallas_call(
        body, name="ctx_kv_backward", grid=(bsz,),
        in_specs=[pl.BlockSpec((1, cl, 256), lambda b: (b, 0, 0)), kv_spec, kv_spec, const(knw), const(bd128)],
        out_specs=[pl.BlockSpec((1, cl, 256), lambda b: (b, 0, 0)), pl.BlockSpec((8, 128), lambda b: (0, 0))],
        out_shape=[jax.ShapeDtypeStruct((bsz, cl, 256), BF16), jax.ShapeDtypeStruct((8, 128), F32)],
        compiler_params=_params(1),
    )(pc, dk2, dv2, knw, bd128)


def _inproj_backward(dps, u, x, dh, scale1p, norm_w, w_t, gw_init, tm, name):
    bsz, s, d = x.shape
    n_p = len(dps)
    nrows = w_t.shape[0]
    shared = scale1p.shape[0] == 1
    with_dx = dh is not None

    def body(*refs):
        dp_refs = refs[:n_p]
        u_ref, x_ref = refs[n_p], refs[n_p + 1]
        k = n_p + 2
        if with_dx:
            dh_ref = refs[k]
            k += 1
        sc_ref, nw_ref, w_ref = refs[k], refs[k + 1], refs[k + 2]
        k += 3
        if gw_init is not None:
            gi_ref = refs[k]
            k += 1
        outs = refs[k:]
        if with_dx:
            gx_ref, gw_ref, mod_ref, gnw_ref = outs
        else:
            gw_ref, mod_ref, gnw_ref = outs
        b, i = pl.program_id(0), pl.program_id(1)
        first = jnp.logical_and(b == 0, i == 0)
        dp = dp_refs[0][0] if n_p == 1 else jnp.concatenate([r[0] for r in dp_refs], axis=1)
        du = jnp.dot(dp, w_ref[...], preferred_element_type=F32)
        gw = lax.dot_general(dp, u_ref[0], TN, preferred_element_type=F32)

        @pl.when(first)
        def _():
            gw_ref[...] = gw
            if gw_init is not None:
                gw_ref[KV_LO:KV_HI, :] += gi_ref[...]

        @pl.when(jnp.logical_not(first))
        def _():
            gw_ref[...] += gw

        xv = x_ref[0]
        rstd = lax.rsqrt(jnp.mean(xv * xv, axis=-1, keepdims=True) + EPS)
        xhat = xv * rstd
        nw, sc = nw_ref[...], sc_ref[0]
        red = lambda v: jnp.sum(v, axis=0, keepdims=True)
        mod_rows = jnp.concatenate([red(du), red(du * (xhat * nw)), jnp.zeros((6, d), F32)], axis=0)
        gnw_rows = jnp.concatenate([red(du * sc * xhat), jnp.zeros((7, d), F32)], axis=0)
        mod_first = first if shared else i == 0

        @pl.when(mod_first)
        def _():
            mod_ref[0] = mod_rows

        @pl.when(jnp.logical_not(mod_first))
        def _():
            mod_ref[0] += mod_rows

        @pl.when(first)
        def _():
            gnw_ref[...] = gnw_rows

        @pl.when(jnp.logical_not(first))
        def _():
            gnw_ref[...] += gnw_rows

        if with_dx:
            dxhat = du * (nw * sc)
            gx_ref[0] = dh_ref[0] + rstd * (dxhat - xhat * jnp.mean(dxhat * xhat, axis=-1, keepdims=True))

    tok = lambda w: pl.BlockSpec((1, tm, w), lambda b, i: (b, i, 0))
    in_specs = [tok(p.shape[2]) for p in dps] + [tok(d), tok(d)]
    args = list(dps) + [u, x]
    if with_dx:
        in_specs.append(tok(d))
        args.append(dh)
    in_specs += [_bcast_spec(scale1p), pl.BlockSpec((1, d), lambda b, i: (0, 0)),
                 pl.BlockSpec(w_t.shape, lambda b, i: (0, 0))]
    args += [scale1p, norm_w, w_t]
    if gw_init is not None:
        in_specs.append(pl.BlockSpec(gw_init.shape, lambda b, i: (0, 0)))
        args.append(gw_init)
    bm = scale1p.shape[0]
    mod_spec = pl.BlockSpec((1, 8, d), (lambda b, i: (0, 0, 0)) if shared else (lambda b, i: (b, 0, 0)))
    out_specs = [pl.BlockSpec((nrows, d), lambda b, i: (0, 0)), mod_spec, pl.BlockSpec((8, d), lambda b, i: (0, 0))]
    out_shape = [jax.ShapeDtypeStruct((nrows, d), F32), jax.ShapeDtypeStruct((bm, 8, d), F32),
                 jax.ShapeDtypeStruct((8, d), F32)]
    if with_dx:
        out_specs.insert(0, tok(d))
        out_shape.insert(0, jax.ShapeDtypeStruct((bsz, s, d), F32))
    res = pl.pallas_call(
        body, name=name, grid=(bsz, s // tm), in_specs=in_specs, out_specs=out_specs, out_shape=out_shape,
        compiler_params=_params(2),
    )(*args)
    return res if with_dx else [None] + list(res)


def _adamw(w, g, m, v, name):
    r, cdim = w.shape
    tr = 256 if r % 256 == 0 and r > 256 else r

    def body(w_ref, g_ref, m_ref, v_ref, d_ref, nm_ref, nv_ref):
        gv = g_ref[...]
        mn = ADAM_B1 * m_ref[...] + (1.0 - ADAM_B1) * gv
        vn = ADAM_B2 * v_ref[...] + (1.0 - ADAM_B2) * (gv * gv)
        m_hat = mn / (1.0 - ADAM_B1 ** ADAM_STEP)
        v_hat = vn / (1.0 - ADAM_B2 ** ADAM_STEP)
        d_ref[...] = -ADAM_LR * (m_hat / (jnp.sqrt(v_hat) + ADAM_EPS) + ADAM_WD * w_ref[...])
        nm_ref[...] = mn
        nv_ref[...] = vn

    spec = pl.BlockSpec((tr, cdim), lambda i: (i, 0))
    return pl.pallas_call(
        body, name=name, grid=(r // tr,), in_specs=[spec] * 4, out_specs=[spec] * 3,
        out_shape=[jax.ShapeDtypeStruct((r, cdim), F32)] * 3, compiler_params=_params(1),
    )(w, g, m, v)


def _rope_tables(s):
    rows = s // GRID_W
    freqs = np.float32(ROPE_THETA) ** (-np.arange(0, ROPE_AXIS_DIM, 2, dtype=np.float32) / np.float32(ROPE_AXIS_DIM))
    ang_r = np.arange(rows, dtype=np.float32)[:, None] * freqs[None, :]
    ang_c = np.arange(GRID_W, dtype=np.float32)[:, None] * freqs[None, :]
    rep = lambda t: jnp.repeat(jnp.asarray(t, dtype=F32), GRID_W, axis=0)
    til = lambda t: jnp.tile(jnp.asarray(t, dtype=F32), (rows, 1))
    cr, sr, cc, sc = rep(np.cos(ang_r)), rep(np.sin(ang_r)), til(np.cos(ang_c)), til(np.sin(ang_c))
    cos64 = jnp.concatenate([cr, cr, cc, cc], axis=1)
    sin64 = jnp.concatenate([-sr, sr, -sc, sc], axis=1)
    return jnp.concatenate([cos64, cos64], axis=1), jnp.concatenate([sin64, sin64], axis=1)


def _pack_rows(parts, rows):
    flat = jnp.concatenate([p.reshape(-1) for p in parts])
    return jnp.pad(flat, (0, rows * D_MODEL - flat.shape[0])).reshape(rows, D_MODEL)


def kernel(x, c, ctx, c_ctx, w_mod, b_mod, norm_w, w_in, q_norm_w, k_norm_w, conv_w, conv_b, conv_ln_w, conv_ln_b, w_pw, b_pw, w_out, loss_target, m_c_ctx, m_w_mod, m_b_mod, m_norm_w, m_w_in, m_q_norm_w, m_k_norm_w, m_conv_w, m_conv_b, m_conv_ln_w, m_conv_ln_b, m_w_pw, m_b_pw, m_w_out, v_c_ctx, v_w_mod, v_b_mod, v_norm_w, v_w_in, v_q_norm_w, v_k_norm_w, v_conv_w, v_conv_b, v_conv_ln_w, v_conv_ln_b, v_w_pw, v_b_pw, v_w_out):
    bsz, s, d = x.shape
    cl = ctx.shape[1]
    xi, yi, ci = lax.axis_index("x"), lax.axis_index("y"), lax.axis_index("c")
    chip = 2 * xi + yi
    dev = 2 * chip + ci
    ncol_mod = w_mod.shape[2]

    w_in_t_loc = w_in[0].T.astype(BF16)
    conv_w_loc = jnp.pad(conv_w[0], ((0, 1), (0, 0)))
    g_in, g_out, g_pw, g_cw = _chip_gather(
        [w_in_t_loc, w_out[0].astype(BF16), w_pw[0].astype(BF16), conv_w_loc], "weight_gather")
    w_in_t = g_in.reshape(D_IN, d)
    w_out_f = g_out.reshape(d, d)
    w_pw_f = g_pw.reshape(D_CONV, D_CONV)
    conv_w_f = g_cw.transpose(1, 0, 2).reshape(32, D_CONV)

    c_all = _all_gather8(jnp.pad(c, ((0, 8 - bsz), (0, 0))), "c_gather").reshape(64, d)
    c_rows = jnp.concatenate([c_all, jnp.pad(c_ctx[None, :], ((0, 15), (0, 0)))], axis=0)
    b_cols = lax.dynamic_slice(b_mod, (0, chip * ncol_mod), (1, ncol_mod))
    sc_rows, mod_cols = _mod_forward(c_rows, w_mod[0], b_cols)
    mod_all = _chip_gather([mod_cols], "mod_gather")[0].transpose(1, 0, 2).reshape(80, 3 * d)
    mod_loc = lax.dynamic_slice(mod_all, (8 * dev, 0), (bsz, 3 * d))
    shift, scale1p, gate = mod_loc[:, None, :d], 1.0 + mod_loc[:, None, d:2 * d], mod_loc[:, None, 2 * d:]
    shift_c, scale1p_c = mod_all[64:65, :d][None], 1.0 + mod_all[64:65, d:2 * d][None]

    cos, sin = _rope_tables(s)
    qnw512 = jnp.tile(q_norm_w, (1, 8))
    knw128 = jnp.tile(k_norm_w, (1, 2))
    bd512 = jnp.kron(jnp.eye(8, dtype=F32), jnp.ones((HEAD_DIM, HEAD_DIM), F32)).astype(BF16)
    bd128 = bd512[:128, :128]

    u, p_qkv, p_za, p_ga, p_gg, p_zc = _norm_inproj(x, shift, scale1p, norm_w, w_in_t, SPLITS, 512, "norm_inproj")
    uc, pc_kv = _norm_inproj(ctx, shift_c, scale1p_c, norm_w, w_in_t[KV_LO:KV_HI], ((0, 256),), cl, "ctx_norm_inproj")
    q, k2x, v2x = _qkv_prep(p_qkv, cos, sin, qnw512, knw128, bd512, bd128, 256, cl)
    k2, v2 = _ctx_kv_prep(pc_kv, knw128, bd128, k2x, v2x)
    o, lse = _attn_forward(q, k2, v2, 256)
    y, cv = _conv_forward(p_ga, p_gg, conv_w_f, conv_b, conv_ln_w, conv_ln_b, w_pw_f, b_pw, 256)
    loss_part, dh, do, dza, dcv, dzc, dgate, gw_out = _outproj_loss(
        x, loss_target, gate, o, p_za, cv, p_zc, w_out_f, 256)

    dy, gw_pw, conv_stats = _conv_token_backward(dcv, y, conv_ln_w, conv_ln_b, w_pw_f, 256)
    da, dg, gcw = _conv_backward(dy, p_ga, p_gg, conv_w_f, 256)
    dq, dk2, dv2 = _attn_backward(q, k2, v2, o, do, lse, 256)
    dqkv, qk_stats = _qkv_backward(p_qkv, dq, dk2, dv2, cos, sin, qnw512, knw128, bd512, bd128, 256, cl)
    dpc, kc_stats = _ctx_kv_backward(pc_kv, dk2, dv2, knw128, bd128)
    _, gw_ctx, modc, gnw_c = _inproj_backward(
        [dpc], uc, ctx, None, scale1p_c, norm_w, w_in_t[KV_LO:KV_HI], None, cl, "ctx_inproj_backward")
    grad_x, gw_in_t, modx, gnw_x = _inproj_backward(
        [dqkv, dza, da, dg, dzc], u, x, dh, scale1p, norm_w, w_in_t, gw_ctx, 256, "inproj_backward")

    g_w_out, g_w_pw = _reduce_scatter([gw_out, gw_pw], "grad_out_pw_reduce")
    (g_w_in_t,) = _reduce_scatter([gw_in_t], "grad_in_reduce")
    dmod_loc = jnp.concatenate([modx[:, 0, :], modx[:, 1, :], dgate[:, 0, :]], axis=1)
    gq = qk_stats[0].reshape(8, HEAD_DIM).sum(axis=0)
    gk = (qk_stats[1, :128] + kc_stats[0]).reshape(2, HEAD_DIM).sum(axis=0)
    packed = _pack_rows([dmod_loc, dmod_loc.sum(axis=0), gnw_x[0] + gnw_c[0], modc[0, 0], modc[0, 1], gq, gk,
                         conv_stats[0], conv_stats[1], conv_stats[2], conv_stats[3], gcw,
                         jnp.sum(loss_part[:, 0, 0])[None]], 32)
    gathered, total = _all_gather8(packed, "small_grads", reduce=True)
    flat = total.reshape(-1)
    offs = [0]

    def take(nelem):
        lo = offs[0]
        offs[0] = lo + nelem
        return flat[lo:lo + nelem]

    take(bsz * 3 * d)
    g_b_mod_x = take(3 * d)
    g_norm_w = take(d)
    dshift_c, dscale_c = take(d), take(d)
    g_qnw, g_knw = take(HEAD_DIM), take(HEAD_DIM)
    g_b_pw, g_ln_w, g_ln_b, g_conv_b = take(D_CONV), take(D_CONV), take(D_CONV), take(D_CONV)
    g_conv_w_full = take(32 * D_CONV).reshape(32, D_CONV)
    loss = take(1)[0] * (0.5 / d)

    dmod_c = jnp.concatenate([dshift_c, dscale_c, jnp.zeros((d,), F32)])
    g_b_mod = (g_b_mod_x + dmod_c)[None, :]
    dmod_rows = jnp.pad(gathered[:, :6, :].reshape(8, bsz, 3 * d), ((0, 0), (0, 8 - bsz), (0, 0))).reshape(64, 3 * d)
    dmod_all = jnp.concatenate([dmod_rows, jnp.pad(dmod_c[None, :], ((0, 15), (0, 0)))], axis=0)
    dmod_cols = lax.dynamic_slice(dmod_all, (0, chip * ncol_mod), (80, ncol_mod))
    g_w_mod, gcc_part = _mod_backward(sc_rows, dmod_cols, w_mod[0])
    gcc_all = _all_gather8(gcc_part, "c_ctx_grad_gather")
    dsilu_ctx = gcc_all[0, 0] + gcc_all[2, 0] + gcc_all[4, 0] + gcc_all[6, 0]
    sg = _sigmoid(c_ctx)
    g_c_ctx = dsilu_ctx * (sg * (1.0 + c_ctx * (1.0 - sg)))

    g_w_in = g_w_in_t.T
    g_conv_w = lax.dynamic_slice(g_conv_w_full, (0, chip * 128), (CONV_WIDTH, 128))

    grads = {
        "c_ctx": g_c_ctx, "w_mod": g_w_mod[None], "b_mod": g_b_mod, "norm_w": g_norm_w[None], "w_in": g_w_in[None],
        "q_norm_w": g_qnw[None], "k_norm_w": g_knw[None], "conv_w": g_conv_w[None], "conv_b": g_conv_b[None],
        "conv_ln_w": g_ln_w[None], "conv_ln_b": g_ln_b[None], "w_pw": g_w_pw[None], "b_pw": g_b_pw[None],
        "w_out": g_w_out[None],
    }
    weights = {
        "c_ctx": (c_ctx, m_c_ctx, v_c_ctx), "w_mod": (w_mod, m_w_mod, v_w_mod), "b_mod": (b_mod, m_b_mod, v_b_mod),
        "norm_w": (norm_w, m_norm_w, v_norm_w), "w_in": (w_in, m_w_in, v_w_in),
        "q_norm_w": (q_norm_w, m_q_norm_w, v_q_norm_w), "k_norm_w": (k_norm_w, m_k_norm_w, v_k_norm_w),
        "conv_w": (conv_w, m_conv_w, v_conv_w), "conv_b": (conv_b, m_conv_b, v_conv_b),
        "conv_ln_w": (conv_ln_w, m_conv_ln_w, v_conv_ln_w), "conv_ln_b": (conv_ln_b, m_conv_ln_b, v_conv_ln_b),
        "w_pw": (w_pw, m_w_pw, v_w_pw), "b_pw": (b_pw, m_b_pw, v_b_pw), "w_out": (w_out, m_w_out, v_w_out),
    }
    names = list(weights)
    deltas, new_ms, new_vs = [], [], []
    for n in names:
        w, m, v = weights[n]
        shape = w.shape
        two_d = (1, shape[0]) if w.ndim == 1 else (shape[-2] if w.ndim == 3 else 1, shape[-1])
        dl, nm, nv = _adamw(w.reshape(two_d), grads[n].reshape(two_d), m.reshape(two_d), v.reshape(two_d), "adamw_" + n)
        deltas.append(dl.reshape(shape))
        new_ms.append(nm.reshape(shape))
        new_vs.append(nv.reshape(shape))
        grads[n] = grads[n].reshape(shape)

    return (loss, grad_x, *[grads[n] for n in names], *deltas, *new_ms, *new_vs)
```

```python
import functools
import math

import jax
import jax.numpy as jnp
import numpy as np
from jax import lax
from jax.experimental import pallas as pl
from jax.experimental.pallas import tpu as pltpu

F32 = jnp.float32
BF16 = jnp.bfloat16
MESH = pl.DeviceIdType.MESH

D_MODEL = 1024
D_ATTN = 512
D_CONV = 512
HEAD_DIM = 64
N_KV = 2
GRID_W = 64
ROPE_AXIS_DIM = 32
ROPE_THETA = 10000.0
CONV_WIDTH = 31
CONV_PAD = 15
HALO = 16
EPS = 1e-6
SPLITS = ((0, 768), (768, 1280), (1280, 1792), (1792, 2304), (2304, 2816))
D_IN = 2816
KV_LO, KV_HI = 512, 768

ADAM_LR = 0.001
ADAM_B1 = 0.9
ADAM_B2 = 0.999
ADAM_EPS = 1e-08
ADAM_WD = 0.01
ADAM_STEP = 10

VMEM_LIMIT = 56 * 1024 * 1024

NT = (((1,), (1,)), ((), ()))
TN = (((0,), (0,)), ((), ()))


def _params(n_axes=0, **kw):
    if n_axes:
        kw["dimension_semantics"] = ("arbitrary",) * n_axes
    return pltpu.CompilerParams(vmem_limit_bytes=VMEM_LIMIT, **kw)


def _sigmoid(x):
    return 1.0 / (1.0 + jnp.exp(-x))


def _silu_and_grad(z):
    s = _sigmoid(z)
    return z * s, s * (1.0 + z * (1.0 - s))


def _seg_mean(v, ones_bd):
    hi = v.astype(BF16)
    lo = (v - hi.astype(F32)).astype(BF16)
    s = jnp.dot(hi, ones_bd, preferred_element_type=F32) + jnp.dot(lo, ones_bd, preferred_element_type=F32)
    return s * (1.0 / HEAD_DIM)


def _partner(v):
    n = v.shape[1]
    lane = lax.broadcasted_iota(jnp.int32, (v.shape[0], 128), 1)
    first = (lane % 32) < 16
    parts = []
    for k in range(n // 128):
        ch = v[:, 128 * k:128 * (k + 1)]
        parts.append(jnp.where(first, pltpu.roll(ch, 112, 1), pltpu.roll(ch, 16, 1)))
    return parts[0] if len(parts) == 1 else jnp.concatenate(parts, axis=1)


def _tile_lanes(t, reps):
    return t if reps == 1 else jnp.concatenate([t] * reps, axis=1)


def _lo_mask(rows):
    return lax.broadcasted_iota(jnp.int32, (rows, 128), 1) < HEAD_DIM


def _gather8_in_vmem(x_ref, out_ref, send_sems, recv_sems, local_sem):
    x, y, c = lax.axis_index("x"), lax.axis_index("y"), lax.axis_index("c")
    me, sibling = (x, y, c), (x, y, 1 - c)
    chips = [(1 - x, y), (x, 1 - y), (1 - x, 1 - y)]

    def slot(px, py, pc):
        return out_ref.at[4 * px + 2 * py + pc]

    def copy(k, block, to, src=None):
        return pltpu.make_async_remote_copy(
            src_ref=slot(*block) if src is None else src, dst_ref=slot(*block),
            send_sem=send_sems.at[k], recv_sem=recv_sems.at[k], device_id=to, device_id_type=MESH)

    mine = pltpu.make_async_copy(x_ref, slot(*me), local_sem)
    mine.start()
    first = [copy(0, me, sibling, src=x_ref)]
    first += [copy(1 + j, me, (*chip, c), src=x_ref) for j, chip in enumerate(chips)]
    for cp in first:
        cp.start()
    passed = [copy(4 + j, (*chip, c), sibling) for j, chip in enumerate(chips)]
    for j, chip in enumerate(chips):
        copy(1 + j, (*chip, c), me).wait_recv()
        passed[j].start()
    copy(0, sibling, me).wait_recv()
    for j, chip in enumerate(chips):
        copy(4 + j, (*chip, 1 - c), me).wait_recv()
    for cp in first + passed:
        cp.wait_send()
    mine.wait()


def _all_gather8(x_local, name, reduce=False):
    r, cdim = x_local.shape

    def body(x_ref, out_ref, *rest):
        if reduce:
            sum_ref, send_sems, recv_sems, local_sem = rest
        else:
            send_sems, recv_sems, local_sem = rest
        _gather8_in_vmem(x_ref, out_ref, send_sems, recv_sems, local_sem)
        if reduce:
            acc = out_ref[0]
            for d in range(1, 8):
                acc = acc + out_ref[d]
            sum_ref[...] = acc

    out_shape = [jax.ShapeDtypeStruct((8, r, cdim), x_local.dtype)]
    out_specs = [pl.BlockSpec(memory_space=pltpu.VMEM)]
    if reduce:
        out_shape.append(jax.ShapeDtypeStruct((r, cdim), x_local.dtype))
        out_specs.append(pl.BlockSpec(memory_space=pltpu.VMEM))
    res = pl.pallas_call(
        body, name=name, out_shape=out_shape,
        in_specs=[pl.BlockSpec(memory_space=pltpu.VMEM)], out_specs=out_specs,
        scratch_shapes=[pltpu.SemaphoreType.DMA((7,)), pltpu.SemaphoreType.DMA((7,)), pltpu.SemaphoreType.DMA],
        compiler_params=_params(),
    )(x_local)
    return res if reduce else res[0]


class _ChipGather:
    def __init__(self, arrs):
        self.arrs = list(arrs)
        n = self.n = len(self.arrs)
        self.in_specs = [pl.BlockSpec(memory_space=pl.ANY)] * n
        self.out_shape = [jax.ShapeDtypeStruct((4,) + a.shape, a.dtype) for a in self.arrs]
        self.out_specs = [pl.BlockSpec(memory_space=pl.ANY)] * n
        self.scratch = [pltpu.SemaphoreType.DMA((6 * n,)), pltpu.SemaphoreType.DMA((6 * n,)),
                        pltpu.SemaphoreType.DMA((n,))]
        self.phases = [self.start, self.forward, self.finish]

    def bind(self, ins, outs, scratch):
        self.ins, self.outs = ins, outs
        self.send_sems, self.recv_sems, self.local_sems = scratch
        self.x, self.y, self.c = lax.axis_index("x"), lax.axis_index("y"), lax.axis_index("c")
        self.chips = [(1 - self.x, self.y), (self.x, 1 - self.y), (1 - self.x, 1 - self.y)]
        self.mychip = 2 * self.x + self.y

    def _copy(self, a, k, chip_idx, cc, to, src=None):
        h = self.arrs[a].shape[0] // 2
        dst = self.outs[a].at[chip_idx, pl.ds(cc * h, h)]
        return pltpu.make_async_remote_copy(
            src_ref=dst if src is None else src, dst_ref=dst, send_sem=self.send_sems.at[6 * a + k],
            recv_sem=self.recv_sems.at[6 * a + k], device_id=to, device_id_type=MESH)

    def _local(self, a):
        return pltpu.make_async_copy(self.ins[a], self.outs[a].at[self.mychip], self.local_sems.at[a])

    def _first(self, a, j):
        h = self.arrs[a].shape[0] // 2
        return self._copy(a, j, self.mychip, self.c, (*self.chips[j], self.c), src=self.ins[a].at[pl.ds(self.c * h, h)])

    def _passed(self, a, j):
        cx, cy = self.chips[j]
        return self._copy(a, 3 + j, 2 * cx + cy, self.c, (self.x, self.y, 1 - self.c))

    def start(self):
        for a in range(self.n):
            self._local(a).start()
            for j in range(3):
                self._first(a, j).start()

    def forward(self):
        for a in range(self.n):
            for j, (cx, cy) in enumerate(self.chips):
                self._copy(a, j, 2 * cx + cy, self.c, (self.x, self.y, self.c)).wait_recv()
                self._passed(a, j).start()

    def finish(self):
        for a in range(self.n):
            for j, (cx, cy) in enumerate(self.chips):
                self._copy(a, 3 + j, 2 * cx + cy, 1 - self.c, (self.x, self.y, self.c)).wait_recv()
        for a in range(self.n):
            for j in range(3):
                self._first(a, j).wait_send()
                self._passed(a, j).wait_send()
            self._local(a).wait()


class _FusedReduce:
    def __init__(self, pieces):
        self.owners = [tuple(o) for _, o in pieces]
        self.arrs = [g.reshape(len(o), 2, g.shape[0] // (2 * len(o)), g.shape[1]) for g, o in pieces]
        n = self.n = len(pieces)
        hc = self.hc = [(v.shape[2], v.shape[3]) for v in self.arrs]
        nts = [len(o) for o in self.owners]
        self.base = [sum(nts[:p]) for p in range(n)]
        anyspec = pl.BlockSpec(memory_space=pl.ANY)
        self.in_specs = [anyspec] * n
        self.out_shape = [jax.ShapeDtypeStruct((nt,) + s, F32) for nt, s in zip(nts, hc)]
        self.out_shape += [jax.ShapeDtypeStruct((2,) + s, F32) for s in hc]
        self.out_specs = [anyspec] * (2 * n)
        self.scratch = [pltpu.VMEM(s, F32) for s in hc] * 2
        self.scratch += [pltpu.VMEM(s, F32) for s in hc]
        self.scratch += [pltpu.VMEM((nt,) + s, BF16) for nt, s in zip(nts, hc)]
        self.scratch += [pltpu.VMEM((3,) + s, BF16) for s in hc]
        self.scratch += [pltpu.VMEM(s, F32) for s in hc]
        tot = sum(nts)
        self.scratch += [pltpu.SemaphoreType.DMA((tot,)), pltpu.SemaphoreType.DMA((tot,)),
                         pltpu.SemaphoreType.DMA((tot,)), pltpu.SemaphoreType.DMA((3 * n,)),
                         pltpu.SemaphoreType.DMA((n,)), pltpu.SemaphoreType.DMA((n,)), pltpu.SemaphoreType.DMA((n,))]
        self.phases = [self.start, self.exchange, self.combine, self.finish]

    def bind(self, ins, outs, scratch):
        n = self.n
        self.g, self.recv_a, self.out = ins, outs[:n], outs[n:]
        self.va, self.vb, self.own = scratch[:n], scratch[n:2 * n], scratch[2 * n:3 * n]
        self.tsend, self.recv_b, self.fin = scratch[3 * n:4 * n], scratch[4 * n:5 * n], scratch[5 * n:6 * n]
        self.sa, self.ra, self.sb, self.rb, self.sc, self.rc, self.lc = scratch[6 * n:]
        self.x, self.y, self.c = lax.axis_index("x"), lax.axis_index("y"), lax.axis_index("c")
        self.mychip = 2 * self.x + self.y
        self.sibling = (self.x, self.y, 1 - self.c)

    def _copy_a(self, p, t):
        k = self.base[p] + t
        return pltpu.make_async_remote_copy(
            src_ref=self.g[p].at[t, 1 - self.c], dst_ref=self.recv_a[p].at[t], send_sem=self.sa.at[k],
            recv_sem=self.ra.at[k], device_id=self.sibling, device_id_type=MESH)

    def _slot(self, owner):
        rel = jnp.bitwise_xor(self.mychip, owner)
        return jnp.where(rel == 2, 0, jnp.where(rel == 1, 1, 2))

    def _copy_b(self, p, t, slot):
        owner = self.owners[p][t]
        return pltpu.make_async_remote_copy(
            src_ref=self.tsend[p].at[t], dst_ref=self.recv_b[p].at[slot], send_sem=self.sb.at[self.base[p] + t],
            recv_sem=self.rb.at[3 * p + slot], device_id=(owner // 2, owner % 2, self.c), device_id_type=MESH)

    def _copy_c(self, p, half):
        return pltpu.make_async_remote_copy(
            src_ref=self.fin[p], dst_ref=self.out[p].at[half], send_sem=self.sc.at[p], recv_sem=self.rc.at[p],
            device_id=self.sibling, device_id_type=MESH)

    def _local_c(self, p):
        return pltpu.make_async_copy(self.fin[p], self.out[p].at[self.c], self.lc.at[p])

    def start(self):
        for p in range(self.n):
            for t in range(len(self.owners[p])):
                self._copy_a(p, t).start()

    def exchange(self):
        for p in range(self.n):
            for t, owner in enumerate(self.owners[p]):
                self._copy_a(p, t).wait_recv()
                pltpu.sync_copy(self.g[p].at[t, self.c], self.va[p])
                pltpu.sync_copy(self.recv_a[p].at[t], self.vb[p])
                mine = self.mychip == owner

                @pl.when(mine)
                def _():
                    self.own[p][...] = self.va[p][...] + self.vb[p][...]

                @pl.when(jnp.logical_not(mine))
                def _():
                    self.tsend[p][t] = (self.va[p][...] + self.vb[p][...]).astype(BF16)
                    self._copy_b(p, t, self._slot(owner)).start()

    def combine(self):
        for p in range(self.n):
            for t, owner in enumerate(self.owners[p]):
                @pl.when(self.mychip == owner)
                def _():
                    acc = self.own[p][...]
                    for j in range(3):
                        self._copy_b(p, t, j).wait_recv()
                        acc = acc + self.recv_b[p][j].astype(F32)
                    self.fin[p][...] = acc
                    self._local_c(p).start()
                    self._copy_c(p, self.c).start()

    def finish(self):
        for p in range(self.n):
            for t, owner in enumerate(self.owners[p]):
                self._copy_a(p, t).wait_send()
                mine = self.mychip == owner

                @pl.when(mine)
                def _():
                    self._copy_c(p, 1 - self.c).wait_recv()
                    self._copy_c(p, self.c).wait_send()
                    self._local_c(p).wait()

                @pl.when(jnp.logical_not(mine))
                def _():
                    self._copy_b(p, t, self._slot(owner)).wait_send()


def _split_fused(refs, n_in, n_out, n_scr, fused):
    if fused is None:
        return refs[:n_in], refs[n_in:n_in + n_out], refs[n_in + n_out:]
    fi, fo = len(fused.in_specs), len(fused.out_specs)
    ins, rest = refs[:n_in], refs[n_in:]
    f_ins, rest = rest[:fi], rest[fi:]
    outs, rest = rest[:n_out], rest[n_out:]
    f_outs, rest = rest[:fo], rest[fo:]
    scr, f_scr = rest[:n_scr], rest[n_scr:]
    fused.bind(f_ins, f_outs, f_scr)
    return ins, outs, scr


def _run_phases(fused, step, at_steps, before):
    if fused is None:
        return
    for phase, (at, first) in zip(fused.phases, at_steps):
        if first == before:
            pl.when(step == at)(phase)


def _front(c_pad, c_ctx_rows, w_mod, b_cols, w_in_t_loc):
    ncol = w_mod.shape[1]
    gather = _ChipGather([w_in_t_loc])

    def body(c_ref, cctx_ref, w_ref, b_ref, win_ref, sc_ref, modg_ref, wing_ref,
             call_ref, ag_send, ag_recv, ag_local, m_send, m_recv, *g_scr):
        gather.bind([win_ref], [wing_ref], g_scr)
        gather.start()
        _gather8_in_vmem(c_ref, call_ref, ag_send, ag_recv, ag_local)
        x, y, c = lax.axis_index("x"), lax.axis_index("y"), lax.axis_index("c")
        chips = [(1 - x, y), (x, 1 - y), (1 - x, 1 - y)]
        mychip = 2 * x + y
        rows = jnp.concatenate([call_ref[dv] for dv in range(8)] + [cctx_ref[...]], axis=0)
        sc = rows * _sigmoid(rows)
        sc_ref[...] = sc
        modg_ref[mychip] = jnp.dot(sc, w_ref[...], preferred_element_type=F32,
                                   precision=lax.Precision.HIGHEST) + b_ref[...]

        def mcopy(j, chip_idx, to):
            return pltpu.make_async_remote_copy(
                src_ref=modg_ref.at[chip_idx], dst_ref=modg_ref.at[chip_idx], send_sem=m_send.at[j],
                recv_sem=m_recv.at[j], device_id=to, device_id_type=MESH)

        sends = [mcopy(j, mychip, (*chip, c)) for j, chip in enumerate(chips)]
        for cp in sends:
            cp.start()
        for j, (cx, cy) in enumerate(chips):
            mcopy(j, 2 * cx + cy, (x, y, c)).wait_recv()
        gather.forward()
        gather.finish()
        for cp in sends:
            cp.wait_send()

    vm = pl.BlockSpec(memory_space=pltpu.VMEM)
    return pl.pallas_call(
        body, name="front_exchange",
        out_shape=[jax.ShapeDtypeStruct((80, D_MODEL), F32), jax.ShapeDtypeStruct((4, 80, ncol), F32)] + gather.out_shape,
        in_specs=[vm, vm, vm, vm] + gather.in_specs, out_specs=[vm, vm] + gather.out_specs,
        scratch_shapes=[pltpu.VMEM((8, 8, D_MODEL), F32), pltpu.SemaphoreType.DMA((7,)), pltpu.SemaphoreType.DMA((7,)),
                        pltpu.SemaphoreType.DMA, pltpu.SemaphoreType.DMA((3,)), pltpu.SemaphoreType.DMA((3,))]
        + gather.scratch,
        compiler_params=_params(),
    )(c_pad, c_ctx_rows, w_mod, b_cols, w_in_t_loc)


def _mod_backward(sc_rows, dmod_cols, w_mod):
    rows, ncol = dmod_cols.shape

    def body(sc_ref, dm_ref, w_ref, gw_ref, gcc_ref):
        gw_ref[...] = lax.dot_general(sc_ref[...], dm_ref[...], TN, preferred_element_type=F32,
                                      precision=lax.Precision.HIGHEST)
        gcc_ref[...] = lax.dot_general(dm_ref[64:72, :], w_ref[...], NT, preferred_element_type=F32,
                                       precision=lax.Precision.HIGHEST)

    return pl.pallas_call(
        body, name="mod_backward",
        out_shape=[jax.ShapeDtypeStruct((D_MODEL, ncol), F32), jax.ShapeDtypeStruct((8, D_MODEL), F32)],
        compiler_params=_params(),
    )(sc_rows, dmod_cols, w_mod)


def _bcast_spec(arr):
    if arr.shape[0] == 1:
        return pl.BlockSpec((1, 1, arr.shape[2]), lambda b, i: (0, 0, 0))
    return pl.BlockSpec((1, 1, arr.shape[2]), lambda b, i: (b, 0, 0))


def _norm_inproj(x, shift, scale1p, norm_w, w_t, splits, tm, name):
    bsz, s, d = x.shape

    def body(x_ref, sh_ref, sc_ref, nw_ref, w_ref, u_ref, *out_refs):
        xv = x_ref[0]
        rstd = lax.rsqrt(jnp.mean(xv * xv, axis=-1, keepdims=True) + EPS)
        u = (xv * rstd * nw_ref[...]) * sc_ref[0] + sh_ref[0]
        ub = u.astype(BF16)
        u_ref[0] = ub
        for (lo, hi), o_ref in zip(splits, out_refs):
            o_ref[0] = lax.dot_general(ub, w_ref[lo:hi, :], NT, preferred_element_type=F32)

    tok = lambda w: pl.BlockSpec((1, tm, w), lambda b, i: (b, i, 0))
    return pl.pallas_call(
        body, name=name, grid=(bsz, s // tm),
        in_specs=[tok(d), _bcast_spec(shift), _bcast_spec(scale1p), pl.BlockSpec((1, d), lambda b, i: (0, 0)),
                  pl.BlockSpec(w_t.shape, lambda b, i: (0, 0))],
        out_specs=[tok(d)] + [tok(hi - lo) for lo, hi in splits],
        out_shape=[jax.ShapeDtypeStruct((bsz, s, d), BF16)]
        + [jax.ShapeDtypeStruct((bsz, s, hi - lo), F32) for lo, hi in splits],
        compiler_params=_params(2),
    )(x, shift, scale1p, norm_w, w_t)


def _dup_heads(kv, lo_mask):
    r = pltpu.roll(kv, HEAD_DIM, 1)
    return jnp.where(lo_mask, kv, r), jnp.where(lo_mask, r, kv)


def _qkv_prep(qkv, cos, sin, qnw, knw, bd512, bd128, ts, row0):
    bsz, s, _ = qkv.shape

    def body(p_ref, cos_ref, sin_ref, qnw_ref, knw_ref, bd512_ref, bd128_ref, q_ref, k_ref, v_ref):
        lo_mask = _lo_mask(ts)
        cos_t, sin_t = cos_ref[...], sin_ref[...]
        qp = p_ref[0, :, 0:512]
        qn = qp * lax.rsqrt(_seg_mean(qp * qp, bd512_ref[...]) + EPS) * qnw_ref[...]
        qr = qn * _tile_lanes(cos_t, 4) + _partner(qn) * _tile_lanes(sin_t, 4)
        q_ref[0] = (qr * (1.0 / math.sqrt(HEAD_DIM))).astype(BF16)
        kp = p_ref[0, :, 512:640]
        kn = kp * lax.rsqrt(_seg_mean(kp * kp, bd128_ref[...]) + EPS) * knw_ref[...]
        kr = kn * cos_t + _partner(kn) * sin_t
        k0, k1 = _dup_heads(kr, lo_mask)
        k_ref[0, 0] = k0.astype(BF16)
        k_ref[0, 1] = k1.astype(BF16)
        v0, v1 = _dup_heads(p_ref[0, :, 640:768], lo_mask)
        v_ref[0, 0] = v0.astype(BF16)
        v_ref[0, 1] = v1.astype(BF16)

    const = lambda a: pl.BlockSpec(a.shape, lambda b, i: (0,) * a.ndim)
    kv_spec = pl.BlockSpec((1, 2, ts, 128), lambda b, i: (b, 0, i + row0 // ts, 0))
    return pl.pallas_call(
        body, name="qkv_prep", grid=(bsz, s // ts),
        in_specs=[pl.BlockSpec((1, ts, 768), lambda b, i: (b, i, 0)),
                  pl.BlockSpec((ts, 128), lambda b, i: (i, 0)), pl.BlockSpec((ts, 128), lambda b, i: (i, 0)),
                  const(qnw), const(knw), const(bd512), const(bd128)],
        out_specs=[pl.BlockSpec((1, ts, 512), lambda b, i: (b, i, 0)), kv_spec, kv_spec],
        out_shape=[jax.ShapeDtypeStruct((bsz, s, 512), BF16), jax.ShapeDtypeStruct((bsz, 2, row0 + s, 128), BF16),
                   jax.ShapeDtypeStruct((bsz, 2, row0 + s, 128), BF16)],
        compiler_params=_params(2),
    )(qkv, cos, sin, qnw, knw, bd512, bd128)


def _ctx_kv_prep(pc, knw, bd128, k2, v2):
    bsz, cl, _ = pc.shape

    def body(p_ref, knw_ref, bd128_ref, k_in, v_in, k_ref, v_ref):
        lo_mask = _lo_mask(cl)
        kp = p_ref[0, :, 0:128]
        kn = kp * lax.rsqrt(_seg_mean(kp * kp, bd128_ref[...]) + EPS) * knw_ref[...]
        k0, k1 = _dup_heads(kn, lo_mask)
        k_ref[0, 0] = k0.astype(BF16)
        k_ref[0, 1] = k1.astype(BF16)
        v0, v1 = _dup_heads(p_ref[0, :, 128:256], lo_mask)
        v_ref[0, 0] = v0.astype(BF16)
        v_ref[0, 1] = v1.astype(BF16)

    const = lambda a: pl.BlockSpec(a.shape, lambda b: (0,) * a.ndim)
    kv_spec = pl.BlockSpec((1, 2, cl, 128), lambda b: (b, 0, 0, 0))
    return pl.pallas_call(
        body, name="ctx_kv_prep", grid=(bsz,),
        in_specs=[pl.BlockSpec((1, cl, 256), lambda b: (b, 0, 0)), const(knw), const(bd128),
                  pl.BlockSpec(memory_space=pl.ANY), pl.BlockSpec(memory_space=pl.ANY)],
        out_specs=[kv_spec, kv_spec],
        out_shape=[jax.ShapeDtypeStruct(k2.shape, BF16), jax.ShapeDtypeStruct(v2.shape, BF16)],
        input_output_aliases={3: 0, 4: 1},
        compiler_params=_params(1),
    )(pc, knw, bd128, k2, v2)


def _attn_forward(q, k2, v2, tq, fused):
    bsz, s, _ = q.shape
    sk = k2.shape[2]
    nq = s // tq
    total = bsz * N_KV * nq
    at_steps = [(0, True), (total // 4, True), (total - 1, False)]

    def body(*refs):
        (q_ref, k_ref, v_ref), (o_ref, lse_ref), _ = _split_fused(refs, 3, 2, 0, fused)
        step = (pl.program_id(0) * N_KV + pl.program_id(1)) * nq + pl.program_id(2)
        _run_phases(fused, step, at_steps, True)
        kk, vv = k_ref[0, 0], v_ref[0, 0]
        lo_mask = _lo_mask(tq)
        for j in range(2):
            qp = q_ref[0, :, 128 * j:128 * (j + 1)]
            outs, lses = [], []
            for half in range(2):
                sel = lo_mask if half == 0 else jnp.logical_not(lo_mask)
                qs = jnp.where(sel, qp, jnp.zeros_like(qp))
                sc = lax.dot_general(qs, kk, NT, preferred_element_type=F32)
                m = jnp.max(sc, axis=-1, keepdims=True)
                p = jnp.exp(sc - m)
                l = jnp.sum(p, axis=-1, keepdims=True)
                o = jnp.dot(p.astype(BF16), vv, preferred_element_type=F32)
                outs.append(o / l)
                lses.append(jnp.broadcast_to(m + jnp.log(l), (tq, 128)))
            o_ref[0, :, 128 * j:128 * (j + 1)] = jnp.where(lo_mask, outs[0], outs[1])
            lse_ref[0, :, 128 * j:128 * (j + 1)] = jnp.where(lo_mask, lses[0], lses[1])
        _run_phases(fused, step, at_steps, False)

    q_spec = pl.BlockSpec((1, tq, 256), lambda b, g, i: (b, i, g))
    kv_spec = pl.BlockSpec((1, 1, sk, 128), lambda b, g, i: (b, g, 0, 0))
    return pl.pallas_call(
        body, name="attn_forward", grid=(bsz, N_KV, nq),
        in_specs=[q_spec, kv_spec, kv_spec] + fused.in_specs, out_specs=[q_spec, q_spec] + fused.out_specs,
        out_shape=[jax.ShapeDtypeStruct((bsz, s, 512), F32)] * 2 + fused.out_shape,
        scratch_shapes=fused.scratch,
        compiler_params=_params(3),
    )(q, k2, v2, *fused.arrs)


def _halo_specs(width, ts, s):
    r = ts // HALO
    last = s // HALO - 1
    return [pl.BlockSpec((1, ts, width), lambda b, i: (b, i, 0)),
            pl.BlockSpec((1, HALO, width), lambda b, i: (b, jnp.maximum(i * r - 1, 0), 0)),
            pl.BlockSpec((1, HALO, width), lambda b, i: (b, jnp.minimum((i + 1) * r, last), 0))]


def _fill_ext(ext_ref, cur, prev, nxt, i, n_tiles, ts):
    ext_ref[0:HALO, :] = jnp.where(i > 0, prev, jnp.zeros_like(prev))
    ext_ref[HALO:HALO + ts, :] = cur
    ext_ref[HALO + ts:2 * HALO + ts, :] = jnp.where(i < n_tiles - 1, nxt, jnp.zeros_like(nxt))


def _fill_shifted(sh_ref, ext_ref, ts):
    n = ts + 2 * HALO - 8
    for r in range(1, 8):
        sh_ref[r - 1, 0:n, :] = ext_ref[pl.ds(r, n), :]


def _window(sh_ref, ext_ref, off, ts):
    q, r = divmod(off, 8)
    if r == 0:
        return ext_ref[pl.ds(off, ts), :]
    return sh_ref[r - 1, pl.ds(8 * q, ts), :]


def _conv_forward(ga, gg, conv_w, conv_b, ln_w, ln_b, w_pw, b_pw, ts):
    bsz, s, dc = ga.shape
    n_tiles = s // ts

    def body(a_ref, ap_ref, an_ref, g_ref, gp_ref, gn_ref, cw_ref, cb_ref, lw_ref, lb_ref, wp_ref, bp_ref,
             y_ref, cv_ref, ext_ref, sh_ref):
        i = pl.program_id(1)
        glu = lambda a, g: a * _sigmoid(g)
        _fill_ext(ext_ref, glu(a_ref[0], g_ref[0]), glu(ap_ref[0], gp_ref[0]), glu(an_ref[0], gn_ref[0]), i, n_tiles, ts)
        _fill_shifted(sh_ref, ext_ref, ts)
        acc = jnp.broadcast_to(cb_ref[...], (ts, dc))
        for j in range(CONV_WIDTH):
            acc = acc + cw_ref[j:j + 1, :] * _window(sh_ref, ext_ref, HALO - CONV_PAD + j, ts)
        y_ref[0] = acc
        mu = jnp.mean(acc, axis=-1, keepdims=True)
        yc = acc - mu
        var = jnp.mean(yc * yc, axis=-1, keepdims=True)
        yn = yc * lax.rsqrt(var + EPS) * lw_ref[...] + lb_ref[...]
        ys = yn * _sigmoid(yn)
        cv_ref[0] = jnp.dot(ys.astype(BF16), wp_ref[...], preferred_element_type=F32) + bp_ref[...]

    const = lambda a: pl.BlockSpec(a.shape, lambda b, i: (0,) * a.ndim)
    return pl.pallas_call(
        body, name="conv_forward", grid=(bsz, n_tiles),
        in_specs=_halo_specs(dc, ts, s) + _halo_specs(dc, ts, s)
        + [const(conv_w), const(conv_b), const(ln_w), const(ln_b), const(w_pw), const(b_pw)],
        out_specs=[pl.BlockSpec((1, ts, dc), lambda b, i: (b, i, 0))] * 2,
        out_shape=[jax.ShapeDtypeStruct((bsz, s, dc), F32)] * 2,
        scratch_shapes=[pltpu.VMEM((ts + 2 * HALO, dc), F32), pltpu.VMEM((7, ts + 2 * HALO, dc), F32)],
        compiler_params=_params(2),
    )(ga, ga, ga, gg, gg, gg, conv_w, conv_b, ln_w, ln_b, w_pw, b_pw)


def _outproj_loss(x, target, gate, o, za, cv, zc, w_out, tm):
    bsz, s, d = x.shape

    def body(x_ref, t_ref, gate_ref, o_ref, za_ref, cv_ref, zc_ref, w_ref,
             loss_ref, dh_ref, do_ref, dza_ref, dcv_ref, dzc_ref, dgate_ref, gw_ref):
        b, i = pl.program_id(0), pl.program_id(1)
        ov, cvv = o_ref[0], cv_ref[0]
        silu_a, dsilu_a = _silu_and_grad(za_ref[0])
        silu_c, dsilu_c = _silu_and_grad(zc_ref[0])
        mix = jnp.concatenate([ov * silu_a, cvv * silu_c], axis=1).astype(BF16)
        out = jnp.dot(mix, w_ref[...], preferred_element_type=F32)
        gate_v = gate_ref[0]
        err = x_ref[0] + gate_v * out - t_ref[0]
        dh = err * (1.0 / d)
        dh_ref[0] = dh
        dout = (dh * gate_v).astype(BF16)
        dmix = lax.dot_general(dout, w_ref[...], NT, preferred_element_type=F32)
        gw = lax.dot_general(mix, dout, TN, preferred_element_type=F32)
        dg = jnp.sum(dh * out, axis=0, keepdims=True)
        sq = jnp.sum(err * err)

        @pl.when(jnp.logical_and(b == 0, i == 0))
        def _():
            gw_ref[...] = gw

        @pl.when(jnp.logical_or(b > 0, i > 0))
        def _():
            gw_ref[...] += gw

        @pl.when(i == 0)
        def _():
            dgate_ref[0] = dg
            loss_ref[...] = jnp.zeros(loss_ref.shape, F32) + sq

        @pl.when(i > 0)
        def _():
            dgate_ref[0] += dg
            loss_ref[...] += sq

        dma, dmc = dmix[:, :D_ATTN], dmix[:, D_ATTN:]
        do_ref[0] = dma * silu_a
        dza_ref[0] = (dma * ov * dsilu_a).astype(BF16)
        dcv_ref[0] = dmc * silu_c
        dzc_ref[0] = (dmc * cvv * dsilu_c).astype(BF16)

    tok = lambda w: pl.BlockSpec((1, tm, w), lambda b, i: (b, i, 0))
    return pl.pallas_call(
        body, name="outproj_loss", grid=(bsz, s // tm),
        in_specs=[tok(d), tok(d), _bcast_spec(gate), tok(512), tok(512), tok(512), tok(512),
                  pl.BlockSpec(w_out.shape, lambda b, i: (0, 0))],
        out_specs=[pl.BlockSpec((1, 8, 128), lambda b, i: (b, 0, 0)), tok(d), tok(512), tok(512), tok(512), tok(512),
                   pl.BlockSpec((1, 1, d), lambda b, i: (b, 0, 0)), pl.BlockSpec((d, d), lambda b, i: (0, 0))],
        out_shape=[jax.ShapeDtypeStruct((bsz, 8, 128), F32), jax.ShapeDtypeStruct((bsz, s, d), F32),
                   jax.ShapeDtypeStruct((bsz, s, 512), F32), jax.ShapeDtypeStruct((bsz, s, 512), BF16),
                   jax.ShapeDtypeStruct((bsz, s, 512), F32), jax.ShapeDtypeStruct((bsz, s, 512), BF16),
                   jax.ShapeDtypeStruct((bsz, 1, d), F32), jax.ShapeDtypeStruct((d, d), F32)],
        compiler_params=_params(2),
    )(x, target, gate, o, za, cv, zc, w_out)


def _conv_token_backward(dcv, y, ln_w, ln_b, w_pw, tm):
    bsz, s, dc = dcv.shape

    def body(dcv_ref, y_ref, lw_ref, lb_ref, wp_ref, dy_ref, gwp_ref, st_ref):
        b, i = pl.program_id(0), pl.program_id(1)
        yv, dcvv = y_ref[0], dcv_ref[0]
        mu = jnp.mean(yv, axis=-1, keepdims=True)
        yc = yv - mu
        rstd = lax.rsqrt(jnp.mean(yc * yc, axis=-1, keepdims=True) + EPS)
        yhat = yc * rstd
        yn = yhat * lw_ref[...] + lb_ref[...]
        ys, dsilu = _silu_and_grad(yn)
        dcvb = dcvv.astype(BF16)
        gwp = lax.dot_general(ys.astype(BF16), dcvb, TN, preferred_element_type=F32)
        dys = lax.dot_general(dcvb, wp_ref[...], NT, preferred_element_type=F32)
        dyn = dys * dsilu
        dyhat = dyn * lw_ref[...]
        dy = rstd * (dyhat - jnp.mean(dyhat, axis=-1, keepdims=True)
                     - yhat * jnp.mean(dyhat * yhat, axis=-1, keepdims=True))
        dy_ref[0] = dy
        red = lambda v: jnp.sum(v, axis=0, keepdims=True)
        stats = jnp.concatenate([red(dcvv), red(dyn * yhat), red(dyn), red(dy), jnp.zeros((4, dc), F32)], axis=0)
        first = jnp.logical_and(b == 0, i == 0)

        @pl.when(first)
        def _():
            gwp_ref[...] = gwp
            st_ref[...] = stats

        @pl.when(jnp.logical_not(first))
        def _():
            gwp_ref[...] += gwp
            st_ref[...] += stats

    tok = pl.BlockSpec((1, tm, dc), lambda b, i: (b, i, 0))
    const = lambda a: pl.BlockSpec(a.shape, lambda b, i: (0,) * a.ndim)
    return pl.pallas_call(
        body, name="conv_token_backward", grid=(bsz, s // tm),
        in_specs=[tok, tok, const(ln_w), const(ln_b), const(w_pw)],
        out_specs=[tok, pl.BlockSpec((dc, dc), lambda b, i: (0, 0)), pl.BlockSpec((8, dc), lambda b, i: (0, 0))],
        out_shape=[jax.ShapeDtypeStruct((bsz, s, dc), F32), jax.ShapeDtypeStruct((dc, dc), F32),
                   jax.ShapeDtypeStruct((8, dc), F32)],
        compiler_params=_params(2),
    )(dcv, y, ln_w, ln_b, w_pw)


def _conv_backward(dy, ga, gg, conv_w, ts):
    bsz, s, dc = dy.shape
    n_tiles = s // ts

    def body(dy_ref, dyp_ref, dyn_ref, a_ref, ap_ref, an_ref, g_ref, gp_ref, gn_ref, cw_ref,
             da_ref, dg_ref, gcw_ref, dyext_ref, ugext_ref, dysh_ref, ugsh_ref):
        b, i = pl.program_id(0), pl.program_id(1)
        dyv = dy_ref[0]
        av, sg = a_ref[0], _sigmoid(g_ref[0])
        glu = lambda a, g: a * _sigmoid(g)
        _fill_ext(dyext_ref, dyv, dyp_ref[0], dyn_ref[0], i, n_tiles, ts)
        _fill_ext(ugext_ref, av * sg, glu(ap_ref[0], gp_ref[0]), glu(an_ref[0], gn_ref[0]), i, n_tiles, ts)
        _fill_shifted(dysh_ref, dyext_ref, ts)
        _fill_shifted(ugsh_ref, ugext_ref, ts)
        dug = jnp.zeros((ts, dc), F32)
        rows = []
        for j in range(CONV_WIDTH):
            dug = dug + cw_ref[j:j + 1, :] * _window(dysh_ref, dyext_ref, HALO + CONV_PAD - j, ts)
            rows.append(jnp.sum(dyv * _window(ugsh_ref, ugext_ref, HALO - CONV_PAD + j, ts), axis=0, keepdims=True))
        rows.append(jnp.zeros((1, dc), F32))
        gcw = jnp.concatenate(rows, axis=0)
        first = jnp.logical_and(b == 0, i == 0)

        @pl.when(first)
        def _():
            gcw_ref[...] = gcw

        @pl.when(jnp.logical_not(first))
        def _():
            gcw_ref[...] += gcw

        da_ref[0] = (dug * sg).astype(BF16)
        dg_ref[0] = (dug * av * sg * (1.0 - sg)).astype(BF16)

    tok = pl.BlockSpec((1, ts, dc), lambda b, i: (b, i, 0))
    return pl.pallas_call(
        body, name="conv_backward", grid=(bsz, n_tiles),
        in_specs=_halo_specs(dc, ts, s) + _halo_specs(dc, ts, s) + _halo_specs(dc, ts, s)
        + [pl.BlockSpec(conv_w.shape, lambda b, i: (0, 0))],
        out_specs=[tok, tok, pl.BlockSpec((32, dc), lambda b, i: (0, 0))],
        out_shape=[jax.ShapeDtypeStruct((bsz, s, dc), BF16), jax.ShapeDtypeStruct((bsz, s, dc), BF16),
                   jax.ShapeDtypeStruct((32, dc), F32)],
        scratch_shapes=[pltpu.VMEM((ts + 2 * HALO, dc), F32), pltpu.VMEM((ts + 2 * HALO, dc), F32),
                        pltpu.VMEM((7, ts + 2 * HALO, dc), F32), pltpu.VMEM((7, ts + 2 * HALO, dc), F32)],
        compiler_params=_params(2),
    )(dy, dy, dy, ga, ga, ga, gg, gg, gg, conv_w)


def _attn_backward(q, k2, v2, o, do, lse, tq, fused):
    bsz, s, _ = q.shape
    sk = k2.shape[2]
    scale = 1.0 / math.sqrt(HEAD_DIM)
    nq = s // tq
    total = bsz * N_KV * nq
    at_steps = [(0, True), (min(3, total - 1), True), (total // 2, True), (total - 1, False)]

    def body(*refs):
        (q_ref, k_ref, v_ref, o_ref, do_ref, lse_ref), (dq_ref, dk_ref, dv_ref), _ = _split_fused(refs, 6, 3, 0, fused)
        i = pl.program_id(2)
        step = (pl.program_id(0) * N_KV + pl.program_id(1)) * nq + i
        _run_phases(fused, step, at_steps, True)
        kk, vv = k_ref[0, 0], v_ref[0, 0]
        lo_mask = _lo_mask(tq)
        dk_acc = jnp.zeros((sk, 128), F32)
        dv_acc = jnp.zeros((sk, 128), F32)
        for j in range(2):
            cols = slice(128 * j, 128 * (j + 1))
            qp, dop, lsep = q_ref[0, :, cols], do_ref[0, :, cols], lse_ref[0, :, cols]
            dprod = dop * o_ref[0, :, cols]
            dqs = []
            for half in range(2):
                sel = lo_mask if half == 0 else jnp.logical_not(lo_mask)
                qs = jnp.where(sel, qp, jnp.zeros_like(qp))
                dos = jnp.where(sel, dop, 0.0).astype(BF16)
                lse_h = jnp.max(jnp.where(sel, lsep, -jnp.inf), axis=-1, keepdims=True)
                delta = jnp.sum(jnp.where(sel, dprod, 0.0), axis=-1, keepdims=True)
                sc = lax.dot_general(qs, kk, NT, preferred_element_type=F32)
                p = jnp.exp(sc - lse_h)
                dp = lax.dot_general(dos, vv, NT, preferred_element_type=F32)
                ds = (p * (dp - delta)).astype(BF16)
                dv_acc = dv_acc + lax.dot_general(p.astype(BF16), dos, TN, preferred_element_type=F32)
                dk_acc = dk_acc + lax.dot_general(ds, qs, TN, preferred_element_type=F32)
                dqs.append(jnp.dot(ds, kk, preferred_element_type=F32))
            dq_ref[0, :, cols] = jnp.where(lo_mask, dqs[0], dqs[1]) * scale

        @pl.when(i == 0)
        def _():
            dk_ref[0, 0] = dk_acc
            dv_ref[0, 0] = dv_acc

        @pl.when(i > 0)
        def _():
            dk_ref[0, 0] += dk_acc
            dv_ref[0, 0] += dv_acc

        _run_phases(fused, step, at_steps, False)

    q_spec = pl.BlockSpec((1, tq, 256), lambda b, g, i: (b, i, g))
    kv_spec = pl.BlockSpec((1, 1, sk, 128), lambda b, g, i: (b, g, 0, 0))
    return pl.pallas_call(
        body, name="attn_backward", grid=(bsz, N_KV, nq),
        in_specs=[q_spec, kv_spec, kv_spec, q_spec, q_spec, q_spec] + fused.in_specs,
        out_specs=[q_spec, kv_spec, kv_spec] + fused.out_specs,
        out_shape=[jax.ShapeDtypeStruct((bsz, s, 512), F32), jax.ShapeDtypeStruct((bsz, 2, sk, 128), F32),
                   jax.ShapeDtypeStruct((bsz, 2, sk, 128), F32)] + fused.out_shape,
        scratch_shapes=fused.scratch,
        compiler_params=_params(3),
    )(q, k2, v2, o, do, lse, *fused.arrs)


def _fold_heads(acc2_ref_val0, acc2_ref_val1, lo_mask):
    f0 = acc2_ref_val0 + pltpu.roll(acc2_ref_val0, HEAD_DIM, 1)
    f1 = acc2_ref_val1 + pltpu.roll(acc2_ref_val1, HEAD_DIM, 1)
    return jnp.where(lo_mask, f0, f1)


def _norm_backward(dn, pre, w, bd):
    rstd = lax.rsqrt(_seg_mean(pre * pre, bd) + EPS)
    xhat = pre * rstd
    dxhat = dn * w
    return rstd * (dxhat - xhat * _seg_mean(dxhat * xhat, bd)), dn * xhat


def _qkv_backward(qkv, dq, dk2, dv2, cos, sin, qnw, knw, bd512, bd128, ts, row0):
    bsz, s, _ = qkv.shape

    def body(p_ref, dq_ref, dk_ref, dv_ref, cos_ref, sin_ref, qnw_ref, knw_ref, bd512_ref, bd128_ref, d_ref, gw_ref):
        b, i = pl.program_id(0), pl.program_id(1)
        lo_mask = _lo_mask(ts)
        cos_t, sin_t = cos_ref[...], sin_ref[...]
        dqr = dq_ref[0]
        dqn = dqr * _tile_lanes(cos_t, 4) + _partner(dqr * _tile_lanes(sin_t, 4))
        dqp, gq = _norm_backward(dqn, p_ref[0, :, 0:512], qnw_ref[...], bd512_ref[...])
        dkr = _fold_heads(dk_ref[0, 0], dk_ref[0, 1], lo_mask)
        dkn = dkr * cos_t + _partner(dkr * sin_t)
        dkp, gk = _norm_backward(dkn, p_ref[0, :, 512:640], knw_ref[...], bd128_ref[...])
        dvp = _fold_heads(dv_ref[0, 0], dv_ref[0, 1], lo_mask)
        d_ref[0] = jnp.concatenate([dqp, dkp, dvp], axis=1).astype(BF16)
        gk512 = jnp.concatenate([jnp.sum(gk, axis=0, keepdims=True), jnp.zeros((1, 384), F32)], axis=1)
        rows = jnp.concatenate([jnp.sum(gq, axis=0, keepdims=True), gk512, jnp.zeros((6, 512), F32)], axis=0)
        first = jnp.logical_and(b == 0, i == 0)

        @pl.when(first)
        def _():
            gw_ref[...] = rows

        @pl.when(jnp.logical_not(first))
        def _():
            gw_ref[...] += rows

    const = lambda a: pl.BlockSpec(a.shape, lambda b, i: (0,) * a.ndim)
    kv_spec = pl.BlockSpec((1, 2, ts, 128), lambda b, i: (b, 0, i + row0 // ts, 0))
    return pl.pallas_call(
        body, name="qkv_backward", grid=(bsz, s // ts),
        in_specs=[pl.BlockSpec((1, ts, 768), lambda b, i: (b, i, 0)), pl.BlockSpec((1, ts, 512), lambda b, i: (b, i, 0)),
                  kv_spec, kv_spec, pl.BlockSpec((ts, 128), lambda b, i: (i, 0)),
                  pl.BlockSpec((ts, 128), lambda b, i: (i, 0)), const(qnw), const(knw), const(bd512), const(bd128)],
        out_specs=[pl.BlockSpec((1, ts, 768), lambda b, i: (b, i, 0)), pl.BlockSpec((8, 512), lambda b, i: (0, 0))],
        out_shape=[jax.ShapeDtypeStruct((bsz, s, 768), BF16), jax.ShapeDtypeStruct((8, 512), F32)],
        compiler_params=_params(2),
    )(qkv, dq, dk2, dv2, cos, sin, qnw, knw, bd512, bd128)


def _ctx_kv_backward(pc, dk2, dv2, knw, bd128):
    bsz, cl, _ = pc.shape

    def body(p_ref, dk_ref, dv_ref, knw_ref, bd128_ref, d_ref, gw_ref):
        b = pl.program_id(0)
        lo_mask = _lo_mask(cl)
        dkn = _fold_heads(dk_ref[0, 0], dk_ref[0, 1], lo_mask)
        dkp, gk = _norm_backward(dkn, p_ref[0, :, 0:128], knw_ref[...], bd128_ref[...])
        dvp = _fold_heads(dv_ref[0, 0], dv_ref[0, 1], lo_mask)
        d_ref[0] = jnp.concatenate([dkp, dvp], axis=1).astype(BF16)
        rows = jnp.concatenate([jnp.sum(gk, axis=0, keepdims=True), jnp.zeros((7, 128), F32)], axis=0)

        @pl.when(b == 0)
        def _():
            gw_ref[...] = rows

        @pl.when(b > 0)
        def _():
            gw_ref[...] += rows

    const = lambda a: pl.BlockSpec(a.shape, lambda b: (0,) * a.ndim)
    kv_spec = pl.BlockSpec((1, 2, cl, 128), lambda b: (b, 0, 0, 0))
    return pl.pallas_call(
        body, name="ctx_kv_backward", grid=(bsz,),
        in_specs=[pl.BlockSpec((1, cl, 256), lambda b: (b, 0, 0)), kv_spec, kv_spec, const(knw), const(bd128)],
        out_specs=[pl.BlockSpec((1, cl, 256), lambda b: (b, 0, 0)), pl.BlockSpec((8, 128), lambda b: (0, 0))],
        out_shape=[jax.ShapeDtypeStruct((bsz, cl, 256), BF16), jax.ShapeDtypeStruct((8, 128), F32)],
        compiler_params=_params(1),
    )(pc, dk2, dv2, knw, bd128)


def _weight_grad(parts, u, init, tm, name):
    bsz, s, d = u.shape
    n_p = len(parts)
    nrows = sum(hi - lo for _, lo, hi in parts)

    def body(*refs):
        p_refs, u_ref = refs[:n_p], refs[n_p]
        gi_ref = refs[n_p + 1] if init is not None else None
        gw_ref = refs[-1]
        first = jnp.logical_and(pl.program_id(0) == 0, pl.program_id(1) == 0)
        dp = jnp.concatenate([r[0, :, lo:hi] for r, (_, lo, hi) in zip(p_refs, parts)], axis=1)
        gw = lax.dot_general(dp, u_ref[0], TN, preferred_element_type=F32)

        @pl.when(first)
        def _():
            gw_ref[...] = gw
            if init is not None:
                gw_ref[KV_LO:KV_HI, :] += gi_ref[...]

        @pl.when(jnp.logical_not(first))
        def _():
            gw_ref[...] += gw

    tok = lambda w: pl.BlockSpec((1, tm, w), lambda b, i: (b, i, 0))
    in_specs = [tok(a.shape[2]) for a, _, _ in parts] + [tok(d)]
    args = [a for a, _, _ in parts] + [u]
    if init is not None:
        in_specs.append(pl.BlockSpec(init.shape, lambda b, i: (0, 0)))
        args.append(init)
    return pl.pallas_call(
        body, name=name, grid=(bsz, s // tm), in_specs=in_specs,
        out_specs=pl.BlockSpec((nrows, d), lambda b, i: (0, 0)), out_shape=jax.ShapeDtypeStruct((nrows, d), F32),
        compiler_params=_params(2),
    )(*args)


def _inproj_backward(dps, x, dh, scale1p, norm_w, w_t, tm, name, fused=None):
    bsz, s, d = x.shape
    n_p = len(dps)
    shared = scale1p.shape[0] == 1
    with_dx = dh is not None
    n_in = n_p + (2 if with_dx else 1) + 3
    n_out = 3 if with_dx else 2
    total = bsz * (s // tm)
    at_steps = [(0, True), (max(1, total // 8), True), (max(2, (3 * total) // 4), True), (total - 1, False)]

    def body(*refs):
        ins, outs, _ = _split_fused(refs, n_in, n_out, 0, fused)
        dp_refs, x_ref = ins[:n_p], ins[n_p]
        dh_ref = ins[n_p + 1] if with_dx else None
        sc_ref, nw_ref, w_ref = ins[-3:]
        mod_ref, gnw_ref = outs[-2:]
        b, i = pl.program_id(0), pl.program_id(1)
        step = b * (s // tm) + i
        _run_phases(fused, step, at_steps, True)
        first = jnp.logical_and(b == 0, i == 0)
        dp = dp_refs[0][0] if n_p == 1 else jnp.concatenate([r[0] for r in dp_refs], axis=1)
        du = jnp.dot(dp, w_ref[...], preferred_element_type=F32)
        xv = x_ref[0]
        rstd = lax.rsqrt(jnp.mean(xv * xv, axis=-1, keepdims=True) + EPS)
        xhat = xv * rstd
        nw, sc = nw_ref[...], sc_ref[0]
        red = lambda v: jnp.sum(v, axis=0, keepdims=True)
        mod_rows = jnp.concatenate([red(du), red(du * (xhat * nw)), jnp.zeros((6, d), F32)], axis=0)
        gnw_rows = jnp.concatenate([red(du * sc * xhat), jnp.zeros((7, d), F32)], axis=0)
        mod_first = first if shared else i == 0

        @pl.when(mod_first)
        def _():
            mod_ref[0] = mod_rows

        @pl.when(jnp.logical_not(mod_first))
        def _():
            mod_ref[0] += mod_rows

        @pl.when(first)
        def _():
            gnw_ref[...] = gnw_rows

        @pl.when(jnp.logical_not(first))
        def _():
            gnw_ref[...] += gnw_rows

        if with_dx:
            dxhat = du * (nw * sc)
            outs[0][0] = dh_ref[0] + rstd * (dxhat - xhat * jnp.mean(dxhat * xhat, axis=-1, keepdims=True))
        _run_phases(fused, step, at_steps, False)

    tok = lambda w: pl.BlockSpec((1, tm, w), lambda b, i: (b, i, 0))
    in_specs = [tok(p.shape[2]) for p in dps] + [tok(d)]
    args = list(dps) + [x]
    if with_dx:
        in_specs.append(tok(d))
        args.append(dh)
    in_specs += [_bcast_spec(scale1p), pl.BlockSpec((1, d), lambda b, i: (0, 0)),
                 pl.BlockSpec(w_t.shape, lambda b, i: (0, 0))]
    args += [scale1p, norm_w, w_t]
    bm = scale1p.shape[0]
    mod_spec = pl.BlockSpec((1, 8, d), (lambda b, i: (0, 0, 0)) if shared else (lambda b, i: (b, 0, 0)))
    out_specs = [mod_spec, pl.BlockSpec((8, d), lambda b, i: (0, 0))]
    out_shape = [jax.ShapeDtypeStruct((bm, 8, d), F32), jax.ShapeDtypeStruct((8, d), F32)]
    if with_dx:
        out_specs.insert(0, tok(d))
        out_shape.insert(0, jax.ShapeDtypeStruct((bsz, s, d), F32))
    scratch = []
    if fused is not None:
        in_specs += fused.in_specs
        args += fused.arrs
        out_specs += fused.out_specs
        out_shape += fused.out_shape
        scratch = fused.scratch
    res = pl.pallas_call(
        body, name=name, grid=(bsz, s // tm), in_specs=in_specs, out_specs=out_specs, out_shape=out_shape,
        scratch_shapes=scratch, compiler_params=_params(2),
    )(*args)
    return list(res) if with_dx else [None] + list(res)


def _adamw(w, g, m, v, name):
    r, cdim = w.shape
    tr = 256 if r % 256 == 0 and r > 256 else r

    def body(w_ref, g_ref, m_ref, v_ref, d_ref, nm_ref, nv_ref):
        gv = g_ref[...]
        mn = ADAM_B1 * m_ref[...] + (1.0 - ADAM_B1) * gv
        vn = ADAM_B2 * v_ref[...] + (1.0 - ADAM_B2) * (gv * gv)
        m_hat = mn / (1.0 - ADAM_B1 ** ADAM_STEP)
        v_hat = vn / (1.0 - ADAM_B2 ** ADAM_STEP)
        d_ref[...] = -ADAM_LR * (m_hat / (jnp.sqrt(v_hat) + ADAM_EPS) + ADAM_WD * w_ref[...])
        nm_ref[...] = mn
        nv_ref[...] = vn

    spec = pl.BlockSpec((tr, cdim), lambda i: (i, 0))
    return pl.pallas_call(
        body, name=name, grid=(r // tr,), in_specs=[spec] * 4, out_specs=[spec] * 3,
        out_shape=[jax.ShapeDtypeStruct((r, cdim), F32)] * 3, compiler_params=_params(1),
    )(w, g, m, v)


def _rope_tables(s):
    rows = s // GRID_W
    freqs = np.float32(ROPE_THETA) ** (-np.arange(0, ROPE_AXIS_DIM, 2, dtype=np.float32) / np.float32(ROPE_AXIS_DIM))
    ang_r = np.arange(rows, dtype=np.float32)[:, None] * freqs[None, :]
    ang_c = np.arange(GRID_W, dtype=np.float32)[:, None] * freqs[None, :]
    rep = lambda t: jnp.repeat(jnp.asarray(t, dtype=F32), GRID_W, axis=0)
    til = lambda t: jnp.tile(jnp.asarray(t, dtype=F32), (rows, 1))
    cr, sr, cc, sc = rep(np.cos(ang_r)), rep(np.sin(ang_r)), til(np.cos(ang_c)), til(np.sin(ang_c))
    cos64 = jnp.concatenate([cr, cr, cc, cc], axis=1)
    sin64 = jnp.concatenate([-sr, sr, -sc, sc], axis=1)
    return jnp.concatenate([cos64, cos64], axis=1), jnp.concatenate([sin64, sin64], axis=1)


def _pack_rows(parts, rows):
    flat = jnp.concatenate([p.reshape(-1) for p in parts])
    return jnp.pad(flat, (0, rows * D_MODEL - flat.shape[0])).reshape(rows, D_MODEL)


def kernel(x, c, ctx, c_ctx, w_mod, b_mod, norm_w, w_in, q_norm_w, k_norm_w, conv_w, conv_b, conv_ln_w, conv_ln_b, w_pw, b_pw, w_out, loss_target, m_c_ctx, m_w_mod, m_b_mod, m_norm_w, m_w_in, m_q_norm_w, m_k_norm_w, m_conv_w, m_conv_b, m_conv_ln_w, m_conv_ln_b, m_w_pw, m_b_pw, m_w_out, v_c_ctx, v_w_mod, v_b_mod, v_norm_w, v_w_in, v_q_norm_w, v_k_norm_w, v_conv_w, v_conv_b, v_conv_ln_w, v_conv_ln_b, v_w_pw, v_b_pw, v_w_out):
    bsz, s, d = x.shape
    cl = ctx.shape[1]
    xi, yi, ci = lax.axis_index("x"), lax.axis_index("y"), lax.axis_index("c")
    chip = 2 * xi + yi
    dev = 2 * chip + ci
    ncol_mod = w_mod.shape[2]

    w_in_t_loc = w_in[0].T.astype(BF16)
    b_cols = lax.dynamic_slice(b_mod, (0, chip * ncol_mod), (1, ncol_mod))
    sc_rows, mod_g, g_in = _front(jnp.pad(c, ((0, 8 - bsz), (0, 0))), jnp.pad(c_ctx[None, :], ((0, 15), (0, 0))),
                                  w_mod[0], b_cols, w_in_t_loc)
    w_in_t = g_in.reshape(D_IN, d)
    mod_all = mod_g.transpose(1, 0, 2).reshape(80, 3 * d)
    mod_loc = lax.dynamic_slice(mod_all, (8 * dev, 0), (bsz, 3 * d))
    shift, scale1p, gate = mod_loc[:, None, :d], 1.0 + mod_loc[:, None, d:2 * d], mod_loc[:, None, 2 * d:]
    shift_c, scale1p_c = mod_all[64:65, :d][None], 1.0 + mod_all[64:65, d:2 * d][None]

    cos, sin = _rope_tables(s)
    qnw512 = jnp.tile(q_norm_w, (1, 8))
    knw128 = jnp.tile(k_norm_w, (1, 2))
    bd512 = jnp.kron(jnp.eye(8, dtype=F32), jnp.ones((HEAD_DIM, HEAD_DIM), F32)).astype(BF16)
    bd128 = bd512[:128, :128]

    u, p_qkv, p_za, p_ga, p_gg, p_zc = _norm_inproj(x, shift, scale1p, norm_w, w_in_t, SPLITS, 512, "norm_inproj")
    uc, pc_kv = _norm_inproj(ctx, shift_c, scale1p_c, norm_w, w_in_t[KV_LO:KV_HI], ((0, 256),), cl, "ctx_norm_inproj")
    q, k2x, v2x = _qkv_prep(p_qkv, cos, sin, qnw512, knw128, bd512, bd128, 256, cl)
    k2, v2 = _ctx_kv_prep(pc_kv, knw128, bd128, k2x, v2x)
    conv_w_loc = jnp.pad(conv_w[0], ((0, 1), (0, 0)))
    o, lse, g_out, g_pw, g_cw = _attn_forward(
        q, k2, v2, 256, _ChipGather([w_out[0].astype(BF16), w_pw[0].astype(BF16), conv_w_loc]))
    w_out_f = g_out.reshape(d, d)
    w_pw_f = g_pw.reshape(D_CONV, D_CONV)
    conv_w_f = g_cw.transpose(1, 0, 2).reshape(32, D_CONV)
    y, cv = _conv_forward(p_ga, p_gg, conv_w_f, conv_b, conv_ln_w, conv_ln_b, w_pw_f, b_pw, 256)
    loss_part, dh, do, dza, dcv, dzc, dgate, gw_out = _outproj_loss(
        x, loss_target, gate, o, p_za, cv, p_zc, w_out_f, 256)

    all_chips, half_rows = (0, 1, 2, 3), D_IN // 2
    dy, gw_pw, conv_stats = _conv_token_backward(dcv, y, conv_ln_w, conv_ln_b, w_pw_f, 256)
    da, dg, gcw = _conv_backward(dy, p_ga, p_gg, conv_w_f, 256)
    tw = min(1024, s)
    gw_hi = _weight_grad([(da, half_rows - SPLITS[2][0], 512), (dg, 0, 512), (dzc, 0, 512)], u, None, tw,
                         "grad_in_rows_hi")
    dq, dk2, dv2, _, _, _, r_out, r_pw, r_hi = _attn_backward(
        q, k2, v2, o, do, lse, 256, _FusedReduce([(gw_out, all_chips), (gw_pw, all_chips), (gw_hi, (2, 3))]))
    dqkv, qk_stats = _qkv_backward(p_qkv, dq, dk2, dv2, cos, sin, qnw512, knw128, bd512, bd128, 256, cl)
    dpc, kc_stats = _ctx_kv_backward(pc_kv, dk2, dv2, knw128, bd128)
    gw_ctx = _weight_grad([(dpc, 0, 256)], uc, None, cl, "grad_in_rows_ctx")
    gw_lo = _weight_grad([(dqkv, 0, 768), (dza, 0, 512), (da, 0, half_rows - SPLITS[2][0])], u, gw_ctx, tw,
                         "grad_in_rows_lo")
    _, modc, gnw_c = _inproj_backward([dpc], ctx, None, scale1p_c, norm_w, w_in_t[KV_LO:KV_HI], cl,
                                      "ctx_inproj_backward")
    grad_x, modx, gnw_x, _, r_lo = _inproj_backward(
        [dqkv, dza, da, dg, dzc], x, dh, scale1p, norm_w, w_in_t, 256, "inproj_backward",
        _FusedReduce([(gw_lo, (0, 1))]))
    g_w_out, g_w_pw = r_out.reshape(d // 4, d), r_pw.reshape(D_CONV // 4, D_CONV)
    g_w_in_t = jnp.where(chip < 2, r_lo, r_hi).reshape(D_IN // 4, d)

    dmod_loc = jnp.concatenate([modx[:, 0, :], modx[:, 1, :], dgate[:, 0, :]], axis=1)
    gq = qk_stats[0].reshape(8, HEAD_DIM).sum(axis=0)
    gk = (qk_stats[1, :128] + kc_stats[0]).reshape(2, HEAD_DIM).sum(axis=0)
    packed = _pack_rows([dmod_loc, dmod_loc.sum(axis=0), gnw_x[0] + gnw_c[0], modc[0, 0], modc[0, 1], gq, gk,
                         conv_stats[0], conv_stats[1], conv_stats[2], conv_stats[3], gcw,
                         jnp.sum(loss_part[:, 0, 0])[None]], 32)
    gathered, total = _all_gather8(packed, "small_grads", reduce=True)
    flat = total.reshape(-1)
    offs = [0]

    def take(nelem):
        lo = offs[0]
        offs[0] = lo + nelem
        return flat[lo:lo + nelem]

    take(bsz * 3 * d)
    g_b_mod_x = take(3 * d)
    g_norm_w = take(d)
    dshift_c, dscale_c = take(d), take(d)
    g_qnw, g_knw = take(HEAD_DIM), take(HEAD_DIM)
    g_b_pw, g_ln_w, g_ln_b, g_conv_b = take(D_CONV), take(D_CONV), take(D_CONV), take(D_CONV)
    g_conv_w_full = take(32 * D_CONV).reshape(32, D_CONV)
    loss = take(1)[0] * (0.5 / d)

    dmod_c = jnp.concatenate([dshift_c, dscale_c, jnp.zeros((d,), F32)])
    g_b_mod = (g_b_mod_x + dmod_c)[None, :]
    dmod_rows = jnp.pad(gathered[:, :6, :].reshape(8, bsz, 3 * d), ((0, 0), (0, 8 - bsz), (0, 0))).reshape(64, 3 * d)
    dmod_all = jnp.concatenate([dmod_rows, jnp.pad(dmod_c[None, :], ((0, 15), (0, 0)))], axis=0)
    dmod_cols = lax.dynamic_slice(dmod_all, (0, chip * ncol_mod), (80, ncol_mod))
    g_w_mod, gcc_part = _mod_backward(sc_rows, dmod_cols, w_mod[0])
    gcc_all = _all_gather8(gcc_part, "c_ctx_grad_gather")
    dsilu_ctx = gcc_all[0, 0] + gcc_all[2, 0] + gcc_all[4, 0] + gcc_all[6, 0]
    sg = _sigmoid(c_ctx)
    g_c_ctx = dsilu_ctx * (sg * (1.0 + c_ctx * (1.0 - sg)))

    g_w_in = g_w_in_t.T
    g_conv_w = lax.dynamic_slice(g_conv_w_full, (0, chip * 128), (CONV_WIDTH, 128))

    grads = {
        "c_ctx": g_c_ctx, "w_mod": g_w_mod[None], "b_mod": g_b_mod, "norm_w": g_norm_w[None], "w_in": g_w_in[None],
        "q_norm_w": g_qnw[None], "k_norm_w": g_knw[None], "conv_w": g_conv_w[None], "conv_b": g_conv_b[None],
        "conv_ln_w": g_ln_w[None], "conv_ln_b": g_ln_b[None], "w_pw": g_w_pw[None], "b_pw": g_b_pw[None],
        "w_out": g_w_out[None],
    }
    weights = {
        "c_ctx": (c_ctx, m_c_ctx, v_c_ctx), "w_mod": (w_mod, m_w_mod, v_w_mod), "b_mod": (b_mod, m_b_mod, v_b_mod),
        "norm_w": (norm_w, m_norm_w, v_norm_w), "w_in": (w_in, m_w_in, v_w_in),
        "q_norm_w": (q_norm_w, m_q_norm_w, v_q_norm_w), "k_norm_w": (k_norm_w, m_k_norm_w, v_k_norm_w),
        "conv_w": (conv_w, m_conv_w, v_conv_w), "conv_b": (conv_b, m_conv_b, v_conv_b),
        "conv_ln_w": (conv_ln_w, m_conv_ln_w, v_conv_ln_w), "conv_ln_b": (conv_ln_b, m_conv_ln_b, v_conv_ln_b),
        "w_pw": (w_pw, m_w_pw, v_w_pw), "b_pw": (b_pw, m_b_pw, v_b_pw), "w_out": (w_out, m_w_out, v_w_out),
    }
    names = list(weights)
    deltas, new_ms, new_vs = [], [], []
    for n in names:
        w, m, v = weights[n]
        shape = w.shape
        two_d = (1, shape[0]) if w.ndim == 1 else (shape[-2] if w.ndim == 3 else 1, shape[-1])
        dl, nm, nv = _adamw(w.reshape(two_d), grads[n].reshape(two_d), m.reshape(two_d), v.reshape(two_d), "adamw_" + n)
        deltas.append(dl.reshape(shape))
        new_ms.append(nm.reshape(shape))
        new_vs.append(nv.reshape(shape))
        grads[n] = grads[n].reshape(shape)

    return (loss, grad_x, *[grads[n] for n in names], *deltas, *new_ms, *new_vs)
```

```python
import functools
import math

import jax
import jax.numpy as jnp
import numpy as np
from jax import lax
from jax.experimental import pallas as pl
from jax.experimental.pallas import tpu as pltpu

F32 = jnp.float32
BF16 = jnp.bfloat16
MESH = pl.DeviceIdType.MESH

D_MODEL = 1024
D_ATTN = 512
D_CONV = 512
HEAD_DIM = 64
N_KV = 2
GRID_W = 64
ROPE_AXIS_DIM = 32
ROPE_THETA = 10000.0
CONV_WIDTH = 31
CONV_PAD = 15
HALO = 16
EPS = 1e-6
SPLITS = ((0, 768), (768, 1280), (1280, 1792), (1792, 2304), (2304, 2816))
D_IN = 2816
KV_LO, KV_HI = 512, 768

ADAM_LR = 0.001
ADAM_B1 = 0.9
ADAM_B2 = 0.999
ADAM_EPS = 1e-08
ADAM_WD = 0.01
ADAM_STEP = 10

VMEM_LIMIT = 56 * 1024 * 1024

NT = (((1,), (1,)), ((), ()))
TN = (((0,), (0,)), ((), ()))


def _params(n_axes=0, **kw):
    if n_axes:
        kw["dimension_semantics"] = ("arbitrary",) * n_axes
    return pltpu.CompilerParams(vmem_limit_bytes=VMEM_LIMIT, **kw)


def _sigmoid(x):
    return 1.0 / (1.0 + jnp.exp(-x))


def _silu_and_grad(z):
    s = _sigmoid(z)
    return z * s, s * (1.0 + z * (1.0 - s))


def _seg_mean(v, ones_bd):
    hi = v.astype(BF16)
    lo = (v - hi.astype(F32)).astype(BF16)
    s = jnp.dot(hi, ones_bd, preferred_element_type=F32) + jnp.dot(lo, ones_bd, preferred_element_type=F32)
    return s * (1.0 / HEAD_DIM)


def _partner(v):
    n = v.shape[1]
    lane = lax.broadcasted_iota(jnp.int32, (v.shape[0], 128), 1)
    first = (lane % 32) < 16
    parts = []
    for k in range(n // 128):
        ch = v[:, 128 * k:128 * (k + 1)]
        parts.append(jnp.where(first, pltpu.roll(ch, 112, 1), pltpu.roll(ch, 16, 1)))
    return parts[0] if len(parts) == 1 else jnp.concatenate(parts, axis=1)


def _tile_lanes(t, reps):
    return t if reps == 1 else jnp.concatenate([t] * reps, axis=1)


def _lo_mask(rows):
    return lax.broadcasted_iota(jnp.int32, (rows, 128), 1) < HEAD_DIM


def _gather8_in_vmem(x_ref, out_ref, send_sems, recv_sems, local_sem):
    x, y, c = lax.axis_index("x"), lax.axis_index("y"), lax.axis_index("c")
    me, sibling = (x, y, c), (x, y, 1 - c)
    chips = [(1 - x, y), (x, 1 - y), (1 - x, 1 - y)]

    def slot(px, py, pc):
        return out_ref.at[4 * px + 2 * py + pc]

    def copy(k, block, to, src=None):
        return pltpu.make_async_remote_copy(
            src_ref=slot(*block) if src is None else src, dst_ref=slot(*block),
            send_sem=send_sems.at[k], recv_sem=recv_sems.at[k], device_id=to, device_id_type=MESH)

    mine = pltpu.make_async_copy(x_ref, slot(*me), local_sem)
    mine.start()
    first = [copy(0, me, sibling, src=x_ref)]
    first += [copy(1 + j, me, (*chip, c), src=x_ref) for j, chip in enumerate(chips)]
    for cp in first:
        cp.start()
    passed = [copy(4 + j, (*chip, c), sibling) for j, chip in enumerate(chips)]
    for j, chip in enumerate(chips):
        copy(1 + j, (*chip, c), me).wait_recv()
        passed[j].start()
    copy(0, sibling, me).wait_recv()
    for j, chip in enumerate(chips):
        copy(4 + j, (*chip, 1 - c), me).wait_recv()
    for cp in first + passed:
        cp.wait_send()
    mine.wait()


def _all_gather8(x_local, name, reduce=False):
    r, cdim = x_local.shape

    def body(x_ref, out_ref, *rest):
        if reduce:
            sum_ref, send_sems, recv_sems, local_sem = rest
        else:
            send_sems, recv_sems, local_sem = rest
        _gather8_in_vmem(x_ref, out_ref, send_sems, recv_sems, local_sem)
        if reduce:
            acc = out_ref[0]
            for d in range(1, 8):
                acc = acc + out_ref[d]
            sum_ref[...] = acc

    out_shape = [jax.ShapeDtypeStruct((8, r, cdim), x_local.dtype)]
    out_specs = [pl.BlockSpec(memory_space=pltpu.VMEM)]
    if reduce:
        out_shape.append(jax.ShapeDtypeStruct((r, cdim), x_local.dtype))
        out_specs.append(pl.BlockSpec(memory_space=pltpu.VMEM))
    res = pl.pallas_call(
        body, name=name, out_shape=out_shape,
        in_specs=[pl.BlockSpec(memory_space=pltpu.VMEM)], out_specs=out_specs,
        scratch_shapes=[pltpu.SemaphoreType.DMA((7,)), pltpu.SemaphoreType.DMA((7,)), pltpu.SemaphoreType.DMA],
        compiler_params=_params(),
    )(x_local)
    return res if reduce else res[0]


class _ChipGather:
    def __init__(self, arrs):
        self.arrs = list(arrs)
        n = self.n = len(self.arrs)
        self.in_specs = [pl.BlockSpec(memory_space=pl.ANY)] * n
        self.out_shape = [jax.ShapeDtypeStruct((4,) + a.shape, a.dtype) for a in self.arrs]
        self.out_specs = [pl.BlockSpec(memory_space=pl.ANY)] * n
        self.scratch = [pltpu.SemaphoreType.DMA((6 * n,)), pltpu.SemaphoreType.DMA((6 * n,)),
                        pltpu.SemaphoreType.DMA((n,))]
        self.phases = [self.start, self.forward, self.finish]

    def bind(self, ins, outs, scratch):
        self.ins, self.outs = ins, outs
        self.send_sems, self.recv_sems, self.local_sems = scratch
        self.x, self.y, self.c = lax.axis_index("x"), lax.axis_index("y"), lax.axis_index("c")
        self.chips = [(1 - self.x, self.y), (self.x, 1 - self.y), (1 - self.x, 1 - self.y)]
        self.mychip = 2 * self.x + self.y

    def _copy(self, a, k, chip_idx, cc, to, src=None):
        h = self.arrs[a].shape[0] // 2
        dst = self.outs[a].at[chip_idx, pl.ds(cc * h, h)]
        return pltpu.make_async_remote_copy(
            src_ref=dst if src is None else src, dst_ref=dst, send_sem=self.send_sems.at[6 * a + k],
            recv_sem=self.recv_sems.at[6 * a + k], device_id=to, device_id_type=MESH)

    def _local(self, a):
        return pltpu.make_async_copy(self.ins[a], self.outs[a].at[self.mychip], self.local_sems.at[a])

    def _first(self, a, j):
        h = self.arrs[a].shape[0] // 2
        return self._copy(a, j, self.mychip, self.c, (*self.chips[j], self.c), src=self.ins[a].at[pl.ds(self.c * h, h)])

    def _passed(self, a, j):
        cx, cy = self.chips[j]
        return self._copy(a, 3 + j, 2 * cx + cy, self.c, (self.x, self.y, 1 - self.c))

    def start(self):
        for a in range(self.n):
            self._local(a).start()
            for j in range(3):
                self._first(a, j).start()

    def forward(self):
        for a in range(self.n):
            for j, (cx, cy) in enumerate(self.chips):
                self._copy(a, j, 2 * cx + cy, self.c, (self.x, self.y, self.c)).wait_recv()
                self._passed(a, j).start()

    def finish(self):
        for a in range(self.n):
            for j, (cx, cy) in enumerate(self.chips):
                self._copy(a, 3 + j, 2 * cx + cy, 1 - self.c, (self.x, self.y, self.c)).wait_recv()
        for a in range(self.n):
            for j in range(3):
                self._first(a, j).wait_send()
                self._passed(a, j).wait_send()
            self._local(a).wait()


class _FusedReduce:
    def __init__(self, pieces):
        self.owners = [tuple(o) for _, o in pieces]
        self.arrs = [g.reshape(len(o), 2, g.shape[0] // (2 * len(o)), g.shape[1]) for g, o in pieces]
        n = self.n = len(pieces)
        hc = self.hc = [(v.shape[2], v.shape[3]) for v in self.arrs]
        nts = [len(o) for o in self.owners]
        self.base = [sum(nts[:p]) for p in range(n)]
        anyspec = pl.BlockSpec(memory_space=pl.ANY)
        self.in_specs = [anyspec] * n
        self.out_shape = [jax.ShapeDtypeStruct((2,) + s, F32) for s in hc]
        self.out_specs = [anyspec] * n
        self.scratch = [pltpu.VMEM((nt,) + s, F32) for nt, s in zip(nts, hc)]
        self.scratch += [pltpu.VMEM((nt,) + s, F32) for nt, s in zip(nts, hc)]
        self.scratch += [pltpu.VMEM(s, F32) for s in hc]
        self.scratch += [pltpu.VMEM((nt,) + s, BF16) for nt, s in zip(nts, hc)]
        self.scratch += [pltpu.VMEM((3,) + s, BF16) for s in hc]
        self.scratch += [pltpu.VMEM(s, F32) for s in hc]
        tot = sum(nts)
        self.scratch += [pltpu.SemaphoreType.DMA((tot,)), pltpu.SemaphoreType.DMA((tot,)),
                         pltpu.SemaphoreType.DMA((tot,)), pltpu.SemaphoreType.DMA((3 * n,)),
                         pltpu.SemaphoreType.DMA((n,)), pltpu.SemaphoreType.DMA((n,)), pltpu.SemaphoreType.DMA((n,)),
                         pltpu.SemaphoreType.DMA((tot,))]
        self.phases = [self.start, self.exchange, self.combine, self.finish]

    def bind(self, ins, outs, scratch):
        n = self.n
        self.g, self.out = ins, outs
        self.va, self.recv_a, self.own = scratch[:n], scratch[n:2 * n], scratch[2 * n:3 * n]
        self.tsend, self.recv_b, self.fin = scratch[3 * n:4 * n], scratch[4 * n:5 * n], scratch[5 * n:6 * n]
        self.sa, self.ra, self.sb, self.rb, self.sc, self.rc, self.lc, self.la = scratch[6 * n:]
        self.x, self.y, self.c = lax.axis_index("x"), lax.axis_index("y"), lax.axis_index("c")
        self.mychip = 2 * self.x + self.y
        self.sibling = (self.x, self.y, 1 - self.c)

    def _copy_a(self, p, t):
        k = self.base[p] + t
        return pltpu.make_async_remote_copy(
            src_ref=self.g[p].at[t, 1 - self.c], dst_ref=self.recv_a[p].at[t], send_sem=self.sa.at[k],
            recv_sem=self.ra.at[k], device_id=self.sibling, device_id_type=MESH)

    def _fetch(self, p, t):
        return pltpu.make_async_copy(self.g[p].at[t, self.c], self.va[p].at[t], self.la.at[self.base[p] + t])

    def _slot(self, owner):
        rel = jnp.bitwise_xor(self.mychip, owner)
        return jnp.where(rel == 2, 0, jnp.where(rel == 1, 1, 2))

    def _copy_b(self, p, t, slot):
        owner = self.owners[p][t]
        return pltpu.make_async_remote_copy(
            src_ref=self.tsend[p].at[t], dst_ref=self.recv_b[p].at[slot], send_sem=self.sb.at[self.base[p] + t],
            recv_sem=self.rb.at[3 * p + slot], device_id=(owner // 2, owner % 2, self.c), device_id_type=MESH)

    def _copy_c(self, p, half):
        return pltpu.make_async_remote_copy(
            src_ref=self.fin[p], dst_ref=self.out[p].at[half], send_sem=self.sc.at[p], recv_sem=self.rc.at[p],
            device_id=self.sibling, device_id_type=MESH)

    def _local_c(self, p):
        return pltpu.make_async_copy(self.fin[p], self.out[p].at[self.c], self.lc.at[p])

    def start(self):
        for p in range(self.n):
            for t in range(len(self.owners[p])):
                self._copy_a(p, t).start()
                self._fetch(p, t).start()

    def exchange(self):
        for p in range(self.n):
            for t, owner in enumerate(self.owners[p]):
                self._copy_a(p, t).wait_recv()
                self._fetch(p, t).wait()
                mine = self.mychip == owner

                @pl.when(mine)
                def _():
                    self.own[p][...] = self.va[p][t] + self.recv_a[p][t]

                @pl.when(jnp.logical_not(mine))
                def _():
                    self.tsend[p][t] = (self.va[p][t] + self.recv_a[p][t]).astype(BF16)
                    self._copy_b(p, t, self._slot(owner)).start()

    def combine(self):
        for p in range(self.n):
            for t, owner in enumerate(self.owners[p]):
                @pl.when(self.mychip == owner)
                def _():
                    acc = self.own[p][...]
                    for j in range(3):
                        self._copy_b(p, t, j).wait_recv()
                        acc = acc + self.recv_b[p][j].astype(F32)
                    self.fin[p][...] = acc
                    self._local_c(p).start()
                    self._copy_c(p, self.c).start()

    def finish(self):
        for p in range(self.n):
            for t, owner in enumerate(self.owners[p]):
                self._copy_a(p, t).wait_send()
                mine = self.mychip == owner

                @pl.when(mine)
                def _():
                    self._copy_c(p, 1 - self.c).wait_recv()
                    self._copy_c(p, self.c).wait_send()
                    self._local_c(p).wait()

                @pl.when(jnp.logical_not(mine))
                def _():
                    self._copy_b(p, t, self._slot(owner)).wait_send()


def _split_fused(refs, n_in, n_out, n_scr, fused):
    if fused is None:
        return refs[:n_in], refs[n_in:n_in + n_out], refs[n_in + n_out:]
    fi, fo = len(fused.in_specs), len(fused.out_specs)
    ins, rest = refs[:n_in], refs[n_in:]
    f_ins, rest = rest[:fi], rest[fi:]
    outs, rest = rest[:n_out], rest[n_out:]
    f_outs, rest = rest[:fo], rest[fo:]
    scr, f_scr = rest[:n_scr], rest[n_scr:]
    fused.bind(f_ins, f_outs, f_scr)
    return ins, outs, scr


def _run_phases(fused, step, at_steps, before):
    if fused is None:
        return
    for phase, (at, first) in zip(fused.phases, at_steps):
        if first == before:
            pl.when(step == at)(phase)


def _front(c_pad, c_ctx_rows, w_mod, b_cols, w_in_t_loc):
    ncol = w_mod.shape[1]
    gather = _ChipGather([w_in_t_loc])

    def body(c_ref, cctx_ref, w_ref, b_ref, win_ref, sc_ref, modg_ref, wing_ref,
             call_ref, ag_send, ag_recv, ag_local, m_send, m_recv, *g_scr):
        gather.bind([win_ref], [wing_ref], g_scr)
        gather.start()
        _gather8_in_vmem(c_ref, call_ref, ag_send, ag_recv, ag_local)
        x, y, c = lax.axis_index("x"), lax.axis_index("y"), lax.axis_index("c")
        chips = [(1 - x, y), (x, 1 - y), (1 - x, 1 - y)]
        mychip = 2 * x + y
        rows = jnp.concatenate([call_ref[dv] for dv in range(8)] + [cctx_ref[...]], axis=0)
        sc = rows * _sigmoid(rows)
        sc_ref[...] = sc
        modg_ref[mychip] = jnp.dot(sc, w_ref[...], preferred_element_type=F32,
                                   precision=lax.Precision.HIGHEST) + b_ref[...]

        def mcopy(j, chip_idx, to):
            return pltpu.make_async_remote_copy(
                src_ref=modg_ref.at[chip_idx], dst_ref=modg_ref.at[chip_idx], send_sem=m_send.at[j],
                recv_sem=m_recv.at[j], device_id=to, device_id_type=MESH)

        sends = [mcopy(j, mychip, (*chip, c)) for j, chip in enumerate(chips)]
        for cp in sends:
            cp.start()
        for j, (cx, cy) in enumerate(chips):
            mcopy(j, 2 * cx + cy, (x, y, c)).wait_recv()
        gather.forward()
        gather.finish()
        for cp in sends:
            cp.wait_send()

    vm = pl.BlockSpec(memory_space=pltpu.VMEM)
    return pl.pallas_call(
        body, name="front_exchange",
        out_shape=[jax.ShapeDtypeStruct((80, D_MODEL), F32), jax.ShapeDtypeStruct((4, 80, ncol), F32)] + gather.out_shape,
        in_specs=[vm, vm, vm, vm] + gather.in_specs, out_specs=[vm, vm] + gather.out_specs,
        scratch_shapes=[pltpu.VMEM((8, 8, D_MODEL), F32), pltpu.SemaphoreType.DMA((7,)), pltpu.SemaphoreType.DMA((7,)),
                        pltpu.SemaphoreType.DMA, pltpu.SemaphoreType.DMA((3,)), pltpu.SemaphoreType.DMA((3,))]
        + gather.scratch,
        compiler_params=_params(),
    )(c_pad, c_ctx_rows, w_mod, b_cols, w_in_t_loc)


def _mod_backward(sc_rows, dmod_cols, w_mod):
    rows, ncol = dmod_cols.shape

    def body(sc_ref, dm_ref, w_ref, gw_ref, gcc_ref):
        gw_ref[...] = lax.dot_general(sc_ref[...], dm_ref[...], TN, preferred_element_type=F32,
                                      precision=lax.Precision.HIGHEST)
        gcc_ref[...] = lax.dot_general(dm_ref[64:72, :], w_ref[...], NT, preferred_element_type=F32,
                                       precision=lax.Precision.HIGHEST)

    return pl.pallas_call(
        body, name="mod_backward",
        out_shape=[jax.ShapeDtypeStruct((D_MODEL, ncol), F32), jax.ShapeDtypeStruct((8, D_MODEL), F32)],
        compiler_params=_params(),
    )(sc_rows, dmod_cols, w_mod)


def _bcast_spec(arr):
    if arr.shape[0] == 1:
        return pl.BlockSpec((1, 1, arr.shape[2]), lambda b, i: (0, 0, 0))
    return pl.BlockSpec((1, 1, arr.shape[2]), lambda b, i: (b, 0, 0))


def _norm_inproj(x, shift, scale1p, norm_w, w_t, splits, tm, name):
    bsz, s, d = x.shape

    def body(x_ref, sh_ref, sc_ref, nw_ref, w_ref, u_ref, *out_refs):
        xv = x_ref[0]
        rstd = lax.rsqrt(jnp.mean(xv * xv, axis=-1, keepdims=True) + EPS)
        u = (xv * rstd * nw_ref[...]) * sc_ref[0] + sh_ref[0]
        ub = u.astype(BF16)
        u_ref[0] = ub
        for (lo, hi), o_ref in zip(splits, out_refs):
            o_ref[0] = lax.dot_general(ub, w_ref[lo:hi, :], NT, preferred_element_type=F32)

    tok = lambda w: pl.BlockSpec((1, tm, w), lambda b, i: (b, i, 0))
    return pl.pallas_call(
        body, name=name, grid=(bsz, s // tm),
        in_specs=[tok(d), _bcast_spec(shift), _bcast_spec(scale1p), pl.BlockSpec((1, d), lambda b, i: (0, 0)),
                  pl.BlockSpec(w_t.shape, lambda b, i: (0, 0))],
        out_specs=[tok(d)] + [tok(hi - lo) for lo, hi in splits],
        out_shape=[jax.ShapeDtypeStruct((bsz, s, d), BF16)]
        + [jax.ShapeDtypeStruct((bsz, s, hi - lo), F32) for lo, hi in splits],
        compiler_params=_params(2),
    )(x, shift, scale1p, norm_w, w_t)


def _dup_heads(kv, lo_mask):
    r = pltpu.roll(kv, HEAD_DIM, 1)
    return jnp.where(lo_mask, kv, r), jnp.where(lo_mask, r, kv)


def _qkv_prep(qkv, cos, sin, qnw, knw, bd512, bd128, ts, row0):
    bsz, s, _ = qkv.shape

    def body(p_ref, cos_ref, sin_ref, qnw_ref, knw_ref, bd512_ref, bd128_ref, q_ref, k_ref, v_ref):
        lo_mask = _lo_mask(ts)
        cos_t, sin_t = cos_ref[...], sin_ref[...]
        qp = p_ref[0, :, 0:512]
        qn = qp * lax.rsqrt(_seg_mean(qp * qp, bd512_ref[...]) + EPS) * qnw_ref[...]
        qr = qn * _tile_lanes(cos_t, 4) + _partner(qn) * _tile_lanes(sin_t, 4)
        q_ref[0] = (qr * (1.0 / math.sqrt(HEAD_DIM))).astype(BF16)
        kp = p_ref[0, :, 512:640]
        kn = kp * lax.rsqrt(_seg_mean(kp * kp, bd128_ref[...]) + EPS) * knw_ref[...]
        kr = kn * cos_t + _partner(kn) * sin_t
        k0, k1 = _dup_heads(kr, lo_mask)
        k_ref[0, 0] = k0.astype(BF16)
        k_ref[0, 1] = k1.astype(BF16)
        v0, v1 = _dup_heads(p_ref[0, :, 640:768], lo_mask)
        v_ref[0, 0] = v0.astype(BF16)
        v_ref[0, 1] = v1.astype(BF16)

    const = lambda a: pl.BlockSpec(a.shape, lambda b, i: (0,) * a.ndim)
    kv_spec = pl.BlockSpec((1, 2, ts, 128), lambda b, i: (b, 0, i + row0 // ts, 0))
    return pl.pallas_call(
        body, name="qkv_prep", grid=(bsz, s // ts),
        in_specs=[pl.BlockSpec((1, ts, 768), lambda b, i: (b, i, 0)),
                  pl.BlockSpec((ts, 128), lambda b, i: (i, 0)), pl.BlockSpec((ts, 128), lambda b, i: (i, 0)),
                  const(qnw), const(knw), const(bd512), const(bd128)],
        out_specs=[pl.BlockSpec((1, ts, 512), lambda b, i: (b, i, 0)), kv_spec, kv_spec],
        out_shape=[jax.ShapeDtypeStruct((bsz, s, 512), BF16), jax.ShapeDtypeStruct((bsz, 2, row0 + s, 128), BF16),
                   jax.ShapeDtypeStruct((bsz, 2, row0 + s, 128), BF16)],
        compiler_params=_params(2),
    )(qkv, cos, sin, qnw, knw, bd512, bd128)


def _ctx_kv_prep(pc, knw, bd128, k2, v2):
    bsz, cl, _ = pc.shape

    def body(p_ref, knw_ref, bd128_ref, k_in, v_in, k_ref, v_ref):
        lo_mask = _lo_mask(cl)
        kp = p_ref[0, :, 0:128]
        kn = kp * lax.rsqrt(_seg_mean(kp * kp, bd128_ref[...]) + EPS) * knw_ref[...]
        k0, k1 = _dup_heads(kn, lo_mask)
        k_ref[0, 0] = k0.astype(BF16)
        k_ref[0, 1] = k1.astype(BF16)
        v0, v1 = _dup_heads(p_ref[0, :, 128:256], lo_mask)
        v_ref[0, 0] = v0.astype(BF16)
        v_ref[0, 1] = v1.astype(BF16)

    const = lambda a: pl.BlockSpec(a.shape, lambda b: (0,) * a.ndim)
    kv_spec = pl.BlockSpec((1, 2, cl, 128), lambda b: (b, 0, 0, 0))
    return pl.pallas_call(
        body, name="ctx_kv_prep", grid=(bsz,),
        in_specs=[pl.BlockSpec((1, cl, 256), lambda b: (b, 0, 0)), const(knw), const(bd128),
                  pl.BlockSpec(memory_space=pl.ANY), pl.BlockSpec(memory_space=pl.ANY)],
        out_specs=[kv_spec, kv_spec],
        out_shape=[jax.ShapeDtypeStruct(k2.shape, BF16), jax.ShapeDtypeStruct(v2.shape, BF16)],
        input_output_aliases={3: 0, 4: 1},
        compiler_params=_params(1),
    )(pc, knw, bd128, k2, v2)


def _attn_forward(q, k2, v2, tq, fused):
    bsz, s, _ = q.shape
    sk = k2.shape[2]
    nq = s // tq
    total = bsz * N_KV * nq
    at_steps = [(0, True), (total // 4, True), (total - 1, False)]

    def body(*refs):
        (q_ref, k_ref, v_ref), (o_ref, lse_ref), _ = _split_fused(refs, 3, 2, 0, fused)
        step = (pl.program_id(0) * N_KV + pl.program_id(1)) * nq + pl.program_id(2)
        _run_phases(fused, step, at_steps, True)
        kk, vv = k_ref[0, 0], v_ref[0, 0]
        lo_mask = _lo_mask(tq)
        for j in range(2):
            qp = q_ref[0, :, 128 * j:128 * (j + 1)]
            outs, lses = [], []
            for half in range(2):
                sel = lo_mask if half == 0 else jnp.logical_not(lo_mask)
                qs = jnp.where(sel, qp, jnp.zeros_like(qp))
                sc = lax.dot_general(qs, kk, NT, preferred_element_type=F32)
                m = jnp.max(sc, axis=-1, keepdims=True)
                p = jnp.exp(sc - m)
                l = jnp.sum(p, axis=-1, keepdims=True)
                o = jnp.dot(p.astype(BF16), vv, preferred_element_type=F32)
                outs.append(o / l)
                lses.append(jnp.broadcast_to(m + jnp.log(l), (tq, 128)))
            o_ref[0, :, 128 * j:128 * (j + 1)] = jnp.where(lo_mask, outs[0], outs[1])
            lse_ref[0, :, 128 * j:128 * (j + 1)] = jnp.where(lo_mask, lses[0], lses[1])
        _run_phases(fused, step, at_steps, False)

    q_spec = pl.BlockSpec((1, tq, 256), lambda b, g, i: (b, i, g))
    kv_spec = pl.BlockSpec((1, 1, sk, 128), lambda b, g, i: (b, g, 0, 0))
    return pl.pallas_call(
        body, name="attn_forward", grid=(bsz, N_KV, nq),
        in_specs=[q_spec, kv_spec, kv_spec] + fused.in_specs, out_specs=[q_spec, q_spec] + fused.out_specs,
        out_shape=[jax.ShapeDtypeStruct((bsz, s, 512), F32)] * 2 + fused.out_shape,
        scratch_shapes=fused.scratch,
        compiler_params=_params(3),
    )(q, k2, v2, *fused.arrs)


def _halo_specs(width, ts, s):
    r = ts // HALO
    last = s // HALO - 1
    return [pl.BlockSpec((1, ts, width), lambda b, i: (b, i, 0)),
            pl.BlockSpec((1, HALO, width), lambda b, i: (b, jnp.maximum(i * r - 1, 0), 0)),
            pl.BlockSpec((1, HALO, width), lambda b, i: (b, jnp.minimum((i + 1) * r, last), 0))]


def _fill_ext(ext_ref, cur, prev, nxt, i, n_tiles, ts):
    ext_ref[0:HALO, :] = jnp.where(i > 0, prev, jnp.zeros_like(prev))
    ext_ref[HALO:HALO + ts, :] = cur
    ext_ref[HALO + ts:2 * HALO + ts, :] = jnp.where(i < n_tiles - 1, nxt, jnp.zeros_like(nxt))


def _fill_shifted(sh_ref, ext_ref, ts):
    n = ts + 2 * HALO - 8
    for r in range(1, 8):
        sh_ref[r - 1, 0:n, :] = ext_ref[pl.ds(r, n), :]


def _window(sh_ref, ext_ref, off, ts):
    q, r = divmod(off, 8)
    if r == 0:
        return ext_ref[pl.ds(off, ts), :]
    return sh_ref[r - 1, pl.ds(8 * q, ts), :]


def _conv_forward(ga, gg, conv_w, conv_b, ln_w, ln_b, w_pw, b_pw, ts):
    bsz, s, dc = ga.shape
    n_tiles = s // ts

    def body(a_ref, ap_ref, an_ref, g_ref, gp_ref, gn_ref, cw_ref, cb_ref, lw_ref, lb_ref, wp_ref, bp_ref,
             y_ref, cv_ref, ext_ref, sh_ref):
        i = pl.program_id(1)
        glu = lambda a, g: a * _sigmoid(g)
        _fill_ext(ext_ref, glu(a_ref[0], g_ref[0]), glu(ap_ref[0], gp_ref[0]), glu(an_ref[0], gn_ref[0]), i, n_tiles, ts)
        _fill_shifted(sh_ref, ext_ref, ts)
        acc = jnp.broadcast_to(cb_ref[...], (ts, dc))
        for j in range(CONV_WIDTH):
            acc = acc + cw_ref[j:j + 1, :] * _window(sh_ref, ext_ref, HALO - CONV_PAD + j, ts)
        y_ref[0] = acc
        mu = jnp.mean(acc, axis=-1, keepdims=True)
        yc = acc - mu
        var = jnp.mean(yc * yc, axis=-1, keepdims=True)
        yn = yc * lax.rsqrt(var + EPS) * lw_ref[...] + lb_ref[...]
        ys = yn * _sigmoid(yn)
        cv_ref[0] = jnp.dot(ys.astype(BF16), wp_ref[...], preferred_element_type=F32) + bp_ref[...]

    const = lambda a: pl.BlockSpec(a.shape, lambda b, i: (0,) * a.ndim)
    return pl.pallas_call(
        body, name="conv_forward", grid=(bsz, n_tiles),
        in_specs=_halo_specs(dc, ts, s) + _halo_specs(dc, ts, s)
        + [const(conv_w), const(conv_b), const(ln_w), const(ln_b), const(w_pw), const(b_pw)],
        out_specs=[pl.BlockSpec((1, ts, dc), lambda b, i: (b, i, 0))] * 2,
        out_shape=[jax.ShapeDtypeStruct((bsz, s, dc), F32)] * 2,
        scratch_shapes=[pltpu.VMEM((ts + 2 * HALO, dc), F32), pltpu.VMEM((7, ts + 2 * HALO, dc), F32)],
        compiler_params=_params(2),
    )(ga, ga, ga, gg, gg, gg, conv_w, conv_b, ln_w, ln_b, w_pw, b_pw)


def _outproj_loss(x, target, gate, o, za, cv, zc, w_out, tm):
    bsz, s, d = x.shape

    def body(x_ref, t_ref, gate_ref, o_ref, za_ref, cv_ref, zc_ref, w_ref,
             loss_ref, dh_ref, do_ref, dza_ref, dcv_ref, dzc_ref, dgate_ref, gw_ref):
        b, i = pl.program_id(0), pl.program_id(1)
        ov, cvv = o_ref[0], cv_ref[0]
        silu_a, dsilu_a = _silu_and_grad(za_ref[0])
        silu_c, dsilu_c = _silu_and_grad(zc_ref[0])
        mix = jnp.concatenate([ov * silu_a, cvv * silu_c], axis=1).astype(BF16)
        out = jnp.dot(mix, w_ref[...], preferred_element_type=F32)
        gate_v = gate_ref[0]
        err = x_ref[0] + gate_v * out - t_ref[0]
        dh = err * (1.0 / d)
        dh_ref[0] = dh
        dout = (dh * gate_v).astype(BF16)
        dmix = lax.dot_general(dout, w_ref[...], NT, preferred_element_type=F32)
        gw = lax.dot_general(mix, dout, TN, preferred_element_type=F32)
        dg = jnp.sum(dh * out, axis=0, keepdims=True)
        sq = jnp.sum(err * err)

        @pl.when(jnp.logical_and(b == 0, i == 0))
        def _():
            gw_ref[...] = gw

        @pl.when(jnp.logical_or(b > 0, i > 0))
        def _():
            gw_ref[...] += gw

        @pl.when(i == 0)
        def _():
            dgate_ref[0] = dg
            loss_ref[...] = jnp.zeros(loss_ref.shape, F32) + sq

        @pl.when(i > 0)
        def _():
            dgate_ref[0] += dg
            loss_ref[...] += sq

        dma, dmc = dmix[:, :D_ATTN], dmix[:, D_ATTN:]
        do_ref[0] = dma * silu_a
        dza_ref[0] = (dma * ov * dsilu_a).astype(BF16)
        dcv_ref[0] = dmc * silu_c
        dzc_ref[0] = (dmc * cvv * dsilu_c).astype(BF16)

    tok = lambda w: pl.BlockSpec((1, tm, w), lambda b, i: (b, i, 0))
    return pl.pallas_call(
        body, name="outproj_loss", grid=(bsz, s // tm),
        in_specs=[tok(d), tok(d), _bcast_spec(gate), tok(512), tok(512), tok(512), tok(512),
                  pl.BlockSpec(w_out.shape, lambda b, i: (0, 0))],
        out_specs=[pl.BlockSpec((1, 8, 128), lambda b, i: (b, 0, 0)), tok(d), tok(512), tok(512), tok(512), tok(512),
                   pl.BlockSpec((1, 1, d), lambda b, i: (b, 0, 0)), pl.BlockSpec((d, d), lambda b, i: (0, 0))],
        out_shape=[jax.ShapeDtypeStruct((bsz, 8, 128), F32), jax.ShapeDtypeStruct((bsz, s, d), F32),
                   jax.ShapeDtypeStruct((bsz, s, 512), F32), jax.ShapeDtypeStruct((bsz, s, 512), BF16),
                   jax.ShapeDtypeStruct((bsz, s, 512), F32), jax.ShapeDtypeStruct((bsz, s, 512), BF16),
                   jax.ShapeDtypeStruct((bsz, 1, d), F32), jax.ShapeDtypeStruct((d, d), F32)],
        compiler_params=_params(2),
    )(x, target, gate, o, za, cv, zc, w_out)


def _conv_token_backward(dcv, y, ln_w, ln_b, w_pw, tm):
    bsz, s, dc = dcv.shape

    def body(dcv_ref, y_ref, lw_ref, lb_ref, wp_ref, dy_ref, gwp_ref, st_ref):
        b, i = pl.program_id(0), pl.program_id(1)
        yv, dcvv = y_ref[0], dcv_ref[0]
        mu = jnp.mean(yv, axis=-1, keepdims=True)
        yc = yv - mu
        rstd = lax.rsqrt(jnp.mean(yc * yc, axis=-1, keepdims=True) + EPS)
        yhat = yc * rstd
        yn = yhat * lw_ref[...] + lb_ref[...]
        ys, dsilu = _silu_and_grad(yn)
        dcvb = dcvv.astype(BF16)
        gwp = lax.dot_general(ys.astype(BF16), dcvb, TN, preferred_element_type=F32)
        dys = lax.dot_general(dcvb, wp_ref[...], NT, preferred_element_type=F32)
        dyn = dys * dsilu
        dyhat = dyn * lw_ref[...]
        dy = rstd * (dyhat - jnp.mean(dyhat, axis=-1, keepdims=True)
                     - yhat * jnp.mean(dyhat * yhat, axis=-1, keepdims=True))
        dy_ref[0] = dy
        red = lambda v: jnp.sum(v, axis=0, keepdims=True)
        stats = jnp.concatenate([red(dcvv), red(dyn * yhat), red(dyn), red(dy), jnp.zeros((4, dc), F32)], axis=0)
        first = jnp.logical_and(b == 0, i == 0)

        @pl.when(first)
        def _():
            gwp_ref[...] = gwp
            st_ref[...] = stats

        @pl.when(jnp.logical_not(first))
        def _():
            gwp_ref[...] += gwp
            st_ref[...] += stats

    tok = pl.BlockSpec((1, tm, dc), lambda b, i: (b, i, 0))
    const = lambda a: pl.BlockSpec(a.shape, lambda b, i: (0,) * a.ndim)
    return pl.pallas_call(
        body, name="conv_token_backward", grid=(bsz, s // tm),
        in_specs=[tok, tok, const(ln_w), const(ln_b), const(w_pw)],
        out_specs=[tok, pl.BlockSpec((dc, dc), lambda b, i: (0, 0)), pl.BlockSpec((8, dc), lambda b, i: (0, 0))],
        out_shape=[jax.ShapeDtypeStruct((bsz, s, dc), F32), jax.ShapeDtypeStruct((dc, dc), F32),
                   jax.ShapeDtypeStruct((8, dc), F32)],
        compiler_params=_params(2),
    )(dcv, y, ln_w, ln_b, w_pw)


def _conv_backward(dy, ga, gg, conv_w, ts):
    bsz, s, dc = dy.shape
    n_tiles = s // ts

    def body(dy_ref, dyp_ref, dyn_ref, a_ref, ap_ref, an_ref, g_ref, gp_ref, gn_ref, cw_ref,
             da_ref, dg_ref, gcw_ref, dyext_ref, ugext_ref, dysh_ref, ugsh_ref):
        b, i = pl.program_id(0), pl.program_id(1)
        dyv = dy_ref[0]
        av, sg = a_ref[0], _sigmoid(g_ref[0])
        glu = lambda a, g: a * _sigmoid(g)
        _fill_ext(dyext_ref, dyv, dyp_ref[0], dyn_ref[0], i, n_tiles, ts)
        _fill_ext(ugext_ref, av * sg, glu(ap_ref[0], gp_ref[0]), glu(an_ref[0], gn_ref[0]), i, n_tiles, ts)
        _fill_shifted(dysh_ref, dyext_ref, ts)
        _fill_shifted(ugsh_ref, ugext_ref, ts)
        dug = jnp.zeros((ts, dc), F32)
        rows = []
        for j in range(CONV_WIDTH):
            dug = dug + cw_ref[j:j + 1, :] * _window(dysh_ref, dyext_ref, HALO + CONV_PAD - j, ts)
            rows.append(jnp.sum(dyv * _window(ugsh_ref, ugext_ref, HALO - CONV_PAD + j, ts), axis=0, keepdims=True))
        rows.append(jnp.zeros((1, dc), F32))
        gcw = jnp.concatenate(rows, axis=0)
        first = jnp.logical_and(b == 0, i == 0)

        @pl.when(first)
        def _():
            gcw_ref[...] = gcw

        @pl.when(jnp.logical_not(first))
        def _():
            gcw_ref[...] += gcw

        da_ref[0] = (dug * sg).astype(BF16)
        dg_ref[0] = (dug * av * sg * (1.0 - sg)).astype(BF16)

    tok = pl.BlockSpec((1, ts, dc), lambda b, i: (b, i, 0))
    return pl.pallas_call(
        body, name="conv_backward", grid=(bsz, n_tiles),
        in_specs=_halo_specs(dc, ts, s) + _halo_specs(dc, ts, s) + _halo_specs(dc, ts, s)
        + [pl.BlockSpec(conv_w.shape, lambda b, i: (0, 0))],
        out_specs=[tok, tok, pl.BlockSpec((32, dc), lambda b, i: (0, 0))],
        out_shape=[jax.ShapeDtypeStruct((bsz, s, dc), BF16), jax.ShapeDtypeStruct((bsz, s, dc), BF16),
                   jax.ShapeDtypeStruct((32, dc), F32)],
        scratch_shapes=[pltpu.VMEM((ts + 2 * HALO, dc), F32), pltpu.VMEM((ts + 2 * HALO, dc), F32),
                        pltpu.VMEM((7, ts + 2 * HALO, dc), F32), pltpu.VMEM((7, ts + 2 * HALO, dc), F32)],
        compiler_params=_params(2),
    )(dy, dy, dy, ga, ga, ga, gg, gg, gg, conv_w)


def _attn_backward(q, k2, v2, o, do, lse, tq, fused):
    bsz, s, _ = q.shape
    sk = k2.shape[2]
    scale = 1.0 / math.sqrt(HEAD_DIM)
    nq = s // tq
    total = bsz * N_KV * nq
    at_steps = [(0, True), (min(3, total - 1), True), (total // 2, True), (total - 1, False)]

    def body(*refs):
        (q_ref, k_ref, v_ref, o_ref, do_ref, lse_ref), (dq_ref, dk_ref, dv_ref), _ = _split_fused(refs, 6, 3, 0, fused)
        i = pl.program_id(2)
        step = (pl.program_id(0) * N_KV + pl.program_id(1)) * nq + i
        _run_phases(fused, step, at_steps, True)
        kk, vv = k_ref[0, 0], v_ref[0, 0]
        lo_mask = _lo_mask(tq)
        dk_acc = jnp.zeros((sk, 128), F32)
        dv_acc = jnp.zeros((sk, 128), F32)
        for j in range(2):
            cols = slice(128 * j, 128 * (j + 1))
            qp, dop, lsep = q_ref[0, :, cols], do_ref[0, :, cols], lse_ref[0, :, cols]
            dprod = dop * o_ref[0, :, cols]
            dqs = []
            for half in range(2):
                sel = lo_mask if half == 0 else jnp.logical_not(lo_mask)
                qs = jnp.where(sel, qp, jnp.zeros_like(qp))
                dos = jnp.where(sel, dop, 0.0).astype(BF16)
                lse_h = jnp.max(jnp.where(sel, lsep, -jnp.inf), axis=-1, keepdims=True)
                delta = jnp.sum(jnp.where(sel, dprod, 0.0), axis=-1, keepdims=True)
                sc = lax.dot_general(qs, kk, NT, preferred_element_type=F32)
                p = jnp.exp(sc - lse_h)
                dp = lax.dot_general(dos, vv, NT, preferred_element_type=F32)
                ds = (p * (dp - delta)).astype(BF16)
                dv_acc = dv_acc + lax.dot_general(p.astype(BF16), dos, TN, preferred_element_type=F32)
                dk_acc = dk_acc + lax.dot_general(ds, qs, TN, preferred_element_type=F32)
                dqs.append(jnp.dot(ds, kk, preferred_element_type=F32))
            dq_ref[0, :, cols] = jnp.where(lo_mask, dqs[0], dqs[1]) * scale

        @pl.when(i == 0)
        def _():
            dk_ref[0, 0] = dk_acc
            dv_ref[0, 0] = dv_acc

        @pl.when(i > 0)
        def _():
            dk_ref[0, 0] += dk_acc
            dv_ref[0, 0] += dv_acc

        _run_phases(fused, step, at_steps, False)

    q_spec = pl.BlockSpec((1, tq, 256), lambda b, g, i: (b, i, g))
    kv_spec = pl.BlockSpec((1, 1, sk, 128), lambda b, g, i: (b, g, 0, 0))
    return pl.pallas_call(
        body, name="attn_backward", grid=(bsz, N_KV, nq),
        in_specs=[q_spec, kv_spec, kv_spec, q_spec, q_spec, q_spec] + fused.in_specs,
        out_specs=[q_spec, kv_spec, kv_spec] + fused.out_specs,
        out_shape=[jax.ShapeDtypeStruct((bsz, s, 512), F32), jax.ShapeDtypeStruct((bsz, 2, sk, 128), F32),
                   jax.ShapeDtypeStruct((bsz, 2, sk, 128), F32)] + fused.out_shape,
        scratch_shapes=fused.scratch,
        compiler_params=_params(3),
    )(q, k2, v2, o, do, lse, *fused.arrs)


def _fold_heads(acc2_ref_val0, acc2_ref_val1, lo_mask):
    f0 = acc2_ref_val0 + pltpu.roll(acc2_ref_val0, HEAD_DIM, 1)
    f1 = acc2_ref_val1 + pltpu.roll(acc2_ref_val1, HEAD_DIM, 1)
    return jnp.where(lo_mask, f0, f1)


def _norm_backward(dn, pre, w, bd):
    rstd = lax.rsqrt(_seg_mean(pre * pre, bd) + EPS)
    xhat = pre * rstd
    dxhat = dn * w
    return rstd * (dxhat - xhat * _seg_mean(dxhat * xhat, bd)), dn * xhat


def _qkv_backward(qkv, dq, dk2, dv2, cos, sin, qnw, knw, bd512, bd128, ts, row0):
    bsz, s, _ = qkv.shape

    def body(p_ref, dq_ref, dk_ref, dv_ref, cos_ref, sin_ref, qnw_ref, knw_ref, bd512_ref, bd128_ref, d_ref, gw_ref):
        b, i = pl.program_id(0), pl.program_id(1)
        lo_mask = _lo_mask(ts)
        cos_t, sin_t = cos_ref[...], sin_ref[...]
        dqr = dq_ref[0]
        dqn = dqr * _tile_lanes(cos_t, 4) + _partner(dqr * _tile_lanes(sin_t, 4))
        dqp, gq = _norm_backward(dqn, p_ref[0, :, 0:512], qnw_ref[...], bd512_ref[...])
        dkr = _fold_heads(dk_ref[0, 0], dk_ref[0, 1], lo_mask)
        dkn = dkr * cos_t + _partner(dkr * sin_t)
        dkp, gk = _norm_backward(dkn, p_ref[0, :, 512:640], knw_ref[...], bd128_ref[...])
        dvp = _fold_heads(dv_ref[0, 0], dv_ref[0, 1], lo_mask)
        d_ref[0] = jnp.concatenate([dqp, dkp, dvp], axis=1).astype(BF16)
        gk512 = jnp.concatenate([jnp.sum(gk, axis=0, keepdims=True), jnp.zeros((1, 384), F32)], axis=1)
        rows = jnp.concatenate([jnp.sum(gq, axis=0, keepdims=True), gk512, jnp.zeros((6, 512), F32)], axis=0)
        first = jnp.logical_and(b == 0, i == 0)

        @pl.when(first)
        def _():
            gw_ref[...] = rows

        @pl.when(jnp.logical_not(first))
        def _():
            gw_ref[...] += rows

    const = lambda a: pl.BlockSpec(a.shape, lambda b, i: (0,) * a.ndim)
    kv_spec = pl.BlockSpec((1, 2, ts, 128), lambda b, i: (b, 0, i + row0 // ts, 0))
    return pl.pallas_call(
        body, name="qkv_backward", grid=(bsz, s // ts),
        in_specs=[pl.BlockSpec((1, ts, 768), lambda b, i: (b, i, 0)), pl.BlockSpec((1, ts, 512), lambda b, i: (b, i, 0)),
                  kv_spec, kv_spec, pl.BlockSpec((ts, 128), lambda b, i: (i, 0)),
                  pl.BlockSpec((ts, 128), lambda b, i: (i, 0)), const(qnw), const(knw), const(bd512), const(bd128)],
        out_specs=[pl.BlockSpec((1, ts, 768), lambda b, i: (b, i, 0)), pl.BlockSpec((8, 512), lambda b, i: (0, 0))],
        out_shape=[jax.ShapeDtypeStruct((bsz, s, 768), BF16), jax.ShapeDtypeStruct((8, 512), F32)],
        compiler_params=_params(2),
    )(qkv, dq, dk2, dv2, cos, sin, qnw, knw, bd512, bd128)


def _ctx_kv_backward(pc, dk2, dv2, knw, bd128):
    bsz, cl, _ = pc.shape

    def body(p_ref, dk_ref, dv_ref, knw_ref, bd128_ref, d_ref, gw_ref):
        b = pl.program_id(0)
        lo_mask = _lo_mask(cl)
        dkn = _fold_heads(dk_ref[0, 0], dk_ref[0, 1], lo_mask)
        dkp, gk = _norm_backward(dkn, p_ref[0, :, 0:128], knw_ref[...], bd128_ref[...])
        dvp = _fold_heads(dv_ref[0, 0], dv_ref[0, 1], lo_mask)
        d_ref[0] = jnp.concatenate([dkp, dvp], axis=1).astype(BF16)
        rows = jnp.concatenate([jnp.sum(gk, axis=0, keepdims=True), jnp.zeros((7, 128), F32)], axis=0)

        @pl.when(b == 0)
        def _():
            gw_ref[...] = rows

        @pl.when(b > 0)
        def _():
            gw_ref[...] += rows

    const = lambda a: pl.BlockSpec(a.shape, lambda b: (0,) * a.ndim)
    kv_spec = pl.BlockSpec((1, 2, cl, 128), lambda b: (b, 0, 0, 0))
    return pl.pallas_call(
        body, name="ctx_kv_backward", grid=(bsz,),
        in_specs=[pl.BlockSpec((1, cl, 256), lambda b: (b, 0, 0)), kv_spec, kv_spec, const(knw), const(bd128)],
        out_specs=[pl.BlockSpec((1, cl, 256), lambda b: (b, 0, 0)), pl.BlockSpec((8, 128), lambda b: (0, 0))],
        out_shape=[jax.ShapeDtypeStruct((bsz, cl, 256), BF16), jax.ShapeDtypeStruct((8, 128), F32)],
        compiler_params=_params(1),
    )(pc, dk2, dv2, knw, bd128)


def _weight_grad(parts, u, init, tm, name):
    bsz, s, d = u.shape
    n_p = len(parts)
    nrows = sum(hi - lo for _, lo, hi in parts)

    def body(*refs):
        p_refs, u_ref = refs[:n_p], refs[n_p]
        gi_ref = refs[n_p + 1] if init is not None else None
        gw_ref = refs[-1]
        first = jnp.logical_and(pl.program_id(0) == 0, pl.program_id(1) == 0)
        dp = jnp.concatenate([r[0, :, lo:hi] for r, (_, lo, hi) in zip(p_refs, parts)], axis=1)
        gw = lax.dot_general(dp, u_ref[0], TN, preferred_element_type=F32)

        @pl.when(first)
        def _():
            gw_ref[...] = gw
            if init is not None:
                gw_ref[KV_LO:KV_HI, :] += gi_ref[...]

        @pl.when(jnp.logical_not(first))
        def _():
            gw_ref[...] += gw

    tok = lambda w: pl.BlockSpec((1, tm, w), lambda b, i: (b, i, 0))
    in_specs = [tok(a.shape[2]) for a, _, _ in parts] + [tok(d)]
    args = [a for a, _, _ in parts] + [u]
    if init is not None:
        in_specs.append(pl.BlockSpec(init.shape, lambda b, i: (0, 0)))
        args.append(init)
    return pl.pallas_call(
        body, name=name, grid=(bsz, s // tm), in_specs=in_specs,
        out_specs=pl.BlockSpec((nrows, d), lambda b, i: (0, 0)), out_shape=jax.ShapeDtypeStruct((nrows, d), F32),
        compiler_params=_params(2),
    )(*args)


def _inproj_backward(dps, x, dh, scale1p, norm_w, w_t, tm, name, fused=None):
    bsz, s, d = x.shape
    n_p = len(dps)
    shared = scale1p.shape[0] == 1
    with_dx = dh is not None
    n_in = n_p + (2 if with_dx else 1) + 3
    n_out = 3 if with_dx else 2
    total = bsz * (s // tm)
    at_steps = [(0, True), (max(1, total // 8), True), (max(2, (3 * total) // 4), True), (total - 1, False)]

    def body(*refs):
        ins, outs, _ = _split_fused(refs, n_in, n_out, 0, fused)
        dp_refs, x_ref = ins[:n_p], ins[n_p]
        dh_ref = ins[n_p + 1] if with_dx else None
        sc_ref, nw_ref, w_ref = ins[-3:]
        mod_ref, gnw_ref = outs[-2:]
        b, i = pl.program_id(0), pl.program_id(1)
        step = b * (s // tm) + i
        _run_phases(fused, step, at_steps, True)
        first = jnp.logical_and(b == 0, i == 0)
        dp = dp_refs[0][0] if n_p == 1 else jnp.concatenate([r[0] for r in dp_refs], axis=1)
        du = jnp.dot(dp, w_ref[...], preferred_element_type=F32)
        xv = x_ref[0]
        rstd = lax.rsqrt(jnp.mean(xv * xv, axis=-1, keepdims=True) + EPS)
        xhat = xv * rstd
        nw, sc = nw_ref[...], sc_ref[0]
        red = lambda v: jnp.sum(v, axis=0, keepdims=True)
        mod_rows = jnp.concatenate([red(du), red(du * (xhat * nw)), jnp.zeros((6, d), F32)], axis=0)
        gnw_rows = jnp.concatenate([red(du * sc * xhat), jnp.zeros((7, d), F32)], axis=0)
        mod_first = first if shared else i == 0

        @pl.when(mod_first)
        def _():
            mod_ref[0] = mod_rows

        @pl.when(jnp.logical_not(mod_first))
        def _():
            mod_ref[0] += mod_rows

        @pl.when(first)
        def _():
            gnw_ref[...] = gnw_rows

        @pl.when(jnp.logical_not(first))
        def _():
            gnw_ref[...] += gnw_rows

        if with_dx:
            dxhat = du * (nw * sc)
            outs[0][0] = dh_ref[0] + rstd * (dxhat - xhat * jnp.mean(dxhat * xhat, axis=-1, keepdims=True))
        _run_phases(fused, step, at_steps, False)

    tok = lambda w: pl.BlockSpec((1, tm, w), lambda b, i: (b, i, 0))
    in_specs = [tok(p.shape[2]) for p in dps] + [tok(d)]
    args = list(dps) + [x]
    if with_dx:
        in_specs.append(tok(d))
        args.append(dh)
    in_specs += [_bcast_spec(scale1p), pl.BlockSpec((1, d), lambda b, i: (0, 0)),
                 pl.BlockSpec(w_t.shape, lambda b, i: (0, 0))]
    args += [scale1p, norm_w, w_t]
    bm = scale1p.shape[0]
    mod_spec = pl.BlockSpec((1, 8, d), (lambda b, i: (0, 0, 0)) if shared else (lambda b, i: (b, 0, 0)))
    out_specs = [mod_spec, pl.BlockSpec((8, d), lambda b, i: (0, 0))]
    out_shape = [jax.ShapeDtypeStruct((bm, 8, d), F32), jax.ShapeDtypeStruct((8, d), F32)]
    if with_dx:
        out_specs.insert(0, tok(d))
        out_shape.insert(0, jax.ShapeDtypeStruct((bsz, s, d), F32))
    scratch = []
    if fused is not None:
        in_specs += fused.in_specs
        args += fused.arrs
        out_specs += fused.out_specs
        out_shape += fused.out_shape
        scratch = fused.scratch
    res = pl.pallas_call(
        body, name=name, grid=(bsz, s // tm), in_specs=in_specs, out_specs=out_specs, out_shape=out_shape,
        scratch_shapes=scratch, compiler_params=_params(2),
    )(*args)
    return list(res) if with_dx else [None] + list(res)


def _adamw(w, g, m, v, name):
    r, cdim = w.shape
    tr = next((t for t in (256, 176) if r % t == 0 and r > t), r)

    def body(w_ref, g_ref, m_ref, v_ref, d_ref, nm_ref, nv_ref):
        gv = g_ref[...]
        mn = ADAM_B1 * m_ref[...] + (1.0 - ADAM_B1) * gv
        vn = ADAM_B2 * v_ref[...] + (1.0 - ADAM_B2) * (gv * gv)
        m_hat = mn / (1.0 - ADAM_B1 ** ADAM_STEP)
        v_hat = vn / (1.0 - ADAM_B2 ** ADAM_STEP)
        d_ref[...] = -ADAM_LR * (m_hat / (jnp.sqrt(v_hat) + ADAM_EPS) + ADAM_WD * w_ref[...])
        nm_ref[...] = mn
        nv_ref[...] = vn

    spec = pl.BlockSpec((tr, cdim), lambda i: (i, 0))
    return pl.pallas_call(
        body, name=name, grid=(r // tr,), in_specs=[spec] * 4, out_specs=[spec] * 3,
        out_shape=[jax.ShapeDtypeStruct((r, cdim), F32)] * 3, compiler_params=_params(1),
    )(w, g, m, v)


def _rope_tables(s):
    rows = s // GRID_W
    freqs = np.float32(ROPE_THETA) ** (-np.arange(0, ROPE_AXIS_DIM, 2, dtype=np.float32) / np.float32(ROPE_AXIS_DIM))
    ang_r = np.arange(rows, dtype=np.float32)[:, None] * freqs[None, :]
    ang_c = np.arange(GRID_W, dtype=np.float32)[:, None] * freqs[None, :]
    zr, zc = np.zeros_like(ang_r), np.zeros_like(ang_c)

    def table(by_row, by_col):
        r = jnp.asarray(np.tile(np.concatenate(by_row + [zr, zr], axis=1), (1, 2)), dtype=F32)
        c = jnp.asarray(np.tile(np.concatenate([zc, zc] + by_col, axis=1), (1, 2)), dtype=F32)
        return jnp.repeat(r, GRID_W, axis=0) + jnp.tile(c, (rows, 1))

    return (table([np.cos(ang_r)] * 2, [np.cos(ang_c)] * 2),
            table([-np.sin(ang_r), np.sin(ang_r)], [-np.sin(ang_c), np.sin(ang_c)]))


def _pack_rows(parts, rows):
    flat = jnp.concatenate([p.reshape(-1) for p in parts])
    return jnp.pad(flat, (0, rows * D_MODEL - flat.shape[0])).reshape(rows, D_MODEL)


def kernel(x, c, ctx, c_ctx, w_mod, b_mod, norm_w, w_in, q_norm_w, k_norm_w, conv_w, conv_b, conv_ln_w, conv_ln_b, w_pw, b_pw, w_out, loss_target, m_c_ctx, m_w_mod, m_b_mod, m_norm_w, m_w_in, m_q_norm_w, m_k_norm_w, m_conv_w, m_conv_b, m_conv_ln_w, m_conv_ln_b, m_w_pw, m_b_pw, m_w_out, v_c_ctx, v_w_mod, v_b_mod, v_norm_w, v_w_in, v_q_norm_w, v_k_norm_w, v_conv_w, v_conv_b, v_conv_ln_w, v_conv_ln_b, v_w_pw, v_b_pw, v_w_out):
    bsz, s, d = x.shape
    cl = ctx.shape[1]
    xi, yi, ci = lax.axis_index("x"), lax.axis_index("y"), lax.axis_index("c")
    chip = 2 * xi + yi
    dev = 2 * chip + ci
    ncol_mod = w_mod.shape[2]

    w_in_t_loc = w_in[0].T.astype(BF16)
    b_cols = lax.dynamic_slice(b_mod, (0, chip * ncol_mod), (1, ncol_mod))
    sc_rows, mod_g, g_in = _front(jnp.pad(c, ((0, 8 - bsz), (0, 0))), jnp.pad(c_ctx[None, :], ((0, 15), (0, 0))),
                                  w_mod[0], b_cols, w_in_t_loc)
    w_in_t = g_in.reshape(D_IN, d)
    mod_all = mod_g.transpose(1, 0, 2).reshape(80, 3 * d)
    mod_loc = lax.dynamic_slice(mod_all, (8 * dev, 0), (bsz, 3 * d))
    shift, scale1p, gate = mod_loc[:, None, :d], 1.0 + mod_loc[:, None, d:2 * d], mod_loc[:, None, 2 * d:]
    shift_c, scale1p_c = mod_all[64:65, :d][None], 1.0 + mod_all[64:65, d:2 * d][None]

    cos, sin = _rope_tables(s)
    qnw512 = jnp.tile(q_norm_w, (1, 8))
    knw128 = jnp.tile(k_norm_w, (1, 2))
    bd512 = jnp.kron(jnp.eye(8, dtype=F32), jnp.ones((HEAD_DIM, HEAD_DIM), F32)).astype(BF16)
    bd128 = bd512[:128, :128]

    u, p_qkv, p_za, p_ga, p_gg, p_zc = _norm_inproj(x, shift, scale1p, norm_w, w_in_t, SPLITS, 512, "norm_inproj")
    uc, pc_kv = _norm_inproj(ctx, shift_c, scale1p_c, norm_w, w_in_t[KV_LO:KV_HI], ((0, 256),), cl, "ctx_norm_inproj")
    q, k2x, v2x = _qkv_prep(p_qkv, cos, sin, qnw512, knw128, bd512, bd128, 256, cl)
    k2, v2 = _ctx_kv_prep(pc_kv, knw128, bd128, k2x, v2x)
    conv_w_loc = jnp.pad(conv_w[0], ((0, 1), (0, 0)))
    o, lse, g_out, g_pw, g_cw = _attn_forward(
        q, k2, v2, 256, _ChipGather([w_out[0].astype(BF16), w_pw[0].astype(BF16), conv_w_loc]))
    w_out_f = g_out.reshape(d, d)
    w_pw_f = g_pw.reshape(D_CONV, D_CONV)
    conv_w_f = g_cw.transpose(1, 0, 2).reshape(32, D_CONV)
    y, cv = _conv_forward(p_ga, p_gg, conv_w_f, conv_b, conv_ln_w, conv_ln_b, w_pw_f, b_pw, 256)
    loss_part, dh, do, dza, dcv, dzc, dgate, gw_out = _outproj_loss(
        x, loss_target, gate, o, p_za, cv, p_zc, w_out_f, 256)

    all_chips, half_rows = (0, 1, 2, 3), D_IN // 2
    dy, gw_pw, conv_stats = _conv_token_backward(dcv, y, conv_ln_w, conv_ln_b, w_pw_f, 256)
    da, dg, gcw = _conv_backward(dy, p_ga, p_gg, conv_w_f, 256)
    tw = min(1024, s)
    gw_hi = _weight_grad([(da, half_rows - SPLITS[2][0], 512), (dg, 0, 512), (dzc, 0, 512)], u, None, tw,
                         "grad_in_rows_hi")
    dq, dk2, dv2, r_out, r_pw, r_hi = _attn_backward(
        q, k2, v2, o, do, lse, 256, _FusedReduce([(gw_out, all_chips), (gw_pw, all_chips), (gw_hi, (2, 3))]))
    dqkv, qk_stats = _qkv_backward(p_qkv, dq, dk2, dv2, cos, sin, qnw512, knw128, bd512, bd128, 256, cl)
    dpc, kc_stats = _ctx_kv_backward(pc_kv, dk2, dv2, knw128, bd128)
    gw_ctx = _weight_grad([(dpc, 0, 256)], uc, None, cl, "grad_in_rows_ctx")
    gw_lo = _weight_grad([(dqkv, 0, 768), (dza, 0, 512), (da, 0, half_rows - SPLITS[2][0])], u, gw_ctx, tw,
                         "grad_in_rows_lo")
    _, modc, gnw_c = _inproj_backward([dpc], ctx, None, scale1p_c, norm_w, w_in_t[KV_LO:KV_HI], cl,
                                      "ctx_inproj_backward")
    grad_x, modx, gnw_x, r_lo = _inproj_backward(
        [dqkv, dza, da, dg, dzc], x, dh, scale1p, norm_w, w_in_t, 256, "inproj_backward",
        _FusedReduce([(gw_lo, (0, 1))]))
    g_w_out, g_w_pw = r_out.reshape(d // 4, d), r_pw.reshape(D_CONV // 4, D_CONV)
    g_w_in_t = jnp.where(chip < 2, r_lo, r_hi).reshape(D_IN // 4, d)

    dmod_loc = jnp.concatenate([modx[:, 0, :], modx[:, 1, :], dgate[:, 0, :]], axis=1)
    gq = qk_stats[0].reshape(8, HEAD_DIM).sum(axis=0)
    gk = (qk_stats[1, :128] + kc_stats[0]).reshape(2, HEAD_DIM).sum(axis=0)
    packed = _pack_rows([dmod_loc, dmod_loc.sum(axis=0), gnw_x[0] + gnw_c[0], modc[0, 0], modc[0, 1], gq, gk,
                         conv_stats[0], conv_stats[1], conv_stats[2], conv_stats[3], gcw,
                         jnp.sum(loss_part[:, 0, 0])[None]], 32)
    gathered, total = _all_gather8(packed, "small_grads", reduce=True)
    flat = total.reshape(-1)
    offs = [0]

    def take(nelem):
        lo = offs[0]
        offs[0] = lo + nelem
        return flat[lo:lo + nelem]

    take(bsz * 3 * d)
    g_b_mod_x = take(3 * d)
    g_norm_w = take(d)
    dshift_c, dscale_c = take(d), take(d)
    g_qnw, g_knw = take(HEAD_DIM), take(HEAD_DIM)
    g_b_pw, g_ln_w, g_ln_b, g_conv_b = take(D_CONV), take(D_CONV), take(D_CONV), take(D_CONV)
    g_conv_w_full = take(32 * D_CONV).reshape(32, D_CONV)
    loss = take(1)[0] * (0.5 / d)

    dmod_c = jnp.concatenate([dshift_c, dscale_c, jnp.zeros((d,), F32)])
    g_b_mod = (g_b_mod_x + dmod_c)[None, :]
    dmod_rows = jnp.pad(gathered[:, :6, :].reshape(8, bsz, 3 * d), ((0, 0), (0, 8 - bsz), (0, 0))).reshape(64, 3 * d)
    dmod_all = jnp.concatenate([dmod_rows, jnp.pad(dmod_c[None, :], ((0, 15), (0, 0)))], axis=0)
    dmod_cols = lax.dynamic_slice(dmod_all, (0, chip * ncol_mod), (80, ncol_mod))
    g_w_mod, gcc_part = _mod_backward(sc_rows, dmod_cols, w_mod[0])
    gcc_all = _all_gather8(gcc_part, "c_ctx_grad_gather")
    dsilu_ctx = gcc_all[0, 0] + gcc_all[2, 0] + gcc_all[4, 0] + gcc_all[6, 0]
    sg = _sigmoid(c_ctx)
    g_c_ctx = dsilu_ctx * (sg * (1.0 + c_ctx * (1.0 - sg)))

    g_w_in = g_w_in_t.T
    g_conv_w = lax.dynamic_slice(g_conv_w_full, (0, chip * 128), (CONV_WIDTH, 128))

    grads = {
        "c_ctx": g_c_ctx, "w_mod": g_w_mod[None], "b_mod": g_b_mod, "norm_w": g_norm_w[None], "w_in": g_w_in[None],
        "q_norm_w": g_qnw[None], "k_norm_w": g_knw[None], "conv_w": g_conv_w[None], "conv_b": g_conv_b[None],
        "conv_ln_w": g_ln_w[None], "conv_ln_b": g_ln_b[None], "w_pw": g_w_pw[None], "b_pw": g_b_pw[None],
        "w_out": g_w_out[None],
    }
    weights = {
        "c_ctx": (c_ctx, m_c_ctx, v_c_ctx), "w_mod": (w_mod, m_w_mod, v_w_mod), "b_mod": (b_mod, m_b_mod, v_b_mod),
        "norm_w": (norm_w, m_norm_w, v_norm_w), "w_in": (w_in, m_w_in, v_w_in),
        "q_norm_w": (q_norm_w, m_q_norm_w, v_q_norm_w), "k_norm_w": (k_norm_w, m_k_norm_w, v_k_norm_w),
        "conv_w": (conv_w, m_conv_w, v_conv_w), "conv_b": (conv_b, m_conv_b, v_conv_b),
        "conv_ln_w": (conv_ln_w, m_conv_ln_w, v_conv_ln_w), "conv_ln_b": (conv_ln_b, m_conv_ln_b, v_conv_ln_b),
        "w_pw": (w_pw, m_w_pw, v_w_pw), "b_pw": (b_pw, m_b_pw, v_b_pw), "w_out": (w_out, m_w_out, v_w_out),
    }
    names = list(weights)
    deltas, new_ms, new_vs = [], [], []
    for n in names:
        w, m, v = weights[n]
        shape = w.shape
        if n == "w_in":
            dl, nm, nv = _adamw(w[0].T, g_w_in_t, m[0].T, v[0].T, "adamw_" + n)
            deltas.append(dl.T[None])
            new_ms.append(nm.T[None])
            new_vs.append(nv.T[None])
            continue
        two_d = (1, shape[0]) if w.ndim == 1 else (shape[-2] if w.ndim == 3 else 1, shape[-1])
        dl, nm, nv = _adamw(w.reshape(two_d), grads[n].reshape(two_d), m.reshape(two_d), v.reshape(two_d), "adamw_" + n)
        deltas.append(dl.reshape(shape))
        new_ms.append(nm.reshape(shape))
        new_vs.append(nv.reshape(shape))
        grads[n] = grads[n].reshape(shape)

    return (loss, grad_x, *[grads[n] for n in names], *deltas, *new_ms, *new_vs)
```

```python
import functools
import math

import jax
import jax.numpy as jnp
import numpy as np
from jax import lax
from jax.experimental import pallas as pl
from jax.experimental.pallas import tpu as pltpu

F32 = jnp.float32
BF16 = jnp.bfloat16
MESH = pl.DeviceIdType.MESH

D_MODEL = 1024
D_ATTN = 512
D_CONV = 512
HEAD_DIM = 64
N_KV = 2
GRID_W = 64
ROPE_AXIS_DIM = 32
ROPE_THETA = 10000.0
CONV_WIDTH = 31
CONV_PAD = 15
HALO = 16
CONV_ROWS = 32
EPS = 1e-6
SPLITS = ((0, 768), (768, 1280), (1280, 1792), (1792, 2304), (2304, 2816))
D_IN = 2816
KV_LO, KV_HI = 512, 768

ADAM_LR = 0.001
ADAM_B1 = 0.9
ADAM_B2 = 0.999
ADAM_EPS = 1e-08
ADAM_WD = 0.01
ADAM_STEP = 10

VMEM_LIMIT = 56 * 1024 * 1024

NT = (((1,), (1,)), ((), ()))
TN = (((0,), (0,)), ((), ()))


def _params(n_axes=0, **kw):
    if n_axes:
        kw["dimension_semantics"] = ("arbitrary",) * n_axes
    return pltpu.CompilerParams(vmem_limit_bytes=VMEM_LIMIT, **kw)


def _sigmoid(x):
    return 1.0 / (1.0 + jnp.exp(-x))


def _silu_and_grad(z):
    s = _sigmoid(z)
    return z * s, s * (1.0 + z * (1.0 - s))


def _seg_mean(v, ones_bd):
    hi = v.astype(BF16)
    lo = (v - hi.astype(F32)).astype(BF16)
    s = jnp.dot(hi, ones_bd, preferred_element_type=F32) + jnp.dot(lo, ones_bd, preferred_element_type=F32)
    return s * (1.0 / HEAD_DIM)


def _partner(v):
    n = v.shape[1]
    lane = lax.broadcasted_iota(jnp.int32, (v.shape[0], 128), 1)
    first = (lane % 32) < 16
    parts = []
    for k in range(n // 128):
        ch = v[:, 128 * k:128 * (k + 1)]
        parts.append(jnp.where(first, pltpu.roll(ch, 112, 1), pltpu.roll(ch, 16, 1)))
    return parts[0] if len(parts) == 1 else jnp.concatenate(parts, axis=1)


def _tile_lanes(t, reps):
    return t if reps == 1 else jnp.concatenate([t] * reps, axis=1)


def _lo_mask(rows):
    return lax.broadcasted_iota(jnp.int32, (rows, 128), 1) < HEAD_DIM


def _gather8_in_vmem(x_ref, out_ref, send_sems, recv_sems, local_sem):
    x, y, c = lax.axis_index("x"), lax.axis_index("y"), lax.axis_index("c")
    me, sibling = (x, y, c), (x, y, 1 - c)
    chips = [(1 - x, y), (x, 1 - y), (1 - x, 1 - y)]

    def slot(px, py, pc):
        return out_ref.at[4 * px + 2 * py + pc]

    def copy(k, block, to, src=None):
        return pltpu.make_async_remote_copy(
            src_ref=slot(*block) if src is None else src, dst_ref=slot(*block),
            send_sem=send_sems.at[k], recv_sem=recv_sems.at[k], device_id=to, device_id_type=MESH)

    mine = pltpu.make_async_copy(x_ref, slot(*me), local_sem)
    mine.start()
    first = [copy(0, me, sibling, src=x_ref)]
    first += [copy(1 + j, me, (*chip, c), src=x_ref) for j, chip in enumerate(chips)]
    for cp in first:
        cp.start()
    passed = [copy(4 + j, (*chip, c), sibling) for j, chip in enumerate(chips)]
    for j, chip in enumerate(chips):
        copy(1 + j, (*chip, c), me).wait_recv()
        passed[j].start()
    copy(0, sibling, me).wait_recv()
    for j, chip in enumerate(chips):
        copy(4 + j, (*chip, 1 - c), me).wait_recv()
    for cp in first + passed:
        cp.wait_send()
    mine.wait()


def _all_gather8(x_local, name, reduce=False):
    r, cdim = x_local.shape

    def body(x_ref, out_ref, *rest):
        if reduce:
            sum_ref, send_sems, recv_sems, local_sem = rest
        else:
            send_sems, recv_sems, local_sem = rest
        _gather8_in_vmem(x_ref, out_ref, send_sems, recv_sems, local_sem)
        if reduce:
            acc = out_ref[0]
            for d in range(1, 8):
                acc = acc + out_ref[d]
            sum_ref[...] = acc

    out_shape = [jax.ShapeDtypeStruct((8, r, cdim), x_local.dtype)]
    out_specs = [pl.BlockSpec(memory_space=pltpu.VMEM)]
    if reduce:
        out_shape.append(jax.ShapeDtypeStruct((r, cdim), x_local.dtype))
        out_specs.append(pl.BlockSpec(memory_space=pltpu.VMEM))
    res = pl.pallas_call(
        body, name=name, out_shape=out_shape,
        in_specs=[pl.BlockSpec(memory_space=pltpu.VMEM)], out_specs=out_specs,
        scratch_shapes=[pltpu.SemaphoreType.DMA((7,)), pltpu.SemaphoreType.DMA((7,)), pltpu.SemaphoreType.DMA],
        compiler_params=_params(),
    )(x_local)
    return res if reduce else res[0]


class _ChipGather:
    def __init__(self, arrs):
        self.arrs = list(arrs)
        n = self.n = len(self.arrs)
        self.in_specs = [pl.BlockSpec(memory_space=pl.ANY)] * n
        self.out_shape = [jax.ShapeDtypeStruct((4,) + a.shape, a.dtype) for a in self.arrs]
        self.out_specs = [pl.BlockSpec(memory_space=pl.ANY)] * n
        self.scratch = [pltpu.SemaphoreType.DMA((6 * n,)), pltpu.SemaphoreType.DMA((6 * n,)),
                        pltpu.SemaphoreType.DMA((n,))]
        self.phases = [self.start, self.forward, self.finish]

    def bind(self, ins, outs, scratch):
        self.ins, self.outs = ins, outs
        self.send_sems, self.recv_sems, self.local_sems = scratch
        self.x, self.y, self.c = lax.axis_index("x"), lax.axis_index("y"), lax.axis_index("c")
        self.chips = [(1 - self.x, self.y), (self.x, 1 - self.y), (1 - self.x, 1 - self.y)]
        self.mychip = 2 * self.x + self.y

    def _copy(self, a, k, chip_idx, cc, to, src=None):
        h = self.arrs[a].shape[0] // 2
        dst = self.outs[a].at[chip_idx, pl.ds(cc * h, h)]
        return pltpu.make_async_remote_copy(
            src_ref=dst if src is None else src, dst_ref=dst, send_sem=self.send_sems.at[6 * a + k],
            recv_sem=self.recv_sems.at[6 * a + k], device_id=to, device_id_type=MESH)

    def _local(self, a):
        return pltpu.make_async_copy(self.ins[a], self.outs[a].at[self.mychip], self.local_sems.at[a])

    def _first(self, a, j):
        h = self.arrs[a].shape[0] // 2
        return self._copy(a, j, self.mychip, self.c, (*self.chips[j], self.c), src=self.ins[a].at[pl.ds(self.c * h, h)])

    def _passed(self, a, j):
        cx, cy = self.chips[j]
        return self._copy(a, 3 + j, 2 * cx + cy, self.c, (self.x, self.y, 1 - self.c))

    def start(self):
        for a in range(self.n):
            self._local(a).start()
            for j in range(3):
                self._first(a, j).start()

    def forward(self):
        for a in range(self.n):
            for j, (cx, cy) in enumerate(self.chips):
                self._copy(a, j, 2 * cx + cy, self.c, (self.x, self.y, self.c)).wait_recv()
                self._passed(a, j).start()

    def finish(self):
        for a in range(self.n):
            for j, (cx, cy) in enumerate(self.chips):
                self._copy(a, 3 + j, 2 * cx + cy, 1 - self.c, (self.x, self.y, self.c)).wait_recv()
        for a in range(self.n):
            for j in range(3):
                self._first(a, j).wait_send()
                self._passed(a, j).wait_send()
            self._local(a).wait()


class _FusedReduce:
    def __init__(self, pieces):
        self.owners = [tuple(o) for _, o in pieces]
        self.arrs = [g.reshape(len(o), 2, g.shape[0] // (2 * len(o)), g.shape[1]) for g, o in pieces]
        n = self.n = len(pieces)
        hc = self.hc = [(v.shape[2], v.shape[3]) for v in self.arrs]
        nts = [len(o) for o in self.owners]
        self.base = [sum(nts[:p]) for p in range(n)]
        anyspec = pl.BlockSpec(memory_space=pl.ANY)
        self.in_specs = [anyspec] * n
        self.out_shape = [jax.ShapeDtypeStruct((2,) + s, F32) for s in hc]
        self.out_specs = [anyspec] * n
        self.scratch = [pltpu.VMEM((nt,) + s, F32) for nt, s in zip(nts, hc)]
        self.scratch += [pltpu.VMEM((nt,) + s, F32) for nt, s in zip(nts, hc)]
        self.scratch += [pltpu.VMEM(s, F32) for s in hc]
        self.scratch += [pltpu.VMEM((nt,) + s, BF16) for nt, s in zip(nts, hc)]
        self.scratch += [pltpu.VMEM((3,) + s, BF16) for s in hc]
        self.scratch += [pltpu.VMEM(s, F32) for s in hc]
        tot = sum(nts)
        self.scratch += [pltpu.SemaphoreType.DMA((tot,)), pltpu.SemaphoreType.DMA((tot,)),
                         pltpu.SemaphoreType.DMA((tot,)), pltpu.SemaphoreType.DMA((3 * n,)),
                         pltpu.SemaphoreType.DMA((n,)), pltpu.SemaphoreType.DMA((n,)), pltpu.SemaphoreType.DMA((n,)),
                         pltpu.SemaphoreType.DMA((tot,))]
        self.phases = [self.start, self.exchange, self.combine, self.finish]

    def bind(self, ins, outs, scratch):
        n = self.n
        self.g, self.out = ins, outs
        self.va, self.recv_a, self.own = scratch[:n], scratch[n:2 * n], scratch[2 * n:3 * n]
        self.tsend, self.recv_b, self.fin = scratch[3 * n:4 * n], scratch[4 * n:5 * n], scratch[5 * n:6 * n]
        self.sa, self.ra, self.sb, self.rb, self.sc, self.rc, self.lc, self.la = scratch[6 * n:]
        self.x, self.y, self.c = lax.axis_index("x"), lax.axis_index("y"), lax.axis_index("c")
        self.mychip = 2 * self.x + self.y
        self.sibling = (self.x, self.y, 1 - self.c)

    def _copy_a(self, p, t):
        k = self.base[p] + t
        return pltpu.make_async_remote_copy(
            src_ref=self.g[p].at[t, 1 - self.c], dst_ref=self.recv_a[p].at[t], send_sem=self.sa.at[k],
            recv_sem=self.ra.at[k], device_id=self.sibling, device_id_type=MESH)

    def _fetch(self, p, t):
        return pltpu.make_async_copy(self.g[p].at[t, self.c], self.va[p].at[t], self.la.at[self.base[p] + t])

    def _slot(self, owner):
        rel = jnp.bitwise_xor(self.mychip, owner)
        return jnp.where(rel == 2, 0, jnp.where(rel == 1, 1, 2))

    def _copy_b(self, p, t, slot):
        owner = self.owners[p][t]
        return pltpu.make_async_remote_copy(
            src_ref=self.tsend[p].at[t], dst_ref=self.recv_b[p].at[slot], send_sem=self.sb.at[self.base[p] + t],
            recv_sem=self.rb.at[3 * p + slot], device_id=(owner // 2, owner % 2, self.c), device_id_type=MESH)

    def _copy_c(self, p, half):
        return pltpu.make_async_remote_copy(
            src_ref=self.fin[p], dst_ref=self.out[p].at[half], send_sem=self.sc.at[p], recv_sem=self.rc.at[p],
            device_id=self.sibling, device_id_type=MESH)

    def _local_c(self, p):
        return pltpu.make_async_copy(self.fin[p], self.out[p].at[self.c], self.lc.at[p])

    def start(self):
        for p in range(self.n):
            for t in range(len(self.owners[p])):
                self._copy_a(p, t).start()
                self._fetch(p, t).start()

    def exchange(self):
        for p in range(self.n):
            for t, owner in enumerate(self.owners[p]):
                self._copy_a(p, t).wait_recv()
                self._fetch(p, t).wait()
                mine = self.mychip == owner

                @pl.when(mine)
                def _():
                    self.own[p][...] = self.va[p][t] + self.recv_a[p][t]

                @pl.when(jnp.logical_not(mine))
                def _():
                    self.tsend[p][t] = (self.va[p][t] + self.recv_a[p][t]).astype(BF16)
                    self._copy_b(p, t, self._slot(owner)).start()

    def combine(self):
        for p in range(self.n):
            for t, owner in enumerate(self.owners[p]):
                @pl.when(self.mychip == owner)
                def _():
                    acc = self.own[p][...]
                    for j in range(3):
                        self._copy_b(p, t, j).wait_recv()
                        acc = acc + self.recv_b[p][j].astype(F32)
                    self.fin[p][...] = acc
                    self._local_c(p).start()
                    self._copy_c(p, self.c).start()

    def finish(self):
        for p in range(self.n):
            for t, owner in enumerate(self.owners[p]):
                self._copy_a(p, t).wait_send()
                mine = self.mychip == owner

                @pl.when(mine)
                def _():
                    self._copy_c(p, 1 - self.c).wait_recv()
                    self._copy_c(p, self.c).wait_send()
                    self._local_c(p).wait()

                @pl.when(jnp.logical_not(mine))
                def _():
                    self._copy_b(p, t, self._slot(owner)).wait_send()


def _split_fused(refs, n_in, n_out, n_scr, fused):
    if fused is None:
        return refs[:n_in], refs[n_in:n_in + n_out], refs[n_in + n_out:]
    fi, fo = len(fused.in_specs), len(fused.out_specs)
    ins, rest = refs[:n_in], refs[n_in:]
    f_ins, rest = rest[:fi], rest[fi:]
    outs, rest = rest[:n_out], rest[n_out:]
    f_outs, rest = rest[:fo], rest[fo:]
    scr, f_scr = rest[:n_scr], rest[n_scr:]
    fused.bind(f_ins, f_outs, f_scr)
    return ins, outs, scr


def _run_phases(fused, step, at_steps, before):
    if fused is None:
        return
    for phase, (at, first) in zip(fused.phases, at_steps):
        if first == before:
            pl.when(step == at)(phase)


def _front(c_pad, c_ctx_rows, w_mod, b_cols, w_in_t_loc):
    ncol = w_mod.shape[1]
    gather = _ChipGather([w_in_t_loc])

    def body(c_ref, cctx_ref, w_ref, b_ref, win_ref, sc_ref, modg_ref, wing_ref,
             call_ref, ag_send, ag_recv, ag_local, m_send, m_recv, *g_scr):
        gather.bind([win_ref], [wing_ref], g_scr)
        gather.start()
        _gather8_in_vmem(c_ref, call_ref, ag_send, ag_recv, ag_local)
        x, y, c = lax.axis_index("x"), lax.axis_index("y"), lax.axis_index("c")
        chips = [(1 - x, y), (x, 1 - y), (1 - x, 1 - y)]
        mychip = 2 * x + y
        rows = jnp.concatenate([call_ref[dv] for dv in range(8)] + [cctx_ref[...]], axis=0)
        sc = rows * _sigmoid(rows)
        sc_ref[...] = sc
        modg_ref[mychip] = jnp.dot(sc, w_ref[...], preferred_element_type=F32,
                                   precision=lax.Precision.HIGHEST) + b_ref[...]

        def mcopy(j, chip_idx, to):
            return pltpu.make_async_remote_copy(
                src_ref=modg_ref.at[chip_idx], dst_ref=modg_ref.at[chip_idx], send_sem=m_send.at[j],
                recv_sem=m_recv.at[j], device_id=to, device_id_type=MESH)

        sends = [mcopy(j, mychip, (*chip, c)) for j, chip in enumerate(chips)]
        for cp in sends:
            cp.start()
        for j, (cx, cy) in enumerate(chips):
            mcopy(j, 2 * cx + cy, (x, y, c)).wait_recv()
        gather.forward()
        gather.finish()
        for cp in sends:
            cp.wait_send()

    vm = pl.BlockSpec(memory_space=pltpu.VMEM)
    return pl.pallas_call(
        body, name="front_exchange",
        out_shape=[jax.ShapeDtypeStruct((80, D_MODEL), F32), jax.ShapeDtypeStruct((4, 80, ncol), F32)] + gather.out_shape,
        in_specs=[vm, vm, vm, vm] + gather.in_specs, out_specs=[vm, vm] + gather.out_specs,
        scratch_shapes=[pltpu.VMEM((8, 8, D_MODEL), F32), pltpu.SemaphoreType.DMA((7,)), pltpu.SemaphoreType.DMA((7,)),
                        pltpu.SemaphoreType.DMA, pltpu.SemaphoreType.DMA((3,)), pltpu.SemaphoreType.DMA((3,))]
        + gather.scratch,
        compiler_params=_params(),
    )(c_pad, c_ctx_rows, w_mod, b_cols, w_in_t_loc)


def _mod_backward(sc_rows, dmod_cols, w_mod):
    rows, ncol = dmod_cols.shape

    def body(sc_ref, dm_ref, w_ref, gw_ref, gcc_ref):
        gw_ref[...] = lax.dot_general(sc_ref[...], dm_ref[...], TN, preferred_element_type=F32,
                                      precision=lax.Precision.HIGHEST)
        gcc_ref[...] = lax.dot_general(dm_ref[64:72, :], w_ref[...], NT, preferred_element_type=F32,
                                       precision=lax.Precision.HIGHEST)

    return pl.pallas_call(
        body, name="mod_backward",
        out_shape=[jax.ShapeDtypeStruct((D_MODEL, ncol), F32), jax.ShapeDtypeStruct((8, D_MODEL), F32)],
        compiler_params=_params(),
    )(sc_rows, dmod_cols, w_mod)


def _bcast_spec(arr):
    if arr.shape[0] == 1:
        return pl.BlockSpec((1, 1, arr.shape[2]), lambda b, i: (0, 0, 0))
    return pl.BlockSpec((1, 1, arr.shape[2]), lambda b, i: (b, 0, 0))


def _norm_inproj(x, shift, scale1p, norm_w, w_t, splits, tm, name):
    bsz, s, d = x.shape

    def body(x_ref, sh_ref, sc_ref, nw_ref, w_ref, u_ref, *out_refs):
        xv = x_ref[0]
        rstd = lax.rsqrt(jnp.mean(xv * xv, axis=-1, keepdims=True) + EPS)
        u = (xv * rstd * nw_ref[...]) * sc_ref[0] + sh_ref[0]
        ub = u.astype(BF16)
        u_ref[0] = ub
        for (lo, hi), o_ref in zip(splits, out_refs):
            o_ref[0] = lax.dot_general(ub, w_ref[lo:hi, :], NT, preferred_element_type=F32)

    tok = lambda w: pl.BlockSpec((1, tm, w), lambda b, i: (b, i, 0))
    return pl.pallas_call(
        body, name=name, grid=(bsz, s // tm),
        in_specs=[tok(d), _bcast_spec(shift), _bcast_spec(scale1p), pl.BlockSpec((1, d), lambda b, i: (0, 0)),
                  pl.BlockSpec(w_t.shape, lambda b, i: (0, 0))],
        out_specs=[tok(d)] + [tok(hi - lo) for lo, hi in splits],
        out_shape=[jax.ShapeDtypeStruct((bsz, s, d), BF16)]
        + [jax.ShapeDtypeStruct((bsz, s, hi - lo), F32) for lo, hi in splits],
        compiler_params=_params(2),
    )(x, shift, scale1p, norm_w, w_t)


def _dup_heads(kv, lo_mask):
    r = pltpu.roll(kv, HEAD_DIM, 1)
    return jnp.where(lo_mask, kv, r), jnp.where(lo_mask, r, kv)


def _qkv_prep(qkv, cos, sin, qnw, knw, bd512, bd128, ts, row0):
    bsz, s, _ = qkv.shape

    def body(p_ref, cos_ref, sin_ref, qnw_ref, knw_ref, bd512_ref, bd128_ref, q_ref, k_ref, v_ref):
        lo_mask = _lo_mask(ts)
        cos_t, sin_t = cos_ref[...], sin_ref[...]
        qp = p_ref[0, :, 0:512]
        qn = qp * lax.rsqrt(_seg_mean(qp * qp, bd512_ref[...]) + EPS) * qnw_ref[...]
        qr = qn * _tile_lanes(cos_t, 4) + _partner(qn) * _tile_lanes(sin_t, 4)
        q_ref[0] = (qr * (1.0 / math.sqrt(HEAD_DIM))).astype(BF16)
        kp = p_ref[0, :, 512:640]
        kn = kp * lax.rsqrt(_seg_mean(kp * kp, bd128_ref[...]) + EPS) * knw_ref[...]
        kr = kn * cos_t + _partner(kn) * sin_t
        k0, k1 = _dup_heads(kr, lo_mask)
        k_ref[0, 0] = k0.astype(BF16)
        k_ref[0, 1] = k1.astype(BF16)
        v0, v1 = _dup_heads(p_ref[0, :, 640:768], lo_mask)
        v_ref[0, 0] = v0.astype(BF16)
        v_ref[0, 1] = v1.astype(BF16)

    const = lambda a: pl.BlockSpec(a.shape, lambda b, i: (0,) * a.ndim)
    kv_spec = pl.BlockSpec((1, 2, ts, 128), lambda b, i: (b, 0, i + row0 // ts, 0))
    return pl.pallas_call(
        body, name="qkv_prep", grid=(bsz, s // ts),
        in_specs=[pl.BlockSpec((1, ts, 768), lambda b, i: (b, i, 0)),
                  pl.BlockSpec((ts, 128), lambda b, i: (i, 0)), pl.BlockSpec((ts, 128), lambda b, i: (i, 0)),
                  const(qnw), const(knw), const(bd512), const(bd128)],
        out_specs=[pl.BlockSpec((1, ts, 512), lambda b, i: (b, i, 0)), kv_spec, kv_spec],
        out_shape=[jax.ShapeDtypeStruct((bsz, s, 512), BF16), jax.ShapeDtypeStruct((bsz, 2, row0 + s, 128), BF16),
                   jax.ShapeDtypeStruct((bsz, 2, row0 + s, 128), BF16)],
        compiler_params=_params(2),
    )(qkv, cos, sin, qnw, knw, bd512, bd128)


def _ctx_kv_prep(pc, knw, bd128, k2, v2):
    bsz, cl, _ = pc.shape

    def body(p_ref, knw_ref, bd128_ref, k_in, v_in, k_ref, v_ref):
        lo_mask = _lo_mask(cl)
        kp = p_ref[0, :, 0:128]
        kn = kp * lax.rsqrt(_seg_mean(kp * kp, bd128_ref[...]) + EPS) * knw_ref[...]
        k0, k1 = _dup_heads(kn, lo_mask)
        k_ref[0, 0] = k0.astype(BF16)
        k_ref[0, 1] = k1.astype(BF16)
        v0, v1 = _dup_heads(p_ref[0, :, 128:256], lo_mask)
        v_ref[0, 0] = v0.astype(BF16)
        v_ref[0, 1] = v1.astype(BF16)

    const = lambda a: pl.BlockSpec(a.shape, lambda b: (0,) * a.ndim)
    kv_spec = pl.BlockSpec((1, 2, cl, 128), lambda b: (b, 0, 0, 0))
    return pl.pallas_call(
        body, name="ctx_kv_prep", grid=(bsz,),
        in_specs=[pl.BlockSpec((1, cl, 256), lambda b: (b, 0, 0)), const(knw), const(bd128),
                  pl.BlockSpec(memory_space=pl.ANY), pl.BlockSpec(memory_space=pl.ANY)],
        out_specs=[kv_spec, kv_spec],
        out_shape=[jax.ShapeDtypeStruct(k2.shape, BF16), jax.ShapeDtypeStruct(v2.shape, BF16)],
        input_output_aliases={3: 0, 4: 1},
        compiler_params=_params(1),
    )(pc, knw, bd128, k2, v2)


def _attn_forward(q, k2, v2, tq, fused):
    bsz, s, _ = q.shape
    sk = k2.shape[2]
    nq = s // tq
    total = bsz * N_KV * nq
    at_steps = [(0, True), (total // 4, True), (total - 1, False)]

    def body(*refs):
        (q_ref, k_ref, v_ref), (o_ref, lse_ref), _ = _split_fused(refs, 3, 2, 0, fused)
        step = (pl.program_id(0) * N_KV + pl.program_id(1)) * nq + pl.program_id(2)
        _run_phases(fused, step, at_steps, True)
        kk = k_ref[0, 0]
        lo_mask = _lo_mask(tq)
        vv = v_ref[0, 0]
        v1 = jnp.where(_lo_mask(sk), vv, jnp.ones_like(vv))
        for j in range(2):
            qp = q_ref[0, :, 128 * j:128 * (j + 1)]
            outs, lses = [], []
            for half in range(2):
                sel = lo_mask if half == 0 else jnp.logical_not(lo_mask)
                qs = jnp.where(sel, qp, jnp.zeros_like(qp))
                sc = lax.dot_general(qs, kk, NT, preferred_element_type=F32)
                m = jnp.max(sc, axis=-1, keepdims=True)
                o = jnp.dot(jnp.exp(sc - m).astype(BF16), v1, preferred_element_type=F32)
                l = pltpu.roll(o, HEAD_DIM, 1)
                outs.append(o / l)
                lses.append(m + jnp.log(l))
            o_ref[0, :, 128 * j:128 * (j + 1)] = jnp.where(lo_mask, outs[0], pltpu.roll(outs[1], HEAD_DIM, 1))
            lse_ref[0, :, 128 * j:128 * (j + 1)] = jnp.where(lo_mask, lses[0], pltpu.roll(lses[1], HEAD_DIM, 1))
        _run_phases(fused, step, at_steps, False)

    q_spec = pl.BlockSpec((1, tq, 256), lambda b, g, i: (b, i, g))
    kv_spec = pl.BlockSpec((1, 1, sk, 128), lambda b, g, i: (b, g, 0, 0))
    return pl.pallas_call(
        body, name="attn_forward", grid=(bsz, N_KV, nq),
        in_specs=[q_spec, kv_spec, kv_spec] + fused.in_specs, out_specs=[q_spec, q_spec] + fused.out_specs,
        out_shape=[jax.ShapeDtypeStruct((bsz, s, 512), F32)] * 2 + fused.out_shape,
        scratch_shapes=fused.scratch,
        compiler_params=_params(3),
    )(q, k2, v2, *fused.arrs)


def _halo_specs(width, ts, s):
    r = ts // HALO
    last = s // HALO - 1
    return [pl.BlockSpec((1, ts, width), lambda b, i: (b, i, 0)),
            pl.BlockSpec((1, HALO, width), lambda b, i: (b, jnp.maximum(i * r - 1, 0), 0)),
            pl.BlockSpec((1, HALO, width), lambda b, i: (b, jnp.minimum((i + 1) * r, last), 0))]


def _fill_ext(ext_ref, cur, prev, nxt, i, n_tiles, ts):
    ext_ref[0:HALO, :] = jnp.where(i > 0, prev, jnp.zeros_like(prev))
    ext_ref[HALO:HALO + ts, :] = cur
    ext_ref[HALO + ts:2 * HALO + ts, :] = jnp.where(i < n_tiles - 1, nxt, jnp.zeros_like(nxt))


def _fill_shifted(sh_ref, ext_ref, ts):
    n = ts + 2 * HALO - 8
    for r in range(1, 8):
        sh_ref[r - 1, 0:n, :] = ext_ref[pl.ds(r, n), :]


def _window(sh_ref, ext_ref, off, rows, r0=0):
    q, r = divmod(off, 8)
    if r == 0:
        return ext_ref[pl.ds(r0 + off, rows), :]
    return sh_ref[r - 1, pl.ds(r0 + 8 * q, rows), :]


def _conv_forward(ga, gg, conv_w, conv_b, ln_w, ln_b, w_pw, b_pw, ts):
    bsz, s, dc = ga.shape
    n_tiles = s // ts

    def body(a_ref, ap_ref, an_ref, g_ref, gp_ref, gn_ref, cw_ref, cb_ref, lw_ref, lb_ref, wp_ref, bp_ref,
             y_ref, cv_ref, ext_ref, sh_ref):
        i = pl.program_id(1)
        glu = lambda a, g: a * _sigmoid(g)
        _fill_ext(ext_ref, glu(a_ref[0], g_ref[0]), glu(ap_ref[0], gp_ref[0]), glu(an_ref[0], gn_ref[0]), i, n_tiles, ts)
        _fill_shifted(sh_ref, ext_ref, ts)

        def row_block(r, carry):
            r0 = pl.multiple_of(r * CONV_ROWS, CONV_ROWS)
            blk = jnp.broadcast_to(cb_ref[...], (CONV_ROWS, dc))
            for j in range(CONV_WIDTH):
                blk = blk + cw_ref[j:j + 1, :] * _window(sh_ref, ext_ref, HALO - CONV_PAD + j, CONV_ROWS, r0)
            y_ref[0, pl.ds(r0, CONV_ROWS), :] = blk
            return carry

        lax.fori_loop(0, ts // CONV_ROWS, row_block, 0)
        acc = y_ref[0]
        mu = jnp.mean(acc, axis=-1, keepdims=True)
        yc = acc - mu
        var = jnp.mean(yc * yc, axis=-1, keepdims=True)
        yn = yc * lax.rsqrt(var + EPS) * lw_ref[...] + lb_ref[...]
        ys = yn * _sigmoid(yn)
        cv_ref[0] = jnp.dot(ys.astype(BF16), wp_ref[...], preferred_element_type=F32) + bp_ref[...]

    const = lambda a: pl.BlockSpec(a.shape, lambda b, i: (0,) * a.ndim)
    return pl.pallas_call(
        body, name="conv_forward", grid=(bsz, n_tiles),
        in_specs=_halo_specs(dc, ts, s) + _halo_specs(dc, ts, s)
        + [const(conv_w), const(conv_b), const(ln_w), const(ln_b), const(w_pw), const(b_pw)],
        out_specs=[pl.BlockSpec((1, ts, dc), lambda b, i: (b, i, 0))] * 2,
        out_shape=[jax.ShapeDtypeStruct((bsz, s, dc), F32)] * 2,
        scratch_shapes=[pltpu.VMEM((ts + 2 * HALO, dc), F32), pltpu.VMEM((7, ts + 2 * HALO, dc), F32)],
        compiler_params=_params(2),
    )(ga, ga, ga, gg, gg, gg, conv_w, conv_b, ln_w, ln_b, w_pw, b_pw)


def _outproj_loss(x, target, gate, o, za, cv, zc, w_out, tm):
    bsz, s, d = x.shape

    def body(x_ref, t_ref, gate_ref, o_ref, za_ref, cv_ref, zc_ref, w_ref,
             loss_ref, dh_ref, do_ref, dza_ref, dcv_ref, dzc_ref, dgate_ref, gw_ref):
        b, i = pl.program_id(0), pl.program_id(1)
        ov, cvv = o_ref[0], cv_ref[0]
        silu_a, dsilu_a = _silu_and_grad(za_ref[0])
        silu_c, dsilu_c = _silu_and_grad(zc_ref[0])
        mix = jnp.concatenate([ov * silu_a, cvv * silu_c], axis=1).astype(BF16)
        out = jnp.dot(mix, w_ref[...], preferred_element_type=F32)
        gate_v = gate_ref[0]
        err = x_ref[0] + gate_v * out - t_ref[0]
        dh = err * (1.0 / d)
        dh_ref[0] = dh
        dout = (dh * gate_v).astype(BF16)
        dmix = lax.dot_general(dout, w_ref[...], NT, preferred_element_type=F32)
        gw = lax.dot_general(mix, dout, TN, preferred_element_type=F32)
        dg = jnp.sum(dh * out, axis=0, keepdims=True)
        sq = jnp.sum(err * err)

        @pl.when(jnp.logical_and(b == 0, i == 0))
        def _():
            gw_ref[...] = gw

        @pl.when(jnp.logical_or(b > 0, i > 0))
        def _():
            gw_ref[...] += gw

        @pl.when(i == 0)
        def _():
            dgate_ref[0] = dg
            loss_ref[...] = jnp.zeros(loss_ref.shape, F32) + sq

        @pl.when(i > 0)
        def _():
            dgate_ref[0] += dg
            loss_ref[...] += sq

        dma, dmc = dmix[:, :D_ATTN], dmix[:, D_ATTN:]
        do_ref[0] = dma * silu_a
        dza_ref[0] = (dma * ov * dsilu_a).astype(BF16)
        dcv_ref[0] = dmc * silu_c
        dzc_ref[0] = (dmc * cvv * dsilu_c).astype(BF16)

    tok = lambda w: pl.BlockSpec((1, tm, w), lambda b, i: (b, i, 0))
    return pl.pallas_call(
        body, name="outproj_loss", grid=(bsz, s // tm),
        in_specs=[tok(d), tok(d), _bcast_spec(gate), tok(512), tok(512), tok(512), tok(512),
                  pl.BlockSpec(w_out.shape, lambda b, i: (0, 0))],
        out_specs=[pl.BlockSpec((1, 8, 128), lambda b, i: (b, 0, 0)), tok(d), tok(512), tok(512), tok(512), tok(512),
                   pl.BlockSpec((1, 1, d), lambda b, i: (b, 0, 0)), pl.BlockSpec((d, d), lambda b, i: (0, 0))],
        out_shape=[jax.ShapeDtypeStruct((bsz, 8, 128), F32), jax.ShapeDtypeStruct((bsz, s, d), F32),
                   jax.ShapeDtypeStruct((bsz, s, 512), F32), jax.ShapeDtypeStruct((bsz, s, 512), BF16),
                   jax.ShapeDtypeStruct((bsz, s, 512), F32), jax.ShapeDtypeStruct((bsz, s, 512), BF16),
                   jax.ShapeDtypeStruct((bsz, 1, d), F32), jax.ShapeDtypeStruct((d, d), F32)],
        compiler_params=_params(2),
    )(x, target, gate, o, za, cv, zc, w_out)


def _conv_token_backward(dcv, y, ln_w, ln_b, w_pw, tm):
    bsz, s, dc = dcv.shape

    def body(dcv_ref, y_ref, lw_ref, lb_ref, wp_ref, dy_ref, gwp_ref, st_ref):
        b, i = pl.program_id(0), pl.program_id(1)
        yv, dcvv = y_ref[0], dcv_ref[0]
        mu = jnp.mean(yv, axis=-1, keepdims=True)
        yc = yv - mu
        rstd = lax.rsqrt(jnp.mean(yc * yc, axis=-1, keepdims=True) + EPS)
        yhat = yc * rstd
        yn = yhat * lw_ref[...] + lb_ref[...]
        ys, dsilu = _silu_and_grad(yn)
        dcvb = dcvv.astype(BF16)
        gwp = lax.dot_general(ys.astype(BF16), dcvb, TN, preferred_element_type=F32)
        dys = lax.dot_general(dcvb, wp_ref[...], NT, preferred_element_type=F32)
        dyn = dys * dsilu
        dyhat = dyn * lw_ref[...]
        dy = rstd * (dyhat - jnp.mean(dyhat, axis=-1, keepdims=True)
                     - yhat * jnp.mean(dyhat * yhat, axis=-1, keepdims=True))
        dy_ref[0] = dy
        red = lambda v: jnp.sum(v, axis=0, keepdims=True)
        stats = jnp.concatenate([red(dcvv), red(dyn * yhat), red(dyn), red(dy), jnp.zeros((4, dc), F32)], axis=0)
        first = jnp.logical_and(b == 0, i == 0)

        @pl.when(first)
        def _():
            gwp_ref[...] = gwp
            st_ref[...] = stats

        @pl.when(jnp.logical_not(first))
        def _():
            gwp_ref[...] += gwp
            st_ref[...] += stats

    tok = pl.BlockSpec((1, tm, dc), lambda b, i: (b, i, 0))
    const = lambda a: pl.BlockSpec(a.shape, lambda b, i: (0,) * a.ndim)
    return pl.pallas_call(
        body, name="conv_token_backward", grid=(bsz, s // tm),
        in_specs=[tok, tok, const(ln_w), const(ln_b), const(w_pw)],
        out_specs=[tok, pl.BlockSpec((dc, dc), lambda b, i: (0, 0)), pl.BlockSpec((8, dc), lambda b, i: (0, 0))],
        out_shape=[jax.ShapeDtypeStruct((bsz, s, dc), F32), jax.ShapeDtypeStruct((dc, dc), F32),
                   jax.ShapeDtypeStruct((8, dc), F32)],
        compiler_params=_params(2),
    )(dcv, y, ln_w, ln_b, w_pw)


def _conv_backward(dy, ga, gg, conv_w, ts):
    bsz, s, dc = dy.shape
    n_tiles = s // ts

    def body(dy_ref, dyp_ref, dyn_ref, a_ref, ap_ref, an_ref, g_ref, gp_ref, gn_ref, cw_ref,
             da_ref, dg_ref, gcw_ref, dyext_ref, ugext_ref, dysh_ref, ugsh_ref, dug_ref, gacc_ref):
        b, i = pl.program_id(0), pl.program_id(1)
        av, sg = a_ref[0], _sigmoid(g_ref[0])
        glu = lambda a, g: a * _sigmoid(g)
        _fill_ext(dyext_ref, dy_ref[0], dyp_ref[0], dyn_ref[0], i, n_tiles, ts)
        _fill_ext(ugext_ref, av * sg, glu(ap_ref[0], gp_ref[0]), glu(an_ref[0], gn_ref[0]), i, n_tiles, ts)
        _fill_shifted(dysh_ref, dyext_ref, ts)
        _fill_shifted(ugsh_ref, ugext_ref, ts)
        gacc_ref[...] = jnp.zeros(gacc_ref.shape, F32)

        def row_block(r, carry):
            r0 = pl.multiple_of(r * CONV_ROWS, CONV_ROWS)
            dyb = dy_ref[0, pl.ds(r0, CONV_ROWS), :]
            acc = jnp.zeros((CONV_ROWS, dc), F32)
            for j in range(CONV_WIDTH):
                acc = acc + cw_ref[j:j + 1, :] * _window(dysh_ref, dyext_ref, HALO + CONV_PAD - j, CONV_ROWS, r0)
                prod = dyb * _window(ugsh_ref, ugext_ref, HALO - CONV_PAD + j, CONV_ROWS, r0)
                part = prod[0:8, :]
                for k in range(8, CONV_ROWS, 8):
                    part = part + prod[k:k + 8, :]
                gacc_ref[j] += part
            dug_ref[pl.ds(r0, CONV_ROWS), :] = acc
            return carry

        lax.fori_loop(0, ts // CONV_ROWS, row_block, 0)
        dug = dug_ref[...]
        gcw = jnp.sum(gacc_ref[...], axis=1)
        first = jnp.logical_and(b == 0, i == 0)

        @pl.when(first)
        def _():
            gcw_ref[...] = gcw

        @pl.when(jnp.logical_not(first))
        def _():
            gcw_ref[...] += gcw

        da_ref[0] = (dug * sg).astype(BF16)
        dg_ref[0] = (dug * av * sg * (1.0 - sg)).astype(BF16)

    tok = pl.BlockSpec((1, ts, dc), lambda b, i: (b, i, 0))
    return pl.pallas_call(
        body, name="conv_backward", grid=(bsz, n_tiles),
        in_specs=_halo_specs(dc, ts, s) + _halo_specs(dc, ts, s) + _halo_specs(dc, ts, s)
        + [pl.BlockSpec(conv_w.shape, lambda b, i: (0, 0))],
        out_specs=[tok, tok, pl.BlockSpec((32, dc), lambda b, i: (0, 0))],
        out_shape=[jax.ShapeDtypeStruct((bsz, s, dc), BF16), jax.ShapeDtypeStruct((bsz, s, dc), BF16),
                   jax.ShapeDtypeStruct((32, dc), F32)],
        scratch_shapes=[pltpu.VMEM((ts + 2 * HALO, dc), F32), pltpu.VMEM((ts + 2 * HALO, dc), F32),
                        pltpu.VMEM((7, ts + 2 * HALO, dc), F32), pltpu.VMEM((7, ts + 2 * HALO, dc), F32),
                        pltpu.VMEM((ts, dc), F32), pltpu.VMEM((32, 8, dc), F32)],
        compiler_params=_params(2),
    )(dy, dy, dy, ga, ga, ga, gg, gg, gg, conv_w)


def _attn_backward(q, k2, v2, o, do, lse, tq, fused):
    bsz, s, _ = q.shape
    sk = k2.shape[2]
    scale = 1.0 / math.sqrt(HEAD_DIM)
    nq = s // tq
    total = bsz * N_KV * nq
    at_steps = [(0, True), (min(3, total - 1), True), (total // 2, True), (total - 1, False)]

    def body(*refs):
        (q_ref, k_ref, v_ref, o_ref, do_ref, lse_ref), (dq_ref, dk_ref, dv_ref), _ = _split_fused(refs, 6, 3, 0, fused)
        i = pl.program_id(2)
        step = (pl.program_id(0) * N_KV + pl.program_id(1)) * nq + i
        _run_phases(fused, step, at_steps, True)
        kk, vv = k_ref[0, 0], v_ref[0, 0]
        lo_mask = _lo_mask(tq)
        dk_acc = jnp.zeros((sk, 128), F32)
        dv_acc = jnp.zeros((sk, 128), F32)
        for j in range(2):
            cols = slice(128 * j, 128 * (j + 1))
            qp, dop, lsep = q_ref[0, :, cols], do_ref[0, :, cols], lse_ref[0, :, cols]
            dprod = dop * o_ref[0, :, cols]
            dqs = []
            for half in range(2):
                sel = lo_mask if half == 0 else jnp.logical_not(lo_mask)
                qs = jnp.where(sel, qp, jnp.zeros_like(qp))
                dos = jnp.where(sel, dop, 0.0).astype(BF16)
                lse_h = jnp.max(jnp.where(sel, lsep, -jnp.inf), axis=-1, keepdims=True)
                delta = jnp.sum(jnp.where(sel, dprod, 0.0), axis=-1, keepdims=True)
                sc = lax.dot_general(qs, kk, NT, preferred_element_type=F32)
                p = jnp.exp(sc - lse_h)
                dp = lax.dot_general(dos, vv, NT, preferred_element_type=F32)
                ds = (p * (dp - delta)).astype(BF16)
                dv_acc = dv_acc + lax.dot_general(p.astype(BF16), dos, TN, preferred_element_type=F32)
                dk_acc = dk_acc + lax.dot_general(ds, qs, TN, preferred_element_type=F32)
                dqs.append(jnp.dot(ds, kk, preferred_element_type=F32))
            dq_ref[0, :, cols] = jnp.where(lo_mask, dqs[0], dqs[1]) * scale

        @pl.when(i == 0)
        def _():
            dk_ref[0, 0] = dk_acc
            dv_ref[0, 0] = dv_acc

        @pl.when(i > 0)
        def _():
            dk_ref[0, 0] += dk_acc
            dv_ref[0, 0] += dv_acc

        _run_phases(fused, step, at_steps, False)

    q_spec = pl.BlockSpec((1, tq, 256), lambda b, g, i: (b, i, g))
    kv_spec = pl.BlockSpec((1, 1, sk, 128), lambda b, g, i: (b, g, 0, 0))
    return pl.pallas_call(
        body, name="attn_backward", grid=(bsz, N_KV, nq),
        in_specs=[q_spec, kv_spec, kv_spec, q_spec, q_spec, q_spec] + fused.in_specs,
        out_specs=[q_spec, kv_spec, kv_spec] + fused.out_specs,
        out_shape=[jax.ShapeDtypeStruct((bsz, s, 512), F32), jax.ShapeDtypeStruct((bsz, 2, sk, 128), F32),
                   jax.ShapeDtypeStruct((bsz, 2, sk, 128), F32)] + fused.out_shape,
        scratch_shapes=fused.scratch,
        compiler_params=_params(3),
    )(q, k2, v2, o, do, lse, *fused.arrs)


def _fold_heads(acc2_ref_val0, acc2_ref_val1, lo_mask):
    f0 = acc2_ref_val0 + pltpu.roll(acc2_ref_val0, HEAD_DIM, 1)
    f1 = acc2_ref_val1 + pltpu.roll(acc2_ref_val1, HEAD_DIM, 1)
    return jnp.where(lo_mask, f0, f1)


def _norm_backward(dn, pre, w, bd):
    rstd = lax.rsqrt(_seg_mean(pre * pre, bd) + EPS)
    xhat = pre * rstd
    dxhat = dn * w
    return rstd * (dxhat - xhat * _seg_mean(dxhat * xhat, bd)), dn * xhat


def _qkv_backward(qkv, dq, dk2, dv2, cos, sin, qnw, knw, bd512, bd128, ts, row0):
    bsz, s, _ = qkv.shape

    def body(p_ref, dq_ref, dk_ref, dv_ref, cos_ref, sin_ref, qnw_ref, knw_ref, bd512_ref, bd128_ref, d_ref, gw_ref):
        b, i = pl.program_id(0), pl.program_id(1)
        lo_mask = _lo_mask(ts)
        cos_t, sin_t = cos_ref[...], sin_ref[...]
        dqr = dq_ref[0]
        dqn = dqr * _tile_lanes(cos_t, 4) + _partner(dqr * _tile_lanes(sin_t, 4))
        dqp, gq = _norm_backward(dqn, p_ref[0, :, 0:512], qnw_ref[...], bd512_ref[...])
        dkr = _fold_heads(dk_ref[0, 0], dk_ref[0, 1], lo_mask)
        dkn = dkr * cos_t + _partner(dkr * sin_t)
        dkp, gk = _norm_backward(dkn, p_ref[0, :, 512:640], knw_ref[...], bd128_ref[...])
        dvp = _fold_heads(dv_ref[0, 0], dv_ref[0, 1], lo_mask)
        d_ref[0] = jnp.concatenate([dqp, dkp, dvp], axis=1).astype(BF16)
        gk512 = jnp.concatenate([jnp.sum(gk, axis=0, keepdims=True), jnp.zeros((1, 384), F32)], axis=1)
        rows = jnp.concatenate([jnp.sum(gq, axis=0, keepdims=True), gk512, jnp.zeros((6, 512), F32)], axis=0)
        first = jnp.logical_and(b == 0, i == 0)

        @pl.when(first)
        def _():
            gw_ref[...] = rows

        @pl.when(jnp.logical_not(first))
        def _():
            gw_ref[...] += rows

    const = lambda a: pl.BlockSpec(a.shape, lambda b, i: (0,) * a.ndim)
    kv_spec = pl.BlockSpec((1, 2, ts, 128), lambda b, i: (b, 0, i + row0 // ts, 0))
    return pl.pallas_call(
        body, name="qkv_backward", grid=(bsz, s // ts),
        in_specs=[pl.BlockSpec((1, ts, 768), lambda b, i: (b, i, 0)), pl.BlockSpec((1, ts, 512), lambda b, i: (b, i, 0)),
                  kv_spec, kv_spec, pl.BlockSpec((ts, 128), lambda b, i: (i, 0)),
                  pl.BlockSpec((ts, 128), lambda b, i: (i, 0)), const(qnw), const(knw), const(bd512), const(bd128)],
        out_specs=[pl.BlockSpec((1, ts, 768), lambda b, i: (b, i, 0)), pl.BlockSpec((8, 512), lambda b, i: (0, 0))],
        out_shape=[jax.ShapeDtypeStruct((bsz, s, 768), BF16), jax.ShapeDtypeStruct((8, 512), F32)],
        compiler_params=_params(2),
    )(qkv, dq, dk2, dv2, cos, sin, qnw, knw, bd512, bd128)


def _ctx_kv_backward(pc, dk2, dv2, knw, bd128):
    bsz, cl, _ = pc.shape

    def body(p_ref, dk_ref, dv_ref, knw_ref, bd128_ref, d_ref, gw_ref):
        b = pl.program_id(0)
        lo_mask = _lo_mask(cl)
        dkn = _fold_heads(dk_ref[0, 0], dk_ref[0, 1], lo_mask)
        dkp, gk = _norm_backward(dkn, p_ref[0, :, 0:128], knw_ref[...], bd128_ref[...])
        dvp = _fold_heads(dv_ref[0, 0], dv_ref[0, 1], lo_mask)
        d_ref[0] = jnp.concatenate([dkp, dvp], axis=1).astype(BF16)
        rows = jnp.concatenate([jnp.sum(gk, axis=0, keepdims=True), jnp.zeros((7, 128), F32)], axis=0)

        @pl.when(b == 0)
        def _():
            gw_ref[...] = rows

        @pl.when(b > 0)
        def _():
            gw_ref[...] += rows

    const = lambda a: pl.BlockSpec(a.shape, lambda b: (0,) * a.ndim)
    kv_spec = pl.BlockSpec((1, 2, cl, 128), lambda b: (b, 0, 0, 0))
    return pl.pallas_call(
        body, name="ctx_kv_backward", grid=(bsz,),
        in_specs=[pl.BlockSpec((1, cl, 256), lambda b: (b, 0, 0)), kv_spec, kv_spec, const(knw), const(bd128)],
        out_specs=[pl.BlockSpec((1, cl, 256), lambda b: (b, 0, 0)), pl.BlockSpec((8, 128), lambda b: (0, 0))],
        out_shape=[jax.ShapeDtypeStruct((bsz, cl, 256), BF16), jax.ShapeDtypeStruct((8, 128), F32)],
        compiler_params=_params(1),
    )(pc, dk2, dv2, knw, bd128)


def _weight_grad(parts, u, init, tm, name):
    bsz, s, d = u.shape
    n_p = len(parts)
    nrows = sum(hi - lo for _, lo, hi in parts)

    def body(*refs):
        p_refs, u_ref = refs[:n_p], refs[n_p]
        gi_ref = refs[n_p + 1] if init is not None else None
        gw_ref = refs[-1]
        first = jnp.logical_and(pl.program_id(0) == 0, pl.program_id(1) == 0)
        dp = jnp.concatenate([r[0, :, lo:hi] for r, (_, lo, hi) in zip(p_refs, parts)], axis=1)
        gw = lax.dot_general(dp, u_ref[0], TN, preferred_element_type=F32)

        @pl.when(first)
        def _():
            gw_ref[...] = gw
            if init is not None:
                gw_ref[KV_LO:KV_HI, :] += gi_ref[...]

        @pl.when(jnp.logical_not(first))
        def _():
            gw_ref[...] += gw

    tok = lambda w: pl.BlockSpec((1, tm, w), lambda b, i: (b, i, 0))
    in_specs = [tok(a.shape[2]) for a, _, _ in parts] + [tok(d)]
    args = [a for a, _, _ in parts] + [u]
    if init is not None:
        in_specs.append(pl.BlockSpec(init.shape, lambda b, i: (0, 0)))
        args.append(init)
    return pl.pallas_call(
        body, name=name, grid=(bsz, s // tm), in_specs=in_specs,
        out_specs=pl.BlockSpec((nrows, d), lambda b, i: (0, 0)), out_shape=jax.ShapeDtypeStruct((nrows, d), F32),
        compiler_params=_params(2),
    )(*args)


def _inproj_backward(dps, x, dh, scale1p, norm_w, w_t, tm, name, fused=None):
    bsz, s, d = x.shape
    n_p = len(dps)
    shared = scale1p.shape[0] == 1
    with_dx = dh is not None
    n_in = n_p + (2 if with_dx else 1) + 3
    n_out = 3 if with_dx else 2
    total = bsz * (s // tm)
    at_steps = [(0, True), (max(1, total // 8), True), (max(2, (3 * total) // 4), True), (total - 1, False)]

    def body(*refs):
        ins, outs, _ = _split_fused(refs, n_in, n_out, 0, fused)
        dp_refs, x_ref = ins[:n_p], ins[n_p]
        dh_ref = ins[n_p + 1] if with_dx else None
        sc_ref, nw_ref, w_ref = ins[-3:]
        mod_ref, gnw_ref = outs[-2:]
        b, i = pl.program_id(0), pl.program_id(1)
        step = b * (s // tm) + i
        _run_phases(fused, step, at_steps, True)
        first = jnp.logical_and(b == 0, i == 0)
        dp = dp_refs[0][0] if n_p == 1 else jnp.concatenate([r[0] for r in dp_refs], axis=1)
        du = jnp.dot(dp, w_ref[...], preferred_element_type=F32)
        xv = x_ref[0]
        rstd = lax.rsqrt(jnp.mean(xv * xv, axis=-1, keepdims=True) + EPS)
        xhat = xv * rstd
        nw, sc = nw_ref[...], sc_ref[0]
        red = lambda v: jnp.sum(v, axis=0, keepdims=True)
        mod_rows = jnp.concatenate([red(du), red(du * (xhat * nw)), jnp.zeros((6, d), F32)], axis=0)
        gnw_rows = jnp.concatenate([red(du * sc * xhat), jnp.zeros((7, d), F32)], axis=0)
        mod_first = first if shared else i == 0

        @pl.when(mod_first)
        def _():
            mod_ref[0] = mod_rows

        @pl.when(jnp.logical_not(mod_first))
        def _():
            mod_ref[0] += mod_rows

        @pl.when(first)
        def _():
            gnw_ref[...] = gnw_rows

        @pl.when(jnp.logical_not(first))
        def _():
            gnw_ref[...] += gnw_rows

        if with_dx:
            dxhat = du * (nw * sc)
            outs[0][0] = dh_ref[0] + rstd * (dxhat - xhat * jnp.mean(dxhat * xhat, axis=-1, keepdims=True))
        _run_phases(fused, step, at_steps, False)

    tok = lambda w: pl.BlockSpec((1, tm, w), lambda b, i: (b, i, 0))
    in_specs = [tok(p.shape[2]) for p in dps] + [tok(d)]
    args = list(dps) + [x]
    if with_dx:
        in_specs.append(tok(d))
        args.append(dh)
    in_specs += [_bcast_spec(scale1p), pl.BlockSpec((1, d), lambda b, i: (0, 0)),
                 pl.BlockSpec(w_t.shape, lambda b, i: (0, 0))]
    args += [scale1p, norm_w, w_t]
    bm = scale1p.shape[0]
    mod_spec = pl.BlockSpec((1, 8, d), (lambda b, i: (0, 0, 0)) if shared else (lambda b, i: (b, 0, 0)))
    out_specs = [mod_spec, pl.BlockSpec((8, d), lambda b, i: (0, 0))]
    out_shape = [jax.ShapeDtypeStruct((bm, 8, d), F32), jax.ShapeDtypeStruct((8, d), F32)]
    if with_dx:
        out_specs.insert(0, tok(d))
        out_shape.insert(0, jax.ShapeDtypeStruct((bsz, s, d), F32))
    scratch = []
    if fused is not None:
        in_specs += fused.in_specs
        args += fused.arrs
        out_specs += fused.out_specs
        out_shape += fused.out_shape
        scratch = fused.scratch
    res = pl.pallas_call(
        body, name=name, grid=(bsz, s // tm), in_specs=in_specs, out_specs=out_specs, out_shape=out_shape,
        scratch_shapes=scratch, compiler_params=_params(2),
    )(*args)
    return list(res) if with_dx else [None] + list(res)


def _adamw(w, g, m, v, name):
    r, cdim = w.shape
    tr = next((t for t in (256, 176) if r % t == 0 and r > t), r)

    def body(w_ref, g_ref, m_ref, v_ref, d_ref, nm_ref, nv_ref):
        gv = g_ref[...]
        mn = ADAM_B1 * m_ref[...] + (1.0 - ADAM_B1) * gv
        vn = ADAM_B2 * v_ref[...] + (1.0 - ADAM_B2) * (gv * gv)
        m_hat = mn / (1.0 - ADAM_B1 ** ADAM_STEP)
        v_hat = vn / (1.0 - ADAM_B2 ** ADAM_STEP)
        d_ref[...] = -ADAM_LR * (m_hat / (jnp.sqrt(v_hat) + ADAM_EPS) + ADAM_WD * w_ref[...])
        nm_ref[...] = mn
        nv_ref[...] = vn

    spec = pl.BlockSpec((tr, cdim), lambda i: (i, 0))
    return pl.pallas_call(
        body, name=name, grid=(r // tr,), in_specs=[spec] * 4, out_specs=[spec] * 3,
        out_shape=[jax.ShapeDtypeStruct((r, cdim), F32)] * 3, compiler_params=_params(1),
    )(w, g, m, v)


def _rope_tables(s):
    rows = s // GRID_W
    freqs = np.float32(ROPE_THETA) ** (-np.arange(0, ROPE_AXIS_DIM, 2, dtype=np.float32) / np.float32(ROPE_AXIS_DIM))
    ang_r = np.arange(rows, dtype=np.float32)[:, None] * freqs[None, :]
    ang_c = np.arange(GRID_W, dtype=np.float32)[:, None] * freqs[None, :]
    zr, zc = np.zeros_like(ang_r), np.zeros_like(ang_c)

    def table(by_row, by_col):
        r = jnp.asarray(np.tile(np.concatenate(by_row + [zr, zr], axis=1), (1, 2)), dtype=F32)
        c = jnp.asarray(np.tile(np.concatenate([zc, zc] + by_col, axis=1), (1, 2)), dtype=F32)
        return jnp.repeat(r, GRID_W, axis=0) + jnp.tile(c, (rows, 1))

    return (table([np.cos(ang_r)] * 2, [np.cos(ang_c)] * 2),
            table([-np.sin(ang_r), np.sin(ang_r)], [-np.sin(ang_c), np.sin(ang_c)]))


def _pack_rows(parts, rows):
    flat = jnp.concatenate([p.reshape(-1) for p in parts])
    return jnp.pad(flat, (0, rows * D_MODEL - flat.shape[0])).reshape(rows, D_MODEL)


def kernel(x, c, ctx, c_ctx, w_mod, b_mod, norm_w, w_in, q_norm_w, k_norm_w, conv_w, conv_b, conv_ln_w, conv_ln_b, w_pw, b_pw, w_out, loss_target, m_c_ctx, m_w_mod, m_b_mod, m_norm_w, m_w_in, m_q_norm_w, m_k_norm_w, m_conv_w, m_conv_b, m_conv_ln_w, m_conv_ln_b, m_w_pw, m_b_pw, m_w_out, v_c_ctx, v_w_mod, v_b_mod, v_norm_w, v_w_in, v_q_norm_w, v_k_norm_w, v_conv_w, v_conv_b, v_conv_ln_w, v_conv_ln_b, v_w_pw, v_b_pw, v_w_out):
    bsz, s, d = x.shape
    cl = ctx.shape[1]
    xi, yi, ci = lax.axis_index("x"), lax.axis_index("y"), lax.axis_index("c")
    chip = 2 * xi + yi
    dev = 2 * chip + ci
    ncol_mod = w_mod.shape[2]

    w_in_t_loc = w_in[0].T.astype(BF16)
    b_cols = lax.dynamic_slice(b_mod, (0, chip * ncol_mod), (1, ncol_mod))
    sc_rows, mod_g, g_in = _front(jnp.pad(c, ((0, 8 - bsz), (0, 0))), jnp.pad(c_ctx[None, :], ((0, 15), (0, 0))),
                                  w_mod[0], b_cols, w_in_t_loc)
    w_in_t = g_in.reshape(D_IN, d)
    mod_all = mod_g.transpose(1, 0, 2).reshape(80, 3 * d)
    mod_loc = lax.dynamic_slice(mod_all, (8 * dev, 0), (bsz, 3 * d))
    shift, scale1p, gate = mod_loc[:, None, :d], 1.0 + mod_loc[:, None, d:2 * d], mod_loc[:, None, 2 * d:]
    shift_c, scale1p_c = mod_all[64:65, :d][None], 1.0 + mod_all[64:65, d:2 * d][None]

    cos, sin = _rope_tables(s)
    qnw512 = jnp.tile(q_norm_w, (1, 8))
    knw128 = jnp.tile(k_norm_w, (1, 2))
    bd512 = jnp.kron(jnp.eye(8, dtype=F32), jnp.ones((HEAD_DIM, HEAD_DIM), F32)).astype(BF16)
    bd128 = bd512[:128, :128]

    u, p_qkv, p_za, p_ga, p_gg, p_zc = _norm_inproj(x, shift, scale1p, norm_w, w_in_t, SPLITS, 512, "norm_inproj")
    uc, pc_kv = _norm_inproj(ctx, shift_c, scale1p_c, norm_w, w_in_t[KV_LO:KV_HI], ((0, 256),), cl, "ctx_norm_inproj")
    q, k2x, v2x = _qkv_prep(p_qkv, cos, sin, qnw512, knw128, bd512, bd128, 256, cl)
    k2, v2 = _ctx_kv_prep(pc_kv, knw128, bd128, k2x, v2x)
    conv_w_loc = jnp.pad(conv_w[0], ((0, 1), (0, 0)))
    o, lse, g_out, g_pw, g_cw = _attn_forward(
        q, k2, v2, 256, _ChipGather([w_out[0].astype(BF16), w_pw[0].astype(BF16), conv_w_loc]))
    w_out_f = g_out.reshape(d, d)
    w_pw_f = g_pw.reshape(D_CONV, D_CONV)
    conv_w_f = g_cw.transpose(1, 0, 2).reshape(32, D_CONV)
    y, cv = _conv_forward(p_ga, p_gg, conv_w_f, conv_b, conv_ln_w, conv_ln_b, w_pw_f, b_pw, 256)
    loss_part, dh, do, dza, dcv, dzc, dgate, gw_out = _outproj_loss(
        x, loss_target, gate, o, p_za, cv, p_zc, w_out_f, 512)

    all_chips, half_rows = (0, 1, 2, 3), D_IN // 2
    dy, gw_pw, conv_stats = _conv_token_backward(dcv, y, conv_ln_w, conv_ln_b, w_pw_f, 256)
    da, dg, gcw = _conv_backward(dy, p_ga, p_gg, conv_w_f, 256)
    tw = min(1024, s)
    gw_hi = _weight_grad([(da, half_rows - SPLITS[2][0], 512), (dg, 0, 512), (dzc, 0, 512)], u, None, tw,
                         "grad_in_rows_hi")
    dq, dk2, dv2, r_out, r_pw, r_hi = _attn_backward(
        q, k2, v2, o, do, lse, 256, _FusedReduce([(gw_out, all_chips), (gw_pw, all_chips), (gw_hi, (2, 3))]))
    dqkv, qk_stats = _qkv_backward(p_qkv, dq, dk2, dv2, cos, sin, qnw512, knw128, bd512, bd128, 256, cl)
    dpc, kc_stats = _ctx_kv_backward(pc_kv, dk2, dv2, knw128, bd128)
    gw_ctx = _weight_grad([(dpc, 0, 256)], uc, None, cl, "grad_in_rows_ctx")
    gw_lo = _weight_grad([(dqkv, 0, 768), (dza, 0, 512), (da, 0, half_rows - SPLITS[2][0])], u, gw_ctx, tw,
                         "grad_in_rows_lo")
    _, modc, gnw_c = _inproj_backward([dpc], ctx, None, scale1p_c, norm_w, w_in_t[KV_LO:KV_HI], cl,
                                      "ctx_inproj_backward")
    grad_x, modx, gnw_x, r_lo = _inproj_backward(
        [dqkv, dza, da, dg, dzc], x, dh, scale1p, norm_w, w_in_t, 256, "inproj_backward",
        _FusedReduce([(gw_lo, (0, 1))]))
    g_w_out, g_w_pw = r_out.reshape(d // 4, d), r_pw.reshape(D_CONV // 4, D_CONV)
    g_w_in_t = jnp.where(chip < 2, r_lo, r_hi).reshape(D_IN // 4, d)

    dmod_loc = jnp.concatenate([modx[:, 0, :], modx[:, 1, :], dgate[:, 0, :]], axis=1)
    gq = qk_stats[0].reshape(8, HEAD_DIM).sum(axis=0)
    gk = (qk_stats[1, :128] + kc_stats[0]).reshape(2, HEAD_DIM).sum(axis=0)
    packed = _pack_rows([dmod_loc, dmod_loc.sum(axis=0), gnw_x[0] + gnw_c[0], modc[0, 0], modc[0, 1], gq, gk,
                         conv_stats[0], conv_stats[1], conv_stats[2], conv_stats[3], gcw,
                         jnp.sum(loss_part[:, 0, 0])[None]], 32)
    gathered, total = _all_gather8(packed, "small_grads", reduce=True)
    flat = total.reshape(-1)
    offs = [0]

    def take(nelem):
        lo = offs[0]
        offs[0] = lo + nelem
        return flat[lo:lo + nelem]

    take(bsz * 3 * d)
    g_b_mod_x = take(3 * d)
    g_norm_w = take(d)
    dshift_c, dscale_c = take(d), take(d)
    g_qnw, g_knw = take(HEAD_DIM), take(HEAD_DIM)
    g_b_pw, g_ln_w, g_ln_b, g_conv_b = take(D_CONV), take(D_CONV), take(D_CONV), take(D_CONV)
    g_conv_w_full = take(32 * D_CONV).reshape(32, D_CONV)
    loss = take(1)[0] * (0.5 / d)

    dmod_c = jnp.concatenate([dshift_c, dscale_c, jnp.zeros((d,), F32)])
    g_b_mod = (g_b_mod_x + dmod_c)[None, :]
    dmod_rows = jnp.pad(gathered[:, :6, :].reshape(8, bsz, 3 * d), ((0, 0), (0, 8 - bsz), (0, 0))).reshape(64, 3 * d)
    dmod_all = jnp.concatenate([dmod_rows, jnp.pad(dmod_c[None, :], ((0, 15), (0, 0)))], axis=0)
    dmod_cols = lax.dynamic_slice(dmod_all, (0, chip * ncol_mod), (80, ncol_mod))
    g_w_mod, gcc_part = _mod_backward(sc_rows, dmod_cols, w_mod[0])
    gcc_all = _all_gather8(gcc_part, "c_ctx_grad_gather")
    dsilu_ctx = gcc_all[0, 0] + gcc_all[2, 0] + gcc_all[4, 0] + gcc_all[6, 0]
    sg = _sigmoid(c_ctx)
    g_c_ctx = dsilu_ctx * (sg * (1.0 + c_ctx * (1.0 - sg)))

    g_w_in = g_w_in_t.T
    g_conv_w = lax.dynamic_slice(g_conv_w_full, (0, chip * 128), (CONV_WIDTH, 128))

    grads = {
        "c_ctx": g_c_ctx, "w_mod": g_w_mod[None], "b_mod": g_b_mod, "norm_w": g_norm_w[None], "w_in": g_w_in[None],
        "q_norm_w": g_qnw[None], "k_norm_w": g_knw[None], "conv_w": g_conv_w[None], "conv_b": g_conv_b[None],
        "conv_ln_w": g_ln_w[None], "conv_ln_b": g_ln_b[None], "w_pw": g_w_pw[None], "b_pw": g_b_pw[None],
        "w_out": g_w_out[None],
    }
    weights = {
        "c_ctx": (c_ctx, m_c_ctx, v_c_ctx), "w_mod": (w_mod, m_w_mod, v_w_mod), "b_mod": (b_mod, m_b_mod, v_b_mod),
        "norm_w": (norm_w, m_norm_w, v_norm_w), "w_in": (w_in, m_w_in, v_w_in),
        "q_norm_w": (q_norm_w, m_q_norm_w, v_q_norm_w), "k_norm_w": (k_norm_w, m_k_norm_w, v_k_norm_w),
        "conv_w": (conv_w, m_conv_w, v_conv_w), "conv_b": (conv_b, m_conv_b, v_conv_b),
        "conv_ln_w": (conv_ln_w, m_conv_ln_w, v_conv_ln_w), "conv_ln_b": (conv_ln_b, m_conv_ln_b, v_conv_ln_b),
        "w_pw": (w_pw, m_w_pw, v_w_pw), "b_pw": (b_pw, m_b_pw, v_b_pw), "w_out": (w_out, m_w_out, v_w_out),
    }
    names = list(weights)
    deltas, new_ms, new_vs = [], [], []
    for n in names:
        w, m, v = weights[n]
        shape = w.shape
        if n == "w_in":
            dl, nm, nv = _adamw(w[0].T, g_w_in_t, m[0].T, v[0].T, "adamw_" + n)
            deltas.append(dl.T[None])
            new_ms.append(nm.T[None])
            new_vs.append(nv.T[None])
            continue
        two_d = (1, shape[0]) if w.ndim == 1 else (shape[-2] if w.ndim == 3 else 1, shape[-1])
        dl, nm, nv = _adamw(w.reshape(two_d), grads[n].reshape(two_d), m.reshape(two_d), v.reshape(two_d), "adamw_" + n)
        deltas.append(dl.reshape(shape))
        new_ms.append(nm.reshape(shape))
        new_vs.append(nv.reshape(shape))
        grads[n] = grads[n].reshape(shape)

    return (loss, grad_x, *[grads[n] for n in names], *deltas, *new_ms, *new_vs)
```

```python
import functools
import math

import jax
import jax.numpy as jnp
import numpy as np
from jax import lax
from jax.experimental import pallas as pl
from jax.experimental.pallas import tpu as pltpu

F32 = jnp.float32
BF16 = jnp.bfloat16
MESH = pl.DeviceIdType.MESH

D_MODEL = 1024
D_ATTN = 512
D_CONV = 512
HEAD_DIM = 64
N_KV = 2
GRID_W = 64
ROPE_AXIS_DIM = 32
ROPE_THETA = 10000.0
CONV_WIDTH = 31
CONV_PAD = 15
HALO = 16
CONV_ROWS = 32
EPS = 1e-6
SPLITS = ((0, 768), (768, 1280), (1280, 1792), (1792, 2304), (2304, 2816))
D_IN = 2816
KV_LO, KV_HI = 512, 768

ADAM_LR = 0.001
ADAM_B1 = 0.9
ADAM_B2 = 0.999
ADAM_EPS = 1e-08
ADAM_WD = 0.01
ADAM_STEP = 10

VMEM_LIMIT = 56 * 1024 * 1024

NT = (((1,), (1,)), ((), ()))
TN = (((0,), (0,)), ((), ()))


def _params(n_axes=0, **kw):
    if n_axes:
        kw["dimension_semantics"] = ("arbitrary",) * n_axes
    return pltpu.CompilerParams(vmem_limit_bytes=VMEM_LIMIT, **kw)


def _sigmoid(x):
    return 1.0 / (1.0 + jnp.exp(-x))


def _silu_and_grad(z):
    s = _sigmoid(z)
    return z * s, s * (1.0 + z * (1.0 - s))


def _seg_mean(v, ones_bd):
    hi = v.astype(BF16)
    lo = (v - hi.astype(F32)).astype(BF16)
    s = jnp.dot(hi, ones_bd, preferred_element_type=F32) + jnp.dot(lo, ones_bd, preferred_element_type=F32)
    return s * (1.0 / HEAD_DIM)


def _partner(v):
    n = v.shape[1]
    lane = lax.broadcasted_iota(jnp.int32, (v.shape[0], 128), 1)
    first = (lane % 32) < 16
    parts = []
    for k in range(n // 128):
        ch = v[:, 128 * k:128 * (k + 1)]
        parts.append(jnp.where(first, pltpu.roll(ch, 112, 1), pltpu.roll(ch, 16, 1)))
    return parts[0] if len(parts) == 1 else jnp.concatenate(parts, axis=1)


def _tile_lanes(t, reps):
    return t if reps == 1 else jnp.concatenate([t] * reps, axis=1)


def _lo_mask(rows):
    return lax.broadcasted_iota(jnp.int32, (rows, 128), 1) < HEAD_DIM


def _gather8_in_vmem(x_ref, out_ref, send_sems, recv_sems, local_sem):
    x, y, c = lax.axis_index("x"), lax.axis_index("y"), lax.axis_index("c")
    me, sibling = (x, y, c), (x, y, 1 - c)
    chips = [(1 - x, y), (x, 1 - y), (1 - x, 1 - y)]

    def slot(px, py, pc):
        return out_ref.at[4 * px + 2 * py + pc]

    def copy(k, block, to, src=None):
        return pltpu.make_async_remote_copy(
            src_ref=slot(*block) if src is None else src, dst_ref=slot(*block),
            send_sem=send_sems.at[k], recv_sem=recv_sems.at[k], device_id=to, device_id_type=MESH)

    mine = pltpu.make_async_copy(x_ref, slot(*me), local_sem)
    mine.start()
    first = [copy(0, me, sibling, src=x_ref)]
    first += [copy(1 + j, me, (*chip, c), src=x_ref) for j, chip in enumerate(chips)]
    for cp in first:
        cp.start()
    passed = [copy(4 + j, (*chip, c), sibling) for j, chip in enumerate(chips)]
    for j, chip in enumerate(chips):
        copy(1 + j, (*chip, c), me).wait_recv()
        passed[j].start()
    copy(0, sibling, me).wait_recv()
    for j, chip in enumerate(chips):
        copy(4 + j, (*chip, 1 - c), me).wait_recv()
    for cp in first + passed:
        cp.wait_send()
    mine.wait()


class _ChipGather:
    def __init__(self, arrs):
        self.arrs = list(arrs)
        n = self.n = len(self.arrs)
        self.in_specs = [pl.BlockSpec(memory_space=pl.ANY)] * n
        self.out_shape = [jax.ShapeDtypeStruct((4,) + a.shape, a.dtype) for a in self.arrs]
        self.out_specs = [pl.BlockSpec(memory_space=pl.ANY)] * n
        self.scratch = [pltpu.SemaphoreType.DMA((6 * n,)), pltpu.SemaphoreType.DMA((6 * n,)),
                        pltpu.SemaphoreType.DMA((n,))]
        self.phases = [self.start, self.forward, self.finish]

    def bind(self, ins, outs, scratch):
        self.ins, self.outs = ins, outs
        self.send_sems, self.recv_sems, self.local_sems = scratch
        self.x, self.y, self.c = lax.axis_index("x"), lax.axis_index("y"), lax.axis_index("c")
        self.chips = [(1 - self.x, self.y), (self.x, 1 - self.y), (1 - self.x, 1 - self.y)]
        self.mychip = 2 * self.x + self.y

    def _copy(self, a, k, chip_idx, cc, to, src=None):
        h = self.arrs[a].shape[0] // 2
        dst = self.outs[a].at[chip_idx, pl.ds(cc * h, h)]
        return pltpu.make_async_remote_copy(
            src_ref=dst if src is None else src, dst_ref=dst, send_sem=self.send_sems.at[6 * a + k],
            recv_sem=self.recv_sems.at[6 * a + k], device_id=to, device_id_type=MESH)

    def _local(self, a):
        return pltpu.make_async_copy(self.ins[a], self.outs[a].at[self.mychip], self.local_sems.at[a])

    def _first(self, a, j):
        h = self.arrs[a].shape[0] // 2
        return self._copy(a, j, self.mychip, self.c, (*self.chips[j], self.c), src=self.ins[a].at[pl.ds(self.c * h, h)])

    def _passed(self, a, j):
        cx, cy = self.chips[j]
        return self._copy(a, 3 + j, 2 * cx + cy, self.c, (self.x, self.y, 1 - self.c))

    def start(self):
        for a in range(self.n):
            self._local(a).start()
            for j in range(3):
                self._first(a, j).start()

    def forward(self):
        for a in range(self.n):
            for j, (cx, cy) in enumerate(self.chips):
                self._copy(a, j, 2 * cx + cy, self.c, (self.x, self.y, self.c)).wait_recv()
                self._passed(a, j).start()

    def finish(self):
        for a in range(self.n):
            for j, (cx, cy) in enumerate(self.chips):
                self._copy(a, 3 + j, 2 * cx + cy, 1 - self.c, (self.x, self.y, self.c)).wait_recv()
        for a in range(self.n):
            for j in range(3):
                self._first(a, j).wait_send()
                self._passed(a, j).wait_send()
            self._local(a).wait()


class _FusedReduce:
    def __init__(self, pieces):
        self.owners = [tuple(o) for _, o in pieces]
        self.arrs = [g.reshape(len(o), 2, g.shape[0] // (2 * len(o)), g.shape[1]) for g, o in pieces]
        n = self.n = len(pieces)
        hc = self.hc = [(v.shape[2], v.shape[3]) for v in self.arrs]
        nts = [len(o) for o in self.owners]
        self.base = [sum(nts[:p]) for p in range(n)]
        anyspec = pl.BlockSpec(memory_space=pl.ANY)
        self.in_specs = [anyspec] * n
        self.out_shape = [jax.ShapeDtypeStruct((2,) + s, F32) for s in hc]
        self.out_specs = [anyspec] * n
        self.scratch = [pltpu.VMEM((nt,) + s, F32) for nt, s in zip(nts, hc)]
        self.scratch += [pltpu.VMEM((nt,) + s, F32) for nt, s in zip(nts, hc)]
        self.scratch += [pltpu.VMEM(s, F32) for s in hc]
        self.scratch += [pltpu.VMEM((nt,) + s, BF16) for nt, s in zip(nts, hc)]
        self.scratch += [pltpu.VMEM((3,) + s, BF16) for s in hc]
        self.scratch += [pltpu.VMEM(s, F32) for s in hc]
        tot = sum(nts)
        self.scratch += [pltpu.SemaphoreType.DMA((tot,)), pltpu.SemaphoreType.DMA((tot,)),
                         pltpu.SemaphoreType.DMA((tot,)), pltpu.SemaphoreType.DMA((3 * n,)),
                         pltpu.SemaphoreType.DMA((n,)), pltpu.SemaphoreType.DMA((n,)), pltpu.SemaphoreType.DMA((n,)),
                         pltpu.SemaphoreType.DMA((tot,))]
        self.phases = [self.start, self.exchange, self.combine, self.finish]

    def bind(self, ins, outs, scratch):
        n = self.n
        self.g, self.out = ins, outs
        self.va, self.recv_a, self.own = scratch[:n], scratch[n:2 * n], scratch[2 * n:3 * n]
        self.tsend, self.recv_b, self.fin = scratch[3 * n:4 * n], scratch[4 * n:5 * n], scratch[5 * n:6 * n]
        self.sa, self.ra, self.sb, self.rb, self.sc, self.rc, self.lc, self.la = scratch[6 * n:]
        self.x, self.y, self.c = lax.axis_index("x"), lax.axis_index("y"), lax.axis_index("c")
        self.mychip = 2 * self.x + self.y
        self.sibling = (self.x, self.y, 1 - self.c)

    def _copy_a(self, p, t):
        k = self.base[p] + t
        return pltpu.make_async_remote_copy(
            src_ref=self.g[p].at[t, 1 - self.c], dst_ref=self.recv_a[p].at[t], send_sem=self.sa.at[k],
            recv_sem=self.ra.at[k], device_id=self.sibling, device_id_type=MESH)

    def _fetch(self, p, t):
        return pltpu.make_async_copy(self.g[p].at[t, self.c], self.va[p].at[t], self.la.at[self.base[p] + t])

    def _slot(self, owner):
        rel = jnp.bitwise_xor(self.mychip, owner)
        return jnp.where(rel == 2, 0, jnp.where(rel == 1, 1, 2))

    def _copy_b(self, p, t, slot):
        owner = self.owners[p][t]
        return pltpu.make_async_remote_copy(
            src_ref=self.tsend[p].at[t], dst_ref=self.recv_b[p].at[slot], send_sem=self.sb.at[self.base[p] + t],
            recv_sem=self.rb.at[3 * p + slot], device_id=(owner // 2, owner % 2, self.c), device_id_type=MESH)

    def _copy_c(self, p, half):
        return pltpu.make_async_remote_copy(
            src_ref=self.fin[p], dst_ref=self.out[p].at[half], send_sem=self.sc.at[p], recv_sem=self.rc.at[p],
            device_id=self.sibling, device_id_type=MESH)

    def _local_c(self, p):
        return pltpu.make_async_copy(self.fin[p], self.out[p].at[self.c], self.lc.at[p])

    def start(self):
        for p in range(self.n):
            for t in range(len(self.owners[p])):
                self._copy_a(p, t).start()
                self._fetch(p, t).start()

    def exchange(self):
        for p in range(self.n):
            for t, owner in enumerate(self.owners[p]):
                self._copy_a(p, t).wait_recv()
                self._fetch(p, t).wait()
                mine = self.mychip == owner

                @pl.when(mine)
                def _():
                    self.own[p][...] = self.va[p][t] + self.recv_a[p][t]

                @pl.when(jnp.logical_not(mine))
                def _():
                    self.tsend[p][t] = (self.va[p][t] + self.recv_a[p][t]).astype(BF16)
                    self._copy_b(p, t, self._slot(owner)).start()

    def combine(self):
        for p in range(self.n):
            for t, owner in enumerate(self.owners[p]):
                @pl.when(self.mychip == owner)
                def _():
                    acc = self.own[p][...]
                    for j in range(3):
                        self._copy_b(p, t, j).wait_recv()
                        acc = acc + self.recv_b[p][j].astype(F32)
                    self.fin[p][...] = acc
                    self._local_c(p).start()
                    self._copy_c(p, self.c).start()

    def finish(self):
        for p in range(self.n):
            for t, owner in enumerate(self.owners[p]):
                self._copy_a(p, t).wait_send()
                mine = self.mychip == owner

                @pl.when(mine)
                def _():
                    self._copy_c(p, 1 - self.c).wait_recv()
                    self._copy_c(p, self.c).wait_send()
                    self._local_c(p).wait()

                @pl.when(jnp.logical_not(mine))
                def _():
                    self._copy_b(p, t, self._slot(owner)).wait_send()


def _split_fused(refs, n_in, n_out, n_scr, fused):
    if fused is None:
        return refs[:n_in], refs[n_in:n_in + n_out], refs[n_in + n_out:]
    fi, fo = len(fused.in_specs), len(fused.out_specs)
    ins, rest = refs[:n_in], refs[n_in:]
    f_ins, rest = rest[:fi], rest[fi:]
    outs, rest = rest[:n_out], rest[n_out:]
    f_outs, rest = rest[:fo], rest[fo:]
    scr, f_scr = rest[:n_scr], rest[n_scr:]
    fused.bind(f_ins, f_outs, f_scr)
    return ins, outs, scr


def _run_phases(fused, step, at_steps, before):
    if fused is None:
        return
    for phase, (at, first) in zip(fused.phases, at_steps):
        if first == before:
            pl.when(step == at)(phase)


def _front(c_pad, c_ctx_rows, w_mod, b_cols, w_in_t_loc):
    ncol = w_mod.shape[1]
    gather = _ChipGather([w_in_t_loc])

    def body(c_ref, cctx_ref, w_ref, b_ref, win_ref, sc_ref, modg_ref, wing_ref,
             call_ref, ag_send, ag_recv, ag_local, m_send, m_recv, *g_scr):
        gather.bind([win_ref], [wing_ref], g_scr)
        gather.start()
        _gather8_in_vmem(c_ref, call_ref, ag_send, ag_recv, ag_local)
        x, y, c = lax.axis_index("x"), lax.axis_index("y"), lax.axis_index("c")
        chips = [(1 - x, y), (x, 1 - y), (1 - x, 1 - y)]
        mychip = 2 * x + y
        rows = jnp.concatenate([call_ref[dv] for dv in range(8)] + [cctx_ref[...]], axis=0)
        sc = rows * _sigmoid(rows)
        sc_ref[...] = sc
        modg_ref[mychip] = jnp.dot(sc, w_ref[...], preferred_element_type=F32,
                                   precision=lax.Precision.HIGHEST) + b_ref[...]

        def mcopy(j, chip_idx, to):
            return pltpu.make_async_remote_copy(
                src_ref=modg_ref.at[chip_idx], dst_ref=modg_ref.at[chip_idx], send_sem=m_send.at[j],
                recv_sem=m_recv.at[j], device_id=to, device_id_type=MESH)

        sends = [mcopy(j, mychip, (*chip, c)) for j, chip in enumerate(chips)]
        for cp in sends:
            cp.start()
        for j, (cx, cy) in enumerate(chips):
            mcopy(j, 2 * cx + cy, (x, y, c)).wait_recv()
        gather.forward()
        gather.finish()
        for cp in sends:
            cp.wait_send()

    vm = pl.BlockSpec(memory_space=pltpu.VMEM)
    return pl.pallas_call(
        body, name="front_exchange",
        out_shape=[jax.ShapeDtypeStruct((80, D_MODEL), F32), jax.ShapeDtypeStruct((4, 80, ncol), F32)] + gather.out_shape,
        in_specs=[vm, vm, vm, vm] + gather.in_specs, out_specs=[vm, vm] + gather.out_specs,
        scratch_shapes=[pltpu.VMEM((8, 8, D_MODEL), F32), pltpu.SemaphoreType.DMA((7,)), pltpu.SemaphoreType.DMA((7,)),
                        pltpu.SemaphoreType.DMA, pltpu.SemaphoreType.DMA((3,)), pltpu.SemaphoreType.DMA((3,))]
        + gather.scratch,
        compiler_params=_params(),
    )(c_pad, c_ctx_rows, w_mod, b_cols, w_in_t_loc)


def _tail_exchange(packed, sc_rows, w_mod, bsz, ctx_row):
    d = D_MODEL
    ncol = w_mod.shape[1]

    def body(p_ref, sc_ref, w_ref, total_ref, gw_ref, gcc_ref, gat_ref, dm_ref, part_ref,
             ag_send, ag_recv, ag_local, g_send, g_recv):
        _gather8_in_vmem(p_ref, gat_ref, ag_send, ag_recv, ag_local)
        acc = gat_ref[0]
        for dv in range(1, 8):
            acc = acc + gat_ref[dv]
        total_ref[...] = acc
        x, y, c = lax.axis_index("x"), lax.axis_index("y"), lax.axis_index("c")
        chips = [(1 - x, y), (x, 1 - y), (1 - x, 1 - y)]
        mychip = 2 * x + y
        dm_ref[...] = jnp.zeros(dm_ref.shape, F32)
        for k in range(4):
            @pl.when(mychip == k)
            def _():
                spans = [(seg, max(k * ncol, seg * d) - seg * d, min((k + 1) * ncol, (seg + 1) * d) - seg * d)
                         for seg in range(3) if k * ncol < (seg + 1) * d and (k + 1) * ncol > seg * d]
                for dv in range(8):
                    for b in range(bsz):
                        dm_ref[8 * dv + b:8 * dv + b + 1, :] = jnp.concatenate(
                            [gat_ref[dv, 3 * b + seg:3 * b + seg + 1, lo:hi] for seg, lo, hi in spans], axis=1)
                dm_ref[64:65, :] = jnp.concatenate(
                    [total_ref[ctx_row + seg:ctx_row + seg + 1, lo:hi] if seg < 2 else jnp.zeros((1, hi - lo), F32)
                     for seg, lo, hi in spans], axis=1)

        dm = dm_ref[...]
        gw_ref[...] = lax.dot_general(sc_ref[...], dm, TN, preferred_element_type=F32,
                                      precision=lax.Precision.HIGHEST)
        part_ref[mychip] = lax.dot_general(dm[64:72, :], w_ref[...], NT, preferred_element_type=F32,
                                           precision=lax.Precision.HIGHEST)

        def gcopy(j, chip_idx, to):
            return pltpu.make_async_remote_copy(
                src_ref=part_ref.at[chip_idx], dst_ref=part_ref.at[chip_idx], send_sem=g_send.at[j],
                recv_sem=g_recv.at[j], device_id=to, device_id_type=MESH)

        sends = [gcopy(j, mychip, (*chip, c)) for j, chip in enumerate(chips)]
        for cp in sends:
            cp.start()
        for j, (cx, cy) in enumerate(chips):
            gcopy(j, 2 * cx + cy, (x, y, c)).wait_recv()
        for cp in sends:
            cp.wait_send()
        gcc_ref[...] = (part_ref[0] + part_ref[1]) + (part_ref[2] + part_ref[3])

    return pl.pallas_call(
        body, name="tail_exchange",
        out_shape=[jax.ShapeDtypeStruct(packed.shape, F32), jax.ShapeDtypeStruct((d, ncol), F32),
                   jax.ShapeDtypeStruct((8, d), F32)],
        scratch_shapes=[pltpu.VMEM((8,) + packed.shape, F32), pltpu.VMEM((80, ncol), F32), pltpu.VMEM((4, 8, d), F32),
                        pltpu.SemaphoreType.DMA((7,)), pltpu.SemaphoreType.DMA((7,)), pltpu.SemaphoreType.DMA,
                        pltpu.SemaphoreType.DMA((3,)), pltpu.SemaphoreType.DMA((3,))],
        compiler_params=_params(),
    )(packed, sc_rows, w_mod)


def _bcast_spec(arr):
    if arr.shape[0] == 1:
        return pl.BlockSpec((1, 1, arr.shape[2]), lambda b, i: (0, 0, 0))
    return pl.BlockSpec((1, 1, arr.shape[2]), lambda b, i: (b, 0, 0))


def _norm_inproj(x, shift, scale1p, norm_w, w_t, splits, tm, name):
    bsz, s, d = x.shape

    def body(x_ref, sh_ref, sc_ref, nw_ref, w_ref, u_ref, *out_refs):
        xv = x_ref[0]
        rstd = lax.rsqrt(jnp.mean(xv * xv, axis=-1, keepdims=True) + EPS)
        u = (xv * rstd * nw_ref[...]) * sc_ref[0] + sh_ref[0]
        ub = u.astype(BF16)
        u_ref[0] = ub
        for (lo, hi), o_ref in zip(splits, out_refs):
            o_ref[0] = lax.dot_general(ub, w_ref[lo:hi, :], NT, preferred_element_type=F32)

    tok = lambda w: pl.BlockSpec((1, tm, w), lambda b, i: (b, i, 0))
    return pl.pallas_call(
        body, name=name, grid=(bsz, s // tm),
        in_specs=[tok(d), _bcast_spec(shift), _bcast_spec(scale1p), pl.BlockSpec((1, d), lambda b, i: (0, 0)),
                  pl.BlockSpec(w_t.shape, lambda b, i: (0, 0))],
        out_specs=[tok(d)] + [tok(hi - lo) for lo, hi in splits],
        out_shape=[jax.ShapeDtypeStruct((bsz, s, d), BF16)]
        + [jax.ShapeDtypeStruct((bsz, s, hi - lo), F32) for lo, hi in splits],
        compiler_params=_params(2),
    )(x, shift, scale1p, norm_w, w_t)


def _dup_heads(kv, lo_mask):
    r = pltpu.roll(kv, HEAD_DIM, 1)
    return jnp.where(lo_mask, kv, r), jnp.where(lo_mask, r, kv)


def _qkv_prep(qkv, cos, sin, qnw, knw, bd512, bd128, ts, row0):
    bsz, s, _ = qkv.shape

    def body(p_ref, cos_ref, sin_ref, qnw_ref, knw_ref, bd512_ref, bd128_ref, q_ref, k_ref, v_ref):
        lo_mask = _lo_mask(ts)
        cos_t, sin_t = cos_ref[...], sin_ref[...]
        qp = p_ref[0, :, 0:512]
        qn = qp * lax.rsqrt(_seg_mean(qp * qp, bd512_ref[...]) + EPS) * qnw_ref[...]
        qr = qn * _tile_lanes(cos_t, 4) + _partner(qn) * _tile_lanes(sin_t, 4)
        q_ref[0] = (qr * (1.0 / math.sqrt(HEAD_DIM))).astype(BF16)
        kp = p_ref[0, :, 512:640]
        kn = kp * lax.rsqrt(_seg_mean(kp * kp, bd128_ref[...]) + EPS) * knw_ref[...]
        kr = kn * cos_t + _partner(kn) * sin_t
        k0, k1 = _dup_heads(kr, lo_mask)
        k_ref[0, 0] = k0.astype(BF16)
        k_ref[0, 1] = k1.astype(BF16)
        v0, v1 = _dup_heads(p_ref[0, :, 640:768], lo_mask)
        v_ref[0, 0] = v0.astype(BF16)
        v_ref[0, 1] = v1.astype(BF16)

    const = lambda a: pl.BlockSpec(a.shape, lambda b, i: (0,) * a.ndim)
    kv_spec = pl.BlockSpec((1, 2, ts, 128), lambda b, i: (b, 0, i + row0 // ts, 0))
    return pl.pallas_call(
        body, name="qkv_prep", grid=(bsz, s // ts),
        in_specs=[pl.BlockSpec((1, ts, 768), lambda b, i: (b, i, 0)),
                  pl.BlockSpec((ts, 128), lambda b, i: (i, 0)), pl.BlockSpec((ts, 128), lambda b, i: (i, 0)),
                  const(qnw), const(knw), const(bd512), const(bd128)],
        out_specs=[pl.BlockSpec((1, ts, 512), lambda b, i: (b, i, 0)), kv_spec, kv_spec],
        out_shape=[jax.ShapeDtypeStruct((bsz, s, 512), BF16), jax.ShapeDtypeStruct((bsz, 2, row0 + s, 128), BF16),
                   jax.ShapeDtypeStruct((bsz, 2, row0 + s, 128), BF16)],
        compiler_params=_params(2),
    )(qkv, cos, sin, qnw, knw, bd512, bd128)


def _ctx_kv_prep(pc, knw, bd128, k2, v2):
    bsz, cl, _ = pc.shape

    def body(p_ref, knw_ref, bd128_ref, k_in, v_in, k_ref, v_ref):
        lo_mask = _lo_mask(cl)
        kp = p_ref[0, :, 0:128]
        kn = kp * lax.rsqrt(_seg_mean(kp * kp, bd128_ref[...]) + EPS) * knw_ref[...]
        k0, k1 = _dup_heads(kn, lo_mask)
        k_ref[0, 0] = k0.astype(BF16)
        k_ref[0, 1] = k1.astype(BF16)
        v0, v1 = _dup_heads(p_ref[0, :, 128:256], lo_mask)
        v_ref[0, 0] = v0.astype(BF16)
        v_ref[0, 1] = v1.astype(BF16)

    const = lambda a: pl.BlockSpec(a.shape, lambda b: (0,) * a.ndim)
    kv_spec = pl.BlockSpec((1, 2, cl, 128), lambda b: (b, 0, 0, 0))
    return pl.pallas_call(
        body, name="ctx_kv_prep", grid=(bsz,),
        in_specs=[pl.BlockSpec((1, cl, 256), lambda b: (b, 0, 0)), const(knw), const(bd128),
                  pl.BlockSpec(memory_space=pl.ANY), pl.BlockSpec(memory_space=pl.ANY)],
        out_specs=[kv_spec, kv_spec],
        out_shape=[jax.ShapeDtypeStruct(k2.shape, BF16), jax.ShapeDtypeStruct(v2.shape, BF16)],
        input_output_aliases={3: 0, 4: 1},
        compiler_params=_params(1),
    )(pc, knw, bd128, k2, v2)


def _attn_forward(q, k2, v2, tq, fused):
    bsz, s, _ = q.shape
    sk = k2.shape[2]
    nq = s // tq
    total = bsz * N_KV * nq
    at_steps = [(0, True), (total // 4, True), (total - 1, False)]

    def body(*refs):
        (q_ref, k_ref, v_ref), (o_ref, lse_ref), _ = _split_fused(refs, 3, 2, 0, fused)
        step = (pl.program_id(0) * N_KV + pl.program_id(1)) * nq + pl.program_id(2)
        _run_phases(fused, step, at_steps, True)
        kk = k_ref[0, 0]
        lo_mask = _lo_mask(tq)
        vv = v_ref[0, 0]
        v1 = jnp.where(_lo_mask(sk), vv, jnp.ones_like(vv))
        for j in range(2):
            qp = q_ref[0, :, 128 * j:128 * (j + 1)]
            outs, lses = [], []
            for half in range(2):
                sel = lo_mask if half == 0 else jnp.logical_not(lo_mask)
                qs = jnp.where(sel, qp, jnp.zeros_like(qp))
                sc = lax.dot_general(qs, kk, NT, preferred_element_type=F32)
                m = jnp.max(sc, axis=-1, keepdims=True)
                o = jnp.dot(jnp.exp(sc - m).astype(BF16), v1, preferred_element_type=F32)
                l = pltpu.roll(o, HEAD_DIM, 1)
                outs.append(o / l)
                lses.append(m + jnp.log(l))
            o_ref[0, :, 128 * j:128 * (j + 1)] = jnp.where(lo_mask, outs[0], pltpu.roll(outs[1], HEAD_DIM, 1))
            lse_ref[0, :, 128 * j:128 * (j + 1)] = jnp.where(lo_mask, lses[0], pltpu.roll(lses[1], HEAD_DIM, 1))
        _run_phases(fused, step, at_steps, False)

    q_spec = pl.BlockSpec((1, tq, 256), lambda b, g, i: (b, i, g))
    kv_spec = pl.BlockSpec((1, 1, sk, 128), lambda b, g, i: (b, g, 0, 0))
    return pl.pallas_call(
        body, name="attn_forward", grid=(bsz, N_KV, nq),
        in_specs=[q_spec, kv_spec, kv_spec] + fused.in_specs, out_specs=[q_spec, q_spec] + fused.out_specs,
        out_shape=[jax.ShapeDtypeStruct((bsz, s, 512), F32)] * 2 + fused.out_shape,
        scratch_shapes=fused.scratch,
        compiler_params=_params(3),
    )(q, k2, v2, *fused.arrs)


def _halo_specs(width, ts, s):
    r = ts // HALO
    last = s // HALO - 1
    return [pl.BlockSpec((1, ts, width), lambda b, i: (b, i, 0)),
            pl.BlockSpec((1, HALO, width), lambda b, i: (b, jnp.maximum(i * r - 1, 0), 0)),
            pl.BlockSpec((1, HALO, width), lambda b, i: (b, jnp.minimum((i + 1) * r, last), 0))]


def _fill_ext(ext_ref, cur, prev, nxt, i, n_tiles, ts):
    ext_ref[0:HALO, :] = jnp.where(i > 0, prev, jnp.zeros_like(prev))
    ext_ref[HALO:HALO + ts, :] = cur
    ext_ref[HALO + ts:2 * HALO + ts, :] = jnp.where(i < n_tiles - 1, nxt, jnp.zeros_like(nxt))


def _fill_shifted(sh_ref, ext_ref, ts):
    n = ts + 2 * HALO - 8
    for r in range(1, 8):
        sh_ref[r - 1, 0:n, :] = ext_ref[pl.ds(r, n), :]


def _window(sh_ref, ext_ref, off, rows, r0=0):
    q, r = divmod(off, 8)
    if r == 0:
        return ext_ref[pl.ds(r0 + off, rows), :]
    return sh_ref[r - 1, pl.ds(r0 + 8 * q, rows), :]


def _conv_forward(ga, gg, conv_w, conv_b, ln_w, ln_b, w_pw, b_pw, ts):
    bsz, s, dc = ga.shape
    n_tiles = s // ts

    def body(a_ref, ap_ref, an_ref, g_ref, gp_ref, gn_ref, cw_ref, cb_ref, lw_ref, lb_ref, wp_ref, bp_ref,
             y_ref, cv_ref, ext_ref, sh_ref):
        i = pl.program_id(1)
        glu = lambda a, g: a * _sigmoid(g)
        _fill_ext(ext_ref, glu(a_ref[0], g_ref[0]), glu(ap_ref[0], gp_ref[0]), glu(an_ref[0], gn_ref[0]), i, n_tiles, ts)
        _fill_shifted(sh_ref, ext_ref, ts)
        acc = jnp.broadcast_to(cb_ref[...], (ts, dc))
        for j in range(CONV_WIDTH):
            acc = acc + cw_ref[j:j + 1, :] * _window(sh_ref, ext_ref, HALO - CONV_PAD + j, ts)
        y_ref[0] = acc
        mu = jnp.mean(acc, axis=-1, keepdims=True)
        yc = acc - mu
        var = jnp.mean(yc * yc, axis=-1, keepdims=True)
        yn = yc * lax.rsqrt(var + EPS) * lw_ref[...] + lb_ref[...]
        ys = yn * _sigmoid(yn)
        cv_ref[0] = jnp.dot(ys.astype(BF16), wp_ref[...], preferred_element_type=F32) + bp_ref[...]

    const = lambda a: pl.BlockSpec(a.shape, lambda b, i: (0,) * a.ndim)
    return pl.pallas_call(
        body, name="conv_forward", grid=(bsz, n_tiles),
        in_specs=_halo_specs(dc, ts, s) + _halo_specs(dc, ts, s)
        + [const(conv_w), const(conv_b), const(ln_w), const(ln_b), const(w_pw), const(b_pw)],
        out_specs=[pl.BlockSpec((1, ts, dc), lambda b, i: (b, i, 0))] * 2,
        out_shape=[jax.ShapeDtypeStruct((bsz, s, dc), F32)] * 2,
        scratch_shapes=[pltpu.VMEM((ts + 2 * HALO, dc), F32), pltpu.VMEM((7, ts + 2 * HALO, dc), F32)],
        compiler_params=_params(2),
    )(ga, ga, ga, gg, gg, gg, conv_w, conv_b, ln_w, ln_b, w_pw, b_pw)


def _outproj_loss(x, target, gate, o, za, cv, zc, w_out, tm):
    bsz, s, d = x.shape

    def body(x_ref, t_ref, gate_ref, o_ref, za_ref, cv_ref, zc_ref, w_ref,
             loss_ref, dh_ref, do_ref, dza_ref, dcv_ref, dzc_ref, dgate_ref, gw_ref):
        b, i = pl.program_id(0), pl.program_id(1)
        ov, cvv = o_ref[0], cv_ref[0]
        silu_a, dsilu_a = _silu_and_grad(za_ref[0])
        silu_c, dsilu_c = _silu_and_grad(zc_ref[0])
        mix = jnp.concatenate([ov * silu_a, cvv * silu_c], axis=1).astype(BF16)
        out = jnp.dot(mix, w_ref[...], preferred_element_type=F32)
        gate_v = gate_ref[0]
        err = x_ref[0] + gate_v * out - t_ref[0]
        dh = err * (1.0 / d)
        dh_ref[0] = dh
        dout = (dh * gate_v).astype(BF16)
        dmix = lax.dot_general(dout, w_ref[...], NT, preferred_element_type=F32)
        gw = lax.dot_general(mix, dout, TN, preferred_element_type=F32)
        dg = jnp.sum(dh * out, axis=0, keepdims=True)
        sq = jnp.sum(err * err)

        @pl.when(jnp.logical_and(b == 0, i == 0))
        def _():
            gw_ref[...] = gw

        @pl.when(jnp.logical_or(b > 0, i > 0))
        def _():
            gw_ref[...] += gw

        @pl.when(i == 0)
        def _():
            dgate_ref[0] = dg
            loss_ref[...] = jnp.zeros(loss_ref.shape, F32) + sq

        @pl.when(i > 0)
        def _():
            dgate_ref[0] += dg
            loss_ref[...] += sq

        dma, dmc = dmix[:, :D_ATTN], dmix[:, D_ATTN:]
        do_ref[0] = dma * silu_a
        dza_ref[0] = (dma * ov * dsilu_a).astype(BF16)
        dcv_ref[0] = dmc * silu_c
        dzc_ref[0] = (dmc * cvv * dsilu_c).astype(BF16)

    tok = lambda w: pl.BlockSpec((1, tm, w), lambda b, i: (b, i, 0))
    return pl.pallas_call(
        body, name="outproj_loss", grid=(bsz, s // tm),
        in_specs=[tok(d), tok(d), _bcast_spec(gate), tok(512), tok(512), tok(512), tok(512),
                  pl.BlockSpec(w_out.shape, lambda b, i: (0, 0))],
        out_specs=[pl.BlockSpec((1, 8, 128), lambda b, i: (b, 0, 0)), tok(d), tok(512), tok(512), tok(512), tok(512),
                   pl.BlockSpec((1, 1, d), lambda b, i: (b, 0, 0)), pl.BlockSpec((d, d), lambda b, i: (0, 0))],
        out_shape=[jax.ShapeDtypeStruct((bsz, 8, 128), F32), jax.ShapeDtypeStruct((bsz, s, d), F32),
                   jax.ShapeDtypeStruct((bsz, s, 512), F32), jax.ShapeDtypeStruct((bsz, s, 512), BF16),
                   jax.ShapeDtypeStruct((bsz, s, 512), F32), jax.ShapeDtypeStruct((bsz, s, 512), BF16),
                   jax.ShapeDtypeStruct((bsz, 1, d), F32), jax.ShapeDtypeStruct((d, d), F32)],
        compiler_params=_params(2),
    )(x, target, gate, o, za, cv, zc, w_out)


def _conv_token_backward(dcv, y, ln_w, ln_b, w_pw, tm):
    bsz, s, dc = dcv.shape

    def body(dcv_ref, y_ref, lw_ref, lb_ref, wp_ref, dy_ref, gwp_ref, st_ref):
        b, i = pl.program_id(0), pl.program_id(1)
        yv, dcvv = y_ref[0], dcv_ref[0]
        mu = jnp.mean(yv, axis=-1, keepdims=True)
        yc = yv - mu
        rstd = lax.rsqrt(jnp.mean(yc * yc, axis=-1, keepdims=True) + EPS)
        yhat = yc * rstd
        yn = yhat * lw_ref[...] + lb_ref[...]
        ys, dsilu = _silu_and_grad(yn)
        dcvb = dcvv.astype(BF16)
        gwp = lax.dot_general(ys.astype(BF16), dcvb, TN, preferred_element_type=F32)
        dys = lax.dot_general(dcvb, wp_ref[...], NT, preferred_element_type=F32)
        dyn = dys * dsilu
        dyhat = dyn * lw_ref[...]
        dy = rstd * (dyhat - jnp.mean(dyhat, axis=-1, keepdims=True)
                     - yhat * jnp.mean(dyhat * yhat, axis=-1, keepdims=True))
        dy_ref[0] = dy
        red = lambda v: jnp.sum(v, axis=0, keepdims=True)
        stats = jnp.concatenate([red(dcvv), red(dyn * yhat), red(dyn), red(dy), jnp.zeros((4, dc), F32)], axis=0)
        first = jnp.logical_and(b == 0, i == 0)

        @pl.when(first)
        def _():
            gwp_ref[...] = gwp
            st_ref[...] = stats

        @pl.when(jnp.logical_not(first))
        def _():
            gwp_ref[...] += gwp
            st_ref[...] += stats

    tok = pl.BlockSpec((1, tm, dc), lambda b, i: (b, i, 0))
    const = lambda a: pl.BlockSpec(a.shape, lambda b, i: (0,) * a.ndim)
    return pl.pallas_call(
        body, name="conv_token_backward", grid=(bsz, s // tm),
        in_specs=[tok, tok, const(ln_w), const(ln_b), const(w_pw)],
        out_specs=[tok, pl.BlockSpec((dc, dc), lambda b, i: (0, 0)), pl.BlockSpec((8, dc), lambda b, i: (0, 0))],
        out_shape=[jax.ShapeDtypeStruct((bsz, s, dc), F32), jax.ShapeDtypeStruct((dc, dc), F32),
                   jax.ShapeDtypeStruct((8, dc), F32)],
        compiler_params=_params(2),
    )(dcv, y, ln_w, ln_b, w_pw)


def _conv_backward(dy, ga, gg, conv_w, ts):
    bsz, s, dc = dy.shape
    n_tiles = s // ts

    def body(dy_ref, dyp_ref, dyn_ref, a_ref, ap_ref, an_ref, g_ref, gp_ref, gn_ref, cw_ref,
             da_ref, dg_ref, gcw_ref, dyext_ref, ugext_ref, dysh_ref, ugsh_ref, dug_ref, gacc_ref):
        b, i = pl.program_id(0), pl.program_id(1)
        av, sg = a_ref[0], _sigmoid(g_ref[0])
        glu = lambda a, g: a * _sigmoid(g)
        _fill_ext(dyext_ref, dy_ref[0], dyp_ref[0], dyn_ref[0], i, n_tiles, ts)
        _fill_ext(ugext_ref, av * sg, glu(ap_ref[0], gp_ref[0]), glu(an_ref[0], gn_ref[0]), i, n_tiles, ts)
        _fill_shifted(dysh_ref, dyext_ref, ts)
        _fill_shifted(ugsh_ref, ugext_ref, ts)
        gacc_ref[...] = jnp.zeros(gacc_ref.shape, F32)

        def row_block(r, carry):
            r0 = pl.multiple_of(r * CONV_ROWS, CONV_ROWS)
            dyb = dy_ref[0, pl.ds(r0, CONV_ROWS), :]
            acc = jnp.zeros((CONV_ROWS, dc), F32)
            for j in range(CONV_WIDTH):
                acc = acc + cw_ref[j:j + 1, :] * _window(dysh_ref, dyext_ref, HALO + CONV_PAD - j, CONV_ROWS, r0)
                prod = dyb * _window(ugsh_ref, ugext_ref, HALO - CONV_PAD + j, CONV_ROWS, r0)
                part = prod[0:8, :]
                for k in range(8, CONV_ROWS, 8):
                    part = part + prod[k:k + 8, :]
                gacc_ref[j] += part
            dug_ref[pl.ds(r0, CONV_ROWS), :] = acc
            return carry

        lax.fori_loop(0, ts // CONV_ROWS, row_block, 0)
        dug = dug_ref[...]
        gcw = jnp.sum(gacc_ref[...], axis=1)
        first = jnp.logical_and(b == 0, i == 0)

        @pl.when(first)
        def _():
            gcw_ref[...] = gcw

        @pl.when(jnp.logical_not(first))
        def _():
            gcw_ref[...] += gcw

        da_ref[0] = (dug * sg).astype(BF16)
        dg_ref[0] = (dug * av * sg * (1.0 - sg)).astype(BF16)

    tok = pl.BlockSpec((1, ts, dc), lambda b, i: (b, i, 0))
    return pl.pallas_call(
        body, name="conv_backward", grid=(bsz, n_tiles),
        in_specs=_halo_specs(dc, ts, s) + _halo_specs(dc, ts, s) + _halo_specs(dc, ts, s)
        + [pl.BlockSpec(conv_w.shape, lambda b, i: (0, 0))],
        out_specs=[tok, tok, pl.BlockSpec((32, dc), lambda b, i: (0, 0))],
        out_shape=[jax.ShapeDtypeStruct((bsz, s, dc), BF16), jax.ShapeDtypeStruct((bsz, s, dc), BF16),
                   jax.ShapeDtypeStruct((32, dc), F32)],
        scratch_shapes=[pltpu.VMEM((ts + 2 * HALO, dc), F32), pltpu.VMEM((ts + 2 * HALO, dc), F32),
                        pltpu.VMEM((7, ts + 2 * HALO, dc), F32), pltpu.VMEM((7, ts + 2 * HALO, dc), F32),
                        pltpu.VMEM((ts, dc), F32), pltpu.VMEM((32, 8, dc), F32)],
        compiler_params=_params(2),
    )(dy, dy, dy, ga, ga, ga, gg, gg, gg, conv_w)


def _attn_backward(q, k2, v2, o, do, lse, tq, fused):
    bsz, s, _ = q.shape
    sk = k2.shape[2]
    scale = 1.0 / math.sqrt(HEAD_DIM)
    nq = s // tq
    total = bsz * N_KV * nq
    at_steps = [(0, True), (total // 5, True), (total // 2, True), (total - 1, False)]

    def body(*refs):
        (q_ref, k_ref, v_ref, o_ref, do_ref, lse_ref), (dq_ref, dk_ref, dv_ref), _ = _split_fused(refs, 6, 3, 0, fused)
        i = pl.program_id(2)
        step = (pl.program_id(0) * N_KV + pl.program_id(1)) * nq + i
        _run_phases(fused, step, at_steps, True)
        kk, vv = k_ref[0, 0], v_ref[0, 0]
        lo_mask = _lo_mask(tq)
        dk_acc = jnp.zeros((sk, 128), F32)
        dv_acc = jnp.zeros((sk, 128), F32)
        for j in range(2):
            cols = slice(128 * j, 128 * (j + 1))
            qp, dop, lsep = q_ref[0, :, cols], do_ref[0, :, cols], lse_ref[0, :, cols]
            dprod = dop * o_ref[0, :, cols]
            dqs = []
            for half in range(2):
                sel = lo_mask if half == 0 else jnp.logical_not(lo_mask)
                qs = jnp.where(sel, qp, jnp.zeros_like(qp))
                dos = jnp.where(sel, dop, 0.0).astype(BF16)
                lse_h = jnp.max(jnp.where(sel, lsep, -jnp.inf), axis=-1, keepdims=True)
                delta = jnp.sum(jnp.where(sel, dprod, 0.0), axis=-1, keepdims=True)
                sc = lax.dot_general(qs, kk, NT, preferred_element_type=F32)
                p = jnp.exp(sc - lse_h)
                dp = lax.dot_general(dos, vv, NT, preferred_element_type=F32)
                ds = (p * (dp - delta)).astype(BF16)
                dv_acc = dv_acc + lax.dot_general(p.astype(BF16), dos, TN, preferred_element_type=F32)
                dk_acc = dk_acc + lax.dot_general(ds, qs, TN, preferred_element_type=F32)
                dqs.append(jnp.dot(ds, kk, preferred_element_type=F32))
            dq_ref[0, :, cols] = jnp.where(lo_mask, dqs[0], dqs[1]) * scale

        @pl.when(i == 0)
        def _():
            dk_ref[0, 0] = dk_acc
            dv_ref[0, 0] = dv_acc

        @pl.when(i > 0)
        def _():
            dk_ref[0, 0] += dk_acc
            dv_ref[0, 0] += dv_acc

        _run_phases(fused, step, at_steps, False)

    q_spec = pl.BlockSpec((1, tq, 256), lambda b, g, i: (b, i, g))
    kv_spec = pl.BlockSpec((1, 1, sk, 128), lambda b, g, i: (b, g, 0, 0))
    return pl.pallas_call(
        body, name="attn_backward", grid=(bsz, N_KV, nq),
        in_specs=[q_spec, kv_spec, kv_spec, q_spec, q_spec, q_spec] + fused.in_specs,
        out_specs=[q_spec, kv_spec, kv_spec] + fused.out_specs,
        out_shape=[jax.ShapeDtypeStruct((bsz, s, 512), F32), jax.ShapeDtypeStruct((bsz, 2, sk, 128), F32),
                   jax.ShapeDtypeStruct((bsz, 2, sk, 128), F32)] + fused.out_shape,
        scratch_shapes=fused.scratch,
        compiler_params=_params(3),
    )(q, k2, v2, o, do, lse, *fused.arrs)


def _fold_heads(acc2_ref_val0, acc2_ref_val1, lo_mask):
    f0 = acc2_ref_val0 + pltpu.roll(acc2_ref_val0, HEAD_DIM, 1)
    f1 = acc2_ref_val1 + pltpu.roll(acc2_ref_val1, HEAD_DIM, 1)
    return jnp.where(lo_mask, f0, f1)


def _norm_backward(dn, pre, w, bd):
    rstd = lax.rsqrt(_seg_mean(pre * pre, bd) + EPS)
    xhat = pre * rstd
    dxhat = dn * w
    return rstd * (dxhat - xhat * _seg_mean(dxhat * xhat, bd)), dn * xhat


def _qkv_backward(qkv, dq, dk2, dv2, cos, sin, qnw, knw, bd512, bd128, ts, row0):
    bsz, s, _ = qkv.shape

    def body(p_ref, dq_ref, dk_ref, dv_ref, cos_ref, sin_ref, qnw_ref, knw_ref, bd512_ref, bd128_ref, d_ref, gw_ref):
        b, i = pl.program_id(0), pl.program_id(1)
        lo_mask = _lo_mask(ts)
        cos_t, sin_t = cos_ref[...], sin_ref[...]
        dqr = dq_ref[0]
        dqn = dqr * _tile_lanes(cos_t, 4) + _partner(dqr * _tile_lanes(sin_t, 4))
        dqp, gq = _norm_backward(dqn, p_ref[0, :, 0:512], qnw_ref[...], bd512_ref[...])
        dkr = _fold_heads(dk_ref[0, 0], dk_ref[0, 1], lo_mask)
        dkn = dkr * cos_t + _partner(dkr * sin_t)
        dkp, gk = _norm_backward(dkn, p_ref[0, :, 512:640], knw_ref[...], bd128_ref[...])
        dvp = _fold_heads(dv_ref[0, 0], dv_ref[0, 1], lo_mask)
        d_ref[0] = jnp.concatenate([dqp, dkp, dvp], axis=1).astype(BF16)
        gk512 = jnp.concatenate([jnp.sum(gk, axis=0, keepdims=True), jnp.zeros((1, 384), F32)], axis=1)
        rows = jnp.concatenate([jnp.sum(gq, axis=0, keepdims=True), gk512, jnp.zeros((6, 512), F32)], axis=0)
        first = jnp.logical_and(b == 0, i == 0)

        @pl.when(first)
        def _():
            gw_ref[...] = rows

        @pl.when(jnp.logical_not(first))
        def _():
            gw_ref[...] += rows

    const = lambda a: pl.BlockSpec(a.shape, lambda b, i: (0,) * a.ndim)
    kv_spec = pl.BlockSpec((1, 2, ts, 128), lambda b, i: (b, 0, i + row0 // ts, 0))
    return pl.pallas_call(
        body, name="qkv_backward", grid=(bsz, s // ts),
        in_specs=[pl.BlockSpec((1, ts, 768), lambda b, i: (b, i, 0)), pl.BlockSpec((1, ts, 512), lambda b, i: (b, i, 0)),
                  kv_spec, kv_spec, pl.BlockSpec((ts, 128), lambda b, i: (i, 0)),
                  pl.BlockSpec((ts, 128), lambda b, i: (i, 0)), const(qnw), const(knw), const(bd512), const(bd128)],
        out_specs=[pl.BlockSpec((1, ts, 768), lambda b, i: (b, i, 0)), pl.BlockSpec((8, 512), lambda b, i: (0, 0))],
        out_shape=[jax.ShapeDtypeStruct((bsz, s, 768), BF16), jax.ShapeDtypeStruct((8, 512), F32)],
        compiler_params=_params(2),
    )(qkv, dq, dk2, dv2, cos, sin, qnw, knw, bd512, bd128)


def _ctx_kv_backward(pc, dk2, dv2, knw, bd128):
    bsz, cl, _ = pc.shape

    def body(p_ref, dk_ref, dv_ref, knw_ref, bd128_ref, d_ref, gw_ref):
        b = pl.program_id(0)
        lo_mask = _lo_mask(cl)
        dkn = _fold_heads(dk_ref[0, 0], dk_ref[0, 1], lo_mask)
        dkp, gk = _norm_backward(dkn, p_ref[0, :, 0:128], knw_ref[...], bd128_ref[...])
        dvp = _fold_heads(dv_ref[0, 0], dv_ref[0, 1], lo_mask)
        d_ref[0] = jnp.concatenate([dkp, dvp], axis=1).astype(BF16)
        rows = jnp.concatenate([jnp.sum(gk, axis=0, keepdims=True), jnp.zeros((7, 128), F32)], axis=0)

        @pl.when(b == 0)
        def _():
            gw_ref[...] = rows

        @pl.when(b > 0)
        def _():
            gw_ref[...] += rows

    const = lambda a: pl.BlockSpec(a.shape, lambda b: (0,) * a.ndim)
    kv_spec = pl.BlockSpec((1, 2, cl, 128), lambda b: (b, 0, 0, 0))
    return pl.pallas_call(
        body, name="ctx_kv_backward", grid=(bsz,),
        in_specs=[pl.BlockSpec((1, cl, 256), lambda b: (b, 0, 0)), kv_spec, kv_spec, const(knw), const(bd128)],
        out_specs=[pl.BlockSpec((1, cl, 256), lambda b: (b, 0, 0)), pl.BlockSpec((8, 128), lambda b: (0, 0))],
        out_shape=[jax.ShapeDtypeStruct((bsz, cl, 256), BF16), jax.ShapeDtypeStruct((8, 128), F32)],
        compiler_params=_params(1),
    )(pc, dk2, dv2, knw, bd128)


def _weight_grad(parts, u, init, tm, name):
    bsz, s, d = u.shape
    n_p = len(parts)
    nrows = sum(hi - lo for _, lo, hi in parts)

    def body(*refs):
        p_refs, u_ref = refs[:n_p], refs[n_p]
        gi_ref = refs[n_p + 1] if init is not None else None
        gw_ref = refs[-1]
        first = jnp.logical_and(pl.program_id(0) == 0, pl.program_id(1) == 0)
        dp = jnp.concatenate([r[0, :, lo:hi] for r, (_, lo, hi) in zip(p_refs, parts)], axis=1)
        gw = lax.dot_general(dp, u_ref[0], TN, preferred_element_type=F32)

        @pl.when(first)
        def _():
            gw_ref[...] = gw
            if init is not None:
                gw_ref[KV_LO:KV_HI, :] += gi_ref[...]

        @pl.when(jnp.logical_not(first))
        def _():
            gw_ref[...] += gw

    tok = lambda w: pl.BlockSpec((1, tm, w), lambda b, i: (b, i, 0))
    in_specs = [tok(a.shape[2]) for a, _, _ in parts] + [tok(d)]
    args = [a for a, _, _ in parts] + [u]
    if init is not None:
        in_specs.append(pl.BlockSpec(init.shape, lambda b, i: (0, 0)))
        args.append(init)
    return pl.pallas_call(
        body, name=name, grid=(bsz, s // tm), in_specs=in_specs,
        out_specs=pl.BlockSpec((nrows, d), lambda b, i: (0, 0)), out_shape=jax.ShapeDtypeStruct((nrows, d), F32),
        compiler_params=_params(2),
    )(*args)


def _inproj_backward(dps, x, dh, scale1p, norm_w, w_t, tm, name, fused=None):
    bsz, s, d = x.shape
    n_p = len(dps)
    shared = scale1p.shape[0] == 1
    with_dx = dh is not None
    n_in = n_p + (2 if with_dx else 1) + 3
    n_out = 3 if with_dx else 2
    total = bsz * (s // tm)
    at_steps = [(0, True), (total // 8, True), ((3 * total) // 4, True), (total - 1, False)]

    def body(*refs):
        ins, outs, _ = _split_fused(refs, n_in, n_out, 0, fused)
        dp_refs, x_ref = ins[:n_p], ins[n_p]
        dh_ref = ins[n_p + 1] if with_dx else None
        sc_ref, nw_ref, w_ref = ins[-3:]
        mod_ref, gnw_ref = outs[-2:]
        b, i = pl.program_id(0), pl.program_id(1)
        step = b * (s // tm) + i
        _run_phases(fused, step, at_steps, True)
        first = jnp.logical_and(b == 0, i == 0)
        dp = dp_refs[0][0] if n_p == 1 else jnp.concatenate([r[0] for r in dp_refs], axis=1)
        du = jnp.dot(dp, w_ref[...], preferred_element_type=F32)
        xv = x_ref[0]
        rstd = lax.rsqrt(jnp.mean(xv * xv, axis=-1, keepdims=True) + EPS)
        xhat = xv * rstd
        nw, sc = nw_ref[...], sc_ref[0]
        red = lambda v: jnp.sum(v, axis=0, keepdims=True)
        mod_rows = jnp.concatenate([red(du), red(du * (xhat * nw)), jnp.zeros((6, d), F32)], axis=0)
        gnw_rows = jnp.concatenate([red(du * sc * xhat), jnp.zeros((7, d), F32)], axis=0)
        mod_first = first if shared else i == 0

        @pl.when(mod_first)
        def _():
            mod_ref[0] = mod_rows

        @pl.when(jnp.logical_not(mod_first))
        def _():
            mod_ref[0] += mod_rows

        @pl.when(first)
        def _():
            gnw_ref[...] = gnw_rows

        @pl.when(jnp.logical_not(first))
        def _():
            gnw_ref[...] += gnw_rows

        if with_dx:
            dxhat = du * (nw * sc)
            outs[0][0] = dh_ref[0] + rstd * (dxhat - xhat * jnp.mean(dxhat * xhat, axis=-1, keepdims=True))
        _run_phases(fused, step, at_steps, False)

    tok = lambda w: pl.BlockSpec((1, tm, w), lambda b, i: (b, i, 0))
    in_specs = [tok(p.shape[2]) for p in dps] + [tok(d)]
    args = list(dps) + [x]
    if with_dx:
        in_specs.append(tok(d))
        args.append(dh)
    in_specs += [_bcast_spec(scale1p), pl.BlockSpec((1, d), lambda b, i: (0, 0)),
                 pl.BlockSpec(w_t.shape, lambda b, i: (0, 0))]
    args += [scale1p, norm_w, w_t]
    bm = scale1p.shape[0]
    mod_spec = pl.BlockSpec((1, 8, d), (lambda b, i: (0, 0, 0)) if shared else (lambda b, i: (b, 0, 0)))
    out_specs = [mod_spec, pl.BlockSpec((8, d), lambda b, i: (0, 0))]
    out_shape = [jax.ShapeDtypeStruct((bm, 8, d), F32), jax.ShapeDtypeStruct((8, d), F32)]
    if with_dx:
        out_specs.insert(0, tok(d))
        out_shape.insert(0, jax.ShapeDtypeStruct((bsz, s, d), F32))
    scratch = []
    if fused is not None:
        in_specs += fused.in_specs
        args += fused.arrs
        out_specs += fused.out_specs
        out_shape += fused.out_shape
        scratch = fused.scratch
    res = pl.pallas_call(
        body, name=name, grid=(bsz, s // tm), in_specs=in_specs, out_specs=out_specs, out_shape=out_shape,
        scratch_shapes=scratch, compiler_params=_params(2),
    )(*args)
    return list(res) if with_dx else [None] + list(res)


def _adamw_update(w_ref, g_ref, m_ref, v_ref, d_ref, nm_ref, nv_ref):
    gv = g_ref[...]
    mn = ADAM_B1 * m_ref[...] + (1.0 - ADAM_B1) * gv
    vn = ADAM_B2 * v_ref[...] + (1.0 - ADAM_B2) * (gv * gv)
    m_hat = mn / (1.0 - ADAM_B1 ** ADAM_STEP)
    v_hat = vn / (1.0 - ADAM_B2 ** ADAM_STEP)
    d_ref[...] = -ADAM_LR * (m_hat / (jnp.sqrt(v_hat) + ADAM_EPS) + ADAM_WD * w_ref[...])
    nm_ref[...] = mn
    nv_ref[...] = vn


def _adamw_small(ws, gs, ms, vs):
    n = len(ws)

    def body(*refs):
        ins, outs = refs[:4 * n], refs[4 * n:]
        for k in range(n):
            _adamw_update(ins[k], ins[n + k], ins[2 * n + k], ins[3 * n + k], outs[3 * k], outs[3 * k + 1],
                          outs[3 * k + 2])

    res = pl.pallas_call(
        body, name="adamw_small",
        out_shape=[jax.ShapeDtypeStruct(w.shape, F32) for w in ws for _ in range(3)], compiler_params=_params(),
    )(*ws, *gs, *ms, *vs)
    return [tuple(res[3 * k:3 * k + 3]) for k in range(n)]


def _adamw(w, g, m, v, name):
    r, cdim = w.shape
    tr = next((t for t in (256, 176) if r % t == 0 and r > t), r)

    def body(*refs):
        _adamw_update(*refs)

    spec = pl.BlockSpec((tr, cdim), lambda i: (i, 0))
    return pl.pallas_call(
        body, name=name, grid=(r // tr,), in_specs=[spec] * 4, out_specs=[spec] * 3,
        out_shape=[jax.ShapeDtypeStruct((r, cdim), F32)] * 3, compiler_params=_params(1),
    )(w, g, m, v)


def _rope_tables(s):
    rows = s // GRID_W
    freqs = np.float32(ROPE_THETA) ** (-np.arange(0, ROPE_AXIS_DIM, 2, dtype=np.float32) / np.float32(ROPE_AXIS_DIM))
    ang_r = np.arange(rows, dtype=np.float32)[:, None] * freqs[None, :]
    ang_c = np.arange(GRID_W, dtype=np.float32)[:, None] * freqs[None, :]
    zr, zc = np.zeros_like(ang_r), np.zeros_like(ang_c)

    def table(by_row, by_col):
        r = jnp.asarray(np.tile(np.concatenate(by_row + [zr, zr], axis=1), (1, 2)), dtype=F32)
        c = jnp.asarray(np.tile(np.concatenate([zc, zc] + by_col, axis=1), (1, 2)), dtype=F32)
        return jnp.repeat(r, GRID_W, axis=0) + jnp.tile(c, (rows, 1))

    return (table([np.cos(ang_r)] * 2, [np.cos(ang_c)] * 2),
            table([-np.sin(ang_r), np.sin(ang_r)], [-np.sin(ang_c), np.sin(ang_c)]))


def _pack_rows(parts, rows):
    flat = jnp.concatenate([p.reshape(-1) for p in parts])
    return jnp.pad(flat, (0, rows * D_MODEL - flat.shape[0])).reshape(rows, D_MODEL)


def kernel(x, c, ctx, c_ctx, w_mod, b_mod, norm_w, w_in, q_norm_w, k_norm_w, conv_w, conv_b, conv_ln_w, conv_ln_b, w_pw, b_pw, w_out, loss_target, m_c_ctx, m_w_mod, m_b_mod, m_norm_w, m_w_in, m_q_norm_w, m_k_norm_w, m_conv_w, m_conv_b, m_conv_ln_w, m_conv_ln_b, m_w_pw, m_b_pw, m_w_out, v_c_ctx, v_w_mod, v_b_mod, v_norm_w, v_w_in, v_q_norm_w, v_k_norm_w, v_conv_w, v_conv_b, v_conv_ln_w, v_conv_ln_b, v_w_pw, v_b_pw, v_w_out):
    bsz, s, d = x.shape
    cl = ctx.shape[1]
    xi, yi, ci = lax.axis_index("x"), lax.axis_index("y"), lax.axis_index("c")
    chip = 2 * xi + yi
    dev = 2 * chip + ci
    ncol_mod = w_mod.shape[2]

    w_in_t_loc = w_in[0].T.astype(BF16)
    b_cols = lax.dynamic_slice(b_mod, (0, chip * ncol_mod), (1, ncol_mod))
    sc_rows, mod_g, g_in = _front(jnp.pad(c, ((0, 8 - bsz), (0, 0))), jnp.pad(c_ctx[None, :], ((0, 15), (0, 0))),
                                  w_mod[0], b_cols, w_in_t_loc)
    w_in_t = g_in.reshape(D_IN, d)
    mod_all = mod_g.transpose(1, 0, 2).reshape(80, 3 * d)
    mod_loc = lax.dynamic_slice(mod_all, (8 * dev, 0), (bsz, 3 * d))
    shift, scale1p, gate = mod_loc[:, None, :d], 1.0 + mod_loc[:, None, d:2 * d], mod_loc[:, None, 2 * d:]
    shift_c, scale1p_c = mod_all[64:65, :d][None], 1.0 + mod_all[64:65, d:2 * d][None]

    cos, sin = _rope_tables(s)
    qnw512 = jnp.tile(q_norm_w, (1, 8))
    knw128 = jnp.tile(k_norm_w, (1, 2))
    bd512 = jnp.kron(jnp.eye(8, dtype=F32), jnp.ones((HEAD_DIM, HEAD_DIM), F32)).astype(BF16)
    bd128 = bd512[:128, :128]

    u, p_qkv, p_za, p_ga, p_gg, p_zc = _norm_inproj(x, shift, scale1p, norm_w, w_in_t, SPLITS, 512, "norm_inproj")
    uc, pc_kv = _norm_inproj(ctx, shift_c, scale1p_c, norm_w, w_in_t[KV_LO:KV_HI], ((0, 256),), cl, "ctx_norm_inproj")
    q, k2x, v2x = _qkv_prep(p_qkv, cos, sin, qnw512, knw128, bd512, bd128, 256, cl)
    k2, v2 = _ctx_kv_prep(pc_kv, knw128, bd128, k2x, v2x)
    conv_w_loc = jnp.pad(conv_w[0], ((0, 1), (0, 0)))
    o, lse, g_out, g_pw, g_cw = _attn_forward(
        q, k2, v2, 256, _ChipGather([w_out[0].astype(BF16), w_pw[0].astype(BF16), conv_w_loc]))
    w_out_f = g_out.reshape(d, d)
    w_pw_f = g_pw.reshape(D_CONV, D_CONV)
    conv_w_f = g_cw.transpose(1, 0, 2).reshape(32, D_CONV)
    y, cv = _conv_forward(p_ga, p_gg, conv_w_f, conv_b, conv_ln_w, conv_ln_b, w_pw_f, b_pw, 256)
    loss_part, dh, do, dza, dcv, dzc, dgate, gw_out = _outproj_loss(
        x, loss_target, gate, o, p_za, cv, p_zc, w_out_f, 512)

    all_chips, half_rows = (0, 1, 2, 3), D_IN // 2
    dy, gw_pw, conv_stats = _conv_token_backward(dcv, y, conv_ln_w, conv_ln_b, w_pw_f, 256)
    da, dg, gcw = _conv_backward(dy, p_ga, p_gg, conv_w_f, 256)
    tw = min(1024, s)
    gw_hi = _weight_grad([(da, half_rows - SPLITS[2][0], 512), (dg, 0, 512), (dzc, 0, 512)], u, None, tw,
                         "grad_in_rows_hi")
    dq, dk2, dv2, r_out, r_pw, r_hi = _attn_backward(
        q, k2, v2, o, do, lse, 256, _FusedReduce([(gw_out, all_chips), (gw_pw, all_chips), (gw_hi, (2, 3))]))
    dqkv, qk_stats = _qkv_backward(p_qkv, dq, dk2, dv2, cos, sin, qnw512, knw128, bd512, bd128, 256, cl)
    dpc, kc_stats = _ctx_kv_backward(pc_kv, dk2, dv2, knw128, bd128)
    gw_ctx = _weight_grad([(dpc, 0, 256)], uc, None, cl, "grad_in_rows_ctx")
    gw_lo = _weight_grad([(dqkv, 0, 768), (dza, 0, 512), (da, 0, half_rows - SPLITS[2][0])], u, gw_ctx, tw,
                         "grad_in_rows_lo")
    _, modc, gnw_c = _inproj_backward([dpc], ctx, None, scale1p_c, norm_w, w_in_t[KV_LO:KV_HI], cl,
                                      "ctx_inproj_backward")
    grad_x, modx, gnw_x, r_lo = _inproj_backward(
        [dqkv, dza, da, dg, dzc], x, dh, scale1p, norm_w, w_in_t, 512, "inproj_backward",
        _FusedReduce([(gw_lo, (0, 1))]))
    g_w_out, g_w_pw = r_out.reshape(d // 4, d), r_pw.reshape(D_CONV // 4, D_CONV)
    g_w_in_t = jnp.where(chip < 2, r_lo, r_hi).reshape(D_IN // 4, d)

    dmod_loc = jnp.concatenate([modx[:, 0, :], modx[:, 1, :], dgate[:, 0, :]], axis=1)
    gq = qk_stats[0].reshape(8, HEAD_DIM).sum(axis=0)
    gk = (qk_stats[1, :128] + kc_stats[0]).reshape(2, HEAD_DIM).sum(axis=0)
    packed = _pack_rows([dmod_loc, dmod_loc.sum(axis=0), gnw_x[0] + gnw_c[0], modc[0, 0], modc[0, 1], gq, gk,
                         conv_stats[0], conv_stats[1], conv_stats[2], conv_stats[3], gcw,
                         jnp.sum(loss_part[:, 0, 0])[None]], 32)
    total, g_w_mod, dsilu_ctx = _tail_exchange(packed, sc_rows, w_mod[0], bsz, 3 * bsz + 4)
    flat = total.reshape(-1)
    offs = [0]

    def take(nelem):
        lo = offs[0]
        offs[0] = lo + nelem
        return flat[lo:lo + nelem]

    take(bsz * 3 * d)
    g_b_mod_x = take(3 * d)
    g_norm_w = take(d)
    dshift_c, dscale_c = take(d), take(d)
    g_qnw, g_knw = take(HEAD_DIM), take(HEAD_DIM)
    g_b_pw, g_ln_w, g_ln_b, g_conv_b = take(D_CONV), take(D_CONV), take(D_CONV), take(D_CONV)
    g_conv_w_full = take(32 * D_CONV).reshape(32, D_CONV)
    loss = take(1)[0] * (0.5 / d)

    dmod_c = jnp.concatenate([dshift_c, dscale_c, jnp.zeros((d,), F32)])
    g_b_mod = (g_b_mod_x + dmod_c)[None, :]
    sg = _sigmoid(c_ctx)
    g_c_ctx = dsilu_ctx[0] * (sg * (1.0 + c_ctx * (1.0 - sg)))

    g_w_in = g_w_in_t.T
    g_conv_w = lax.dynamic_slice(g_conv_w_full, (0, chip * 128), (CONV_WIDTH, 128))

    grads = {
        "c_ctx": g_c_ctx, "w_mod": g_w_mod[None], "b_mod": g_b_mod, "norm_w": g_norm_w[None], "w_in": g_w_in[None],
        "q_norm_w": g_qnw[None], "k_norm_w": g_knw[None], "conv_w": g_conv_w[None], "conv_b": g_conv_b[None],
        "conv_ln_w": g_ln_w[None], "conv_ln_b": g_ln_b[None], "w_pw": g_w_pw[None], "b_pw": g_b_pw[None],
        "w_out": g_w_out[None],
    }
    weights = {
        "c_ctx": (c_ctx, m_c_ctx, v_c_ctx), "w_mod": (w_mod, m_w_mod, v_w_mod), "b_mod": (b_mod, m_b_mod, v_b_mod),
        "norm_w": (norm_w, m_norm_w, v_norm_w), "w_in": (w_in, m_w_in, v_w_in),
        "q_norm_w": (q_norm_w, m_q_norm_w, v_q_norm_w), "k_norm_w": (k_norm_w, m_k_norm_w, v_k_norm_w),
        "conv_w": (conv_w, m_conv_w, v_conv_w), "conv_b": (conv_b, m_conv_b, v_conv_b),
        "conv_ln_w": (conv_ln_w, m_conv_ln_w, v_conv_ln_w), "conv_ln_b": (conv_ln_b, m_conv_ln_b, v_conv_ln_b),
        "w_pw": (w_pw, m_w_pw, v_w_pw), "b_pw": (b_pw, m_b_pw, v_b_pw), "w_out": (w_out, m_w_out, v_w_out),
    }
    names = list(weights)
    big = ("w_mod", "w_in", "w_out")
    as_2d = lambda a: a.reshape((1, a.shape[0]) if a.ndim == 1 else (a.shape[-2] if a.ndim == 3 else 1, a.shape[-1]))
    small = [n for n in names if n not in big]
    w_g_m_v = zip(*[[as_2d(a) for a in (weights[n][0], grads[n], weights[n][1], weights[n][2])] for n in small])
    updates = dict(zip(small, _adamw_small(*[list(col) for col in w_g_m_v])))
    for n in ("w_mod", "w_out"):
        w, m, v = weights[n]
        updates[n] = _adamw(as_2d(w), as_2d(grads[n]), as_2d(m), as_2d(v), "adamw_" + n)
    w, m, v = weights["w_in"]
    updates["w_in"] = tuple(r.T for r in _adamw(w[0].T, g_w_in_t, m[0].T, v[0].T, "adamw_w_in"))
    deltas, new_ms, new_vs = ([updates[n][k].reshape(weights[n][0].shape) for n in names] for k in range(3))
    grads = {n: grads[n].reshape(weights[n][0].shape) for n in names}

    return (loss, grad_x, *[grads[n] for n in names], *deltas, *new_ms, *new_vs)
```

```python
import functools
import math

import jax
import jax.numpy as jnp
import numpy as np
from jax import lax
from jax.experimental import pallas as pl
from jax.experimental.pallas import tpu as pltpu

F32 = jnp.float32
BF16 = jnp.bfloat16
MESH = pl.DeviceIdType.MESH

D_MODEL = 1024
D_ATTN = 512
D_CONV = 512
HEAD_DIM = 64
N_KV = 2
GRID_W = 64
ROPE_AXIS_DIM = 32
ROPE_THETA = 10000.0
CONV_WIDTH = 31
CONV_PAD = 15
HALO = 16
CONV_ROWS = 32
EPS = 1e-6
SPLITS = ((0, 768), (768, 1280), (1280, 1792), (1792, 2304), (2304, 2816))
D_IN = 2816
KV_LO, KV_HI = 512, 768

ADAM_LR = 0.001
ADAM_B1 = 0.9
ADAM_B2 = 0.999
ADAM_EPS = 1e-08
ADAM_WD = 0.01
ADAM_STEP = 10

VMEM_LIMIT = 56 * 1024 * 1024

NT = (((1,), (1,)), ((), ()))
TN = (((0,), (0,)), ((), ()))


def _params(n_axes=0, **kw):
    if n_axes:
        kw["dimension_semantics"] = ("arbitrary",) * n_axes
    return pltpu.CompilerParams(vmem_limit_bytes=VMEM_LIMIT, **kw)


def _sigmoid(x):
    return 1.0 / (1.0 + jnp.exp(-x))


def _silu_and_grad(z):
    s = _sigmoid(z)
    return z * s, s * (1.0 + z * (1.0 - s))


def _seg_mean(v, ones_bd):
    hi = v.astype(BF16)
    lo = (v - hi.astype(F32)).astype(BF16)
    s = jnp.dot(hi, ones_bd, preferred_element_type=F32) + jnp.dot(lo, ones_bd, preferred_element_type=F32)
    return s * (1.0 / HEAD_DIM)


def _partner(v):
    n = v.shape[1]
    lane = lax.broadcasted_iota(jnp.int32, (v.shape[0], 128), 1)
    first = (lane % 32) < 16
    parts = []
    for k in range(n // 128):
        ch = v[:, 128 * k:128 * (k + 1)]
        parts.append(jnp.where(first, pltpu.roll(ch, 112, 1), pltpu.roll(ch, 16, 1)))
    return parts[0] if len(parts) == 1 else jnp.concatenate(parts, axis=1)


def _tile_lanes(t, reps):
    return t if reps == 1 else jnp.concatenate([t] * reps, axis=1)


def _lo_mask(rows):
    return lax.broadcasted_iota(jnp.int32, (rows, 128), 1) < HEAD_DIM


def _gather8_in_vmem(x_ref, out_ref, send_sems, recv_sems, local_sem):
    x, y, c = lax.axis_index("x"), lax.axis_index("y"), lax.axis_index("c")
    me, sibling = (x, y, c), (x, y, 1 - c)
    chips = [(1 - x, y), (x, 1 - y), (1 - x, 1 - y)]

    def slot(px, py, pc):
        return out_ref.at[4 * px + 2 * py + pc]

    def copy(k, block, to, src=None):
        return pltpu.make_async_remote_copy(
            src_ref=slot(*block) if src is None else src, dst_ref=slot(*block),
            send_sem=send_sems.at[k], recv_sem=recv_sems.at[k], device_id=to, device_id_type=MESH)

    mine = pltpu.make_async_copy(x_ref, slot(*me), local_sem)
    mine.start()
    first = [copy(0, me, sibling, src=x_ref)]
    first += [copy(1 + j, me, (*chip, c), src=x_ref) for j, chip in enumerate(chips)]
    for cp in first:
        cp.start()
    passed = [copy(4 + j, (*chip, c), sibling) for j, chip in enumerate(chips)]
    for j, chip in enumerate(chips):
        copy(1 + j, (*chip, c), me).wait_recv()
        passed[j].start()
    copy(0, sibling, me).wait_recv()
    for j, chip in enumerate(chips):
        copy(4 + j, (*chip, 1 - c), me).wait_recv()
    for cp in first + passed:
        cp.wait_send()
    mine.wait()


class _ChipGather:
    def __init__(self, arrs):
        self.arrs = list(arrs)
        n = self.n = len(self.arrs)
        self.in_specs = [pl.BlockSpec(memory_space=pl.ANY)] * n
        self.out_shape = [jax.ShapeDtypeStruct((4,) + a.shape, a.dtype) for a in self.arrs]
        self.out_specs = [pl.BlockSpec(memory_space=pl.ANY)] * n
        self.scratch = [pltpu.SemaphoreType.DMA((6 * n,)), pltpu.SemaphoreType.DMA((6 * n,)),
                        pltpu.SemaphoreType.DMA((n,))]
        self.phases = [self.start, self.forward, self.finish]

    def bind(self, ins, outs, scratch):
        self.ins, self.outs = ins, outs
        self.send_sems, self.recv_sems, self.local_sems = scratch
        self.x, self.y, self.c = lax.axis_index("x"), lax.axis_index("y"), lax.axis_index("c")
        self.chips = [(1 - self.x, self.y), (self.x, 1 - self.y), (1 - self.x, 1 - self.y)]
        self.mychip = 2 * self.x + self.y

    def _copy(self, a, k, chip_idx, cc, to, src=None):
        h = self.arrs[a].shape[0] // 2
        dst = self.outs[a].at[chip_idx, pl.ds(cc * h, h)]
        return pltpu.make_async_remote_copy(
            src_ref=dst if src is None else src, dst_ref=dst, send_sem=self.send_sems.at[6 * a + k],
            recv_sem=self.recv_sems.at[6 * a + k], device_id=to, device_id_type=MESH)

    def _local(self, a):
        return pltpu.make_async_copy(self.ins[a], self.outs[a].at[self.mychip], self.local_sems.at[a])

    def _first(self, a, j):
        h = self.arrs[a].shape[0] // 2
        return self._copy(a, j, self.mychip, self.c, (*self.chips[j], self.c), src=self.ins[a].at[pl.ds(self.c * h, h)])

    def _passed(self, a, j):
        cx, cy = self.chips[j]
        return self._copy(a, 3 + j, 2 * cx + cy, self.c, (self.x, self.y, 1 - self.c))

    def start(self):
        for a in range(self.n):
            self._local(a).start()
            for j in range(3):
                self._first(a, j).start()

    def forward(self):
        for a in range(self.n):
            for j, (cx, cy) in enumerate(self.chips):
                self._copy(a, j, 2 * cx + cy, self.c, (self.x, self.y, self.c)).wait_recv()
                self._passed(a, j).start()

    def finish(self):
        for a in range(self.n):
            for j, (cx, cy) in enumerate(self.chips):
                self._copy(a, 3 + j, 2 * cx + cy, 1 - self.c, (self.x, self.y, self.c)).wait_recv()
        for a in range(self.n):
            for j in range(3):
                self._first(a, j).wait_send()
                self._passed(a, j).wait_send()
            self._local(a).wait()


class _FusedReduce:
    def __init__(self, pieces):
        self.owners = [tuple(o) for _, o in pieces]
        self.arrs = [g.reshape(len(o), 2, g.shape[0] // (2 * len(o)), g.shape[1]) for g, o in pieces]
        n = self.n = len(pieces)
        hc = self.hc = [(v.shape[2], v.shape[3]) for v in self.arrs]
        nts = [len(o) for o in self.owners]
        self.base = [sum(nts[:p]) for p in range(n)]
        anyspec = pl.BlockSpec(memory_space=pl.ANY)
        self.in_specs = [anyspec] * n
        self.out_shape = [jax.ShapeDtypeStruct((2,) + s, F32) for s in hc]
        self.out_specs = [anyspec] * n
        self.scratch = [pltpu.VMEM((nt,) + s, F32) for nt, s in zip(nts, hc)]
        self.scratch += [pltpu.VMEM((nt,) + s, F32) for nt, s in zip(nts, hc)]
        self.scratch += [pltpu.VMEM(s, F32) for s in hc]
        self.scratch += [pltpu.VMEM((nt,) + s, BF16) for nt, s in zip(nts, hc)]
        self.scratch += [pltpu.VMEM((3,) + s, BF16) for s in hc]
        self.scratch += [pltpu.VMEM(s, F32) for s in hc]
        tot = sum(nts)
        self.scratch += [pltpu.SemaphoreType.DMA((tot,)), pltpu.SemaphoreType.DMA((tot,)),
                         pltpu.SemaphoreType.DMA((tot,)), pltpu.SemaphoreType.DMA((3 * n,)),
                         pltpu.SemaphoreType.DMA((n,)), pltpu.SemaphoreType.DMA((n,)), pltpu.SemaphoreType.DMA((n,)),
                         pltpu.SemaphoreType.DMA((tot,))]
        self.phases = [self.start, self.exchange, self.combine, self.finish]

    def bind(self, ins, outs, scratch):
        n = self.n
        self.g, self.out = ins, outs
        self.va, self.recv_a, self.own = scratch[:n], scratch[n:2 * n], scratch[2 * n:3 * n]
        self.tsend, self.recv_b, self.fin = scratch[3 * n:4 * n], scratch[4 * n:5 * n], scratch[5 * n:6 * n]
        self.sa, self.ra, self.sb, self.rb, self.sc, self.rc, self.lc, self.la = scratch[6 * n:]
        self.x, self.y, self.c = lax.axis_index("x"), lax.axis_index("y"), lax.axis_index("c")
        self.mychip = 2 * self.x + self.y
        self.sibling = (self.x, self.y, 1 - self.c)

    def _copy_a(self, p, t):
        k = self.base[p] + t
        return pltpu.make_async_remote_copy(
            src_ref=self.g[p].at[t, 1 - self.c], dst_ref=self.recv_a[p].at[t], send_sem=self.sa.at[k],
            recv_sem=self.ra.at[k], device_id=self.sibling, device_id_type=MESH)

    def _fetch(self, p, t):
        return pltpu.make_async_copy(self.g[p].at[t, self.c], self.va[p].at[t], self.la.at[self.base[p] + t])

    def _slot(self, owner):
        rel = jnp.bitwise_xor(self.mychip, owner)
        return jnp.where(rel == 2, 0, jnp.where(rel == 1, 1, 2))

    def _copy_b(self, p, t, slot):
        owner = self.owners[p][t]
        return pltpu.make_async_remote_copy(
            src_ref=self.tsend[p].at[t], dst_ref=self.recv_b[p].at[slot], send_sem=self.sb.at[self.base[p] + t],
            recv_sem=self.rb.at[3 * p + slot], device_id=(owner // 2, owner % 2, self.c), device_id_type=MESH)

    def _copy_c(self, p, half):
        return pltpu.make_async_remote_copy(
            src_ref=self.fin[p], dst_ref=self.out[p].at[half], send_sem=self.sc.at[p], recv_sem=self.rc.at[p],
            device_id=self.sibling, device_id_type=MESH)

    def _local_c(self, p):
        return pltpu.make_async_copy(self.fin[p], self.out[p].at[self.c], self.lc.at[p])

    def start(self):
        for p in range(self.n):
            for t in range(len(self.owners[p])):
                self._copy_a(p, t).start()
                self._fetch(p, t).start()

    def exchange(self):
        for p in range(self.n):
            for t, owner in enumerate(self.owners[p]):
                self._copy_a(p, t).wait_recv()
                self._fetch(p, t).wait()
                mine = self.mychip == owner

                @pl.when(mine)
                def _():
                    self.own[p][...] = self.va[p][t] + self.recv_a[p][t]

                @pl.when(jnp.logical_not(mine))
                def _():
                    self.tsend[p][t] = (self.va[p][t] + self.recv_a[p][t]).astype(BF16)
                    self._copy_b(p, t, self._slot(owner)).start()

    def combine(self):
        for p in range(self.n):
            for t, owner in enumerate(self.owners[p]):
                @pl.when(self.mychip == owner)
                def _():
                    acc = self.own[p][...]
                    for j in range(3):
                        self._copy_b(p, t, j).wait_recv()
                        acc = acc + self.recv_b[p][j].astype(F32)
                    self.fin[p][...] = acc
                    self._local_c(p).start()
                    self._copy_c(p, self.c).start()

    def finish(self):
        for p in range(self.n):
            for t, owner in enumerate(self.owners[p]):
                self._copy_a(p, t).wait_send()
                mine = self.mychip == owner

                @pl.when(mine)
                def _():
                    self._copy_c(p, 1 - self.c).wait_recv()
                    self._copy_c(p, self.c).wait_send()
                    self._local_c(p).wait()

                @pl.when(jnp.logical_not(mine))
                def _():
                    self._copy_b(p, t, self._slot(owner)).wait_send()


def _split_fused(refs, n_in, n_out, n_scr, fused):
    if fused is None:
        return refs[:n_in], refs[n_in:n_in + n_out], refs[n_in + n_out:]
    fi, fo = len(fused.in_specs), len(fused.out_specs)
    ins, rest = refs[:n_in], refs[n_in:]
    f_ins, rest = rest[:fi], rest[fi:]
    outs, rest = rest[:n_out], rest[n_out:]
    f_outs, rest = rest[:fo], rest[fo:]
    scr, f_scr = rest[:n_scr], rest[n_scr:]
    fused.bind(f_ins, f_outs, f_scr)
    return ins, outs, scr


def _run_phases(fused, step, at_steps, before):
    if fused is None:
        return
    for phase, (at, first) in zip(fused.phases, at_steps):
        if first == before:
            pl.when(step == at)(phase)


def _front(c_pad, c_ctx_rows, w_mod, b_cols, w_in_t_loc):
    ncol = w_mod.shape[1]
    gather = _ChipGather([w_in_t_loc])

    def body(c_ref, cctx_ref, w_ref, b_ref, win_ref, sc_ref, modg_ref, wing_ref,
             call_ref, ag_send, ag_recv, ag_local, m_send, m_recv, *g_scr):
        gather.bind([win_ref], [wing_ref], g_scr)
        gather.start()
        _gather8_in_vmem(c_ref, call_ref, ag_send, ag_recv, ag_local)
        x, y, c = lax.axis_index("x"), lax.axis_index("y"), lax.axis_index("c")
        chips = [(1 - x, y), (x, 1 - y), (1 - x, 1 - y)]
        mychip = 2 * x + y
        rows = jnp.concatenate([call_ref[dv] for dv in range(8)] + [cctx_ref[...]], axis=0)
        sc = rows * _sigmoid(rows)
        sc_ref[...] = sc
        modg_ref[mychip] = jnp.dot(sc, w_ref[...], preferred_element_type=F32,
                                   precision=lax.Precision.HIGHEST) + b_ref[...]

        def mcopy(j, chip_idx, to):
            return pltpu.make_async_remote_copy(
                src_ref=modg_ref.at[chip_idx], dst_ref=modg_ref.at[chip_idx], send_sem=m_send.at[j],
                recv_sem=m_recv.at[j], device_id=to, device_id_type=MESH)

        sends = [mcopy(j, mychip, (*chip, c)) for j, chip in enumerate(chips)]
        for cp in sends:
            cp.start()
        for j, (cx, cy) in enumerate(chips):
            mcopy(j, 2 * cx + cy, (x, y, c)).wait_recv()
        gather.forward()
        gather.finish()
        for cp in sends:
            cp.wait_send()

    vm = pl.BlockSpec(memory_space=pltpu.VMEM)
    return pl.pallas_call(
        body, name="front_exchange",
        out_shape=[jax.ShapeDtypeStruct((80, D_MODEL), F32), jax.ShapeDtypeStruct((4, 80, ncol), F32)] + gather.out_shape,
        in_specs=[vm, vm, vm, vm] + gather.in_specs, out_specs=[vm, vm] + gather.out_specs,
        scratch_shapes=[pltpu.VMEM((8, 8, D_MODEL), F32), pltpu.SemaphoreType.DMA((7,)), pltpu.SemaphoreType.DMA((7,)),
                        pltpu.SemaphoreType.DMA, pltpu.SemaphoreType.DMA((3,)), pltpu.SemaphoreType.DMA((3,))]
        + gather.scratch,
        compiler_params=_params(),
    )(c_pad, c_ctx_rows, w_mod, b_cols, w_in_t_loc)


def _tail_exchange(packed, sc_rows, w_mod, bsz, ctx_row):
    d = D_MODEL
    ncol = w_mod.shape[1]

    def body(p_ref, sc_ref, w_ref, total_ref, gw_ref, gcc_ref, gat_ref, dm_ref, part_ref,
             ag_send, ag_recv, ag_local, g_send, g_recv):
        _gather8_in_vmem(p_ref, gat_ref, ag_send, ag_recv, ag_local)
        acc = gat_ref[0]
        for dv in range(1, 8):
            acc = acc + gat_ref[dv]
        total_ref[...] = acc
        x, y, c = lax.axis_index("x"), lax.axis_index("y"), lax.axis_index("c")
        chips = [(1 - x, y), (x, 1 - y), (1 - x, 1 - y)]
        mychip = 2 * x + y
        dm_ref[...] = jnp.zeros(dm_ref.shape, F32)
        for k in range(4):
            @pl.when(mychip == k)
            def _():
                spans = [(seg, max(k * ncol, seg * d) - seg * d, min((k + 1) * ncol, (seg + 1) * d) - seg * d)
                         for seg in range(3) if k * ncol < (seg + 1) * d and (k + 1) * ncol > seg * d]
                for dv in range(8):
                    for b in range(bsz):
                        dm_ref[8 * dv + b:8 * dv + b + 1, :] = jnp.concatenate(
                            [gat_ref[dv, 3 * b + seg:3 * b + seg + 1, lo:hi] for seg, lo, hi in spans], axis=1)
                dm_ref[64:65, :] = jnp.concatenate(
                    [total_ref[ctx_row + seg:ctx_row + seg + 1, lo:hi] if seg < 2 else jnp.zeros((1, hi - lo), F32)
                     for seg, lo, hi in spans], axis=1)

        dm = dm_ref[...]
        gw_ref[...] = lax.dot_general(sc_ref[...], dm, TN, preferred_element_type=F32,
                                      precision=lax.Precision.HIGHEST)
        part_ref[mychip] = lax.dot_general(dm[64:72, :], w_ref[...], NT, preferred_element_type=F32,
                                           precision=lax.Precision.HIGHEST)

        def gcopy(j, chip_idx, to):
            return pltpu.make_async_remote_copy(
                src_ref=part_ref.at[chip_idx], dst_ref=part_ref.at[chip_idx], send_sem=g_send.at[j],
                recv_sem=g_recv.at[j], device_id=to, device_id_type=MESH)

        sends = [gcopy(j, mychip, (*chip, c)) for j, chip in enumerate(chips)]
        for cp in sends:
            cp.start()
        for j, (cx, cy) in enumerate(chips):
            gcopy(j, 2 * cx + cy, (x, y, c)).wait_recv()
        for cp in sends:
            cp.wait_send()
        gcc_ref[...] = (part_ref[0] + part_ref[1]) + (part_ref[2] + part_ref[3])

    return pl.pallas_call(
        body, name="tail_exchange",
        out_shape=[jax.ShapeDtypeStruct(packed.shape, F32), jax.ShapeDtypeStruct((d, ncol), F32),
                   jax.ShapeDtypeStruct((8, d), F32)],
        scratch_shapes=[pltpu.VMEM((8,) + packed.shape, F32), pltpu.VMEM((80, ncol), F32), pltpu.VMEM((4, 8, d), F32),
                        pltpu.SemaphoreType.DMA((7,)), pltpu.SemaphoreType.DMA((7,)), pltpu.SemaphoreType.DMA,
                        pltpu.SemaphoreType.DMA((3,)), pltpu.SemaphoreType.DMA((3,))],
        compiler_params=_params(),
    )(packed, sc_rows, w_mod)


def _bcast_spec(arr):
    if arr.shape[0] == 1:
        return pl.BlockSpec((1, 1, arr.shape[2]), lambda b, i: (0, 0, 0))
    return pl.BlockSpec((1, 1, arr.shape[2]), lambda b, i: (b, 0, 0))


def _norm_inproj(x, shift, scale1p, norm_w, w_t, splits, tm, name):
    bsz, s, d = x.shape

    def body(x_ref, sh_ref, sc_ref, nw_ref, w_ref, u_ref, *out_refs):
        xv = x_ref[0]
        rstd = lax.rsqrt(jnp.mean(xv * xv, axis=-1, keepdims=True) + EPS)
        u = (xv * rstd * nw_ref[...]) * sc_ref[0] + sh_ref[0]
        ub = u.astype(BF16)
        u_ref[0] = ub
        for (lo, hi), o_ref in zip(splits, out_refs):
            o_ref[0] = lax.dot_general(ub, w_ref[lo:hi, :], NT, preferred_element_type=F32)

    tok = lambda w: pl.BlockSpec((1, tm, w), lambda b, i: (b, i, 0))
    return pl.pallas_call(
        body, name=name, grid=(bsz, s // tm),
        in_specs=[tok(d), _bcast_spec(shift), _bcast_spec(scale1p), pl.BlockSpec((1, d), lambda b, i: (0, 0)),
                  pl.BlockSpec(w_t.shape, lambda b, i: (0, 0))],
        out_specs=[tok(d)] + [tok(hi - lo) for lo, hi in splits],
        out_shape=[jax.ShapeDtypeStruct((bsz, s, d), BF16)]
        + [jax.ShapeDtypeStruct((bsz, s, hi - lo), F32) for lo, hi in splits],
        compiler_params=_params(2),
    )(x, shift, scale1p, norm_w, w_t)


def _dup_heads(kv, lo_mask):
    r = pltpu.roll(kv, HEAD_DIM, 1)
    return jnp.where(lo_mask, kv, r), jnp.where(lo_mask, r, kv)


def _qkv_prep(qkv, cos, sin, qnw, knw, bd512, bd128, ts, row0):
    bsz, s, _ = qkv.shape

    def body(p_ref, cos_ref, sin_ref, qnw_ref, knw_ref, bd512_ref, bd128_ref, q_ref, k_ref, v_ref, kt_ref):
        lo_mask = _lo_mask(ts)
        cos_t, sin_t = cos_ref[...], sin_ref[...]
        qp = p_ref[0, :, 0:512]
        qn = qp * lax.rsqrt(_seg_mean(qp * qp, bd512_ref[...]) + EPS) * qnw_ref[...]
        qr = qn * _tile_lanes(cos_t, 4) + _partner(qn) * _tile_lanes(sin_t, 4)
        q_ref[0] = (qr * (1.0 / math.sqrt(HEAD_DIM))).astype(BF16)
        kp = p_ref[0, :, 512:640]
        kn = kp * lax.rsqrt(_seg_mean(kp * kp, bd128_ref[...]) + EPS) * knw_ref[...]
        kr = kn * cos_t + _partner(kn) * sin_t
        k0, k1 = _dup_heads(kr, lo_mask)
        k_ref[0, 0] = k0.astype(BF16)
        k_ref[0, 1] = k1.astype(BF16)
        v0, v1 = _dup_heads(p_ref[0, :, 640:768], lo_mask)
        v_ref[0, 0] = v0.astype(BF16)
        v_ref[0, 1] = v1.astype(BF16)
        kt_ref[0] = kr.T.astype(BF16)

    const = lambda a: pl.BlockSpec(a.shape, lambda b, i: (0,) * a.ndim)
    kv_spec = pl.BlockSpec((1, 2, ts, 128), lambda b, i: (b, 0, i + row0 // ts, 0))
    return pl.pallas_call(
        body, name="qkv_prep", grid=(bsz, s // ts),
        in_specs=[pl.BlockSpec((1, ts, 768), lambda b, i: (b, i, 0)),
                  pl.BlockSpec((ts, 128), lambda b, i: (i, 0)), pl.BlockSpec((ts, 128), lambda b, i: (i, 0)),
                  const(qnw), const(knw), const(bd512), const(bd128)],
        out_specs=[pl.BlockSpec((1, ts, 512), lambda b, i: (b, i, 0)), kv_spec, kv_spec,
                   pl.BlockSpec((1, 128, ts), lambda b, i: (b, 0, i + row0 // ts))],
        out_shape=[jax.ShapeDtypeStruct((bsz, s, 512), BF16), jax.ShapeDtypeStruct((bsz, 2, row0 + s, 128), BF16),
                   jax.ShapeDtypeStruct((bsz, 2, row0 + s, 128), BF16),
                   jax.ShapeDtypeStruct((bsz, 128, row0 + s), BF16)],
        compiler_params=_params(2),
    )(qkv, cos, sin, qnw, knw, bd512, bd128)


def _ctx_kv_prep(pc, knw, bd128, k2, v2, kt):
    bsz, cl, _ = pc.shape

    def body(p_ref, knw_ref, bd128_ref, k_in, v_in, kt_in, k_ref, v_ref, kt_ref):
        lo_mask = _lo_mask(cl)
        kp = p_ref[0, :, 0:128]
        kn = kp * lax.rsqrt(_seg_mean(kp * kp, bd128_ref[...]) + EPS) * knw_ref[...]
        k0, k1 = _dup_heads(kn, lo_mask)
        k_ref[0, 0] = k0.astype(BF16)
        k_ref[0, 1] = k1.astype(BF16)
        v0, v1 = _dup_heads(p_ref[0, :, 128:256], lo_mask)
        v_ref[0, 0] = v0.astype(BF16)
        v_ref[0, 1] = v1.astype(BF16)
        kt_ref[0] = kn.T.astype(BF16)

    const = lambda a: pl.BlockSpec(a.shape, lambda b: (0,) * a.ndim)
    kv_spec = pl.BlockSpec((1, 2, cl, 128), lambda b: (b, 0, 0, 0))
    anyspec = pl.BlockSpec(memory_space=pl.ANY)
    return pl.pallas_call(
        body, name="ctx_kv_prep", grid=(bsz,),
        in_specs=[pl.BlockSpec((1, cl, 256), lambda b: (b, 0, 0)), const(knw), const(bd128), anyspec, anyspec, anyspec],
        out_specs=[kv_spec, kv_spec, pl.BlockSpec((1, 128, cl), lambda b: (b, 0, 0))],
        out_shape=[jax.ShapeDtypeStruct(a.shape, BF16) for a in (k2, v2, kt)],
        input_output_aliases={3: 0, 4: 1, 5: 2},
        compiler_params=_params(1),
    )(pc, knw, bd128, k2, v2, kt)


def _attn_forward(q, k2, v2, tq, fused):
    bsz, s, _ = q.shape
    sk = k2.shape[2]
    nq = s // tq
    total = bsz * N_KV * nq
    at_steps = [(0, True), (total // 4, True), (total - 1, False)]

    def body(*refs):
        (q_ref, k_ref, v_ref), (o_ref, lse_ref), _ = _split_fused(refs, 3, 2, 0, fused)
        step = (pl.program_id(0) * N_KV + pl.program_id(1)) * nq + pl.program_id(2)
        _run_phases(fused, step, at_steps, True)
        kk = k_ref[0, 0]
        lo_mask = _lo_mask(tq)
        vv = v_ref[0, 0]
        v1 = jnp.where(_lo_mask(sk), vv, jnp.ones_like(vv))
        for j in range(2):
            qp = q_ref[0, :, 128 * j:128 * (j + 1)]
            outs, lses = [], []
            for half in range(2):
                sel = lo_mask if half == 0 else jnp.logical_not(lo_mask)
                qs = jnp.where(sel, qp, jnp.zeros_like(qp))
                sc = lax.dot_general(qs, kk, NT, preferred_element_type=F32)
                m = jnp.max(sc, axis=-1, keepdims=True)
                o = jnp.dot(jnp.exp(sc - m).astype(BF16), v1, preferred_element_type=F32)
                l = pltpu.roll(o, HEAD_DIM, 1)
                outs.append(o / l)
                lses.append(m + jnp.log(l))
            o_ref[0, :, 128 * j:128 * (j + 1)] = jnp.where(lo_mask, outs[0], pltpu.roll(outs[1], HEAD_DIM, 1))
            lse_ref[0, :, 128 * j:128 * (j + 1)] = jnp.where(lo_mask, lses[0], pltpu.roll(lses[1], HEAD_DIM, 1))
        _run_phases(fused, step, at_steps, False)

    q_spec = pl.BlockSpec((1, tq, 256), lambda b, g, i: (b, i, g))
    kv_spec = pl.BlockSpec((1, 1, sk, 128), lambda b, g, i: (b, g, 0, 0))
    return pl.pallas_call(
        body, name="attn_forward", grid=(bsz, N_KV, nq),
        in_specs=[q_spec, kv_spec, kv_spec] + fused.in_specs, out_specs=[q_spec, q_spec] + fused.out_specs,
        out_shape=[jax.ShapeDtypeStruct((bsz, s, 512), F32)] * 2 + fused.out_shape,
        scratch_shapes=fused.scratch,
        compiler_params=_params(3),
    )(q, k2, v2, *fused.arrs)


def _halo_specs(width, ts, s):
    r = ts // HALO
    last = s // HALO - 1
    return [pl.BlockSpec((1, ts, width), lambda b, i: (b, i, 0)),
            pl.BlockSpec((1, HALO, width), lambda b, i: (b, jnp.maximum(i * r - 1, 0), 0)),
            pl.BlockSpec((1, HALO, width), lambda b, i: (b, jnp.minimum((i + 1) * r, last), 0))]


def _fill_ext(ext_ref, cur, prev, nxt, i, n_tiles, ts):
    ext_ref[0:HALO, :] = jnp.where(i > 0, prev, jnp.zeros_like(prev))
    ext_ref[HALO:HALO + ts, :] = cur
    ext_ref[HALO + ts:2 * HALO + ts, :] = jnp.where(i < n_tiles - 1, nxt, jnp.zeros_like(nxt))


def _fill_shifted(sh_ref, ext_ref, ts):
    n = ts + 2 * HALO - 8
    for r in range(1, 8):
        sh_ref[r - 1, 0:n, :] = ext_ref[pl.ds(r, n), :]


def _window(sh_ref, ext_ref, off, rows, r0=0):
    q, r = divmod(off, 8)
    if r == 0:
        return ext_ref[pl.ds(r0 + off, rows), :]
    return sh_ref[r - 1, pl.ds(r0 + 8 * q, rows), :]


def _conv_forward(ga, gg, conv_w, conv_b, ln_w, ln_b, w_pw, b_pw, ts):
    bsz, s, dc = ga.shape
    n_tiles = s // ts

    def body(a_ref, ap_ref, an_ref, g_ref, gp_ref, gn_ref, cw_ref, cb_ref, lw_ref, lb_ref, wp_ref, bp_ref,
             y_ref, cv_ref, ext_ref, sh_ref):
        i = pl.program_id(1)
        glu = lambda a, g: a * _sigmoid(g)
        _fill_ext(ext_ref, glu(a_ref[0], g_ref[0]), glu(ap_ref[0], gp_ref[0]), glu(an_ref[0], gn_ref[0]), i, n_tiles, ts)
        _fill_shifted(sh_ref, ext_ref, ts)
        acc = jnp.broadcast_to(cb_ref[...], (ts, dc))
        for j in range(CONV_WIDTH):
            acc = acc + cw_ref[j:j + 1, :] * _window(sh_ref, ext_ref, HALO - CONV_PAD + j, ts)
        y_ref[0] = acc
        mu = jnp.mean(acc, axis=-1, keepdims=True)
        yc = acc - mu
        var = jnp.mean(yc * yc, axis=-1, keepdims=True)
        yn = yc * lax.rsqrt(var + EPS) * lw_ref[...] + lb_ref[...]
        ys = yn * _sigmoid(yn)
        cv_ref[0] = jnp.dot(ys.astype(BF16), wp_ref[...], preferred_element_type=F32) + bp_ref[...]

    const = lambda a: pl.BlockSpec(a.shape, lambda b, i: (0,) * a.ndim)
    return pl.pallas_call(
        body, name="conv_forward", grid=(bsz, n_tiles),
        in_specs=_halo_specs(dc, ts, s) + _halo_specs(dc, ts, s)
        + [const(conv_w), const(conv_b), const(ln_w), const(ln_b), const(w_pw), const(b_pw)],
        out_specs=[pl.BlockSpec((1, ts, dc), lambda b, i: (b, i, 0))] * 2,
        out_shape=[jax.ShapeDtypeStruct((bsz, s, dc), F32)] * 2,
        scratch_shapes=[pltpu.VMEM((ts + 2 * HALO, dc), F32), pltpu.VMEM((7, ts + 2 * HALO, dc), F32)],
        compiler_params=_params(2),
    )(ga, ga, ga, gg, gg, gg, conv_w, conv_b, ln_w, ln_b, w_pw, b_pw)


def _outproj_loss(x, target, gate, o, za, cv, zc, w_out, tm):
    bsz, s, d = x.shape

    def body(x_ref, t_ref, gate_ref, o_ref, za_ref, cv_ref, zc_ref, w_ref,
             loss_ref, dh_ref, do_ref, dza_ref, dcv_ref, dzc_ref, dgate_ref, gw_ref):
        b, i = pl.program_id(0), pl.program_id(1)
        ov, cvv = o_ref[0], cv_ref[0]
        silu_a, dsilu_a = _silu_and_grad(za_ref[0])
        silu_c, dsilu_c = _silu_and_grad(zc_ref[0])
        mix = jnp.concatenate([ov * silu_a, cvv * silu_c], axis=1).astype(BF16)
        out = jnp.dot(mix, w_ref[...], preferred_element_type=F32)
        gate_v = gate_ref[0]
        err = x_ref[0] + gate_v * out - t_ref[0]
        dh = err * (1.0 / d)
        dh_ref[0] = dh
        dout = (dh * gate_v).astype(BF16)
        dmix = lax.dot_general(dout, w_ref[...], NT, preferred_element_type=F32)
        gw = lax.dot_general(mix, dout, TN, preferred_element_type=F32)
        dg = jnp.sum(dh * out, axis=0, keepdims=True)
        sq = jnp.sum(err * err)

        @pl.when(jnp.logical_and(b == 0, i == 0))
        def _():
            gw_ref[...] = gw

        @pl.when(jnp.logical_or(b > 0, i > 0))
        def _():
            gw_ref[...] += gw

        @pl.when(i == 0)
        def _():
            dgate_ref[0] = dg
            loss_ref[...] = jnp.zeros(loss_ref.shape, F32) + sq

        @pl.when(i > 0)
        def _():
            dgate_ref[0] += dg
            loss_ref[...] += sq

        dma, dmc = dmix[:, :D_ATTN], dmix[:, D_ATTN:]
        do_ref[0] = dma * silu_a
        dza_ref[0] = (dma * ov * dsilu_a).astype(BF16)
        dcv_ref[0] = dmc * silu_c
        dzc_ref[0] = (dmc * cvv * dsilu_c).astype(BF16)

    tok = lambda w: pl.BlockSpec((1, tm, w), lambda b, i: (b, i, 0))
    return pl.pallas_call(
        body, name="outproj_loss", grid=(bsz, s // tm),
        in_specs=[tok(d), tok(d), _bcast_spec(gate), tok(512), tok(512), tok(512), tok(512),
                  pl.BlockSpec(w_out.shape, lambda b, i: (0, 0))],
        out_specs=[pl.BlockSpec((1, 8, 128), lambda b, i: (b, 0, 0)), tok(d), tok(512), tok(512), tok(512), tok(512),
                   pl.BlockSpec((1, 1, d), lambda b, i: (b, 0, 0)), pl.BlockSpec((d, d), lambda b, i: (0, 0))],
        out_shape=[jax.ShapeDtypeStruct((bsz, 8, 128), F32), jax.ShapeDtypeStruct((bsz, s, d), F32),
                   jax.ShapeDtypeStruct((bsz, s, 512), F32), jax.ShapeDtypeStruct((bsz, s, 512), BF16),
                   jax.ShapeDtypeStruct((bsz, s, 512), F32), jax.ShapeDtypeStruct((bsz, s, 512), BF16),
                   jax.ShapeDtypeStruct((bsz, 1, d), F32), jax.ShapeDtypeStruct((d, d), F32)],
        compiler_params=_params(2),
    )(x, target, gate, o, za, cv, zc, w_out)


def _conv_token_backward(dcv, y, ln_w, ln_b, w_pw, tm):
    bsz, s, dc = dcv.shape

    def body(dcv_ref, y_ref, lw_ref, lb_ref, wp_ref, dy_ref, gwp_ref, st_ref):
        b, i = pl.program_id(0), pl.program_id(1)
        yv, dcvv = y_ref[0], dcv_ref[0]
        mu = jnp.mean(yv, axis=-1, keepdims=True)
        yc = yv - mu
        rstd = lax.rsqrt(jnp.mean(yc * yc, axis=-1, keepdims=True) + EPS)
        yhat = yc * rstd
        yn = yhat * lw_ref[...] + lb_ref[...]
        ys, dsilu = _silu_and_grad(yn)
        dcvb = dcvv.astype(BF16)
        gwp = lax.dot_general(ys.astype(BF16), dcvb, TN, preferred_element_type=F32)
        dys = lax.dot_general(dcvb, wp_ref[...], NT, preferred_element_type=F32)
        dyn = dys * dsilu
        dyhat = dyn * lw_ref[...]
        dy = rstd * (dyhat - jnp.mean(dyhat, axis=-1, keepdims=True)
                     - yhat * jnp.mean(dyhat * yhat, axis=-1, keepdims=True))
        dy_ref[0] = dy
        red = lambda v: jnp.sum(v, axis=0, keepdims=True)
        stats = jnp.concatenate([red(dcvv), red(dyn * yhat), red(dyn), red(dy), jnp.zeros((4, dc), F32)], axis=0)
        first = jnp.logical_and(b == 0, i == 0)

        @pl.when(first)
        def _():
            gwp_ref[...] = gwp
            st_ref[...] = stats

        @pl.when(jnp.logical_not(first))
        def _():
            gwp_ref[...] += gwp
            st_ref[...] += stats

    tok = pl.BlockSpec((1, tm, dc), lambda b, i: (b, i, 0))
    const = lambda a: pl.BlockSpec(a.shape, lambda b, i: (0,) * a.ndim)
    return pl.pallas_call(
        body, name="conv_token_backward", grid=(bsz, s // tm),
        in_specs=[tok, tok, const(ln_w), const(ln_b), const(w_pw)],
        out_specs=[tok, pl.BlockSpec((dc, dc), lambda b, i: (0, 0)), pl.BlockSpec((8, dc), lambda b, i: (0, 0))],
        out_shape=[jax.ShapeDtypeStruct((bsz, s, dc), F32), jax.ShapeDtypeStruct((dc, dc), F32),
                   jax.ShapeDtypeStruct((8, dc), F32)],
        compiler_params=_params(2),
    )(dcv, y, ln_w, ln_b, w_pw)


def _conv_backward(dy, ga, gg, conv_w, ts):
    bsz, s, dc = dy.shape
    n_tiles = s // ts

    def body(dy_ref, dyp_ref, dyn_ref, a_ref, ap_ref, an_ref, g_ref, gp_ref, gn_ref, cw_ref,
             da_ref, dg_ref, gcw_ref, dyext_ref, ugext_ref, dysh_ref, ugsh_ref, dug_ref, gacc_ref):
        b, i = pl.program_id(0), pl.program_id(1)
        av, sg = a_ref[0], _sigmoid(g_ref[0])
        glu = lambda a, g: a * _sigmoid(g)
        _fill_ext(dyext_ref, dy_ref[0], dyp_ref[0], dyn_ref[0], i, n_tiles, ts)
        _fill_ext(ugext_ref, av * sg, glu(ap_ref[0], gp_ref[0]), glu(an_ref[0], gn_ref[0]), i, n_tiles, ts)
        _fill_shifted(dysh_ref, dyext_ref, ts)
        _fill_shifted(ugsh_ref, ugext_ref, ts)
        gacc_ref[...] = jnp.zeros(gacc_ref.shape, F32)

        def row_block(r, carry):
            r0 = pl.multiple_of(r * CONV_ROWS, CONV_ROWS)
            dyb = dy_ref[0, pl.ds(r0, CONV_ROWS), :]
            acc = jnp.zeros((CONV_ROWS, dc), F32)
            for j in range(CONV_WIDTH):
                acc = acc + cw_ref[j:j + 1, :] * _window(dysh_ref, dyext_ref, HALO + CONV_PAD - j, CONV_ROWS, r0)
                prod = dyb * _window(ugsh_ref, ugext_ref, HALO - CONV_PAD + j, CONV_ROWS, r0)
                part = prod[0:8, :]
                for k in range(8, CONV_ROWS, 8):
                    part = part + prod[k:k + 8, :]
                gacc_ref[j] += part
            dug_ref[pl.ds(r0, CONV_ROWS), :] = acc
            return carry

        lax.fori_loop(0, ts // CONV_ROWS, row_block, 0)
        dug = dug_ref[...]
        gcw = jnp.sum(gacc_ref[...], axis=1)
        first = jnp.logical_and(b == 0, i == 0)

        @pl.when(first)
        def _():
            gcw_ref[...] = gcw

        @pl.when(jnp.logical_not(first))
        def _():
            gcw_ref[...] += gcw

        da_ref[0] = (dug * sg).astype(BF16)
        dg_ref[0] = (dug * av * sg * (1.0 - sg)).astype(BF16)

    tok = pl.BlockSpec((1, ts, dc), lambda b, i: (b, i, 0))
    return pl.pallas_call(
        body, name="conv_backward", grid=(bsz, n_tiles),
        in_specs=_halo_specs(dc, ts, s) + _halo_specs(dc, ts, s) + _halo_specs(dc, ts, s)
        + [pl.BlockSpec(conv_w.shape, lambda b, i: (0, 0))],
        out_specs=[tok, tok, pl.BlockSpec((32, dc), lambda b, i: (0, 0))],
        out_shape=[jax.ShapeDtypeStruct((bsz, s, dc), BF16), jax.ShapeDtypeStruct((bsz, s, dc), BF16),
                   jax.ShapeDtypeStruct((32, dc), F32)],
        scratch_shapes=[pltpu.VMEM((ts + 2 * HALO, dc), F32), pltpu.VMEM((ts + 2 * HALO, dc), F32),
                        pltpu.VMEM((7, ts + 2 * HALO, dc), F32), pltpu.VMEM((7, ts + 2 * HALO, dc), F32),
                        pltpu.VMEM((ts, dc), F32), pltpu.VMEM((32, 8, dc), F32)],
        compiler_params=_params(2),
    )(dy, dy, dy, ga, ga, ga, gg, gg, gg, conv_w)


def _attn_backward(q, k2, v2, kt, o, do, lse, tq, fused):
    bsz, s, _ = q.shape
    sk = k2.shape[2]
    scale = 1.0 / math.sqrt(HEAD_DIM)
    nq = s // tq
    total = bsz * N_KV * nq
    at_steps = [(0, True), (total // 5, True), (total // 2, True), (total - 1, False)]

    def body(*refs):
        (q_ref, k_ref, v_ref, kt_ref, o_ref, do_ref, lse_ref), (dq_ref, dk_ref, dv_ref), _ = _split_fused(
            refs, 7, 3, 0, fused)
        g, i = pl.program_id(1), pl.program_id(2)
        step = (pl.program_id(0) * N_KV + g) * nq + i
        _run_phases(fused, step, at_steps, True)
        kk, vv = k_ref[0, 0], v_ref[0, 0]
        kgt = kt_ref[0, pl.ds(pl.multiple_of(g * HEAD_DIM, HEAD_DIM), HEAD_DIM), :]
        lo_mask = _lo_mask(tq)
        dk_acc = jnp.zeros((HEAD_DIM, sk), F32)
        dv_acc = jnp.zeros((HEAD_DIM, sk), F32)
        for j in range(2):
            cols = slice(128 * j, 128 * (j + 1))
            qp, dop, lsep = q_ref[0, :, cols], do_ref[0, :, cols], lse_ref[0, :, cols]
            dprod = dop * o_ref[0, :, cols]
            q_t = qp.astype(F32).T.astype(BF16)
            do_t = dop.T.astype(BF16)
            dq_t = []
            for half in range(2):
                sel = lo_mask if half == 0 else jnp.logical_not(lo_mask)
                rows = slice(HEAD_DIM * half, HEAD_DIM * (half + 1))
                qs = jnp.where(sel, qp, jnp.zeros_like(qp))
                dos = jnp.where(sel, dop, 0.0).astype(BF16)
                lse_h = jnp.max(jnp.where(sel, lsep, -jnp.inf), axis=-1, keepdims=True)
                delta = jnp.sum(jnp.where(sel, dprod, 0.0), axis=-1, keepdims=True)
                sc = lax.dot_general(qs, kk, NT, preferred_element_type=F32)
                p = jnp.exp(sc - lse_h)
                dp = lax.dot_general(dos, vv, NT, preferred_element_type=F32)
                ds = (p * (dp - delta)).astype(BF16)
                dv_acc = dv_acc + jnp.dot(do_t[rows, :], p.astype(BF16), preferred_element_type=F32)
                dk_acc = dk_acc + jnp.dot(q_t[rows, :], ds, preferred_element_type=F32)
                dq_t.append(lax.dot_general(kgt, ds, NT, preferred_element_type=F32))
            dq_ref[0, :, cols] = (jnp.concatenate(dq_t, axis=0) * scale).T

        @pl.when(i == 0)
        def _():
            dk_ref[0, 0] = dk_acc
            dv_ref[0, 0] = dv_acc

        @pl.when(i > 0)
        def _():
            dk_ref[0, 0] += dk_acc
            dv_ref[0, 0] += dv_acc

        _run_phases(fused, step, at_steps, False)

    q_spec = pl.BlockSpec((1, tq, 256), lambda b, g, i: (b, i, g))
    kv_spec = pl.BlockSpec((1, 1, sk, 128), lambda b, g, i: (b, g, 0, 0))
    acc_spec = pl.BlockSpec((1, 1, HEAD_DIM, sk), lambda b, g, i: (b, g, 0, 0))
    return pl.pallas_call(
        body, name="attn_backward", grid=(bsz, N_KV, nq),
        in_specs=[q_spec, kv_spec, kv_spec, pl.BlockSpec((1, 128, sk), lambda b, g, i: (b, 0, 0)), q_spec, q_spec,
                  q_spec] + fused.in_specs,
        out_specs=[q_spec, acc_spec, acc_spec] + fused.out_specs,
        out_shape=[jax.ShapeDtypeStruct((bsz, s, 512), F32), jax.ShapeDtypeStruct((bsz, 2, HEAD_DIM, sk), F32),
                   jax.ShapeDtypeStruct((bsz, 2, HEAD_DIM, sk), F32)] + fused.out_shape,
        scratch_shapes=fused.scratch,
        compiler_params=_params(3),
    )(q, k2, v2, kt, o, do, lse, *fused.arrs)


def _heads_to_lanes(acc_ref):
    return jnp.concatenate([acc_ref[0, 0], acc_ref[0, 1]], axis=0).T


def _norm_backward(dn, pre, w, bd):
    rstd = lax.rsqrt(_seg_mean(pre * pre, bd) + EPS)
    xhat = pre * rstd
    dxhat = dn * w
    return rstd * (dxhat - xhat * _seg_mean(dxhat * xhat, bd)), dn * xhat


def _qkv_backward(qkv, dq, dk2, dv2, cos, sin, qnw, knw, bd512, bd128, ts, row0):
    bsz, s, _ = qkv.shape

    def body(p_ref, dq_ref, dk_ref, dv_ref, cos_ref, sin_ref, qnw_ref, knw_ref, bd512_ref, bd128_ref, d_ref, gw_ref):
        b, i = pl.program_id(0), pl.program_id(1)
        lo_mask = _lo_mask(ts)
        cos_t, sin_t = cos_ref[...], sin_ref[...]
        dqr = dq_ref[0]
        dqn = dqr * _tile_lanes(cos_t, 4) + _partner(dqr * _tile_lanes(sin_t, 4))
        dqp, gq = _norm_backward(dqn, p_ref[0, :, 0:512], qnw_ref[...], bd512_ref[...])
        dkr = _heads_to_lanes(dk_ref)
        dkn = dkr * cos_t + _partner(dkr * sin_t)
        dkp, gk = _norm_backward(dkn, p_ref[0, :, 512:640], knw_ref[...], bd128_ref[...])
        dvp = _heads_to_lanes(dv_ref)
        d_ref[0] = jnp.concatenate([dqp, dkp, dvp], axis=1).astype(BF16)
        gk512 = jnp.concatenate([jnp.sum(gk, axis=0, keepdims=True), jnp.zeros((1, 384), F32)], axis=1)
        rows = jnp.concatenate([jnp.sum(gq, axis=0, keepdims=True), gk512, jnp.zeros((6, 512), F32)], axis=0)
        first = jnp.logical_and(b == 0, i == 0)

        @pl.when(first)
        def _():
            gw_ref[...] = rows

        @pl.when(jnp.logical_not(first))
        def _():
            gw_ref[...] += rows

    const = lambda a: pl.BlockSpec(a.shape, lambda b, i: (0,) * a.ndim)
    kv_spec = pl.BlockSpec((1, 2, HEAD_DIM, ts), lambda b, i: (b, 0, 0, i + row0 // ts))
    return pl.pallas_call(
        body, name="qkv_backward", grid=(bsz, s // ts),
        in_specs=[pl.BlockSpec((1, ts, 768), lambda b, i: (b, i, 0)), pl.BlockSpec((1, ts, 512), lambda b, i: (b, i, 0)),
                  kv_spec, kv_spec, pl.BlockSpec((ts, 128), lambda b, i: (i, 0)),
                  pl.BlockSpec((ts, 128), lambda b, i: (i, 0)), const(qnw), const(knw), const(bd512), const(bd128)],
        out_specs=[pl.BlockSpec((1, ts, 768), lambda b, i: (b, i, 0)), pl.BlockSpec((8, 512), lambda b, i: (0, 0))],
        out_shape=[jax.ShapeDtypeStruct((bsz, s, 768), BF16), jax.ShapeDtypeStruct((8, 512), F32)],
        compiler_params=_params(2),
    )(qkv, dq, dk2, dv2, cos, sin, qnw, knw, bd512, bd128)


def _ctx_kv_backward(pc, dk2, dv2, knw, bd128):
    bsz, cl, _ = pc.shape

    def body(p_ref, dk_ref, dv_ref, knw_ref, bd128_ref, d_ref, gw_ref):
        b = pl.program_id(0)
        lo_mask = _lo_mask(cl)
        dkn = _heads_to_lanes(dk_ref)
        dkp, gk = _norm_backward(dkn, p_ref[0, :, 0:128], knw_ref[...], bd128_ref[...])
        dvp = _heads_to_lanes(dv_ref)
        d_ref[0] = jnp.concatenate([dkp, dvp], axis=1).astype(BF16)
        rows = jnp.concatenate([jnp.sum(gk, axis=0, keepdims=True), jnp.zeros((7, 128), F32)], axis=0)

        @pl.when(b == 0)
        def _():
            gw_ref[...] = rows

        @pl.when(b > 0)
        def _():
            gw_ref[...] += rows

    const = lambda a: pl.BlockSpec(a.shape, lambda b: (0,) * a.ndim)
    kv_spec = pl.BlockSpec((1, 2, HEAD_DIM, cl), lambda b: (b, 0, 0, 0))
    return pl.pallas_call(
        body, name="ctx_kv_backward", grid=(bsz,),
        in_specs=[pl.BlockSpec((1, cl, 256), lambda b: (b, 0, 0)), kv_spec, kv_spec, const(knw), const(bd128)],
        out_specs=[pl.BlockSpec((1, cl, 256), lambda b: (b, 0, 0)), pl.BlockSpec((8, 128), lambda b: (0, 0))],
        out_shape=[jax.ShapeDtypeStruct((bsz, cl, 256), BF16), jax.ShapeDtypeStruct((8, 128), F32)],
        compiler_params=_params(1),
    )(pc, dk2, dv2, knw, bd128)


def _weight_grad(parts, u, init, tm, name):
    bsz, s, d = u.shape
    n_p = len(parts)
    nrows = sum(hi - lo for _, lo, hi in parts)

    def body(*refs):
        p_refs, u_ref = refs[:n_p], refs[n_p]
        gi_ref = refs[n_p + 1] if init is not None else None
        gw_ref = refs[-1]
        first = jnp.logical_and(pl.program_id(0) == 0, pl.program_id(1) == 0)
        dp = jnp.concatenate([r[0, :, lo:hi] for r, (_, lo, hi) in zip(p_refs, parts)], axis=1)
        gw = lax.dot_general(dp, u_ref[0], TN, preferred_element_type=F32)

        @pl.when(first)
        def _():
            gw_ref[...] = gw
            if init is not None:
                gw_ref[KV_LO:KV_HI, :] += gi_ref[...]

        @pl.when(jnp.logical_not(first))
        def _():
            gw_ref[...] += gw

    tok = lambda w: pl.BlockSpec((1, tm, w), lambda b, i: (b, i, 0))
    in_specs = [tok(a.shape[2]) for a, _, _ in parts] + [tok(d)]
    args = [a for a, _, _ in parts] + [u]
    if init is not None:
        in_specs.append(pl.BlockSpec(init.shape, lambda b, i: (0, 0)))
        args.append(init)
    return pl.pallas_call(
        body, name=name, grid=(bsz, s // tm), in_specs=in_specs,
        out_specs=pl.BlockSpec((nrows, d), lambda b, i: (0, 0)), out_shape=jax.ShapeDtypeStruct((nrows, d), F32),
        compiler_params=_params(2),
    )(*args)


def _inproj_backward(dps, x, dh, scale1p, norm_w, w_t, tm, name, fused=None):
    bsz, s, d = x.shape
    n_p = len(dps)
    shared = scale1p.shape[0] == 1
    with_dx = dh is not None
    n_in = n_p + (2 if with_dx else 1) + 3
    n_out = 3 if with_dx else 2
    total = bsz * (s // tm)
    at_steps = [(0, True), (total // 8, True), ((3 * total) // 4, True), (total - 1, False)]

    def body(*refs):
        ins, outs, _ = _split_fused(refs, n_in, n_out, 0, fused)
        dp_refs, x_ref = ins[:n_p], ins[n_p]
        dh_ref = ins[n_p + 1] if with_dx else None
        sc_ref, nw_ref, w_ref = ins[-3:]
        mod_ref, gnw_ref = outs[-2:]
        b, i = pl.program_id(0), pl.program_id(1)
        step = b * (s // tm) + i
        _run_phases(fused, step, at_steps, True)
        first = jnp.logical_and(b == 0, i == 0)
        dp = dp_refs[0][0] if n_p == 1 else jnp.concatenate([r[0] for r in dp_refs], axis=1)
        du = jnp.dot(dp, w_ref[...], preferred_element_type=F32)
        xv = x_ref[0]
        rstd = lax.rsqrt(jnp.mean(xv * xv, axis=-1, keepdims=True) + EPS)
        xhat = xv * rstd
        nw, sc = nw_ref[...], sc_ref[0]
        red = lambda v: jnp.sum(v, axis=0, keepdims=True)
        mod_rows = jnp.concatenate([red(du), red(du * (xhat * nw)), jnp.zeros((6, d), F32)], axis=0)
        gnw_rows = jnp.concatenate([red(du * sc * xhat), jnp.zeros((7, d), F32)], axis=0)
        mod_first = first if shared else i == 0

        @pl.when(mod_first)
        def _():
            mod_ref[0] = mod_rows

        @pl.when(jnp.logical_not(mod_first))
        def _():
            mod_ref[0] += mod_rows

        @pl.when(first)
        def _():
            gnw_ref[...] = gnw_rows

        @pl.when(jnp.logical_not(first))
        def _():
            gnw_ref[...] += gnw_rows

        if with_dx:
            dxhat = du * (nw * sc)
            outs[0][0] = dh_ref[0] + rstd * (dxhat - xhat * jnp.mean(dxhat * xhat, axis=-1, keepdims=True))
        _run_phases(fused, step, at_steps, False)

    tok = lambda w: pl.BlockSpec((1, tm, w), lambda b, i: (b, i, 0))
    in_specs = [tok(p.shape[2]) for p in dps] + [tok(d)]
    args = list(dps) + [x]
    if with_dx:
        in_specs.append(tok(d))
        args.append(dh)
    in_specs += [_bcast_spec(scale1p), pl.BlockSpec((1, d), lambda b, i: (0, 0)),
                 pl.BlockSpec(w_t.shape, lambda b, i: (0, 0))]
    args += [scale1p, norm_w, w_t]
    bm = scale1p.shape[0]
    mod_spec = pl.BlockSpec((1, 8, d), (lambda b, i: (0, 0, 0)) if shared else (lambda b, i: (b, 0, 0)))
    out_specs = [mod_spec, pl.BlockSpec((8, d), lambda b, i: (0, 0))]
    out_shape = [jax.ShapeDtypeStruct((bm, 8, d), F32), jax.ShapeDtypeStruct((8, d), F32)]
    if with_dx:
        out_specs.insert(0, tok(d))
        out_shape.insert(0, jax.ShapeDtypeStruct((bsz, s, d), F32))
    scratch = []
    if fused is not None:
        in_specs += fused.in_specs
        args += fused.arrs
        out_specs += fused.out_specs
        out_shape += fused.out_shape
        scratch = fused.scratch
    res = pl.pallas_call(
        body, name=name, grid=(bsz, s // tm), in_specs=in_specs, out_specs=out_specs, out_shape=out_shape,
        scratch_shapes=scratch, compiler_params=_params(2),
    )(*args)
    return list(res) if with_dx else [None] + list(res)


def _adamw_update(w_ref, g_ref, m_ref, v_ref, d_ref, nm_ref, nv_ref):
    gv = g_ref[...]
    mn = ADAM_B1 * m_ref[...] + (1.0 - ADAM_B1) * gv
    vn = ADAM_B2 * v_ref[...] + (1.0 - ADAM_B2) * (gv * gv)
    m_hat = mn / (1.0 - ADAM_B1 ** ADAM_STEP)
    v_hat = vn / (1.0 - ADAM_B2 ** ADAM_STEP)
    d_ref[...] = -ADAM_LR * (m_hat / (jnp.sqrt(v_hat) + ADAM_EPS) + ADAM_WD * w_ref[...])
    nm_ref[...] = mn
    nv_ref[...] = vn


def _adamw_small(ws, gs, ms, vs):
    n = len(ws)

    def body(*refs):
        ins, outs = refs[:4 * n], refs[4 * n:]
        for k in range(n):
            _adamw_update(ins[k], ins[n + k], ins[2 * n + k], ins[3 * n + k], outs[3 * k], outs[3 * k + 1],
                          outs[3 * k + 2])

    res = pl.pallas_call(
        body, name="adamw_small",
        out_shape=[jax.ShapeDtypeStruct(w.shape, F32) for w in ws for _ in range(3)], compiler_params=_params(),
    )(*ws, *gs, *ms, *vs)
    return [tuple(res[3 * k:3 * k + 3]) for k in range(n)]


def _adamw(w, g, m, v, name):
    r, cdim = w.shape
    tr = next((t for t in (256, 176) if r % t == 0 and r > t), r)

    def body(*refs):
        _adamw_update(*refs)

    spec = pl.BlockSpec((tr, cdim), lambda i: (i, 0))
    return pl.pallas_call(
        body, name=name, grid=(r // tr,), in_specs=[spec] * 4, out_specs=[spec] * 3,
        out_shape=[jax.ShapeDtypeStruct((r, cdim), F32)] * 3, compiler_params=_params(1),
    )(w, g, m, v)


def _rope_tables(s):
    rows = s // GRID_W
    freqs = np.float32(ROPE_THETA) ** (-np.arange(0, ROPE_AXIS_DIM, 2, dtype=np.float32) / np.float32(ROPE_AXIS_DIM))
    ang_r = np.arange(rows, dtype=np.float32)[:, None] * freqs[None, :]
    ang_c = np.arange(GRID_W, dtype=np.float32)[:, None] * freqs[None, :]
    zr, zc = np.zeros_like(ang_r), np.zeros_like(ang_c)

    def table(by_row, by_col):
        r = jnp.asarray(np.tile(np.concatenate(by_row + [zr, zr], axis=1), (1, 2)), dtype=F32)
        c = jnp.asarray(np.tile(np.concatenate([zc, zc] + by_col, axis=1), (1, 2)), dtype=F32)
        return jnp.repeat(r, GRID_W, axis=0) + jnp.tile(c, (rows, 1))

    return (table([np.cos(ang_r)] * 2, [np.cos(ang_c)] * 2),
            table([-np.sin(ang_r), np.sin(ang_r)], [-np.sin(ang_c), np.sin(ang_c)]))


def _pack_rows(parts, rows):
    flat = jnp.concatenate([p.reshape(-1) for p in parts])
    return jnp.pad(flat, (0, rows * D_MODEL - flat.shape[0])).reshape(rows, D_MODEL)


def kernel(x, c, ctx, c_ctx, w_mod, b_mod, norm_w, w_in, q_norm_w, k_norm_w, conv_w, conv_b, conv_ln_w, conv_ln_b, w_pw, b_pw, w_out, loss_target, m_c_ctx, m_w_mod, m_b_mod, m_norm_w, m_w_in, m_q_norm_w, m_k_norm_w, m_conv_w, m_conv_b, m_conv_ln_w, m_conv_ln_b, m_w_pw, m_b_pw, m_w_out, v_c_ctx, v_w_mod, v_b_mod, v_norm_w, v_w_in, v_q_norm_w, v_k_norm_w, v_conv_w, v_conv_b, v_conv_ln_w, v_conv_ln_b, v_w_pw, v_b_pw, v_w_out):
    bsz, s, d = x.shape
    cl = ctx.shape[1]
    xi, yi, ci = lax.axis_index("x"), lax.axis_index("y"), lax.axis_index("c")
    chip = 2 * xi + yi
    dev = 2 * chip + ci
    ncol_mod = w_mod.shape[2]

    w_in_t_loc = w_in[0].T.astype(BF16)
    b_cols = lax.dynamic_slice(b_mod, (0, chip * ncol_mod), (1, ncol_mod))
    sc_rows, mod_g, g_in = _front(jnp.pad(c, ((0, 8 - bsz), (0, 0))), jnp.pad(c_ctx[None, :], ((0, 15), (0, 0))),
                                  w_mod[0], b_cols, w_in_t_loc)
    w_in_t = g_in.reshape(D_IN, d)
    mod_all = mod_g.transpose(1, 0, 2).reshape(80, 3 * d)
    mod_loc = lax.dynamic_slice(mod_all, (8 * dev, 0), (bsz, 3 * d))
    shift, scale1p, gate = mod_loc[:, None, :d], 1.0 + mod_loc[:, None, d:2 * d], mod_loc[:, None, 2 * d:]
    shift_c, scale1p_c = mod_all[64:65, :d][None], 1.0 + mod_all[64:65, d:2 * d][None]

    cos, sin = _rope_tables(s)
    qnw512 = jnp.tile(q_norm_w, (1, 8))
    knw128 = jnp.tile(k_norm_w, (1, 2))
    bd512 = jnp.kron(jnp.eye(8, dtype=F32), jnp.ones((HEAD_DIM, HEAD_DIM), F32)).astype(BF16)
    bd128 = bd512[:128, :128]

    u, p_qkv, p_za, p_ga, p_gg, p_zc = _norm_inproj(x, shift, scale1p, norm_w, w_in_t, SPLITS, 512, "norm_inproj")
    uc, pc_kv = _norm_inproj(ctx, shift_c, scale1p_c, norm_w, w_in_t[KV_LO:KV_HI], ((0, 256),), cl, "ctx_norm_inproj")
    q, k2x, v2x, ktx = _qkv_prep(p_qkv, cos, sin, qnw512, knw128, bd512, bd128, 256, cl)
    k2, v2, kt = _ctx_kv_prep(pc_kv, knw128, bd128, k2x, v2x, ktx)
    conv_w_loc = jnp.pad(conv_w[0], ((0, 1), (0, 0)))
    o, lse, g_out, g_pw, g_cw = _attn_forward(
        q, k2, v2, 256, _ChipGather([w_out[0].astype(BF16), w_pw[0].astype(BF16), conv_w_loc]))
    w_out_f = g_out.reshape(d, d)
    w_pw_f = g_pw.reshape(D_CONV, D_CONV)
    conv_w_f = g_cw.transpose(1, 0, 2).reshape(32, D_CONV)
    y, cv = _conv_forward(p_ga, p_gg, conv_w_f, conv_b, conv_ln_w, conv_ln_b, w_pw_f, b_pw, 256)
    loss_part, dh, do, dza, dcv, dzc, dgate, gw_out = _outproj_loss(
        x, loss_target, gate, o, p_za, cv, p_zc, w_out_f, 512)

    all_chips, half_rows = (0, 1, 2, 3), D_IN // 2
    dy, gw_pw, conv_stats = _conv_token_backward(dcv, y, conv_ln_w, conv_ln_b, w_pw_f, 256)
    da, dg, gcw = _conv_backward(dy, p_ga, p_gg, conv_w_f, 256)
    tw = min(1024, s)
    gw_hi = _weight_grad([(da, half_rows - SPLITS[2][0], 512), (dg, 0, 512), (dzc, 0, 512)], u, None, tw,
                         "grad_in_rows_hi")
    dq, dkt, dvt, r_out, r_pw, r_hi = _attn_backward(
        q, k2, v2, kt, o, do, lse, 256, _FusedReduce([(gw_out, all_chips), (gw_pw, all_chips), (gw_hi, (2, 3))]))
    dqkv, qk_stats = _qkv_backward(p_qkv, dq, dkt, dvt, cos, sin, qnw512, knw128, bd512, bd128, 256, cl)
    dpc, kc_stats = _ctx_kv_backward(pc_kv, dkt, dvt, knw128, bd128)
    gw_ctx = _weight_grad([(dpc, 0, 256)], uc, None, cl, "grad_in_rows_ctx")
    gw_lo = _weight_grad([(dqkv, 0, 768), (dza, 0, 512), (da, 0, half_rows - SPLITS[2][0])], u, gw_ctx, tw,
                         "grad_in_rows_lo")
    _, modc, gnw_c = _inproj_backward([dpc], ctx, None, scale1p_c, norm_w, w_in_t[KV_LO:KV_HI], cl,
                                      "ctx_inproj_backward")
    grad_x, modx, gnw_x, r_lo = _inproj_backward(
        [dqkv, dza, da, dg, dzc], x, dh, scale1p, norm_w, w_in_t, 512, "inproj_backward",
        _FusedReduce([(gw_lo, (0, 1))]))
    g_w_out, g_w_pw = r_out.reshape(d // 4, d), r_pw.reshape(D_CONV // 4, D_CONV)
    g_w_in_t = jnp.where(chip < 2, r_lo, r_hi).reshape(D_IN // 4, d)

    dmod_loc = jnp.concatenate([modx[:, 0, :], modx[:, 1, :], dgate[:, 0, :]], axis=1)
    gq = qk_stats[0].reshape(8, HEAD_DIM).sum(axis=0)
    gk = (qk_stats[1, :128] + kc_stats[0]).reshape(2, HEAD_DIM).sum(axis=0)
    packed = _pack_rows([dmod_loc, dmod_loc.sum(axis=0), gnw_x[0] + gnw_c[0], modc[0, 0], modc[0, 1], gq, gk,
                         conv_stats[0], conv_stats[1], conv_stats[2], conv_stats[3], gcw,
                         jnp.sum(loss_part[:, 0, 0])[None]], 32)
    total, g_w_mod, dsilu_ctx = _tail_exchange(packed, sc_rows, w_mod[0], bsz, 3 * bsz + 4)
    flat = total.reshape(-1)
    offs = [0]

    def take(nelem):
        lo = offs[0]
        offs[0] = lo + nelem
        return flat[lo:lo + nelem]

    take(bsz * 3 * d)
    g_b_mod_x = take(3 * d)
    g_norm_w = take(d)
    dshift_c, dscale_c = take(d), take(d)
    g_qnw, g_knw = take(HEAD_DIM), take(HEAD_DIM)
    g_b_pw, g_ln_w, g_ln_b, g_conv_b = take(D_CONV), take(D_CONV), take(D_CONV), take(D_CONV)
    g_conv_w_full = take(32 * D_CONV).reshape(32, D_CONV)
    loss = take(1)[0] * (0.5 / d)

    dmod_c = jnp.concatenate([dshift_c, dscale_c, jnp.zeros((d,), F32)])
    g_b_mod = (g_b_mod_x + dmod_c)[None, :]
    sg = _sigmoid(c_ctx)
    g_c_ctx = dsilu_ctx[0] * (sg * (1.0 + c_ctx * (1.0 - sg)))

    g_w_in = g_w_in_t.T
    g_conv_w = lax.dynamic_slice(g_conv_w_full, (0, chip * 128), (CONV_WIDTH, 128))

    grads = {
        "c_ctx": g_c_ctx, "w_mod": g_w_mod[None], "b_mod": g_b_mod, "norm_w": g_norm_w[None], "w_in": g_w_in[None],
        "q_norm_w": g_qnw[None], "k_norm_w": g_knw[None], "conv_w": g_conv_w[None], "conv_b": g_conv_b[None],
        "conv_ln_w": g_ln_w[None], "conv_ln_b": g_ln_b[None], "w_pw": g_w_pw[None], "b_pw": g_b_pw[None],
        "w_out": g_w_out[None],
    }
    weights = {
        "c_ctx": (c_ctx, m_c_ctx, v_c_ctx), "w_mod": (w_mod, m_w_mod, v_w_mod), "b_mod": (b_mod, m_b_mod, v_b_mod),
        "norm_w": (norm_w, m_norm_w, v_norm_w), "w_in": (w_in, m_w_in, v_w_in),
        "q_norm_w": (q_norm_w, m_q_norm_w, v_q_norm_w), "k_norm_w": (k_norm_w, m_k_norm_w, v_k_norm_w),
        "conv_w": (conv_w, m_conv_w, v_conv_w), "conv_b": (conv_b, m_conv_b, v_conv_b),
        "conv_ln_w": (conv_ln_w, m_conv_ln_w, v_conv_ln_w), "conv_ln_b": (conv_ln_b, m_conv_ln_b, v_conv_ln_b),
        "w_pw": (w_pw, m_w_pw, v_w_pw), "b_pw": (b_pw, m_b_pw, v_b_pw), "w_out": (w_out, m_w_out, v_w_out),
    }
    names = list(weights)
    big = ("w_mod", "w_in", "w_out")
    as_2d = lambda a: a.reshape((1, a.shape[0]) if a.ndim == 1 else (a.shape[-2] if a.ndim == 3 else 1, a.shape[-1]))
    small = [n for n in names if n not in big]
    w_g_m_v = zip(*[[as_2d(a) for a in (weights[n][0], grads[n], weights[n][1], weights[n][2])] for n in small])
    updates = dict(zip(small, _adamw_small(*[list(col) for col in w_g_m_v])))
    for n in ("w_mod", "w_out"):
        w, m, v = weights[n]
        updates[n] = _adamw(as_2d(w), as_2d(grads[n]), as_2d(m), as_2d(v), "adamw_" + n)
    w, m, v = weights["w_in"]
    updates["w_in"] = tuple(r.T for r in _adamw(w[0].T, g_w_in_t, m[0].T, v[0].T, "adamw_w_in"))
    deltas, new_ms, new_vs = ([updates[n][k].reshape(weights[n][0].shape) for n in names] for k in range(3))
    grads = {n: grads[n].reshape(weights[n][0].shape) for n in names}

    return (loss, grad_x, *[grads[n] for n in names], *deltas, *new_ms, *new_vs)
```

```python
import functools
import math

import jax
import jax.numpy as jnp
import numpy as np
from jax import lax
from jax.experimental import pallas as pl
from jax.experimental.pallas import tpu as pltpu

F32 = jnp.float32
BF16 = jnp.bfloat16
MESH = pl.DeviceIdType.MESH

D_MODEL = 1024
D_ATTN = 512
D_CONV = 512
HEAD_DIM = 64
N_KV = 2
GRID_W = 64
ROPE_AXIS_DIM = 32
ROPE_THETA = 10000.0
CONV_WIDTH = 31
CONV_PAD = 15
HALO = 16
CONV_ROWS = 32
EPS = 1e-6
SPLITS = ((0, 768), (768, 1280), (1280, 1792), (1792, 2304), (2304, 2816))
D_IN = 2816
KV_LO, KV_HI = 512, 768

ADAM_LR = 0.001
ADAM_B1 = 0.9
ADAM_B2 = 0.999
ADAM_EPS = 1e-08
ADAM_WD = 0.01
ADAM_STEP = 10

VMEM_LIMIT = 56 * 1024 * 1024

NT = (((1,), (1,)), ((), ()))
TN = (((0,), (0,)), ((), ()))


def _params(n_axes=0, **kw):
    if n_axes:
        kw["dimension_semantics"] = ("arbitrary",) * n_axes
    return pltpu.CompilerParams(vmem_limit_bytes=VMEM_LIMIT, **kw)


def _sigmoid(x):
    return 1.0 / (1.0 + jnp.exp(-x))


def _silu_and_grad(z):
    s = _sigmoid(z)
    return z * s, s * (1.0 + z * (1.0 - s))


def _seg_mean(v, ones_bd):
    hi = v.astype(BF16)
    lo = (v - hi.astype(F32)).astype(BF16)
    s = jnp.dot(hi, ones_bd, preferred_element_type=F32) + jnp.dot(lo, ones_bd, preferred_element_type=F32)
    return s * (1.0 / HEAD_DIM)


def _partner(v):
    n = v.shape[1]
    lane = lax.broadcasted_iota(jnp.int32, (v.shape[0], 128), 1)
    first = (lane % 32) < 16
    parts = []
    for k in range(n // 128):
        ch = v[:, 128 * k:128 * (k + 1)]
        parts.append(jnp.where(first, pltpu.roll(ch, 112, 1), pltpu.roll(ch, 16, 1)))
    return parts[0] if len(parts) == 1 else jnp.concatenate(parts, axis=1)


def _tile_lanes(t, reps):
    return t if reps == 1 else jnp.concatenate([t] * reps, axis=1)


def _lo_mask(rows):
    return lax.broadcasted_iota(jnp.int32, (rows, 128), 1) < HEAD_DIM


def _gather8_in_vmem(x_ref, out_ref, send_sems, recv_sems, local_sem):
    x, y, c = lax.axis_index("x"), lax.axis_index("y"), lax.axis_index("c")
    me, sibling = (x, y, c), (x, y, 1 - c)
    chips = [(1 - x, y), (x, 1 - y), (1 - x, 1 - y)]

    def slot(px, py, pc):
        return out_ref.at[4 * px + 2 * py + pc]

    def copy(k, block, to, src=None):
        return pltpu.make_async_remote_copy(
            src_ref=slot(*block) if src is None else src, dst_ref=slot(*block),
            send_sem=send_sems.at[k], recv_sem=recv_sems.at[k], device_id=to, device_id_type=MESH)

    mine = pltpu.make_async_copy(x_ref, slot(*me), local_sem)
    mine.start()
    first = [copy(0, me, sibling, src=x_ref)]
    first += [copy(1 + j, me, (*chip, c), src=x_ref) for j, chip in enumerate(chips)]
    for cp in first:
        cp.start()
    passed = [copy(4 + j, (*chip, c), sibling) for j, chip in enumerate(chips)]
    for j, chip in enumerate(chips):
        copy(1 + j, (*chip, c), me).wait_recv()
        passed[j].start()
    copy(0, sibling, me).wait_recv()
    for j, chip in enumerate(chips):
        copy(4 + j, (*chip, 1 - c), me).wait_recv()
    for cp in first + passed:
        cp.wait_send()
    mine.wait()


class _ChipGather:
    def __init__(self, arrs):
        self.arrs = list(arrs)
        n = self.n = len(self.arrs)
        self.in_specs = [pl.BlockSpec(memory_space=pl.ANY)] * n
        self.out_shape = [jax.ShapeDtypeStruct((4,) + a.shape, a.dtype) for a in self.arrs]
        self.out_specs = [pl.BlockSpec(memory_space=pl.ANY)] * n
        self.scratch = [pltpu.SemaphoreType.DMA((6 * n,)), pltpu.SemaphoreType.DMA((6 * n,)),
                        pltpu.SemaphoreType.DMA((n,))]
        self.phases = [self.start, self.forward, self.finish]

    def bind(self, ins, outs, scratch):
        self.ins, self.outs = ins, outs
        self.send_sems, self.recv_sems, self.local_sems = scratch
        self.x, self.y, self.c = lax.axis_index("x"), lax.axis_index("y"), lax.axis_index("c")
        self.chips = [(1 - self.x, self.y), (self.x, 1 - self.y), (1 - self.x, 1 - self.y)]
        self.mychip = 2 * self.x + self.y

    def _copy(self, a, k, chip_idx, cc, to, src=None):
        h = self.arrs[a].shape[0] // 2
        dst = self.outs[a].at[chip_idx, pl.ds(cc * h, h)]
        return pltpu.make_async_remote_copy(
            src_ref=dst if src is None else src, dst_ref=dst, send_sem=self.send_sems.at[6 * a + k],
            recv_sem=self.recv_sems.at[6 * a + k], device_id=to, device_id_type=MESH)

    def _local(self, a):
        return pltpu.make_async_copy(self.ins[a], self.outs[a].at[self.mychip], self.local_sems.at[a])

    def _first(self, a, j):
        h = self.arrs[a].shape[0] // 2
        return self._copy(a, j, self.mychip, self.c, (*self.chips[j], self.c), src=self.ins[a].at[pl.ds(self.c * h, h)])

    def _passed(self, a, j):
        cx, cy = self.chips[j]
        return self._copy(a, 3 + j, 2 * cx + cy, self.c, (self.x, self.y, 1 - self.c))

    def start(self):
        for a in range(self.n):
            self._local(a).start()
            for j in range(3):
                self._first(a, j).start()

    def forward(self):
        for a in range(self.n):
            for j, (cx, cy) in enumerate(self.chips):
                self._copy(a, j, 2 * cx + cy, self.c, (self.x, self.y, self.c)).wait_recv()
                self._passed(a, j).start()

    def finish(self):
        for a in range(self.n):
            for j, (cx, cy) in enumerate(self.chips):
                self._copy(a, 3 + j, 2 * cx + cy, 1 - self.c, (self.x, self.y, self.c)).wait_recv()
        for a in range(self.n):
            for j in range(3):
                self._first(a, j).wait_send()
                self._passed(a, j).wait_send()
            self._local(a).wait()


class _FusedReduce:
    def __init__(self, pieces):
        self.owners = [tuple(o) for _, o in pieces]
        self.arrs = [g.reshape(len(o), 2, g.shape[0] // (2 * len(o)), g.shape[1]) for g, o in pieces]
        n = self.n = len(pieces)
        hc = self.hc = [(v.shape[2], v.shape[3]) for v in self.arrs]
        nts = [len(o) for o in self.owners]
        self.base = [sum(nts[:p]) for p in range(n)]
        anyspec = pl.BlockSpec(memory_space=pl.ANY)
        self.in_specs = [anyspec] * n
        self.out_shape = [jax.ShapeDtypeStruct((2,) + s, F32) for s in hc]
        self.out_specs = [anyspec] * n
        self.scratch = [pltpu.VMEM((nt,) + s, F32) for nt, s in zip(nts, hc)]
        self.scratch += [pltpu.VMEM((nt,) + s, F32) for nt, s in zip(nts, hc)]
        self.scratch += [pltpu.VMEM(s, F32) for s in hc]
        self.scratch += [pltpu.VMEM((nt,) + s, BF16) for nt, s in zip(nts, hc)]
        self.scratch += [pltpu.VMEM((3,) + s, BF16) for s in hc]
        self.scratch += [pltpu.VMEM(s, F32) for s in hc]
        tot = sum(nts)
        self.scratch += [pltpu.SemaphoreType.DMA((tot,)), pltpu.SemaphoreType.DMA((tot,)),
                         pltpu.SemaphoreType.DMA((tot,)), pltpu.SemaphoreType.DMA((3 * n,)),
                         pltpu.SemaphoreType.DMA((n,)), pltpu.SemaphoreType.DMA((n,)), pltpu.SemaphoreType.DMA((n,)),
                         pltpu.SemaphoreType.DMA((tot,))]
        self.phases = [self.start, self.exchange, self.combine, self.finish]

    def bind(self, ins, outs, scratch):
        n = self.n
        self.g, self.out = ins, outs
        self.va, self.recv_a, self.own = scratch[:n], scratch[n:2 * n], scratch[2 * n:3 * n]
        self.tsend, self.recv_b, self.fin = scratch[3 * n:4 * n], scratch[4 * n:5 * n], scratch[5 * n:6 * n]
        self.sa, self.ra, self.sb, self.rb, self.sc, self.rc, self.lc, self.la = scratch[6 * n:]
        self.x, self.y, self.c = lax.axis_index("x"), lax.axis_index("y"), lax.axis_index("c")
        self.mychip = 2 * self.x + self.y
        self.sibling = (self.x, self.y, 1 - self.c)

    def _copy_a(self, p, t):
        k = self.base[p] + t
        return pltpu.make_async_remote_copy(
            src_ref=self.g[p].at[t, 1 - self.c], dst_ref=self.recv_a[p].at[t], send_sem=self.sa.at[k],
            recv_sem=self.ra.at[k], device_id=self.sibling, device_id_type=MESH)

    def _fetch(self, p, t):
        return pltpu.make_async_copy(self.g[p].at[t, self.c], self.va[p].at[t], self.la.at[self.base[p] + t])

    def _slot(self, owner):
        rel = jnp.bitwise_xor(self.mychip, owner)
        return jnp.where(rel == 2, 0, jnp.where(rel == 1, 1, 2))

    def _copy_b(self, p, t, slot):
        owner = self.owners[p][t]
        return pltpu.make_async_remote_copy(
            src_ref=self.tsend[p].at[t], dst_ref=self.recv_b[p].at[slot], send_sem=self.sb.at[self.base[p] + t],
            recv_sem=self.rb.at[3 * p + slot], device_id=(owner // 2, owner % 2, self.c), device_id_type=MESH)

    def _copy_c(self, p, half):
        return pltpu.make_async_remote_copy(
            src_ref=self.fin[p], dst_ref=self.out[p].at[half], send_sem=self.sc.at[p], recv_sem=self.rc.at[p],
            device_id=self.sibling, device_id_type=MESH)

    def _local_c(self, p):
        return pltpu.make_async_copy(self.fin[p], self.out[p].at[self.c], self.lc.at[p])

    def start(self):
        for p in range(self.n):
            for t in range(len(self.owners[p])):
                self._copy_a(p, t).start()
                self._fetch(p, t).start()

    def exchange(self):
        for p in range(self.n):
            for t, owner in enumerate(self.owners[p]):
                self._copy_a(p, t).wait_recv()
                self._fetch(p, t).wait()
                mine = self.mychip == owner

                @pl.when(mine)
                def _():
                    self.own[p][...] = self.va[p][t] + self.recv_a[p][t]

                @pl.when(jnp.logical_not(mine))
                def _():
                    self.tsend[p][t] = (self.va[p][t] + self.recv_a[p][t]).astype(BF16)
                    self._copy_b(p, t, self._slot(owner)).start()

    def combine(self):
        for p in range(self.n):
            for t, owner in enumerate(self.owners[p]):
                @pl.when(self.mychip == owner)
                def _():
                    acc = self.own[p][...]
                    for j in range(3):
                        self._copy_b(p, t, j).wait_recv()
                        acc = acc + self.recv_b[p][j].astype(F32)
                    self.fin[p][...] = acc
                    self._local_c(p).start()
                    self._copy_c(p, self.c).start()

    def finish(self):
        for p in range(self.n):
            for t, owner in enumerate(self.owners[p]):
                self._copy_a(p, t).wait_send()
                mine = self.mychip == owner

                @pl.when(mine)
                def _():
                    self._copy_c(p, 1 - self.c).wait_recv()
                    self._copy_c(p, self.c).wait_send()
                    self._local_c(p).wait()

                @pl.when(jnp.logical_not(mine))
                def _():
                    self._copy_b(p, t, self._slot(owner)).wait_send()


def _split_fused(refs, n_in, n_out, n_scr, fused):
    if fused is None:
        return refs[:n_in], refs[n_in:n_in + n_out], refs[n_in + n_out:]
    fi, fo = len(fused.in_specs), len(fused.out_specs)
    ins, rest = refs[:n_in], refs[n_in:]
    f_ins, rest = rest[:fi], rest[fi:]
    outs, rest = rest[:n_out], rest[n_out:]
    f_outs, rest = rest[:fo], rest[fo:]
    scr, f_scr = rest[:n_scr], rest[n_scr:]
    fused.bind(f_ins, f_outs, f_scr)
    return ins, outs, scr


def _run_phases(fused, step, at_steps, before):
    if fused is None:
        return
    for phase, (at, first) in zip(fused.phases, at_steps):
        if first == before:
            pl.when(step == at)(phase)


def _front(c_pad, c_ctx_rows, w_mod, b_cols, w_in_t_loc):
    ncol = w_mod.shape[1]
    gather = _ChipGather([w_in_t_loc])

    def body(c_ref, cctx_ref, w_ref, b_ref, win_ref, sc_ref, modg_ref, wing_ref,
             call_ref, ag_send, ag_recv, ag_local, m_send, m_recv, *g_scr):
        gather.bind([win_ref], [wing_ref], g_scr)
        gather.start()
        _gather8_in_vmem(c_ref, call_ref, ag_send, ag_recv, ag_local)
        x, y, c = lax.axis_index("x"), lax.axis_index("y"), lax.axis_index("c")
        chips = [(1 - x, y), (x, 1 - y), (1 - x, 1 - y)]
        mychip = 2 * x + y
        rows = jnp.concatenate([call_ref[dv] for dv in range(8)] + [cctx_ref[...]], axis=0)
        sc = rows * _sigmoid(rows)
        sc_ref[...] = sc
        modg_ref[mychip] = jnp.dot(sc, w_ref[...], preferred_element_type=F32,
                                   precision=lax.Precision.HIGHEST) + b_ref[...]

        def mcopy(j, chip_idx, to):
            return pltpu.make_async_remote_copy(
                src_ref=modg_ref.at[chip_idx], dst_ref=modg_ref.at[chip_idx], send_sem=m_send.at[j],
                recv_sem=m_recv.at[j], device_id=to, device_id_type=MESH)

        sends = [mcopy(j, mychip, (*chip, c)) for j, chip in enumerate(chips)]
        for cp in sends:
            cp.start()
        for j, (cx, cy) in enumerate(chips):
            mcopy(j, 2 * cx + cy, (x, y, c)).wait_recv()
        gather.forward()
        gather.finish()
        for cp in sends:
            cp.wait_send()

    vm = pl.BlockSpec(memory_space=pltpu.VMEM)
    return pl.pallas_call(
        body, name="front_exchange",
        out_shape=[jax.ShapeDtypeStruct((80, D_MODEL), F32), jax.ShapeDtypeStruct((4, 80, ncol), F32)] + gather.out_shape,
        in_specs=[vm, vm, vm, vm] + gather.in_specs, out_specs=[vm, vm] + gather.out_specs,
        scratch_shapes=[pltpu.VMEM((8, 8, D_MODEL), F32), pltpu.SemaphoreType.DMA((7,)), pltpu.SemaphoreType.DMA((7,)),
                        pltpu.SemaphoreType.DMA, pltpu.SemaphoreType.DMA((3,)), pltpu.SemaphoreType.DMA((3,))]
        + gather.scratch,
        compiler_params=_params(),
    )(c_pad, c_ctx_rows, w_mod, b_cols, w_in_t_loc)


def _tail_exchange(packed, sc_rows, w_mod, bsz, ctx_row):
    d = D_MODEL
    ncol = w_mod.shape[1]

    def body(p_ref, sc_ref, w_ref, total_ref, gw_ref, gcc_ref, gat_ref, dm_ref, part_ref,
             ag_send, ag_recv, ag_local, g_send, g_recv):
        _gather8_in_vmem(p_ref, gat_ref, ag_send, ag_recv, ag_local)
        acc = gat_ref[0]
        for dv in range(1, 8):
            acc = acc + gat_ref[dv]
        total_ref[...] = acc
        x, y, c = lax.axis_index("x"), lax.axis_index("y"), lax.axis_index("c")
        chips = [(1 - x, y), (x, 1 - y), (1 - x, 1 - y)]
        mychip = 2 * x + y
        dm_ref[...] = jnp.zeros(dm_ref.shape, F32)
        for k in range(4):
            @pl.when(mychip == k)
            def _():
                spans = [(seg, max(k * ncol, seg * d) - seg * d, min((k + 1) * ncol, (seg + 1) * d) - seg * d)
                         for seg in range(3) if k * ncol < (seg + 1) * d and (k + 1) * ncol > seg * d]
                for dv in range(8):
                    for b in range(bsz):
                        dm_ref[8 * dv + b:8 * dv + b + 1, :] = jnp.concatenate(
                            [gat_ref[dv, 3 * b + seg:3 * b + seg + 1, lo:hi] for seg, lo, hi in spans], axis=1)
                dm_ref[64:65, :] = jnp.concatenate(
                    [total_ref[ctx_row + seg:ctx_row + seg + 1, lo:hi] if seg < 2 else jnp.zeros((1, hi - lo), F32)
                     for seg, lo, hi in spans], axis=1)

        dm = dm_ref[...]
        gw_ref[...] = lax.dot_general(sc_ref[...], dm, TN, preferred_element_type=F32,
                                      precision=lax.Precision.HIGHEST)
        part_ref[mychip] = lax.dot_general(dm[64:72, :], w_ref[...], NT, preferred_element_type=F32,
                                           precision=lax.Precision.HIGHEST)

        def gcopy(j, chip_idx, to):
            return pltpu.make_async_remote_copy(
                src_ref=part_ref.at[chip_idx], dst_ref=part_ref.at[chip_idx], send_sem=g_send.at[j],
                recv_sem=g_recv.at[j], device_id=to, device_id_type=MESH)

        sends = [gcopy(j, mychip, (*chip, c)) for j, chip in enumerate(chips)]
        for cp in sends:
            cp.start()
        for j, (cx, cy) in enumerate(chips):
            gcopy(j, 2 * cx + cy, (x, y, c)).wait_recv()
        for cp in sends:
            cp.wait_send()
        gcc_ref[...] = (part_ref[0] + part_ref[1]) + (part_ref[2] + part_ref[3])

    return pl.pallas_call(
        body, name="tail_exchange",
        out_shape=[jax.ShapeDtypeStruct(packed.shape, F32), jax.ShapeDtypeStruct((d, ncol), F32),
                   jax.ShapeDtypeStruct((8, d), F32)],
        scratch_shapes=[pltpu.VMEM((8,) + packed.shape, F32), pltpu.VMEM((80, ncol), F32), pltpu.VMEM((4, 8, d), F32),
                        pltpu.SemaphoreType.DMA((7,)), pltpu.SemaphoreType.DMA((7,)), pltpu.SemaphoreType.DMA,
                        pltpu.SemaphoreType.DMA((3,)), pltpu.SemaphoreType.DMA((3,))],
        compiler_params=_params(),
    )(packed, sc_rows, w_mod)


def _bcast_spec(arr):
    if arr.shape[0] == 1:
        return pl.BlockSpec((1, 1, arr.shape[2]), lambda b, i: (0, 0, 0))
    return pl.BlockSpec((1, 1, arr.shape[2]), lambda b, i: (b, 0, 0))


def _norm_inproj(x, shift, scale1p, norm_w, w_t, splits, tm, name):
    bsz, s, d = x.shape

    def body(x_ref, sh_ref, sc_ref, nw_ref, w_ref, u_ref, *out_refs):
        xv = x_ref[0]
        rstd = lax.rsqrt(jnp.mean(xv * xv, axis=-1, keepdims=True) + EPS)
        u = (xv * rstd * nw_ref[...]) * sc_ref[0] + sh_ref[0]
        ub = u.astype(BF16)
        u_ref[0] = ub
        for (lo, hi), o_ref in zip(splits, out_refs):
            o_ref[0] = lax.dot_general(ub, w_ref[lo:hi, :], NT, preferred_element_type=F32)

    tok = lambda w: pl.BlockSpec((1, tm, w), lambda b, i: (b, i, 0))
    return pl.pallas_call(
        body, name=name, grid=(bsz, s // tm),
        in_specs=[tok(d), _bcast_spec(shift), _bcast_spec(scale1p), pl.BlockSpec((1, d), lambda b, i: (0, 0)),
                  pl.BlockSpec(w_t.shape, lambda b, i: (0, 0))],
        out_specs=[tok(d)] + [tok(hi - lo) for lo, hi in splits],
        out_shape=[jax.ShapeDtypeStruct((bsz, s, d), BF16)]
        + [jax.ShapeDtypeStruct((bsz, s, hi - lo), F32) for lo, hi in splits],
        compiler_params=_params(2),
    )(x, shift, scale1p, norm_w, w_t)


def _dup_heads(kv, lo_mask):
    r = pltpu.roll(kv, HEAD_DIM, 1)
    return jnp.where(lo_mask, kv, r), jnp.where(lo_mask, r, kv)


def _qkv_prep(qkv, cos, sin, qnw, knw, bd512, bd128, ts, row0):
    bsz, s, _ = qkv.shape

    def body(p_ref, cos_ref, sin_ref, qnw_ref, knw_ref, bd512_ref, bd128_ref, q_ref, k_ref, v_ref, kt_ref, vt_ref):
        lo_mask = _lo_mask(ts)
        cos_t, sin_t = cos_ref[...], sin_ref[...]
        qp = p_ref[0, :, 0:512]
        qn = qp * lax.rsqrt(_seg_mean(qp * qp, bd512_ref[...]) + EPS) * qnw_ref[...]
        qr = qn * _tile_lanes(cos_t, 4) + _partner(qn) * _tile_lanes(sin_t, 4)
        q_ref[0] = (qr * (1.0 / math.sqrt(HEAD_DIM))).astype(BF16)
        kp = p_ref[0, :, 512:640]
        kn = kp * lax.rsqrt(_seg_mean(kp * kp, bd128_ref[...]) + EPS) * knw_ref[...]
        kr = kn * cos_t + _partner(kn) * sin_t
        k0, k1 = _dup_heads(kr, lo_mask)
        k_ref[0, 0] = k0.astype(BF16)
        k_ref[0, 1] = k1.astype(BF16)
        vp = p_ref[0, :, 640:768]
        v0, v1 = _dup_heads(vp, lo_mask)
        v_ref[0, 0] = v0.astype(BF16)
        v_ref[0, 1] = v1.astype(BF16)
        kt_ref[0] = kr.T.astype(BF16)
        vt_ref[0] = vp.T.astype(BF16)

    const = lambda a: pl.BlockSpec(a.shape, lambda b, i: (0,) * a.ndim)
    kv_spec = pl.BlockSpec((1, 2, ts, 128), lambda b, i: (b, 0, i + row0 // ts, 0))
    t_spec = pl.BlockSpec((1, 128, ts), lambda b, i: (b, 0, i + row0 // ts))
    return pl.pallas_call(
        body, name="qkv_prep", grid=(bsz, s // ts),
        in_specs=[pl.BlockSpec((1, ts, 768), lambda b, i: (b, i, 0)),
                  pl.BlockSpec((ts, 128), lambda b, i: (i, 0)), pl.BlockSpec((ts, 128), lambda b, i: (i, 0)),
                  const(qnw), const(knw), const(bd512), const(bd128)],
        out_specs=[pl.BlockSpec((1, ts, 512), lambda b, i: (b, i, 0)), kv_spec, kv_spec, t_spec, t_spec],
        out_shape=[jax.ShapeDtypeStruct((bsz, s, 512), BF16), jax.ShapeDtypeStruct((bsz, 2, row0 + s, 128), BF16),
                   jax.ShapeDtypeStruct((bsz, 2, row0 + s, 128), BF16),
                   jax.ShapeDtypeStruct((bsz, 128, row0 + s), BF16), jax.ShapeDtypeStruct((bsz, 128, row0 + s), BF16)],
        compiler_params=_params(2),
    )(qkv, cos, sin, qnw, knw, bd512, bd128)


def _ctx_kv_prep(pc, knw, bd128, k2, v2, kt, vt):
    bsz, cl, _ = pc.shape

    def body(p_ref, knw_ref, bd128_ref, k_in, v_in, kt_in, vt_in, k_ref, v_ref, kt_ref, vt_ref):
        lo_mask = _lo_mask(cl)
        kp = p_ref[0, :, 0:128]
        kn = kp * lax.rsqrt(_seg_mean(kp * kp, bd128_ref[...]) + EPS) * knw_ref[...]
        k0, k1 = _dup_heads(kn, lo_mask)
        k_ref[0, 0] = k0.astype(BF16)
        k_ref[0, 1] = k1.astype(BF16)
        vp = p_ref[0, :, 128:256]
        v0, v1 = _dup_heads(vp, lo_mask)
        v_ref[0, 0] = v0.astype(BF16)
        v_ref[0, 1] = v1.astype(BF16)
        kt_ref[0] = kn.T.astype(BF16)
        vt_ref[0] = vp.T.astype(BF16)

    const = lambda a: pl.BlockSpec(a.shape, lambda b: (0,) * a.ndim)
    kv_spec = pl.BlockSpec((1, 2, cl, 128), lambda b: (b, 0, 0, 0))
    t_spec = pl.BlockSpec((1, 128, cl), lambda b: (b, 0, 0))
    anyspec = pl.BlockSpec(memory_space=pl.ANY)
    return pl.pallas_call(
        body, name="ctx_kv_prep", grid=(bsz,),
        in_specs=[pl.BlockSpec((1, cl, 256), lambda b: (b, 0, 0)), const(knw), const(bd128)] + [anyspec] * 4,
        out_specs=[kv_spec, kv_spec, t_spec, t_spec],
        out_shape=[jax.ShapeDtypeStruct(a.shape, BF16) for a in (k2, v2, kt, vt)],
        input_output_aliases={3: 0, 4: 1, 5: 2, 6: 3},
        compiler_params=_params(1),
    )(pc, knw, bd128, k2, v2, kt, vt)


def _attn_forward(q, k2, vt, tq, fused):
    bsz, s, _ = q.shape
    sk = k2.shape[2]
    nq = s // tq
    total = bsz * N_KV * nq
    at_steps = [(0, True), (total // 4, True), (total - 1, False)]

    def body(*refs):
        (q_ref, k_ref, vt_ref), (o_ref, lse_ref), _ = _split_fused(refs, 3, 2, 0, fused)
        g = pl.program_id(1)
        step = (pl.program_id(0) * N_KV + g) * nq + pl.program_id(2)
        _run_phases(fused, step, at_steps, True)
        kk = k_ref[0, 0]
        lo_mask = _lo_mask(tq)
        vt_aug = jnp.concatenate([vt_ref[0, pl.ds(pl.multiple_of(g * HEAD_DIM, HEAD_DIM), HEAD_DIM), :],
                                  jnp.ones((16, sk), BF16)], axis=0)
        ps, ms = [], []
        for j in range(2):
            qp = q_ref[0, :, 128 * j:128 * (j + 1)]
            for half in range(2):
                sel = lo_mask if half == 0 else jnp.logical_not(lo_mask)
                qs = jnp.where(sel, qp, jnp.zeros_like(qp))
                sc = lax.dot_general(qs, kk, NT, preferred_element_type=F32)
                m = jnp.max(sc, axis=-1, keepdims=True)
                ps.append(jnp.exp(sc - m).astype(BF16))
                ms.append(m)
        ots = [lax.dot_general(vt_aug, p, NT, preferred_element_type=F32) for p in ps]
        for j in range(2):
            o_t, l_t = [], []
            for half in range(2):
                ot = ots[2 * j + half]
                l = ot[HEAD_DIM:HEAD_DIM + 1, :]
                o_t.append(ot[0:HEAD_DIM, :] / l)
                l_t.append(jnp.broadcast_to(l, (HEAD_DIM, tq)))
            o_ref[0, :, 128 * j:128 * (j + 1)] = jnp.concatenate(o_t, axis=0).T
            lse_ref[0, :, 128 * j:128 * (j + 1)] = (jnp.where(lo_mask, ms[2 * j], ms[2 * j + 1])
                                                    + jnp.log(jnp.concatenate(l_t, axis=0).T))
        _run_phases(fused, step, at_steps, False)

    q_spec = pl.BlockSpec((1, tq, 256), lambda b, g, i: (b, i, g))
    kv_spec = pl.BlockSpec((1, 1, sk, 128), lambda b, g, i: (b, g, 0, 0))
    return pl.pallas_call(
        body, name="attn_forward", grid=(bsz, N_KV, nq),
        in_specs=[q_spec, kv_spec, pl.BlockSpec((1, 128, sk), lambda b, g, i: (b, 0, 0))] + fused.in_specs,
        out_specs=[q_spec, q_spec] + fused.out_specs,
        out_shape=[jax.ShapeDtypeStruct((bsz, s, 512), F32)] * 2 + fused.out_shape,
        scratch_shapes=fused.scratch,
        compiler_params=_params(3),
    )(q, k2, vt, *fused.arrs)


def _halo_specs(width, ts, s):
    r = ts // HALO
    last = s // HALO - 1
    return [pl.BlockSpec((1, ts, width), lambda b, i: (b, i, 0)),
            pl.BlockSpec((1, HALO, width), lambda b, i: (b, jnp.maximum(i * r - 1, 0), 0)),
            pl.BlockSpec((1, HALO, width), lambda b, i: (b, jnp.minimum((i + 1) * r, last), 0))]


def _fill_ext(ext_ref, cur, prev, nxt, i, n_tiles, ts):
    ext_ref[0:HALO, :] = jnp.where(i > 0, prev, jnp.zeros_like(prev))
    ext_ref[HALO:HALO + ts, :] = cur
    ext_ref[HALO + ts:2 * HALO + ts, :] = jnp.where(i < n_tiles - 1, nxt, jnp.zeros_like(nxt))


def _fill_shifted(sh_ref, ext_ref, ts):
    n = ts + 2 * HALO - 8
    for r in range(1, 8):
        sh_ref[r - 1, 0:n, :] = ext_ref[pl.ds(r, n), :]


def _window(sh_ref, ext_ref, off, rows, r0=0):
    q, r = divmod(off, 8)
    if r == 0:
        return ext_ref[pl.ds(r0 + off, rows), :]
    return sh_ref[r - 1, pl.ds(r0 + 8 * q, rows), :]


def _conv_forward(ga, gg, conv_w, conv_b, ln_w, ln_b, w_pw, b_pw, ts):
    bsz, s, dc = ga.shape
    n_tiles = s // ts

    def body(a_ref, ap_ref, an_ref, g_ref, gp_ref, gn_ref, cw_ref, cb_ref, lw_ref, lb_ref, wp_ref, bp_ref,
             y_ref, cv_ref, ext_ref, sh_ref):
        i = pl.program_id(1)
        glu = lambda a, g: a * _sigmoid(g)
        _fill_ext(ext_ref, glu(a_ref[0], g_ref[0]), glu(ap_ref[0], gp_ref[0]), glu(an_ref[0], gn_ref[0]), i, n_tiles, ts)
        _fill_shifted(sh_ref, ext_ref, ts)
        acc = jnp.broadcast_to(cb_ref[...], (ts, dc))
        for j in range(CONV_WIDTH):
            acc = acc + cw_ref[j:j + 1, :] * _window(sh_ref, ext_ref, HALO - CONV_PAD + j, ts)
        y_ref[0] = acc
        mu = jnp.mean(acc, axis=-1, keepdims=True)
        yc = acc - mu
        var = jnp.mean(yc * yc, axis=-1, keepdims=True)
        yn = yc * lax.rsqrt(var + EPS) * lw_ref[...] + lb_ref[...]
        ys = yn * _sigmoid(yn)
        cv_ref[0] = jnp.dot(ys.astype(BF16), wp_ref[...], preferred_element_type=F32) + bp_ref[...]

    const = lambda a: pl.BlockSpec(a.shape, lambda b, i: (0,) * a.ndim)
    return pl.pallas_call(
        body, name="conv_forward", grid=(bsz, n_tiles),
        in_specs=_halo_specs(dc, ts, s) + _halo_specs(dc, ts, s)
        + [const(conv_w), const(conv_b), const(ln_w), const(ln_b), const(w_pw), const(b_pw)],
        out_specs=[pl.BlockSpec((1, ts, dc), lambda b, i: (b, i, 0))] * 2,
        out_shape=[jax.ShapeDtypeStruct((bsz, s, dc), F32)] * 2,
        scratch_shapes=[pltpu.VMEM((ts + 2 * HALO, dc), F32), pltpu.VMEM((7, ts + 2 * HALO, dc), F32)],
        compiler_params=_params(2),
    )(ga, ga, ga, gg, gg, gg, conv_w, conv_b, ln_w, ln_b, w_pw, b_pw)


def _outproj_loss(x, target, gate, o, za, cv, zc, w_out, tm):
    bsz, s, d = x.shape

    def body(x_ref, t_ref, gate_ref, o_ref, za_ref, cv_ref, zc_ref, w_ref,
             loss_ref, dh_ref, do_ref, dza_ref, dcv_ref, dzc_ref, dgate_ref, gw_ref):
        b, i = pl.program_id(0), pl.program_id(1)
        ov, cvv = o_ref[0], cv_ref[0]
        silu_a, dsilu_a = _silu_and_grad(za_ref[0])
        silu_c, dsilu_c = _silu_and_grad(zc_ref[0])
        mix = jnp.concatenate([ov * silu_a, cvv * silu_c], axis=1).astype(BF16)
        out = jnp.dot(mix, w_ref[...], preferred_element_type=F32)
        gate_v = gate_ref[0]
        err = x_ref[0] + gate_v * out - t_ref[0]
        dh = err * (1.0 / d)
        dh_ref[0] = dh
        dout = (dh * gate_v).astype(BF16)
        dmix = lax.dot_general(dout, w_ref[...], NT, preferred_element_type=F32)
        gw = lax.dot_general(mix, dout, TN, preferred_element_type=F32)
        dg = jnp.sum(dh * out, axis=0, keepdims=True)
        sq = jnp.sum(err * err)

        @pl.when(jnp.logical_and(b == 0, i == 0))
        def _():
            gw_ref[...] = gw

        @pl.when(jnp.logical_or(b > 0, i > 0))
        def _():
            gw_ref[...] += gw

        @pl.when(i == 0)
        def _():
            dgate_ref[0] = dg
            loss_ref[...] = jnp.zeros(loss_ref.shape, F32) + sq

        @pl.when(i > 0)
        def _():
            dgate_ref[0] += dg
            loss_ref[...] += sq

        dma, dmc = dmix[:, :D_ATTN], dmix[:, D_ATTN:]
        do_ref[0] = dma * silu_a
        dza_ref[0] = (dma * ov * dsilu_a).astype(BF16)
        dcv_ref[0] = dmc * silu_c
        dzc_ref[0] = (dmc * cvv * dsilu_c).astype(BF16)

    tok = lambda w: pl.BlockSpec((1, tm, w), lambda b, i: (b, i, 0))
    return pl.pallas_call(
        body, name="outproj_loss", grid=(bsz, s // tm),
        in_specs=[tok(d), tok(d), _bcast_spec(gate), tok(512), tok(512), tok(512), tok(512),
                  pl.BlockSpec(w_out.shape, lambda b, i: (0, 0))],
        out_specs=[pl.BlockSpec((1, 8, 128), lambda b, i: (b, 0, 0)), tok(d), tok(512), tok(512), tok(512), tok(512),
                   pl.BlockSpec((1, 1, d), lambda b, i: (b, 0, 0)), pl.BlockSpec((d, d), lambda b, i: (0, 0))],
        out_shape=[jax.ShapeDtypeStruct((bsz, 8, 128), F32), jax.ShapeDtypeStruct((bsz, s, d), F32),
                   jax.ShapeDtypeStruct((bsz, s, 512), F32), jax.ShapeDtypeStruct((bsz, s, 512), BF16),
                   jax.ShapeDtypeStruct((bsz, s, 512), F32), jax.ShapeDtypeStruct((bsz, s, 512), BF16),
                   jax.ShapeDtypeStruct((bsz, 1, d), F32), jax.ShapeDtypeStruct((d, d), F32)],
        compiler_params=_params(2),
    )(x, target, gate, o, za, cv, zc, w_out)


def _conv_token_backward(dcv, y, ln_w, ln_b, w_pw, tm):
    bsz, s, dc = dcv.shape

    def body(dcv_ref, y_ref, lw_ref, lb_ref, wp_ref, dy_ref, gwp_ref, st_ref):
        b, i = pl.program_id(0), pl.program_id(1)
        yv, dcvv = y_ref[0], dcv_ref[0]
        mu = jnp.mean(yv, axis=-1, keepdims=True)
        yc = yv - mu
        rstd = lax.rsqrt(jnp.mean(yc * yc, axis=-1, keepdims=True) + EPS)
        yhat = yc * rstd
        yn = yhat * lw_ref[...] + lb_ref[...]
        ys, dsilu = _silu_and_grad(yn)
        dcvb = dcvv.astype(BF16)
        gwp = lax.dot_general(ys.astype(BF16), dcvb, TN, preferred_element_type=F32)
        dys = lax.dot_general(dcvb, wp_ref[...], NT, preferred_element_type=F32)
        dyn = dys * dsilu
        dyhat = dyn * lw_ref[...]
        dy = rstd * (dyhat - jnp.mean(dyhat, axis=-1, keepdims=True)
                     - yhat * jnp.mean(dyhat * yhat, axis=-1, keepdims=True))
        dy_ref[0] = dy
        red = lambda v: jnp.sum(v, axis=0, keepdims=True)
        stats = jnp.concatenate([red(dcvv), red(dyn * yhat), red(dyn), red(dy), jnp.zeros((4, dc), F32)], axis=0)
        first = jnp.logical_and(b == 0, i == 0)

        @pl.when(first)
        def _():
            gwp_ref[...] = gwp
            st_ref[...] = stats

        @pl.when(jnp.logical_not(first))
        def _():
            gwp_ref[...] += gwp
            st_ref[...] += stats

    tok = pl.BlockSpec((1, tm, dc), lambda b, i: (b, i, 0))
    const = lambda a: pl.BlockSpec(a.shape, lambda b, i: (0,) * a.ndim)
    return pl.pallas_call(
        body, name="conv_token_backward", grid=(bsz, s // tm),
        in_specs=[tok, tok, const(ln_w), const(ln_b), const(w_pw)],
        out_specs=[tok, pl.BlockSpec((dc, dc), lambda b, i: (0, 0)), pl.BlockSpec((8, dc), lambda b, i: (0, 0))],
        out_shape=[jax.ShapeDtypeStruct((bsz, s, dc), F32), jax.ShapeDtypeStruct((dc, dc), F32),
                   jax.ShapeDtypeStruct((8, dc), F32)],
        compiler_params=_params(2),
    )(dcv, y, ln_w, ln_b, w_pw)


def _conv_backward(dy, ga, gg, conv_w, ts):
    bsz, s, dc = dy.shape
    n_tiles = s // ts

    def body(dy_ref, dyp_ref, dyn_ref, a_ref, ap_ref, an_ref, g_ref, gp_ref, gn_ref, cw_ref,
             da_ref, dg_ref, gcw_ref, dyext_ref, ugext_ref, dysh_ref, ugsh_ref, dug_ref, gacc_ref):
        b, i = pl.program_id(0), pl.program_id(1)
        av, sg = a_ref[0], _sigmoid(g_ref[0])
        glu = lambda a, g: a * _sigmoid(g)
        _fill_ext(dyext_ref, dy_ref[0], dyp_ref[0], dyn_ref[0], i, n_tiles, ts)
        _fill_ext(ugext_ref, av * sg, glu(ap_ref[0], gp_ref[0]), glu(an_ref[0], gn_ref[0]), i, n_tiles, ts)
        _fill_shifted(dysh_ref, dyext_ref, ts)
        _fill_shifted(ugsh_ref, ugext_ref, ts)
        gacc_ref[...] = jnp.zeros(gacc_ref.shape, F32)

        def row_block(r, carry):
            r0 = pl.multiple_of(r * CONV_ROWS, CONV_ROWS)
            dyb = dy_ref[0, pl.ds(r0, CONV_ROWS), :]
            acc = jnp.zeros((CONV_ROWS, dc), F32)
            for j in range(CONV_WIDTH):
                acc = acc + cw_ref[j:j + 1, :] * _window(dysh_ref, dyext_ref, HALO + CONV_PAD - j, CONV_ROWS, r0)
                prod = dyb * _window(ugsh_ref, ugext_ref, HALO - CONV_PAD + j, CONV_ROWS, r0)
                part = prod[0:8, :]
                for k in range(8, CONV_ROWS, 8):
                    part = part + prod[k:k + 8, :]
                gacc_ref[j] += part
            dug_ref[pl.ds(r0, CONV_ROWS), :] = acc
            return carry

        lax.fori_loop(0, ts // CONV_ROWS, row_block, 0)
        dug = dug_ref[...]
        gcw = jnp.sum(gacc_ref[...], axis=1)
        first = jnp.logical_and(b == 0, i == 0)

        @pl.when(first)
        def _():
            gcw_ref[...] = gcw

        @pl.when(jnp.logical_not(first))
        def _():
            gcw_ref[...] += gcw

        da_ref[0] = (dug * sg).astype(BF16)
        dg_ref[0] = (dug * av * sg * (1.0 - sg)).astype(BF16)

    tok = pl.BlockSpec((1, ts, dc), lambda b, i: (b, i, 0))
    return pl.pallas_call(
        body, name="conv_backward", grid=(bsz, n_tiles),
        in_specs=_halo_specs(dc, ts, s) + _halo_specs(dc, ts, s) + _halo_specs(dc, ts, s)
        + [pl.BlockSpec(conv_w.shape, lambda b, i: (0, 0))],
        out_specs=[tok, tok, pl.BlockSpec((32, dc), lambda b, i: (0, 0))],
        out_shape=[jax.ShapeDtypeStruct((bsz, s, dc), BF16), jax.ShapeDtypeStruct((bsz, s, dc), BF16),
                   jax.ShapeDtypeStruct((32, dc), F32)],
        scratch_shapes=[pltpu.VMEM((ts + 2 * HALO, dc), F32), pltpu.VMEM((ts + 2 * HALO, dc), F32),
                        pltpu.VMEM((7, ts + 2 * HALO, dc), F32), pltpu.VMEM((7, ts + 2 * HALO, dc), F32),
                        pltpu.VMEM((ts, dc), F32), pltpu.VMEM((32, 8, dc), F32)],
        compiler_params=_params(2),
    )(dy, dy, dy, ga, ga, ga, gg, gg, gg, conv_w)


def _attn_backward(q, k2, v2, kt, o, do, lse, tq, fused):
    bsz, s, _ = q.shape
    sk = k2.shape[2]
    scale = 1.0 / math.sqrt(HEAD_DIM)
    nq = s // tq
    total = bsz * N_KV * nq
    at_steps = [(0, True), (total // 5, True), (total // 2, True), (total - 1, False)]

    def body(*refs):
        (q_ref, k_ref, v_ref, kt_ref, o_ref, do_ref, lse_ref), (dq_ref, dk_ref, dv_ref), _ = _split_fused(
            refs, 7, 3, 0, fused)
        g, i = pl.program_id(1), pl.program_id(2)
        step = (pl.program_id(0) * N_KV + g) * nq + i
        _run_phases(fused, step, at_steps, True)
        kk, vv = k_ref[0, 0], v_ref[0, 0]
        kgt = kt_ref[0, pl.ds(pl.multiple_of(g * HEAD_DIM, HEAD_DIM), HEAD_DIM), :]
        lo_mask = _lo_mask(tq)
        ps, dss, q_ts, do_ts = [], [], [], []
        for j in range(2):
            cols = slice(128 * j, 128 * (j + 1))
            qp, dop, lsep = q_ref[0, :, cols], do_ref[0, :, cols], lse_ref[0, :, cols]
            dprod = dop * o_ref[0, :, cols]
            q_t = qp.astype(F32).T.astype(BF16)
            do_t = dop.T.astype(BF16)
            for half in range(2):
                sel = lo_mask if half == 0 else jnp.logical_not(lo_mask)
                rows = slice(HEAD_DIM * half, HEAD_DIM * (half + 1))
                qs = jnp.where(sel, qp, jnp.zeros_like(qp))
                dos = jnp.where(sel, dop, 0.0).astype(BF16)
                lse_h = jnp.max(jnp.where(sel, lsep, -jnp.inf), axis=-1, keepdims=True)
                delta = jnp.sum(jnp.where(sel, dprod, 0.0), axis=-1, keepdims=True)
                sc = lax.dot_general(qs, kk, NT, preferred_element_type=F32)
                p = jnp.exp(sc - lse_h)
                dp = lax.dot_general(dos, vv, NT, preferred_element_type=F32)
                dss.append((p * (dp - delta)).astype(BF16))
                ps.append(p.astype(BF16))
                q_ts.append(q_t[rows, :])
                do_ts.append(do_t[rows, :])
        dv_acc = jnp.dot(do_ts[0], ps[0], preferred_element_type=F32)
        dk_acc = jnp.dot(q_ts[0], dss[0], preferred_element_type=F32)
        for h in range(1, 4):
            dv_acc = dv_acc + jnp.dot(do_ts[h], ps[h], preferred_element_type=F32)
            dk_acc = dk_acc + jnp.dot(q_ts[h], dss[h], preferred_element_type=F32)
        for j in range(2):
            dq_t = [lax.dot_general(kgt, dss[2 * j + half], NT, preferred_element_type=F32) for half in range(2)]
            dq_ref[0, :, 128 * j:128 * (j + 1)] = (jnp.concatenate(dq_t, axis=0) * scale).T

        @pl.when(i == 0)
        def _():
            dk_ref[0, 0] = dk_acc
            dv_ref[0, 0] = dv_acc

        @pl.when(i > 0)
        def _():
            dk_ref[0, 0] += dk_acc
            dv_ref[0, 0] += dv_acc

        _run_phases(fused, step, at_steps, False)

    q_spec = pl.BlockSpec((1, tq, 256), lambda b, g, i: (b, i, g))
    kv_spec = pl.BlockSpec((1, 1, sk, 128), lambda b, g, i: (b, g, 0, 0))
    acc_spec = pl.BlockSpec((1, 1, HEAD_DIM, sk), lambda b, g, i: (b, g, 0, 0))
    return pl.pallas_call(
        body, name="attn_backward", grid=(bsz, N_KV, nq),
        in_specs=[q_spec, kv_spec, kv_spec, pl.BlockSpec((1, 128, sk), lambda b, g, i: (b, 0, 0)), q_spec, q_spec,
                  q_spec] + fused.in_specs,
        out_specs=[q_spec, acc_spec, acc_spec] + fused.out_specs,
        out_shape=[jax.ShapeDtypeStruct((bsz, s, 512), F32), jax.ShapeDtypeStruct((bsz, 2, HEAD_DIM, sk), F32),
                   jax.ShapeDtypeStruct((bsz, 2, HEAD_DIM, sk), F32)] + fused.out_shape,
        scratch_shapes=fused.scratch,
        compiler_params=_params(3),
    )(q, k2, v2, kt, o, do, lse, *fused.arrs)


def _heads_to_lanes(acc_ref):
    return jnp.concatenate([acc_ref[0, 0], acc_ref[0, 1]], axis=0).T


def _norm_backward(dn, pre, w, bd):
    rstd = lax.rsqrt(_seg_mean(pre * pre, bd) + EPS)
    xhat = pre * rstd
    dxhat = dn * w
    return rstd * (dxhat - xhat * _seg_mean(dxhat * xhat, bd)), dn * xhat


def _qkv_backward(qkv, dq, dk2, dv2, cos, sin, qnw, knw, bd512, bd128, ts, row0):
    bsz, s, _ = qkv.shape

    def body(p_ref, dq_ref, dk_ref, dv_ref, cos_ref, sin_ref, qnw_ref, knw_ref, bd512_ref, bd128_ref, d_ref, gw_ref):
        b, i = pl.program_id(0), pl.program_id(1)
        lo_mask = _lo_mask(ts)
        cos_t, sin_t = cos_ref[...], sin_ref[...]
        dqr = dq_ref[0]
        dqn = dqr * _tile_lanes(cos_t, 4) + _partner(dqr * _tile_lanes(sin_t, 4))
        dqp, gq = _norm_backward(dqn, p_ref[0, :, 0:512], qnw_ref[...], bd512_ref[...])
        dkr = _heads_to_lanes(dk_ref)
        dkn = dkr * cos_t + _partner(dkr * sin_t)
        dkp, gk = _norm_backward(dkn, p_ref[0, :, 512:640], knw_ref[...], bd128_ref[...])
        dvp = _heads_to_lanes(dv_ref)
        d_ref[0] = jnp.concatenate([dqp, dkp, dvp], axis=1).astype(BF16)
        gk512 = jnp.concatenate([jnp.sum(gk, axis=0, keepdims=True), jnp.zeros((1, 384), F32)], axis=1)
        rows = jnp.concatenate([jnp.sum(gq, axis=0, keepdims=True), gk512, jnp.zeros((6, 512), F32)], axis=0)
        first = jnp.logical_and(b == 0, i == 0)

        @pl.when(first)
        def _():
            gw_ref[...] = rows

        @pl.when(jnp.logical_not(first))
        def _():
            gw_ref[...] += rows

    const = lambda a: pl.BlockSpec(a.shape, lambda b, i: (0,) * a.ndim)
    kv_spec = pl.BlockSpec((1, 2, HEAD_DIM, ts), lambda b, i: (b, 0, 0, i + row0 // ts))
    return pl.pallas_call(
        body, name="qkv_backward", grid=(bsz, s // ts),
        in_specs=[pl.BlockSpec((1, ts, 768), lambda b, i: (b, i, 0)), pl.BlockSpec((1, ts, 512), lambda b, i: (b, i, 0)),
                  kv_spec, kv_spec, pl.BlockSpec((ts, 128), lambda b, i: (i, 0)),
                  pl.BlockSpec((ts, 128), lambda b, i: (i, 0)), const(qnw), const(knw), const(bd512), const(bd128)],
        out_specs=[pl.BlockSpec((1, ts, 768), lambda b, i: (b, i, 0)), pl.BlockSpec((8, 512), lambda b, i: (0, 0))],
        out_shape=[jax.ShapeDtypeStruct((bsz, s, 768), BF16), jax.ShapeDtypeStruct((8, 512), F32)],
        compiler_params=_params(2),
    )(qkv, dq, dk2, dv2, cos, sin, qnw, knw, bd512, bd128)


def _ctx_kv_backward(pc, dk2, dv2, knw, bd128):
    bsz, cl, _ = pc.shape

    def body(p_ref, dk_ref, dv_ref, knw_ref, bd128_ref, d_ref, gw_ref):
        b = pl.program_id(0)
        lo_mask = _lo_mask(cl)
        dkn = _heads_to_lanes(dk_ref)
        dkp, gk = _norm_backward(dkn, p_ref[0, :, 0:128], knw_ref[...], bd128_ref[...])
        dvp = _heads_to_lanes(dv_ref)
        d_ref[0] = jnp.concatenate([dkp, dvp], axis=1).astype(BF16)
        rows = jnp.concatenate([jnp.sum(gk, axis=0, keepdims=True), jnp.zeros((7, 128), F32)], axis=0)

        @pl.when(b == 0)
        def _():
            gw_ref[...] = rows

        @pl.when(b > 0)
        def _():
            gw_ref[...] += rows

    const = lambda a: pl.BlockSpec(a.shape, lambda b: (0,) * a.ndim)
    kv_spec = pl.BlockSpec((1, 2, HEAD_DIM, cl), lambda b: (b, 0, 0, 0))
    return pl.pallas_call(
        body, name="ctx_kv_backward", grid=(bsz,),
        in_specs=[pl.BlockSpec((1, cl, 256), lambda b: (b, 0, 0)), kv_spec, kv_spec, const(knw), const(bd128)],
        out_specs=[pl.BlockSpec((1, cl, 256), lambda b: (b, 0, 0)), pl.BlockSpec((8, 128), lambda b: (0, 0))],
        out_shape=[jax.ShapeDtypeStruct((bsz, cl, 256), BF16), jax.ShapeDtypeStruct((8, 128), F32)],
        compiler_params=_params(1),
    )(pc, dk2, dv2, knw, bd128)


def _weight_grad(parts, u, init, tm, name):
    bsz, s, d = u.shape
    n_p = len(parts)
    nrows = sum(hi - lo for _, lo, hi in parts)

    def body(*refs):
        p_refs, u_ref = refs[:n_p], refs[n_p]
        gi_ref = refs[n_p + 1] if init is not None else None
        gw_ref = refs[-1]
        first = jnp.logical_and(pl.program_id(0) == 0, pl.program_id(1) == 0)
        dp = jnp.concatenate([r[0, :, lo:hi] for r, (_, lo, hi) in zip(p_refs, parts)], axis=1)
        gw = lax.dot_general(dp, u_ref[0], TN, preferred_element_type=F32)

        @pl.when(first)
        def _():
            gw_ref[...] = gw
            if init is not None:
                gw_ref[KV_LO:KV_HI, :] += gi_ref[...]

        @pl.when(jnp.logical_not(first))
        def _():
            gw_ref[...] += gw

    tok = lambda w: pl.BlockSpec((1, tm, w), lambda b, i: (b, i, 0))
    in_specs = [tok(a.shape[2]) for a, _, _ in parts] + [tok(d)]
    args = [a for a, _, _ in parts] + [u]
    if init is not None:
        in_specs.append(pl.BlockSpec(init.shape, lambda b, i: (0, 0)))
        args.append(init)
    return pl.pallas_call(
        body, name=name, grid=(bsz, s // tm), in_specs=in_specs,
        out_specs=pl.BlockSpec((nrows, d), lambda b, i: (0, 0)), out_shape=jax.ShapeDtypeStruct((nrows, d), F32),
        compiler_params=_params(2),
    )(*args)


def _inproj_backward(dps, x, dh, scale1p, norm_w, w_t, tm, name, fused=None):
    bsz, s, d = x.shape
    n_p = len(dps)
    shared = scale1p.shape[0] == 1
    with_dx = dh is not None
    n_in = n_p + (2 if with_dx else 1) + 3
    n_out = 3 if with_dx else 2
    total = bsz * (s // tm)
    at_steps = [(0, True), (total // 8, True), ((3 * total) // 4, True), (total - 1, False)]

    def body(*refs):
        ins, outs, _ = _split_fused(refs, n_in, n_out, 0, fused)
        dp_refs, x_ref = ins[:n_p], ins[n_p]
        dh_ref = ins[n_p + 1] if with_dx else None
        sc_ref, nw_ref, w_ref = ins[-3:]
        mod_ref, gnw_ref = outs[-2:]
        b, i = pl.program_id(0), pl.program_id(1)
        step = b * (s // tm) + i
        _run_phases(fused, step, at_steps, True)
        first = jnp.logical_and(b == 0, i == 0)
        dp = dp_refs[0][0] if n_p == 1 else jnp.concatenate([r[0] for r in dp_refs], axis=1)
        du = jnp.dot(dp, w_ref[...], preferred_element_type=F32)
        xv = x_ref[0]
        rstd = lax.rsqrt(jnp.mean(xv * xv, axis=-1, keepdims=True) + EPS)
        xhat = xv * rstd
        nw, sc = nw_ref[...], sc_ref[0]
        red = lambda v: jnp.sum(v, axis=0, keepdims=True)
        mod_rows = jnp.concatenate([red(du), red(du * (xhat * nw)), jnp.zeros((6, d), F32)], axis=0)
        gnw_rows = jnp.concatenate([red(du * sc * xhat), jnp.zeros((7, d), F32)], axis=0)
        mod_first = first if shared else i == 0

        @pl.when(mod_first)
        def _():
            mod_ref[0] = mod_rows

        @pl.when(jnp.logical_not(mod_first))
        def _():
            mod_ref[0] += mod_rows

        @pl.when(first)
        def _():
            gnw_ref[...] = gnw_rows

        @pl.when(jnp.logical_not(first))
        def _():
            gnw_ref[...] += gnw_rows

        if with_dx:
            dxhat = du * (nw * sc)
            outs[0][0] = dh_ref[0] + rstd * (dxhat - xhat * jnp.mean(dxhat * xhat, axis=-1, keepdims=True))
        _run_phases(fused, step, at_steps, False)

    tok = lambda w: pl.BlockSpec((1, tm, w), lambda b, i: (b, i, 0))
    in_specs = [tok(p.shape[2]) for p in dps] + [tok(d)]
    args = list(dps) + [x]
    if with_dx:
        in_specs.append(tok(d))
        args.append(dh)
    in_specs += [_bcast_spec(scale1p), pl.BlockSpec((1, d), lambda b, i: (0, 0)),
                 pl.BlockSpec(w_t.shape, lambda b, i: (0, 0))]
    args += [scale1p, norm_w, w_t]
    bm = scale1p.shape[0]
    mod_spec = pl.BlockSpec((1, 8, d), (lambda b, i: (0, 0, 0)) if shared else (lambda b, i: (b, 0, 0)))
    out_specs = [mod_spec, pl.BlockSpec((8, d), lambda b, i: (0, 0))]
    out_shape = [jax.ShapeDtypeStruct((bm, 8, d), F32), jax.ShapeDtypeStruct((8, d), F32)]
    if with_dx:
        out_specs.insert(0, tok(d))
        out_shape.insert(0, jax.ShapeDtypeStruct((bsz, s, d), F32))
    scratch = []
    if fused is not None:
        in_specs += fused.in_specs
        args += fused.arrs
        out_specs += fused.out_specs
        out_shape += fused.out_shape
        scratch = fused.scratch
    res = pl.pallas_call(
        body, name=name, grid=(bsz, s // tm), in_specs=in_specs, out_specs=out_specs, out_shape=out_shape,
        scratch_shapes=scratch, compiler_params=_params(2),
    )(*args)
    return list(res) if with_dx else [None] + list(res)


def _adamw_update(w_ref, g_ref, m_ref, v_ref, d_ref, nm_ref, nv_ref):
    gv = g_ref[...]
    mn = ADAM_B1 * m_ref[...] + (1.0 - ADAM_B1) * gv
    vn = ADAM_B2 * v_ref[...] + (1.0 - ADAM_B2) * (gv * gv)
    m_hat = mn / (1.0 - ADAM_B1 ** ADAM_STEP)
    v_hat = vn / (1.0 - ADAM_B2 ** ADAM_STEP)
    d_ref[...] = -ADAM_LR * (m_hat / (jnp.sqrt(v_hat) + ADAM_EPS) + ADAM_WD * w_ref[...])
    nm_ref[...] = mn
    nv_ref[...] = vn


def _adamw_small(ws, gs, ms, vs):
    n = len(ws)

    def body(*refs):
        ins, outs = refs[:4 * n], refs[4 * n:]
        for k in range(n):
            _adamw_update(ins[k], ins[n + k], ins[2 * n + k], ins[3 * n + k], outs[3 * k], outs[3 * k + 1],
                          outs[3 * k + 2])

    res = pl.pallas_call(
        body, name="adamw_small",
        out_shape=[jax.ShapeDtypeStruct(w.shape, F32) for w in ws for _ in range(3)], compiler_params=_params(),
    )(*ws, *gs, *ms, *vs)
    return [tuple(res[3 * k:3 * k + 3]) for k in range(n)]


def _adamw(w, g, m, v, name):
    r, cdim = w.shape
    tr = next((t for t in (256, 176) if r % t == 0 and r > t), r)

    def body(*refs):
        _adamw_update(*refs)

    spec = pl.BlockSpec((tr, cdim), lambda i: (i, 0))
    return pl.pallas_call(
        body, name=name, grid=(r // tr,), in_specs=[spec] * 4, out_specs=[spec] * 3,
        out_shape=[jax.ShapeDtypeStruct((r, cdim), F32)] * 3, compiler_params=_params(1),
    )(w, g, m, v)


def _rope_tables(s):
    rows = s // GRID_W
    freqs = np.float32(ROPE_THETA) ** (-np.arange(0, ROPE_AXIS_DIM, 2, dtype=np.float32) / np.float32(ROPE_AXIS_DIM))
    ang_r = np.arange(rows, dtype=np.float32)[:, None] * freqs[None, :]
    ang_c = np.arange(GRID_W, dtype=np.float32)[:, None] * freqs[None, :]
    zr, zc = np.zeros_like(ang_r), np.zeros_like(ang_c)

    def table(by_row, by_col):
        r = jnp.asarray(np.tile(np.concatenate(by_row + [zr, zr], axis=1), (1, 2)), dtype=F32)
        c = jnp.asarray(np.tile(np.concatenate([zc, zc] + by_col, axis=1), (1, 2)), dtype=F32)
        return jnp.repeat(r, GRID_W, axis=0) + jnp.tile(c, (rows, 1))

    return (table([np.cos(ang_r)] * 2, [np.cos(ang_c)] * 2),
            table([-np.sin(ang_r), np.sin(ang_r)], [-np.sin(ang_c), np.sin(ang_c)]))


def _pack_rows(parts, rows):
    flat = jnp.concatenate([p.reshape(-1) for p in parts])
    return jnp.pad(flat, (0, rows * D_MODEL - flat.shape[0])).reshape(rows, D_MODEL)


def kernel(x, c, ctx, c_ctx, w_mod, b_mod, norm_w, w_in, q_norm_w, k_norm_w, conv_w, conv_b, conv_ln_w, conv_ln_b, w_pw, b_pw, w_out, loss_target, m_c_ctx, m_w_mod, m_b_mod, m_norm_w, m_w_in, m_q_norm_w, m_k_norm_w, m_conv_w, m_conv_b, m_conv_ln_w, m_conv_ln_b, m_w_pw, m_b_pw, m_w_out, v_c_ctx, v_w_mod, v_b_mod, v_norm_w, v_w_in, v_q_norm_w, v_k_norm_w, v_conv_w, v_conv_b, v_conv_ln_w, v_conv_ln_b, v_w_pw, v_b_pw, v_w_out):
    bsz, s, d = x.shape
    cl = ctx.shape[1]
    xi, yi, ci = lax.axis_index("x"), lax.axis_index("y"), lax.axis_index("c")
    chip = 2 * xi + yi
    dev = 2 * chip + ci
    ncol_mod = w_mod.shape[2]

    w_in_t_loc = w_in[0].T.astype(BF16)
    b_cols = lax.dynamic_slice(b_mod, (0, chip * ncol_mod), (1, ncol_mod))
    sc_rows, mod_g, g_in = _front(jnp.pad(c, ((0, 8 - bsz), (0, 0))), jnp.pad(c_ctx[None, :], ((0, 15), (0, 0))),
                                  w_mod[0], b_cols, w_in_t_loc)
    w_in_t = g_in.reshape(D_IN, d)
    mod_all = mod_g.transpose(1, 0, 2).reshape(80, 3 * d)
    mod_loc = lax.dynamic_slice(mod_all, (8 * dev, 0), (bsz, 3 * d))
    shift, scale1p, gate = mod_loc[:, None, :d], 1.0 + mod_loc[:, None, d:2 * d], mod_loc[:, None, 2 * d:]
    shift_c, scale1p_c = mod_all[64:65, :d][None], 1.0 + mod_all[64:65, d:2 * d][None]

    cos, sin = _rope_tables(s)
    qnw512 = jnp.tile(q_norm_w, (1, 8))
    knw128 = jnp.tile(k_norm_w, (1, 2))
    bd512 = jnp.kron(jnp.eye(8, dtype=F32), jnp.ones((HEAD_DIM, HEAD_DIM), F32)).astype(BF16)
    bd128 = bd512[:128, :128]

    u, p_qkv, p_za, p_ga, p_gg, p_zc = _norm_inproj(x, shift, scale1p, norm_w, w_in_t, SPLITS, 512, "norm_inproj")
    uc, pc_kv = _norm_inproj(ctx, shift_c, scale1p_c, norm_w, w_in_t[KV_LO:KV_HI], ((0, 256),), cl, "ctx_norm_inproj")
    q, k2x, v2x, ktx, vtx = _qkv_prep(p_qkv, cos, sin, qnw512, knw128, bd512, bd128, 256, cl)
    k2, v2, kt, vt = _ctx_kv_prep(pc_kv, knw128, bd128, k2x, v2x, ktx, vtx)
    conv_w_loc = jnp.pad(conv_w[0], ((0, 1), (0, 0)))
    o, lse, g_out, g_pw, g_cw = _attn_forward(
        q, k2, vt, 256, _ChipGather([w_out[0].astype(BF16), w_pw[0].astype(BF16), conv_w_loc]))
    w_out_f = g_out.reshape(d, d)
    w_pw_f = g_pw.reshape(D_CONV, D_CONV)
    conv_w_f = g_cw.transpose(1, 0, 2).reshape(32, D_CONV)
    y, cv = _conv_forward(p_ga, p_gg, conv_w_f, conv_b, conv_ln_w, conv_ln_b, w_pw_f, b_pw, 256)
    loss_part, dh, do, dza, dcv, dzc, dgate, gw_out = _outproj_loss(
        x, loss_target, gate, o, p_za, cv, p_zc, w_out_f, 512)

    all_chips, half_rows = (0, 1, 2, 3), D_IN // 2
    dy, gw_pw, conv_stats = _conv_token_backward(dcv, y, conv_ln_w, conv_ln_b, w_pw_f, 256)
    da, dg, gcw = _conv_backward(dy, p_ga, p_gg, conv_w_f, 256)
    tw = min(1024, s)
    gw_hi = _weight_grad([(da, half_rows - SPLITS[2][0], 512), (dg, 0, 512), (dzc, 0, 512)], u, None, tw,
                         "grad_in_rows_hi")
    dq, dkt, dvt, r_out, r_pw, r_hi = _attn_backward(
        q, k2, v2, kt, o, do, lse, 256, _FusedReduce([(gw_out, all_chips), (gw_pw, all_chips), (gw_hi, (2, 3))]))
    dqkv, qk_stats = _qkv_backward(p_qkv, dq, dkt, dvt, cos, sin, qnw512, knw128, bd512, bd128, 256, cl)
    dpc, kc_stats = _ctx_kv_backward(pc_kv, dkt, dvt, knw128, bd128)
    gw_ctx = _weight_grad([(dpc, 0, 256)], uc, None, cl, "grad_in_rows_ctx")
    gw_lo = _weight_grad([(dqkv, 0, 768), (dza, 0, 512), (da, 0, half_rows - SPLITS[2][0])], u, gw_ctx, tw,
                         "grad_in_rows_lo")
    _, modc, gnw_c = _inproj_backward([dpc], ctx, None, scale1p_c, norm_w, w_in_t[KV_LO:KV_HI], cl,
                                      "ctx_inproj_backward")
    grad_x, modx, gnw_x, r_lo = _inproj_backward(
        [dqkv, dza, da, dg, dzc], x, dh, scale1p, norm_w, w_in_t, 512, "inproj_backward",
        _FusedReduce([(gw_lo, (0, 1))]))
    g_w_out, g_w_pw = r_out.reshape(d // 4, d), r_pw.reshape(D_CONV // 4, D_CONV)
    g_w_in_t = jnp.where(chip < 2, r_lo, r_hi).reshape(D_IN // 4, d)

    dmod_loc = jnp.concatenate([modx[:, 0, :], modx[:, 1, :], dgate[:, 0, :]], axis=1)
    gq = qk_stats[0].reshape(8, HEAD_DIM).sum(axis=0)
    gk = (qk_stats[1, :128] + kc_stats[0]).reshape(2, HEAD_DIM).sum(axis=0)
    packed = _pack_rows([dmod_loc, dmod_loc.sum(axis=0), gnw_x[0] + gnw_c[0], modc[0, 0], modc[0, 1], gq, gk,
                         conv_stats[0], conv_stats[1], conv_stats[2], conv_stats[3], gcw,
                         jnp.sum(loss_part[:, 0, 0])[None]], 32)
    total, g_w_mod, dsilu_ctx = _tail_exchange(packed, sc_rows, w_mod[0], bsz, 3 * bsz + 4)
    flat = total.reshape(-1)
    offs = [0]

    def take(nelem):
        lo = offs[0]
        offs[0] = lo + nelem
        return flat[lo:lo + nelem]

    take(bsz * 3 * d)
    g_b_mod_x = take(3 * d)
    g_norm_w = take(d)
    dshift_c, dscale_c = take(d), take(d)
    g_qnw, g_knw = take(HEAD_DIM), take(HEAD_DIM)
    g_b_pw, g_ln_w, g_ln_b, g_conv_b = take(D_CONV), take(D_CONV), take(D_CONV), take(D_CONV)
    g_conv_w_full = take(32 * D_CONV).reshape(32, D_CONV)
    loss = take(1)[0] * (0.5 / d)

    dmod_c = jnp.concatenate([dshift_c, dscale_c, jnp.zeros((d,), F32)])
    g_b_mod = (g_b_mod_x + dmod_c)[None, :]
    sg = _sigmoid(c_ctx)
    g_c_ctx = dsilu_ctx[0] * (sg * (1.0 + c_ctx * (1.0 - sg)))

    g_w_in = g_w_in_t.T
    g_conv_w = lax.dynamic_slice(g_conv_w_full, (0, chip * 128), (CONV_WIDTH, 128))

    grads = {
        "c_ctx": g_c_ctx, "w_mod": g_w_mod[None], "b_mod": g_b_mod, "norm_w": g_norm_w[None], "w_in": g_w_in[None],
        "q_norm_w": g_qnw[None], "k_norm_w": g_knw[None], "conv_w": g_conv_w[None], "conv_b": g_conv_b[None],
        "conv_ln_w": g_ln_w[None], "conv_ln_b": g_ln_b[None], "w_pw": g_w_pw[None], "b_pw": g_b_pw[None],
        "w_out": g_w_out[None],
    }
    weights = {
        "c_ctx": (c_ctx, m_c_ctx, v_c_ctx), "w_mod": (w_mod, m_w_mod, v_w_mod), "b_mod": (b_mod, m_b_mod, v_b_mod),
        "norm_w": (norm_w, m_norm_w, v_norm_w), "w_in": (w_in, m_w_in, v_w_in),
        "q_norm_w": (q_norm_w, m_q_norm_w, v_q_norm_w), "k_norm_w": (k_norm_w, m_k_norm_w, v_k_norm_w),
        "conv_w": (conv_w, m_conv_w, v_conv_w), "conv_b": (conv_b, m_conv_b, v_conv_b),
        "conv_ln_w": (conv_ln_w, m_conv_ln_w, v_conv_ln_w), "conv_ln_b": (conv_ln_b, m_conv_ln_b, v_conv_ln_b),
        "w_pw": (w_pw, m_w_pw, v_w_pw), "b_pw": (b_pw, m_b_pw, v_b_pw), "w_out": (w_out, m_w_out, v_w_out),
    }
    names = list(weights)
    big = ("w_mod", "w_in", "w_out")
    as_2d = lambda a: a.reshape((1, a.shape[0]) if a.ndim == 1 else (a.shape[-2] if a.ndim == 3 else 1, a.shape[-1]))
    small = [n for n in names if n not in big]
    w_g_m_v = zip(*[[as_2d(a) for a in (weights[n][0], grads[n], weights[n][1], weights[n][2])] for n in small])
    updates = dict(zip(small, _adamw_small(*[list(col) for col in w_g_m_v])))
    for n in ("w_mod", "w_out"):
        w, m, v = weights[n]
        updates[n] = _adamw(as_2d(w), as_2d(grads[n]), as_2d(m), as_2d(v), "adamw_" + n)
    w, m, v = weights["w_in"]
    updates["w_in"] = tuple(r.T for r in _adamw(w[0].T, g_w_in_t, m[0].T, v[0].T, "adamw_w_in"))
    deltas, new_ms, new_vs = ([updates[n][k].reshape(weights[n][0].shape) for n in names] for k in range(3))
    grads = {n: grads[n].reshape(weights[n][0].shape) for n in names}

    return (loss, grad_x, *[grads[n] for n in names], *deltas, *new_ms, *new_vs)
```

```python
import functools
import math

import jax
import jax.numpy as jnp
import numpy as np
from jax import lax
from jax.experimental import pallas as pl
from jax.experimental.pallas import tpu as pltpu

F32 = jnp.float32
BF16 = jnp.bfloat16
MESH = pl.DeviceIdType.MESH

D_MODEL = 1024
D_ATTN = 512
D_CONV = 512
HEAD_DIM = 64
N_KV = 2
GRID_W = 64
ROPE_AXIS_DIM = 32
ROPE_THETA = 10000.0
CONV_WIDTH = 31
CONV_PAD = 15
HALO = 16
CONV_ROWS = 32
EPS = 1e-6
SPLITS = ((0, 768), (768, 1280), (1280, 1792), (1792, 2304), (2304, 2816))
D_IN = 2816
KV_LO, KV_HI = 512, 768

ADAM_LR = 0.001
ADAM_B1 = 0.9
ADAM_B2 = 0.999
ADAM_EPS = 1e-08
ADAM_WD = 0.01
ADAM_STEP = 10

VMEM_LIMIT = 56 * 1024 * 1024

NT = (((1,), (1,)), ((), ()))
TN = (((0,), (0,)), ((), ()))


def _params(n_axes=0, **kw):
    if n_axes:
        kw["dimension_semantics"] = ("arbitrary",) * n_axes
    return pltpu.CompilerParams(vmem_limit_bytes=VMEM_LIMIT, **kw)


def _sigmoid(x):
    return 1.0 / (1.0 + jnp.exp(-x))


def _silu_and_grad(z):
    s = _sigmoid(z)
    return z * s, s * (1.0 + z * (1.0 - s))


def _seg_mean(v, ones_bd):
    hi = v.astype(BF16)
    lo = (v - hi.astype(F32)).astype(BF16)
    s = jnp.dot(hi, ones_bd, preferred_element_type=F32) + jnp.dot(lo, ones_bd, preferred_element_type=F32)
    return s * (1.0 / HEAD_DIM)


def _partner(v):
    n = v.shape[1]
    lane = lax.broadcasted_iota(jnp.int32, (v.shape[0], 128), 1)
    first = (lane % 32) < 16
    parts = []
    for k in range(n // 128):
        ch = v[:, 128 * k:128 * (k + 1)]
        parts.append(jnp.where(first, pltpu.roll(ch, 112, 1), pltpu.roll(ch, 16, 1)))
    return parts[0] if len(parts) == 1 else jnp.concatenate(parts, axis=1)


def _tile_lanes(t, reps):
    return t if reps == 1 else jnp.concatenate([t] * reps, axis=1)


def _lo_mask(rows):
    return lax.broadcasted_iota(jnp.int32, (rows, 128), 1) < HEAD_DIM


def _gather8_in_vmem(x_ref, out_ref, send_sems, recv_sems, local_sem):
    x, y, c = lax.axis_index("x"), lax.axis_index("y"), lax.axis_index("c")
    me, sibling = (x, y, c), (x, y, 1 - c)
    chips = [(1 - x, y), (x, 1 - y), (1 - x, 1 - y)]

    def slot(px, py, pc):
        return out_ref.at[4 * px + 2 * py + pc]

    def copy(k, block, to, src=None):
        return pltpu.make_async_remote_copy(
            src_ref=slot(*block) if src is None else src, dst_ref=slot(*block),
            send_sem=send_sems.at[k], recv_sem=recv_sems.at[k], device_id=to, device_id_type=MESH)

    mine = pltpu.make_async_copy(x_ref, slot(*me), local_sem)
    mine.start()
    first = [copy(0, me, sibling, src=x_ref)]
    first += [copy(1 + j, me, (*chip, c), src=x_ref) for j, chip in enumerate(chips)]
    for cp in first:
        cp.start()
    passed = [copy(4 + j, (*chip, c), sibling) for j, chip in enumerate(chips)]
    for j, chip in enumerate(chips):
        copy(1 + j, (*chip, c), me).wait_recv()
        passed[j].start()
    copy(0, sibling, me).wait_recv()
    for j, chip in enumerate(chips):
        copy(4 + j, (*chip, 1 - c), me).wait_recv()
    for cp in first + passed:
        cp.wait_send()
    mine.wait()


class _ChipGather:
    def __init__(self, arrs):
        self.arrs = list(arrs)
        n = self.n = len(self.arrs)
        self.in_specs = [pl.BlockSpec(memory_space=pl.ANY)] * n
        self.out_shape = [jax.ShapeDtypeStruct((4,) + a.shape, a.dtype) for a in self.arrs]
        self.out_specs = [pl.BlockSpec(memory_space=pl.ANY)] * n
        self.scratch = [pltpu.SemaphoreType.DMA((6 * n,)), pltpu.SemaphoreType.DMA((6 * n,)),
                        pltpu.SemaphoreType.DMA((n,))]
        self.phases = [self.start, self.forward, self.finish]

    def bind(self, ins, outs, scratch):
        self.ins, self.outs = ins, outs
        self.send_sems, self.recv_sems, self.local_sems = scratch
        self.x, self.y, self.c = lax.axis_index("x"), lax.axis_index("y"), lax.axis_index("c")
        self.chips = [(1 - self.x, self.y), (self.x, 1 - self.y), (1 - self.x, 1 - self.y)]
        self.mychip = 2 * self.x + self.y

    def _copy(self, a, k, chip_idx, cc, to, src=None):
        h = self.arrs[a].shape[0] // 2
        dst = self.outs[a].at[chip_idx, pl.ds(cc * h, h)]
        return pltpu.make_async_remote_copy(
            src_ref=dst if src is None else src, dst_ref=dst, send_sem=self.send_sems.at[6 * a + k],
            recv_sem=self.recv_sems.at[6 * a + k], device_id=to, device_id_type=MESH)

    def _local(self, a):
        return pltpu.make_async_copy(self.ins[a], self.outs[a].at[self.mychip], self.local_sems.at[a])

    def _first(self, a, j):
        h = self.arrs[a].shape[0] // 2
        return self._copy(a, j, self.mychip, self.c, (*self.chips[j], self.c), src=self.ins[a].at[pl.ds(self.c * h, h)])

    def _passed(self, a, j):
        cx, cy = self.chips[j]
        return self._copy(a, 3 + j, 2 * cx + cy, self.c, (self.x, self.y, 1 - self.c))

    def start(self):
        for a in range(self.n):
            self._local(a).start()
            for j in range(3):
                self._first(a, j).start()

    def forward(self):
        for a in range(self.n):
            for j, (cx, cy) in enumerate(self.chips):
                self._copy(a, j, 2 * cx + cy, self.c, (self.x, self.y, self.c)).wait_recv()
                self._passed(a, j).start()

    def finish(self):
        for a in range(self.n):
            for j, (cx, cy) in enumerate(self.chips):
                self._copy(a, 3 + j, 2 * cx + cy, 1 - self.c, (self.x, self.y, self.c)).wait_recv()
        for a in range(self.n):
            for j in range(3):
                self._first(a, j).wait_send()
                self._passed(a, j).wait_send()
            self._local(a).wait()


class _FusedReduce:
    def __init__(self, pieces):
        self.owners = [tuple(o) for _, o in pieces]
        self.arrs = [g.reshape(len(o), 2, g.shape[0] // (2 * len(o)), g.shape[1]) for g, o in pieces]
        n = self.n = len(pieces)
        hc = self.hc = [(v.shape[2], v.shape[3]) for v in self.arrs]
        nts = [len(o) for o in self.owners]
        self.base = [sum(nts[:p]) for p in range(n)]
        anyspec = pl.BlockSpec(memory_space=pl.ANY)
        self.in_specs = [anyspec] * n
        self.out_shape = [jax.ShapeDtypeStruct((2,) + s, F32) for s in hc]
        self.out_specs = [anyspec] * n
        self.scratch = [pltpu.VMEM((nt,) + s, F32) for nt, s in zip(nts, hc)]
        self.scratch += [pltpu.VMEM((nt,) + s, F32) for nt, s in zip(nts, hc)]
        self.scratch += [pltpu.VMEM(s, F32) for s in hc]
        self.scratch += [pltpu.VMEM((nt,) + s, BF16) for nt, s in zip(nts, hc)]
        self.scratch += [pltpu.VMEM((3,) + s, BF16) for s in hc]
        self.scratch += [pltpu.VMEM(s, F32) for s in hc]
        tot = sum(nts)
        self.scratch += [pltpu.SemaphoreType.DMA((tot,)), pltpu.SemaphoreType.DMA((tot,)),
                         pltpu.SemaphoreType.DMA((tot,)), pltpu.SemaphoreType.DMA((3 * n,)),
                         pltpu.SemaphoreType.DMA((n,)), pltpu.SemaphoreType.DMA((n,)), pltpu.SemaphoreType.DMA((n,)),
                         pltpu.SemaphoreType.DMA((tot,))]
        self.phases = [self.start, self.exchange, self.combine, self.finish]

    def bind(self, ins, outs, scratch):
        n = self.n
        self.g, self.out = ins, outs
        self.va, self.recv_a, self.own = scratch[:n], scratch[n:2 * n], scratch[2 * n:3 * n]
        self.tsend, self.recv_b, self.fin = scratch[3 * n:4 * n], scratch[4 * n:5 * n], scratch[5 * n:6 * n]
        self.sa, self.ra, self.sb, self.rb, self.sc, self.rc, self.lc, self.la = scratch[6 * n:]
        self.x, self.y, self.c = lax.axis_index("x"), lax.axis_index("y"), lax.axis_index("c")
        self.mychip = 2 * self.x + self.y
        self.sibling = (self.x, self.y, 1 - self.c)

    def _copy_a(self, p, t):
        k = self.base[p] + t
        return pltpu.make_async_remote_copy(
            src_ref=self.g[p].at[t, 1 - self.c], dst_ref=self.recv_a[p].at[t], send_sem=self.sa.at[k],
            recv_sem=self.ra.at[k], device_id=self.sibling, device_id_type=MESH)

    def _fetch(self, p, t):
        return pltpu.make_async_copy(self.g[p].at[t, self.c], self.va[p].at[t], self.la.at[self.base[p] + t])

    def _slot(self, owner):
        rel = jnp.bitwise_xor(self.mychip, owner)
        return jnp.where(rel == 2, 0, jnp.where(rel == 1, 1, 2))

    def _copy_b(self, p, t, slot):
        owner = self.owners[p][t]
        return pltpu.make_async_remote_copy(
            src_ref=self.tsend[p].at[t], dst_ref=self.recv_b[p].at[slot], send_sem=self.sb.at[self.base[p] + t],
            recv_sem=self.rb.at[3 * p + slot], device_id=(owner // 2, owner % 2, self.c), device_id_type=MESH)

    def _copy_c(self, p, half):
        return pltpu.make_async_remote_copy(
            src_ref=self.fin[p], dst_ref=self.out[p].at[half], send_sem=self.sc.at[p], recv_sem=self.rc.at[p],
            device_id=self.sibling, device_id_type=MESH)

    def _local_c(self, p):
        return pltpu.make_async_copy(self.fin[p], self.out[p].at[self.c], self.lc.at[p])

    def start(self):
        for p in range(self.n):
            for t in range(len(self.owners[p])):
                self._copy_a(p, t).start()
                self._fetch(p, t).start()

    def exchange(self):
        for p in range(self.n):
            for t, owner in enumerate(self.owners[p]):
                self._copy_a(p, t).wait_recv()
                self._fetch(p, t).wait()
                mine = self.mychip == owner

                @pl.when(mine)
                def _():
                    self.own[p][...] = self.va[p][t] + self.recv_a[p][t]

                @pl.when(jnp.logical_not(mine))
                def _():
                    self.tsend[p][t] = (self.va[p][t] + self.recv_a[p][t]).astype(BF16)
                    self._copy_b(p, t, self._slot(owner)).start()

    def combine(self):
        for p in range(self.n):
            for t, owner in enumerate(self.owners[p]):
                @pl.when(self.mychip == owner)
                def _():
                    acc = self.own[p][...]
                    for j in range(3):
                        self._copy_b(p, t, j).wait_recv()
                        acc = acc + self.recv_b[p][j].astype(F32)
                    self.fin[p][...] = acc
                    self._local_c(p).start()
                    self._copy_c(p, self.c).start()

    def finish(self):
        for p in range(self.n):
            for t, owner in enumerate(self.owners[p]):
                self._copy_a(p, t).wait_send()
                mine = self.mychip == owner

                @pl.when(mine)
                def _():
                    self._copy_c(p, 1 - self.c).wait_recv()
                    self._copy_c(p, self.c).wait_send()
                    self._local_c(p).wait()

                @pl.when(jnp.logical_not(mine))
                def _():
                    self._copy_b(p, t, self._slot(owner)).wait_send()


def _split_fused(refs, n_in, n_out, n_scr, fused):
    if fused is None:
        return refs[:n_in], refs[n_in:n_in + n_out], refs[n_in + n_out:]
    fi, fo = len(fused.in_specs), len(fused.out_specs)
    ins, rest = refs[:n_in], refs[n_in:]
    f_ins, rest = rest[:fi], rest[fi:]
    outs, rest = rest[:n_out], rest[n_out:]
    f_outs, rest = rest[:fo], rest[fo:]
    scr, f_scr = rest[:n_scr], rest[n_scr:]
    fused.bind(f_ins, f_outs, f_scr)
    return ins, outs, scr


def _run_phases(fused, step, at_steps, before):
    if fused is None:
        return
    for phase, (at, first) in zip(fused.phases, at_steps):
        if first == before:
            pl.when(step == at)(phase)


def _front(c_pad, c_ctx_rows, w_mod, b_cols, w_in_t_loc):
    ncol = w_mod.shape[1]
    gather = _ChipGather([w_in_t_loc])

    def body(c_ref, cctx_ref, w_ref, b_ref, win_ref, sc_ref, modg_ref, wing_ref,
             call_ref, ag_send, ag_recv, ag_local, m_send, m_recv, *g_scr):
        gather.bind([win_ref], [wing_ref], g_scr)
        _gather8_in_vmem(c_ref, call_ref, ag_send, ag_recv, ag_local)
        gather.start()
        x, y, c = lax.axis_index("x"), lax.axis_index("y"), lax.axis_index("c")
        chips = [(1 - x, y), (x, 1 - y), (1 - x, 1 - y)]
        mychip = 2 * x + y
        rows = jnp.concatenate([call_ref[dv] for dv in range(8)] + [cctx_ref[...]], axis=0)
        sc = rows * _sigmoid(rows)
        sc_ref[...] = sc
        modg_ref[mychip] = jnp.dot(sc, w_ref[...], preferred_element_type=F32,
                                   precision=lax.Precision.HIGHEST) + b_ref[...]

        def mcopy(j, chip_idx, to):
            return pltpu.make_async_remote_copy(
                src_ref=modg_ref.at[chip_idx], dst_ref=modg_ref.at[chip_idx], send_sem=m_send.at[j],
                recv_sem=m_recv.at[j], device_id=to, device_id_type=MESH)

        sends = [mcopy(j, mychip, (*chip, c)) for j, chip in enumerate(chips)]
        for cp in sends:
            cp.start()
        for j, (cx, cy) in enumerate(chips):
            mcopy(j, 2 * cx + cy, (x, y, c)).wait_recv()
        gather.forward()
        gather.finish()
        for cp in sends:
            cp.wait_send()

    vm = pl.BlockSpec(memory_space=pltpu.VMEM)
    return pl.pallas_call(
        body, name="front_exchange",
        out_shape=[jax.ShapeDtypeStruct((80, D_MODEL), F32), jax.ShapeDtypeStruct((4, 80, ncol), F32)] + gather.out_shape,
        in_specs=[vm, vm, vm, vm] + gather.in_specs, out_specs=[vm, vm] + gather.out_specs,
        scratch_shapes=[pltpu.VMEM((8, 8, D_MODEL), F32), pltpu.SemaphoreType.DMA((7,)), pltpu.SemaphoreType.DMA((7,)),
                        pltpu.SemaphoreType.DMA, pltpu.SemaphoreType.DMA((3,)), pltpu.SemaphoreType.DMA((3,))]
        + gather.scratch,
        compiler_params=_params(),
    )(c_pad, c_ctx_rows, w_mod, b_cols, w_in_t_loc)


def _tail_exchange(packed, sc_rows, w_mod, bsz, ctx_row):
    d = D_MODEL
    ncol = w_mod.shape[1]

    def body(p_ref, sc_ref, w_ref, total_ref, gw_ref, gcc_ref, gat_ref, dm_ref, part_ref,
             ag_send, ag_recv, ag_local, g_send, g_recv):
        _gather8_in_vmem(p_ref, gat_ref, ag_send, ag_recv, ag_local)
        acc = gat_ref[0]
        for dv in range(1, 8):
            acc = acc + gat_ref[dv]
        total_ref[...] = acc
        x, y, c = lax.axis_index("x"), lax.axis_index("y"), lax.axis_index("c")
        chips = [(1 - x, y), (x, 1 - y), (1 - x, 1 - y)]
        mychip = 2 * x + y
        dm_ref[...] = jnp.zeros(dm_ref.shape, F32)
        for k in range(4):
            @pl.when(mychip == k)
            def _():
                spans = [(seg, max(k * ncol, seg * d) - seg * d, min((k + 1) * ncol, (seg + 1) * d) - seg * d)
                         for seg in range(3) if k * ncol < (seg + 1) * d and (k + 1) * ncol > seg * d]
                for dv in range(8):
                    for b in range(bsz):
                        dm_ref[8 * dv + b:8 * dv + b + 1, :] = jnp.concatenate(
                            [gat_ref[dv, 3 * b + seg:3 * b + seg + 1, lo:hi] for seg, lo, hi in spans], axis=1)
                dm_ref[64:65, :] = jnp.concatenate(
                    [total_ref[ctx_row + seg:ctx_row + seg + 1, lo:hi] if seg < 2 else jnp.zeros((1, hi - lo), F32)
                     for seg, lo, hi in spans], axis=1)

        dm = dm_ref[...]
        gw_ref[...] = lax.dot_general(sc_ref[...], dm, TN, preferred_element_type=F32,
                                      precision=lax.Precision.HIGHEST)
        part_ref[mychip] = lax.dot_general(dm[64:72, :], w_ref[...], NT, preferred_element_type=F32,
                                           precision=lax.Precision.HIGHEST)

        def gcopy(j, chip_idx, to):
            return pltpu.make_async_remote_copy(
                src_ref=part_ref.at[chip_idx], dst_ref=part_ref.at[chip_idx], send_sem=g_send.at[j],
                recv_sem=g_recv.at[j], device_id=to, device_id_type=MESH)

        sends = [gcopy(j, mychip, (*chip, c)) for j, chip in enumerate(chips)]
        for cp in sends:
            cp.start()
        for j, (cx, cy) in enumerate(chips):
            gcopy(j, 2 * cx + cy, (x, y, c)).wait_recv()
        for cp in sends:
            cp.wait_send()
        gcc_ref[...] = (part_ref[0] + part_ref[1]) + (part_ref[2] + part_ref[3])

    return pl.pallas_call(
        body, name="tail_exchange",
        out_shape=[jax.ShapeDtypeStruct(packed.shape, F32), jax.ShapeDtypeStruct((d, ncol), F32),
                   jax.ShapeDtypeStruct((8, d), F32)],
        scratch_shapes=[pltpu.VMEM((8,) + packed.shape, F32), pltpu.VMEM((80, ncol), F32), pltpu.VMEM((4, 8, d), F32),
                        pltpu.SemaphoreType.DMA((7,)), pltpu.SemaphoreType.DMA((7,)), pltpu.SemaphoreType.DMA,
                        pltpu.SemaphoreType.DMA((3,)), pltpu.SemaphoreType.DMA((3,))],
        compiler_params=_params(),
    )(packed, sc_rows, w_mod)


def _bcast_spec(arr):
    if arr.shape[0] == 1:
        return pl.BlockSpec((1, 1, arr.shape[2]), lambda b, i: (0, 0, 0))
    return pl.BlockSpec((1, 1, arr.shape[2]), lambda b, i: (b, 0, 0))


def _norm_inproj(x, shift, scale1p, norm_w, w_t, splits, tm, name):
    bsz, s, d = x.shape

    def body(x_ref, sh_ref, sc_ref, nw_ref, w_ref, u_ref, *out_refs):
        xv = x_ref[0]
        rstd = lax.rsqrt(jnp.mean(xv * xv, axis=-1, keepdims=True) + EPS)
        u = (xv * rstd * nw_ref[...]) * sc_ref[0] + sh_ref[0]
        ub = u.astype(BF16)
        u_ref[0] = ub
        for (lo, hi), o_ref in zip(splits, out_refs):
            o_ref[0] = lax.dot_general(ub, w_ref[lo:hi, :], NT, preferred_element_type=F32)

    tok = lambda w: pl.BlockSpec((1, tm, w), lambda b, i: (b, i, 0))
    return pl.pallas_call(
        body, name=name, grid=(bsz, s // tm),
        in_specs=[tok(d), _bcast_spec(shift), _bcast_spec(scale1p), pl.BlockSpec((1, d), lambda b, i: (0, 0)),
                  pl.BlockSpec(w_t.shape, lambda b, i: (0, 0))],
        out_specs=[tok(d)] + [tok(hi - lo) for lo, hi in splits],
        out_shape=[jax.ShapeDtypeStruct((bsz, s, d), BF16)]
        + [jax.ShapeDtypeStruct((bsz, s, hi - lo), F32) for lo, hi in splits],
        compiler_params=_params(2),
    )(x, shift, scale1p, norm_w, w_t)


def _dup_heads(kv, lo_mask):
    r = pltpu.roll(kv, HEAD_DIM, 1)
    return jnp.where(lo_mask, kv, r), jnp.where(lo_mask, r, kv)


def _qkv_prep(qkv, cos, sin, qnw, knw, bd512, bd128, ts, row0):
    bsz, s, _ = qkv.shape

    def body(p_ref, cos_ref, sin_ref, qnw_ref, knw_ref, bd512_ref, bd128_ref, q_ref, k_ref, v_ref, kt_ref, vt_ref):
        lo_mask = _lo_mask(ts)
        cos_t, sin_t = cos_ref[...], sin_ref[...]
        qp = p_ref[0, :, 0:512]
        qn = qp * lax.rsqrt(_seg_mean(qp * qp, bd512_ref[...]) + EPS) * qnw_ref[...]
        qr = qn * _tile_lanes(cos_t, 4) + _partner(qn) * _tile_lanes(sin_t, 4)
        q_ref[0] = (qr * (1.0 / math.sqrt(HEAD_DIM))).astype(BF16)
        kp = p_ref[0, :, 512:640]
        kn = kp * lax.rsqrt(_seg_mean(kp * kp, bd128_ref[...]) + EPS) * knw_ref[...]
        kr = kn * cos_t + _partner(kn) * sin_t
        k0, k1 = _dup_heads(kr, lo_mask)
        k_ref[0, 0] = k0.astype(BF16)
        k_ref[0, 1] = k1.astype(BF16)
        vp = p_ref[0, :, 640:768]
        v0, v1 = _dup_heads(vp, lo_mask)
        v_ref[0, 0] = v0.astype(BF16)
        v_ref[0, 1] = v1.astype(BF16)
        kt_ref[0] = kr.T.astype(BF16)
        vt_ref[0] = vp.T.astype(BF16)

    const = lambda a: pl.BlockSpec(a.shape, lambda b, i: (0,) * a.ndim)
    kv_spec = pl.BlockSpec((1, 2, ts, 128), lambda b, i: (b, 0, i + row0 // ts, 0))
    t_spec = pl.BlockSpec((1, 128, ts), lambda b, i: (b, 0, i + row0 // ts))
    return pl.pallas_call(
        body, name="qkv_prep", grid=(bsz, s // ts),
        in_specs=[pl.BlockSpec((1, ts, 768), lambda b, i: (b, i, 0)),
                  pl.BlockSpec((ts, 128), lambda b, i: (i, 0)), pl.BlockSpec((ts, 128), lambda b, i: (i, 0)),
                  const(qnw), const(knw), const(bd512), const(bd128)],
        out_specs=[pl.BlockSpec((1, ts, 512), lambda b, i: (b, i, 0)), kv_spec, kv_spec, t_spec, t_spec],
        out_shape=[jax.ShapeDtypeStruct((bsz, s, 512), BF16), jax.ShapeDtypeStruct((bsz, 2, row0 + s, 128), BF16),
                   jax.ShapeDtypeStruct((bsz, 2, row0 + s, 128), BF16),
                   jax.ShapeDtypeStruct((bsz, 128, row0 + s), BF16), jax.ShapeDtypeStruct((bsz, 128, row0 + s), BF16)],
        compiler_params=_params(2),
    )(qkv, cos, sin, qnw, knw, bd512, bd128)


def _ctx_kv_prep(pc, knw, bd128, k2, v2, kt, vt):
    bsz, cl, _ = pc.shape

    def body(p_ref, knw_ref, bd128_ref, k_in, v_in, kt_in, vt_in, k_ref, v_ref, kt_ref, vt_ref):
        lo_mask = _lo_mask(cl)
        kp = p_ref[0, :, 0:128]
        kn = kp * lax.rsqrt(_seg_mean(kp * kp, bd128_ref[...]) + EPS) * knw_ref[...]
        k0, k1 = _dup_heads(kn, lo_mask)
        k_ref[0, 0] = k0.astype(BF16)
        k_ref[0, 1] = k1.astype(BF16)
        vp = p_ref[0, :, 128:256]
        v0, v1 = _dup_heads(vp, lo_mask)
        v_ref[0, 0] = v0.astype(BF16)
        v_ref[0, 1] = v1.astype(BF16)
        kt_ref[0] = kn.T.astype(BF16)
        vt_ref[0] = vp.T.astype(BF16)

    const = lambda a: pl.BlockSpec(a.shape, lambda b: (0,) * a.ndim)
    kv_spec = pl.BlockSpec((1, 2, cl, 128), lambda b: (b, 0, 0, 0))
    t_spec = pl.BlockSpec((1, 128, cl), lambda b: (b, 0, 0))
    anyspec = pl.BlockSpec(memory_space=pl.ANY)
    return pl.pallas_call(
        body, name="ctx_kv_prep", grid=(bsz,),
        in_specs=[pl.BlockSpec((1, cl, 256), lambda b: (b, 0, 0)), const(knw), const(bd128)] + [anyspec] * 4,
        out_specs=[kv_spec, kv_spec, t_spec, t_spec],
        out_shape=[jax.ShapeDtypeStruct(a.shape, BF16) for a in (k2, v2, kt, vt)],
        input_output_aliases={3: 0, 4: 1, 5: 2, 6: 3},
        compiler_params=_params(1),
    )(pc, knw, bd128, k2, v2, kt, vt)


def _attn_forward(q, k2, vt, tq, fused):
    bsz, s, _ = q.shape
    sk = k2.shape[2]
    nq = s // tq
    total = bsz * N_KV * nq
    at_steps = [(0, True), (total // 4, True), (total - 1, False)]

    def body(*refs):
        (q_ref, k_ref, vt_ref), (o_ref, lse_ref), _ = _split_fused(refs, 3, 2, 0, fused)
        g = pl.program_id(1)
        step = (pl.program_id(0) * N_KV + g) * nq + pl.program_id(2)
        _run_phases(fused, step, at_steps, True)
        kk = k_ref[0, 0]
        lo_mask = _lo_mask(tq)
        vt_aug = jnp.concatenate([vt_ref[0, pl.ds(pl.multiple_of(g * HEAD_DIM, HEAD_DIM), HEAD_DIM), :],
                                  jnp.ones((16, sk), BF16)], axis=0)
        ps, ms = [], []
        for j in range(2):
            qp = q_ref[0, :, 128 * j:128 * (j + 1)]
            for half in range(2):
                sel = lo_mask if half == 0 else jnp.logical_not(lo_mask)
                qs = jnp.where(sel, qp, jnp.zeros_like(qp))
                sc = lax.dot_general(qs, kk, NT, preferred_element_type=F32)
                m = jnp.max(sc, axis=-1, keepdims=True)
                ps.append(jnp.exp(sc - m).astype(BF16))
                ms.append(m)
        ots = [lax.dot_general(vt_aug, p, NT, preferred_element_type=F32) for p in ps]
        for j in range(2):
            o_t, l_t = [], []
            for half in range(2):
                ot = ots[2 * j + half]
                l = ot[HEAD_DIM:HEAD_DIM + 1, :]
                o_t.append(ot[0:HEAD_DIM, :] / l)
                l_t.append(jnp.broadcast_to(l, (HEAD_DIM, tq)))
            o_ref[0, :, 128 * j:128 * (j + 1)] = jnp.concatenate(o_t, axis=0).T
            lse_ref[0, :, 128 * j:128 * (j + 1)] = (jnp.where(lo_mask, ms[2 * j], ms[2 * j + 1])
                                                    + jnp.log(jnp.concatenate(l_t, axis=0).T))
        _run_phases(fused, step, at_steps, False)

    q_spec = pl.BlockSpec((1, tq, 256), lambda b, g, i: (b, i, g))
    kv_spec = pl.BlockSpec((1, 1, sk, 128), lambda b, g, i: (b, g, 0, 0))
    return pl.pallas_call(
        body, name="attn_forward", grid=(bsz, N_KV, nq),
        in_specs=[q_spec, kv_spec, pl.BlockSpec((1, 128, sk), lambda b, g, i: (b, 0, 0))] + fused.in_specs,
        out_specs=[q_spec, q_spec] + fused.out_specs,
        out_shape=[jax.ShapeDtypeStruct((bsz, s, 512), F32)] * 2 + fused.out_shape,
        scratch_shapes=fused.scratch,
        compiler_params=_params(3),
    )(q, k2, vt, *fused.arrs)


def _halo_specs(width, ts, s):
    r = ts // HALO
    last = s // HALO - 1
    return [pl.BlockSpec((1, ts, width), lambda b, i: (b, i, 0)),
            pl.BlockSpec((1, HALO, width), lambda b, i: (b, jnp.maximum(i * r - 1, 0), 0)),
            pl.BlockSpec((1, HALO, width), lambda b, i: (b, jnp.minimum((i + 1) * r, last), 0))]


def _fill_ext(ext_ref, cur, prev, nxt, i, n_tiles, ts):
    ext_ref[0:HALO, :] = jnp.where(i > 0, prev, jnp.zeros_like(prev))
    ext_ref[HALO:HALO + ts, :] = cur
    ext_ref[HALO + ts:2 * HALO + ts, :] = jnp.where(i < n_tiles - 1, nxt, jnp.zeros_like(nxt))


def _fill_shifted(sh_ref, ext_ref, ts):
    n = ts + 2 * HALO - 8
    for r in range(1, 8):
        sh_ref[r - 1, 0:n, :] = ext_ref[pl.ds(r, n), :]


def _window(sh_ref, ext_ref, off, rows, r0=0):
    q, r = divmod(off, 8)
    if r == 0:
        return ext_ref[pl.ds(r0 + off, rows), :]
    return sh_ref[r - 1, pl.ds(r0 + 8 * q, rows), :]


def _conv_forward(ga, gg, conv_w, conv_b, ln_w, ln_b, w_pw, b_pw, ts):
    bsz, s, dc = ga.shape
    n_tiles = s // ts

    def body(a_ref, ap_ref, an_ref, g_ref, gp_ref, gn_ref, cw_ref, cb_ref, lw_ref, lb_ref, wp_ref, bp_ref,
             y_ref, cv_ref, ext_ref, sh_ref):
        i = pl.program_id(1)
        glu = lambda a, g: a * _sigmoid(g)
        _fill_ext(ext_ref, glu(a_ref[0], g_ref[0]), glu(ap_ref[0], gp_ref[0]), glu(an_ref[0], gn_ref[0]), i, n_tiles, ts)
        _fill_shifted(sh_ref, ext_ref, ts)
        acc = jnp.broadcast_to(cb_ref[...], (ts, dc))
        for j in range(CONV_WIDTH):
            acc = acc + cw_ref[j:j + 1, :] * _window(sh_ref, ext_ref, HALO - CONV_PAD + j, ts)
        y_ref[0] = acc
        mu = jnp.mean(acc, axis=-1, keepdims=True)
        yc = acc - mu
        var = jnp.mean(yc * yc, axis=-1, keepdims=True)
        yn = yc * lax.rsqrt(var + EPS) * lw_ref[...] + lb_ref[...]
        ys = yn * _sigmoid(yn)
        cv_ref[0] = jnp.dot(ys.astype(BF16), wp_ref[...], preferred_element_type=F32) + bp_ref[...]

    const = lambda a: pl.BlockSpec(a.shape, lambda b, i: (0,) * a.ndim)
    return pl.pallas_call(
        body, name="conv_forward", grid=(bsz, n_tiles),
        in_specs=_halo_specs(dc, ts, s) + _halo_specs(dc, ts, s)
        + [const(conv_w), const(conv_b), const(ln_w), const(ln_b), const(w_pw), const(b_pw)],
        out_specs=[pl.BlockSpec((1, ts, dc), lambda b, i: (b, i, 0))] * 2,
        out_shape=[jax.ShapeDtypeStruct((bsz, s, dc), F32)] * 2,
        scratch_shapes=[pltpu.VMEM((ts + 2 * HALO, dc), F32), pltpu.VMEM((7, ts + 2 * HALO, dc), F32)],
        compiler_params=_params(2),
    )(ga, ga, ga, gg, gg, gg, conv_w, conv_b, ln_w, ln_b, w_pw, b_pw)


def _outproj_loss(x, target, gate, o, za, cv, zc, w_out, tm):
    bsz, s, d = x.shape

    def body(x_ref, t_ref, gate_ref, o_ref, za_ref, cv_ref, zc_ref, w_ref,
             loss_ref, dh_ref, do_ref, dza_ref, dcv_ref, dzc_ref, dgate_ref, gw_ref):
        b, i = pl.program_id(0), pl.program_id(1)
        ov, cvv = o_ref[0], cv_ref[0]
        silu_a, dsilu_a = _silu_and_grad(za_ref[0])
        silu_c, dsilu_c = _silu_and_grad(zc_ref[0])
        mix = jnp.concatenate([ov * silu_a, cvv * silu_c], axis=1).astype(BF16)
        out = jnp.dot(mix, w_ref[...], preferred_element_type=F32)
        gate_v = gate_ref[0]
        err = x_ref[0] + gate_v * out - t_ref[0]
        dh = err * (1.0 / d)
        dh_ref[0] = dh
        dout = (dh * gate_v).astype(BF16)
        dmix = lax.dot_general(dout, w_ref[...], NT, preferred_element_type=F32)
        gw = lax.dot_general(mix, dout, TN, preferred_element_type=F32)
        dg = jnp.sum(dh * out, axis=0, keepdims=True)
        sq = jnp.sum(err * err)

        @pl.when(jnp.logical_and(b == 0, i == 0))
        def _():
            gw_ref[...] = gw

        @pl.when(jnp.logical_or(b > 0, i > 0))
        def _():
            gw_ref[...] += gw

        @pl.when(i == 0)
        def _():
            dgate_ref[0] = dg
            loss_ref[...] = jnp.zeros(loss_ref.shape, F32) + sq

        @pl.when(i > 0)
        def _():
            dgate_ref[0] += dg
            loss_ref[...] += sq

        dma, dmc = dmix[:, :D_ATTN], dmix[:, D_ATTN:]
        do_ref[0] = dma * silu_a
        dza_ref[0] = (dma * ov * dsilu_a).astype(BF16)
        dcv_ref[0] = dmc * silu_c
        dzc_ref[0] = (dmc * cvv * dsilu_c).astype(BF16)

    tok = lambda w: pl.BlockSpec((1, tm, w), lambda b, i: (b, i, 0))
    return pl.pallas_call(
        body, name="outproj_loss", grid=(bsz, s // tm),
        in_specs=[tok(d), tok(d), _bcast_spec(gate), tok(512), tok(512), tok(512), tok(512),
                  pl.BlockSpec(w_out.shape, lambda b, i: (0, 0))],
        out_specs=[pl.BlockSpec((1, 8, 128), lambda b, i: (b, 0, 0)), tok(d), tok(512), tok(512), tok(512), tok(512),
                   pl.BlockSpec((1, 1, d), lambda b, i: (b, 0, 0)), pl.BlockSpec((d, d), lambda b, i: (0, 0))],
        out_shape=[jax.ShapeDtypeStruct((bsz, 8, 128), F32), jax.ShapeDtypeStruct((bsz, s, d), F32),
                   jax.ShapeDtypeStruct((bsz, s, 512), F32), jax.ShapeDtypeStruct((bsz, s, 512), BF16),
                   jax.ShapeDtypeStruct((bsz, s, 512), F32), jax.ShapeDtypeStruct((bsz, s, 512), BF16),
                   jax.ShapeDtypeStruct((bsz, 1, d), F32), jax.ShapeDtypeStruct((d, d), F32)],
        compiler_params=_params(2),
    )(x, target, gate, o, za, cv, zc, w_out)


def _conv_token_backward(dcv, y, ln_w, ln_b, w_pw, tm):
    bsz, s, dc = dcv.shape

    def body(dcv_ref, y_ref, lw_ref, lb_ref, wp_ref, dy_ref, gwp_ref, st_ref):
        b, i = pl.program_id(0), pl.program_id(1)
        yv, dcvv = y_ref[0], dcv_ref[0]
        mu = jnp.mean(yv, axis=-1, keepdims=True)
        yc = yv - mu
        rstd = lax.rsqrt(jnp.mean(yc * yc, axis=-1, keepdims=True) + EPS)
        yhat = yc * rstd
        yn = yhat * lw_ref[...] + lb_ref[...]
        ys, dsilu = _silu_and_grad(yn)
        dcvb = dcvv.astype(BF16)
        gwp = lax.dot_general(ys.astype(BF16), dcvb, TN, preferred_element_type=F32)
        dys = lax.dot_general(dcvb, wp_ref[...], NT, preferred_element_type=F32)
        dyn = dys * dsilu
        dyhat = dyn * lw_ref[...]
        dy = rstd * (dyhat - jnp.mean(dyhat, axis=-1, keepdims=True)
                     - yhat * jnp.mean(dyhat * yhat, axis=-1, keepdims=True))
        dy_ref[0] = dy
        red = lambda v: jnp.sum(v, axis=0, keepdims=True)
        stats = jnp.concatenate([red(dcvv), red(dyn * yhat), red(dyn), red(dy), jnp.zeros((4, dc), F32)], axis=0)
        first = jnp.logical_and(b == 0, i == 0)

        @pl.when(first)
        def _():
            gwp_ref[...] = gwp
            st_ref[...] = stats

        @pl.when(jnp.logical_not(first))
        def _():
            gwp_ref[...] += gwp
            st_ref[...] += stats

    tok = pl.BlockSpec((1, tm, dc), lambda b, i: (b, i, 0))
    const = lambda a: pl.BlockSpec(a.shape, lambda b, i: (0,) * a.ndim)
    return pl.pallas_call(
        body, name="conv_token_backward", grid=(bsz, s // tm),
        in_specs=[tok, tok, const(ln_w), const(ln_b), const(w_pw)],
        out_specs=[tok, pl.BlockSpec((dc, dc), lambda b, i: (0, 0)), pl.BlockSpec((8, dc), lambda b, i: (0, 0))],
        out_shape=[jax.ShapeDtypeStruct((bsz, s, dc), F32), jax.ShapeDtypeStruct((dc, dc), F32),
                   jax.ShapeDtypeStruct((8, dc), F32)],
        compiler_params=_params(2),
    )(dcv, y, ln_w, ln_b, w_pw)


def _conv_backward(dy, ga, gg, conv_w, ts):
    bsz, s, dc = dy.shape
    n_tiles = s // ts

    def body(dy_ref, dyp_ref, dyn_ref, a_ref, ap_ref, an_ref, g_ref, gp_ref, gn_ref, cw_ref,
             da_ref, dg_ref, gcw_ref, dyext_ref, ugext_ref, dysh_ref, ugsh_ref, dug_ref, gacc_ref):
        b, i = pl.program_id(0), pl.program_id(1)
        av, sg = a_ref[0], _sigmoid(g_ref[0])
        glu = lambda a, g: a * _sigmoid(g)
        _fill_ext(dyext_ref, dy_ref[0], dyp_ref[0], dyn_ref[0], i, n_tiles, ts)
        _fill_ext(ugext_ref, av * sg, glu(ap_ref[0], gp_ref[0]), glu(an_ref[0], gn_ref[0]), i, n_tiles, ts)
        _fill_shifted(dysh_ref, dyext_ref, ts)
        _fill_shifted(ugsh_ref, ugext_ref, ts)
        gacc_ref[...] = jnp.zeros(gacc_ref.shape, F32)

        def row_block(r, carry):
            r0 = pl.multiple_of(r * CONV_ROWS, CONV_ROWS)
            dyb = dy_ref[0, pl.ds(r0, CONV_ROWS), :]
            acc = jnp.zeros((CONV_ROWS, dc), F32)
            for j in range(CONV_WIDTH):
                acc = acc + cw_ref[j:j + 1, :] * _window(dysh_ref, dyext_ref, HALO + CONV_PAD - j, CONV_ROWS, r0)
                prod = dyb * _window(ugsh_ref, ugext_ref, HALO - CONV_PAD + j, CONV_ROWS, r0)
                part = prod[0:8, :]
                for k in range(8, CONV_ROWS, 8):
                    part = part + prod[k:k + 8, :]
                gacc_ref[j] += part
            dug_ref[pl.ds(r0, CONV_ROWS), :] = acc
            return carry

        lax.fori_loop(0, ts // CONV_ROWS, row_block, 0)
        dug = dug_ref[...]
        gcw = jnp.sum(gacc_ref[...], axis=1)
        first = jnp.logical_and(b == 0, i == 0)

        @pl.when(first)
        def _():
            gcw_ref[...] = gcw

        @pl.when(jnp.logical_not(first))
        def _():
            gcw_ref[...] += gcw

        da_ref[0] = (dug * sg).astype(BF16)
        dg_ref[0] = (dug * av * sg * (1.0 - sg)).astype(BF16)

    tok = pl.BlockSpec((1, ts, dc), lambda b, i: (b, i, 0))
    return pl.pallas_call(
        body, name="conv_backward", grid=(bsz, n_tiles),
        in_specs=_halo_specs(dc, ts, s) + _halo_specs(dc, ts, s) + _halo_specs(dc, ts, s)
        + [pl.BlockSpec(conv_w.shape, lambda b, i: (0, 0))],
        out_specs=[tok, tok, pl.BlockSpec((32, dc), lambda b, i: (0, 0))],
        out_shape=[jax.ShapeDtypeStruct((bsz, s, dc), BF16), jax.ShapeDtypeStruct((bsz, s, dc), BF16),
                   jax.ShapeDtypeStruct((32, dc), F32)],
        scratch_shapes=[pltpu.VMEM((ts + 2 * HALO, dc), F32), pltpu.VMEM((ts + 2 * HALO, dc), F32),
                        pltpu.VMEM((7, ts + 2 * HALO, dc), F32), pltpu.VMEM((7, ts + 2 * HALO, dc), F32),
                        pltpu.VMEM((ts, dc), F32), pltpu.VMEM((32, 8, dc), F32)],
        compiler_params=_params(2),
    )(dy, dy, dy, ga, ga, ga, gg, gg, gg, conv_w)


def _attn_backward(q, k2, v2, kt, o, do, lse, tq, fused):
    bsz, s, _ = q.shape
    sk = k2.shape[2]
    scale = 1.0 / math.sqrt(HEAD_DIM)
    nq = s // tq
    total = bsz * N_KV * nq
    at_steps = [(0, True), (total // 5, True), (total // 2, True), (total - 1, False)]

    def body(*refs):
        (q_ref, k_ref, v_ref, kt_ref, o_ref, do_ref, lse_ref), (dq_ref, dk_ref, dv_ref), _ = _split_fused(
            refs, 7, 3, 0, fused)
        g, i = pl.program_id(1), pl.program_id(2)
        step = (pl.program_id(0) * N_KV + g) * nq + i
        _run_phases(fused, step, at_steps, True)
        kk, vv = k_ref[0, 0], v_ref[0, 0]
        kgt = kt_ref[0, pl.ds(pl.multiple_of(g * HEAD_DIM, HEAD_DIM), HEAD_DIM), :]
        lo_mask = _lo_mask(tq)
        dk_acc = jnp.zeros((HEAD_DIM, sk), F32)
        dv_acc = jnp.zeros((HEAD_DIM, sk), F32)
        for j in range(2):
            cols = slice(128 * j, 128 * (j + 1))
            qp, dop, lsep = q_ref[0, :, cols], do_ref[0, :, cols], lse_ref[0, :, cols]
            dprod = dop * o_ref[0, :, cols]
            q_t = qp.astype(F32).T.astype(BF16)
            do_t = dop.T.astype(BF16)
            dq_t = []
            for half in range(2):
                sel = lo_mask if half == 0 else jnp.logical_not(lo_mask)
                rows = slice(HEAD_DIM * half, HEAD_DIM * (half + 1))
                qs = jnp.where(sel, qp, jnp.zeros_like(qp))
                dos = jnp.where(sel, dop, 0.0).astype(BF16)
                lse_h = jnp.max(jnp.where(sel, lsep, -jnp.inf), axis=-1, keepdims=True)
                delta = jnp.sum(jnp.where(sel, dprod, 0.0), axis=-1, keepdims=True)
                sc = lax.dot_general(qs, kk, NT, preferred_element_type=F32)
                p = jnp.exp(sc - lse_h)
                dp = lax.dot_general(dos, vv, NT, preferred_element_type=F32)
                ds = (p * (dp - delta)).astype(BF16)
                dv_acc = dv_acc + jnp.dot(do_t[rows, :], p.astype(BF16), preferred_element_type=F32)
                dk_acc = dk_acc + jnp.dot(q_t[rows, :], ds, preferred_element_type=F32)
                dq_t.append(lax.dot_general(kgt, ds, NT, preferred_element_type=F32))
            dq_ref[0, :, cols] = (jnp.concatenate(dq_t, axis=0) * scale).T

        @pl.when(i == 0)
        def _():
            dk_ref[0, 0] = dk_acc
            dv_ref[0, 0] = dv_acc

        @pl.when(i > 0)
        def _():
            dk_ref[0, 0] += dk_acc
            dv_ref[0, 0] += dv_acc

        _run_phases(fused, step, at_steps, False)

    q_spec = pl.BlockSpec((1, tq, 256), lambda b, g, i: (b, i, g))
    kv_spec = pl.BlockSpec((1, 1, sk, 128), lambda b, g, i: (b, g, 0, 0))
    acc_spec = pl.BlockSpec((1, 1, HEAD_DIM, sk), lambda b, g, i: (b, g, 0, 0))
    return pl.pallas_call(
        body, name="attn_backward", grid=(bsz, N_KV, nq),
        in_specs=[q_spec, kv_spec, kv_spec, pl.BlockSpec((1, 128, sk), lambda b, g, i: (b, 0, 0)), q_spec, q_spec,
                  q_spec] + fused.in_specs,
        out_specs=[q_spec, acc_spec, acc_spec] + fused.out_specs,
        out_shape=[jax.ShapeDtypeStruct((bsz, s, 512), F32), jax.ShapeDtypeStruct((bsz, 2, HEAD_DIM, sk), F32),
                   jax.ShapeDtypeStruct((bsz, 2, HEAD_DIM, sk), F32)] + fused.out_shape,
        scratch_shapes=fused.scratch,
        compiler_params=_params(3),
    )(q, k2, v2, kt, o, do, lse, *fused.arrs)


def _heads_to_lanes(acc_ref):
    return jnp.concatenate([acc_ref[0, 0], acc_ref[0, 1]], axis=0).T


def _norm_backward(dn, pre, w, bd):
    rstd = lax.rsqrt(_seg_mean(pre * pre, bd) + EPS)
    xhat = pre * rstd
    dxhat = dn * w
    return rstd * (dxhat - xhat * _seg_mean(dxhat * xhat, bd)), dn * xhat


def _qkv_backward(qkv, dq, dk2, dv2, cos, sin, qnw, knw, bd512, bd128, ts, row0):
    bsz, s, _ = qkv.shape

    def body(p_ref, dq_ref, dk_ref, dv_ref, cos_ref, sin_ref, qnw_ref, knw_ref, bd512_ref, bd128_ref, d_ref, gw_ref):
        b, i = pl.program_id(0), pl.program_id(1)
        lo_mask = _lo_mask(ts)
        cos_t, sin_t = cos_ref[...], sin_ref[...]
        dqr = dq_ref[0]
        dqn = dqr * _tile_lanes(cos_t, 4) + _partner(dqr * _tile_lanes(sin_t, 4))
        dqp, gq = _norm_backward(dqn, p_ref[0, :, 0:512], qnw_ref[...], bd512_ref[...])
        dkr = _heads_to_lanes(dk_ref)
        dkn = dkr * cos_t + _partner(dkr * sin_t)
        dkp, gk = _norm_backward(dkn, p_ref[0, :, 512:640], knw_ref[...], bd128_ref[...])
        dvp = _heads_to_lanes(dv_ref)
        d_ref[0] = jnp.concatenate([dqp, dkp, dvp], axis=1).astype(BF16)
        gk512 = jnp.concatenate([jnp.sum(gk, axis=0, keepdims=True), jnp.zeros((1, 384), F32)], axis=1)
        rows = jnp.concatenate([jnp.sum(gq, axis=0, keepdims=True), gk512, jnp.zeros((6, 512), F32)], axis=0)
        first = jnp.logical_and(b == 0, i == 0)

        @pl.when(first)
        def _():
            gw_ref[...] = rows

        @pl.when(jnp.logical_not(first))
        def _():
            gw_ref[...] += rows

    const = lambda a: pl.BlockSpec(a.shape, lambda b, i: (0,) * a.ndim)
    kv_spec = pl.BlockSpec((1, 2, HEAD_DIM, ts), lambda b, i: (b, 0, 0, i + row0 // ts))
    return pl.pallas_call(
        body, name="qkv_backward", grid=(bsz, s // ts),
        in_specs=[pl.BlockSpec((1, ts, 768), lambda b, i: (b, i, 0)), pl.BlockSpec((1, ts, 512), lambda b, i: (b, i, 0)),
                  kv_spec, kv_spec, pl.BlockSpec((ts, 128), lambda b, i: (i, 0)),
                  pl.BlockSpec((ts, 128), lambda b, i: (i, 0)), const(qnw), const(knw), const(bd512), const(bd128)],
        out_specs=[pl.BlockSpec((1, ts, 768), lambda b, i: (b, i, 0)), pl.BlockSpec((8, 512), lambda b, i: (0, 0))],
        out_shape=[jax.ShapeDtypeStruct((bsz, s, 768), BF16), jax.ShapeDtypeStruct((8, 512), F32)],
        compiler_params=_params(2),
    )(qkv, dq, dk2, dv2, cos, sin, qnw, knw, bd512, bd128)


def _ctx_kv_backward(pc, dk2, dv2, knw, bd128):
    bsz, cl, _ = pc.shape

    def body(p_ref, dk_ref, dv_ref, knw_ref, bd128_ref, d_ref, gw_ref):
        b = pl.program_id(0)
        lo_mask = _lo_mask(cl)
        dkn = _heads_to_lanes(dk_ref)
        dkp, gk = _norm_backward(dkn, p_ref[0, :, 0:128], knw_ref[...], bd128_ref[...])
        dvp = _heads_to_lanes(dv_ref)
        d_ref[0] = jnp.concatenate([dkp, dvp], axis=1).astype(BF16)
        rows = jnp.concatenate([jnp.sum(gk, axis=0, keepdims=True), jnp.zeros((7, 128), F32)], axis=0)

        @pl.when(b == 0)
        def _():
            gw_ref[...] = rows

        @pl.when(b > 0)
        def _():
            gw_ref[...] += rows

    const = lambda a: pl.BlockSpec(a.shape, lambda b: (0,) * a.ndim)
    kv_spec = pl.BlockSpec((1, 2, HEAD_DIM, cl), lambda b: (b, 0, 0, 0))
    return pl.pallas_call(
        body, name="ctx_kv_backward", grid=(bsz,),
        in_specs=[pl.BlockSpec((1, cl, 256), lambda b: (b, 0, 0)), kv_spec, kv_spec, const(knw), const(bd128)],
        out_specs=[pl.BlockSpec((1, cl, 256), lambda b: (b, 0, 0)), pl.BlockSpec((8, 128), lambda b: (0, 0))],
        out_shape=[jax.ShapeDtypeStruct((bsz, cl, 256), BF16), jax.ShapeDtypeStruct((8, 128), F32)],
        compiler_params=_params(1),
    )(pc, dk2, dv2, knw, bd128)


def _weight_grad(parts, u, init, tm, name):
    bsz, s, d = u.shape
    n_p = len(parts)
    nrows = sum(hi - lo for _, lo, hi in parts)

    def body(*refs):
        p_refs, u_ref = refs[:n_p], refs[n_p]
        gi_ref = refs[n_p + 1] if init is not None else None
        gw_ref = refs[-1]
        first = jnp.logical_and(pl.program_id(0) == 0, pl.program_id(1) == 0)
        dp = jnp.concatenate([r[0, :, lo:hi] for r, (_, lo, hi) in zip(p_refs, parts)], axis=1)
        gw = lax.dot_general(dp, u_ref[0], TN, preferred_element_type=F32)

        @pl.when(first)
        def _():
            gw_ref[...] = gw
            if init is not None:
                gw_ref[KV_LO:KV_HI, :] += gi_ref[...]

        @pl.when(jnp.logical_not(first))
        def _():
            gw_ref[...] += gw

    tok = lambda w: pl.BlockSpec((1, tm, w), lambda b, i: (b, i, 0))
    in_specs = [tok(a.shape[2]) for a, _, _ in parts] + [tok(d)]
    args = [a for a, _, _ in parts] + [u]
    if init is not None:
        in_specs.append(pl.BlockSpec(init.shape, lambda b, i: (0, 0)))
        args.append(init)
    return pl.pallas_call(
        body, name=name, grid=(bsz, s // tm), in_specs=in_specs,
        out_specs=pl.BlockSpec((nrows, d), lambda b, i: (0, 0)), out_shape=jax.ShapeDtypeStruct((nrows, d), F32),
        compiler_params=_params(2),
    )(*args)


def _inproj_backward(dps, x, dh, scale1p, norm_w, w_t, tm, name, fused=None):
    bsz, s, d = x.shape
    n_p = len(dps)
    shared = scale1p.shape[0] == 1
    with_dx = dh is not None
    n_in = n_p + (2 if with_dx else 1) + 3
    n_out = 3 if with_dx else 2
    total = bsz * (s // tm)
    at_steps = [(0, True), (total // 8, True), ((3 * total) // 4, True), (total - 1, False)]

    def body(*refs):
        ins, outs, _ = _split_fused(refs, n_in, n_out, 0, fused)
        dp_refs, x_ref = ins[:n_p], ins[n_p]
        dh_ref = ins[n_p + 1] if with_dx else None
        sc_ref, nw_ref, w_ref = ins[-3:]
        mod_ref, gnw_ref = outs[-2:]
        b, i = pl.program_id(0), pl.program_id(1)
        step = b * (s // tm) + i
        _run_phases(fused, step, at_steps, True)
        first = jnp.logical_and(b == 0, i == 0)
        dp = dp_refs[0][0] if n_p == 1 else jnp.concatenate([r[0] for r in dp_refs], axis=1)
        du = jnp.dot(dp, w_ref[...], preferred_element_type=F32)
        xv = x_ref[0]
        rstd = lax.rsqrt(jnp.mean(xv * xv, axis=-1, keepdims=True) + EPS)
        xhat = xv * rstd
        nw, sc = nw_ref[...], sc_ref[0]
        red = lambda v: jnp.sum(v, axis=0, keepdims=True)
        mod_rows = jnp.concatenate([red(du), red(du * (xhat * nw)), jnp.zeros((6, d), F32)], axis=0)
        gnw_rows = jnp.concatenate([red(du * sc * xhat), jnp.zeros((7, d), F32)], axis=0)
        mod_first = first if shared else i == 0

        @pl.when(mod_first)
        def _():
            mod_ref[0] = mod_rows

        @pl.when(jnp.logical_not(mod_first))
        def _():
            mod_ref[0] += mod_rows

        @pl.when(first)
        def _():
            gnw_ref[...] = gnw_rows

        @pl.when(jnp.logical_not(first))
        def _():
            gnw_ref[...] += gnw_rows

        if with_dx:
            dxhat = du * (nw * sc)
            outs[0][0] = dh_ref[0] + rstd * (dxhat - xhat * jnp.mean(dxhat * xhat, axis=-1, keepdims=True))
        _run_phases(fused, step, at_steps, False)

    tok = lambda w: pl.BlockSpec((1, tm, w), lambda b, i: (b, i, 0))
    in_specs = [tok(p.shape[2]) for p in dps] + [tok(d)]
    args = list(dps) + [x]
    if with_dx:
        in_specs.append(tok(d))
        args.append(dh)
    in_specs += [_bcast_spec(scale1p), pl.BlockSpec((1, d), lambda b, i: (0, 0)),
                 pl.BlockSpec(w_t.shape, lambda b, i: (0, 0))]
    args += [scale1p, norm_w, w_t]
    bm = scale1p.shape[0]
    mod_spec = pl.BlockSpec((1, 8, d), (lambda b, i: (0, 0, 0)) if shared else (lambda b, i: (b, 0, 0)))
    out_specs = [mod_spec, pl.BlockSpec((8, d), lambda b, i: (0, 0))]
    out_shape = [jax.ShapeDtypeStruct((bm, 8, d), F32), jax.ShapeDtypeStruct((8, d), F32)]
    if with_dx:
        out_specs.insert(0, tok(d))
        out_shape.insert(0, jax.ShapeDtypeStruct((bsz, s, d), F32))
    scratch = []
    if fused is not None:
        in_specs += fused.in_specs
        args += fused.arrs
        out_specs += fused.out_specs
        out_shape += fused.out_shape
        scratch = fused.scratch
    res = pl.pallas_call(
        body, name=name, grid=(bsz, s // tm), in_specs=in_specs, out_specs=out_specs, out_shape=out_shape,
        scratch_shapes=scratch, compiler_params=_params(2),
    )(*args)
    return list(res) if with_dx else [None] + list(res)


def _adamw_update(w_ref, g_ref, m_ref, v_ref, d_ref, nm_ref, nv_ref):
    gv = g_ref[...]
    mn = ADAM_B1 * m_ref[...] + (1.0 - ADAM_B1) * gv
    vn = ADAM_B2 * v_ref[...] + (1.0 - ADAM_B2) * (gv * gv)
    m_hat = mn / (1.0 - ADAM_B1 ** ADAM_STEP)
    v_hat = vn / (1.0 - ADAM_B2 ** ADAM_STEP)
    d_ref[...] = -ADAM_LR * (m_hat / (jnp.sqrt(v_hat) + ADAM_EPS) + ADAM_WD * w_ref[...])
    nm_ref[...] = mn
    nv_ref[...] = vn


def _adamw_small(ws, gs, ms, vs):
    n = len(ws)

    def body(*refs):
        ins, outs = refs[:4 * n], refs[4 * n:]
        for k in range(n):
            _adamw_update(ins[k], ins[n + k], ins[2 * n + k], ins[3 * n + k], outs[3 * k], outs[3 * k + 1],
                          outs[3 * k + 2])

    res = pl.pallas_call(
        body, name="adamw_small",
        out_shape=[jax.ShapeDtypeStruct(w.shape, F32) for w in ws for _ in range(3)], compiler_params=_params(),
    )(*ws, *gs, *ms, *vs)
    return [tuple(res[3 * k:3 * k + 3]) for k in range(n)]


def _adamw(w, g, m, v, name):
    r, cdim = w.shape
    tr = next((t for t in (256, 176) if r % t == 0 and r > t), r)

    def body(*refs):
        _adamw_update(*refs)

    spec = pl.BlockSpec((tr, cdim), lambda i: (i, 0))
    return pl.pallas_call(
        body, name=name, grid=(r // tr,), in_specs=[spec] * 4, out_specs=[spec] * 3,
        out_shape=[jax.ShapeDtypeStruct((r, cdim), F32)] * 3, compiler_params=_params(1),
    )(w, g, m, v)


def _rope_tables(s):
    rows = s // GRID_W
    freqs = np.float32(ROPE_THETA) ** (-np.arange(0, ROPE_AXIS_DIM, 2, dtype=np.float32) / np.float32(ROPE_AXIS_DIM))
    ang_r = np.arange(rows, dtype=np.float32)[:, None] * freqs[None, :]
    ang_c = np.arange(GRID_W, dtype=np.float32)[:, None] * freqs[None, :]
    zr, zc = np.zeros_like(ang_r), np.zeros_like(ang_c)

    def table(by_row, by_col):
        r = jnp.asarray(np.tile(np.concatenate(by_row + [zr, zr], axis=1), (1, 2)), dtype=F32)
        c = jnp.asarray(np.tile(np.concatenate([zc, zc] + by_col, axis=1), (1, 2)), dtype=F32)
        return jnp.repeat(r, GRID_W, axis=0) + jnp.tile(c, (rows, 1))

    return (table([np.cos(ang_r)] * 2, [np.cos(ang_c)] * 2),
            table([-np.sin(ang_r), np.sin(ang_r)], [-np.sin(ang_c), np.sin(ang_c)]))


def _pack_rows(parts, rows):
    flat = jnp.concatenate([p.reshape(-1) for p in parts])
    return jnp.pad(flat, (0, rows * D_MODEL - flat.shape[0])).reshape(rows, D_MODEL)


def kernel(x, c, ctx, c_ctx, w_mod, b_mod, norm_w, w_in, q_norm_w, k_norm_w, conv_w, conv_b, conv_ln_w, conv_ln_b, w_pw, b_pw, w_out, loss_target, m_c_ctx, m_w_mod, m_b_mod, m_norm_w, m_w_in, m_q_norm_w, m_k_norm_w, m_conv_w, m_conv_b, m_conv_ln_w, m_conv_ln_b, m_w_pw, m_b_pw, m_w_out, v_c_ctx, v_w_mod, v_b_mod, v_norm_w, v_w_in, v_q_norm_w, v_k_norm_w, v_conv_w, v_conv_b, v_conv_ln_w, v_conv_ln_b, v_w_pw, v_b_pw, v_w_out):
    bsz, s, d = x.shape
    cl = ctx.shape[1]
    xi, yi, ci = lax.axis_index("x"), lax.axis_index("y"), lax.axis_index("c")
    chip = 2 * xi + yi
    dev = 2 * chip + ci
    ncol_mod = w_mod.shape[2]

    w_in_t_loc = w_in[0].T.astype(BF16)
    b_cols = lax.dynamic_slice(b_mod, (0, chip * ncol_mod), (1, ncol_mod))
    sc_rows, mod_g, g_in = _front(jnp.pad(c, ((0, 8 - bsz), (0, 0))), jnp.pad(c_ctx[None, :], ((0, 15), (0, 0))),
                                  w_mod[0], b_cols, w_in_t_loc)
    w_in_t = g_in.reshape(D_IN, d)
    mod_all = mod_g.transpose(1, 0, 2).reshape(80, 3 * d)
    mod_loc = lax.dynamic_slice(mod_all, (8 * dev, 0), (bsz, 3 * d))
    shift, scale1p, gate = mod_loc[:, None, :d], 1.0 + mod_loc[:, None, d:2 * d], mod_loc[:, None, 2 * d:]
    shift_c, scale1p_c = mod_all[64:65, :d][None], 1.0 + mod_all[64:65, d:2 * d][None]

    cos, sin = _rope_tables(s)
    qnw512 = jnp.tile(q_norm_w, (1, 8))
    knw128 = jnp.tile(k_norm_w, (1, 2))
    bd512 = jnp.kron(jnp.eye(8, dtype=F32), jnp.ones((HEAD_DIM, HEAD_DIM), F32)).astype(BF16)
    bd128 = bd512[:128, :128]

    u, p_qkv, p_za, p_ga, p_gg, p_zc = _norm_inproj(x, shift, scale1p, norm_w, w_in_t, SPLITS, 512, "norm_inproj")
    uc, pc_kv = _norm_inproj(ctx, shift_c, scale1p_c, norm_w, w_in_t[KV_LO:KV_HI], ((0, 256),), cl, "ctx_norm_inproj")
    q, k2x, v2x, ktx, vtx = _qkv_prep(p_qkv, cos, sin, qnw512, knw128, bd512, bd128, 256, cl)
    k2, v2, kt, vt = _ctx_kv_prep(pc_kv, knw128, bd128, k2x, v2x, ktx, vtx)
    conv_w_loc = jnp.pad(conv_w[0], ((0, 1), (0, 0)))
    o, lse, g_out, g_pw, g_cw = _attn_forward(
        q, k2, vt, 256, _ChipGather([w_out[0].astype(BF16), w_pw[0].astype(BF16), conv_w_loc]))
    w_out_f = g_out.reshape(d, d)
    w_pw_f = g_pw.reshape(D_CONV, D_CONV)
    conv_w_f = g_cw.transpose(1, 0, 2).reshape(32, D_CONV)
    y, cv = _conv_forward(p_ga, p_gg, conv_w_f, conv_b, conv_ln_w, conv_ln_b, w_pw_f, b_pw, 256)
    loss_part, dh, do, dza, dcv, dzc, dgate, gw_out = _outproj_loss(
        x, loss_target, gate, o, p_za, cv, p_zc, w_out_f, 512)

    all_chips, half_rows = (0, 1, 2, 3), D_IN // 2
    dy, gw_pw, conv_stats = _conv_token_backward(dcv, y, conv_ln_w, conv_ln_b, w_pw_f, 256)
    da, dg, gcw = _conv_backward(dy, p_ga, p_gg, conv_w_f, 256)
    tw = min(1024, s)
    gw_hi = _weight_grad([(da, half_rows - SPLITS[2][0], 512), (dg, 0, 512), (dzc, 0, 512)], u, None, tw,
                         "grad_in_rows_hi")
    dq, dkt, dvt, r_out, r_pw, r_hi = _attn_backward(
        q, k2, v2, kt, o, do, lse, 256, _FusedReduce([(gw_out, all_chips), (gw_pw, all_chips), (gw_hi, (2, 3))]))
    dqkv, qk_stats = _qkv_backward(p_qkv, dq, dkt, dvt, cos, sin, qnw512, knw128, bd512, bd128, 256, cl)
    dpc, kc_stats = _ctx_kv_backward(pc_kv, dkt, dvt, knw128, bd128)
    gw_ctx = _weight_grad([(dpc, 0, 256)], uc, None, cl, "grad_in_rows_ctx")
    gw_lo = _weight_grad([(dqkv, 0, 768), (dza, 0, 512), (da, 0, half_rows - SPLITS[2][0])], u, gw_ctx, tw,
                         "grad_in_rows_lo")
    _, modc, gnw_c = _inproj_backward([dpc], ctx, None, scale1p_c, norm_w, w_in_t[KV_LO:KV_HI], cl,
                                      "ctx_inproj_backward")
    grad_x, modx, gnw_x, r_lo = _inproj_backward(
        [dqkv, dza, da, dg, dzc], x, dh, scale1p, norm_w, w_in_t, 512, "inproj_backward",
        _FusedReduce([(gw_lo, (0, 1))]))
    g_w_out, g_w_pw = r_out.reshape(d // 4, d), r_pw.reshape(D_CONV // 4, D_CONV)
    g_w_in_t = jnp.where(chip < 2, r_lo, r_hi).reshape(D_IN // 4, d)

    dmod_loc = jnp.concatenate([modx[:, 0, :], modx[:, 1, :], dgate[:, 0, :]], axis=1)
    gq = qk_stats[0].reshape(8, HEAD_DIM).sum(axis=0)
    gk = (qk_stats[1, :128] + kc_stats[0]).reshape(2, HEAD_DIM).sum(axis=0)
    packed = _pack_rows([dmod_loc, dmod_loc.sum(axis=0), gnw_x[0] + gnw_c[0], modc[0, 0], modc[0, 1], gq, gk,
                         conv_stats[0], conv_stats[1], conv_stats[2], conv_stats[3], gcw,
                         jnp.sum(loss_part[:, 0, 0])[None]], 32)
    total, g_w_mod, dsilu_ctx = _tail_exchange(packed, sc_rows, w_mod[0], bsz, 3 * bsz + 4)
    flat = total.reshape(-1)
    offs = [0]

    def take(nelem):
        lo = offs[0]
        offs[0] = lo + nelem
        return flat[lo:lo + nelem]

    take(bsz * 3 * d)
    g_b_mod_x = take(3 * d)
    g_norm_w = take(d)
    dshift_c, dscale_c = take(d), take(d)
    g_qnw, g_knw = take(HEAD_DIM), take(HEAD_DIM)
    g_b_pw, g_ln_w, g_ln_b, g_conv_b = take(D_CONV), take(D_CONV), take(D_CONV), take(D_CONV)
    g_conv_w_full = take(32 * D_CONV).reshape(32, D_CONV)
    loss = take(1)[0] * (0.5 / d)

    dmod_c = jnp.concatenate([dshift_c, dscale_c, jnp.zeros((d,), F32)])
    g_b_mod = (g_b_mod_x + dmod_c)[None, :]
    sg = _sigmoid(c_ctx)
    g_c_ctx = dsilu_ctx[0] * (sg * (1.0 + c_ctx * (1.0 - sg)))

    g_w_in = g_w_in_t.T
    g_conv_w = lax.dynamic_slice(g_conv_w_full, (0, chip * 128), (CONV_WIDTH, 128))

    grads = {
        "c_ctx": g_c_ctx, "w_mod": g_w_mod[None], "b_mod": g_b_mod, "norm_w": g_norm_w[None], "w_in": g_w_in[None],
        "q_norm_w": g_qnw[None], "k_norm_w": g_knw[None], "conv_w": g_conv_w[None], "conv_b": g_conv_b[None],
        "conv_ln_w": g_ln_w[None], "conv_ln_b": g_ln_b[None], "w_pw": g_w_pw[None], "b_pw": g_b_pw[None],
        "w_out": g_w_out[None],
    }
    weights = {
        "c_ctx": (c_ctx, m_c_ctx, v_c_ctx), "w_mod": (w_mod, m_w_mod, v_w_mod), "b_mod": (b_mod, m_b_mod, v_b_mod),
        "norm_w": (norm_w, m_norm_w, v_norm_w), "w_in": (w_in, m_w_in, v_w_in),
        "q_norm_w": (q_norm_w, m_q_norm_w, v_q_norm_w), "k_norm_w": (k_norm_w, m_k_norm_w, v_k_norm_w),
        "conv_w": (conv_w, m_conv_w, v_conv_w), "conv_b": (conv_b, m_conv_b, v_conv_b),
        "conv_ln_w": (conv_ln_w, m_conv_ln_w, v_conv_ln_w), "conv_ln_b": (conv_ln_b, m_conv_ln_b, v_conv_ln_b),
        "w_pw": (w_pw, m_w_pw, v_w_pw), "b_pw": (b_pw, m_b_pw, v_b_pw), "w_out": (w_out, m_w_out, v_w_out),
    }
    names = list(weights)
    big = ("w_mod", "w_in", "w_out")
    as_2d = lambda a: a.reshape((1, a.shape[0]) if a.ndim == 1 else (a.shape[-2] if a.ndim == 3 else 1, a.shape[-1]))
    small = [n for n in names if n not in big]
    w_g_m_v = zip(*[[as_2d(a) for a in (weights[n][0], grads[n], weights[n][1], weights[n][2])] for n in small])
    updates = dict(zip(small, _adamw_small(*[list(col) for col in w_g_m_v])))
    for n in ("w_mod", "w_out"):
        w, m, v = weights[n]
        updates[n] = _adamw(as_2d(w), as_2d(grads[n]), as_2d(m), as_2d(v), "adamw_" + n)
    w, m, v = weights["w_in"]
    updates["w_in"] = tuple(r.T for r in _adamw(w[0].T, g_w_in_t, m[0].T, v[0].T, "adamw_w_in"))
    deltas, new_ms, new_vs = ([updates[n][k].reshape(weights[n][0].shape) for n in names] for k in range(3))
    grads = {n: grads[n].reshape(weights[n][0].shape) for n in names}

    return (loss, grad_x, *[grads[n] for n in names], *deltas, *new_ms, *new_vs)
```

```python
import functools
import math

import jax
import jax.numpy as jnp
import numpy as np
from jax import lax
from jax.experimental import pallas as pl
from jax.experimental.pallas import tpu as pltpu

F32 = jnp.float32
BF16 = jnp.bfloat16
MESH = pl.DeviceIdType.MESH

D_MODEL = 1024
D_ATTN = 512
D_CONV = 512
HEAD_DIM = 64
N_KV = 2
GRID_W = 64
ROPE_AXIS_DIM = 32
ROPE_THETA = 10000.0
CONV_WIDTH = 31
CONV_PAD = 15
HALO = 16
CONV_ROWS = 32
EPS = 1e-6
SPLITS = ((0, 768), (768, 1280), (1280, 1792), (1792, 2304), (2304, 2816))
D_IN = 2816
KV_LO, KV_HI = 512, 768

ADAM_LR = 0.001
ADAM_B1 = 0.9
ADAM_B2 = 0.999
ADAM_EPS = 1e-08
ADAM_WD = 0.01
ADAM_STEP = 10

VMEM_LIMIT = 56 * 1024 * 1024

NT = (((1,), (1,)), ((), ()))
TN = (((0,), (0,)), ((), ()))


def _params(n_axes=0, **kw):
    if n_axes:
        kw["dimension_semantics"] = ("arbitrary",) * n_axes
    return pltpu.CompilerParams(vmem_limit_bytes=VMEM_LIMIT, **kw)


def _sigmoid(x):
    return 1.0 / (1.0 + jnp.exp(-x))


def _silu_and_grad(z):
    s = _sigmoid(z)
    return z * s, s * (1.0 + z * (1.0 - s))


def _seg_mean(v, ones_bd):
    hi = v.astype(BF16)
    lo = (v - hi.astype(F32)).astype(BF16)
    s = jnp.dot(hi, ones_bd, preferred_element_type=F32) + jnp.dot(lo, ones_bd, preferred_element_type=F32)
    return s * (1.0 / HEAD_DIM)


def _partner(v):
    n = v.shape[1]
    lane = lax.broadcasted_iota(jnp.int32, (v.shape[0], 128), 1)
    first = (lane % 32) < 16
    parts = []
    for k in range(n // 128):
        ch = v[:, 128 * k:128 * (k + 1)]
        parts.append(jnp.where(first, pltpu.roll(ch, 112, 1), pltpu.roll(ch, 16, 1)))
    return parts[0] if len(parts) == 1 else jnp.concatenate(parts, axis=1)


def _tile_lanes(t, reps):
    return t if reps == 1 else jnp.concatenate([t] * reps, axis=1)


def _lo_mask(rows):
    return lax.broadcasted_iota(jnp.int32, (rows, 128), 1) < HEAD_DIM


def _gather8_in_vmem(x_ref, out_ref, send_sems, recv_sems, local_sem):
    x, y, c = lax.axis_index("x"), lax.axis_index("y"), lax.axis_index("c")
    me, sibling = (x, y, c), (x, y, 1 - c)
    chips = [(1 - x, y), (x, 1 - y), (1 - x, 1 - y)]

    def slot(px, py, pc):
        return out_ref.at[4 * px + 2 * py + pc]

    def copy(k, block, to, src=None):
        return pltpu.make_async_remote_copy(
            src_ref=slot(*block) if src is None else src, dst_ref=slot(*block),
            send_sem=send_sems.at[k], recv_sem=recv_sems.at[k], device_id=to, device_id_type=MESH)

    mine = pltpu.make_async_copy(x_ref, slot(*me), local_sem)
    mine.start()
    first = [copy(0, me, sibling, src=x_ref)]
    first += [copy(1 + j, me, (*chip, c), src=x_ref) for j, chip in enumerate(chips)]
    for cp in first:
        cp.start()
    passed = [copy(4 + j, (*chip, c), sibling) for j, chip in enumerate(chips)]
    for j, chip in enumerate(chips):
        copy(1 + j, (*chip, c), me).wait_recv()
        passed[j].start()
    copy(0, sibling, me).wait_recv()
    for j, chip in enumerate(chips):
        copy(4 + j, (*chip, 1 - c), me).wait_recv()
    for cp in first + passed:
        cp.wait_send()
    mine.wait()


class _ChipGather:
    def __init__(self, arrs):
        self.arrs = list(arrs)
        n = self.n = len(self.arrs)
        self.in_specs = [pl.BlockSpec(memory_space=pl.ANY)] * n
        self.out_shape = [jax.ShapeDtypeStruct((4,) + a.shape, a.dtype) for a in self.arrs]
        self.out_specs = [pl.BlockSpec(memory_space=pl.ANY)] * n
        self.scratch = [pltpu.SemaphoreType.DMA((6 * n,)), pltpu.SemaphoreType.DMA((6 * n,)),
                        pltpu.SemaphoreType.DMA((n,))]
        self.phases = [self.start, self.forward, self.finish]

    def bind(self, ins, outs, scratch):
        self.ins, self.outs = ins, outs
        self.send_sems, self.recv_sems, self.local_sems = scratch
        self.x, self.y, self.c = lax.axis_index("x"), lax.axis_index("y"), lax.axis_index("c")
        self.chips = [(1 - self.x, self.y), (self.x, 1 - self.y), (1 - self.x, 1 - self.y)]
        self.mychip = 2 * self.x + self.y

    def _copy(self, a, k, chip_idx, cc, to, src=None):
        h = self.arrs[a].shape[0] // 2
        dst = self.outs[a].at[chip_idx, pl.ds(cc * h, h)]
        return pltpu.make_async_remote_copy(
            src_ref=dst if src is None else src, dst_ref=dst, send_sem=self.send_sems.at[6 * a + k],
            recv_sem=self.recv_sems.at[6 * a + k], device_id=to, device_id_type=MESH)

    def _local(self, a):
        return pltpu.make_async_copy(self.ins[a], self.outs[a].at[self.mychip], self.local_sems.at[a])

    def _first(self, a, j):
        h = self.arrs[a].shape[0] // 2
        return self._copy(a, j, self.mychip, self.c, (*self.chips[j], self.c), src=self.ins[a].at[pl.ds(self.c * h, h)])

    def _passed(self, a, j):
        cx, cy = self.chips[j]
        return self._copy(a, 3 + j, 2 * cx + cy, self.c, (self.x, self.y, 1 - self.c))

    def start(self):
        for a in range(self.n):
            self._local(a).start()
            for j in range(3):
                self._first(a, j).start()

    def forward(self):
        for a in range(self.n):
            for j, (cx, cy) in enumerate(self.chips):
                self._copy(a, j, 2 * cx + cy, self.c, (self.x, self.y, self.c)).wait_recv()
                self._passed(a, j).start()

    def finish(self):
        for a in range(self.n):
            for j, (cx, cy) in enumerate(self.chips):
                self._copy(a, 3 + j, 2 * cx + cy, 1 - self.c, (self.x, self.y, self.c)).wait_recv()
        for a in range(self.n):
            for j in range(3):
                self._first(a, j).wait_send()
                self._passed(a, j).wait_send()
            self._local(a).wait()


class _FusedReduce:
    def __init__(self, pieces):
        self.owners = [tuple(o) for _, o in pieces]
        self.arrs = [g.reshape(len(o), 2, g.shape[0] // (2 * len(o)), g.shape[1]) for g, o in pieces]
        n = self.n = len(pieces)
        hc = self.hc = [(v.shape[2], v.shape[3]) for v in self.arrs]
        nts = [len(o) for o in self.owners]
        self.base = [sum(nts[:p]) for p in range(n)]
        anyspec = pl.BlockSpec(memory_space=pl.ANY)
        self.in_specs = [anyspec] * n
        self.out_shape = [jax.ShapeDtypeStruct((2,) + s, F32) for s in hc]
        self.out_specs = [anyspec] * n
        self.scratch = [pltpu.VMEM((nt,) + s, F32) for nt, s in zip(nts, hc)]
        self.scratch += [pltpu.VMEM((nt,) + s, F32) for nt, s in zip(nts, hc)]
        self.scratch += [pltpu.VMEM(s, F32) for s in hc]
        self.scratch += [pltpu.VMEM((nt,) + s, BF16) for nt, s in zip(nts, hc)]
        self.scratch += [pltpu.VMEM((3,) + s, BF16) for s in hc]
        self.scratch += [pltpu.VMEM(s, F32) for s in hc]
        tot = sum(nts)
        self.scratch += [pltpu.SemaphoreType.DMA((tot,)), pltpu.SemaphoreType.DMA((tot,)),
                         pltpu.SemaphoreType.DMA((tot,)), pltpu.SemaphoreType.DMA((3 * n,)),
                         pltpu.SemaphoreType.DMA((n,)), pltpu.SemaphoreType.DMA((n,)), pltpu.SemaphoreType.DMA((n,)),
                         pltpu.SemaphoreType.DMA((tot,))]
        self.phases = [self.start, self.exchange, self.combine, self.finish]

    def bind(self, ins, outs, scratch):
        n = self.n
        self.g, self.out = ins, outs
        self.va, self.recv_a, self.own = scratch[:n], scratch[n:2 * n], scratch[2 * n:3 * n]
        self.tsend, self.recv_b, self.fin = scratch[3 * n:4 * n], scratch[4 * n:5 * n], scratch[5 * n:6 * n]
        self.sa, self.ra, self.sb, self.rb, self.sc, self.rc, self.lc, self.la = scratch[6 * n:]
        self.x, self.y, self.c = lax.axis_index("x"), lax.axis_index("y"), lax.axis_index("c")
        self.mychip = 2 * self.x + self.y
        self.sibling = (self.x, self.y, 1 - self.c)

    def _copy_a(self, p, t):
        k = self.base[p] + t
        return pltpu.make_async_remote_copy(
            src_ref=self.g[p].at[t, 1 - self.c], dst_ref=self.recv_a[p].at[t], send_sem=self.sa.at[k],
            recv_sem=self.ra.at[k], device_id=self.sibling, device_id_type=MESH)

    def _fetch(self, p, t):
        return pltpu.make_async_copy(self.g[p].at[t, self.c], self.va[p].at[t], self.la.at[self.base[p] + t])

    def _slot(self, owner):
        rel = jnp.bitwise_xor(self.mychip, owner)
        return jnp.where(rel == 2, 0, jnp.where(rel == 1, 1, 2))

    def _copy_b(self, p, t, slot):
        owner = self.owners[p][t]
        return pltpu.make_async_remote_copy(
            src_ref=self.tsend[p].at[t], dst_ref=self.recv_b[p].at[slot], send_sem=self.sb.at[self.base[p] + t],
            recv_sem=self.rb.at[3 * p + slot], device_id=(owner // 2, owner % 2, self.c), device_id_type=MESH)

    def _copy_c(self, p, half):
        return pltpu.make_async_remote_copy(
            src_ref=self.fin[p], dst_ref=self.out[p].at[half], send_sem=self.sc.at[p], recv_sem=self.rc.at[p],
            device_id=self.sibling, device_id_type=MESH)

    def _local_c(self, p):
        return pltpu.make_async_copy(self.fin[p], self.out[p].at[self.c], self.lc.at[p])

    def start(self):
        for p in range(self.n):
            for t in range(len(self.owners[p])):
                self._copy_a(p, t).start()
                self._fetch(p, t).start()

    def exchange(self):
        for p in range(self.n):
            for t, owner in enumerate(self.owners[p]):
                self._copy_a(p, t).wait_recv()
                self._fetch(p, t).wait()
                mine = self.mychip == owner

                @pl.when(mine)
                def _():
                    self.own[p][...] = self.va[p][t] + self.recv_a[p][t]

                @pl.when(jnp.logical_not(mine))
                def _():
                    self.tsend[p][t] = (self.va[p][t] + self.recv_a[p][t]).astype(BF16)
                    self._copy_b(p, t, self._slot(owner)).start()

    def combine(self):
        for p in range(self.n):
            for t, owner in enumerate(self.owners[p]):
                @pl.when(self.mychip == owner)
                def _():
                    acc = self.own[p][...]
                    for j in range(3):
                        self._copy_b(p, t, j).wait_recv()
                        acc = acc + self.recv_b[p][j].astype(F32)
                    self.fin[p][...] = acc
                    self._local_c(p).start()
                    self._copy_c(p, self.c).start()

    def finish(self):
        for p in range(self.n):
            for t, owner in enumerate(self.owners[p]):
                self._copy_a(p, t).wait_send()
                mine = self.mychip == owner

                @pl.when(mine)
                def _():
                    self._copy_c(p, 1 - self.c).wait_recv()
                    self._copy_c(p, self.c).wait_send()
                    self._local_c(p).wait()

                @pl.when(jnp.logical_not(mine))
                def _():
                    self._copy_b(p, t, self._slot(owner)).wait_send()


def _split_fused(refs, n_in, n_out, n_scr, fused):
    if fused is None:
        return refs[:n_in], refs[n_in:n_in + n_out], refs[n_in + n_out:]
    fi, fo = len(fused.in_specs), len(fused.out_specs)
    ins, rest = refs[:n_in], refs[n_in:]
    f_ins, rest = rest[:fi], rest[fi:]
    outs, rest = rest[:n_out], rest[n_out:]
    f_outs, rest = rest[:fo], rest[fo:]
    scr, f_scr = rest[:n_scr], rest[n_scr:]
    fused.bind(f_ins, f_outs, f_scr)
    return ins, outs, scr


def _run_phases(fused, step, at_steps, before):
    if fused is None:
        return
    for phase, (at, first) in zip(fused.phases, at_steps):
        if first == before:
            pl.when(step == at)(phase)


def _front(c_pad, c_ctx_rows, w_mod, b_cols, w_in_t_loc):
    ncol = w_mod.shape[1]
    gather = _ChipGather([w_in_t_loc])

    def body(c_ref, cctx_ref, w_ref, b_ref, win_ref, sc_ref, modg_ref, wing_ref,
             call_ref, ag_send, ag_recv, ag_local, m_send, m_recv, *g_scr):
        gather.bind([win_ref], [wing_ref], g_scr)
        _gather8_in_vmem(c_ref, call_ref, ag_send, ag_recv, ag_local)
        gather.start()
        x, y, c = lax.axis_index("x"), lax.axis_index("y"), lax.axis_index("c")
        chips = [(1 - x, y), (x, 1 - y), (1 - x, 1 - y)]
        mychip = 2 * x + y
        rows = jnp.concatenate([call_ref[dv] for dv in range(8)] + [cctx_ref[...]], axis=0)
        sc = rows * _sigmoid(rows)
        sc_ref[...] = sc
        modg_ref[mychip] = jnp.dot(sc, w_ref[...], preferred_element_type=F32,
                                   precision=lax.Precision.HIGHEST) + b_ref[...]

        def mcopy(j, chip_idx, to):
            return pltpu.make_async_remote_copy(
                src_ref=modg_ref.at[chip_idx], dst_ref=modg_ref.at[chip_idx], send_sem=m_send.at[j],
                recv_sem=m_recv.at[j], device_id=to, device_id_type=MESH)

        sends = [mcopy(j, mychip, (*chip, c)) for j, chip in enumerate(chips)]
        for cp in sends:
            cp.start()
        for j, (cx, cy) in enumerate(chips):
            mcopy(j, 2 * cx + cy, (x, y, c)).wait_recv()
        gather.forward()
        gather.finish()
        for cp in sends:
            cp.wait_send()

    vm = pl.BlockSpec(memory_space=pltpu.VMEM)
    return pl.pallas_call(
        body, name="front_exchange",
        out_shape=[jax.ShapeDtypeStruct((80, D_MODEL), F32), jax.ShapeDtypeStruct((4, 80, ncol), F32)] + gather.out_shape,
        in_specs=[vm, vm, vm, vm] + gather.in_specs, out_specs=[vm, vm] + gather.out_specs,
        scratch_shapes=[pltpu.VMEM((8, 8, D_MODEL), F32), pltpu.SemaphoreType.DMA((7,)), pltpu.SemaphoreType.DMA((7,)),
                        pltpu.SemaphoreType.DMA, pltpu.SemaphoreType.DMA((3,)), pltpu.SemaphoreType.DMA((3,))]
        + gather.scratch,
        compiler_params=_params(),
    )(c_pad, c_ctx_rows, w_mod, b_cols, w_in_t_loc)


def _tail_exchange(packed, sc_rows, w_mod, bsz, ctx_row):
    d = D_MODEL
    ncol = w_mod.shape[1]

    def body(p_ref, sc_ref, w_ref, total_ref, gw_ref, gcc_ref, gat_ref, dm_ref, part_ref,
             ag_send, ag_recv, ag_local, g_send, g_recv):
        _gather8_in_vmem(p_ref, gat_ref, ag_send, ag_recv, ag_local)
        acc = gat_ref[0]
        for dv in range(1, 8):
            acc = acc + gat_ref[dv]
        total_ref[...] = acc
        x, y, c = lax.axis_index("x"), lax.axis_index("y"), lax.axis_index("c")
        chips = [(1 - x, y), (x, 1 - y), (1 - x, 1 - y)]
        mychip = 2 * x + y
        dm_ref[...] = jnp.zeros(dm_ref.shape, F32)
        for k in range(4):
            @pl.when(mychip == k)
            def _():
                spans = [(seg, max(k * ncol, seg * d) - seg * d, min((k + 1) * ncol, (seg + 1) * d) - seg * d)
                         for seg in range(3) if k * ncol < (seg + 1) * d and (k + 1) * ncol > seg * d]
                for dv in range(8):
                    for b in range(bsz):
                        dm_ref[8 * dv + b:8 * dv + b + 1, :] = jnp.concatenate(
                            [gat_ref[dv, 3 * b + seg:3 * b + seg + 1, lo:hi] for seg, lo, hi in spans], axis=1)
                dm_ref[64:65, :] = jnp.concatenate(
                    [total_ref[ctx_row + seg:ctx_row + seg + 1, lo:hi] if seg < 2 else jnp.zeros((1, hi - lo), F32)
                     for seg, lo, hi in spans], axis=1)

        dm = dm_ref[...]
        gw_ref[...] = lax.dot_general(sc_ref[...], dm, TN, preferred_element_type=F32,
                                      precision=lax.Precision.HIGHEST)
        part_ref[mychip] = lax.dot_general(dm[64:72, :], w_ref[...], NT, preferred_element_type=F32,
                                           precision=lax.Precision.HIGHEST)

        def gcopy(j, chip_idx, to):
            return pltpu.make_async_remote_copy(
                src_ref=part_ref.at[chip_idx], dst_ref=part_ref.at[chip_idx], send_sem=g_send.at[j],
                recv_sem=g_recv.at[j], device_id=to, device_id_type=MESH)

        sends = [gcopy(j, mychip, (*chip, c)) for j, chip in enumerate(chips)]
        for cp in sends:
            cp.start()
        for j, (cx, cy) in enumerate(chips):
            gcopy(j, 2 * cx + cy, (x, y, c)).wait_recv()
        for cp in sends:
            cp.wait_send()
        gcc_ref[...] = (part_ref[0] + part_ref[1]) + (part_ref[2] + part_ref[3])

    return pl.pallas_call(
        body, name="tail_exchange",
        out_shape=[jax.ShapeDtypeStruct(packed.shape, F32), jax.ShapeDtypeStruct((d, ncol), F32),
                   jax.ShapeDtypeStruct((8, d), F32)],
        scratch_shapes=[pltpu.VMEM((8,) + packed.shape, F32), pltpu.VMEM((80, ncol), F32), pltpu.VMEM((4, 8, d), F32),
                        pltpu.SemaphoreType.DMA((7,)), pltpu.SemaphoreType.DMA((7,)), pltpu.SemaphoreType.DMA,
                        pltpu.SemaphoreType.DMA((3,)), pltpu.SemaphoreType.DMA((3,))],
        compiler_params=_params(),
    )(packed, sc_rows, w_mod)


def _bcast_spec(arr):
    if arr.shape[0] == 1:
        return pl.BlockSpec((1, 1, arr.shape[2]), lambda b, i: (0, 0, 0))
    return pl.BlockSpec((1, 1, arr.shape[2]), lambda b, i: (b, 0, 0))


def _norm_inproj(x, shift, scale1p, norm_w, w_t, splits, tm, name):
    bsz, s, d = x.shape

    def body(x_ref, sh_ref, sc_ref, nw_ref, w_ref, u_ref, *out_refs):
        xv = x_ref[0]
        rstd = lax.rsqrt(jnp.mean(xv * xv, axis=-1, keepdims=True) + EPS)
        u = (xv * rstd * nw_ref[...]) * sc_ref[0] + sh_ref[0]
        ub = u.astype(BF16)
        u_ref[0] = ub
        for (lo, hi), o_ref in zip(splits, out_refs):
            o_ref[0] = lax.dot_general(ub, w_ref[lo:hi, :], NT, preferred_element_type=F32)

    tok = lambda w: pl.BlockSpec((1, tm, w), lambda b, i: (b, i, 0))
    return pl.pallas_call(
        body, name=name, grid=(bsz, s // tm),
        in_specs=[tok(d), _bcast_spec(shift), _bcast_spec(scale1p), pl.BlockSpec((1, d), lambda b, i: (0, 0)),
                  pl.BlockSpec(w_t.shape, lambda b, i: (0, 0))],
        out_specs=[tok(d)] + [tok(hi - lo) for lo, hi in splits],
        out_shape=[jax.ShapeDtypeStruct((bsz, s, d), BF16)]
        + [jax.ShapeDtypeStruct((bsz, s, hi - lo), F32) for lo, hi in splits],
        compiler_params=_params(2),
    )(x, shift, scale1p, norm_w, w_t)


def _dup_heads(kv, lo_mask):
    r = pltpu.roll(kv, HEAD_DIM, 1)
    return jnp.where(lo_mask, kv, r), jnp.where(lo_mask, r, kv)


def _qkv_prep(qkv, cos, sin, qnw, knw, bd512, bd128, ts, row0):
    bsz, s, _ = qkv.shape

    def body(p_ref, cos_ref, sin_ref, qnw_ref, knw_ref, bd512_ref, bd128_ref, q_ref, k_ref, v_ref, kt_ref, vt_ref):
        lo_mask = _lo_mask(ts)
        cos_t, sin_t = cos_ref[...], sin_ref[...]
        qp = p_ref[0, :, 0:512]
        qn = qp * lax.rsqrt(_seg_mean(qp * qp, bd512_ref[...]) + EPS) * qnw_ref[...]
        qr = qn * _tile_lanes(cos_t, 4) + _partner(qn) * _tile_lanes(sin_t, 4)
        q_ref[0] = (qr * (1.0 / math.sqrt(HEAD_DIM))).astype(BF16)
        kp = p_ref[0, :, 512:640]
        kn = kp * lax.rsqrt(_seg_mean(kp * kp, bd128_ref[...]) + EPS) * knw_ref[...]
        kr = kn * cos_t + _partner(kn) * sin_t
        k0, k1 = _dup_heads(kr, lo_mask)
        k_ref[0, 0] = k0.astype(BF16)
        k_ref[0, 1] = k1.astype(BF16)
        vp = p_ref[0, :, 640:768]
        v0, v1 = _dup_heads(vp, lo_mask)
        v_ref[0, 0] = v0.astype(BF16)
        v_ref[0, 1] = v1.astype(BF16)
        kt_ref[0] = kr.T.astype(BF16)
        vt_ref[0] = vp.T.astype(BF16)

    const = lambda a: pl.BlockSpec(a.shape, lambda b, i: (0,) * a.ndim)
    kv_spec = pl.BlockSpec((1, 2, ts, 128), lambda b, i: (b, 0, i + row0 // ts, 0))
    t_spec = pl.BlockSpec((1, 128, ts), lambda b, i: (b, 0, i + row0 // ts))
    return pl.pallas_call(
        body, name="qkv_prep", grid=(bsz, s // ts),
        in_specs=[pl.BlockSpec((1, ts, 768), lambda b, i: (b, i, 0)),
                  pl.BlockSpec((ts, 128), lambda b, i: (i, 0)), pl.BlockSpec((ts, 128), lambda b, i: (i, 0)),
                  const(qnw), const(knw), const(bd512), const(bd128)],
        out_specs=[pl.BlockSpec((1, ts, 512), lambda b, i: (b, i, 0)), kv_spec, kv_spec, t_spec, t_spec],
        out_shape=[jax.ShapeDtypeStruct((bsz, s, 512), BF16), jax.ShapeDtypeStruct((bsz, 2, row0 + s, 128), BF16),
                   jax.ShapeDtypeStruct((bsz, 2, row0 + s, 128), BF16),
                   jax.ShapeDtypeStruct((bsz, 128, row0 + s), BF16), jax.ShapeDtypeStruct((bsz, 128, row0 + s), BF16)],
        compiler_params=_params(2),
    )(qkv, cos, sin, qnw, knw, bd512, bd128)


def _ctx_kv_prep(pc, knw, bd128, k2, v2, kt, vt):
    bsz, cl, _ = pc.shape

    def body(p_ref, knw_ref, bd128_ref, k_in, v_in, kt_in, vt_in, k_ref, v_ref, kt_ref, vt_ref):
        lo_mask = _lo_mask(cl)
        kp = p_ref[0, :, 0:128]
        kn = kp * lax.rsqrt(_seg_mean(kp * kp, bd128_ref[...]) + EPS) * knw_ref[...]
        k0, k1 = _dup_heads(kn, lo_mask)
        k_ref[0, 0] = k0.astype(BF16)
        k_ref[0, 1] = k1.astype(BF16)
        vp = p_ref[0, :, 128:256]
        v0, v1 = _dup_heads(vp, lo_mask)
        v_ref[0, 0] = v0.astype(BF16)
        v_ref[0, 1] = v1.astype(BF16)
        kt_ref[0] = kn.T.astype(BF16)
        vt_ref[0] = vp.T.astype(BF16)

    const = lambda a: pl.BlockSpec(a.shape, lambda b: (0,) * a.ndim)
    kv_spec = pl.BlockSpec((1, 2, cl, 128), lambda b: (b, 0, 0, 0))
    t_spec = pl.BlockSpec((1, 128, cl), lambda b: (b, 0, 0))
    anyspec = pl.BlockSpec(memory_space=pl.ANY)
    return pl.pallas_call(
        body, name="ctx_kv_prep", grid=(bsz,),
        in_specs=[pl.BlockSpec((1, cl, 256), lambda b: (b, 0, 0)), const(knw), const(bd128)] + [anyspec] * 4,
        out_specs=[kv_spec, kv_spec, t_spec, t_spec],
        out_shape=[jax.ShapeDtypeStruct(a.shape, BF16) for a in (k2, v2, kt, vt)],
        input_output_aliases={3: 0, 4: 1, 5: 2, 6: 3},
        compiler_params=_params(1),
    )(pc, knw, bd128, k2, v2, kt, vt)


def _attn_forward(q, k2, vt, tq, fused):
    bsz, s, _ = q.shape
    sk = k2.shape[2]
    nq = s // tq
    total = bsz * N_KV * nq
    at_steps = [(0, True), (total // 4, True), (total - 1, False)]

    def body(*refs):
        (q_ref, k_ref, vt_ref), (o_ref, lse_ref), _ = _split_fused(refs, 3, 2, 0, fused)
        g = pl.program_id(1)
        step = (pl.program_id(0) * N_KV + g) * nq + pl.program_id(2)
        _run_phases(fused, step, at_steps, True)
        kk = k_ref[0, 0]
        lo_mask = _lo_mask(tq)
        vt_aug = jnp.concatenate([vt_ref[0, pl.ds(pl.multiple_of(g * HEAD_DIM, HEAD_DIM), HEAD_DIM), :],
                                  jnp.ones((16, sk), BF16)], axis=0)
        ps, ms = [], []
        for j in range(2):
            qp = q_ref[0, :, 128 * j:128 * (j + 1)]
            for half in range(2):
                sel = lo_mask if half == 0 else jnp.logical_not(lo_mask)
                qs = jnp.where(sel, qp, jnp.zeros_like(qp))
                sc = lax.dot_general(qs, kk, NT, preferred_element_type=F32)
                m = jnp.max(sc, axis=-1, keepdims=True)
                ps.append(jnp.exp(sc - m).astype(BF16))
                ms.append(m)
        ots = [lax.dot_general(vt_aug, p, NT, preferred_element_type=F32) for p in ps]
        for j in range(2):
            o_t, l_t = [], []
            for half in range(2):
                ot = ots[2 * j + half]
                l = ot[HEAD_DIM:HEAD_DIM + 1, :]
                o_t.append(ot[0:HEAD_DIM, :] / l)
                l_t.append(jnp.broadcast_to(l, (HEAD_DIM, tq)))
            o_ref[0, :, 128 * j:128 * (j + 1)] = jnp.concatenate(o_t, axis=0).T
            lse_ref[0, :, 128 * j:128 * (j + 1)] = (jnp.where(lo_mask, ms[2 * j], ms[2 * j + 1])
                                                    + jnp.log(jnp.concatenate(l_t, axis=0).T))
        _run_phases(fused, step, at_steps, False)

    q_spec = pl.BlockSpec((1, tq, 256), lambda b, g, i: (b, i, g))
    kv_spec = pl.BlockSpec((1, 1, sk, 128), lambda b, g, i: (b, g, 0, 0))
    return pl.pallas_call(
        body, name="attn_forward", grid=(bsz, N_KV, nq),
        in_specs=[q_spec, kv_spec, pl.BlockSpec((1, 128, sk), lambda b, g, i: (b, 0, 0))] + fused.in_specs,
        out_specs=[q_spec, q_spec] + fused.out_specs,
        out_shape=[jax.ShapeDtypeStruct((bsz, s, 512), F32)] * 2 + fused.out_shape,
        scratch_shapes=fused.scratch,
        compiler_params=_params(3),
    )(q, k2, vt, *fused.arrs)


def _halo_specs(width, ts, s):
    r = ts // HALO
    last = s // HALO - 1
    return [pl.BlockSpec((1, ts, width), lambda b, i: (b, i, 0)),
            pl.BlockSpec((1, HALO, width), lambda b, i: (b, jnp.maximum(i * r - 1, 0), 0)),
            pl.BlockSpec((1, HALO, width), lambda b, i: (b, jnp.minimum((i + 1) * r, last), 0))]


def _fill_ext(ext_ref, cur, prev, nxt, i, n_tiles, ts):
    ext_ref[0:HALO, :] = jnp.where(i > 0, prev, jnp.zeros_like(prev))
    ext_ref[HALO:HALO + ts, :] = cur
    ext_ref[HALO + ts:2 * HALO + ts, :] = jnp.where(i < n_tiles - 1, nxt, jnp.zeros_like(nxt))


def _fill_shifted(sh_ref, ext_ref, ts):
    n = ts + 2 * HALO - 8
    for r in range(1, 8):
        sh_ref[r - 1, 0:n, :] = ext_ref[pl.ds(r, n), :]


def _window(sh_ref, ext_ref, off, rows, r0=0):
    q, r = divmod(off, 8)
    if r == 0:
        return ext_ref[pl.ds(r0 + off, rows), :]
    return sh_ref[r - 1, pl.ds(r0 + 8 * q, rows), :]


def _conv_forward(ga, gg, conv_w, conv_b, ln_w, ln_b, w_pw, b_pw, ts):
    bsz, s, dc = ga.shape
    n_tiles = s // ts

    def body(a_ref, ap_ref, an_ref, g_ref, gp_ref, gn_ref, cw_ref, cb_ref, lw_ref, lb_ref, wp_ref, bp_ref,
             y_ref, cv_ref, ext_ref, sh_ref):
        i = pl.program_id(1)
        glu = lambda a, g: a * _sigmoid(g)
        _fill_ext(ext_ref, glu(a_ref[0], g_ref[0]), glu(ap_ref[0], gp_ref[0]), glu(an_ref[0], gn_ref[0]), i, n_tiles, ts)
        _fill_shifted(sh_ref, ext_ref, ts)
        acc = jnp.broadcast_to(cb_ref[...], (ts, dc))
        for j in range(CONV_WIDTH):
            acc = acc + cw_ref[j:j + 1, :] * _window(sh_ref, ext_ref, HALO - CONV_PAD + j, ts)
        y_ref[0] = acc
        mu = jnp.mean(acc, axis=-1, keepdims=True)
        yc = acc - mu
        var = jnp.mean(yc * yc, axis=-1, keepdims=True)
        yn = yc * lax.rsqrt(var + EPS) * lw_ref[...] + lb_ref[...]
        ys = yn * _sigmoid(yn)
        cv_ref[0] = jnp.dot(ys.astype(BF16), wp_ref[...], preferred_element_type=F32) + bp_ref[...]

    const = lambda a: pl.BlockSpec(a.shape, lambda b, i: (0,) * a.ndim)
    return pl.pallas_call(
        body, name="conv_forward", grid=(bsz, n_tiles),
        in_specs=_halo_specs(dc, ts, s) + _halo_specs(dc, ts, s)
        + [const(conv_w), const(conv_b), const(ln_w), const(ln_b), const(w_pw), const(b_pw)],
        out_specs=[pl.BlockSpec((1, ts, dc), lambda b, i: (b, i, 0))] * 2,
        out_shape=[jax.ShapeDtypeStruct((bsz, s, dc), F32)] * 2,
        scratch_shapes=[pltpu.VMEM((ts + 2 * HALO, dc), F32), pltpu.VMEM((7, ts + 2 * HALO, dc), F32)],
        compiler_params=_params(2),
    )(ga, ga, ga, gg, gg, gg, conv_w, conv_b, ln_w, ln_b, w_pw, b_pw)


def _outproj_loss(x, target, gate, o, za, cv, zc, w_out, tm):
    bsz, s, d = x.shape

    def body(x_ref, t_ref, gate_ref, o_ref, za_ref, cv_ref, zc_ref, w_ref,
             loss_ref, dh_ref, do_ref, dza_ref, dcv_ref, dzc_ref, dgate_ref, gw_ref):
        b, i = pl.program_id(0), pl.program_id(1)
        ov, cvv = o_ref[0], cv_ref[0]
        silu_a, dsilu_a = _silu_and_grad(za_ref[0])
        silu_c, dsilu_c = _silu_and_grad(zc_ref[0])
        mix = jnp.concatenate([ov * silu_a, cvv * silu_c], axis=1).astype(BF16)
        out = jnp.dot(mix, w_ref[...], preferred_element_type=F32)
        gate_v = gate_ref[0]
        err = x_ref[0] + gate_v * out - t_ref[0]
        dh = err * (1.0 / d)
        dh_ref[0] = dh
        dout = (dh * gate_v).astype(BF16)
        dmix = lax.dot_general(dout, w_ref[...], NT, preferred_element_type=F32)
        gw = lax.dot_general(mix, dout, TN, preferred_element_type=F32)
        dg = jnp.sum(dh * out, axis=0, keepdims=True)
        sq = jnp.sum(err * err)

        @pl.when(jnp.logical_and(b == 0, i == 0))
        def _():
            gw_ref[...] = gw

        @pl.when(jnp.logical_or(b > 0, i > 0))
        def _():
            gw_ref[...] += gw

        @pl.when(i == 0)
        def _():
            dgate_ref[0] = dg
            loss_ref[...] = jnp.zeros(loss_ref.shape, F32) + sq

        @pl.when(i > 0)
        def _():
            dgate_ref[0] += dg
            loss_ref[...] += sq

        dma, dmc = dmix[:, :D_ATTN], dmix[:, D_ATTN:]
        do_ref[0] = dma * silu_a
        dza_ref[0] = (dma * ov * dsilu_a).astype(BF16)
        dcv_ref[0] = dmc * silu_c
        dzc_ref[0] = (dmc * cvv * dsilu_c).astype(BF16)

    tok = lambda w: pl.BlockSpec((1, tm, w), lambda b, i: (b, i, 0))
    return pl.pallas_call(
        body, name="outproj_loss", grid=(bsz, s // tm),
        in_specs=[tok(d), tok(d), _bcast_spec(gate), tok(512), tok(512), tok(512), tok(512),
                  pl.BlockSpec(w_out.shape, lambda b, i: (0, 0))],
        out_specs=[pl.BlockSpec((1, 8, 128), lambda b, i: (b, 0, 0)), tok(d), tok(512), tok(512), tok(512), tok(512),
                   pl.BlockSpec((1, 1, d), lambda b, i: (b, 0, 0)), pl.BlockSpec((d, d), lambda b, i: (0, 0))],
        out_shape=[jax.ShapeDtypeStruct((bsz, 8, 128), F32), jax.ShapeDtypeStruct((bsz, s, d), F32),
                   jax.ShapeDtypeStruct((bsz, s, 512), F32), jax.ShapeDtypeStruct((bsz, s, 512), BF16),
                   jax.ShapeDtypeStruct((bsz, s, 512), F32), jax.ShapeDtypeStruct((bsz, s, 512), BF16),
                   jax.ShapeDtypeStruct((bsz, 1, d), F32), jax.ShapeDtypeStruct((d, d), F32)],
        compiler_params=_params(2),
    )(x, target, gate, o, za, cv, zc, w_out)


def _conv_token_backward(dcv, y, ln_w, ln_b, w_pw, tm):
    bsz, s, dc = dcv.shape

    def body(dcv_ref, y_ref, lw_ref, lb_ref, wp_ref, dy_ref, gwp_ref, st_ref):
        b, i = pl.program_id(0), pl.program_id(1)
        yv, dcvv = y_ref[0], dcv_ref[0]
        mu = jnp.mean(yv, axis=-1, keepdims=True)
        yc = yv - mu
        rstd = lax.rsqrt(jnp.mean(yc * yc, axis=-1, keepdims=True) + EPS)
        yhat = yc * rstd
        yn = yhat * lw_ref[...] + lb_ref[...]
        ys, dsilu = _silu_and_grad(yn)
        dcvb = dcvv.astype(BF16)
        gwp = lax.dot_general(ys.astype(BF16), dcvb, TN, preferred_element_type=F32)
        dys = lax.dot_general(dcvb, wp_ref[...], NT, preferred_element_type=F32)
        dyn = dys * dsilu
        dyhat = dyn * lw_ref[...]
        dy = rstd * (dyhat - jnp.mean(dyhat, axis=-1, keepdims=True)
                     - yhat * jnp.mean(dyhat * yhat, axis=-1, keepdims=True))
        dy_ref[0] = dy
        red = lambda v: jnp.sum(v, axis=0, keepdims=True)
        stats = jnp.concatenate([red(dcvv), red(dyn * yhat), red(dyn), red(dy), jnp.zeros((4, dc), F32)], axis=0)
        first = jnp.logical_and(b == 0, i == 0)

        @pl.when(first)
        def _():
            gwp_ref[...] = gwp
            st_ref[...] = stats

        @pl.when(jnp.logical_not(first))
        def _():
            gwp_ref[...] += gwp
            st_ref[...] += stats

    tok = pl.BlockSpec((1, tm, dc), lambda b, i: (b, i, 0))
    const = lambda a: pl.BlockSpec(a.shape, lambda b, i: (0,) * a.ndim)
    return pl.pallas_call(
        body, name="conv_token_backward", grid=(bsz, s // tm),
        in_specs=[tok, tok, const(ln_w), const(ln_b), const(w_pw)],
        out_specs=[tok, pl.BlockSpec((dc, dc), lambda b, i: (0, 0)), pl.BlockSpec((8, dc), lambda b, i: (0, 0))],
        out_shape=[jax.ShapeDtypeStruct((bsz, s, dc), F32), jax.ShapeDtypeStruct((dc, dc), F32),
                   jax.ShapeDtypeStruct((8, dc), F32)],
        compiler_params=_params(2),
    )(dcv, y, ln_w, ln_b, w_pw)


def _conv_backward(dy, ga, gg, conv_w, ts):
    bsz, s, dc = dy.shape
    n_tiles = s // ts

    def body(dy_ref, dyp_ref, dyn_ref, a_ref, g_ref, cw_ref,
             da_ref, dg_ref, gcw_ref, dyext_ref, dysh_ref, ug_ref, dug_ref, gacc_ref):
        b, i = pl.program_id(0), pl.program_id(1)
        av, sg = a_ref[0], _sigmoid(g_ref[0])
        ug_ref[...] = av * sg
        _fill_ext(dyext_ref, dy_ref[0], dyp_ref[0], dyn_ref[0], i, n_tiles, ts)
        _fill_shifted(dysh_ref, dyext_ref, ts)
        gacc_ref[...] = jnp.zeros(gacc_ref.shape, F32)

        def row_block(r, carry):
            r0 = pl.multiple_of(r * CONV_ROWS, CONV_ROWS)
            ugb = ug_ref[pl.ds(r0, CONV_ROWS), :]
            acc = jnp.zeros((CONV_ROWS, dc), F32)
            for j in range(CONV_WIDTH):
                win = _window(dysh_ref, dyext_ref, HALO + CONV_PAD - j, CONV_ROWS, r0)
                acc = acc + cw_ref[j:j + 1, :] * win
                prod = ugb * win
                part = prod[0:8, :]
                for k in range(8, CONV_ROWS, 8):
                    part = part + prod[k:k + 8, :]
                gacc_ref[j] += part
            dug_ref[pl.ds(r0, CONV_ROWS), :] = acc
            return carry

        lax.fori_loop(0, ts // CONV_ROWS, row_block, 0)
        dug = dug_ref[...]
        gcw = jnp.sum(gacc_ref[...], axis=1)
        first = jnp.logical_and(b == 0, i == 0)

        @pl.when(first)
        def _():
            gcw_ref[...] = gcw

        @pl.when(jnp.logical_not(first))
        def _():
            gcw_ref[...] += gcw

        da_ref[0] = (dug * sg).astype(BF16)
        dg_ref[0] = (dug * av * sg * (1.0 - sg)).astype(BF16)

    tok = pl.BlockSpec((1, ts, dc), lambda b, i: (b, i, 0))
    return pl.pallas_call(
        body, name="conv_backward", grid=(bsz, n_tiles),
        in_specs=_halo_specs(dc, ts, s) + [tok, tok, pl.BlockSpec(conv_w.shape, lambda b, i: (0, 0))],
        out_specs=[tok, tok, pl.BlockSpec((32, dc), lambda b, i: (0, 0))],
        out_shape=[jax.ShapeDtypeStruct((bsz, s, dc), BF16), jax.ShapeDtypeStruct((bsz, s, dc), BF16),
                   jax.ShapeDtypeStruct((32, dc), F32)],
        scratch_shapes=[pltpu.VMEM((ts + 2 * HALO, dc), F32), pltpu.VMEM((7, ts + 2 * HALO, dc), F32),
                        pltpu.VMEM((ts, dc), F32), pltpu.VMEM((ts, dc), F32), pltpu.VMEM((32, 8, dc), F32)],
        compiler_params=_params(2),
    )(dy, dy, dy, ga, gg, conv_w)


def _attn_backward(q, k2, v2, kt, o, do, lse, tq, fused):
    bsz, s, _ = q.shape
    sk = k2.shape[2]
    scale = 1.0 / math.sqrt(HEAD_DIM)
    nq = s // tq
    total = bsz * N_KV * nq
    at_steps = [(0, True), (total // 5, True), (total // 2, True), (total - 1, False)]

    def body(*refs):
        (q_ref, k_ref, v_ref, kt_ref, o_ref, do_ref, lse_ref), (dq_ref, dk_ref, dv_ref), _ = _split_fused(
            refs, 7, 3, 0, fused)
        g, i = pl.program_id(1), pl.program_id(2)
        step = (pl.program_id(0) * N_KV + g) * nq + i
        _run_phases(fused, step, at_steps, True)
        kk, vv = k_ref[0, 0], v_ref[0, 0]
        kgt = kt_ref[0, pl.ds(pl.multiple_of(g * HEAD_DIM, HEAD_DIM), HEAD_DIM), :]
        lo_mask = _lo_mask(tq)
        dk_acc = jnp.zeros((HEAD_DIM, sk), F32)
        dv_acc = jnp.zeros((HEAD_DIM, sk), F32)
        for j in range(2):
            cols = slice(128 * j, 128 * (j + 1))
            qp, dop, lsep = q_ref[0, :, cols], do_ref[0, :, cols], lse_ref[0, :, cols]
            dprod = dop * o_ref[0, :, cols]
            q_t = qp.astype(F32).T.astype(BF16)
            do_t = dop.T.astype(BF16)
            dq_t = []
            for half in range(2):
                sel = lo_mask if half == 0 else jnp.logical_not(lo_mask)
                rows = slice(HEAD_DIM * half, HEAD_DIM * (half + 1))
                qs = jnp.where(sel, qp, jnp.zeros_like(qp))
                dos = jnp.where(sel, dop, 0.0).astype(BF16)
                lse_h = jnp.max(jnp.where(sel, lsep, -jnp.inf), axis=-1, keepdims=True)
                delta = jnp.sum(jnp.where(sel, dprod, 0.0), axis=-1, keepdims=True)
                sc = lax.dot_general(qs, kk, NT, preferred_element_type=F32)
                p = jnp.exp(sc - lse_h)
                dp = lax.dot_general(dos, vv, NT, preferred_element_type=F32)
                ds = (p * (dp - delta)).astype(BF16)
                dv_acc = dv_acc + jnp.dot(do_t[rows, :], p.astype(BF16), preferred_element_type=F32)
                dk_acc = dk_acc + jnp.dot(q_t[rows, :], ds, preferred_element_type=F32)
                dq_t.append(lax.dot_general(kgt, ds, NT, preferred_element_type=F32))
            dq_ref[0, :, cols] = (jnp.concatenate(dq_t, axis=0) * scale).T

        @pl.when(i == 0)
        def _():
            dk_ref[0, 0] = dk_acc
            dv_ref[0, 0] = dv_acc

        @pl.when(i > 0)
        def _():
            dk_ref[0, 0] += dk_acc
            dv_ref[0, 0] += dv_acc

        _run_phases(fused, step, at_steps, False)

    q_spec = pl.BlockSpec((1, tq, 256), lambda b, g, i: (b, i, g))
    kv_spec = pl.BlockSpec((1, 1, sk, 128), lambda b, g, i: (b, g, 0, 0))
    acc_spec = pl.BlockSpec((1, 1, HEAD_DIM, sk), lambda b, g, i: (b, g, 0, 0))
    return pl.pallas_call(
        body, name="attn_backward", grid=(bsz, N_KV, nq),
        in_specs=[q_spec, kv_spec, kv_spec, pl.BlockSpec((1, 128, sk), lambda b, g, i: (b, 0, 0)), q_spec, q_spec,
                  q_spec] + fused.in_specs,
        out_specs=[q_spec, acc_spec, acc_spec] + fused.out_specs,
        out_shape=[jax.ShapeDtypeStruct((bsz, s, 512), F32), jax.ShapeDtypeStruct((bsz, 2, HEAD_DIM, sk), F32),
                   jax.ShapeDtypeStruct((bsz, 2, HEAD_DIM, sk), F32)] + fused.out_shape,
        scratch_shapes=fused.scratch,
        compiler_params=_params(3),
    )(q, k2, v2, kt, o, do, lse, *fused.arrs)


def _heads_to_lanes(acc_ref):
    return jnp.concatenate([acc_ref[0, 0], acc_ref[0, 1]], axis=0).T


def _norm_backward(dn, pre, w, bd):
    rstd = lax.rsqrt(_seg_mean(pre * pre, bd) + EPS)
    xhat = pre * rstd
    dxhat = dn * w
    return rstd * (dxhat - xhat * _seg_mean(dxhat * xhat, bd)), dn * xhat


def _qkv_backward(qkv, dq, dk2, dv2, cos, sin, qnw, knw, bd512, bd128, ts, row0):
    bsz, s, _ = qkv.shape

    def body(p_ref, dq_ref, dk_ref, dv_ref, cos_ref, sin_ref, qnw_ref, knw_ref, bd512_ref, bd128_ref, d_ref, gw_ref):
        b, i = pl.program_id(0), pl.program_id(1)
        lo_mask = _lo_mask(ts)
        cos_t, sin_t = cos_ref[...], sin_ref[...]
        dqr = dq_ref[0]
        dqn = dqr * _tile_lanes(cos_t, 4) + _partner(dqr * _tile_lanes(sin_t, 4))
        dqp, gq = _norm_backward(dqn, p_ref[0, :, 0:512], qnw_ref[...], bd512_ref[...])
        dkr = _heads_to_lanes(dk_ref)
        dkn = dkr * cos_t + _partner(dkr * sin_t)
        dkp, gk = _norm_backward(dkn, p_ref[0, :, 512:640], knw_ref[...], bd128_ref[...])
        dvp = _heads_to_lanes(dv_ref)
        d_ref[0] = jnp.concatenate([dqp, dkp, dvp], axis=1).astype(BF16)
        gk512 = jnp.concatenate([jnp.sum(gk, axis=0, keepdims=True), jnp.zeros((1, 384), F32)], axis=1)
        rows = jnp.concatenate([jnp.sum(gq, axis=0, keepdims=True), gk512, jnp.zeros((6, 512), F32)], axis=0)
        first = jnp.logical_and(b == 0, i == 0)

        @pl.when(first)
        def _():
            gw_ref[...] = rows

        @pl.when(jnp.logical_not(first))
        def _():
            gw_ref[...] += rows

    const = lambda a: pl.BlockSpec(a.shape, lambda b, i: (0,) * a.ndim)
    kv_spec = pl.BlockSpec((1, 2, HEAD_DIM, ts), lambda b, i: (b, 0, 0, i + row0 // ts))
    return pl.pallas_call(
        body, name="qkv_backward", grid=(bsz, s // ts),
        in_specs=[pl.BlockSpec((1, ts, 768), lambda b, i: (b, i, 0)), pl.BlockSpec((1, ts, 512), lambda b, i: (b, i, 0)),
                  kv_spec, kv_spec, pl.BlockSpec((ts, 128), lambda b, i: (i, 0)),
                  pl.BlockSpec((ts, 128), lambda b, i: (i, 0)), const(qnw), const(knw), const(bd512), const(bd128)],
        out_specs=[pl.BlockSpec((1, ts, 768), lambda b, i: (b, i, 0)), pl.BlockSpec((8, 512), lambda b, i: (0, 0))],
        out_shape=[jax.ShapeDtypeStruct((bsz, s, 768), BF16), jax.ShapeDtypeStruct((8, 512), F32)],
        compiler_params=_params(2),
    )(qkv, dq, dk2, dv2, cos, sin, qnw, knw, bd512, bd128)


def _ctx_kv_backward(pc, dk2, dv2, knw, bd128):
    bsz, cl, _ = pc.shape

    def body(p_ref, dk_ref, dv_ref, knw_ref, bd128_ref, d_ref, gw_ref):
        b = pl.program_id(0)
        lo_mask = _lo_mask(cl)
        dkn = _heads_to_lanes(dk_ref)
        dkp, gk = _norm_backward(dkn, p_ref[0, :, 0:128], knw_ref[...], bd128_ref[...])
        dvp = _heads_to_lanes(dv_ref)
        d_ref[0] = jnp.concatenate([dkp, dvp], axis=1).astype(BF16)
        rows = jnp.concatenate([jnp.sum(gk, axis=0, keepdims=True), jnp.zeros((7, 128), F32)], axis=0)

        @pl.when(b == 0)
        def _():
            gw_ref[...] = rows

        @pl.when(b > 0)
        def _():
            gw_ref[...] += rows

    const = lambda a: pl.BlockSpec(a.shape, lambda b: (0,) * a.ndim)
    kv_spec = pl.BlockSpec((1, 2, HEAD_DIM, cl), lambda b: (b, 0, 0, 0))
    return pl.pallas_call(
        body, name="ctx_kv_backward", grid=(bsz,),
        in_specs=[pl.BlockSpec((1, cl, 256), lambda b: (b, 0, 0)), kv_spec, kv_spec, const(knw), const(bd128)],
        out_specs=[pl.BlockSpec((1, cl, 256), lambda b: (b, 0, 0)), pl.BlockSpec((8, 128), lambda b: (0, 0))],
        out_shape=[jax.ShapeDtypeStruct((bsz, cl, 256), BF16), jax.ShapeDtypeStruct((8, 128), F32)],
        compiler_params=_params(1),
    )(pc, dk2, dv2, knw, bd128)


def _weight_grad(parts, u, init, tm, name):
    bsz, s, d = u.shape
    n_p = len(parts)
    nrows = sum(hi - lo for _, lo, hi in parts)

    def body(*refs):
        p_refs, u_ref = refs[:n_p], refs[n_p]
        gi_ref = refs[n_p + 1] if init is not None else None
        gw_ref = refs[-1]
        first = jnp.logical_and(pl.program_id(0) == 0, pl.program_id(1) == 0)
        dp = jnp.concatenate([r[0, :, lo:hi] for r, (_, lo, hi) in zip(p_refs, parts)], axis=1)
        gw = lax.dot_general(dp, u_ref[0], TN, preferred_element_type=F32)

        @pl.when(first)
        def _():
            gw_ref[...] = gw
            if init is not None:
                gw_ref[KV_LO:KV_HI, :] += gi_ref[...]

        @pl.when(jnp.logical_not(first))
        def _():
            gw_ref[...] += gw

    tok = lambda w: pl.BlockSpec((1, tm, w), lambda b, i: (b, i, 0))
    in_specs = [tok(a.shape[2]) for a, _, _ in parts] + [tok(d)]
    args = [a for a, _, _ in parts] + [u]
    if init is not None:
        in_specs.append(pl.BlockSpec(init.shape, lambda b, i: (0, 0)))
        args.append(init)
    return pl.pallas_call(
        body, name=name, grid=(bsz, s // tm), in_specs=in_specs,
        out_specs=pl.BlockSpec((nrows, d), lambda b, i: (0, 0)), out_shape=jax.ShapeDtypeStruct((nrows, d), F32),
        compiler_params=_params(2),
    )(*args)


def _inproj_backward(dps, x, dh, scale1p, norm_w, w_t, tm, name, fused=None):
    bsz, s, d = x.shape
    n_p = len(dps)
    shared = scale1p.shape[0] == 1
    with_dx = dh is not None
    n_in = n_p + (2 if with_dx else 1) + 3
    n_out = 3 if with_dx else 2
    total = bsz * (s // tm)
    at_steps = [(0, True), (total // 8, True), ((3 * total) // 4, True), (total - 1, False)]

    def body(*refs):
        ins, outs, _ = _split_fused(refs, n_in, n_out, 0, fused)
        dp_refs, x_ref = ins[:n_p], ins[n_p]
        dh_ref = ins[n_p + 1] if with_dx else None
        sc_ref, nw_ref, w_ref = ins[-3:]
        mod_ref, gnw_ref = outs[-2:]
        b, i = pl.program_id(0), pl.program_id(1)
        step = b * (s // tm) + i
        _run_phases(fused, step, at_steps, True)
        first = jnp.logical_and(b == 0, i == 0)
        dp = dp_refs[0][0] if n_p == 1 else jnp.concatenate([r[0] for r in dp_refs], axis=1)
        du = jnp.dot(dp, w_ref[...], preferred_element_type=F32)
        xv = x_ref[0]
        rstd = lax.rsqrt(jnp.mean(xv * xv, axis=-1, keepdims=True) + EPS)
        xhat = xv * rstd
        nw, sc = nw_ref[...], sc_ref[0]
        red = lambda v: jnp.sum(v, axis=0, keepdims=True)
        mod_rows = jnp.concatenate([red(du), red(du * (xhat * nw)), jnp.zeros((6, d), F32)], axis=0)
        gnw_rows = jnp.concatenate([red(du * sc * xhat), jnp.zeros((7, d), F32)], axis=0)
        mod_first = first if shared else i == 0

        @pl.when(mod_first)
        def _():
            mod_ref[0] = mod_rows

        @pl.when(jnp.logical_not(mod_first))
        def _():
            mod_ref[0] += mod_rows

        @pl.when(first)
        def _():
            gnw_ref[...] = gnw_rows

        @pl.when(jnp.logical_not(first))
        def _():
            gnw_ref[...] += gnw_rows

        if with_dx:
            dxhat = du * (nw * sc)
            outs[0][0] = dh_ref[0] + rstd * (dxhat - xhat * jnp.mean(dxhat * xhat, axis=-1, keepdims=True))
        _run_phases(fused, step, at_steps, False)

    tok = lambda w: pl.BlockSpec((1, tm, w), lambda b, i: (b, i, 0))
    in_specs = [tok(p.shape[2]) for p in dps] + [tok(d)]
    args = list(dps) + [x]
    if with_dx:
        in_specs.append(tok(d))
        args.append(dh)
    in_specs += [_bcast_spec(scale1p), pl.BlockSpec((1, d), lambda b, i: (0, 0)),
                 pl.BlockSpec(w_t.shape, lambda b, i: (0, 0))]
    args += [scale1p, norm_w, w_t]
    bm = scale1p.shape[0]
    mod_spec = pl.BlockSpec((1, 8, d), (lambda b, i: (0, 0, 0)) if shared else (lambda b, i: (b, 0, 0)))
    out_specs = [mod_spec, pl.BlockSpec((8, d), lambda b, i: (0, 0))]
    out_shape = [jax.ShapeDtypeStruct((bm, 8, d), F32), jax.ShapeDtypeStruct((8, d), F32)]
    if with_dx:
        out_specs.insert(0, tok(d))
        out_shape.insert(0, jax.ShapeDtypeStruct((bsz, s, d), F32))
    scratch = []
    if fused is not None:
        in_specs += fused.in_specs
        args += fused.arrs
        out_specs += fused.out_specs
        out_shape += fused.out_shape
        scratch = fused.scratch
    res = pl.pallas_call(
        body, name=name, grid=(bsz, s // tm), in_specs=in_specs, out_specs=out_specs, out_shape=out_shape,
        scratch_shapes=scratch, compiler_params=_params(2),
    )(*args)
    return list(res) if with_dx else [None] + list(res)


def _adamw_update(w_ref, g_ref, m_ref, v_ref, d_ref, nm_ref, nv_ref):
    gv = g_ref[...]
    mn = ADAM_B1 * m_ref[...] + (1.0 - ADAM_B1) * gv
    vn = ADAM_B2 * v_ref[...] + (1.0 - ADAM_B2) * (gv * gv)
    m_hat = mn / (1.0 - ADAM_B1 ** ADAM_STEP)
    v_hat = vn / (1.0 - ADAM_B2 ** ADAM_STEP)
    d_ref[...] = -ADAM_LR * (m_hat / (jnp.sqrt(v_hat) + ADAM_EPS) + ADAM_WD * w_ref[...])
    nm_ref[...] = mn
    nv_ref[...] = vn


def _adamw_small(ws, gs, ms, vs):
    n = len(ws)

    def body(*refs):
        ins, outs = refs[:4 * n], refs[4 * n:]
        for k in range(n):
            _adamw_update(ins[k], ins[n + k], ins[2 * n + k], ins[3 * n + k], outs[3 * k], outs[3 * k + 1],
                          outs[3 * k + 2])

    res = pl.pallas_call(
        body, name="adamw_small",
        out_shape=[jax.ShapeDtypeStruct(w.shape, F32) for w in ws for _ in range(3)], compiler_params=_params(),
    )(*ws, *gs, *ms, *vs)
    return [tuple(res[3 * k:3 * k + 3]) for k in range(n)]


def _adamw(w, g, m, v, name):
    r, cdim = w.shape
    tr = next((t for t in (256, 176) if r % t == 0 and r > t), r)

    def body(*refs):
        _adamw_update(*refs)

    spec = pl.BlockSpec((tr, cdim), lambda i: (i, 0))
    return pl.pallas_call(
        body, name=name, grid=(r // tr,), in_specs=[spec] * 4, out_specs=[spec] * 3,
        out_shape=[jax.ShapeDtypeStruct((r, cdim), F32)] * 3, compiler_params=_params(1),
    )(w, g, m, v)


def _rope_tables(s):
    rows = s // GRID_W
    freqs = np.float32(ROPE_THETA) ** (-np.arange(0, ROPE_AXIS_DIM, 2, dtype=np.float32) / np.float32(ROPE_AXIS_DIM))
    ang_r = np.arange(rows, dtype=np.float32)[:, None] * freqs[None, :]
    ang_c = np.arange(GRID_W, dtype=np.float32)[:, None] * freqs[None, :]
    zr, zc = np.zeros_like(ang_r), np.zeros_like(ang_c)

    def table(by_row, by_col):
        r = jnp.asarray(np.tile(np.concatenate(by_row + [zr, zr], axis=1), (1, 2)), dtype=F32)
        c = jnp.asarray(np.tile(np.concatenate([zc, zc] + by_col, axis=1), (1, 2)), dtype=F32)
        return jnp.repeat(r, GRID_W, axis=0) + jnp.tile(c, (rows, 1))

    return (table([np.cos(ang_r)] * 2, [np.cos(ang_c)] * 2),
            table([-np.sin(ang_r), np.sin(ang_r)], [-np.sin(ang_c), np.sin(ang_c)]))


def _pack_rows(parts, rows):
    flat = jnp.concatenate([p.reshape(-1) for p in parts])
    return jnp.pad(flat, (0, rows * D_MODEL - flat.shape[0])).reshape(rows, D_MODEL)


def kernel(x, c, ctx, c_ctx, w_mod, b_mod, norm_w, w_in, q_norm_w, k_norm_w, conv_w, conv_b, conv_ln_w, conv_ln_b, w_pw, b_pw, w_out, loss_target, m_c_ctx, m_w_mod, m_b_mod, m_norm_w, m_w_in, m_q_norm_w, m_k_norm_w, m_conv_w, m_conv_b, m_conv_ln_w, m_conv_ln_b, m_w_pw, m_b_pw, m_w_out, v_c_ctx, v_w_mod, v_b_mod, v_norm_w, v_w_in, v_q_norm_w, v_k_norm_w, v_conv_w, v_conv_b, v_conv_ln_w, v_conv_ln_b, v_w_pw, v_b_pw, v_w_out):
    bsz, s, d = x.shape
    cl = ctx.shape[1]
    xi, yi, ci = lax.axis_index("x"), lax.axis_index("y"), lax.axis_index("c")
    chip = 2 * xi + yi
    dev = 2 * chip + ci
    ncol_mod = w_mod.shape[2]

    w_in_t_loc = w_in[0].T.astype(BF16)
    b_cols = lax.dynamic_slice(b_mod, (0, chip * ncol_mod), (1, ncol_mod))
    sc_rows, mod_g, g_in = _front(jnp.pad(c, ((0, 8 - bsz), (0, 0))), jnp.pad(c_ctx[None, :], ((0, 15), (0, 0))),
                                  w_mod[0], b_cols, w_in_t_loc)
    w_in_t = g_in.reshape(D_IN, d)
    mod_all = mod_g.transpose(1, 0, 2).reshape(80, 3 * d)
    mod_loc = lax.dynamic_slice(mod_all, (8 * dev, 0), (bsz, 3 * d))
    shift, scale1p, gate = mod_loc[:, None, :d], 1.0 + mod_loc[:, None, d:2 * d], mod_loc[:, None, 2 * d:]
    shift_c, scale1p_c = mod_all[64:65, :d][None], 1.0 + mod_all[64:65, d:2 * d][None]

    cos, sin = _rope_tables(s)
    qnw512 = jnp.tile(q_norm_w, (1, 8))
    knw128 = jnp.tile(k_norm_w, (1, 2))
    bd512 = jnp.kron(jnp.eye(8, dtype=F32), jnp.ones((HEAD_DIM, HEAD_DIM), F32)).astype(BF16)
    bd128 = bd512[:128, :128]

    u, p_qkv, p_za, p_ga, p_gg, p_zc = _norm_inproj(x, shift, scale1p, norm_w, w_in_t, SPLITS, 512, "norm_inproj")
    uc, pc_kv = _norm_inproj(ctx, shift_c, scale1p_c, norm_w, w_in_t[KV_LO:KV_HI], ((0, 256),), cl, "ctx_norm_inproj")
    q, k2x, v2x, ktx, vtx = _qkv_prep(p_qkv, cos, sin, qnw512, knw128, bd512, bd128, 256, cl)
    k2, v2, kt, vt = _ctx_kv_prep(pc_kv, knw128, bd128, k2x, v2x, ktx, vtx)
    conv_w_loc = jnp.pad(conv_w[0], ((0, 1), (0, 0)))
    o, lse, g_out, g_pw, g_cw = _attn_forward(
        q, k2, vt, 256, _ChipGather([w_out[0].astype(BF16), w_pw[0].astype(BF16), conv_w_loc]))
    w_out_f = g_out.reshape(d, d)
    w_pw_f = g_pw.reshape(D_CONV, D_CONV)
    conv_w_f = g_cw.transpose(1, 0, 2).reshape(32, D_CONV)
    y, cv = _conv_forward(p_ga, p_gg, conv_w_f, conv_b, conv_ln_w, conv_ln_b, w_pw_f, b_pw, 256)
    loss_part, dh, do, dza, dcv, dzc, dgate, gw_out = _outproj_loss(
        x, loss_target, gate, o, p_za, cv, p_zc, w_out_f, 512)

    all_chips, half_rows = (0, 1, 2, 3), D_IN // 2
    dy, gw_pw, conv_stats = _conv_token_backward(dcv, y, conv_ln_w, conv_ln_b, w_pw_f, 256)
    da, dg, gcw = _conv_backward(dy, p_ga, p_gg, conv_w_f, 256)
    tw = min(1024, s)
    gw_hi = _weight_grad([(da, half_rows - SPLITS[2][0], 512), (dg, 0, 512), (dzc, 0, 512)], u, None, tw,
                         "grad_in_rows_hi")
    dq, dkt, dvt, r_out, r_pw, r_hi = _attn_backward(
        q, k2, v2, kt, o, do, lse, 256, _FusedReduce([(gw_out, all_chips), (gw_pw, all_chips), (gw_hi, (2, 3))]))
    dqkv, qk_stats = _qkv_backward(p_qkv, dq, dkt, dvt, cos, sin, qnw512, knw128, bd512, bd128, 256, cl)
    dpc, kc_stats = _ctx_kv_backward(pc_kv, dkt, dvt, knw128, bd128)
    gw_ctx = _weight_grad([(dpc, 0, 256)], uc, None, cl, "grad_in_rows_ctx")
    gw_lo = _weight_grad([(dqkv, 0, 768), (dza, 0, 512), (da, 0, half_rows - SPLITS[2][0])], u, gw_ctx, tw,
                         "grad_in_rows_lo")
    _, modc, gnw_c = _inproj_backward([dpc], ctx, None, scale1p_c, norm_w, w_in_t[KV_LO:KV_HI], cl,
                                      "ctx_inproj_backward")
    grad_x, modx, gnw_x, r_lo = _inproj_backward(
        [dqkv, dza, da, dg, dzc], x, dh, scale1p, norm_w, w_in_t, 512, "inproj_backward",
        _FusedReduce([(gw_lo, (0, 1))]))
    g_w_out, g_w_pw = r_out.reshape(d // 4, d), r_pw.reshape(D_CONV // 4, D_CONV)
    g_w_in_t = jnp.where(chip < 2, r_lo, r_hi).reshape(D_IN // 4, d)

    dmod_loc = jnp.concatenate([modx[:, 0, :], modx[:, 1, :], dgate[:, 0, :]], axis=1)
    gq = qk_stats[0].reshape(8, HEAD_DIM).sum(axis=0)
    gk = (qk_stats[1, :128] + kc_stats[0]).reshape(2, HEAD_DIM).sum(axis=0)
    packed = _pack_rows([dmod_loc, dmod_loc.sum(axis=0), gnw_x[0] + gnw_c[0], modc[0, 0], modc[0, 1], gq, gk,
                         conv_stats[0], conv_stats[1], conv_stats[2], conv_stats[3], gcw,
                         jnp.sum(loss_part[:, 0, 0])[None]], 32)
    total, g_w_mod, dsilu_ctx = _tail_exchange(packed, sc_rows, w_mod[0], bsz, 3 * bsz + 4)
    flat = total.reshape(-1)
    offs = [0]

    def take(nelem):
        lo = offs[0]
        offs[0] = lo + nelem
        return flat[lo:lo + nelem]

    take(bsz * 3 * d)
    g_b_mod_x = take(3 * d)
    g_norm_w = take(d)
    dshift_c, dscale_c = take(d), take(d)
    g_qnw, g_knw = take(HEAD_DIM), take(HEAD_DIM)
    g_b_pw, g_ln_w, g_ln_b, g_conv_b = take(D_CONV), take(D_CONV), take(D_CONV), take(D_CONV)
    g_conv_w_full = take(32 * D_CONV).reshape(32, D_CONV)
    loss = take(1)[0] * (0.5 / d)

    dmod_c = jnp.concatenate([dshift_c, dscale_c, jnp.zeros((d,), F32)])
    g_b_mod = (g_b_mod_x + dmod_c)[None, :]
    sg = _sigmoid(c_ctx)
    g_c_ctx = dsilu_ctx[0] * (sg * (1.0 + c_ctx * (1.0 - sg)))

    g_w_in = g_w_in_t.T
    g_conv_w = lax.dynamic_slice(g_conv_w_full, (0, chip * 128), (CONV_WIDTH, 128))

    grads = {
        "c_ctx": g_c_ctx, "w_mod": g_w_mod[None], "b_mod": g_b_mod, "norm_w": g_norm_w[None], "w_in": g_w_in[None],
        "q_norm_w": g_qnw[None], "k_norm_w": g_knw[None], "conv_w": g_conv_w[None], "conv_b": g_conv_b[None],
        "conv_ln_w": g_ln_w[None], "conv_ln_b": g_ln_b[None], "w_pw": g_w_pw[None], "b_pw": g_b_pw[None],
        "w_out": g_w_out[None],
    }
    weights = {
        "c_ctx": (c_ctx, m_c_ctx, v_c_ctx), "w_mod": (w_mod, m_w_mod, v_w_mod), "b_mod": (b_mod, m_b_mod, v_b_mod),
        "norm_w": (norm_w, m_norm_w, v_norm_w), "w_in": (w_in, m_w_in, v_w_in),
        "q_norm_w": (q_norm_w, m_q_norm_w, v_q_norm_w), "k_norm_w": (k_norm_w, m_k_norm_w, v_k_norm_w),
        "conv_w": (conv_w, m_conv_w, v_conv_w), "conv_b": (conv_b, m_conv_b, v_conv_b),
        "conv_ln_w": (conv_ln_w, m_conv_ln_w, v_conv_ln_w), "conv_ln_b": (conv_ln_b, m_conv_ln_b, v_conv_ln_b),
        "w_pw": (w_pw, m_w_pw, v_w_pw), "b_pw": (b_pw, m_b_pw, v_b_pw), "w_out": (w_out, m_w_out, v_w_out),
    }
    names = list(weights)
    big = ("w_mod", "w_in", "w_out")
    as_2d = lambda a: a.reshape((1, a.shape[0]) if a.ndim == 1 else (a.shape[-2] if a.ndim == 3 else 1, a.shape[-1]))
    small = [n for n in names if n not in big]
    w_g_m_v = zip(*[[as_2d(a) for a in (weights[n][0], grads[n], weights[n][1], weights[n][2])] for n in small])
    updates = dict(zip(small, _adamw_small(*[list(col) for col in w_g_m_v])))
    for n in ("w_mod", "w_out"):
        w, m, v = weights[n]
        updates[n] = _adamw(as_2d(w), as_2d(grads[n]), as_2d(m), as_2d(v), "adamw_" + n)
    w, m, v = weights["w_in"]
    updates["w_in"] = tuple(r.T for r in _adamw(w[0].T, g_w_in_t, m[0].T, v[0].T, "adamw_w_in"))
    deltas, new_ms, new_vs = ([updates[n][k].reshape(weights[n][0].shape) for n in names] for k in range(3))
    grads = {n: grads[n].reshape(weights[n][0].shape) for n in names}

    return (loss, grad_x, *[grads[n] for n in names], *deltas, *new_ms, *new_vs)
```

```python
import functools
import math

import jax
import jax.numpy as jnp
import numpy as np
from jax import lax
from jax.experimental import pallas as pl
from jax.experimental.pallas import tpu as pltpu

F32 = jnp.float32
BF16 = jnp.bfloat16
MESH = pl.DeviceIdType.MESH

D_MODEL = 1024
D_ATTN = 512
D_CONV = 512
HEAD_DIM = 64
N_KV = 2
GRID_W = 64
ROPE_AXIS_DIM = 32
ROPE_THETA = 10000.0
CONV_WIDTH = 31
CONV_PAD = 15
HALO = 16
CONV_ROWS = 32
EPS = 1e-6
SPLITS = ((0, 768), (768, 1280), (1280, 1792), (1792, 2304), (2304, 2816))
D_IN = 2816
KV_LO, KV_HI = 512, 768

ADAM_LR = 0.001
ADAM_B1 = 0.9
ADAM_B2 = 0.999
ADAM_EPS = 1e-08
ADAM_WD = 0.01
ADAM_STEP = 10

VMEM_LIMIT = 56 * 1024 * 1024

NT = (((1,), (1,)), ((), ()))
TN = (((0,), (0,)), ((), ()))


def _params(n_axes=0, **kw):
    if n_axes:
        kw["dimension_semantics"] = ("arbitrary",) * n_axes
    return pltpu.CompilerParams(vmem_limit_bytes=VMEM_LIMIT, **kw)


def _sigmoid(x):
    return 1.0 / (1.0 + jnp.exp(-x))


def _silu_and_grad(z):
    s = _sigmoid(z)
    return z * s, s * (1.0 + z * (1.0 - s))


def _seg_mean(v, ones_bd):
    hi = v.astype(BF16)
    lo = (v - hi.astype(F32)).astype(BF16)
    s = jnp.dot(hi, ones_bd, preferred_element_type=F32) + jnp.dot(lo, ones_bd, preferred_element_type=F32)
    return s * (1.0 / HEAD_DIM)


def _partner(v):
    n = v.shape[1]
    lane = lax.broadcasted_iota(jnp.int32, (v.shape[0], 128), 1)
    first = (lane % 32) < 16
    parts = []
    for k in range(n // 128):
        ch = v[:, 128 * k:128 * (k + 1)]
        parts.append(jnp.where(first, pltpu.roll(ch, 112, 1), pltpu.roll(ch, 16, 1)))
    return parts[0] if len(parts) == 1 else jnp.concatenate(parts, axis=1)


def _tile_lanes(t, reps):
    return t if reps == 1 else jnp.concatenate([t] * reps, axis=1)


def _lo_mask(rows):
    return lax.broadcasted_iota(jnp.int32, (rows, 128), 1) < HEAD_DIM


def _gather8_in_vmem(x_ref, out_ref, send_sems, recv_sems, local_sem):
    x, y, c = lax.axis_index("x"), lax.axis_index("y"), lax.axis_index("c")
    me, sibling = (x, y, c), (x, y, 1 - c)
    chips = [(1 - x, y), (x, 1 - y), (1 - x, 1 - y)]

    def slot(px, py, pc):
        return out_ref.at[4 * px + 2 * py + pc]

    def copy(k, block, to, src=None):
        return pltpu.make_async_remote_copy(
            src_ref=slot(*block) if src is None else src, dst_ref=slot(*block),
            send_sem=send_sems.at[k], recv_sem=recv_sems.at[k], device_id=to, device_id_type=MESH)

    mine = pltpu.make_async_copy(x_ref, slot(*me), local_sem)
    mine.start()
    first = [copy(0, me, sibling, src=x_ref)]
    first += [copy(1 + j, me, (*chip, c), src=x_ref) for j, chip in enumerate(chips)]
    for cp in first:
        cp.start()
    passed = [copy(4 + j, (*chip, c), sibling) for j, chip in enumerate(chips)]
    for j, chip in enumerate(chips):
        copy(1 + j, (*chip, c), me).wait_recv()
        passed[j].start()
    copy(0, sibling, me).wait_recv()
    for j, chip in enumerate(chips):
        copy(4 + j, (*chip, 1 - c), me).wait_recv()
    for cp in first + passed:
        cp.wait_send()
    mine.wait()


class _ChipGather:
    def __init__(self, arrs):
        self.arrs = list(arrs)
        n = self.n = len(self.arrs)
        self.in_specs = [pl.BlockSpec(memory_space=pl.ANY)] * n
        self.out_shape = [jax.ShapeDtypeStruct((4,) + a.shape, a.dtype) for a in self.arrs]
        self.out_specs = [pl.BlockSpec(memory_space=pl.ANY)] * n
        self.scratch = [pltpu.SemaphoreType.DMA((6 * n,)), pltpu.SemaphoreType.DMA((6 * n,)),
                        pltpu.SemaphoreType.DMA((n,))]
        self.phases = [self.start, self.forward, self.finish]

    def bind(self, ins, outs, scratch):
        self.ins, self.outs = ins, outs
        self.send_sems, self.recv_sems, self.local_sems = scratch
        self.x, self.y, self.c = lax.axis_index("x"), lax.axis_index("y"), lax.axis_index("c")
        self.chips = [(1 - self.x, self.y), (self.x, 1 - self.y), (1 - self.x, 1 - self.y)]
        self.mychip = 2 * self.x + self.y

    def _copy(self, a, k, chip_idx, cc, to, src=None):
        h = self.arrs[a].shape[0] // 2
        dst = self.outs[a].at[chip_idx, pl.ds(cc * h, h)]
        return pltpu.make_async_remote_copy(
            src_ref=dst if src is None else src, dst_ref=dst, send_sem=self.send_sems.at[6 * a + k],
            recv_sem=self.recv_sems.at[6 * a + k], device_id=to, device_id_type=MESH)

    def _local(self, a):
        return pltpu.make_async_copy(self.ins[a], self.outs[a].at[self.mychip], self.local_sems.at[a])

    def _first(self, a, j):
        h = self.arrs[a].shape[0] // 2
        return self._copy(a, j, self.mychip, self.c, (*self.chips[j], self.c), src=self.ins[a].at[pl.ds(self.c * h, h)])

    def _passed(self, a, j):
        cx, cy = self.chips[j]
        return self._copy(a, 3 + j, 2 * cx + cy, self.c, (self.x, self.y, 1 - self.c))

    def start(self):
        for a in range(self.n):
            self._local(a).start()
            for j in range(3):
                self._first(a, j).start()

    def forward(self):
        for a in range(self.n):
            for j, (cx, cy) in enumerate(self.chips):
                self._copy(a, j, 2 * cx + cy, self.c, (self.x, self.y, self.c)).wait_recv()
                self._passed(a, j).start()

    def finish(self):
        for a in range(self.n):
            for j, (cx, cy) in enumerate(self.chips):
                self._copy(a, 3 + j, 2 * cx + cy, 1 - self.c, (self.x, self.y, self.c)).wait_recv()
        for a in range(self.n):
            for j in range(3):
                self._first(a, j).wait_send()
                self._passed(a, j).wait_send()
            self._local(a).wait()


class _FusedReduce:
    def __init__(self, pieces):
        self.owners = [tuple(o) for _, o in pieces]
        self.arrs = [g.reshape(len(o), 2, g.shape[0] // (2 * len(o)), g.shape[1]) for g, o in pieces]
        n = self.n = len(pieces)
        hc = self.hc = [(v.shape[2], v.shape[3]) for v in self.arrs]
        nts = [len(o) for o in self.owners]
        self.base = [sum(nts[:p]) for p in range(n)]
        anyspec = pl.BlockSpec(memory_space=pl.ANY)
        self.in_specs = [anyspec] * n
        self.out_shape = [jax.ShapeDtypeStruct((2,) + s, F32) for s in hc]
        self.out_specs = [anyspec] * n
        self.scratch = [pltpu.VMEM((nt,) + s, F32) for nt, s in zip(nts, hc)]
        self.scratch += [pltpu.VMEM((nt,) + s, F32) for nt, s in zip(nts, hc)]
        self.scratch += [pltpu.VMEM(s, F32) for s in hc]
        self.scratch += [pltpu.VMEM((nt,) + s, BF16) for nt, s in zip(nts, hc)]
        self.scratch += [pltpu.VMEM((3,) + s, BF16) for s in hc]
        self.scratch += [pltpu.VMEM(s, F32) for s in hc]
        tot = sum(nts)
        self.scratch += [pltpu.SemaphoreType.DMA((tot,)), pltpu.SemaphoreType.DMA((tot,)),
                         pltpu.SemaphoreType.DMA((tot,)), pltpu.SemaphoreType.DMA((3 * n,)),
                         pltpu.SemaphoreType.DMA((n,)), pltpu.SemaphoreType.DMA((n,)), pltpu.SemaphoreType.DMA((n,)),
                         pltpu.SemaphoreType.DMA((tot,))]
        self.phases = [self.start, self.exchange, self.combine, self.finish]

    def bind(self, ins, outs, scratch):
        n = self.n
        self.g, self.out = ins, outs
        self.va, self.recv_a, self.own = scratch[:n], scratch[n:2 * n], scratch[2 * n:3 * n]
        self.tsend, self.recv_b, self.fin = scratch[3 * n:4 * n], scratch[4 * n:5 * n], scratch[5 * n:6 * n]
        self.sa, self.ra, self.sb, self.rb, self.sc, self.rc, self.lc, self.la = scratch[6 * n:]
        self.x, self.y, self.c = lax.axis_index("x"), lax.axis_index("y"), lax.axis_index("c")
        self.mychip = 2 * self.x + self.y
        self.sibling = (self.x, self.y, 1 - self.c)

    def _copy_a(self, p, t):
        k = self.base[p] + t
        return pltpu.make_async_remote_copy(
            src_ref=self.g[p].at[t, 1 - self.c], dst_ref=self.recv_a[p].at[t], send_sem=self.sa.at[k],
            recv_sem=self.ra.at[k], device_id=self.sibling, device_id_type=MESH)

    def _fetch(self, p, t):
        return pltpu.make_async_copy(self.g[p].at[t, self.c], self.va[p].at[t], self.la.at[self.base[p] + t])

    def _slot(self, owner):
        rel = jnp.bitwise_xor(self.mychip, owner)
        return jnp.where(rel == 2, 0, jnp.where(rel == 1, 1, 2))

    def _copy_b(self, p, t, slot):
        owner = self.owners[p][t]
        return pltpu.make_async_remote_copy(
            src_ref=self.tsend[p].at[t], dst_ref=self.recv_b[p].at[slot], send_sem=self.sb.at[self.base[p] + t],
            recv_sem=self.rb.at[3 * p + slot], device_id=(owner // 2, owner % 2, self.c), device_id_type=MESH)

    def _copy_c(self, p, half):
        return pltpu.make_async_remote_copy(
            src_ref=self.fin[p], dst_ref=self.out[p].at[half], send_sem=self.sc.at[p], recv_sem=self.rc.at[p],
            device_id=self.sibling, device_id_type=MESH)

    def _local_c(self, p):
        return pltpu.make_async_copy(self.fin[p], self.out[p].at[self.c], self.lc.at[p])

    def start(self):
        for p in range(self.n):
            for t in range(len(self.owners[p])):
                self._copy_a(p, t).start()
                self._fetch(p, t).start()

    def exchange(self):
        for p in range(self.n):
            for t, owner in enumerate(self.owners[p]):
                self._copy_a(p, t).wait_recv()
                self._fetch(p, t).wait()
                mine = self.mychip == owner

                @pl.when(mine)
                def _():
                    self.own[p][...] = self.va[p][t] + self.recv_a[p][t]

                @pl.when(jnp.logical_not(mine))
                def _():
                    self.tsend[p][t] = (self.va[p][t] + self.recv_a[p][t]).astype(BF16)
                    self._copy_b(p, t, self._slot(owner)).start()

    def combine(self):
        for p in range(self.n):
            for t, owner in enumerate(self.owners[p]):
                @pl.when(self.mychip == owner)
                def _():
                    acc = self.own[p][...]
                    for j in range(3):
                        self._copy_b(p, t, j).wait_recv()
                        acc = acc + self.recv_b[p][j].astype(F32)
                    self.fin[p][...] = acc
                    self._local_c(p).start()
                    self._copy_c(p, self.c).start()

    def finish(self):
        for p in range(self.n):
            for t, owner in enumerate(self.owners[p]):
                self._copy_a(p, t).wait_send()
                mine = self.mychip == owner

                @pl.when(mine)
                def _():
                    self._copy_c(p, 1 - self.c).wait_recv()
                    self._copy_c(p, self.c).wait_send()
                    self._local_c(p).wait()

                @pl.when(jnp.logical_not(mine))
                def _():
                    self._copy_b(p, t, self._slot(owner)).wait_send()


def _split_fused(refs, n_in, n_out, n_scr, fused):
    if fused is None:
        return refs[:n_in], refs[n_in:n_in + n_out], refs[n_in + n_out:]
    fi, fo = len(fused.in_specs), len(fused.out_specs)
    ins, rest = refs[:n_in], refs[n_in:]
    f_ins, rest = rest[:fi], rest[fi:]
    outs, rest = rest[:n_out], rest[n_out:]
    f_outs, rest = rest[:fo], rest[fo:]
    scr, f_scr = rest[:n_scr], rest[n_scr:]
    fused.bind(f_ins, f_outs, f_scr)
    return ins, outs, scr


def _run_phases(fused, step, at_steps, before):
    if fused is None:
        return
    for phase, (at, first) in zip(fused.phases, at_steps):
        if first == before:
            pl.when(step == at)(phase)


def _front(c_pad, c_ctx_rows, w_mod, b_cols, w_in_t_loc):
    ncol = w_mod.shape[1]
    gather = _ChipGather([w_in_t_loc])

    def body(c_ref, cctx_ref, w_ref, b_ref, win_ref, sc_ref, modg_ref, wing_ref,
             call_ref, ag_send, ag_recv, ag_local, m_send, m_recv, *g_scr):
        gather.bind([win_ref], [wing_ref], g_scr)
        _gather8_in_vmem(c_ref, call_ref, ag_send, ag_recv, ag_local)
        gather.start()
        x, y, c = lax.axis_index("x"), lax.axis_index("y"), lax.axis_index("c")
        chips = [(1 - x, y), (x, 1 - y), (1 - x, 1 - y)]
        mychip = 2 * x + y
        rows = jnp.concatenate([call_ref[dv] for dv in range(8)] + [cctx_ref[...]], axis=0)
        sc = rows * _sigmoid(rows)
        sc_ref[...] = sc
        modg_ref[mychip] = jnp.dot(sc, w_ref[...], preferred_element_type=F32,
                                   precision=lax.Precision.HIGHEST) + b_ref[...]

        def mcopy(j, chip_idx, to):
            return pltpu.make_async_remote_copy(
                src_ref=modg_ref.at[chip_idx], dst_ref=modg_ref.at[chip_idx], send_sem=m_send.at[j],
                recv_sem=m_recv.at[j], device_id=to, device_id_type=MESH)

        sends = [mcopy(j, mychip, (*chip, c)) for j, chip in enumerate(chips)]
        for cp in sends:
            cp.start()
        for j, (cx, cy) in enumerate(chips):
            mcopy(j, 2 * cx + cy, (x, y, c)).wait_recv()
        gather.forward()
        gather.finish()
        for cp in sends:
            cp.wait_send()

    vm = pl.BlockSpec(memory_space=pltpu.VMEM)
    return pl.pallas_call(
        body, name="front_exchange",
        out_shape=[jax.ShapeDtypeStruct((80, D_MODEL), F32), jax.ShapeDtypeStruct((4, 80, ncol), F32)] + gather.out_shape,
        in_specs=[vm, vm, vm, vm] + gather.in_specs, out_specs=[vm, vm] + gather.out_specs,
        scratch_shapes=[pltpu.VMEM((8, 8, D_MODEL), F32), pltpu.SemaphoreType.DMA((7,)), pltpu.SemaphoreType.DMA((7,)),
                        pltpu.SemaphoreType.DMA, pltpu.SemaphoreType.DMA((3,)), pltpu.SemaphoreType.DMA((3,))]
        + gather.scratch,
        compiler_params=_params(),
    )(c_pad, c_ctx_rows, w_mod, b_cols, w_in_t_loc)


def _tail_exchange(packed, sc_rows, w_mod, bsz, ctx_row):
    d = D_MODEL
    ncol = w_mod.shape[1]

    def body(p_ref, sc_ref, w_ref, total_ref, gw_ref, gcc_ref, gat_ref, dm_ref, part_ref,
             ag_send, ag_recv, ag_local, g_send, g_recv):
        _gather8_in_vmem(p_ref, gat_ref, ag_send, ag_recv, ag_local)
        acc = gat_ref[0]
        for dv in range(1, 8):
            acc = acc + gat_ref[dv]
        total_ref[...] = acc
        x, y, c = lax.axis_index("x"), lax.axis_index("y"), lax.axis_index("c")
        chips = [(1 - x, y), (x, 1 - y), (1 - x, 1 - y)]
        mychip = 2 * x + y
        dm_ref[...] = jnp.zeros(dm_ref.shape, F32)
        for k in range(4):
            @pl.when(mychip == k)
            def _():
                spans = [(seg, max(k * ncol, seg * d) - seg * d, min((k + 1) * ncol, (seg + 1) * d) - seg * d)
                         for seg in range(3) if k * ncol < (seg + 1) * d and (k + 1) * ncol > seg * d]
                for dv in range(8):
                    for b in range(bsz):
                        dm_ref[8 * dv + b:8 * dv + b + 1, :] = jnp.concatenate(
                            [gat_ref[dv, 3 * b + seg:3 * b + seg + 1, lo:hi] for seg, lo, hi in spans], axis=1)
                dm_ref[64:65, :] = jnp.concatenate(
                    [total_ref[ctx_row + seg:ctx_row + seg + 1, lo:hi] if seg < 2 else jnp.zeros((1, hi - lo), F32)
                     for seg, lo, hi in spans], axis=1)

        dm = dm_ref[...]
        gw_ref[...] = lax.dot_general(sc_ref[...], dm, TN, preferred_element_type=F32,
                                      precision=lax.Precision.HIGHEST)
        part_ref[mychip] = lax.dot_general(dm[64:72, :], w_ref[...], NT, preferred_element_type=F32,
                                           precision=lax.Precision.HIGHEST)

        def gcopy(j, chip_idx, to):
            return pltpu.make_async_remote_copy(
                src_ref=part_ref.at[chip_idx], dst_ref=part_ref.at[chip_idx], send_sem=g_send.at[j],
                recv_sem=g_recv.at[j], device_id=to, device_id_type=MESH)

        sends = [gcopy(j, mychip, (*chip, c)) for j, chip in enumerate(chips)]
        for cp in sends:
            cp.start()
        for j, (cx, cy) in enumerate(chips):
            gcopy(j, 2 * cx + cy, (x, y, c)).wait_recv()
        for cp in sends:
            cp.wait_send()
        gcc_ref[...] = (part_ref[0] + part_ref[1]) + (part_ref[2] + part_ref[3])

    return pl.pallas_call(
        body, name="tail_exchange",
        out_shape=[jax.ShapeDtypeStruct(packed.shape, F32), jax.ShapeDtypeStruct((d, ncol), F32),
                   jax.ShapeDtypeStruct((8, d), F32)],
        scratch_shapes=[pltpu.VMEM((8,) + packed.shape, F32), pltpu.VMEM((80, ncol), F32), pltpu.VMEM((4, 8, d), F32),
                        pltpu.SemaphoreType.DMA((7,)), pltpu.SemaphoreType.DMA((7,)), pltpu.SemaphoreType.DMA,
                        pltpu.SemaphoreType.DMA((3,)), pltpu.SemaphoreType.DMA((3,))],
        compiler_params=_params(),
    )(packed, sc_rows, w_mod)


def _bcast_spec(arr):
    if arr.shape[0] == 1:
        return pl.BlockSpec((1, 1, arr.shape[2]), lambda b, i: (0, 0, 0))
    return pl.BlockSpec((1, 1, arr.shape[2]), lambda b, i: (b, 0, 0))


def _norm_inproj(x, shift, scale1p, norm_w, w_t, splits, tm, name):
    bsz, s, d = x.shape

    def body(x_ref, sh_ref, sc_ref, nw_ref, w_ref, u_ref, *out_refs):
        xv = x_ref[0]
        rstd = lax.rsqrt(jnp.mean(xv * xv, axis=-1, keepdims=True) + EPS)
        u = (xv * rstd * nw_ref[...]) * sc_ref[0] + sh_ref[0]
        ub = u.astype(BF16)
        u_ref[0] = ub
        for (lo, hi), o_ref in zip(splits, out_refs):
            o_ref[0] = lax.dot_general(ub, w_ref[lo:hi, :], NT, preferred_element_type=F32)

    tok = lambda w: pl.BlockSpec((1, tm, w), lambda b, i: (b, i, 0))
    return pl.pallas_call(
        body, name=name, grid=(bsz, s // tm),
        in_specs=[tok(d), _bcast_spec(shift), _bcast_spec(scale1p), pl.BlockSpec((1, d), lambda b, i: (0, 0)),
                  pl.BlockSpec(w_t.shape, lambda b, i: (0, 0))],
        out_specs=[tok(d)] + [tok(hi - lo) for lo, hi in splits],
        out_shape=[jax.ShapeDtypeStruct((bsz, s, d), BF16)]
        + [jax.ShapeDtypeStruct((bsz, s, hi - lo), F32) for lo, hi in splits],
        compiler_params=_params(2),
    )(x, shift, scale1p, norm_w, w_t)


def _dup_heads(kv, lo_mask):
    r = pltpu.roll(kv, HEAD_DIM, 1)
    return jnp.where(lo_mask, kv, r), jnp.where(lo_mask, r, kv)


def _qkv_prep(qkv, cos, sin, qnw, knw, bd512, bd128, ts, row0):
    bsz, s, _ = qkv.shape

    def body(p_ref, cos_ref, sin_ref, qnw_ref, knw_ref, bd512_ref, bd128_ref, q_ref, k_ref, v_ref, kt_ref, vt_ref):
        lo_mask = _lo_mask(ts)
        cos_t, sin_t = cos_ref[...], sin_ref[...]
        qp = p_ref[0, :, 0:512]
        qn = qp * lax.rsqrt(_seg_mean(qp * qp, bd512_ref[...]) + EPS) * qnw_ref[...]
        qr = qn * _tile_lanes(cos_t, 4) + _partner(qn) * _tile_lanes(sin_t, 4)
        q_ref[0] = (qr * (1.0 / math.sqrt(HEAD_DIM))).astype(BF16)
        kp = p_ref[0, :, 512:640]
        kn = kp * lax.rsqrt(_seg_mean(kp * kp, bd128_ref[...]) + EPS) * knw_ref[...]
        kr = kn * cos_t + _partner(kn) * sin_t
        k0, k1 = _dup_heads(kr, lo_mask)
        k_ref[0, 0] = k0.astype(BF16)
        k_ref[0, 1] = k1.astype(BF16)
        vp = p_ref[0, :, 640:768]
        v0, v1 = _dup_heads(vp, lo_mask)
        v_ref[0, 0] = v0.astype(BF16)
        v_ref[0, 1] = v1.astype(BF16)
        kt_ref[0] = kr.T.astype(BF16)
        vt_ref[0] = vp.T.astype(BF16)

    const = lambda a: pl.BlockSpec(a.shape, lambda b, i: (0,) * a.ndim)
    kv_spec = pl.BlockSpec((1, 2, ts, 128), lambda b, i: (b, 0, i + row0 // ts, 0))
    t_spec = pl.BlockSpec((1, 128, ts), lambda b, i: (b, 0, i + row0 // ts))
    return pl.pallas_call(
        body, name="qkv_prep", grid=(bsz, s // ts),
        in_specs=[pl.BlockSpec((1, ts, 768), lambda b, i: (b, i, 0)),
                  pl.BlockSpec((ts, 128), lambda b, i: (i, 0)), pl.BlockSpec((ts, 128), lambda b, i: (i, 0)),
                  const(qnw), const(knw), const(bd512), const(bd128)],
        out_specs=[pl.BlockSpec((1, ts, 512), lambda b, i: (b, i, 0)), kv_spec, kv_spec, t_spec, t_spec],
        out_shape=[jax.ShapeDtypeStruct((bsz, s, 512), BF16), jax.ShapeDtypeStruct((bsz, 2, row0 + s, 128), BF16),
                   jax.ShapeDtypeStruct((bsz, 2, row0 + s, 128), BF16),
                   jax.ShapeDtypeStruct((bsz, 128, row0 + s), BF16), jax.ShapeDtypeStruct((bsz, 128, row0 + s), BF16)],
        compiler_params=_params(2),
    )(qkv, cos, sin, qnw, knw, bd512, bd128)


def _ctx_kv_prep(pc, knw, bd128, k2, v2, kt, vt):
    bsz, cl, _ = pc.shape

    def body(p_ref, knw_ref, bd128_ref, k_in, v_in, kt_in, vt_in, k_ref, v_ref, kt_ref, vt_ref):
        lo_mask = _lo_mask(cl)
        kp = p_ref[0, :, 0:128]
        kn = kp * lax.rsqrt(_seg_mean(kp * kp, bd128_ref[...]) + EPS) * knw_ref[...]
        k0, k1 = _dup_heads(kn, lo_mask)
        k_ref[0, 0] = k0.astype(BF16)
        k_ref[0, 1] = k1.astype(BF16)
        vp = p_ref[0, :, 128:256]
        v0, v1 = _dup_heads(vp, lo_mask)
        v_ref[0, 0] = v0.astype(BF16)
        v_ref[0, 1] = v1.astype(BF16)
        kt_ref[0] = kn.T.astype(BF16)
        vt_ref[0] = vp.T.astype(BF16)

    const = lambda a: pl.BlockSpec(a.shape, lambda b: (0,) * a.ndim)
    kv_spec = pl.BlockSpec((1, 2, cl, 128), lambda b: (b, 0, 0, 0))
    t_spec = pl.BlockSpec((1, 128, cl), lambda b: (b, 0, 0))
    anyspec = pl.BlockSpec(memory_space=pl.ANY)
    return pl.pallas_call(
        body, name="ctx_kv_prep", grid=(bsz,),
        in_specs=[pl.BlockSpec((1, cl, 256), lambda b: (b, 0, 0)), const(knw), const(bd128)] + [anyspec] * 4,
        out_specs=[kv_spec, kv_spec, t_spec, t_spec],
        out_shape=[jax.ShapeDtypeStruct(a.shape, BF16) for a in (k2, v2, kt, vt)],
        input_output_aliases={3: 0, 4: 1, 5: 2, 6: 3},
        compiler_params=_params(1),
    )(pc, knw, bd128, k2, v2, kt, vt)


def _attn_forward(q, k2, vt, tq, fused):
    bsz, s, _ = q.shape
    sk = k2.shape[2]
    nq = s // tq
    total = bsz * N_KV * nq
    at_steps = [(0, True), (total // 4, True), (total - 1, False)]

    def body(*refs):
        (q_ref, k_ref, vt_ref), (o_ref, lse_ref), _ = _split_fused(refs, 3, 2, 0, fused)
        g = pl.program_id(1)
        step = (pl.program_id(0) * N_KV + g) * nq + pl.program_id(2)
        _run_phases(fused, step, at_steps, True)
        kk = k_ref[0, 0]
        lo_mask = _lo_mask(tq)
        vt_aug = jnp.concatenate([vt_ref[0, pl.ds(pl.multiple_of(g * HEAD_DIM, HEAD_DIM), HEAD_DIM), :],
                                  jnp.ones((16, sk), BF16)], axis=0)
        ps, ms = [], []
        for j in range(2):
            qp = q_ref[0, :, 128 * j:128 * (j + 1)]
            for half in range(2):
                sel = lo_mask if half == 0 else jnp.logical_not(lo_mask)
                qs = jnp.where(sel, qp, jnp.zeros_like(qp))
                sc = lax.dot_general(qs, kk, NT, preferred_element_type=F32)
                m = jnp.max(sc, axis=-1, keepdims=True)
                ps.append(jnp.exp(sc - m).astype(BF16))
                ms.append(m)
        ots = [lax.dot_general(vt_aug, p, NT, preferred_element_type=F32) for p in ps]
        for j in range(2):
            o_t, l_t = [], []
            for half in range(2):
                ot = ots[2 * j + half]
                l = ot[HEAD_DIM:HEAD_DIM + 1, :]
                o_t.append(ot[0:HEAD_DIM, :] / l)
                l_t.append(jnp.broadcast_to(l, (HEAD_DIM, tq)))
            o_ref[0, :, 128 * j:128 * (j + 1)] = jnp.concatenate(o_t, axis=0).T
            lse_ref[0, :, 128 * j:128 * (j + 1)] = (jnp.where(lo_mask, ms[2 * j], ms[2 * j + 1])
                                                    + jnp.log(jnp.concatenate(l_t, axis=0).T))
        _run_phases(fused, step, at_steps, False)

    q_spec = pl.BlockSpec((1, tq, 256), lambda b, g, i: (b, i, g))
    kv_spec = pl.BlockSpec((1, 1, sk, 128), lambda b, g, i: (b, g, 0, 0))
    return pl.pallas_call(
        body, name="attn_forward", grid=(bsz, N_KV, nq),
        in_specs=[q_spec, kv_spec, pl.BlockSpec((1, 128, sk), lambda b, g, i: (b, 0, 0))] + fused.in_specs,
        out_specs=[q_spec, q_spec] + fused.out_specs,
        out_shape=[jax.ShapeDtypeStruct((bsz, s, 512), F32)] * 2 + fused.out_shape,
        scratch_shapes=fused.scratch,
        compiler_params=_params(3),
    )(q, k2, vt, *fused.arrs)


def _halo_specs(width, ts, s):
    r = ts // HALO
    last = s // HALO - 1
    return [pl.BlockSpec((1, ts, width), lambda b, i: (b, i, 0)),
            pl.BlockSpec((1, HALO, width), lambda b, i: (b, jnp.maximum(i * r - 1, 0), 0)),
            pl.BlockSpec((1, HALO, width), lambda b, i: (b, jnp.minimum((i + 1) * r, last), 0))]


def _fill_ext(ext_ref, cur, prev, nxt, i, n_tiles, ts):
    ext_ref[0:HALO, :] = jnp.where(i > 0, prev, jnp.zeros_like(prev))
    ext_ref[HALO:HALO + ts, :] = cur
    ext_ref[HALO + ts:2 * HALO + ts, :] = jnp.where(i < n_tiles - 1, nxt, jnp.zeros_like(nxt))


def _fill_shifted(sh_ref, ext_ref, ts):
    n = ts + 2 * HALO - 8
    for r in range(1, 8):
        sh_ref[r - 1, 0:n, :] = ext_ref[pl.ds(r, n), :]


def _window(sh_ref, ext_ref, off, rows, r0=0):
    q, r = divmod(off, 8)
    if r == 0:
        return ext_ref[pl.ds(r0 + off, rows), :]
    return sh_ref[r - 1, pl.ds(r0 + 8 * q, rows), :]


def _conv_forward(ga, gg, conv_w, conv_b, ln_w, ln_b, w_pw, b_pw, ts):
    bsz, s, dc = ga.shape
    n_tiles = s // ts

    def body(a_ref, ap_ref, an_ref, g_ref, gp_ref, gn_ref, cw_ref, cb_ref, lw_ref, lb_ref, wp_ref, bp_ref,
             y_ref, cv_ref, ext_ref, sh_ref):
        i = pl.program_id(1)
        glu = lambda a, g: a * _sigmoid(g)
        _fill_ext(ext_ref, glu(a_ref[0], g_ref[0]), glu(ap_ref[0], gp_ref[0]), glu(an_ref[0], gn_ref[0]), i, n_tiles, ts)
        _fill_shifted(sh_ref, ext_ref, ts)
        acc = jnp.broadcast_to(cb_ref[...], (ts, dc))
        for j in range(CONV_WIDTH):
            acc = acc + cw_ref[j:j + 1, :] * _window(sh_ref, ext_ref, HALO - CONV_PAD + j, ts)
        y_ref[0] = acc
        mu = jnp.mean(acc, axis=-1, keepdims=True)
        yc = acc - mu
        var = jnp.mean(yc * yc, axis=-1, keepdims=True)
        yn = yc * lax.rsqrt(var + EPS) * lw_ref[...] + lb_ref[...]
        ys = yn * _sigmoid(yn)
        cv_ref[0] = jnp.dot(ys.astype(BF16), wp_ref[...], preferred_element_type=F32) + bp_ref[...]

    const = lambda a: pl.BlockSpec(a.shape, lambda b, i: (0,) * a.ndim)
    return pl.pallas_call(
        body, name="conv_forward", grid=(bsz, n_tiles),
        in_specs=_halo_specs(dc, ts, s) + _halo_specs(dc, ts, s)
        + [const(conv_w), const(conv_b), const(ln_w), const(ln_b), const(w_pw), const(b_pw)],
        out_specs=[pl.BlockSpec((1, ts, dc), lambda b, i: (b, i, 0))] * 2,
        out_shape=[jax.ShapeDtypeStruct((bsz, s, dc), F32)] * 2,
        scratch_shapes=[pltpu.VMEM((ts + 2 * HALO, dc), F32), pltpu.VMEM((7, ts + 2 * HALO, dc), F32)],
        compiler_params=_params(2),
    )(ga, ga, ga, gg, gg, gg, conv_w, conv_b, ln_w, ln_b, w_pw, b_pw)


def _outproj_loss(x, target, gate, o, za, cv, zc, w_out, tm):
    bsz, s, d = x.shape

    def body(x_ref, t_ref, gate_ref, o_ref, za_ref, cv_ref, zc_ref, w_ref,
             loss_ref, dh_ref, do_ref, dza_ref, dcv_ref, dzc_ref, dgate_ref, gw_ref):
        b, i = pl.program_id(0), pl.program_id(1)
        ov, cvv = o_ref[0], cv_ref[0]
        silu_a, dsilu_a = _silu_and_grad(za_ref[0])
        silu_c, dsilu_c = _silu_and_grad(zc_ref[0])
        mix = jnp.concatenate([ov * silu_a, cvv * silu_c], axis=1).astype(BF16)
        out = jnp.dot(mix, w_ref[...], preferred_element_type=F32)
        gate_v = gate_ref[0]
        err = x_ref[0] + gate_v * out - t_ref[0]
        dh = err * (1.0 / d)
        dh_ref[0] = dh
        dout = (dh * gate_v).astype(BF16)
        dmix = lax.dot_general(dout, w_ref[...], NT, preferred_element_type=F32)
        gw = lax.dot_general(mix, dout, TN, preferred_element_type=F32)
        dg = jnp.sum(dh * out, axis=0, keepdims=True)
        sq = jnp.sum(err * err)

        @pl.when(jnp.logical_and(b == 0, i == 0))
        def _():
            gw_ref[...] = gw

        @pl.when(jnp.logical_or(b > 0, i > 0))
        def _():
            gw_ref[...] += gw

        @pl.when(i == 0)
        def _():
            dgate_ref[0] = dg
            loss_ref[...] = jnp.zeros(loss_ref.shape, F32) + sq

        @pl.when(i > 0)
        def _():
            dgate_ref[0] += dg
            loss_ref[...] += sq

        dma, dmc = dmix[:, :D_ATTN], dmix[:, D_ATTN:]
        do_ref[0] = dma * silu_a
        dza_ref[0] = (dma * ov * dsilu_a).astype(BF16)
        dcv_ref[0] = dmc * silu_c
        dzc_ref[0] = (dmc * cvv * dsilu_c).astype(BF16)

    tok = lambda w: pl.BlockSpec((1, tm, w), lambda b, i: (b, i, 0))
    return pl.pallas_call(
        body, name="outproj_loss", grid=(bsz, s // tm),
        in_specs=[tok(d), tok(d), _bcast_spec(gate), tok(512), tok(512), tok(512), tok(512),
                  pl.BlockSpec(w_out.shape, lambda b, i: (0, 0))],
        out_specs=[pl.BlockSpec((1, 8, 128), lambda b, i: (b, 0, 0)), tok(d), tok(512), tok(512), tok(512), tok(512),
                   pl.BlockSpec((1, 1, d), lambda b, i: (b, 0, 0)), pl.BlockSpec((d, d), lambda b, i: (0, 0))],
        out_shape=[jax.ShapeDtypeStruct((bsz, 8, 128), F32), jax.ShapeDtypeStruct((bsz, s, d), F32),
                   jax.ShapeDtypeStruct((bsz, s, 512), F32), jax.ShapeDtypeStruct((bsz, s, 512), BF16),
                   jax.ShapeDtypeStruct((bsz, s, 512), F32), jax.ShapeDtypeStruct((bsz, s, 512), BF16),
                   jax.ShapeDtypeStruct((bsz, 1, d), F32), jax.ShapeDtypeStruct((d, d), F32)],
        compiler_params=_params(2),
    )(x, target, gate, o, za, cv, zc, w_out)


def _conv_token_backward(dcv, y, ln_w, ln_b, w_pw, tm):
    bsz, s, dc = dcv.shape

    def body(dcv_ref, y_ref, lw_ref, lb_ref, wp_ref, dy_ref, gwp_ref, st_ref):
        b, i = pl.program_id(0), pl.program_id(1)
        yv, dcvv = y_ref[0], dcv_ref[0]
        mu = jnp.mean(yv, axis=-1, keepdims=True)
        yc = yv - mu
        rstd = lax.rsqrt(jnp.mean(yc * yc, axis=-1, keepdims=True) + EPS)
        yhat = yc * rstd
        yn = yhat * lw_ref[...] + lb_ref[...]
        ys, dsilu = _silu_and_grad(yn)
        dcvb = dcvv.astype(BF16)
        gwp = lax.dot_general(ys.astype(BF16), dcvb, TN, preferred_element_type=F32)
        dys = lax.dot_general(dcvb, wp_ref[...], NT, preferred_element_type=F32)
        dyn = dys * dsilu
        dyhat = dyn * lw_ref[...]
        dy = rstd * (dyhat - jnp.mean(dyhat, axis=-1, keepdims=True)
                     - yhat * jnp.mean(dyhat * yhat, axis=-1, keepdims=True))
        dy_ref[0] = dy
        red = lambda v: jnp.sum(v, axis=0, keepdims=True)
        stats = jnp.concatenate([red(dcvv), red(dyn * yhat), red(dyn), red(dy), jnp.zeros((4, dc), F32)], axis=0)
        first = jnp.logical_and(b == 0, i == 0)

        @pl.when(first)
        def _():
            gwp_ref[...] = gwp
            st_ref[...] = stats

        @pl.when(jnp.logical_not(first))
        def _():
            gwp_ref[...] += gwp
            st_ref[...] += stats

    tok = pl.BlockSpec((1, tm, dc), lambda b, i: (b, i, 0))
    const = lambda a: pl.BlockSpec(a.shape, lambda b, i: (0,) * a.ndim)
    return pl.pallas_call(
        body, name="conv_token_backward", grid=(bsz, s // tm),
        in_specs=[tok, tok, const(ln_w), const(ln_b), const(w_pw)],
        out_specs=[tok, pl.BlockSpec((dc, dc), lambda b, i: (0, 0)), pl.BlockSpec((8, dc), lambda b, i: (0, 0))],
        out_shape=[jax.ShapeDtypeStruct((bsz, s, dc), F32), jax.ShapeDtypeStruct((dc, dc), F32),
                   jax.ShapeDtypeStruct((8, dc), F32)],
        compiler_params=_params(2),
    )(dcv, y, ln_w, ln_b, w_pw)


def _conv_backward(dy, ga, gg, conv_w, ts):
    bsz, s, dc = dy.shape
    n_tiles = s // ts

    def body(dy_ref, dyp_ref, dyn_ref, a_ref, g_ref, cw_ref,
             da_ref, dg_ref, gcw_ref, dyext_ref, dysh_ref, ug_ref, dug_ref, gacc_ref):
        b, i = pl.program_id(0), pl.program_id(1)
        av, sg = a_ref[0], _sigmoid(g_ref[0])
        ug_ref[...] = av * sg
        _fill_ext(dyext_ref, dy_ref[0], dyp_ref[0], dyn_ref[0], i, n_tiles, ts)
        _fill_shifted(dysh_ref, dyext_ref, ts)
        gacc_ref[...] = jnp.zeros(gacc_ref.shape, F32)

        def row_block(r, carry):
            r0 = pl.multiple_of(r * CONV_ROWS, CONV_ROWS)
            ugb = ug_ref[pl.ds(r0, CONV_ROWS), :]
            acc = jnp.zeros((CONV_ROWS, dc), F32)
            for j in range(CONV_WIDTH):
                win = _window(dysh_ref, dyext_ref, HALO + CONV_PAD - j, CONV_ROWS, r0)
                acc = acc + cw_ref[j:j + 1, :] * win
                prod = ugb * win
                part = prod[0:8, :]
                for k in range(8, CONV_ROWS, 8):
                    part = part + prod[k:k + 8, :]
                gacc_ref[j] += part
            dug_ref[pl.ds(r0, CONV_ROWS), :] = acc
            return carry

        lax.fori_loop(0, ts // CONV_ROWS, row_block, 0)
        dug = dug_ref[...]
        gcw = jnp.sum(gacc_ref[...], axis=1)
        first = jnp.logical_and(b == 0, i == 0)

        @pl.when(first)
        def _():
            gcw_ref[...] = gcw

        @pl.when(jnp.logical_not(first))
        def _():
            gcw_ref[...] += gcw

        da_ref[0] = (dug * sg).astype(BF16)
        dg_ref[0] = (dug * av * sg * (1.0 - sg)).astype(BF16)

    tok = pl.BlockSpec((1, ts, dc), lambda b, i: (b, i, 0))
    return pl.pallas_call(
        body, name="conv_backward", grid=(bsz, n_tiles),
        in_specs=_halo_specs(dc, ts, s) + [tok, tok, pl.BlockSpec(conv_w.shape, lambda b, i: (0, 0))],
        out_specs=[tok, tok, pl.BlockSpec((32, dc), lambda b, i: (0, 0))],
        out_shape=[jax.ShapeDtypeStruct((bsz, s, dc), BF16), jax.ShapeDtypeStruct((bsz, s, dc), BF16),
                   jax.ShapeDtypeStruct((32, dc), F32)],
        scratch_shapes=[pltpu.VMEM((ts + 2 * HALO, dc), F32), pltpu.VMEM((7, ts + 2 * HALO, dc), F32),
                        pltpu.VMEM((ts, dc), F32), pltpu.VMEM((ts, dc), F32), pltpu.VMEM((32, 8, dc), F32)],
        compiler_params=_params(2),
    )(dy, dy, dy, ga, gg, conv_w)


def _attn_backward(q, k2, v2, kt, o, do, lse, tq, fused):
    bsz, s, _ = q.shape
    sk = k2.shape[2]
    scale = 1.0 / math.sqrt(HEAD_DIM)
    nq = s // tq
    total = bsz * N_KV * nq
    at_steps = [(0, True), (total // 5, True), (total // 2, True), (total - 1, False)]

    def body(*refs):
        (q_ref, k_ref, v_ref, kt_ref, o_ref, do_ref, lse_ref), (dq_ref, dk_ref, dv_ref), _ = _split_fused(
            refs, 7, 3, 0, fused)
        g, i = pl.program_id(1), pl.program_id(2)
        step = (pl.program_id(0) * N_KV + g) * nq + i
        _run_phases(fused, step, at_steps, True)
        kk, vv = k_ref[0, 0], v_ref[0, 0]
        kgt = kt_ref[0, pl.ds(pl.multiple_of(g * HEAD_DIM, HEAD_DIM), HEAD_DIM), :]
        lo_mask = _lo_mask(tq)
        dk_acc = jnp.zeros((HEAD_DIM, sk), F32)
        dv_acc = jnp.zeros((HEAD_DIM, sk), F32)
        for j in range(2):
            cols = slice(128 * j, 128 * (j + 1))
            qp, dop, lsep = q_ref[0, :, cols], do_ref[0, :, cols], lse_ref[0, :, cols]
            dprod = dop * o_ref[0, :, cols]
            q_t = qp.astype(F32).T.astype(BF16)
            do_t = dop.T.astype(BF16)
            dq_t = []
            for half in range(2):
                sel = lo_mask if half == 0 else jnp.logical_not(lo_mask)
                rows = slice(HEAD_DIM * half, HEAD_DIM * (half + 1))
                qs = jnp.where(sel, qp, jnp.zeros_like(qp))
                dos = jnp.where(sel, dop, 0.0).astype(BF16)
                lse_h = jnp.max(jnp.where(sel, lsep, -jnp.inf), axis=-1, keepdims=True)
                delta = jnp.sum(jnp.where(sel, dprod, 0.0), axis=-1, keepdims=True)
                sc = lax.dot_general(qs, kk, NT, preferred_element_type=F32)
                p = jnp.exp(sc - lse_h)
                dp = lax.dot_general(dos, vv, NT, preferred_element_type=F32)
                ds = (p * (dp - delta)).astype(BF16)
                dv_acc = dv_acc + jnp.dot(do_t[rows, :], p.astype(BF16), preferred_element_type=F32)
                dk_acc = dk_acc + jnp.dot(q_t[rows, :], ds, preferred_element_type=F32)
                dq_t.append(lax.dot_general(kgt, ds, NT, preferred_element_type=F32))
            dq_ref[0, :, cols] = (jnp.concatenate(dq_t, axis=0) * scale).T

        @pl.when(i == 0)
        def _():
            dk_ref[0, 0] = dk_acc
            dv_ref[0, 0] = dv_acc

        @pl.when(i > 0)
        def _():
            dk_ref[0, 0] += dk_acc
            dv_ref[0, 0] += dv_acc

        _run_phases(fused, step, at_steps, False)

    q_spec = pl.BlockSpec((1, tq, 256), lambda b, g, i: (b, i, g))
    kv_spec = pl.BlockSpec((1, 1, sk, 128), lambda b, g, i: (b, g, 0, 0))
    acc_spec = pl.BlockSpec((1, 1, HEAD_DIM, sk), lambda b, g, i: (b, g, 0, 0))
    return pl.pallas_call(
        body, name="attn_backward", grid=(bsz, N_KV, nq),
        in_specs=[q_spec, kv_spec, kv_spec, pl.BlockSpec((1, 128, sk), lambda b, g, i: (b, 0, 0)), q_spec, q_spec,
                  q_spec] + fused.in_specs,
        out_specs=[q_spec, acc_spec, acc_spec] + fused.out_specs,
        out_shape=[jax.ShapeDtypeStruct((bsz, s, 512), F32), jax.ShapeDtypeStruct((bsz, 2, HEAD_DIM, sk), F32),
                   jax.ShapeDtypeStruct((bsz, 2, HEAD_DIM, sk), F32)] + fused.out_shape,
        scratch_shapes=fused.scratch,
        compiler_params=_params(3),
    )(q, k2, v2, kt, o, do, lse, *fused.arrs)


def _heads_to_lanes(acc_ref):
    return jnp.concatenate([acc_ref[0, 0], acc_ref[0, 1]], axis=0).T


def _norm_backward(dn, pre, w, bd):
    rstd = lax.rsqrt(_seg_mean(pre * pre, bd) + EPS)
    xhat = pre * rstd
    dxhat = dn * w
    return rstd * (dxhat - xhat * _seg_mean(dxhat * xhat, bd)), dn * xhat


def _qkv_backward(qkv, dq, dk2, dv2, cos, sin, qnw, knw, bd512, bd128, ts, row0):
    bsz, s, _ = qkv.shape

    def body(p_ref, dq_ref, dk_ref, dv_ref, cos_ref, sin_ref, qnw_ref, knw_ref, bd512_ref, bd128_ref, d_ref, gw_ref):
        b, i = pl.program_id(0), pl.program_id(1)
        lo_mask = _lo_mask(ts)
        cos_t, sin_t = cos_ref[...], sin_ref[...]
        dqr = dq_ref[0]
        dqn = dqr * _tile_lanes(cos_t, 4) + _partner(dqr * _tile_lanes(sin_t, 4))
        dqp, gq = _norm_backward(dqn, p_ref[0, :, 0:512], qnw_ref[...], bd512_ref[...])
        dkr = _heads_to_lanes(dk_ref)
        dkn = dkr * cos_t + _partner(dkr * sin_t)
        dkp, gk = _norm_backward(dkn, p_ref[0, :, 512:640], knw_ref[...], bd128_ref[...])
        dvp = _heads_to_lanes(dv_ref)
        d_ref[0] = jnp.concatenate([dqp, dkp, dvp], axis=1).astype(BF16)
        gk512 = jnp.concatenate([jnp.sum(gk, axis=0, keepdims=True), jnp.zeros((1, 384), F32)], axis=1)
        rows = jnp.concatenate([jnp.sum(gq, axis=0, keepdims=True), gk512, jnp.zeros((6, 512), F32)], axis=0)
        first = jnp.logical_and(b == 0, i == 0)

        @pl.when(first)
        def _():
            gw_ref[...] = rows

        @pl.when(jnp.logical_not(first))
        def _():
            gw_ref[...] += rows

    const = lambda a: pl.BlockSpec(a.shape, lambda b, i: (0,) * a.ndim)
    kv_spec = pl.BlockSpec((1, 2, HEAD_DIM, ts), lambda b, i: (b, 0, 0, i + row0 // ts))
    return pl.pallas_call(
        body, name="qkv_backward", grid=(bsz, s // ts),
        in_specs=[pl.BlockSpec((1, ts, 768), lambda b, i: (b, i, 0)), pl.BlockSpec((1, ts, 512), lambda b, i: (b, i, 0)),
                  kv_spec, kv_spec, pl.BlockSpec((ts, 128), lambda b, i: (i, 0)),
                  pl.BlockSpec((ts, 128), lambda b, i: (i, 0)), const(qnw), const(knw), const(bd512), const(bd128)],
        out_specs=[pl.BlockSpec((1, ts, 768), lambda b, i: (b, i, 0)), pl.BlockSpec((8, 512), lambda b, i: (0, 0))],
        out_shape=[jax.ShapeDtypeStruct((bsz, s, 768), BF16), jax.ShapeDtypeStruct((8, 512), F32)],
        compiler_params=_params(2),
    )(qkv, dq, dk2, dv2, cos, sin, qnw, knw, bd512, bd128)


def _ctx_kv_backward(pc, dk2, dv2, knw, bd128):
    bsz, cl, _ = pc.shape

    def body(p_ref, dk_ref, dv_ref, knw_ref, bd128_ref, d_ref, gw_ref):
        b = pl.program_id(0)
        lo_mask = _lo_mask(cl)
        dkn = _heads_to_lanes(dk_ref)
        dkp, gk = _norm_backward(dkn, p_ref[0, :, 0:128], knw_ref[...], bd128_ref[...])
        dvp = _heads_to_lanes(dv_ref)
        d_ref[0] = jnp.concatenate([dkp, dvp], axis=1).astype(BF16)
        rows = jnp.concatenate([jnp.sum(gk, axis=0, keepdims=True), jnp.zeros((7, 128), F32)], axis=0)

        @pl.when(b == 0)
        def _():
            gw_ref[...] = rows

        @pl.when(b > 0)
        def _():
            gw_ref[...] += rows

    const = lambda a: pl.BlockSpec(a.shape, lambda b: (0,) * a.ndim)
    kv_spec = pl.BlockSpec((1, 2, HEAD_DIM, cl), lambda b: (b, 0, 0, 0))
    return pl.pallas_call(
        body, name="ctx_kv_backward", grid=(bsz,),
        in_specs=[pl.BlockSpec((1, cl, 256), lambda b: (b, 0, 0)), kv_spec, kv_spec, const(knw), const(bd128)],
        out_specs=[pl.BlockSpec((1, cl, 256), lambda b: (b, 0, 0)), pl.BlockSpec((8, 128), lambda b: (0, 0))],
        out_shape=[jax.ShapeDtypeStruct((bsz, cl, 256), BF16), jax.ShapeDtypeStruct((8, 128), F32)],
        compiler_params=_params(1),
    )(pc, dk2, dv2, knw, bd128)


def _weight_grad(parts, u, init, tm, name):
    bsz, s, d = u.shape
    n_p = len(parts)
    nrows = sum(hi - lo for _, lo, hi in parts)

    def body(*refs):
        p_refs, u_ref = refs[:n_p], refs[n_p]
        gi_ref = refs[n_p + 1] if init is not None else None
        gw_ref = refs[-1]
        first = jnp.logical_and(pl.program_id(0) == 0, pl.program_id(1) == 0)
        dp = jnp.concatenate([r[0, :, lo:hi] for r, (_, lo, hi) in zip(p_refs, parts)], axis=1)
        gw = lax.dot_general(dp, u_ref[0], TN, preferred_element_type=F32)

        @pl.when(first)
        def _():
            gw_ref[...] = gw
            if init is not None:
                gw_ref[KV_LO:KV_HI, :] += gi_ref[...]

        @pl.when(jnp.logical_not(first))
        def _():
            gw_ref[...] += gw

    tok = lambda w: pl.BlockSpec((1, tm, w), lambda b, i: (b, i, 0))
    in_specs = [tok(a.shape[2]) for a, _, _ in parts] + [tok(d)]
    args = [a for a, _, _ in parts] + [u]
    if init is not None:
        in_specs.append(pl.BlockSpec(init.shape, lambda b, i: (0, 0)))
        args.append(init)
    return pl.pallas_call(
        body, name=name, grid=(bsz, s // tm), in_specs=in_specs,
        out_specs=pl.BlockSpec((nrows, d), lambda b, i: (0, 0)), out_shape=jax.ShapeDtypeStruct((nrows, d), F32),
        compiler_params=_params(2),
    )(*args)


def _inproj_backward(dps, x, dh, scale1p, norm_w, w_t, tm, name, fused=None):
    bsz, s, d = x.shape
    n_p = len(dps)
    shared = scale1p.shape[0] == 1
    with_dx = dh is not None
    n_in = n_p + (2 if with_dx else 1) + 3
    n_out = 3 if with_dx else 2
    total = bsz * (s // tm)
    at_steps = [(0, True), (total // 8, True), ((3 * total) // 4, True), (total - 1, False)]

    def body(*refs):
        ins, outs, _ = _split_fused(refs, n_in, n_out, 0, fused)
        dp_refs, x_ref = ins[:n_p], ins[n_p]
        dh_ref = ins[n_p + 1] if with_dx else None
        sc_ref, nw_ref, w_ref = ins[-3:]
        mod_ref, gnw_ref = outs[-2:]
        b, i = pl.program_id(0), pl.program_id(1)
        step = b * (s // tm) + i
        _run_phases(fused, step, at_steps, True)
        first = jnp.logical_and(b == 0, i == 0)
        dp = dp_refs[0][0] if n_p == 1 else jnp.concatenate([r[0] for r in dp_refs], axis=1)
        du = jnp.dot(dp, w_ref[...], preferred_element_type=F32)
        xv = x_ref[0]
        rstd = lax.rsqrt(jnp.mean(xv * xv, axis=-1, keepdims=True) + EPS)
        xhat = xv * rstd
        nw, sc = nw_ref[...], sc_ref[0]
        red = lambda v: jnp.sum(v, axis=0, keepdims=True)
        mod_rows = jnp.concatenate([red(du), red(du * (xhat * nw)), jnp.zeros((6, d), F32)], axis=0)
        gnw_rows = jnp.concatenate([red(du * sc * xhat), jnp.zeros((7, d), F32)], axis=0)
        mod_first = first if shared else i == 0

        @pl.when(mod_first)
        def _():
            mod_ref[0] = mod_rows

        @pl.when(jnp.logical_not(mod_first))
        def _():
            mod_ref[0] += mod_rows

        @pl.when(first)
        def _():
            gnw_ref[...] = gnw_rows

        @pl.when(jnp.logical_not(first))
        def _():
            gnw_ref[...] += gnw_rows

        if with_dx:
            dxhat = du * (nw * sc)
            outs[0][0] = dh_ref[0] + rstd * (dxhat - xhat * jnp.mean(dxhat * xhat, axis=-1, keepdims=True))
        _run_phases(fused, step, at_steps, False)

    tok = lambda w: pl.BlockSpec((1, tm, w), lambda b, i: (b, i, 0))
    in_specs = [tok(p.shape[2]) for p in dps] + [tok(d)]
    args = list(dps) + [x]
    if with_dx:
        in_specs.append(tok(d))
        args.append(dh)
    in_specs += [_bcast_spec(scale1p), pl.BlockSpec((1, d), lambda b, i: (0, 0)),
                 pl.BlockSpec(w_t.shape, lambda b, i: (0, 0))]
    args += [scale1p, norm_w, w_t]
    bm = scale1p.shape[0]
    mod_spec = pl.BlockSpec((1, 8, d), (lambda b, i: (0, 0, 0)) if shared else (lambda b, i: (b, 0, 0)))
    out_specs = [mod_spec, pl.BlockSpec((8, d), lambda b, i: (0, 0))]
    out_shape = [jax.ShapeDtypeStruct((bm, 8, d), F32), jax.ShapeDtypeStruct((8, d), F32)]
    if with_dx:
        out_specs.insert(0, tok(d))
        out_shape.insert(0, jax.ShapeDtypeStruct((bsz, s, d), F32))
    scratch = []
    if fused is not None:
        in_specs += fused.in_specs
        args += fused.arrs
        out_specs += fused.out_specs
        out_shape += fused.out_shape
        scratch = fused.scratch
    res = pl.pallas_call(
        body, name=name, grid=(bsz, s // tm), in_specs=in_specs, out_specs=out_specs, out_shape=out_shape,
        scratch_shapes=scratch, compiler_params=_params(2),
    )(*args)
    return list(res) if with_dx else [None] + list(res)


def _adamw_update(w_ref, g_ref, m_ref, v_ref, d_ref, nm_ref, nv_ref):
    gv = g_ref[...]
    mn = ADAM_B1 * m_ref[...] + (1.0 - ADAM_B1) * gv
    vn = ADAM_B2 * v_ref[...] + (1.0 - ADAM_B2) * (gv * gv)
    m_hat = mn / (1.0 - ADAM_B1 ** ADAM_STEP)
    v_hat = vn / (1.0 - ADAM_B2 ** ADAM_STEP)
    d_ref[...] = -ADAM_LR * (m_hat / (jnp.sqrt(v_hat) + ADAM_EPS) + ADAM_WD * w_ref[...])
    nm_ref[...] = mn
    nv_ref[...] = vn


def _adamw_small(ws, gs, ms, vs):
    n = len(ws)

    def body(*refs):
        ins, outs = refs[:4 * n], refs[4 * n:]
        for k in range(n):
            _adamw_update(ins[k], ins[n + k], ins[2 * n + k], ins[3 * n + k], outs[3 * k], outs[3 * k + 1],
                          outs[3 * k + 2])

    res = pl.pallas_call(
        body, name="adamw_small",
        out_shape=[jax.ShapeDtypeStruct(w.shape, F32) for w in ws for _ in range(3)], compiler_params=_params(),
    )(*ws, *gs, *ms, *vs)
    return [tuple(res[3 * k:3 * k + 3]) for k in range(n)]


def _adamw(w, g, m, v, name):
    r, cdim = w.shape
    tr = next((t for t in (256, 176) if r % t == 0 and r > t), r)

    def body(*refs):
        _adamw_update(*refs)

    spec = pl.BlockSpec((tr, cdim), lambda i: (i, 0))
    return pl.pallas_call(
        body, name=name, grid=(r // tr,), in_specs=[spec] * 4, out_specs=[spec] * 3,
        out_shape=[jax.ShapeDtypeStruct((r, cdim), F32)] * 3, compiler_params=_params(1),
    )(w, g, m, v)


def _rope_tables(s):
    rows = s // GRID_W
    freqs = np.float32(ROPE_THETA) ** (-np.arange(0, ROPE_AXIS_DIM, 2, dtype=np.float32) / np.float32(ROPE_AXIS_DIM))
    ang_r = np.arange(rows, dtype=np.float32)[:, None] * freqs[None, :]
    ang_c = np.arange(GRID_W, dtype=np.float32)[:, None] * freqs[None, :]
    zr, zc = np.zeros_like(ang_r), np.zeros_like(ang_c)

    def table(by_row, by_col):
        r = jnp.asarray(np.tile(np.concatenate(by_row + [zr, zr], axis=1), (1, 2)), dtype=F32)
        c = jnp.asarray(np.tile(np.concatenate([zc, zc] + by_col, axis=1), (1, 2)), dtype=F32)
        return jnp.repeat(r, GRID_W, axis=0) + jnp.tile(c, (rows, 1))

    return (table([np.cos(ang_r)] * 2, [np.cos(ang_c)] * 2),
            table([-np.sin(ang_r), np.sin(ang_r)], [-np.sin(ang_c), np.sin(ang_c)]))


def _pack_rows(parts, rows):
    flat = jnp.concatenate([p.reshape(-1) for p in parts])
    return jnp.pad(flat, (0, rows * D_MODEL - flat.shape[0])).reshape(rows, D_MODEL)


def kernel(x, c, ctx, c_ctx, w_mod, b_mod, norm_w, w_in, q_norm_w, k_norm_w, conv_w, conv_b, conv_ln_w, conv_ln_b, w_pw, b_pw, w_out, loss_target, m_c_ctx, m_w_mod, m_b_mod, m_norm_w, m_w_in, m_q_norm_w, m_k_norm_w, m_conv_w, m_conv_b, m_conv_ln_w, m_conv_ln_b, m_w_pw, m_b_pw, m_w_out, v_c_ctx, v_w_mod, v_b_mod, v_norm_w, v_w_in, v_q_norm_w, v_k_norm_w, v_conv_w, v_conv_b, v_conv_ln_w, v_conv_ln_b, v_w_pw, v_b_pw, v_w_out):
    bsz, s, d = x.shape
    cl = ctx.shape[1]
    xi, yi, ci = lax.axis_index("x"), lax.axis_index("y"), lax.axis_index("c")
    chip = 2 * xi + yi
    dev = 2 * chip + ci
    ncol_mod = w_mod.shape[2]

    w_in_t_loc = w_in[0].T.astype(BF16)
    b_cols = lax.dynamic_slice(b_mod, (0, chip * ncol_mod), (1, ncol_mod))
    sc_rows, mod_g, g_in = _front(jnp.pad(c, ((0, 8 - bsz), (0, 0))), jnp.pad(c_ctx[None, :], ((0, 15), (0, 0))),
                                  w_mod[0], b_cols, w_in_t_loc)
    w_in_t = g_in.reshape(D_IN, d)
    mod_all = mod_g.transpose(1, 0, 2).reshape(80, 3 * d)
    mod_loc = lax.dynamic_slice(mod_all, (8 * dev, 0), (bsz, 3 * d))
    shift, scale1p, gate = mod_loc[:, None, :d], 1.0 + mod_loc[:, None, d:2 * d], mod_loc[:, None, 2 * d:]
    shift_c, scale1p_c = mod_all[64:65, :d][None], 1.0 + mod_all[64:65, d:2 * d][None]

    cos, sin = _rope_tables(s)
    qnw512 = jnp.tile(q_norm_w, (1, 8))
    knw128 = jnp.tile(k_norm_w, (1, 2))
    bd512 = jnp.kron(jnp.eye(8, dtype=F32), jnp.ones((HEAD_DIM, HEAD_DIM), F32)).astype(BF16)
    bd128 = bd512[:128, :128]

    u, p_qkv, p_za, p_ga, p_gg, p_zc = _norm_inproj(x, shift, scale1p, norm_w, w_in_t, SPLITS, 512, "norm_inproj")
    uc, pc_kv = _norm_inproj(ctx, shift_c, scale1p_c, norm_w, w_in_t[KV_LO:KV_HI], ((0, 256),), cl, "ctx_norm_inproj")
    q, k2x, v2x, ktx, vtx = _qkv_prep(p_qkv, cos, sin, qnw512, knw128, bd512, bd128, 256, cl)
    k2, v2, kt, vt = _ctx_kv_prep(pc_kv, knw128, bd128, k2x, v2x, ktx, vtx)
    conv_w_loc = jnp.pad(conv_w[0], ((0, 1), (0, 0)))
    o, lse, g_out, g_pw, g_cw = _attn_forward(
        q, k2, vt, 512, _ChipGather([w_out[0].astype(BF16), w_pw[0].astype(BF16), conv_w_loc]))
    w_out_f = g_out.reshape(d, d)
    w_pw_f = g_pw.reshape(D_CONV, D_CONV)
    conv_w_f = g_cw.transpose(1, 0, 2).reshape(32, D_CONV)
    y, cv = _conv_forward(p_ga, p_gg, conv_w_f, conv_b, conv_ln_w, conv_ln_b, w_pw_f, b_pw, 256)
    loss_part, dh, do, dza, dcv, dzc, dgate, gw_out = _outproj_loss(
        x, loss_target, gate, o, p_za, cv, p_zc, w_out_f, 512)

    all_chips, half_rows = (0, 1, 2, 3), D_IN // 2
    dy, gw_pw, conv_stats = _conv_token_backward(dcv, y, conv_ln_w, conv_ln_b, w_pw_f, 256)
    da, dg, gcw = _conv_backward(dy, p_ga, p_gg, conv_w_f, 256)
    tw = min(1024, s)
    gw_hi = _weight_grad([(da, half_rows - SPLITS[2][0], 512), (dg, 0, 512), (dzc, 0, 512)], u, None, tw,
                         "grad_in_rows_hi")
    dq, dkt, dvt, r_out, r_pw, r_hi = _attn_backward(
        q, k2, v2, kt, o, do, lse, 512, _FusedReduce([(gw_out, all_chips), (gw_pw, all_chips), (gw_hi, (2, 3))]))
    dqkv, qk_stats = _qkv_backward(p_qkv, dq, dkt, dvt, cos, sin, qnw512, knw128, bd512, bd128, 256, cl)
    dpc, kc_stats = _ctx_kv_backward(pc_kv, dkt, dvt, knw128, bd128)
    gw_ctx = _weight_grad([(dpc, 0, 256)], uc, None, cl, "grad_in_rows_ctx")
    gw_lo = _weight_grad([(dqkv, 0, 768), (dza, 0, 512), (da, 0, half_rows - SPLITS[2][0])], u, gw_ctx, tw,
                         "grad_in_rows_lo")
    _, modc, gnw_c = _inproj_backward([dpc], ctx, None, scale1p_c, norm_w, w_in_t[KV_LO:KV_HI], cl,
                                      "ctx_inproj_backward")
    grad_x, modx, gnw_x, r_lo = _inproj_backward(
        [dqkv, dza, da, dg, dzc], x, dh, scale1p, norm_w, w_in_t, 512, "inproj_backward",
        _FusedReduce([(gw_lo, (0, 1))]))
    g_w_out, g_w_pw = r_out.reshape(d // 4, d), r_pw.reshape(D_CONV // 4, D_CONV)
    g_w_in_t = jnp.where(chip < 2, r_lo, r_hi).reshape(D_IN // 4, d)

    dmod_loc = jnp.concatenate([modx[:, 0, :], modx[:, 1, :], dgate[:, 0, :]], axis=1)
    gq = qk_stats[0].reshape(8, HEAD_DIM).sum(axis=0)
    gk = (qk_stats[1, :128] + kc_stats[0]).reshape(2, HEAD_DIM).sum(axis=0)
    packed = _pack_rows([dmod_loc, dmod_loc.sum(axis=0), gnw_x[0] + gnw_c[0], modc[0, 0], modc[0, 1], gq, gk,
                         conv_stats[0], conv_stats[1], conv_stats[2], conv_stats[3], gcw,
                         jnp.sum(loss_part[:, 0, 0])[None]], 32)
    total, g_w_mod, dsilu_ctx = _tail_exchange(packed, sc_rows, w_mod[0], bsz, 3 * bsz + 4)
    flat = total.reshape(-1)
    offs = [0]

    def take(nelem):
        lo = offs[0]
        offs[0] = lo + nelem
        return flat[lo:lo + nelem]

    take(bsz * 3 * d)
    g_b_mod_x = take(3 * d)
    g_norm_w = take(d)
    dshift_c, dscale_c = take(d), take(d)
    g_qnw, g_knw = take(HEAD_DIM), take(HEAD_DIM)
    g_b_pw, g_ln_w, g_ln_b, g_conv_b = take(D_CONV), take(D_CONV), take(D_CONV), take(D_CONV)
    g_conv_w_full = take(32 * D_CONV).reshape(32, D_CONV)
    loss = take(1)[0] * (0.5 / d)

    dmod_c = jnp.concatenate([dshift_c, dscale_c, jnp.zeros((d,), F32)])
    g_b_mod = (g_b_mod_x + dmod_c)[None, :]
    sg = _sigmoid(c_ctx)
    g_c_ctx = dsilu_ctx[0] * (sg * (1.0 + c_ctx * (1.0 - sg)))

    g_w_in = g_w_in_t.T
    g_conv_w = lax.dynamic_slice(g_conv_w_full, (0, chip * 128), (CONV_WIDTH, 128))

    grads = {
        "c_ctx": g_c_ctx, "w_mod": g_w_mod[None], "b_mod": g_b_mod, "norm_w": g_norm_w[None], "w_in": g_w_in[None],
        "q_norm_w": g_qnw[None], "k_norm_w": g_knw[None], "conv_w": g_conv_w[None], "conv_b": g_conv_b[None],
        "conv_ln_w": g_ln_w[None], "conv_ln_b": g_ln_b[None], "w_pw": g_w_pw[None], "b_pw": g_b_pw[None],
        "w_out": g_w_out[None],
    }
    weights = {
        "c_ctx": (c_ctx, m_c_ctx, v_c_ctx), "w_mod": (w_mod, m_w_mod, v_w_mod), "b_mod": (b_mod, m_b_mod, v_b_mod),
        "norm_w": (norm_w, m_norm_w, v_norm_w), "w_in": (w_in, m_w_in, v_w_in),
        "q_norm_w": (q_norm_w, m_q_norm_w, v_q_norm_w), "k_norm_w": (k_norm_w, m_k_norm_w, v_k_norm_w),
        "conv_w": (conv_w, m_conv_w, v_conv_w), "conv_b": (conv_b, m_conv_b, v_conv_b),
        "conv_ln_w": (conv_ln_w, m_conv_ln_w, v_conv_ln_w), "conv_ln_b": (conv_ln_b, m_conv_ln_b, v_conv_ln_b),
        "w_pw": (w_pw, m_w_pw, v_w_pw), "b_pw": (b_pw, m_b_pw, v_b_pw), "w_out": (w_out, m_w_out, v_w_out),
    }
    names = list(weights)
    big = ("w_mod", "w_in", "w_out")
    as_2d = lambda a: a.reshape((1, a.shape[0]) if a.ndim == 1 else (a.shape[-2] if a.ndim == 3 else 1, a.shape[-1]))
    small = [n for n in names if n not in big]
    w_g_m_v = zip(*[[as_2d(a) for a in (weights[n][0], grads[n], weights[n][1], weights[n][2])] for n in small])
    updates = dict(zip(small, _adamw_small(*[list(col) for col in w_g_m_v])))
    for n in ("w_mod", "w_out"):
        w, m, v = weights[n]
        updates[n] = _adamw(as_2d(w), as_2d(grads[n]), as_2d(m), as_2d(v), "adamw_" + n)
    w, m, v = weights["w_in"]
    updates["w_in"] = tuple(r.T for r in _adamw(w[0].T, g_w_in_t, m[0].T, v[0].T, "adamw_w_in"))
    deltas, new_ms, new_vs = ([updates[n][k].reshape(weights[n][0].shape) for n in names] for k in range(3))
    grads = {n: grads[n].reshape(weights[n][0].shape) for n in names}

    return (loss, grad_x, *[grads[n] for n in names], *deltas, *new_ms, *new_vs)
```

```python
import functools
import math

import jax
import jax.numpy as jnp
import numpy as np
from jax import lax
from jax.experimental import pallas as pl
from jax.experimental.pallas import tpu as pltpu

F32 = jnp.float32
BF16 = jnp.bfloat16
MESH = pl.DeviceIdType.MESH

D_MODEL = 1024
D_ATTN = 512
D_CONV = 512
HEAD_DIM = 64
N_KV = 2
GRID_W = 64
ROPE_AXIS_DIM = 32
ROPE_THETA = 10000.0
CONV_WIDTH = 31
CONV_PAD = 15
HALO = 16
CONV_ROWS = 32
EPS = 1e-6
SPLITS = ((0, 768), (768, 1280), (1280, 1792), (1792, 2304), (2304, 2816))
D_IN = 2816
KV_LO, KV_HI = 512, 768

ADAM_LR = 0.001
ADAM_B1 = 0.9
ADAM_B2 = 0.999
ADAM_EPS = 1e-08
ADAM_WD = 0.01
ADAM_STEP = 10

VMEM_LIMIT = 56 * 1024 * 1024

NT = (((1,), (1,)), ((), ()))
TN = (((0,), (0,)), ((), ()))


def _params(n_axes=0, **kw):
    if n_axes:
        kw["dimension_semantics"] = ("arbitrary",) * n_axes
    return pltpu.CompilerParams(vmem_limit_bytes=VMEM_LIMIT, **kw)


def _sigmoid(x):
    return 1.0 / (1.0 + jnp.exp(-x))


def _silu_and_grad(z):
    s = _sigmoid(z)
    return z * s, s * (1.0 + z * (1.0 - s))


def _seg_mean(v, ones_bd):
    hi = v.astype(BF16)
    lo = (v - hi.astype(F32)).astype(BF16)
    s = jnp.dot(hi, ones_bd, preferred_element_type=F32) + jnp.dot(lo, ones_bd, preferred_element_type=F32)
    return s * (1.0 / HEAD_DIM)


def _partner(v):
    n = v.shape[1]
    lane = lax.broadcasted_iota(jnp.int32, (v.shape[0], 128), 1)
    first = (lane % 32) < 16
    parts = []
    for k in range(n // 128):
        ch = v[:, 128 * k:128 * (k + 1)]
        parts.append(jnp.where(first, pltpu.roll(ch, 112, 1), pltpu.roll(ch, 16, 1)))
    return parts[0] if len(parts) == 1 else jnp.concatenate(parts, axis=1)


def _tile_lanes(t, reps):
    return t if reps == 1 else jnp.concatenate([t] * reps, axis=1)


def _lo_mask(rows):
    return lax.broadcasted_iota(jnp.int32, (rows, 128), 1) < HEAD_DIM


def _gather8_in_vmem(x_ref, out_ref, send_sems, recv_sems, local_sem):
    x, y, c = lax.axis_index("x"), lax.axis_index("y"), lax.axis_index("c")
    me, sibling = (x, y, c), (x, y, 1 - c)
    chips = [(1 - x, y), (x, 1 - y), (1 - x, 1 - y)]

    def slot(px, py, pc):
        return out_ref.at[4 * px + 2 * py + pc]

    def copy(k, block, to, src=None):
        return pltpu.make_async_remote_copy(
            src_ref=slot(*block) if src is None else src, dst_ref=slot(*block),
            send_sem=send_sems.at[k], recv_sem=recv_sems.at[k], device_id=to, device_id_type=MESH)

    mine = pltpu.make_async_copy(x_ref, slot(*me), local_sem)
    mine.start()
    first = [copy(0, me, sibling, src=x_ref)]
    first += [copy(1 + j, me, (*chip, c), src=x_ref) for j, chip in enumerate(chips)]
    for cp in first:
        cp.start()
    passed = [copy(4 + j, (*chip, c), sibling) for j, chip in enumerate(chips)]
    for j, chip in enumerate(chips):
        copy(1 + j, (*chip, c), me).wait_recv()
        passed[j].start()
    copy(0, sibling, me).wait_recv()
    for j, chip in enumerate(chips):
        copy(4 + j, (*chip, 1 - c), me).wait_recv()
    for cp in first + passed:
        cp.wait_send()
    mine.wait()


class _ChipGather:
    def __init__(self, arrs):
        self.arrs = list(arrs)
        n = self.n = len(self.arrs)
        self.in_specs = [pl.BlockSpec(memory_space=pl.ANY)] * n
        self.out_shape = [jax.ShapeDtypeStruct((4,) + a.shape, a.dtype) for a in self.arrs]
        self.out_specs = [pl.BlockSpec(memory_space=pl.ANY)] * n
        self.scratch = [pltpu.SemaphoreType.DMA((6 * n,)), pltpu.SemaphoreType.DMA((6 * n,)),
                        pltpu.SemaphoreType.DMA((n,))]
        self.phases = [self.start, self.forward, self.finish]

    def bind(self, ins, outs, scratch):
        self.ins, self.outs = ins, outs
        self.send_sems, self.recv_sems, self.local_sems = scratch
        self.x, self.y, self.c = lax.axis_index("x"), lax.axis_index("y"), lax.axis_index("c")
        self.chips = [(1 - self.x, self.y), (self.x, 1 - self.y), (1 - self.x, 1 - self.y)]
        self.mychip = 2 * self.x + self.y

    def _copy(self, a, k, chip_idx, cc, to, src=None):
        h = self.arrs[a].shape[0] // 2
        dst = self.outs[a].at[chip_idx, pl.ds(cc * h, h)]
        return pltpu.make_async_remote_copy(
            src_ref=dst if src is None else src, dst_ref=dst, send_sem=self.send_sems.at[6 * a + k],
            recv_sem=self.recv_sems.at[6 * a + k], device_id=to, device_id_type=MESH)

    def _local(self, a):
        return pltpu.make_async_copy(self.ins[a], self.outs[a].at[self.mychip], self.local_sems.at[a])

    def _first(self, a, j):
        h = self.arrs[a].shape[0] // 2
        return self._copy(a, j, self.mychip, self.c, (*self.chips[j], self.c), src=self.ins[a].at[pl.ds(self.c * h, h)])

    def _passed(self, a, j):
        cx, cy = self.chips[j]
        return self._copy(a, 3 + j, 2 * cx + cy, self.c, (self.x, self.y, 1 - self.c))

    def start(self):
        for a in range(self.n):
            self._local(a).start()
            for j in range(3):
                self._first(a, j).start()

    def forward(self):
        for a in range(self.n):
            for j, (cx, cy) in enumerate(self.chips):
                self._copy(a, j, 2 * cx + cy, self.c, (self.x, self.y, self.c)).wait_recv()
                self._passed(a, j).start()

    def finish(self):
        for a in range(self.n):
            for j, (cx, cy) in enumerate(self.chips):
                self._copy(a, 3 + j, 2 * cx + cy, 1 - self.c, (self.x, self.y, self.c)).wait_recv()
        for a in range(self.n):
            for j in range(3):
                self._first(a, j).wait_send()
                self._passed(a, j).wait_send()
            self._local(a).wait()


class _FusedReduce:
    def __init__(self, pieces):
        self.owners = [tuple(o) for _, o in pieces]
        self.arrs = [g.reshape(len(o), 2, g.shape[0] // (2 * len(o)), g.shape[1]) for g, o in pieces]
        n = self.n = len(pieces)
        hc = self.hc = [(v.shape[2], v.shape[3]) for v in self.arrs]
        nts = [len(o) for o in self.owners]
        self.base = [sum(nts[:p]) for p in range(n)]
        anyspec = pl.BlockSpec(memory_space=pl.ANY)
        self.in_specs = [anyspec] * n
        self.out_shape = [jax.ShapeDtypeStruct((2,) + s, F32) for s in hc]
        self.out_specs = [anyspec] * n
        self.scratch = [pltpu.VMEM((nt,) + s, F32) for nt, s in zip(nts, hc)]
        self.scratch += [pltpu.VMEM((nt,) + s, F32) for nt, s in zip(nts, hc)]
        self.scratch += [pltpu.VMEM(s, F32) for s in hc]
        self.scratch += [pltpu.VMEM((nt,) + s, BF16) for nt, s in zip(nts, hc)]
        self.scratch += [pltpu.VMEM((3,) + s, BF16) for s in hc]
        self.scratch += [pltpu.VMEM(s, F32) for s in hc]
        tot = sum(nts)
        self.scratch += [pltpu.SemaphoreType.DMA((tot,)), pltpu.SemaphoreType.DMA((tot,)),
                         pltpu.SemaphoreType.DMA((tot,)), pltpu.SemaphoreType.DMA((3 * n,)),
                         pltpu.SemaphoreType.DMA((n,)), pltpu.SemaphoreType.DMA((n,)), pltpu.SemaphoreType.DMA((n,)),
                         pltpu.SemaphoreType.DMA((tot,))]
        self.phases = [self.start, self.exchange, self.combine, self.finish]

    def bind(self, ins, outs, scratch):
        n = self.n
        self.g, self.out = ins, outs
        self.va, self.recv_a, self.own = scratch[:n], scratch[n:2 * n], scratch[2 * n:3 * n]
        self.tsend, self.recv_b, self.fin = scratch[3 * n:4 * n], scratch[4 * n:5 * n], scratch[5 * n:6 * n]
        self.sa, self.ra, self.sb, self.rb, self.sc, self.rc, self.lc, self.la = scratch[6 * n:]
        self.x, self.y, self.c = lax.axis_index("x"), lax.axis_index("y"), lax.axis_index("c")
        self.mychip = 2 * self.x + self.y
        self.sibling = (self.x, self.y, 1 - self.c)

    def _copy_a(self, p, t):
        k = self.base[p] + t
        return pltpu.make_async_remote_copy(
            src_ref=self.g[p].at[t, 1 - self.c], dst_ref=self.recv_a[p].at[t], send_sem=self.sa.at[k],
            recv_sem=self.ra.at[k], device_id=self.sibling, device_id_type=MESH)

    def _fetch(self, p, t):
        return pltpu.make_async_copy(self.g[p].at[t, self.c], self.va[p].at[t], self.la.at[self.base[p] + t])

    def _slot(self, owner):
        rel = jnp.bitwise_xor(self.mychip, owner)
        return jnp.where(rel == 2, 0, jnp.where(rel == 1, 1, 2))

    def _copy_b(self, p, t, slot):
        owner = self.owners[p][t]
        return pltpu.make_async_remote_copy(
            src_ref=self.tsend[p].at[t], dst_ref=self.recv_b[p].at[slot], send_sem=self.sb.at[self.base[p] + t],
            recv_sem=self.rb.at[3 * p + slot], device_id=(owner // 2, owner % 2, self.c), device_id_type=MESH)

    def _copy_c(self, p, half):
        return pltpu.make_async_remote_copy(
            src_ref=self.fin[p], dst_ref=self.out[p].at[half], send_sem=self.sc.at[p], recv_sem=self.rc.at[p],
            device_id=self.sibling, device_id_type=MESH)

    def _local_c(self, p):
        return pltpu.make_async_copy(self.fin[p], self.out[p].at[self.c], self.lc.at[p])

    def start(self):
        for p in range(self.n):
            for t in range(len(self.owners[p])):
                self._copy_a(p, t).start()
                self._fetch(p, t).start()

    def exchange(self):
        for p in range(self.n):
            for t, owner in enumerate(self.owners[p]):
                self._copy_a(p, t).wait_recv()
                self._fetch(p, t).wait()
                mine = self.mychip == owner

                @pl.when(mine)
                def _():
                    self.own[p][...] = self.va[p][t] + self.recv_a[p][t]

                @pl.when(jnp.logical_not(mine))
                def _():
                    self.tsend[p][t] = (self.va[p][t] + self.recv_a[p][t]).astype(BF16)
                    self._copy_b(p, t, self._slot(owner)).start()

    def combine(self):
        for p in range(self.n):
            for t, owner in enumerate(self.owners[p]):
                @pl.when(self.mychip == owner)
                def _():
                    acc = self.own[p][...]
                    for j in range(3):
                        self._copy_b(p, t, j).wait_recv()
                        acc = acc + self.recv_b[p][j].astype(F32)
                    self.fin[p][...] = acc
                    self._local_c(p).start()
                    self._copy_c(p, self.c).start()

    def finish(self):
        for p in range(self.n):
            for t, owner in enumerate(self.owners[p]):
                self._copy_a(p, t).wait_send()
                mine = self.mychip == owner

                @pl.when(mine)
                def _():
                    self._copy_c(p, 1 - self.c).wait_recv()
                    self._copy_c(p, self.c).wait_send()
                    self._local_c(p).wait()

                @pl.when(jnp.logical_not(mine))
                def _():
                    self._copy_b(p, t, self._slot(owner)).wait_send()


def _split_fused(refs, n_in, n_out, n_scr, fused):
    if fused is None:
        return refs[:n_in], refs[n_in:n_in + n_out], refs[n_in + n_out:]
    fi, fo = len(fused.in_specs), len(fused.out_specs)
    ins, rest = refs[:n_in], refs[n_in:]
    f_ins, rest = rest[:fi], rest[fi:]
    outs, rest = rest[:n_out], rest[n_out:]
    f_outs, rest = rest[:fo], rest[fo:]
    scr, f_scr = rest[:n_scr], rest[n_scr:]
    fused.bind(f_ins, f_outs, f_scr)
    return ins, outs, scr


def _run_phases(fused, step, at_steps, before):
    if fused is None:
        return
    for phase, (at, first) in zip(fused.phases, at_steps):
        if first == before:
            pl.when(step == at)(phase)


def _front(c_pad, c_ctx_rows, w_mod, b_cols, w_in_t_loc):
    ncol = w_mod.shape[1]
    gather = _ChipGather([w_in_t_loc])

    def body(c_ref, cctx_ref, w_ref, b_ref, win_ref, sc_ref, modg_ref, wing_ref,
             call_ref, ag_send, ag_recv, ag_local, m_send, m_recv, *g_scr):
        gather.bind([win_ref], [wing_ref], g_scr)
        _gather8_in_vmem(c_ref, call_ref, ag_send, ag_recv, ag_local)
        gather.start()
        x, y, c = lax.axis_index("x"), lax.axis_index("y"), lax.axis_index("c")
        chips = [(1 - x, y), (x, 1 - y), (1 - x, 1 - y)]
        mychip = 2 * x + y
        rows = jnp.concatenate([call_ref[dv] for dv in range(8)] + [cctx_ref[...]], axis=0)
        sc = rows * _sigmoid(rows)
        sc_ref[...] = sc
        modg_ref[mychip] = jnp.dot(sc, w_ref[...], preferred_element_type=F32,
                                   precision=lax.Precision.HIGHEST) + b_ref[...]

        def mcopy(j, chip_idx, to):
            return pltpu.make_async_remote_copy(
                src_ref=modg_ref.at[chip_idx], dst_ref=modg_ref.at[chip_idx], send_sem=m_send.at[j],
                recv_sem=m_recv.at[j], device_id=to, device_id_type=MESH)

        sends = [mcopy(j, mychip, (*chip, c)) for j, chip in enumerate(chips)]
        for cp in sends:
            cp.start()
        for j, (cx, cy) in enumerate(chips):
            mcopy(j, 2 * cx + cy, (x, y, c)).wait_recv()
        gather.forward()
        gather.finish()
        for cp in sends:
            cp.wait_send()

    vm = pl.BlockSpec(memory_space=pltpu.VMEM)
    return pl.pallas_call(
        body, name="front_exchange",
        out_shape=[jax.ShapeDtypeStruct((80, D_MODEL), F32), jax.ShapeDtypeStruct((4, 80, ncol), F32)] + gather.out_shape,
        in_specs=[vm, vm, vm, vm] + gather.in_specs, out_specs=[vm, vm] + gather.out_specs,
        scratch_shapes=[pltpu.VMEM((8, 8, D_MODEL), F32), pltpu.SemaphoreType.DMA((7,)), pltpu.SemaphoreType.DMA((7,)),
                        pltpu.SemaphoreType.DMA, pltpu.SemaphoreType.DMA((3,)), pltpu.SemaphoreType.DMA((3,))]
        + gather.scratch,
        compiler_params=_params(),
    )(c_pad, c_ctx_rows, w_mod, b_cols, w_in_t_loc)


def _tail_exchange(packed, sc_rows, w_mod, bsz, ctx_row):
    d = D_MODEL
    ncol = w_mod.shape[1]

    def body(p_ref, sc_ref, w_ref, total_ref, gw_ref, gcc_ref, gat_ref, dm_ref, part_ref,
             ag_send, ag_recv, ag_local, g_send, g_recv):
        _gather8_in_vmem(p_ref, gat_ref, ag_send, ag_recv, ag_local)
        acc = gat_ref[0]
        for dv in range(1, 8):
            acc = acc + gat_ref[dv]
        total_ref[...] = acc
        x, y, c = lax.axis_index("x"), lax.axis_index("y"), lax.axis_index("c")
        chips = [(1 - x, y), (x, 1 - y), (1 - x, 1 - y)]
        mychip = 2 * x + y
        dm_ref[...] = jnp.zeros(dm_ref.shape, F32)
        for k in range(4):
            @pl.when(mychip == k)
            def _():
                spans = [(seg, max(k * ncol, seg * d) - seg * d, min((k + 1) * ncol, (seg + 1) * d) - seg * d)
                         for seg in range(3) if k * ncol < (seg + 1) * d and (k + 1) * ncol > seg * d]
                for dv in range(8):
                    for b in range(bsz):
                        dm_ref[8 * dv + b:8 * dv + b + 1, :] = jnp.concatenate(
                            [gat_ref[dv, 3 * b + seg:3 * b + seg + 1, lo:hi] for seg, lo, hi in spans], axis=1)
                dm_ref[64:65, :] = jnp.concatenate(
                    [total_ref[ctx_row + seg:ctx_row + seg + 1, lo:hi] if seg < 2 else jnp.zeros((1, hi - lo), F32)
                     for seg, lo, hi in spans], axis=1)

        dm = dm_ref[...]
        gw_ref[...] = lax.dot_general(sc_ref[...], dm, TN, preferred_element_type=F32,
                                      precision=lax.Precision.HIGHEST)
        part_ref[mychip] = lax.dot_general(dm[64:72, :], w_ref[...], NT, preferred_element_type=F32,
                                           precision=lax.Precision.HIGHEST)

        def gcopy(j, chip_idx, to):
            return pltpu.make_async_remote_copy(
                src_ref=part_ref.at[chip_idx], dst_ref=part_ref.at[chip_idx], send_sem=g_send.at[j],
                recv_sem=g_recv.at[j], device_id=to, device_id_type=MESH)

        sends = [gcopy(j, mychip, (*chip, c)) for j, chip in enumerate(chips)]
        for cp in sends:
            cp.start()
        for j, (cx, cy) in enumerate(chips):
            gcopy(j, 2 * cx + cy, (x, y, c)).wait_recv()
        for cp in sends:
            cp.wait_send()
        gcc_ref[...] = (part_ref[0] + part_ref[1]) + (part_ref[2] + part_ref[3])

    return pl.pallas_call(
        body, name="tail_exchange",
        out_shape=[jax.ShapeDtypeStruct(packed.shape, F32), jax.ShapeDtypeStruct((d, ncol), F32),
                   jax.ShapeDtypeStruct((8, d), F32)],
        scratch_shapes=[pltpu.VMEM((8,) + packed.shape, F32), pltpu.VMEM((80, ncol), F32), pltpu.VMEM((4, 8, d), F32),
                        pltpu.SemaphoreType.DMA((7,)), pltpu.SemaphoreType.DMA((7,)), pltpu.SemaphoreType.DMA,
                        pltpu.SemaphoreType.DMA((3,)), pltpu.SemaphoreType.DMA((3,))],
        compiler_params=_params(),
    )(packed, sc_rows, w_mod)


def _bcast_spec(arr):
    if arr.shape[0] == 1:
        return pl.BlockSpec((1, 1, arr.shape[2]), lambda b, i: (0, 0, 0))
    return pl.BlockSpec((1, 1, arr.shape[2]), lambda b, i: (b, 0, 0))


def _norm_inproj(x, shift, scale1p, norm_w, w_t, splits, tm, name):
    bsz, s, d = x.shape

    def body(x_ref, sh_ref, sc_ref, nw_ref, w_ref, u_ref, *out_refs):
        xv = x_ref[0]
        rstd = lax.rsqrt(jnp.mean(xv * xv, axis=-1, keepdims=True) + EPS)
        u = (xv * rstd * nw_ref[...]) * sc_ref[0] + sh_ref[0]
        ub = u.astype(BF16)
        u_ref[0] = ub
        for (lo, hi), o_ref in zip(splits, out_refs):
            o_ref[0] = lax.dot_general(ub, w_ref[lo:hi, :], NT, preferred_element_type=F32)

    tok = lambda w: pl.BlockSpec((1, tm, w), lambda b, i: (b, i, 0))
    return pl.pallas_call(
        body, name=name, grid=(bsz, s // tm),
        in_specs=[tok(d), _bcast_spec(shift), _bcast_spec(scale1p), pl.BlockSpec((1, d), lambda b, i: (0, 0)),
                  pl.BlockSpec(w_t.shape, lambda b, i: (0, 0))],
        out_specs=[tok(d)] + [tok(hi - lo) for lo, hi in splits],
        out_shape=[jax.ShapeDtypeStruct((bsz, s, d), BF16)]
        + [jax.ShapeDtypeStruct((bsz, s, hi - lo), F32) for lo, hi in splits],
        compiler_params=_params(2),
    )(x, shift, scale1p, norm_w, w_t)


def _dup_heads(kv, lo_mask):
    r = pltpu.roll(kv, HEAD_DIM, 1)
    return jnp.where(lo_mask, kv, r), jnp.where(lo_mask, r, kv)


def _qkv_prep(qkv, cos, sin, qnw, knw, bd512, bd128, ts, row0):
    bsz, s, _ = qkv.shape

    def body(p_ref, cos_ref, sin_ref, qnw_ref, knw_ref, bd512_ref, bd128_ref, q_ref, k_ref, v_ref, kt_ref, vt_ref):
        lo_mask = _lo_mask(ts)
        cos_t, sin_t = cos_ref[...], sin_ref[...]
        qp = p_ref[0, :, 0:512]
        qn = qp * lax.rsqrt(_seg_mean(qp * qp, bd512_ref[...]) + EPS) * qnw_ref[...]
        qr = qn * _tile_lanes(cos_t, 4) + _partner(qn) * _tile_lanes(sin_t, 4)
        q_ref[0] = (qr * (1.0 / math.sqrt(HEAD_DIM))).astype(BF16)
        kp = p_ref[0, :, 512:640]
        kn = kp * lax.rsqrt(_seg_mean(kp * kp, bd128_ref[...]) + EPS) * knw_ref[...]
        kr = kn * cos_t + _partner(kn) * sin_t
        k0, k1 = _dup_heads(kr, lo_mask)
        k_ref[0, 0] = k0.astype(BF16)
        k_ref[0, 1] = k1.astype(BF16)
        vp = p_ref[0, :, 640:768]
        v0, v1 = _dup_heads(vp, lo_mask)
        v_ref[0, 0] = v0.astype(BF16)
        v_ref[0, 1] = v1.astype(BF16)
        kt_ref[0] = kr.T.astype(BF16)
        vt_ref[0] = vp.T.astype(BF16)

    const = lambda a: pl.BlockSpec(a.shape, lambda b, i: (0,) * a.ndim)
    kv_spec = pl.BlockSpec((1, 2, ts, 128), lambda b, i: (b, 0, i + row0 // ts, 0))
    t_spec = pl.BlockSpec((1, 128, ts), lambda b, i: (b, 0, i + row0 // ts))
    return pl.pallas_call(
        body, name="qkv_prep", grid=(bsz, s // ts),
        in_specs=[pl.BlockSpec((1, ts, 768), lambda b, i: (b, i, 0)),
                  pl.BlockSpec((ts, 128), lambda b, i: (i, 0)), pl.BlockSpec((ts, 128), lambda b, i: (i, 0)),
                  const(qnw), const(knw), const(bd512), const(bd128)],
        out_specs=[pl.BlockSpec((1, ts, 512), lambda b, i: (b, i, 0)), kv_spec, kv_spec, t_spec, t_spec],
        out_shape=[jax.ShapeDtypeStruct((bsz, s, 512), BF16), jax.ShapeDtypeStruct((bsz, 2, row0 + s, 128), BF16),
                   jax.ShapeDtypeStruct((bsz, 2, row0 + s, 128), BF16),
                   jax.ShapeDtypeStruct((bsz, 128, row0 + s), BF16), jax.ShapeDtypeStruct((bsz, 128, row0 + s), BF16)],
        compiler_params=_params(2),
    )(qkv, cos, sin, qnw, knw, bd512, bd128)


def _ctx_kv_prep(pc, knw, bd128, k2, v2, kt, vt):
    bsz, cl, _ = pc.shape

    def body(p_ref, knw_ref, bd128_ref, k_in, v_in, kt_in, vt_in, k_ref, v_ref, kt_ref, vt_ref):
        lo_mask = _lo_mask(cl)
        kp = p_ref[0, :, 0:128]
        kn = kp * lax.rsqrt(_seg_mean(kp * kp, bd128_ref[...]) + EPS) * knw_ref[...]
        k0, k1 = _dup_heads(kn, lo_mask)
        k_ref[0, 0] = k0.astype(BF16)
        k_ref[0, 1] = k1.astype(BF16)
        vp = p_ref[0, :, 128:256]
        v0, v1 = _dup_heads(vp, lo_mask)
        v_ref[0, 0] = v0.astype(BF16)
        v_ref[0, 1] = v1.astype(BF16)
        kt_ref[0] = kn.T.astype(BF16)
        vt_ref[0] = vp.T.astype(BF16)

    const = lambda a: pl.BlockSpec(a.shape, lambda b: (0,) * a.ndim)
    kv_spec = pl.BlockSpec((1, 2, cl, 128), lambda b: (b, 0, 0, 0))
    t_spec = pl.BlockSpec((1, 128, cl), lambda b: (b, 0, 0))
    anyspec = pl.BlockSpec(memory_space=pl.ANY)
    return pl.pallas_call(
        body, name="ctx_kv_prep", grid=(bsz,),
        in_specs=[pl.BlockSpec((1, cl, 256), lambda b: (b, 0, 0)), const(knw), const(bd128)] + [anyspec] * 4,
        out_specs=[kv_spec, kv_spec, t_spec, t_spec],
        out_shape=[jax.ShapeDtypeStruct(a.shape, BF16) for a in (k2, v2, kt, vt)],
        input_output_aliases={3: 0, 4: 1, 5: 2, 6: 3},
        compiler_params=_params(1),
    )(pc, knw, bd128, k2, v2, kt, vt)


def _attn_forward(q, k2, vt, tq, fused):
    bsz, s, _ = q.shape
    sk = k2.shape[2]
    nq = s // tq
    total = bsz * N_KV * nq
    at_steps = [(0, True), (total // 4, True), (total - 1, False)]

    def body(*refs):
        (q_ref, k_ref, vt_ref), (o_ref, lse_ref), _ = _split_fused(refs, 3, 2, 0, fused)
        g = pl.program_id(1)
        step = (pl.program_id(0) * N_KV + g) * nq + pl.program_id(2)
        _run_phases(fused, step, at_steps, True)
        kk = k_ref[0, 0]
        lo_mask = _lo_mask(tq)
        vt_aug = jnp.concatenate([vt_ref[0, pl.ds(pl.multiple_of(g * HEAD_DIM, HEAD_DIM), HEAD_DIM), :],
                                  jnp.ones((16, sk), BF16)], axis=0)
        ps, ms = [], []
        for j in range(2):
            qp = q_ref[0, :, 128 * j:128 * (j + 1)]
            for half in range(2):
                sel = lo_mask if half == 0 else jnp.logical_not(lo_mask)
                qs = jnp.where(sel, qp, jnp.zeros_like(qp))
                sc = lax.dot_general(qs, kk, NT, preferred_element_type=F32)
                m = jnp.max(sc, axis=-1, keepdims=True)
                ps.append(jnp.exp(sc - m).astype(BF16))
                ms.append(m)
        ots = [lax.dot_general(vt_aug, p, NT, preferred_element_type=F32) for p in ps]
        for j in range(2):
            o_t, l_t = [], []
            for half in range(2):
                ot = ots[2 * j + half]
                l = ot[HEAD_DIM:HEAD_DIM + 1, :]
                o_t.append(ot[0:HEAD_DIM, :] / l)
                l_t.append(jnp.broadcast_to(l, (HEAD_DIM, tq)))
            o_ref[0, :, 128 * j:128 * (j + 1)] = jnp.concatenate(o_t, axis=0).T
            lse_ref[0, :, 128 * j:128 * (j + 1)] = (jnp.where(lo_mask, ms[2 * j], ms[2 * j + 1])
                                                    + jnp.log(jnp.concatenate(l_t, axis=0).T))
        _run_phases(fused, step, at_steps, False)

    q_spec = pl.BlockSpec((1, tq, 256), lambda b, g, i: (b, i, g))
    kv_spec = pl.BlockSpec((1, 1, sk, 128), lambda b, g, i: (b, g, 0, 0))
    return pl.pallas_call(
        body, name="attn_forward", grid=(bsz, N_KV, nq),
        in_specs=[q_spec, kv_spec, pl.BlockSpec((1, 128, sk), lambda b, g, i: (b, 0, 0))] + fused.in_specs,
        out_specs=[q_spec, q_spec] + fused.out_specs,
        out_shape=[jax.ShapeDtypeStruct((bsz, s, 512), F32)] * 2 + fused.out_shape,
        scratch_shapes=fused.scratch,
        compiler_params=_params(3),
    )(q, k2, vt, *fused.arrs)


def _halo_specs(width, ts, s):
    r = ts // HALO
    last = s // HALO - 1
    return [pl.BlockSpec((1, ts, width), lambda b, i: (b, i, 0)),
            pl.BlockSpec((1, HALO, width), lambda b, i: (b, jnp.maximum(i * r - 1, 0), 0)),
            pl.BlockSpec((1, HALO, width), lambda b, i: (b, jnp.minimum((i + 1) * r, last), 0))]


def _fill_ext(ext_ref, cur, prev, nxt, i, n_tiles, ts):
    ext_ref[0:HALO, :] = jnp.where(i > 0, prev, jnp.zeros_like(prev))
    ext_ref[HALO:HALO + ts, :] = cur
    ext_ref[HALO + ts:2 * HALO + ts, :] = jnp.where(i < n_tiles - 1, nxt, jnp.zeros_like(nxt))


def _fill_shifted(sh_ref, ext_ref, ts):
    n = ts + 2 * HALO - 8
    for r in range(1, 8):
        sh_ref[r - 1, 0:n, :] = ext_ref[pl.ds(r, n), :]


def _window(sh_ref, ext_ref, off, rows, r0=0):
    q, r = divmod(off, 8)
    if r == 0:
        return ext_ref[pl.ds(r0 + off, rows), :]
    return sh_ref[r - 1, pl.ds(r0 + 8 * q, rows), :]


def _conv_forward(ga, gg, conv_w, conv_b, ln_w, ln_b, w_pw, b_pw, ts):
    bsz, s, dc = ga.shape
    n_tiles = s // ts

    def body(a_ref, ap_ref, an_ref, g_ref, gp_ref, gn_ref, cw_ref, cb_ref, lw_ref, lb_ref, wp_ref, bp_ref,
             y_ref, cv_ref, ext_ref, sh_ref):
        i = pl.program_id(1)
        glu = lambda a, g: a * _sigmoid(g)
        _fill_ext(ext_ref, glu(a_ref[0], g_ref[0]), glu(ap_ref[0], gp_ref[0]), glu(an_ref[0], gn_ref[0]), i, n_tiles, ts)
        _fill_shifted(sh_ref, ext_ref, ts)
        acc = jnp.broadcast_to(cb_ref[...], (ts, dc))
        for j in range(CONV_WIDTH):
            acc = acc + cw_ref[j:j + 1, :] * _window(sh_ref, ext_ref, HALO - CONV_PAD + j, ts)
        y_ref[0] = acc
        mu = jnp.mean(acc, axis=-1, keepdims=True)
        yc = acc - mu
        var = jnp.mean(yc * yc, axis=-1, keepdims=True)
        yn = yc * lax.rsqrt(var + EPS) * lw_ref[...] + lb_ref[...]
        ys = yn * _sigmoid(yn)
        cv_ref[0] = jnp.dot(ys.astype(BF16), wp_ref[...], preferred_element_type=F32) + bp_ref[...]

    const = lambda a: pl.BlockSpec(a.shape, lambda b, i: (0,) * a.ndim)
    return pl.pallas_call(
        body, name="conv_forward", grid=(bsz, n_tiles),
        in_specs=_halo_specs(dc, ts, s) + _halo_specs(dc, ts, s)
        + [const(conv_w), const(conv_b), const(ln_w), const(ln_b), const(w_pw), const(b_pw)],
        out_specs=[pl.BlockSpec((1, ts, dc), lambda b, i: (b, i, 0))] * 2,
        out_shape=[jax.ShapeDtypeStruct((bsz, s, dc), F32)] * 2,
        scratch_shapes=[pltpu.VMEM((ts + 2 * HALO, dc), F32), pltpu.VMEM((7, ts + 2 * HALO, dc), F32)],
        compiler_params=_params(2),
    )(ga, ga, ga, gg, gg, gg, conv_w, conv_b, ln_w, ln_b, w_pw, b_pw)


def _outproj_loss(x, target, gate, o, za, cv, zc, w_out, tm):
    bsz, s, d = x.shape

    def body(x_ref, t_ref, gate_ref, o_ref, za_ref, cv_ref, zc_ref, w_ref,
             loss_ref, dh_ref, do_ref, dza_ref, dcv_ref, dzc_ref, dgate_ref, gw_ref):
        b, i = pl.program_id(0), pl.program_id(1)
        ov, cvv = o_ref[0], cv_ref[0]
        silu_a, dsilu_a = _silu_and_grad(za_ref[0])
        silu_c, dsilu_c = _silu_and_grad(zc_ref[0])
        mix = jnp.concatenate([ov * silu_a, cvv * silu_c], axis=1).astype(BF16)
        out = jnp.dot(mix, w_ref[...], preferred_element_type=F32)
        gate_v = gate_ref[0]
        err = x_ref[0] + gate_v * out - t_ref[0]
        dh = err * (1.0 / d)
        dh_ref[0] = dh
        dout = (dh * gate_v).astype(BF16)
        dmix = lax.dot_general(dout, w_ref[...], NT, preferred_element_type=F32)
        gw = lax.dot_general(mix, dout, TN, preferred_element_type=F32)
        dg = jnp.sum(dh * out, axis=0, keepdims=True)
        sq = jnp.sum(err * err)

        @pl.when(jnp.logical_and(b == 0, i == 0))
        def _():
            gw_ref[...] = gw

        @pl.when(jnp.logical_or(b > 0, i > 0))
        def _():
            gw_ref[...] += gw

        @pl.when(i == 0)
        def _():
            dgate_ref[0] = dg
            loss_ref[...] = jnp.zeros(loss_ref.shape, F32) + sq

        @pl.when(i > 0)
        def _():
            dgate_ref[0] += dg
            loss_ref[...] += sq

        dma, dmc = dmix[:, :D_ATTN], dmix[:, D_ATTN:]
        do_ref[0] = dma * silu_a
        dza_ref[0] = (dma * ov * dsilu_a).astype(BF16)
        dcv_ref[0] = dmc * silu_c
        dzc_ref[0] = (dmc * cvv * dsilu_c).astype(BF16)

    tok = lambda w: pl.BlockSpec((1, tm, w), lambda b, i: (b, i, 0))
    return pl.pallas_call(
        body, name="outproj_loss", grid=(bsz, s // tm),
        in_specs=[tok(d), tok(d), _bcast_spec(gate), tok(512), tok(512), tok(512), tok(512),
                  pl.BlockSpec(w_out.shape, lambda b, i: (0, 0))],
        out_specs=[pl.BlockSpec((1, 8, 128), lambda b, i: (b, 0, 0)), tok(d), tok(512), tok(512), tok(512), tok(512),
                   pl.BlockSpec((1, 1, d), lambda b, i: (b, 0, 0)), pl.BlockSpec((d, d), lambda b, i: (0, 0))],
        out_shape=[jax.ShapeDtypeStruct((bsz, 8, 128), F32), jax.ShapeDtypeStruct((bsz, s, d), F32),
                   jax.ShapeDtypeStruct((bsz, s, 512), F32), jax.ShapeDtypeStruct((bsz, s, 512), BF16),
                   jax.ShapeDtypeStruct((bsz, s, 512), F32), jax.ShapeDtypeStruct((bsz, s, 512), BF16),
                   jax.ShapeDtypeStruct((bsz, 1, d), F32), jax.ShapeDtypeStruct((d, d), F32)],
        compiler_params=_params(2),
    )(x, target, gate, o, za, cv, zc, w_out)


def _conv_token_backward(dcv, y, ln_w, ln_b, w_pw, tm):
    bsz, s, dc = dcv.shape

    def body(dcv_ref, y_ref, lw_ref, lb_ref, wp_ref, dy_ref, gwp_ref, st_ref):
        b, i = pl.program_id(0), pl.program_id(1)
        yv, dcvv = y_ref[0], dcv_ref[0]
        mu = jnp.mean(yv, axis=-1, keepdims=True)
        yc = yv - mu
        rstd = lax.rsqrt(jnp.mean(yc * yc, axis=-1, keepdims=True) + EPS)
        yhat = yc * rstd
        yn = yhat * lw_ref[...] + lb_ref[...]
        ys, dsilu = _silu_and_grad(yn)
        dcvb = dcvv.astype(BF16)
        gwp = lax.dot_general(ys.astype(BF16), dcvb, TN, preferred_element_type=F32)
        dys = lax.dot_general(dcvb, wp_ref[...], NT, preferred_element_type=F32)
        dyn = dys * dsilu
        dyhat = dyn * lw_ref[...]
        dy = rstd * (dyhat - jnp.mean(dyhat, axis=-1, keepdims=True)
                     - yhat * jnp.mean(dyhat * yhat, axis=-1, keepdims=True))
        dy_ref[0] = dy
        red = lambda v: jnp.sum(v, axis=0, keepdims=True)
        stats = jnp.concatenate([red(dcvv), red(dyn * yhat), red(dyn), red(dy), jnp.zeros((4, dc), F32)], axis=0)
        first = jnp.logical_and(b == 0, i == 0)

        @pl.when(first)
        def _():
            gwp_ref[...] = gwp
            st_ref[...] = stats

        @pl.when(jnp.logical_not(first))
        def _():
            gwp_ref[...] += gwp
            st_ref[...] += stats

    tok = pl.BlockSpec((1, tm, dc), lambda b, i: (b, i, 0))
    const = lambda a: pl.BlockSpec(a.shape, lambda b, i: (0,) * a.ndim)
    return pl.pallas_call(
        body, name="conv_token_backward", grid=(bsz, s // tm),
        in_specs=[tok, tok, const(ln_w), const(ln_b), const(w_pw)],
        out_specs=[tok, pl.BlockSpec((dc, dc), lambda b, i: (0, 0)), pl.BlockSpec((8, dc), lambda b, i: (0, 0))],
        out_shape=[jax.ShapeDtypeStruct((bsz, s, dc), F32), jax.ShapeDtypeStruct((dc, dc), F32),
                   jax.ShapeDtypeStruct((8, dc), F32)],
        compiler_params=_params(2),
    )(dcv, y, ln_w, ln_b, w_pw)


def _conv_backward(dy, ga, gg, conv_w, ts):
    bsz, s, dc = dy.shape
    n_tiles = s // ts

    def body(dy_ref, dyp_ref, dyn_ref, a_ref, g_ref, cw_ref,
             da_ref, dg_ref, gcw_ref, dyext_ref, dysh_ref, ug_ref, dug_ref, gacc_ref):
        b, i = pl.program_id(0), pl.program_id(1)
        av, sg = a_ref[0], _sigmoid(g_ref[0])
        ug_ref[...] = av * sg
        _fill_ext(dyext_ref, dy_ref[0], dyp_ref[0], dyn_ref[0], i, n_tiles, ts)
        _fill_shifted(dysh_ref, dyext_ref, ts)
        gacc_ref[...] = jnp.zeros(gacc_ref.shape, F32)

        def row_block(r, carry):
            r0 = pl.multiple_of(r * CONV_ROWS, CONV_ROWS)
            ugb = ug_ref[pl.ds(r0, CONV_ROWS), :]
            acc = jnp.zeros((CONV_ROWS, dc), F32)
            for j in range(CONV_WIDTH):
                win = _window(dysh_ref, dyext_ref, HALO + CONV_PAD - j, CONV_ROWS, r0)
                acc = acc + cw_ref[j:j + 1, :] * win
                prod = ugb * win
                part = prod[0:8, :]
                for k in range(8, CONV_ROWS, 8):
                    part = part + prod[k:k + 8, :]
                gacc_ref[j] += part
            dug_ref[pl.ds(r0, CONV_ROWS), :] = acc
            return carry

        lax.fori_loop(0, ts // CONV_ROWS, row_block, 0)
        dug = dug_ref[...]
        gcw = jnp.sum(gacc_ref[...], axis=1)
        first = jnp.logical_and(b == 0, i == 0)

        @pl.when(first)
        def _():
            gcw_ref[...] = gcw

        @pl.when(jnp.logical_not(first))
        def _():
            gcw_ref[...] += gcw

        da_ref[0] = (dug * sg).astype(BF16)
        dg_ref[0] = (dug * av * sg * (1.0 - sg)).astype(BF16)

    tok = pl.BlockSpec((1, ts, dc), lambda b, i: (b, i, 0))
    return pl.pallas_call(
        body, name="conv_backward", grid=(bsz, n_tiles),
        in_specs=_halo_specs(dc, ts, s) + [tok, tok, pl.BlockSpec(conv_w.shape, lambda b, i: (0, 0))],
        out_specs=[tok, tok, pl.BlockSpec((32, dc), lambda b, i: (0, 0))],
        out_shape=[jax.ShapeDtypeStruct((bsz, s, dc), BF16), jax.ShapeDtypeStruct((bsz, s, dc), BF16),
                   jax.ShapeDtypeStruct((32, dc), F32)],
        scratch_shapes=[pltpu.VMEM((ts + 2 * HALO, dc), F32), pltpu.VMEM((7, ts + 2 * HALO, dc), F32),
                        pltpu.VMEM((ts, dc), F32), pltpu.VMEM((ts, dc), F32), pltpu.VMEM((32, 8, dc), F32)],
        compiler_params=_params(2),
    )(dy, dy, dy, ga, gg, conv_w)


def _attn_backward(q, k2, v2, kt, o, do, lse, tq, fused):
    bsz, s, _ = q.shape
    sk = k2.shape[2]
    scale = 1.0 / math.sqrt(HEAD_DIM)
    nq = s // tq
    total = bsz * N_KV * nq
    at_steps = [(0, True), (total // 5, True), (total // 2, True), (total - 1, False)]

    def body(*refs):
        (q_ref, k_ref, v_ref, kt_ref, o_ref, do_ref, lse_ref), (dq_ref, dk_ref, dv_ref), _ = _split_fused(
            refs, 7, 3, 0, fused)
        g, i = pl.program_id(1), pl.program_id(2)
        step = (pl.program_id(0) * N_KV + g) * nq + i
        _run_phases(fused, step, at_steps, True)
        kk, vv = k_ref[0, 0], v_ref[0, 0]
        kgt = kt_ref[0, pl.ds(pl.multiple_of(g * HEAD_DIM, HEAD_DIM), HEAD_DIM), :]
        lo_mask = _lo_mask(tq)
        dk_acc = jnp.zeros((HEAD_DIM, sk), F32)
        dv_acc = jnp.zeros((HEAD_DIM, sk), F32)
        for j in range(2):
            cols = slice(128 * j, 128 * (j + 1))
            qp, dop, lsep = q_ref[0, :, cols], do_ref[0, :, cols], lse_ref[0, :, cols]
            dprod = dop * o_ref[0, :, cols]
            q_t = qp.astype(F32).T.astype(BF16)
            do_t = dop.T.astype(BF16)
            dq_t = []
            for half in range(2):
                sel = lo_mask if half == 0 else jnp.logical_not(lo_mask)
                rows = slice(HEAD_DIM * half, HEAD_DIM * (half + 1))
                qs = jnp.where(sel, qp, jnp.zeros_like(qp))
                dos = jnp.where(sel, dop, 0.0).astype(BF16)
                lse_h = jnp.max(jnp.where(sel, lsep, -jnp.inf), axis=-1, keepdims=True)
                delta = jnp.sum(jnp.where(sel, dprod, 0.0), axis=-1, keepdims=True)
                sc = lax.dot_general(qs, kk, NT, preferred_element_type=F32)
                p = jnp.exp(sc - lse_h)
                dp = lax.dot_general(dos, vv, NT, preferred_element_type=F32)
                ds = (p * (dp - delta)).astype(BF16)
                dv_acc = dv_acc + jnp.dot(do_t[rows, :], p.astype(BF16), preferred_element_type=F32)
                dk_acc = dk_acc + jnp.dot(q_t[rows, :], ds, preferred_element_type=F32)
                dq_t.append(lax.dot_general(kgt, ds, NT, preferred_element_type=F32))
            dq_ref[0, :, cols] = (jnp.concatenate(dq_t, axis=0) * scale).T

        @pl.when(i == 0)
        def _():
            dk_ref[0, 0] = dk_acc
            dv_ref[0, 0] = dv_acc

        @pl.when(i > 0)
        def _():
            dk_ref[0, 0] += dk_acc
            dv_ref[0, 0] += dv_acc

        _run_phases(fused, step, at_steps, False)

    q_spec = pl.BlockSpec((1, tq, 256), lambda b, g, i: (b, i, g))
    kv_spec = pl.BlockSpec((1, 1, sk, 128), lambda b, g, i: (b, g, 0, 0))
    acc_spec = pl.BlockSpec((1, 1, HEAD_DIM, sk), lambda b, g, i: (b, g, 0, 0))
    return pl.pallas_call(
        body, name="attn_backward", grid=(bsz, N_KV, nq),
        in_specs=[q_spec, kv_spec, kv_spec, pl.BlockSpec((1, 128, sk), lambda b, g, i: (b, 0, 0)), q_spec, q_spec,
                  q_spec] + fused.in_specs,
        out_specs=[q_spec, acc_spec, acc_spec] + fused.out_specs,
        out_shape=[jax.ShapeDtypeStruct((bsz, s, 512), F32), jax.ShapeDtypeStruct((bsz, 2, HEAD_DIM, sk), F32),
                   jax.ShapeDtypeStruct((bsz, 2, HEAD_DIM, sk), F32)] + fused.out_shape,
        scratch_shapes=fused.scratch,
        compiler_params=_params(3),
    )(q, k2, v2, kt, o, do, lse, *fused.arrs)


def _heads_to_lanes(acc_ref):
    return jnp.concatenate([acc_ref[0, 0], acc_ref[0, 1]], axis=0).T


def _norm_backward(dn, pre, w, bd):
    rstd = lax.rsqrt(_seg_mean(pre * pre, bd) + EPS)
    xhat = pre * rstd
    dxhat = dn * w
    return rstd * (dxhat - xhat * _seg_mean(dxhat * xhat, bd)), dn * xhat


def _qkv_backward(qkv, dq, dk2, dv2, cos, sin, qnw, knw, bd512, bd128, ts, row0):
    bsz, s, _ = qkv.shape

    def body(p_ref, dq_ref, dk_ref, dv_ref, cos_ref, sin_ref, qnw_ref, knw_ref, bd512_ref, bd128_ref, d_ref, gw_ref):
        b, i = pl.program_id(0), pl.program_id(1)
        lo_mask = _lo_mask(ts)
        cos_t, sin_t = cos_ref[...], sin_ref[...]
        dqr = dq_ref[0]
        dqn = dqr * _tile_lanes(cos_t, 4) + _partner(dqr * _tile_lanes(sin_t, 4))
        dqp, gq = _norm_backward(dqn, p_ref[0, :, 0:512], qnw_ref[...], bd512_ref[...])
        dkr = _heads_to_lanes(dk_ref)
        dkn = dkr * cos_t + _partner(dkr * sin_t)
        dkp, gk = _norm_backward(dkn, p_ref[0, :, 512:640], knw_ref[...], bd128_ref[...])
        dvp = _heads_to_lanes(dv_ref)
        d_ref[0] = jnp.concatenate([dqp, dkp, dvp], axis=1).astype(BF16)
        gk512 = jnp.concatenate([jnp.sum(gk, axis=0, keepdims=True), jnp.zeros((1, 384), F32)], axis=1)
        rows = jnp.concatenate([jnp.sum(gq, axis=0, keepdims=True), gk512, jnp.zeros((6, 512), F32)], axis=0)
        first = jnp.logical_and(b == 0, i == 0)

        @pl.when(first)
        def _():
            gw_ref[...] = rows

        @pl.when(jnp.logical_not(first))
        def _():
            gw_ref[...] += rows

    const = lambda a: pl.BlockSpec(a.shape, lambda b, i: (0,) * a.ndim)
    kv_spec = pl.BlockSpec((1, 2, HEAD_DIM, ts), lambda b, i: (b, 0, 0, i + row0 // ts))
    return pl.pallas_call(
        body, name="qkv_backward", grid=(bsz, s // ts),
        in_specs=[pl.BlockSpec((1, ts, 768), lambda b, i: (b, i, 0)), pl.BlockSpec((1, ts, 512), lambda b, i: (b, i, 0)),
                  kv_spec, kv_spec, pl.BlockSpec((ts, 128), lambda b, i: (i, 0)),
                  pl.BlockSpec((ts, 128), lambda b, i: (i, 0)), const(qnw), const(knw), const(bd512), const(bd128)],
        out_specs=[pl.BlockSpec((1, ts, 768), lambda b, i: (b, i, 0)), pl.BlockSpec((8, 512), lambda b, i: (0, 0))],
        out_shape=[jax.ShapeDtypeStruct((bsz, s, 768), BF16), jax.ShapeDtypeStruct((8, 512), F32)],
        compiler_params=_params(2),
    )(qkv, dq, dk2, dv2, cos, sin, qnw, knw, bd512, bd128)


def _ctx_kv_backward(pc, dk2, dv2, knw, bd128):
    bsz, cl, _ = pc.shape

    def body(p_ref, dk_ref, dv_ref, knw_ref, bd128_ref, d_ref, gw_ref):
        b = pl.program_id(0)
        lo_mask = _lo_mask(cl)
        dkn = _heads_to_lanes(dk_ref)
        dkp, gk = _norm_backward(dkn, p_ref[0, :, 0:128], knw_ref[...], bd128_ref[...])
        dvp = _heads_to_lanes(dv_ref)
        d_ref[0] = jnp.concatenate([dkp, dvp], axis=1).astype(BF16)
        rows = jnp.concatenate([jnp.sum(gk, axis=0, keepdims=True), jnp.zeros((7, 128), F32)], axis=0)

        @pl.when(b == 0)
        def _():
            gw_ref[...] = rows

        @pl.when(b > 0)
        def _():
            gw_ref[...] += rows

    const = lambda a: pl.BlockSpec(a.shape, lambda b: (0,) * a.ndim)
    kv_spec = pl.BlockSpec((1, 2, HEAD_DIM, cl), lambda b: (b, 0, 0, 0))
    return pl.pallas_call(
        body, name="ctx_kv_backward", grid=(bsz,),
        in_specs=[pl.BlockSpec((1, cl, 256), lambda b: (b, 0, 0)), kv_spec, kv_spec, const(knw), const(bd128)],
        out_specs=[pl.BlockSpec((1, cl, 256), lambda b: (b, 0, 0)), pl.BlockSpec((8, 128), lambda b: (0, 0))],
        out_shape=[jax.ShapeDtypeStruct((bsz, cl, 256), BF16), jax.ShapeDtypeStruct((8, 128), F32)],
        compiler_params=_params(1),
    )(pc, dk2, dv2, knw, bd128)


def _weight_grad(parts, u, init, tm, name):
    bsz, s, d = u.shape
    n_p = len(parts)
    nrows = sum(hi - lo for _, lo, hi in parts)

    def body(*refs):
        p_refs, u_ref = refs[:n_p], refs[n_p]
        gi_ref = refs[n_p + 1] if init is not None else None
        gw_ref = refs[-1]
        first = jnp.logical_and(pl.program_id(0) == 0, pl.program_id(1) == 0)
        dp = jnp.concatenate([r[0, :, lo:hi] for r, (_, lo, hi) in zip(p_refs, parts)], axis=1)
        gw = lax.dot_general(dp, u_ref[0], TN, preferred_element_type=F32)

        @pl.when(first)
        def _():
            gw_ref[...] = gw
            if init is not None:
                gw_ref[KV_LO:KV_HI, :] += gi_ref[...]

        @pl.when(jnp.logical_not(first))
        def _():
            gw_ref[...] += gw

    tok = lambda w: pl.BlockSpec((1, tm, w), lambda b, i: (b, i, 0))
    in_specs = [tok(a.shape[2]) for a, _, _ in parts] + [tok(d)]
    args = [a for a, _, _ in parts] + [u]
    if init is not None:
        in_specs.append(pl.BlockSpec(init.shape, lambda b, i: (0, 0)))
        args.append(init)
    return pl.pallas_call(
        body, name=name, grid=(bsz, s // tm), in_specs=in_specs,
        out_specs=pl.BlockSpec((nrows, d), lambda b, i: (0, 0)), out_shape=jax.ShapeDtypeStruct((nrows, d), F32),
        compiler_params=_params(2),
    )(*args)


def _inproj_backward(dps, x, dh, scale1p, norm_w, w_t, tm, name, fused=None):
    bsz, s, d = x.shape
    n_p = len(dps)
    shared = scale1p.shape[0] == 1
    with_dx = dh is not None
    n_in = n_p + (2 if with_dx else 1) + 3
    n_out = 3 if with_dx else 2
    total = bsz * (s // tm)
    at_steps = [(0, True), (total // 8, True), ((3 * total) // 4, True), (total - 1, False)]

    def body(*refs):
        ins, outs, _ = _split_fused(refs, n_in, n_out, 0, fused)
        dp_refs, x_ref = ins[:n_p], ins[n_p]
        dh_ref = ins[n_p + 1] if with_dx else None
        sc_ref, nw_ref, w_ref = ins[-3:]
        mod_ref, gnw_ref = outs[-2:]
        b, i = pl.program_id(0), pl.program_id(1)
        step = b * (s // tm) + i
        _run_phases(fused, step, at_steps, True)
        first = jnp.logical_and(b == 0, i == 0)
        dp = dp_refs[0][0] if n_p == 1 else jnp.concatenate([r[0] for r in dp_refs], axis=1)
        du = jnp.dot(dp, w_ref[...], preferred_element_type=F32)
        xv = x_ref[0]
        rstd = lax.rsqrt(jnp.mean(xv * xv, axis=-1, keepdims=True) + EPS)
        xhat = xv * rstd
        nw, sc = nw_ref[...], sc_ref[0]
        red = lambda v: jnp.sum(v, axis=0, keepdims=True)
        mod_rows = jnp.concatenate([red(du), red(du * (xhat * nw)), jnp.zeros((6, d), F32)], axis=0)
        gnw_rows = jnp.concatenate([red(du * sc * xhat), jnp.zeros((7, d), F32)], axis=0)
        mod_first = first if shared else i == 0

        @pl.when(mod_first)
        def _():
            mod_ref[0] = mod_rows

        @pl.when(jnp.logical_not(mod_first))
        def _():
            mod_ref[0] += mod_rows

        @pl.when(first)
        def _():
            gnw_ref[...] = gnw_rows

        @pl.when(jnp.logical_not(first))
        def _():
            gnw_ref[...] += gnw_rows

        if with_dx:
            dxhat = du * (nw * sc)
            outs[0][0] = dh_ref[0] + rstd * (dxhat - xhat * jnp.mean(dxhat * xhat, axis=-1, keepdims=True))
        _run_phases(fused, step, at_steps, False)

    tok = lambda w: pl.BlockSpec((1, tm, w), lambda b, i: (b, i, 0))
    in_specs = [tok(p.shape[2]) for p in dps] + [tok(d)]
    args = list(dps) + [x]
    if with_dx:
        in_specs.append(tok(d))
        args.append(dh)
    in_specs += [_bcast_spec(scale1p), pl.BlockSpec((1, d), lambda b, i: (0, 0)),
                 pl.BlockSpec(w_t.shape, lambda b, i: (0, 0))]
    args += [scale1p, norm_w, w_t]
    bm = scale1p.shape[0]
    mod_spec = pl.BlockSpec((1, 8, d), (lambda b, i: (0, 0, 0)) if shared else (lambda b, i: (b, 0, 0)))
    out_specs = [mod_spec, pl.BlockSpec((8, d), lambda b, i: (0, 0))]
    out_shape = [jax.ShapeDtypeStruct((bm, 8, d), F32), jax.ShapeDtypeStruct((8, d), F32)]
    if with_dx:
        out_specs.insert(0, tok(d))
        out_shape.insert(0, jax.ShapeDtypeStruct((bsz, s, d), F32))
    scratch = []
    if fused is not None:
        in_specs += fused.in_specs
        args += fused.arrs
        out_specs += fused.out_specs
        out_shape += fused.out_shape
        scratch = fused.scratch
    res = pl.pallas_call(
        body, name=name, grid=(bsz, s // tm), in_specs=in_specs, out_specs=out_specs, out_shape=out_shape,
        scratch_shapes=scratch, compiler_params=_params(2),
    )(*args)
    return list(res) if with_dx else [None] + list(res)


def _adamw_update(w_ref, g_ref, m_ref, v_ref, d_ref, nm_ref, nv_ref):
    gv = g_ref[...]
    mn = ADAM_B1 * m_ref[...] + (1.0 - ADAM_B1) * gv
    vn = ADAM_B2 * v_ref[...] + (1.0 - ADAM_B2) * (gv * gv)
    m_hat = mn / (1.0 - ADAM_B1 ** ADAM_STEP)
    v_hat = vn / (1.0 - ADAM_B2 ** ADAM_STEP)
    d_ref[...] = -ADAM_LR * (m_hat / (jnp.sqrt(v_hat) + ADAM_EPS) + ADAM_WD * w_ref[...])
    nm_ref[...] = mn
    nv_ref[...] = vn


def _adamw_small(ws, gs, ms, vs):
    n = len(ws)

    def body(*refs):
        ins, outs = refs[:4 * n], refs[4 * n:]
        for k in range(n):
            _adamw_update(ins[k], ins[n + k], ins[2 * n + k], ins[3 * n + k], outs[3 * k], outs[3 * k + 1],
                          outs[3 * k + 2])

    res = pl.pallas_call(
        body, name="adamw_small",
        out_shape=[jax.ShapeDtypeStruct(w.shape, F32) for w in ws for _ in range(3)], compiler_params=_params(),
    )(*ws, *gs, *ms, *vs)
    return [tuple(res[3 * k:3 * k + 3]) for k in range(n)]


def _adamw(w, g, m, v, name):
    r, cdim = w.shape
    tr = next((t for t in (256, 176) if r % t == 0 and r > t), r)

    def body(*refs):
        _adamw_update(*refs)

    spec = pl.BlockSpec((tr, cdim), lambda i: (i, 0))
    return pl.pallas_call(
        body, name=name, grid=(r // tr,), in_specs=[spec] * 4, out_specs=[spec] * 3,
        out_shape=[jax.ShapeDtypeStruct((r, cdim), F32)] * 3, compiler_params=_params(1),
    )(w, g, m, v)


def _rope_tables(s):
    rows = s // GRID_W
    freqs = np.float32(ROPE_THETA) ** (-np.arange(0, ROPE_AXIS_DIM, 2, dtype=np.float32) / np.float32(ROPE_AXIS_DIM))
    ang_r = np.arange(rows, dtype=np.float32)[:, None] * freqs[None, :]
    ang_c = np.arange(GRID_W, dtype=np.float32)[:, None] * freqs[None, :]
    zr, zc = np.zeros_like(ang_r), np.zeros_like(ang_c)

    def table(by_row, by_col):
        r = jnp.asarray(np.tile(np.concatenate(by_row + [zr, zr], axis=1), (1, 2)), dtype=F32)
        c = jnp.asarray(np.tile(np.concatenate([zc, zc] + by_col, axis=1), (1, 2)), dtype=F32)
        return jnp.repeat(r, GRID_W, axis=0) + jnp.tile(c, (rows, 1))

    return (table([np.cos(ang_r)] * 2, [np.cos(ang_c)] * 2),
            table([-np.sin(ang_r), np.sin(ang_r)], [-np.sin(ang_c), np.sin(ang_c)]))


def _pack_rows(parts, rows):
    flat = jnp.concatenate([p.reshape(-1) for p in parts])
    return jnp.pad(flat, (0, rows * D_MODEL - flat.shape[0])).reshape(rows, D_MODEL)


def kernel(x, c, ctx, c_ctx, w_mod, b_mod, norm_w, w_in, q_norm_w, k_norm_w, conv_w, conv_b, conv_ln_w, conv_ln_b, w_pw, b_pw, w_out, loss_target, m_c_ctx, m_w_mod, m_b_mod, m_norm_w, m_w_in, m_q_norm_w, m_k_norm_w, m_conv_w, m_conv_b, m_conv_ln_w, m_conv_ln_b, m_w_pw, m_b_pw, m_w_out, v_c_ctx, v_w_mod, v_b_mod, v_norm_w, v_w_in, v_q_norm_w, v_k_norm_w, v_conv_w, v_conv_b, v_conv_ln_w, v_conv_ln_b, v_w_pw, v_b_pw, v_w_out):
    bsz, s, d = x.shape
    cl = ctx.shape[1]
    xi, yi, ci = lax.axis_index("x"), lax.axis_index("y"), lax.axis_index("c")
    chip = 2 * xi + yi
    dev = 2 * chip + ci
    ncol_mod = w_mod.shape[2]

    w_in_t_loc = w_in[0].T.astype(BF16)
    b_cols = lax.dynamic_slice(b_mod, (0, chip * ncol_mod), (1, ncol_mod))
    sc_rows, mod_g, g_in = _front(jnp.pad(c, ((0, 8 - bsz), (0, 0))), jnp.pad(c_ctx[None, :], ((0, 15), (0, 0))),
                                  w_mod[0], b_cols, w_in_t_loc)
    w_in_t = g_in.reshape(D_IN, d)
    mod_all = mod_g.transpose(1, 0, 2).reshape(80, 3 * d)
    mod_loc = lax.dynamic_slice(mod_all, (8 * dev, 0), (bsz, 3 * d))
    shift, scale1p, gate = mod_loc[:, None, :d], 1.0 + mod_loc[:, None, d:2 * d], mod_loc[:, None, 2 * d:]
    shift_c, scale1p_c = mod_all[64:65, :d][None], 1.0 + mod_all[64:65, d:2 * d][None]

    cos, sin = _rope_tables(s)
    qnw512 = jnp.tile(q_norm_w, (1, 8))
    knw128 = jnp.tile(k_norm_w, (1, 2))
    bd512 = jnp.kron(jnp.eye(8, dtype=F32), jnp.ones((HEAD_DIM, HEAD_DIM), F32)).astype(BF16)
    bd128 = bd512[:128, :128]

    u, p_qkv, p_za, p_ga, p_gg, p_zc = _norm_inproj(x, shift, scale1p, norm_w, w_in_t, SPLITS, 512, "norm_inproj")
    uc, pc_kv = _norm_inproj(ctx, shift_c, scale1p_c, norm_w, w_in_t[KV_LO:KV_HI], ((0, 256),), cl, "ctx_norm_inproj")
    q, k2x, v2x, ktx, vtx = _qkv_prep(p_qkv, cos, sin, qnw512, knw128, bd512, bd128, 256, cl)
    k2, v2, kt, vt = _ctx_kv_prep(pc_kv, knw128, bd128, k2x, v2x, ktx, vtx)
    conv_w_loc = jnp.pad(conv_w[0], ((0, 1), (0, 0)))
    o, lse, g_out, g_pw, g_cw = _attn_forward(
        q, k2, vt, min(1024, s), _ChipGather([w_out[0].astype(BF16), w_pw[0].astype(BF16), conv_w_loc]))
    w_out_f = g_out.reshape(d, d)
    w_pw_f = g_pw.reshape(D_CONV, D_CONV)
    conv_w_f = g_cw.transpose(1, 0, 2).reshape(32, D_CONV)
    y, cv = _conv_forward(p_ga, p_gg, conv_w_f, conv_b, conv_ln_w, conv_ln_b, w_pw_f, b_pw, 512)
    loss_part, dh, do, dza, dcv, dzc, dgate, gw_out = _outproj_loss(
        x, loss_target, gate, o, p_za, cv, p_zc, w_out_f, 512)

    all_chips, half_rows = (0, 1, 2, 3), D_IN // 2
    dy, gw_pw, conv_stats = _conv_token_backward(dcv, y, conv_ln_w, conv_ln_b, w_pw_f, 512)
    da, dg, gcw = _conv_backward(dy, p_ga, p_gg, conv_w_f, 512)
    tw = min(1024, s)
    gw_hi = _weight_grad([(da, half_rows - SPLITS[2][0], 512), (dg, 0, 512), (dzc, 0, 512)], u, None, tw,
                         "grad_in_rows_hi")
    dq, dkt, dvt, r_out, r_pw, r_hi = _attn_backward(
        q, k2, v2, kt, o, do, lse, 512, _FusedReduce([(gw_out, all_chips), (gw_pw, all_chips), (gw_hi, (2, 3))]))
    dqkv, qk_stats = _qkv_backward(p_qkv, dq, dkt, dvt, cos, sin, qnw512, knw128, bd512, bd128, 256, cl)
    dpc, kc_stats = _ctx_kv_backward(pc_kv, dkt, dvt, knw128, bd128)
    gw_ctx = _weight_grad([(dpc, 0, 256)], uc, None, cl, "grad_in_rows_ctx")
    gw_lo = _weight_grad([(dqkv, 0, 768), (dza, 0, 512), (da, 0, half_rows - SPLITS[2][0])], u, gw_ctx, tw,
                         "grad_in_rows_lo")
    _, modc, gnw_c = _inproj_backward([dpc], ctx, None, scale1p_c, norm_w, w_in_t[KV_LO:KV_HI], cl,
                                      "ctx_inproj_backward")
    grad_x, modx, gnw_x, r_lo = _inproj_backward(
        [dqkv, dza, da, dg, dzc], x, dh, scale1p, norm_w, w_in_t, 512, "inproj_backward",
        _FusedReduce([(gw_lo, (0, 1))]))
    g_w_out, g_w_pw = r_out.reshape(d // 4, d), r_pw.reshape(D_CONV // 4, D_CONV)
    g_w_in_t = jnp.where(chip < 2, r_lo, r_hi).reshape(D_IN // 4, d)

    dmod_loc = jnp.concatenate([modx[:, 0, :], modx[:, 1, :], dgate[:, 0, :]], axis=1)
    gq = qk_stats[0].reshape(8, HEAD_DIM).sum(axis=0)
    gk = (qk_stats[1, :128] + kc_stats[0]).reshape(2, HEAD_DIM).sum(axis=0)
    packed = _pack_rows([dmod_loc, dmod_loc.sum(axis=0), gnw_x[0] + gnw_c[0], modc[0, 0], modc[0, 1], gq, gk,
                         conv_stats[0], conv_stats[1], conv_stats[2], conv_stats[3], gcw,
                         jnp.sum(loss_part[:, 0, 0])[None]], 32)
    total, g_w_mod, dsilu_ctx = _tail_exchange(packed, sc_rows, w_mod[0], bsz, 3 * bsz + 4)
    flat = total.reshape(-1)
    offs = [0]

    def take(nelem):
        lo = offs[0]
        offs[0] = lo + nelem
        return flat[lo:lo + nelem]

    take(bsz * 3 * d)
    g_b_mod_x = take(3 * d)
    g_norm_w = take(d)
    dshift_c, dscale_c = take(d), take(d)
    g_qnw, g_knw = take(HEAD_DIM), take(HEAD_DIM)
    g_b_pw, g_ln_w, g_ln_b, g_conv_b = take(D_CONV), take(D_CONV), take(D_CONV), take(D_CONV)
    g_conv_w_full = take(32 * D_CONV).reshape(32, D_CONV)
    loss = take(1)[0] * (0.5 / d)

    dmod_c = jnp.concatenate([dshift_c, dscale_c, jnp.zeros((d,), F32)])
    g_b_mod = (g_b_mod_x + dmod_c)[None, :]
    sg = _sigmoid(c_ctx)
    g_c_ctx = dsilu_ctx[0] * (sg * (1.0 + c_ctx * (1.0 - sg)))

    g_w_in = g_w_in_t.T
    g_conv_w = lax.dynamic_slice(g_conv_w_full, (0, chip * 128), (CONV_WIDTH, 128))

    grads = {
        "c_ctx": g_c_ctx, "w_mod": g_w_mod[None], "b_mod": g_b_mod, "norm_w": g_norm_w[None], "w_in": g_w_in[None],
        "q_norm_w": g_qnw[None], "k_norm_w": g_knw[None], "conv_w": g_conv_w[None], "conv_b": g_conv_b[None],
        "conv_ln_w": g_ln_w[None], "conv_ln_b": g_ln_b[None], "w_pw": g_w_pw[None], "b_pw": g_b_pw[None],
        "w_out": g_w_out[None],
    }
    weights = {
        "c_ctx": (c_ctx, m_c_ctx, v_c_ctx), "w_mod": (w_mod, m_w_mod, v_w_mod), "b_mod": (b_mod, m_b_mod, v_b_mod),
        "norm_w": (norm_w, m_norm_w, v_norm_w), "w_in": (w_in, m_w_in, v_w_in),
        "q_norm_w": (q_norm_w, m_q_norm_w, v_q_norm_w), "k_norm_w": (k_norm_w, m_k_norm_w, v_k_norm_w),
        "conv_w": (conv_w, m_conv_w, v_conv_w), "conv_b": (conv_b, m_conv_b, v_conv_b),
        "conv_ln_w": (conv_ln_w, m_conv_ln_w, v_conv_ln_w), "conv_ln_b": (conv_ln_b, m_conv_ln_b, v_conv_ln_b),
        "w_pw": (w_pw, m_w_pw, v_w_pw), "b_pw": (b_pw, m_b_pw, v_b_pw), "w_out": (w_out, m_w_out, v_w_out),
    }
    names = list(weights)
    big = ("w_mod", "w_in", "w_out")
    as_2d = lambda a: a.reshape((1, a.shape[0]) if a.ndim == 1 else (a.shape[-2] if a.ndim == 3 else 1, a.shape[-1]))
    small = [n for n in names if n not in big]
    w_g_m_v = zip(*[[as_2d(a) for a in (weights[n][0], grads[n], weights[n][1], weights[n][2])] for n in small])
    updates = dict(zip(small, _adamw_small(*[list(col) for col in w_g_m_v])))
    for n in ("w_mod", "w_out"):
        w, m, v = weights[n]
        updates[n] = _adamw(as_2d(w), as_2d(grads[n]), as_2d(m), as_2d(v), "adamw_" + n)
    w, m, v = weights["w_in"]
    updates["w_in"] = tuple(r.T for r in _adamw(w[0].T, g_w_in_t, m[0].T, v[0].T, "adamw_w_in"))
    deltas, new_ms, new_vs = ([updates[n][k].reshape(weights[n][0].shape) for n in names] for k in range(3))
    grads = {n: grads[n].reshape(weights[n][0].shape) for n in names}

    return (loss, grad_x, *[grads[n] for n in names], *deltas, *new_ms, *new_vs)
```

```python
import functools
import math

import jax
import jax.numpy as jnp
import numpy as np
from jax import lax
from jax.experimental import pallas as pl
from jax.experimental.pallas import tpu as pltpu

F32 = jnp.float32
BF16 = jnp.bfloat16
MESH = pl.DeviceIdType.MESH

D_MODEL = 1024
D_ATTN = 512
D_CONV = 512
HEAD_DIM = 64
N_KV = 2
GRID_W = 64
ROPE_AXIS_DIM = 32
ROPE_THETA = 10000.0
CONV_WIDTH = 31
CONV_PAD = 15
HALO = 16
CONV_ROWS = 32
EPS = 1e-6
SPLITS = ((0, 768), (768, 1280), (1280, 1792), (1792, 2304), (2304, 2816))
D_IN = 2816
KV_LO, KV_HI = 512, 768

ADAM_LR = 0.001
ADAM_B1 = 0.9
ADAM_B2 = 0.999
ADAM_EPS = 1e-08
ADAM_WD = 0.01
ADAM_STEP = 10

VMEM_LIMIT = 56 * 1024 * 1024

NT = (((1,), (1,)), ((), ()))
TN = (((0,), (0,)), ((), ()))


def _params(n_axes=0, **kw):
    if n_axes:
        kw["dimension_semantics"] = ("arbitrary",) * n_axes
    return pltpu.CompilerParams(vmem_limit_bytes=VMEM_LIMIT, **kw)


def _sigmoid(x):
    return 1.0 / (1.0 + jnp.exp(-x))


def _silu_and_grad(z):
    s = _sigmoid(z)
    return z * s, s * (1.0 + z * (1.0 - s))


def _seg_mean(v, ones_bd):
    hi = v.astype(BF16)
    lo = (v - hi.astype(F32)).astype(BF16)
    s = jnp.dot(hi, ones_bd, preferred_element_type=F32) + jnp.dot(lo, ones_bd, preferred_element_type=F32)
    return s * (1.0 / HEAD_DIM)


def _partner(v):
    n = v.shape[1]
    lane = lax.broadcasted_iota(jnp.int32, (v.shape[0], 128), 1)
    first = (lane % 32) < 16
    parts = []
    for k in range(n // 128):
        ch = v[:, 128 * k:128 * (k + 1)]
        parts.append(jnp.where(first, pltpu.roll(ch, 112, 1), pltpu.roll(ch, 16, 1)))
    return parts[0] if len(parts) == 1 else jnp.concatenate(parts, axis=1)


def _tile_lanes(t, reps):
    return t if reps == 1 else jnp.concatenate([t] * reps, axis=1)


def _lo_mask(rows):
    return lax.broadcasted_iota(jnp.int32, (rows, 128), 1) < HEAD_DIM


def _gather8_in_vmem(x_ref, out_ref, send_sems, recv_sems, local_sem):
    x, y, c = lax.axis_index("x"), lax.axis_index("y"), lax.axis_index("c")
    me, sibling = (x, y, c), (x, y, 1 - c)
    chips = [(1 - x, y), (x, 1 - y), (1 - x, 1 - y)]

    def slot(px, py, pc):
        return out_ref.at[4 * px + 2 * py + pc]

    def copy(k, block, to, src=None):
        return pltpu.make_async_remote_copy(
            src_ref=slot(*block) if src is None else src, dst_ref=slot(*block),
            send_sem=send_sems.at[k], recv_sem=recv_sems.at[k], device_id=to, device_id_type=MESH)

    mine = pltpu.make_async_copy(x_ref, slot(*me), local_sem)
    mine.start()
    first = [copy(0, me, sibling, src=x_ref)]
    first += [copy(1 + j, me, (*chip, c), src=x_ref) for j, chip in enumerate(chips)]
    for cp in first:
        cp.start()
    passed = [copy(4 + j, (*chip, c), sibling) for j, chip in enumerate(chips)]
    for j, chip in enumerate(chips):
        copy(1 + j, (*chip, c), me).wait_recv()
        passed[j].start()
    copy(0, sibling, me).wait_recv()
    for j, chip in enumerate(chips):
        copy(4 + j, (*chip, 1 - c), me).wait_recv()
    for cp in first + passed:
        cp.wait_send()
    mine.wait()


class _ChipGather:
    def __init__(self, arrs):
        self.arrs = list(arrs)
        n = self.n = len(self.arrs)
        self.in_specs = [pl.BlockSpec(memory_space=pl.ANY)] * n
        self.out_shape = [jax.ShapeDtypeStruct((4,) + a.shape, a.dtype) for a in self.arrs]
        self.out_specs = [pl.BlockSpec(memory_space=pl.ANY)] * n
        self.scratch = [pltpu.SemaphoreType.DMA((6 * n,)), pltpu.SemaphoreType.DMA((6 * n,)),
                        pltpu.SemaphoreType.DMA((n,))]
        self.phases = [self.start, self.forward, self.finish]

    def bind(self, ins, outs, scratch):
        self.ins, self.outs = ins, outs
        self.send_sems, self.recv_sems, self.local_sems = scratch
        self.x, self.y, self.c = lax.axis_index("x"), lax.axis_index("y"), lax.axis_index("c")
        self.chips = [(1 - self.x, self.y), (self.x, 1 - self.y), (1 - self.x, 1 - self.y)]
        self.mychip = 2 * self.x + self.y

    def _copy(self, a, k, chip_idx, cc, to, src=None):
        h = self.arrs[a].shape[0] // 2
        dst = self.outs[a].at[chip_idx, pl.ds(cc * h, h)]
        return pltpu.make_async_remote_copy(
            src_ref=dst if src is None else src, dst_ref=dst, send_sem=self.send_sems.at[6 * a + k],
            recv_sem=self.recv_sems.at[6 * a + k], device_id=to, device_id_type=MESH)

    def _local(self, a):
        return pltpu.make_async_copy(self.ins[a], self.outs[a].at[self.mychip], self.local_sems.at[a])

    def _first(self, a, j):
        h = self.arrs[a].shape[0] // 2
        return self._copy(a, j, self.mychip, self.c, (*self.chips[j], self.c), src=self.ins[a].at[pl.ds(self.c * h, h)])

    def _passed(self, a, j):
        cx, cy = self.chips[j]
        return self._copy(a, 3 + j, 2 * cx + cy, self.c, (self.x, self.y, 1 - self.c))

    def start(self):
        for a in range(self.n):
            self._local(a).start()
            for j in range(3):
                self._first(a, j).start()

    def forward(self):
        for a in range(self.n):
            for j, (cx, cy) in enumerate(self.chips):
                self._copy(a, j, 2 * cx + cy, self.c, (self.x, self.y, self.c)).wait_recv()
                self._passed(a, j).start()

    def finish(self):
        for a in range(self.n):
            for j, (cx, cy) in enumerate(self.chips):
                self._copy(a, 3 + j, 2 * cx + cy, 1 - self.c, (self.x, self.y, self.c)).wait_recv()
        for a in range(self.n):
            for j in range(3):
                self._first(a, j).wait_send()
                self._passed(a, j).wait_send()
            self._local(a).wait()


class _FusedReduce:
    def __init__(self, pieces):
        self.owners = [tuple(o) for _, o in pieces]
        self.arrs = [g.reshape(len(o), 2, g.shape[0] // (2 * len(o)), g.shape[1]) for g, o in pieces]
        n = self.n = len(pieces)
        hc = self.hc = [(v.shape[2], v.shape[3]) for v in self.arrs]
        nts = [len(o) for o in self.owners]
        self.base = [sum(nts[:p]) for p in range(n)]
        anyspec = pl.BlockSpec(memory_space=pl.ANY)
        self.in_specs = [anyspec] * n
        self.out_shape = [jax.ShapeDtypeStruct((2,) + s, F32) for s in hc]
        self.out_specs = [anyspec] * n
        self.scratch = [pltpu.VMEM((nt,) + s, F32) for nt, s in zip(nts, hc)]
        self.scratch += [pltpu.VMEM((nt,) + s, F32) for nt, s in zip(nts, hc)]
        self.scratch += [pltpu.VMEM(s, F32) for s in hc]
        self.scratch += [pltpu.VMEM((nt,) + s, BF16) for nt, s in zip(nts, hc)]
        self.scratch += [pltpu.VMEM((3,) + s, BF16) for s in hc]
        self.scratch += [pltpu.VMEM(s, F32) for s in hc]
        tot = sum(nts)
        self.scratch += [pltpu.SemaphoreType.DMA((tot,)), pltpu.SemaphoreType.DMA((tot,)),
                         pltpu.SemaphoreType.DMA((tot,)), pltpu.SemaphoreType.DMA((3 * n,)),
                         pltpu.SemaphoreType.DMA((n,)), pltpu.SemaphoreType.DMA((n,)), pltpu.SemaphoreType.DMA((n,)),
                         pltpu.SemaphoreType.DMA((tot,))]
        self.phases = [self.start, self.exchange, self.combine, self.finish]

    def bind(self, ins, outs, scratch):
        n = self.n
        self.g, self.out = ins, outs
        self.va, self.recv_a, self.own = scratch[:n], scratch[n:2 * n], scratch[2 * n:3 * n]
        self.tsend, self.recv_b, self.fin = scratch[3 * n:4 * n], scratch[4 * n:5 * n], scratch[5 * n:6 * n]
        self.sa, self.ra, self.sb, self.rb, self.sc, self.rc, self.lc, self.la = scratch[6 * n:]
        self.x, self.y, self.c = lax.axis_index("x"), lax.axis_index("y"), lax.axis_index("c")
        self.mychip = 2 * self.x + self.y
        self.sibling = (self.x, self.y, 1 - self.c)

    def _copy_a(self, p, t):
        k = self.base[p] + t
        return pltpu.make_async_remote_copy(
            src_ref=self.g[p].at[t, 1 - self.c], dst_ref=self.recv_a[p].at[t], send_sem=self.sa.at[k],
            recv_sem=self.ra.at[k], device_id=self.sibling, device_id_type=MESH)

    def _fetch(self, p, t):
        return pltpu.make_async_copy(self.g[p].at[t, self.c], self.va[p].at[t], self.la.at[self.base[p] + t])

    def _slot(self, owner):
        rel = jnp.bitwise_xor(self.mychip, owner)
        return jnp.where(rel == 2, 0, jnp.where(rel == 1, 1, 2))

    def _copy_b(self, p, t, slot):
        owner = self.owners[p][t]
        return pltpu.make_async_remote_copy(
            src_ref=self.tsend[p].at[t], dst_ref=self.recv_b[p].at[slot], send_sem=self.sb.at[self.base[p] + t],
            recv_sem=self.rb.at[3 * p + slot], device_id=(owner // 2, owner % 2, self.c), device_id_type=MESH)

    def _copy_c(self, p, half):
        return pltpu.make_async_remote_copy(
            src_ref=self.fin[p], dst_ref=self.out[p].at[half], send_sem=self.sc.at[p], recv_sem=self.rc.at[p],
            device_id=self.sibling, device_id_type=MESH)

    def _local_c(self, p):
        return pltpu.make_async_copy(self.fin[p], self.out[p].at[self.c], self.lc.at[p])

    def start(self):
        for p in range(self.n):
            for t in range(len(self.owners[p])):
                self._copy_a(p, t).start()
                self._fetch(p, t).start()

    def exchange(self):
        for p in range(self.n):
            for t, owner in enumerate(self.owners[p]):
                self._copy_a(p, t).wait_recv()
                self._fetch(p, t).wait()
                mine = self.mychip == owner

                @pl.when(mine)
                def _():
                    self.own[p][...] = self.va[p][t] + self.recv_a[p][t]

                @pl.when(jnp.logical_not(mine))
                def _():
                    self.tsend[p][t] = (self.va[p][t] + self.recv_a[p][t]).astype(BF16)
                    self._copy_b(p, t, self._slot(owner)).start()

    def combine(self):
        for p in range(self.n):
            for t, owner in enumerate(self.owners[p]):
                @pl.when(self.mychip == owner)
                def _():
                    acc = self.own[p][...]
                    for j in range(3):
                        self._copy_b(p, t, j).wait_recv()
                        acc = acc + self.recv_b[p][j].astype(F32)
                    self.fin[p][...] = acc
                    self._local_c(p).start()
                    self._copy_c(p, self.c).start()

    def finish(self):
        for p in range(self.n):
            for t, owner in enumerate(self.owners[p]):
                self._copy_a(p, t).wait_send()
                mine = self.mychip == owner

                @pl.when(mine)
                def _():
                    self._copy_c(p, 1 - self.c).wait_recv()
                    self._copy_c(p, self.c).wait_send()
                    self._local_c(p).wait()

                @pl.when(jnp.logical_not(mine))
                def _():
                    self._copy_b(p, t, self._slot(owner)).wait_send()


def _split_fused(refs, n_in, n_out, n_scr, fused):
    if fused is None:
        return refs[:n_in], refs[n_in:n_in + n_out], refs[n_in + n_out:]
    fi, fo = len(fused.in_specs), len(fused.out_specs)
    ins, rest = refs[:n_in], refs[n_in:]
    f_ins, rest = rest[:fi], rest[fi:]
    outs, rest = rest[:n_out], rest[n_out:]
    f_outs, rest = rest[:fo], rest[fo:]
    scr, f_scr = rest[:n_scr], rest[n_scr:]
    fused.bind(f_ins, f_outs, f_scr)
    return ins, outs, scr


def _run_phases(fused, step, at_steps, before):
    if fused is None:
        return
    for phase, (at, first) in zip(fused.phases, at_steps):
        if first == before:
            pl.when(step == at)(phase)


def _front(c_pad, c_ctx_rows, w_mod, b_cols, w_in_t_loc):
    ncol = w_mod.shape[1]
    gather = _ChipGather([w_in_t_loc])

    def body(c_ref, cctx_ref, w_ref, b_ref, win_ref, sc_ref, modg_ref, wing_ref,
             call_ref, ag_send, ag_recv, ag_local, m_send, m_recv, *g_scr):
        gather.bind([win_ref], [wing_ref], g_scr)
        _gather8_in_vmem(c_ref, call_ref, ag_send, ag_recv, ag_local)
        gather.start()
        x, y, c = lax.axis_index("x"), lax.axis_index("y"), lax.axis_index("c")
        chips = [(1 - x, y), (x, 1 - y), (1 - x, 1 - y)]
        mychip = 2 * x + y
        rows = jnp.concatenate([call_ref[dv] for dv in range(8)] + [cctx_ref[...]], axis=0)
        sc = rows * _sigmoid(rows)
        sc_ref[...] = sc
        modg_ref[mychip] = jnp.dot(sc, w_ref[...], preferred_element_type=F32,
                                   precision=lax.Precision.HIGHEST) + b_ref[...]

        def mcopy(j, chip_idx, to):
            return pltpu.make_async_remote_copy(
                src_ref=modg_ref.at[chip_idx], dst_ref=modg_ref.at[chip_idx], send_sem=m_send.at[j],
                recv_sem=m_recv.at[j], device_id=to, device_id_type=MESH)

        sends = [mcopy(j, mychip, (*chip, c)) for j, chip in enumerate(chips)]
        for cp in sends:
            cp.start()
        for j, (cx, cy) in enumerate(chips):
            mcopy(j, 2 * cx + cy, (x, y, c)).wait_recv()
        gather.forward()
        gather.finish()
        for cp in sends:
            cp.wait_send()

    vm = pl.BlockSpec(memory_space=pltpu.VMEM)
    return pl.pallas_call(
        body, name="front_exchange",
        out_shape=[jax.ShapeDtypeStruct((80, D_MODEL), F32), jax.ShapeDtypeStruct((4, 80, ncol), F32)] + gather.out_shape,
        in_specs=[vm, vm, vm, vm] + gather.in_specs, out_specs=[vm, vm] + gather.out_specs,
        scratch_shapes=[pltpu.VMEM((8, 8, D_MODEL), F32), pltpu.SemaphoreType.DMA((7,)), pltpu.SemaphoreType.DMA((7,)),
                        pltpu.SemaphoreType.DMA, pltpu.SemaphoreType.DMA((3,)), pltpu.SemaphoreType.DMA((3,))]
        + gather.scratch,
        compiler_params=_params(),
    )(c_pad, c_ctx_rows, w_mod, b_cols, w_in_t_loc)


def _tail_exchange(packed, sc_rows, w_mod, bsz, ctx_row):
    d = D_MODEL
    ncol = w_mod.shape[1]

    def body(p_ref, sc_ref, w_ref, total_ref, gw_ref, gcc_ref, gat_ref, dm_ref, part_ref,
             ag_send, ag_recv, ag_local, g_send, g_recv):
        _gather8_in_vmem(p_ref, gat_ref, ag_send, ag_recv, ag_local)
        acc = gat_ref[0]
        for dv in range(1, 8):
            acc = acc + gat_ref[dv]
        total_ref[...] = acc
        x, y, c = lax.axis_index("x"), lax.axis_index("y"), lax.axis_index("c")
        chips = [(1 - x, y), (x, 1 - y), (1 - x, 1 - y)]
        mychip = 2 * x + y
        dm_ref[...] = jnp.zeros(dm_ref.shape, F32)
        for k in range(4):
            @pl.when(mychip == k)
            def _():
                spans = [(seg, max(k * ncol, seg * d) - seg * d, min((k + 1) * ncol, (seg + 1) * d) - seg * d)
                         for seg in range(3) if k * ncol < (seg + 1) * d and (k + 1) * ncol > seg * d]
                for dv in range(8):
                    for b in range(bsz):
                        dm_ref[8 * dv + b:8 * dv + b + 1, :] = jnp.concatenate(
                            [gat_ref[dv, 3 * b + seg:3 * b + seg + 1, lo:hi] for seg, lo, hi in spans], axis=1)
                dm_ref[64:65, :] = jnp.concatenate(
                    [total_ref[ctx_row + seg:ctx_row + seg + 1, lo:hi] if seg < 2 else jnp.zeros((1, hi - lo), F32)
                     for seg, lo, hi in spans], axis=1)

        dm = dm_ref[...]
        gw_ref[...] = lax.dot_general(sc_ref[...], dm, TN, preferred_element_type=F32,
                                      precision=lax.Precision.HIGHEST)
        part_ref[mychip] = lax.dot_general(dm[64:72, :], w_ref[...], NT, preferred_element_type=F32,
                                           precision=lax.Precision.HIGHEST)

        def gcopy(j, chip_idx, to):
            return pltpu.make_async_remote_copy(
                src_ref=part_ref.at[chip_idx], dst_ref=part_ref.at[chip_idx], send_sem=g_send.at[j],
                recv_sem=g_recv.at[j], device_id=to, device_id_type=MESH)

        sends = [gcopy(j, mychip, (*chip, c)) for j, chip in enumerate(chips)]
        for cp in sends:
            cp.start()
        for j, (cx, cy) in enumerate(chips):
            gcopy(j, 2 * cx + cy, (x, y, c)).wait_recv()
        for cp in sends:
            cp.wait_send()
        gcc_ref[...] = (part_ref[0] + part_ref[1]) + (part_ref[2] + part_ref[3])

    return pl.pallas_call(
        body, name="tail_exchange",
        out_shape=[jax.ShapeDtypeStruct(packed.shape, F32), jax.ShapeDtypeStruct((d, ncol), F32),
                   jax.ShapeDtypeStruct((8, d), F32)],
        scratch_shapes=[pltpu.VMEM((8,) + packed.shape, F32), pltpu.VMEM((80, ncol), F32), pltpu.VMEM((4, 8, d), F32),
                        pltpu.SemaphoreType.DMA((7,)), pltpu.SemaphoreType.DMA((7,)), pltpu.SemaphoreType.DMA,
                        pltpu.SemaphoreType.DMA((3,)), pltpu.SemaphoreType.DMA((3,))],
        compiler_params=_params(),
    )(packed, sc_rows, w_mod)


def _bcast_spec(arr):
    if arr.shape[0] == 1:
        return pl.BlockSpec((1, 1, arr.shape[2]), lambda b, i: (0, 0, 0))
    return pl.BlockSpec((1, 1, arr.shape[2]), lambda b, i: (b, 0, 0))


def _norm_inproj(x, shift, scale1p, norm_w, w_t, splits, tm, name):
    bsz, s, d = x.shape

    def body(x_ref, sh_ref, sc_ref, nw_ref, w_ref, u_ref, *out_refs):
        xv = x_ref[0]
        rstd = lax.rsqrt(jnp.mean(xv * xv, axis=-1, keepdims=True) + EPS)
        u = (xv * rstd * nw_ref[...]) * sc_ref[0] + sh_ref[0]
        ub = u.astype(BF16)
        u_ref[0] = ub
        for (lo, hi), o_ref in zip(splits, out_refs):
            o_ref[0] = lax.dot_general(ub, w_ref[lo:hi, :], NT, preferred_element_type=F32)

    tok = lambda w: pl.BlockSpec((1, tm, w), lambda b, i: (b, i, 0))
    return pl.pallas_call(
        body, name=name, grid=(bsz, s // tm),
        in_specs=[tok(d), _bcast_spec(shift), _bcast_spec(scale1p), pl.BlockSpec((1, d), lambda b, i: (0, 0)),
                  pl.BlockSpec(w_t.shape, lambda b, i: (0, 0))],
        out_specs=[tok(d)] + [tok(hi - lo) for lo, hi in splits],
        out_shape=[jax.ShapeDtypeStruct((bsz, s, d), BF16)]
        + [jax.ShapeDtypeStruct((bsz, s, hi - lo), F32) for lo, hi in splits],
        compiler_params=_params(2),
    )(x, shift, scale1p, norm_w, w_t)


def _dup_heads(kv, lo_mask):
    r = pltpu.roll(kv, HEAD_DIM, 1)
    return jnp.where(lo_mask, kv, r), jnp.where(lo_mask, r, kv)


def _qkv_prep(qkv, cos, sin, qnw, knw, bd512, bd128, ts, extra):
    bsz, s, _ = qkv.shape

    def body(p_ref, cos_ref, sin_ref, qnw_ref, knw_ref, bd512_ref, bd128_ref, q_ref, k_ref, v_ref, kt_ref, vt_ref):
        lo_mask = _lo_mask(ts)
        cos_t, sin_t = cos_ref[...], sin_ref[...]
        qp = p_ref[0, :, 0:512]
        qn = qp * lax.rsqrt(_seg_mean(qp * qp, bd512_ref[...]) + EPS) * qnw_ref[...]
        qr = qn * _tile_lanes(cos_t, 4) + _partner(qn) * _tile_lanes(sin_t, 4)
        q_ref[0] = (qr * (1.0 / math.sqrt(HEAD_DIM))).astype(BF16)
        kp = p_ref[0, :, 512:640]
        kn = kp * lax.rsqrt(_seg_mean(kp * kp, bd128_ref[...]) + EPS) * knw_ref[...]
        kr = kn * cos_t + _partner(kn) * sin_t
        k0, k1 = _dup_heads(kr, lo_mask)
        k_ref[0, 0] = k0.astype(BF16)
        k_ref[0, 1] = k1.astype(BF16)
        vp = p_ref[0, :, 640:768]
        v0, v1 = _dup_heads(vp, lo_mask)
        v_ref[0, 0] = v0.astype(BF16)
        v_ref[0, 1] = v1.astype(BF16)
        kt_ref[0] = kr.T.astype(BF16)
        vt_ref[0] = vp.T.astype(BF16)

    const = lambda a: pl.BlockSpec(a.shape, lambda b, i: (0,) * a.ndim)
    kv_spec = pl.BlockSpec((1, 2, ts, 128), lambda b, i: (b, 0, i, 0))
    t_spec = pl.BlockSpec((1, 128, ts), lambda b, i: (b, 0, i))
    return pl.pallas_call(
        body, name="qkv_prep", grid=(bsz, s // ts),
        in_specs=[pl.BlockSpec((1, ts, 768), lambda b, i: (b, i, 0)),
                  pl.BlockSpec((ts, 128), lambda b, i: (i, 0)), pl.BlockSpec((ts, 128), lambda b, i: (i, 0)),
                  const(qnw), const(knw), const(bd512), const(bd128)],
        out_specs=[pl.BlockSpec((1, ts, 512), lambda b, i: (b, i, 0)), kv_spec, kv_spec, t_spec, t_spec],
        out_shape=[jax.ShapeDtypeStruct((bsz, s, 512), BF16), jax.ShapeDtypeStruct((bsz, 2, s + extra, 128), BF16),
                   jax.ShapeDtypeStruct((bsz, 2, s + extra, 128), BF16),
                   jax.ShapeDtypeStruct((bsz, 128, s + extra), BF16), jax.ShapeDtypeStruct((bsz, 128, s + extra), BF16)],
        compiler_params=_params(2),
    )(qkv, cos, sin, qnw, knw, bd512, bd128)


def _ctx_kv_prep(pc, knw, bd128, k2, v2, kt, vt):
    bsz, cl, _ = pc.shape
    blk = k2.shape[2] // cl - 1

    def body(p_ref, knw_ref, bd128_ref, k_in, v_in, kt_in, vt_in, k_ref, v_ref, kt_ref, vt_ref):
        lo_mask = _lo_mask(cl)
        kp = p_ref[0, :, 0:128]
        kn = kp * lax.rsqrt(_seg_mean(kp * kp, bd128_ref[...]) + EPS) * knw_ref[...]
        k0, k1 = _dup_heads(kn, lo_mask)
        k_ref[0, 0] = k0.astype(BF16)
        k_ref[0, 1] = k1.astype(BF16)
        vp = p_ref[0, :, 128:256]
        v0, v1 = _dup_heads(vp, lo_mask)
        v_ref[0, 0] = v0.astype(BF16)
        v_ref[0, 1] = v1.astype(BF16)
        kt_ref[0] = kn.T.astype(BF16)
        vt_ref[0] = vp.T.astype(BF16)

    const = lambda a: pl.BlockSpec(a.shape, lambda b: (0,) * a.ndim)
    kv_spec = pl.BlockSpec((1, 2, cl, 128), lambda b: (b, 0, blk, 0))
    t_spec = pl.BlockSpec((1, 128, cl), lambda b: (b, 0, blk))
    anyspec = pl.BlockSpec(memory_space=pl.ANY)
    return pl.pallas_call(
        body, name="ctx_kv_prep", grid=(bsz,),
        in_specs=[pl.BlockSpec((1, cl, 256), lambda b: (b, 0, 0)), const(knw), const(bd128)] + [anyspec] * 4,
        out_specs=[kv_spec, kv_spec, t_spec, t_spec],
        out_shape=[jax.ShapeDtypeStruct(a.shape, BF16) for a in (k2, v2, kt, vt)],
        input_output_aliases={3: 0, 4: 1, 5: 2, 6: 3},
        compiler_params=_params(1),
    )(pc, knw, bd128, k2, v2, kt, vt)


def _attn_forward(q, k2, vt, tq, fused):
    bsz, s, _ = q.shape
    sk = k2.shape[2]
    nq = s // tq
    total = bsz * N_KV * nq
    at_steps = [(0, True), (total // 4, True), (total - 1, False)]

    def body(*refs):
        (q_ref, k_ref, vt_ref), (o_ref, lse_ref), _ = _split_fused(refs, 3, 2, 0, fused)
        g = pl.program_id(1)
        step = (pl.program_id(0) * N_KV + g) * nq + pl.program_id(2)
        _run_phases(fused, step, at_steps, True)
        kk = k_ref[0, 0]
        lo_mask = _lo_mask(tq)
        vt_aug = jnp.concatenate([vt_ref[0, pl.ds(pl.multiple_of(g * HEAD_DIM, HEAD_DIM), HEAD_DIM), :],
                                  jnp.ones((16, sk), BF16)], axis=0)
        ps, ms = [], []
        for j in range(2):
            qp = q_ref[0, :, 128 * j:128 * (j + 1)]
            for half in range(2):
                sel = lo_mask if half == 0 else jnp.logical_not(lo_mask)
                qs = jnp.where(sel, qp, jnp.zeros_like(qp))
                sc = lax.dot_general(qs, kk, NT, preferred_element_type=F32)
                m = jnp.max(sc, axis=-1, keepdims=True)
                ps.append(jnp.exp(sc - m).astype(BF16))
                ms.append(m)
        ots = [lax.dot_general(vt_aug, p, NT, preferred_element_type=F32) for p in ps]
        for j in range(2):
            o_t, l_t = [], []
            for half in range(2):
                ot = ots[2 * j + half]
                l = ot[HEAD_DIM:HEAD_DIM + 1, :]
                o_t.append(ot[0:HEAD_DIM, :] / l)
                l_t.append(jnp.broadcast_to(l, (HEAD_DIM, tq)))
            o_ref[0, :, 128 * j:128 * (j + 1)] = jnp.concatenate(o_t, axis=0).T
            lse_ref[0, :, 128 * j:128 * (j + 1)] = (jnp.where(lo_mask, ms[2 * j], ms[2 * j + 1])
                                                    + jnp.log(jnp.concatenate(l_t, axis=0).T))
        _run_phases(fused, step, at_steps, False)

    q_spec = pl.BlockSpec((1, tq, 256), lambda b, g, i: (b, i, g))
    kv_spec = pl.BlockSpec((1, 1, sk, 128), lambda b, g, i: (b, g, 0, 0))
    return pl.pallas_call(
        body, name="attn_forward", grid=(bsz, N_KV, nq),
        in_specs=[q_spec, kv_spec, pl.BlockSpec((1, 128, sk), lambda b, g, i: (b, 0, 0))] + fused.in_specs,
        out_specs=[q_spec, q_spec] + fused.out_specs,
        out_shape=[jax.ShapeDtypeStruct((bsz, s, 512), F32)] * 2 + fused.out_shape,
        scratch_shapes=fused.scratch,
        compiler_params=_params(3),
    )(q, k2, vt, *fused.arrs)


def _halo_specs(width, ts, s):
    r = ts // HALO
    last = s // HALO - 1
    return [pl.BlockSpec((1, ts, width), lambda b, i: (b, i, 0)),
            pl.BlockSpec((1, HALO, width), lambda b, i: (b, jnp.maximum(i * r - 1, 0), 0)),
            pl.BlockSpec((1, HALO, width), lambda b, i: (b, jnp.minimum((i + 1) * r, last), 0))]


def _fill_ext(ext_ref, cur, prev, nxt, i, n_tiles, ts):
    ext_ref[0:HALO, :] = jnp.where(i > 0, prev, jnp.zeros_like(prev))
    ext_ref[HALO:HALO + ts, :] = cur
    ext_ref[HALO + ts:2 * HALO + ts, :] = jnp.where(i < n_tiles - 1, nxt, jnp.zeros_like(nxt))


def _fill_shifted(sh_ref, ext_ref, ts):
    n = ts + 2 * HALO - 8
    for r in range(1, 8):
        sh_ref[r - 1, 0:n, :] = ext_ref[pl.ds(r, n), :]


def _window(sh_ref, ext_ref, off, rows, r0=0):
    q, r = divmod(off, 8)
    if r == 0:
        return ext_ref[pl.ds(r0 + off, rows), :]
    return sh_ref[r - 1, pl.ds(r0 + 8 * q, rows), :]


def _conv_forward(ga, gg, conv_w, conv_b, ln_w, ln_b, w_pw, b_pw, ts):
    bsz, s, dc = ga.shape
    n_tiles = s // ts

    def body(a_ref, ap_ref, an_ref, g_ref, gp_ref, gn_ref, cw_ref, cb_ref, lw_ref, lb_ref, wp_ref, bp_ref,
             y_ref, cv_ref, ext_ref, sh_ref):
        i = pl.program_id(1)
        glu = lambda a, g: a * _sigmoid(g)
        _fill_ext(ext_ref, glu(a_ref[0], g_ref[0]), glu(ap_ref[0], gp_ref[0]), glu(an_ref[0], gn_ref[0]), i, n_tiles, ts)
        _fill_shifted(sh_ref, ext_ref, ts)
        acc = jnp.broadcast_to(cb_ref[...], (ts, dc))
        for j in range(CONV_WIDTH):
            acc = acc + cw_ref[j:j + 1, :] * _window(sh_ref, ext_ref, HALO - CONV_PAD + j, ts)
        y_ref[0] = acc
        mu = jnp.mean(acc, axis=-1, keepdims=True)
        yc = acc - mu
        var = jnp.mean(yc * yc, axis=-1, keepdims=True)
        yn = yc * lax.rsqrt(var + EPS) * lw_ref[...] + lb_ref[...]
        ys = yn * _sigmoid(yn)
        cv_ref[0] = jnp.dot(ys.astype(BF16), wp_ref[...], preferred_element_type=F32) + bp_ref[...]

    const = lambda a: pl.BlockSpec(a.shape, lambda b, i: (0,) * a.ndim)
    return pl.pallas_call(
        body, name="conv_forward", grid=(bsz, n_tiles),
        in_specs=_halo_specs(dc, ts, s) + _halo_specs(dc, ts, s)
        + [const(conv_w), const(conv_b), const(ln_w), const(ln_b), const(w_pw), const(b_pw)],
        out_specs=[pl.BlockSpec((1, ts, dc), lambda b, i: (b, i, 0))] * 2,
        out_shape=[jax.ShapeDtypeStruct((bsz, s, dc), F32)] * 2,
        scratch_shapes=[pltpu.VMEM((ts + 2 * HALO, dc), F32), pltpu.VMEM((7, ts + 2 * HALO, dc), F32)],
        compiler_params=_params(2),
    )(ga, ga, ga, gg, gg, gg, conv_w, conv_b, ln_w, ln_b, w_pw, b_pw)


def _outproj_loss(x, target, gate, o, za, cv, zc, w_out, tm):
    bsz, s, d = x.shape

    def body(x_ref, t_ref, gate_ref, o_ref, za_ref, cv_ref, zc_ref, w_ref,
             loss_ref, dh_ref, do_ref, dza_ref, dcv_ref, dzc_ref, dgate_ref, gw_ref):
        b, i = pl.program_id(0), pl.program_id(1)
        ov, cvv = o_ref[0], cv_ref[0]
        silu_a, dsilu_a = _silu_and_grad(za_ref[0])
        silu_c, dsilu_c = _silu_and_grad(zc_ref[0])
        mix = jnp.concatenate([ov * silu_a, cvv * silu_c], axis=1).astype(BF16)
        out = jnp.dot(mix, w_ref[...], preferred_element_type=F32)
        gate_v = gate_ref[0]
        err = x_ref[0] + gate_v * out - t_ref[0]
        dh = err * (1.0 / d)
        dh_ref[0] = dh
        dout = (dh * gate_v).astype(BF16)
        dmix = lax.dot_general(dout, w_ref[...], NT, preferred_element_type=F32)
        gw = lax.dot_general(mix, dout, TN, preferred_element_type=F32)
        dg = jnp.sum(dh * out, axis=0, keepdims=True)
        sq = jnp.sum(err * err)

        @pl.when(jnp.logical_and(b == 0, i == 0))
        def _():
            gw_ref[...] = gw

        @pl.when(jnp.logical_or(b > 0, i > 0))
        def _():
            gw_ref[...] += gw

        @pl.when(i == 0)
        def _():
            dgate_ref[0] = dg
            loss_ref[...] = jnp.zeros(loss_ref.shape, F32) + sq

        @pl.when(i > 0)
        def _():
            dgate_ref[0] += dg
            loss_ref[...] += sq

        dma, dmc = dmix[:, :D_ATTN], dmix[:, D_ATTN:]
        do_ref[0] = dma * silu_a
        dza_ref[0] = (dma * ov * dsilu_a).astype(BF16)
        dcv_ref[0] = dmc * silu_c
        dzc_ref[0] = (dmc * cvv * dsilu_c).astype(BF16)

    tok = lambda w: pl.BlockSpec((1, tm, w), lambda b, i: (b, i, 0))
    return pl.pallas_call(
        body, name="outproj_loss", grid=(bsz, s // tm),
        in_specs=[tok(d), tok(d), _bcast_spec(gate), tok(512), tok(512), tok(512), tok(512),
                  pl.BlockSpec(w_out.shape, lambda b, i: (0, 0))],
        out_specs=[pl.BlockSpec((1, 8, 128), lambda b, i: (b, 0, 0)), tok(d), tok(512), tok(512), tok(512), tok(512),
                   pl.BlockSpec((1, 1, d), lambda b, i: (b, 0, 0)), pl.BlockSpec((d, d), lambda b, i: (0, 0))],
        out_shape=[jax.ShapeDtypeStruct((bsz, 8, 128), F32), jax.ShapeDtypeStruct((bsz, s, d), F32),
                   jax.ShapeDtypeStruct((bsz, s, 512), F32), jax.ShapeDtypeStruct((bsz, s, 512), BF16),
                   jax.ShapeDtypeStruct((bsz, s, 512), F32), jax.ShapeDtypeStruct((bsz, s, 512), BF16),
                   jax.ShapeDtypeStruct((bsz, 1, d), F32), jax.ShapeDtypeStruct((d, d), F32)],
        compiler_params=_params(2),
    )(x, target, gate, o, za, cv, zc, w_out)


def _conv_token_backward(dcv, y, ln_w, ln_b, w_pw, tm):
    bsz, s, dc = dcv.shape

    def body(dcv_ref, y_ref, lw_ref, lb_ref, wp_ref, dy_ref, gwp_ref, st_ref):
        b, i = pl.program_id(0), pl.program_id(1)
        yv, dcvv = y_ref[0], dcv_ref[0]
        mu = jnp.mean(yv, axis=-1, keepdims=True)
        yc = yv - mu
        rstd = lax.rsqrt(jnp.mean(yc * yc, axis=-1, keepdims=True) + EPS)
        yhat = yc * rstd
        yn = yhat * lw_ref[...] + lb_ref[...]
        ys, dsilu = _silu_and_grad(yn)
        dcvb = dcvv.astype(BF16)
        gwp = lax.dot_general(ys.astype(BF16), dcvb, TN, preferred_element_type=F32)
        dys = lax.dot_general(dcvb, wp_ref[...], NT, preferred_element_type=F32)
        dyn = dys * dsilu
        dyhat = dyn * lw_ref[...]
        dy = rstd * (dyhat - jnp.mean(dyhat, axis=-1, keepdims=True)
                     - yhat * jnp.mean(dyhat * yhat, axis=-1, keepdims=True))
        dy_ref[0] = dy
        red = lambda v: jnp.sum(v, axis=0, keepdims=True)
        stats = jnp.concatenate([red(dcvv), red(dyn * yhat), red(dyn), red(dy), jnp.zeros((4, dc), F32)], axis=0)
        first = jnp.logical_and(b == 0, i == 0)

        @pl.when(first)
        def _():
            gwp_ref[...] = gwp
            st_ref[...] = stats

        @pl.when(jnp.logical_not(first))
        def _():
            gwp_ref[...] += gwp
            st_ref[...] += stats

    tok = pl.BlockSpec((1, tm, dc), lambda b, i: (b, i, 0))
    const = lambda a: pl.BlockSpec(a.shape, lambda b, i: (0,) * a.ndim)
    return pl.pallas_call(
        body, name="conv_token_backward", grid=(bsz, s // tm),
        in_specs=[tok, tok, const(ln_w), const(ln_b), const(w_pw)],
        out_specs=[tok, pl.BlockSpec((dc, dc), lambda b, i: (0, 0)), pl.BlockSpec((8, dc), lambda b, i: (0, 0))],
        out_shape=[jax.ShapeDtypeStruct((bsz, s, dc), F32), jax.ShapeDtypeStruct((dc, dc), F32),
                   jax.ShapeDtypeStruct((8, dc), F32)],
        compiler_params=_params(2),
    )(dcv, y, ln_w, ln_b, w_pw)


def _conv_backward(dy, ga, gg, conv_w, ts):
    bsz, s, dc = dy.shape
    n_tiles = s // ts

    def body(dy_ref, dyp_ref, dyn_ref, a_ref, g_ref, cw_ref,
             da_ref, dg_ref, gcw_ref, dyext_ref, dysh_ref, ug_ref, dug_ref, gacc_ref):
        b, i = pl.program_id(0), pl.program_id(1)
        av, sg = a_ref[0], _sigmoid(g_ref[0])
        ug_ref[...] = av * sg
        _fill_ext(dyext_ref, dy_ref[0], dyp_ref[0], dyn_ref[0], i, n_tiles, ts)
        _fill_shifted(dysh_ref, dyext_ref, ts)
        gacc_ref[...] = jnp.zeros(gacc_ref.shape, F32)

        def row_block(r, carry):
            r0 = pl.multiple_of(r * CONV_ROWS, CONV_ROWS)
            ugb = ug_ref[pl.ds(r0, CONV_ROWS), :]
            acc = jnp.zeros((CONV_ROWS, dc), F32)
            for j in range(CONV_WIDTH):
                win = _window(dysh_ref, dyext_ref, HALO + CONV_PAD - j, CONV_ROWS, r0)
                acc = acc + cw_ref[j:j + 1, :] * win
                prod = ugb * win
                part = prod[0:8, :]
                for k in range(8, CONV_ROWS, 8):
                    part = part + prod[k:k + 8, :]
                gacc_ref[j] += part
            dug_ref[pl.ds(r0, CONV_ROWS), :] = acc
            return carry

        lax.fori_loop(0, ts // CONV_ROWS, row_block, 0)
        dug = dug_ref[...]
        gcw = jnp.sum(gacc_ref[...], axis=1)
        first = jnp.logical_and(b == 0, i == 0)

        @pl.when(first)
        def _():
            gcw_ref[...] = gcw

        @pl.when(jnp.logical_not(first))
        def _():
            gcw_ref[...] += gcw

        da_ref[0] = (dug * sg).astype(BF16)
        dg_ref[0] = (dug * av * sg * (1.0 - sg)).astype(BF16)

    tok = pl.BlockSpec((1, ts, dc), lambda b, i: (b, i, 0))
    return pl.pallas_call(
        body, name="conv_backward", grid=(bsz, n_tiles),
        in_specs=_halo_specs(dc, ts, s) + [tok, tok, pl.BlockSpec(conv_w.shape, lambda b, i: (0, 0))],
        out_specs=[tok, tok, pl.BlockSpec((32, dc), lambda b, i: (0, 0))],
        out_shape=[jax.ShapeDtypeStruct((bsz, s, dc), BF16), jax.ShapeDtypeStruct((bsz, s, dc), BF16),
                   jax.ShapeDtypeStruct((32, dc), F32)],
        scratch_shapes=[pltpu.VMEM((ts + 2 * HALO, dc), F32), pltpu.VMEM((7, ts + 2 * HALO, dc), F32),
                        pltpu.VMEM((ts, dc), F32), pltpu.VMEM((ts, dc), F32), pltpu.VMEM((32, 8, dc), F32)],
        compiler_params=_params(2),
    )(dy, dy, dy, ga, gg, conv_w)


def _attn_backward(q, k2, v2, kt, o, do, lse, tq, fused):
    bsz, s, _ = q.shape
    sk = k2.shape[2]
    scale = 1.0 / math.sqrt(HEAD_DIM)
    nq = s // tq
    total = bsz * N_KV * nq
    at_steps = [(0, True), (total // 5, True), (total // 2, True), (total - 1, False)]

    def body(*refs):
        (q_ref, k_ref, v_ref, kt_ref, o_ref, do_ref, lse_ref), (dq_ref, dk_ref, dv_ref), _ = _split_fused(
            refs, 7, 3, 0, fused)
        g, i = pl.program_id(1), pl.program_id(2)
        step = (pl.program_id(0) * N_KV + g) * nq + i
        _run_phases(fused, step, at_steps, True)
        kk, vv = k_ref[0, 0], v_ref[0, 0]
        kgt = kt_ref[0, pl.ds(pl.multiple_of(g * HEAD_DIM, HEAD_DIM), HEAD_DIM), :]
        lo_mask = _lo_mask(tq)
        dk_acc = jnp.zeros((HEAD_DIM, sk), F32)
        dv_acc = jnp.zeros((HEAD_DIM, sk), F32)
        for j in range(2):
            cols = slice(128 * j, 128 * (j + 1))
            qp, dop, lsep = q_ref[0, :, cols], do_ref[0, :, cols], lse_ref[0, :, cols]
            dprod = dop * o_ref[0, :, cols]
            q_t = qp.astype(F32).T.astype(BF16)
            do_t = dop.T.astype(BF16)
            dq_t = []
            for half in range(2):
                sel = lo_mask if half == 0 else jnp.logical_not(lo_mask)
                rows = slice(HEAD_DIM * half, HEAD_DIM * (half + 1))
                qs = jnp.where(sel, qp, jnp.zeros_like(qp))
                dos = jnp.where(sel, dop, 0.0).astype(BF16)
                lse_h = jnp.max(jnp.where(sel, lsep, -jnp.inf), axis=-1, keepdims=True)
                delta = jnp.sum(jnp.where(sel, dprod, 0.0), axis=-1, keepdims=True)
                sc = lax.dot_general(qs, kk, NT, preferred_element_type=F32)
                p = jnp.exp(sc - lse_h)
                dp = lax.dot_general(dos, vv, NT, preferred_element_type=F32)
                ds = (p * (dp - delta)).astype(BF16)
                dv_acc = dv_acc + jnp.dot(do_t[rows, :], p.astype(BF16), preferred_element_type=F32)
                dk_acc = dk_acc + jnp.dot(q_t[rows, :], ds, preferred_element_type=F32)
                dq_t.append(lax.dot_general(kgt, ds, NT, preferred_element_type=F32))
            dq_ref[0, :, cols] = (jnp.concatenate(dq_t, axis=0) * scale).T

        @pl.when(i == 0)
        def _():
            dk_ref[0, 0] = dk_acc
            dv_ref[0, 0] = dv_acc

        @pl.when(i > 0)
        def _():
            dk_ref[0, 0] += dk_acc
            dv_ref[0, 0] += dv_acc

        _run_phases(fused, step, at_steps, False)

    q_spec = pl.BlockSpec((1, tq, 256), lambda b, g, i: (b, i, g))
    kv_spec = pl.BlockSpec((1, 1, sk, 128), lambda b, g, i: (b, g, 0, 0))
    acc_spec = pl.BlockSpec((1, 1, HEAD_DIM, sk), lambda b, g, i: (b, g, 0, 0))
    return pl.pallas_call(
        body, name="attn_backward", grid=(bsz, N_KV, nq),
        in_specs=[q_spec, kv_spec, kv_spec, pl.BlockSpec((1, 128, sk), lambda b, g, i: (b, 0, 0)), q_spec, q_spec,
                  q_spec] + fused.in_specs,
        out_specs=[q_spec, acc_spec, acc_spec] + fused.out_specs,
        out_shape=[jax.ShapeDtypeStruct((bsz, s, 512), F32), jax.ShapeDtypeStruct((bsz, 2, HEAD_DIM, sk), F32),
                   jax.ShapeDtypeStruct((bsz, 2, HEAD_DIM, sk), F32)] + fused.out_shape,
        scratch_shapes=fused.scratch,
        compiler_params=_params(3),
    )(q, k2, v2, kt, o, do, lse, *fused.arrs)


def _heads_to_lanes(acc_ref):
    return jnp.concatenate([acc_ref[0, 0], acc_ref[0, 1]], axis=0).T


def _norm_backward(dn, pre, w, bd):
    rstd = lax.rsqrt(_seg_mean(pre * pre, bd) + EPS)
    xhat = pre * rstd
    dxhat = dn * w
    return rstd * (dxhat - xhat * _seg_mean(dxhat * xhat, bd)), dn * xhat


def _qkv_backward(qkv, dq, dk2, dv2, cos, sin, qnw, knw, bd512, bd128, ts):
    bsz, s, _ = qkv.shape

    def body(p_ref, dq_ref, dk_ref, dv_ref, cos_ref, sin_ref, qnw_ref, knw_ref, bd512_ref, bd128_ref, d_ref, gw_ref):
        b, i = pl.program_id(0), pl.program_id(1)
        lo_mask = _lo_mask(ts)
        cos_t, sin_t = cos_ref[...], sin_ref[...]
        dqr = dq_ref[0]
        dqn = dqr * _tile_lanes(cos_t, 4) + _partner(dqr * _tile_lanes(sin_t, 4))
        dqp, gq = _norm_backward(dqn, p_ref[0, :, 0:512], qnw_ref[...], bd512_ref[...])
        dkr = _heads_to_lanes(dk_ref)
        dkn = dkr * cos_t + _partner(dkr * sin_t)
        dkp, gk = _norm_backward(dkn, p_ref[0, :, 512:640], knw_ref[...], bd128_ref[...])
        dvp = _heads_to_lanes(dv_ref)
        d_ref[0] = jnp.concatenate([dqp, dkp, dvp], axis=1).astype(BF16)
        gk512 = jnp.concatenate([jnp.sum(gk, axis=0, keepdims=True), jnp.zeros((1, 384), F32)], axis=1)
        rows = jnp.concatenate([jnp.sum(gq, axis=0, keepdims=True), gk512, jnp.zeros((6, 512), F32)], axis=0)
        first = jnp.logical_and(b == 0, i == 0)

        @pl.when(first)
        def _():
            gw_ref[...] = rows

        @pl.when(jnp.logical_not(first))
        def _():
            gw_ref[...] += rows

    const = lambda a: pl.BlockSpec(a.shape, lambda b, i: (0,) * a.ndim)
    kv_spec = pl.BlockSpec((1, 2, HEAD_DIM, ts), lambda b, i: (b, 0, 0, i))
    return pl.pallas_call(
        body, name="qkv_backward", grid=(bsz, s // ts),
        in_specs=[pl.BlockSpec((1, ts, 768), lambda b, i: (b, i, 0)), pl.BlockSpec((1, ts, 512), lambda b, i: (b, i, 0)),
                  kv_spec, kv_spec, pl.BlockSpec((ts, 128), lambda b, i: (i, 0)),
                  pl.BlockSpec((ts, 128), lambda b, i: (i, 0)), const(qnw), const(knw), const(bd512), const(bd128)],
        out_specs=[pl.BlockSpec((1, ts, 768), lambda b, i: (b, i, 0)), pl.BlockSpec((8, 512), lambda b, i: (0, 0))],
        out_shape=[jax.ShapeDtypeStruct((bsz, s, 768), BF16), jax.ShapeDtypeStruct((8, 512), F32)],
        compiler_params=_params(2),
    )(qkv, dq, dk2, dv2, cos, sin, qnw, knw, bd512, bd128)


def _ctx_kv_backward(pc, dk2, dv2, knw, bd128):
    bsz, cl, _ = pc.shape

    def body(p_ref, dk_ref, dv_ref, knw_ref, bd128_ref, d_ref, gw_ref):
        b = pl.program_id(0)
        lo_mask = _lo_mask(cl)
        dkn = _heads_to_lanes(dk_ref)
        dkp, gk = _norm_backward(dkn, p_ref[0, :, 0:128], knw_ref[...], bd128_ref[...])
        dvp = _heads_to_lanes(dv_ref)
        d_ref[0] = jnp.concatenate([dkp, dvp], axis=1).astype(BF16)
        rows = jnp.concatenate([jnp.sum(gk, axis=0, keepdims=True), jnp.zeros((7, 128), F32)], axis=0)

        @pl.when(b == 0)
        def _():
            gw_ref[...] = rows

        @pl.when(b > 0)
        def _():
            gw_ref[...] += rows

    const = lambda a: pl.BlockSpec(a.shape, lambda b: (0,) * a.ndim)
    blk = dk2.shape[3] // cl - 1
    kv_spec = pl.BlockSpec((1, 2, HEAD_DIM, cl), lambda b: (b, 0, 0, blk))
    return pl.pallas_call(
        body, name="ctx_kv_backward", grid=(bsz,),
        in_specs=[pl.BlockSpec((1, cl, 256), lambda b: (b, 0, 0)), kv_spec, kv_spec, const(knw), const(bd128)],
        out_specs=[pl.BlockSpec((1, cl, 256), lambda b: (b, 0, 0)), pl.BlockSpec((8, 128), lambda b: (0, 0))],
        out_shape=[jax.ShapeDtypeStruct((bsz, cl, 256), BF16), jax.ShapeDtypeStruct((8, 128), F32)],
        compiler_params=_params(1),
    )(pc, dk2, dv2, knw, bd128)


def _weight_grad(parts, u, init, tm, name):
    bsz, s, d = u.shape
    n_p = len(parts)
    nrows = sum(hi - lo for _, lo, hi in parts)

    def body(*refs):
        p_refs, u_ref = refs[:n_p], refs[n_p]
        gi_ref = refs[n_p + 1] if init is not None else None
        gw_ref = refs[-1]
        first = jnp.logical_and(pl.program_id(0) == 0, pl.program_id(1) == 0)
        dp = jnp.concatenate([r[0, :, lo:hi] for r, (_, lo, hi) in zip(p_refs, parts)], axis=1)
        gw = lax.dot_general(dp, u_ref[0], TN, preferred_element_type=F32)

        @pl.when(first)
        def _():
            gw_ref[...] = gw
            if init is not None:
                gw_ref[KV_LO:KV_HI, :] += gi_ref[...]

        @pl.when(jnp.logical_not(first))
        def _():
            gw_ref[...] += gw

    tok = lambda w: pl.BlockSpec((1, tm, w), lambda b, i: (b, i, 0))
    in_specs = [tok(a.shape[2]) for a, _, _ in parts] + [tok(d)]
    args = [a for a, _, _ in parts] + [u]
    if init is not None:
        in_specs.append(pl.BlockSpec(init.shape, lambda b, i: (0, 0)))
        args.append(init)
    return pl.pallas_call(
        body, name=name, grid=(bsz, s // tm), in_specs=in_specs,
        out_specs=pl.BlockSpec((nrows, d), lambda b, i: (0, 0)), out_shape=jax.ShapeDtypeStruct((nrows, d), F32),
        compiler_params=_params(2),
    )(*args)


def _inproj_backward(dps, x, dh, scale1p, norm_w, w_t, tm, name, fused=None):
    bsz, s, d = x.shape
    n_p = len(dps)
    shared = scale1p.shape[0] == 1
    with_dx = dh is not None
    n_in = n_p + (2 if with_dx else 1) + 3
    n_out = 3 if with_dx else 2
    total = bsz * (s // tm)
    at_steps = [(0, True), (total // 8, True), ((3 * total) // 4, True), (total - 1, False)]

    def body(*refs):
        ins, outs, _ = _split_fused(refs, n_in, n_out, 0, fused)
        dp_refs, x_ref = ins[:n_p], ins[n_p]
        dh_ref = ins[n_p + 1] if with_dx else None
        sc_ref, nw_ref, w_ref = ins[-3:]
        mod_ref, gnw_ref = outs[-2:]
        b, i = pl.program_id(0), pl.program_id(1)
        step = b * (s // tm) + i
        _run_phases(fused, step, at_steps, True)
        first = jnp.logical_and(b == 0, i == 0)
        dp = dp_refs[0][0] if n_p == 1 else jnp.concatenate([r[0] for r in dp_refs], axis=1)
        du = jnp.dot(dp, w_ref[...], preferred_element_type=F32)
        xv = x_ref[0]
        rstd = lax.rsqrt(jnp.mean(xv * xv, axis=-1, keepdims=True) + EPS)
        xhat = xv * rstd
        nw, sc = nw_ref[...], sc_ref[0]
        red = lambda v: jnp.sum(v, axis=0, keepdims=True)
        mod_rows = jnp.concatenate([red(du), red(du * (xhat * nw)), jnp.zeros((6, d), F32)], axis=0)
        gnw_rows = jnp.concatenate([red(du * sc * xhat), jnp.zeros((7, d), F32)], axis=0)
        mod_first = first if shared else i == 0

        @pl.when(mod_first)
        def _():
            mod_ref[0] = mod_rows

        @pl.when(jnp.logical_not(mod_first))
        def _():
            mod_ref[0] += mod_rows

        @pl.when(first)
        def _():
            gnw_ref[...] = gnw_rows

        @pl.when(jnp.logical_not(first))
        def _():
            gnw_ref[...] += gnw_rows

        if with_dx:
            dxhat = du * (nw * sc)
            outs[0][0] = dh_ref[0] + rstd * (dxhat - xhat * jnp.mean(dxhat * xhat, axis=-1, keepdims=True))
        _run_phases(fused, step, at_steps, False)

    tok = lambda w: pl.BlockSpec((1, tm, w), lambda b, i: (b, i, 0))
    in_specs = [tok(p.shape[2]) for p in dps] + [tok(d)]
    args = list(dps) + [x]
    if with_dx:
        in_specs.append(tok(d))
        args.append(dh)
    in_specs += [_bcast_spec(scale1p), pl.BlockSpec((1, d), lambda b, i: (0, 0)),
                 pl.BlockSpec(w_t.shape, lambda b, i: (0, 0))]
    args += [scale1p, norm_w, w_t]
    bm = scale1p.shape[0]
    mod_spec = pl.BlockSpec((1, 8, d), (lambda b, i: (0, 0, 0)) if shared else (lambda b, i: (b, 0, 0)))
    out_specs = [mod_spec, pl.BlockSpec((8, d), lambda b, i: (0, 0))]
    out_shape = [jax.ShapeDtypeStruct((bm, 8, d), F32), jax.ShapeDtypeStruct((8, d), F32)]
    if with_dx:
        out_specs.insert(0, tok(d))
        out_shape.insert(0, jax.ShapeDtypeStruct((bsz, s, d), F32))
    scratch = []
    if fused is not None:
        in_specs += fused.in_specs
        args += fused.arrs
        out_specs += fused.out_specs
        out_shape += fused.out_shape
        scratch = fused.scratch
    res = pl.pallas_call(
        body, name=name, grid=(bsz, s // tm), in_specs=in_specs, out_specs=out_specs, out_shape=out_shape,
        scratch_shapes=scratch, compiler_params=_params(2),
    )(*args)
    return list(res) if with_dx else [None] + list(res)


def _adamw_update(w_ref, g_ref, m_ref, v_ref, d_ref, nm_ref, nv_ref):
    gv = g_ref[...]
    mn = ADAM_B1 * m_ref[...] + (1.0 - ADAM_B1) * gv
    vn = ADAM_B2 * v_ref[...] + (1.0 - ADAM_B2) * (gv * gv)
    m_hat = mn / (1.0 - ADAM_B1 ** ADAM_STEP)
    v_hat = vn / (1.0 - ADAM_B2 ** ADAM_STEP)
    d_ref[...] = -ADAM_LR * (m_hat / (jnp.sqrt(v_hat) + ADAM_EPS) + ADAM_WD * w_ref[...])
    nm_ref[...] = mn
    nv_ref[...] = vn


def _adamw_small(ws, gs, ms, vs):
    n = len(ws)

    def body(*refs):
        ins, outs = refs[:4 * n], refs[4 * n:]
        for k in range(n):
            _adamw_update(ins[k], ins[n + k], ins[2 * n + k], ins[3 * n + k], outs[3 * k], outs[3 * k + 1],
                          outs[3 * k + 2])

    res = pl.pallas_call(
        body, name="adamw_small",
        out_shape=[jax.ShapeDtypeStruct(w.shape, F32) for w in ws for _ in range(3)], compiler_params=_params(),
    )(*ws, *gs, *ms, *vs)
    return [tuple(res[3 * k:3 * k + 3]) for k in range(n)]


def _adamw(w, g, m, v, name):
    r, cdim = w.shape
    tr = next((t for t in (256, 176) if r % t == 0 and r > t), r)

    def body(*refs):
        _adamw_update(*refs)

    spec = pl.BlockSpec((tr, cdim), lambda i: (i, 0))
    return pl.pallas_call(
        body, name=name, grid=(r // tr,), in_specs=[spec] * 4, out_specs=[spec] * 3,
        out_shape=[jax.ShapeDtypeStruct((r, cdim), F32)] * 3, compiler_params=_params(1),
    )(w, g, m, v)


def _rope_tables(s):
    rows = s // GRID_W
    freqs = np.float32(ROPE_THETA) ** (-np.arange(0, ROPE_AXIS_DIM, 2, dtype=np.float32) / np.float32(ROPE_AXIS_DIM))
    ang_r = np.arange(rows, dtype=np.float32)[:, None] * freqs[None, :]
    ang_c = np.arange(GRID_W, dtype=np.float32)[:, None] * freqs[None, :]
    zr, zc = np.zeros_like(ang_r), np.zeros_like(ang_c)

    def table(by_row, by_col):
        r = jnp.asarray(np.tile(np.concatenate(by_row + [zr, zr], axis=1), (1, 2)), dtype=F32)
        c = jnp.asarray(np.tile(np.concatenate([zc, zc] + by_col, axis=1), (1, 2)), dtype=F32)
        return jnp.repeat(r, GRID_W, axis=0) + jnp.tile(c, (rows, 1))

    return (table([np.cos(ang_r)] * 2, [np.cos(ang_c)] * 2),
            table([-np.sin(ang_r), np.sin(ang_r)], [-np.sin(ang_c), np.sin(ang_c)]))


def _pack_rows(parts, rows):
    flat = jnp.concatenate([p.reshape(-1) for p in parts])
    return jnp.pad(flat, (0, rows * D_MODEL - flat.shape[0])).reshape(rows, D_MODEL)


def kernel(x, c, ctx, c_ctx, w_mod, b_mod, norm_w, w_in, q_norm_w, k_norm_w, conv_w, conv_b, conv_ln_w, conv_ln_b, w_pw, b_pw, w_out, loss_target, m_c_ctx, m_w_mod, m_b_mod, m_norm_w, m_w_in, m_q_norm_w, m_k_norm_w, m_conv_w, m_conv_b, m_conv_ln_w, m_conv_ln_b, m_w_pw, m_b_pw, m_w_out, v_c_ctx, v_w_mod, v_b_mod, v_norm_w, v_w_in, v_q_norm_w, v_k_norm_w, v_conv_w, v_conv_b, v_conv_ln_w, v_conv_ln_b, v_w_pw, v_b_pw, v_w_out):
    bsz, s, d = x.shape
    cl = ctx.shape[1]
    xi, yi, ci = lax.axis_index("x"), lax.axis_index("y"), lax.axis_index("c")
    chip = 2 * xi + yi
    dev = 2 * chip + ci
    ncol_mod = w_mod.shape[2]

    w_in_t_loc = w_in[0].T.astype(BF16)
    b_cols = lax.dynamic_slice(b_mod, (0, chip * ncol_mod), (1, ncol_mod))
    sc_rows, mod_g, g_in = _front(jnp.pad(c, ((0, 8 - bsz), (0, 0))), jnp.pad(c_ctx[None, :], ((0, 15), (0, 0))),
                                  w_mod[0], b_cols, w_in_t_loc)
    w_in_t = g_in.reshape(D_IN, d)
    mod_all = mod_g.transpose(1, 0, 2).reshape(80, 3 * d)
    mod_loc = lax.dynamic_slice(mod_all, (8 * dev, 0), (bsz, 3 * d))
    shift, scale1p, gate = mod_loc[:, None, :d], 1.0 + mod_loc[:, None, d:2 * d], mod_loc[:, None, 2 * d:]
    shift_c, scale1p_c = mod_all[64:65, :d][None], 1.0 + mod_all[64:65, d:2 * d][None]

    cos, sin = _rope_tables(s)
    qnw512 = jnp.tile(q_norm_w, (1, 8))
    knw128 = jnp.tile(k_norm_w, (1, 2))
    bd512 = jnp.kron(jnp.eye(8, dtype=F32), jnp.ones((HEAD_DIM, HEAD_DIM), F32)).astype(BF16)
    bd128 = bd512[:128, :128]

    u, p_qkv, p_za, p_ga, p_gg, p_zc = _norm_inproj(x, shift, scale1p, norm_w, w_in_t, SPLITS, min(1024, s),
                                                    "norm_inproj")
    uc, pc_kv = _norm_inproj(ctx, shift_c, scale1p_c, norm_w, w_in_t[KV_LO:KV_HI], ((0, 256),), cl, "ctx_norm_inproj")
    q, k2x, v2x, ktx, vtx = _qkv_prep(p_qkv, cos, sin, qnw512, knw128, bd512, bd128, 512, cl)
    k2, v2, kt, vt = _ctx_kv_prep(pc_kv, knw128, bd128, k2x, v2x, ktx, vtx)
    conv_w_loc = jnp.pad(conv_w[0], ((0, 1), (0, 0)))
    o, lse, g_out, g_pw, g_cw = _attn_forward(
        q, k2, vt, min(1024, s), _ChipGather([w_out[0].astype(BF16), w_pw[0].astype(BF16), conv_w_loc]))
    w_out_f = g_out.reshape(d, d)
    w_pw_f = g_pw.reshape(D_CONV, D_CONV)
    conv_w_f = g_cw.transpose(1, 0, 2).reshape(32, D_CONV)
    y, cv = _conv_forward(p_ga, p_gg, conv_w_f, conv_b, conv_ln_w, conv_ln_b, w_pw_f, b_pw, 512)
    loss_part, dh, do, dza, dcv, dzc, dgate, gw_out = _outproj_loss(
        x, loss_target, gate, o, p_za, cv, p_zc, w_out_f, 512)

    all_chips, half_rows = (0, 1, 2, 3), D_IN // 2
    dy, gw_pw, conv_stats = _conv_token_backward(dcv, y, conv_ln_w, conv_ln_b, w_pw_f, 512)
    da, dg, gcw = _conv_backward(dy, p_ga, p_gg, conv_w_f, 512)
    tw = min(1024, s)
    gw_hi = _weight_grad([(da, half_rows - SPLITS[2][0], 512), (dg, 0, 512), (dzc, 0, 512)], u, None, tw,
                         "grad_in_rows_hi")
    dq, dkt, dvt, r_out, r_pw, r_hi = _attn_backward(
        q, k2, v2, kt, o, do, lse, 512, _FusedReduce([(gw_out, all_chips), (gw_pw, all_chips), (gw_hi, (2, 3))]))
    dqkv, qk_stats = _qkv_backward(p_qkv, dq, dkt, dvt, cos, sin, qnw512, knw128, bd512, bd128, 512)
    dpc, kc_stats = _ctx_kv_backward(pc_kv, dkt, dvt, knw128, bd128)
    gw_ctx = _weight_grad([(dpc, 0, 256)], uc, None, cl, "grad_in_rows_ctx")
    gw_lo = _weight_grad([(dqkv, 0, 768), (dza, 0, 512), (da, 0, half_rows - SPLITS[2][0])], u, gw_ctx, tw,
                         "grad_in_rows_lo")
    _, modc, gnw_c = _inproj_backward([dpc], ctx, None, scale1p_c, norm_w, w_in_t[KV_LO:KV_HI], cl,
                                      "ctx_inproj_backward")
    grad_x, modx, gnw_x, r_lo = _inproj_backward(
        [dqkv, dza, da, dg, dzc], x, dh, scale1p, norm_w, w_in_t, 512, "inproj_backward",
        _FusedReduce([(gw_lo, (0, 1))]))
    g_w_out, g_w_pw = r_out.reshape(d // 4, d), r_pw.reshape(D_CONV // 4, D_CONV)
    g_w_in_t = jnp.where(chip < 2, r_lo, r_hi).reshape(D_IN // 4, d)

    dmod_loc = jnp.concatenate([modx[:, 0, :], modx[:, 1, :], dgate[:, 0, :]], axis=1)
    gq = qk_stats[0].reshape(8, HEAD_DIM).sum(axis=0)
    gk = (qk_stats[1, :128] + kc_stats[0]).reshape(2, HEAD_DIM).sum(axis=0)
    packed = _pack_rows([dmod_loc, dmod_loc.sum(axis=0), gnw_x[0] + gnw_c[0], modc[0, 0], modc[0, 1], gq, gk,
                         conv_stats[0], conv_stats[1], conv_stats[2], conv_stats[3], gcw,
                         jnp.sum(loss_part[:, 0, 0])[None]], 32)
    total, g_w_mod, dsilu_ctx = _tail_exchange(packed, sc_rows, w_mod[0], bsz, 3 * bsz + 4)
    flat = total.reshape(-1)
    offs = [0]

    def take(nelem):
        lo = offs[0]
        offs[0] = lo + nelem
        return flat[lo:lo + nelem]

    take(bsz * 3 * d)
    g_b_mod_x = take(3 * d)
    g_norm_w = take(d)
    dshift_c, dscale_c = take(d), take(d)
    g_qnw, g_knw = take(HEAD_DIM), take(HEAD_DIM)
    g_b_pw, g_ln_w, g_ln_b, g_conv_b = take(D_CONV), take(D_CONV), take(D_CONV), take(D_CONV)
    g_conv_w_full = take(32 * D_CONV).reshape(32, D_CONV)
    loss = take(1)[0] * (0.5 / d)

    dmod_c = jnp.concatenate([dshift_c, dscale_c, jnp.zeros((d,), F32)])
    g_b_mod = (g_b_mod_x + dmod_c)[None, :]
    sg = _sigmoid(c_ctx)
    g_c_ctx = dsilu_ctx[0] * (sg * (1.0 + c_ctx * (1.0 - sg)))

    g_w_in = g_w_in_t.T
    g_conv_w = lax.dynamic_slice(g_conv_w_full, (0, chip * 128), (CONV_WIDTH, 128))

    grads = {
        "c_ctx": g_c_ctx, "w_mod": g_w_mod[None], "b_mod": g_b_mod, "norm_w": g_norm_w[None], "w_in": g_w_in[None],
        "q_norm_w": g_qnw[None], "k_norm_w": g_knw[None], "conv_w": g_conv_w[None], "conv_b": g_conv_b[None],
        "conv_ln_w": g_ln_w[None], "conv_ln_b": g_ln_b[None], "w_pw": g_w_pw[None], "b_pw": g_b_pw[None],
        "w_out": g_w_out[None],
    }
    weights = {
        "c_ctx": (c_ctx, m_c_ctx, v_c_ctx), "w_mod": (w_mod, m_w_mod, v_w_mod), "b_mod": (b_mod, m_b_mod, v_b_mod),
        "norm_w": (norm_w, m_norm_w, v_norm_w), "w_in": (w_in, m_w_in, v_w_in),
        "q_norm_w": (q_norm_w, m_q_norm_w, v_q_norm_w), "k_norm_w": (k_norm_w, m_k_norm_w, v_k_norm_w),
        "conv_w": (conv_w, m_conv_w, v_conv_w), "conv_b": (conv_b, m_conv_b, v_conv_b),
        "conv_ln_w": (conv_ln_w, m_conv_ln_w, v_conv_ln_w), "conv_ln_b": (conv_ln_b, m_conv_ln_b, v_conv_ln_b),
        "w_pw": (w_pw, m_w_pw, v_w_pw), "b_pw": (b_pw, m_b_pw, v_b_pw), "w_out": (w_out, m_w_out, v_w_out),
    }
    names = list(weights)
    big = ("w_mod", "w_in", "w_out")
    as_2d = lambda a: a.reshape((1, a.shape[0]) if a.ndim == 1 else (a.shape[-2] if a.ndim == 3 else 1, a.shape[-1]))
    small = [n for n in names if n not in big]
    w_g_m_v = zip(*[[as_2d(a) for a in (weights[n][0], grads[n], weights[n][1], weights[n][2])] for n in small])
    updates = dict(zip(small, _adamw_small(*[list(col) for col in w_g_m_v])))
    for n in ("w_mod", "w_out"):
        w, m, v = weights[n]
        updates[n] = _adamw(as_2d(w), as_2d(grads[n]), as_2d(m), as_2d(v), "adamw_" + n)
    w, m, v = weights["w_in"]
    updates["w_in"] = tuple(r.T for r in _adamw(w[0].T, g_w_in_t, m[0].T, v[0].T, "adamw_w_in"))
    deltas, new_ms, new_vs = ([updates[n][k].reshape(weights[n][0].shape) for n in names] for k in range(3))
    grads = {n: grads[n].reshape(weights[n][0].shape) for n in names}

    return (loss, grad_x, *[grads[n] for n in names], *deltas, *new_ms, *new_vs)
```

```python
import functools
import math

import jax
import jax.numpy as jnp
import numpy as np
from jax import lax
from jax.experimental import pallas as pl
from jax.experimental.pallas import tpu as pltpu

F32 = jnp.float32
BF16 = jnp.bfloat16
MESH = pl.DeviceIdType.MESH

D_MODEL = 1024
D_ATTN = 512
D_CONV = 512
HEAD_DIM = 64
N_KV = 2
GRID_W = 64
ROPE_AXIS_DIM = 32
ROPE_THETA = 10000.0
CONV_WIDTH = 31
CONV_PAD = 15
HALO = 16
CONV_ROWS = 32
EPS = 1e-6
SPLITS = ((0, 768), (768, 1280), (1280, 1792), (1792, 2304), (2304, 2816))
D_IN = 2816
KV_LO, KV_HI = 512, 768

ADAM_LR = 0.001
ADAM_B1 = 0.9
ADAM_B2 = 0.999
ADAM_EPS = 1e-08
ADAM_WD = 0.01
ADAM_STEP = 10

VMEM_LIMIT = 56 * 1024 * 1024

TILE_TOKENS = 512
TILE_INPROJ = 1024
TILE_ATTN_FWD = 1024
TILE_ATTN_BWD = 512

NT = (((1,), (1,)), ((), ()))
TN = (((0,), (0,)), ((), ()))


def _params(n_axes=0, **kw):
    if n_axes:
        kw["dimension_semantics"] = ("arbitrary",) * n_axes
    return pltpu.CompilerParams(vmem_limit_bytes=VMEM_LIMIT, **kw)


def _sigmoid(x):
    return 1.0 / (1.0 + jnp.exp(-x))


def _silu_and_grad(z):
    s = _sigmoid(z)
    return z * s, s * (1.0 + z * (1.0 - s))


def _seg_mean(v, ones_bd):
    hi = v.astype(BF16)
    lo = (v - hi.astype(F32)).astype(BF16)
    s = jnp.dot(hi, ones_bd, preferred_element_type=F32) + jnp.dot(lo, ones_bd, preferred_element_type=F32)
    return s * (1.0 / HEAD_DIM)


def _partner(v):
    n = v.shape[1]
    lane = lax.broadcasted_iota(jnp.int32, (v.shape[0], 128), 1)
    first = (lane % 32) < 16
    parts = []
    for k in range(n // 128):
        ch = v[:, 128 * k:128 * (k + 1)]
        parts.append(jnp.where(first, pltpu.roll(ch, 112, 1), pltpu.roll(ch, 16, 1)))
    return parts[0] if len(parts) == 1 else jnp.concatenate(parts, axis=1)


def _tile_lanes(t, reps):
    return t if reps == 1 else jnp.concatenate([t] * reps, axis=1)


def _lo_mask(rows):
    return lax.broadcasted_iota(jnp.int32, (rows, 128), 1) < HEAD_DIM


def _gather8_in_vmem(x_ref, out_ref, send_sems, recv_sems, local_sem):
    x, y, c = lax.axis_index("x"), lax.axis_index("y"), lax.axis_index("c")
    me, sibling = (x, y, c), (x, y, 1 - c)
    chips = [(1 - x, y), (x, 1 - y), (1 - x, 1 - y)]

    def slot(px, py, pc):
        return out_ref.at[4 * px + 2 * py + pc]

    def copy(k, block, to, src=None):
        return pltpu.make_async_remote_copy(
            src_ref=slot(*block) if src is None else src, dst_ref=slot(*block),
            send_sem=send_sems.at[k], recv_sem=recv_sems.at[k], device_id=to, device_id_type=MESH)

    mine = pltpu.make_async_copy(x_ref, slot(*me), local_sem)
    mine.start()
    first = [copy(0, me, sibling, src=x_ref)]
    first += [copy(1 + j, me, (*chip, c), src=x_ref) for j, chip in enumerate(chips)]
    for cp in first:
        cp.start()
    passed = [copy(4 + j, (*chip, c), sibling) for j, chip in enumerate(chips)]
    for j, chip in enumerate(chips):
        copy(1 + j, (*chip, c), me).wait_recv()
        passed[j].start()
    copy(0, sibling, me).wait_recv()
    for j, chip in enumerate(chips):
        copy(4 + j, (*chip, 1 - c), me).wait_recv()
    for cp in first + passed:
        cp.wait_send()
    mine.wait()


class _ChipGather:
    def __init__(self, arrs):
        self.arrs = list(arrs)
        n = self.n = len(self.arrs)
        self.in_specs = [pl.BlockSpec(memory_space=pl.ANY)] * n
        self.out_shape = [jax.ShapeDtypeStruct((4,) + a.shape, a.dtype) for a in self.arrs]
        self.out_specs = [pl.BlockSpec(memory_space=pl.ANY)] * n
        self.scratch = [pltpu.SemaphoreType.DMA((6 * n,)), pltpu.SemaphoreType.DMA((6 * n,)),
                        pltpu.SemaphoreType.DMA((n,))]
        self.phases = [self.start, self.forward, self.finish]

    def bind(self, ins, outs, scratch):
        self.ins, self.outs = ins, outs
        self.send_sems, self.recv_sems, self.local_sems = scratch
        self.x, self.y, self.c = lax.axis_index("x"), lax.axis_index("y"), lax.axis_index("c")
        self.chips = [(1 - self.x, self.y), (self.x, 1 - self.y), (1 - self.x, 1 - self.y)]
        self.mychip = 2 * self.x + self.y

    def _copy(self, a, k, chip_idx, cc, to, src=None):
        h = self.arrs[a].shape[0] // 2
        dst = self.outs[a].at[chip_idx, pl.ds(cc * h, h)]
        return pltpu.make_async_remote_copy(
            src_ref=dst if src is None else src, dst_ref=dst, send_sem=self.send_sems.at[6 * a + k],
            recv_sem=self.recv_sems.at[6 * a + k], device_id=to, device_id_type=MESH)

    def _local(self, a):
        return pltpu.make_async_copy(self.ins[a], self.outs[a].at[self.mychip], self.local_sems.at[a])

    def _first(self, a, j):
        h = self.arrs[a].shape[0] // 2
        return self._copy(a, j, self.mychip, self.c, (*self.chips[j], self.c), src=self.ins[a].at[pl.ds(self.c * h, h)])

    def _passed(self, a, j):
        cx, cy = self.chips[j]
        return self._copy(a, 3 + j, 2 * cx + cy, self.c, (self.x, self.y, 1 - self.c))

    def start(self):
        for a in range(self.n):
            self._local(a).start()
            for j in range(3):
                self._first(a, j).start()

    def forward(self):
        for a in range(self.n):
            for j, (cx, cy) in enumerate(self.chips):
                self._copy(a, j, 2 * cx + cy, self.c, (self.x, self.y, self.c)).wait_recv()
                self._passed(a, j).start()

    def finish(self):
        for a in range(self.n):
            for j, (cx, cy) in enumerate(self.chips):
                self._copy(a, 3 + j, 2 * cx + cy, 1 - self.c, (self.x, self.y, self.c)).wait_recv()
        for a in range(self.n):
            for j in range(3):
                self._first(a, j).wait_send()
                self._passed(a, j).wait_send()
            self._local(a).wait()


class _FusedReduce:
    def __init__(self, pieces):
        self.owners = [tuple(o) for _, o in pieces]
        self.arrs = [g.reshape(len(o), 2, g.shape[0] // (2 * len(o)), g.shape[1]) for g, o in pieces]
        n = self.n = len(pieces)
        hc = self.hc = [(v.shape[2], v.shape[3]) for v in self.arrs]
        nts = [len(o) for o in self.owners]
        self.base = [sum(nts[:p]) for p in range(n)]
        anyspec = pl.BlockSpec(memory_space=pl.ANY)
        self.in_specs = [anyspec] * n
        self.out_shape = [jax.ShapeDtypeStruct((2,) + s, F32) for s in hc]
        self.out_specs = [anyspec] * n
        self.scratch = [pltpu.VMEM((nt,) + s, F32) for nt, s in zip(nts, hc)]
        self.scratch += [pltpu.VMEM((nt,) + s, F32) for nt, s in zip(nts, hc)]
        self.scratch += [pltpu.VMEM(s, F32) for s in hc]
        self.scratch += [pltpu.VMEM((nt,) + s, BF16) for nt, s in zip(nts, hc)]
        self.scratch += [pltpu.VMEM((3,) + s, BF16) for s in hc]
        self.scratch += [pltpu.VMEM(s, F32) for s in hc]
        tot = sum(nts)
        self.scratch += [pltpu.SemaphoreType.DMA((tot,)), pltpu.SemaphoreType.DMA((tot,)),
                         pltpu.SemaphoreType.DMA((tot,)), pltpu.SemaphoreType.DMA((3 * n,)),
                         pltpu.SemaphoreType.DMA((n,)), pltpu.SemaphoreType.DMA((n,)), pltpu.SemaphoreType.DMA((n,)),
                         pltpu.SemaphoreType.DMA((tot,))]
        self.phases = [self.start, self.exchange, self.combine, self.finish]

    def bind(self, ins, outs, scratch):
        n = self.n
        self.g, self.out = ins, outs
        self.va, self.recv_a, self.own = scratch[:n], scratch[n:2 * n], scratch[2 * n:3 * n]
        self.tsend, self.recv_b, self.fin = scratch[3 * n:4 * n], scratch[4 * n:5 * n], scratch[5 * n:6 * n]
        self.sa, self.ra, self.sb, self.rb, self.sc, self.rc, self.lc, self.la = scratch[6 * n:]
        self.x, self.y, self.c = lax.axis_index("x"), lax.axis_index("y"), lax.axis_index("c")
        self.mychip = 2 * self.x + self.y
        self.sibling = (self.x, self.y, 1 - self.c)

    def _copy_a(self, p, t):
        k = self.base[p] + t
        return pltpu.make_async_remote_copy(
            src_ref=self.g[p].at[t, 1 - self.c], dst_ref=self.recv_a[p].at[t], send_sem=self.sa.at[k],
            recv_sem=self.ra.at[k], device_id=self.sibling, device_id_type=MESH)

    def _fetch(self, p, t):
        return pltpu.make_async_copy(self.g[p].at[t, self.c], self.va[p].at[t], self.la.at[self.base[p] + t])

    def _slot(self, owner):
        rel = jnp.bitwise_xor(self.mychip, owner)
        return jnp.where(rel == 2, 0, jnp.where(rel == 1, 1, 2))

    def _copy_b(self, p, t, slot):
        owner = self.owners[p][t]
        return pltpu.make_async_remote_copy(
            src_ref=self.tsend[p].at[t], dst_ref=self.recv_b[p].at[slot], send_sem=self.sb.at[self.base[p] + t],
            recv_sem=self.rb.at[3 * p + slot], device_id=(owner // 2, owner % 2, self.c), device_id_type=MESH)

    def _copy_c(self, p, half):
        return pltpu.make_async_remote_copy(
            src_ref=self.fin[p], dst_ref=self.out[p].at[half], send_sem=self.sc.at[p], recv_sem=self.rc.at[p],
            device_id=self.sibling, device_id_type=MESH)

    def _local_c(self, p):
        return pltpu.make_async_copy(self.fin[p], self.out[p].at[self.c], self.lc.at[p])

    def start(self):
        for p in range(self.n):
            for t in range(len(self.owners[p])):
                self._copy_a(p, t).start()
                self._fetch(p, t).start()

    def exchange(self):
        for p in range(self.n):
            for t, owner in enumerate(self.owners[p]):
                self._copy_a(p, t).wait_recv()
                self._fetch(p, t).wait()
                mine = self.mychip == owner

                @pl.when(mine)
                def _():
                    self.own[p][...] = self.va[p][t] + self.recv_a[p][t]

                @pl.when(jnp.logical_not(mine))
                def _():
                    self.tsend[p][t] = (self.va[p][t] + self.recv_a[p][t]).astype(BF16)
                    self._copy_b(p, t, self._slot(owner)).start()

    def combine(self):
        for p in range(self.n):
            for t, owner in enumerate(self.owners[p]):
                @pl.when(self.mychip == owner)
                def _():
                    acc = self.own[p][...]
                    for j in range(3):
                        self._copy_b(p, t, j).wait_recv()
                        acc = acc + self.recv_b[p][j].astype(F32)
                    self.fin[p][...] = acc
                    self._local_c(p).start()
                    self._copy_c(p, self.c).start()

    def finish(self):
        for p in range(self.n):
            for t, owner in enumerate(self.owners[p]):
                self._copy_a(p, t).wait_send()
                mine = self.mychip == owner

                @pl.when(mine)
                def _():
                    self._copy_c(p, 1 - self.c).wait_recv()
                    self._copy_c(p, self.c).wait_send()
                    self._local_c(p).wait()

                @pl.when(jnp.logical_not(mine))
                def _():
                    self._copy_b(p, t, self._slot(owner)).wait_send()


def _split_fused(refs, n_in, n_out, n_scr, fused):
    if fused is None:
        return refs[:n_in], refs[n_in:n_in + n_out], refs[n_in + n_out:]
    fi, fo = len(fused.in_specs), len(fused.out_specs)
    ins, rest = refs[:n_in], refs[n_in:]
    f_ins, rest = rest[:fi], rest[fi:]
    outs, rest = rest[:n_out], rest[n_out:]
    f_outs, rest = rest[:fo], rest[fo:]
    scr, f_scr = rest[:n_scr], rest[n_scr:]
    fused.bind(f_ins, f_outs, f_scr)
    return ins, outs, scr


def _run_phases(fused, step, at_steps, before):
    if fused is None:
        return
    for phase, (at, first) in zip(fused.phases, at_steps):
        if first == before:
            pl.when(step == at)(phase)


def _front(c_pad, c_ctx_rows, w_mod, b_cols, w_in_t_loc):
    ncol = w_mod.shape[1]
    gather = _ChipGather([w_in_t_loc])

    def body(c_ref, cctx_ref, w_ref, b_ref, win_ref, sc_ref, modg_ref, wing_ref,
             call_ref, ag_send, ag_recv, ag_local, m_send, m_recv, *g_scr):
        gather.bind([win_ref], [wing_ref], g_scr)
        _gather8_in_vmem(c_ref, call_ref, ag_send, ag_recv, ag_local)
        gather.start()
        x, y, c = lax.axis_index("x"), lax.axis_index("y"), lax.axis_index("c")
        chips = [(1 - x, y), (x, 1 - y), (1 - x, 1 - y)]
        mychip = 2 * x + y
        rows = jnp.concatenate([call_ref[dv] for dv in range(8)] + [cctx_ref[...]], axis=0)
        sc = rows * _sigmoid(rows)
        sc_ref[...] = sc
        modg_ref[mychip] = jnp.dot(sc, w_ref[...], preferred_element_type=F32,
                                   precision=lax.Precision.HIGHEST) + b_ref[...]

        def mcopy(j, chip_idx, to):
            return pltpu.make_async_remote_copy(
                src_ref=modg_ref.at[chip_idx], dst_ref=modg_ref.at[chip_idx], send_sem=m_send.at[j],
                recv_sem=m_recv.at[j], device_id=to, device_id_type=MESH)

        sends = [mcopy(j, mychip, (*chip, c)) for j, chip in enumerate(chips)]
        for cp in sends:
            cp.start()
        for j, (cx, cy) in enumerate(chips):
            mcopy(j, 2 * cx + cy, (x, y, c)).wait_recv()
        gather.forward()
        gather.finish()
        for cp in sends:
            cp.wait_send()

    vm = pl.BlockSpec(memory_space=pltpu.VMEM)
    return pl.pallas_call(
        body, name="front_exchange",
        out_shape=[jax.ShapeDtypeStruct((80, D_MODEL), F32), jax.ShapeDtypeStruct((4, 80, ncol), F32)] + gather.out_shape,
        in_specs=[vm, vm, vm, vm] + gather.in_specs, out_specs=[vm, vm] + gather.out_specs,
        scratch_shapes=[pltpu.VMEM((8, 8, D_MODEL), F32), pltpu.SemaphoreType.DMA((7,)), pltpu.SemaphoreType.DMA((7,)),
                        pltpu.SemaphoreType.DMA, pltpu.SemaphoreType.DMA((3,)), pltpu.SemaphoreType.DMA((3,))]
        + gather.scratch,
        compiler_params=_params(),
    )(c_pad, c_ctx_rows, w_mod, b_cols, w_in_t_loc)


def _tail_exchange(packed, sc_rows, w_mod, bsz, ctx_row):
    d = D_MODEL
    ncol = w_mod.shape[1]

    def body(p_ref, sc_ref, w_ref, total_ref, gw_ref, gcc_ref, gat_ref, dm_ref, part_ref,
             ag_send, ag_recv, ag_local, g_send, g_recv):
        _gather8_in_vmem(p_ref, gat_ref, ag_send, ag_recv, ag_local)
        acc = gat_ref[0]
        for dv in range(1, 8):
            acc = acc + gat_ref[dv]
        total_ref[...] = acc
        x, y, c = lax.axis_index("x"), lax.axis_index("y"), lax.axis_index("c")
        chips = [(1 - x, y), (x, 1 - y), (1 - x, 1 - y)]
        mychip = 2 * x + y
        dm_ref[...] = jnp.zeros(dm_ref.shape, F32)
        for k in range(4):
            @pl.when(mychip == k)
            def _():
                spans = [(seg, max(k * ncol, seg * d) - seg * d, min((k + 1) * ncol, (seg + 1) * d) - seg * d)
                         for seg in range(3) if k * ncol < (seg + 1) * d and (k + 1) * ncol > seg * d]
                for dv in range(8):
                    for b in range(bsz):
                        dm_ref[8 * dv + b:8 * dv + b + 1, :] = jnp.concatenate(
                            [gat_ref[dv, 3 * b + seg:3 * b + seg + 1, lo:hi] for seg, lo, hi in spans], axis=1)
                dm_ref[64:65, :] = jnp.concatenate(
                    [total_ref[ctx_row + seg:ctx_row + seg + 1, lo:hi] if seg < 2 else jnp.zeros((1, hi - lo), F32)
                     for seg, lo, hi in spans], axis=1)

        dm = dm_ref[...]
        gw_ref[...] = lax.dot_general(sc_ref[...], dm, TN, preferred_element_type=F32,
                                      precision=lax.Precision.HIGHEST)
        part_ref[mychip] = lax.dot_general(dm[64:72, :], w_ref[...], NT, preferred_element_type=F32,
                                           precision=lax.Precision.HIGHEST)

        def gcopy(j, chip_idx, to):
            return pltpu.make_async_remote_copy(
                src_ref=part_ref.at[chip_idx], dst_ref=part_ref.at[chip_idx], send_sem=g_send.at[j],
                recv_sem=g_recv.at[j], device_id=to, device_id_type=MESH)

        sends = [gcopy(j, mychip, (*chip, c)) for j, chip in enumerate(chips)]
        for cp in sends:
            cp.start()
        for j, (cx, cy) in enumerate(chips):
            gcopy(j, 2 * cx + cy, (x, y, c)).wait_recv()
        for cp in sends:
            cp.wait_send()
        gcc_ref[...] = (part_ref[0] + part_ref[1]) + (part_ref[2] + part_ref[3])

    return pl.pallas_call(
        body, name="tail_exchange",
        out_shape=[jax.ShapeDtypeStruct(packed.shape, F32), jax.ShapeDtypeStruct((d, ncol), F32),
                   jax.ShapeDtypeStruct((8, d), F32)],
        scratch_shapes=[pltpu.VMEM((8,) + packed.shape, F32), pltpu.VMEM((80, ncol), F32), pltpu.VMEM((4, 8, d), F32),
                        pltpu.SemaphoreType.DMA((7,)), pltpu.SemaphoreType.DMA((7,)), pltpu.SemaphoreType.DMA,
                        pltpu.SemaphoreType.DMA((3,)), pltpu.SemaphoreType.DMA((3,))],
        compiler_params=_params(),
    )(packed, sc_rows, w_mod)


def _bcast_spec(arr):
    if arr.shape[0] == 1:
        return pl.BlockSpec((1, 1, arr.shape[2]), lambda b, i: (0, 0, 0))
    return pl.BlockSpec((1, 1, arr.shape[2]), lambda b, i: (b, 0, 0))


def _norm_inproj(x, shift, scale1p, norm_w, w_t, splits, tm, name):
    bsz, s, d = x.shape

    def body(x_ref, sh_ref, sc_ref, nw_ref, w_ref, u_ref, *out_refs):
        xv = x_ref[0]
        rstd = lax.rsqrt(jnp.mean(xv * xv, axis=-1, keepdims=True) + EPS)
        u = (xv * rstd * nw_ref[...]) * sc_ref[0] + sh_ref[0]
        ub = u.astype(BF16)
        u_ref[0] = ub
        for (lo, hi), o_ref in zip(splits, out_refs):
            o_ref[0] = lax.dot_general(ub, w_ref[lo:hi, :], NT, preferred_element_type=F32)

    tok = lambda w: pl.BlockSpec((1, tm, w), lambda b, i: (b, i, 0))
    return pl.pallas_call(
        body, name=name, grid=(bsz, s // tm),
        in_specs=[tok(d), _bcast_spec(shift), _bcast_spec(scale1p), pl.BlockSpec((1, d), lambda b, i: (0, 0)),
                  pl.BlockSpec(w_t.shape, lambda b, i: (0, 0))],
        out_specs=[tok(d)] + [tok(hi - lo) for lo, hi in splits],
        out_shape=[jax.ShapeDtypeStruct((bsz, s, d), BF16)]
        + [jax.ShapeDtypeStruct((bsz, s, hi - lo), F32) for lo, hi in splits],
        compiler_params=_params(2),
    )(x, shift, scale1p, norm_w, w_t)


def _dup_heads(kv, lo_mask):
    r = pltpu.roll(kv, HEAD_DIM, 1)
    return jnp.where(lo_mask, kv, r), jnp.where(lo_mask, r, kv)


def _qkv_prep(qkv, cos, sin, qnw, knw, bd512, bd128, ts, extra):
    bsz, s, _ = qkv.shape

    def body(p_ref, cos_ref, sin_ref, qnw_ref, knw_ref, bd512_ref, bd128_ref, q_ref, k_ref, v_ref, kt_ref, vt_ref):
        lo_mask = _lo_mask(ts)
        cos_t, sin_t = cos_ref[...], sin_ref[...]
        qp = p_ref[0, :, 0:512]
        qn = qp * lax.rsqrt(_seg_mean(qp * qp, bd512_ref[...]) + EPS) * qnw_ref[...]
        qr = qn * _tile_lanes(cos_t, 4) + _partner(qn) * _tile_lanes(sin_t, 4)
        q_ref[0] = (qr * (1.0 / math.sqrt(HEAD_DIM))).astype(BF16)
        kp = p_ref[0, :, 512:640]
        kn = kp * lax.rsqrt(_seg_mean(kp * kp, bd128_ref[...]) + EPS) * knw_ref[...]
        kr = kn * cos_t + _partner(kn) * sin_t
        k0, k1 = _dup_heads(kr, lo_mask)
        k_ref[0, 0] = k0.astype(BF16)
        k_ref[0, 1] = k1.astype(BF16)
        vp = p_ref[0, :, 640:768]
        v0, v1 = _dup_heads(vp, lo_mask)
        v_ref[0, 0] = v0.astype(BF16)
        v_ref[0, 1] = v1.astype(BF16)
        kt_ref[0] = kr.T.astype(BF16)
        vt_ref[0] = vp.T.astype(BF16)

    const = lambda a: pl.BlockSpec(a.shape, lambda b, i: (0,) * a.ndim)
    kv_spec = pl.BlockSpec((1, 2, ts, 128), lambda b, i: (b, 0, i, 0))
    t_spec = pl.BlockSpec((1, 128, ts), lambda b, i: (b, 0, i))
    return pl.pallas_call(
        body, name="qkv_prep", grid=(bsz, s // ts),
        in_specs=[pl.BlockSpec((1, ts, 768), lambda b, i: (b, i, 0)),
                  pl.BlockSpec((ts, 128), lambda b, i: (i, 0)), pl.BlockSpec((ts, 128), lambda b, i: (i, 0)),
                  const(qnw), const(knw), const(bd512), const(bd128)],
        out_specs=[pl.BlockSpec((1, ts, 512), lambda b, i: (b, i, 0)), kv_spec, kv_spec, t_spec, t_spec],
        out_shape=[jax.ShapeDtypeStruct((bsz, s, 512), BF16), jax.ShapeDtypeStruct((bsz, 2, s + extra, 128), BF16),
                   jax.ShapeDtypeStruct((bsz, 2, s + extra, 128), BF16),
                   jax.ShapeDtypeStruct((bsz, 128, s + extra), BF16), jax.ShapeDtypeStruct((bsz, 128, s + extra), BF16)],
        compiler_params=_params(2),
    )(qkv, cos, sin, qnw, knw, bd512, bd128)


def _ctx_kv_prep(pc, knw, bd128, k2, v2, kt, vt):
    bsz, cl, _ = pc.shape
    blk = k2.shape[2] // cl - 1

    def body(p_ref, knw_ref, bd128_ref, k_in, v_in, kt_in, vt_in, k_ref, v_ref, kt_ref, vt_ref):
        lo_mask = _lo_mask(cl)
        kp = p_ref[0, :, 0:128]
        kn = kp * lax.rsqrt(_seg_mean(kp * kp, bd128_ref[...]) + EPS) * knw_ref[...]
        k0, k1 = _dup_heads(kn, lo_mask)
        k_ref[0, 0] = k0.astype(BF16)
        k_ref[0, 1] = k1.astype(BF16)
        vp = p_ref[0, :, 128:256]
        v0, v1 = _dup_heads(vp, lo_mask)
        v_ref[0, 0] = v0.astype(BF16)
        v_ref[0, 1] = v1.astype(BF16)
        kt_ref[0] = kn.T.astype(BF16)
        vt_ref[0] = vp.T.astype(BF16)

    const = lambda a: pl.BlockSpec(a.shape, lambda b: (0,) * a.ndim)
    kv_spec = pl.BlockSpec((1, 2, cl, 128), lambda b: (b, 0, blk, 0))
    t_spec = pl.BlockSpec((1, 128, cl), lambda b: (b, 0, blk))
    anyspec = pl.BlockSpec(memory_space=pl.ANY)
    return pl.pallas_call(
        body, name="ctx_kv_prep", grid=(bsz,),
        in_specs=[pl.BlockSpec((1, cl, 256), lambda b: (b, 0, 0)), const(knw), const(bd128)] + [anyspec] * 4,
        out_specs=[kv_spec, kv_spec, t_spec, t_spec],
        out_shape=[jax.ShapeDtypeStruct(a.shape, BF16) for a in (k2, v2, kt, vt)],
        input_output_aliases={3: 0, 4: 1, 5: 2, 6: 3},
        compiler_params=_params(1),
    )(pc, knw, bd128, k2, v2, kt, vt)


def _attn_forward(q, k2, vt, tq, fused):
    bsz, s, _ = q.shape
    sk = k2.shape[2]
    nq = s // tq
    total = bsz * N_KV * nq
    at_steps = [(0, True), (total // 4, True), (total - 1, False)]

    def body(*refs):
        (q_ref, k_ref, vt_ref), (o_ref, lse_ref), _ = _split_fused(refs, 3, 2, 0, fused)
        g = pl.program_id(1)
        step = (pl.program_id(0) * N_KV + g) * nq + pl.program_id(2)
        _run_phases(fused, step, at_steps, True)
        kk = k_ref[0, 0]
        lo_mask = _lo_mask(tq)
        vt_aug = jnp.concatenate([vt_ref[0, pl.ds(pl.multiple_of(g * HEAD_DIM, HEAD_DIM), HEAD_DIM), :],
                                  jnp.ones((16, sk), BF16)], axis=0)
        ps, ms = [], []
        for j in range(2):
            qp = q_ref[0, :, 128 * j:128 * (j + 1)]
            for half in range(2):
                sel = lo_mask if half == 0 else jnp.logical_not(lo_mask)
                qs = jnp.where(sel, qp, jnp.zeros_like(qp))
                sc = lax.dot_general(qs, kk, NT, preferred_element_type=F32)
                m = jnp.max(sc, axis=-1, keepdims=True)
                ps.append(jnp.exp(sc - m).astype(BF16))
                ms.append(m)
        ots = [lax.dot_general(vt_aug, p, NT, preferred_element_type=F32) for p in ps]
        for j in range(2):
            o_t, l_t = [], []
            for half in range(2):
                ot = ots[2 * j + half]
                l = ot[HEAD_DIM:HEAD_DIM + 1, :]
                o_t.append(ot[0:HEAD_DIM, :] / l)
                l_t.append(jnp.broadcast_to(l, (HEAD_DIM, tq)))
            o_ref[0, :, 128 * j:128 * (j + 1)] = jnp.concatenate(o_t, axis=0).T
            lse_ref[0, :, 128 * j:128 * (j + 1)] = (jnp.where(lo_mask, ms[2 * j], ms[2 * j + 1])
                                                    + jnp.log(jnp.concatenate(l_t, axis=0).T))
        _run_phases(fused, step, at_steps, False)

    q_spec = pl.BlockSpec((1, tq, 256), lambda b, g, i: (b, i, g))
    kv_spec = pl.BlockSpec((1, 1, sk, 128), lambda b, g, i: (b, g, 0, 0))
    return pl.pallas_call(
        body, name="attn_forward", grid=(bsz, N_KV, nq),
        in_specs=[q_spec, kv_spec, pl.BlockSpec((1, 128, sk), lambda b, g, i: (b, 0, 0))] + fused.in_specs,
        out_specs=[q_spec, q_spec] + fused.out_specs,
        out_shape=[jax.ShapeDtypeStruct((bsz, s, 512), F32)] * 2 + fused.out_shape,
        scratch_shapes=fused.scratch,
        compiler_params=_params(3),
    )(q, k2, vt, *fused.arrs)


def _halo_specs(width, ts, s):
    r = ts // HALO
    last = s // HALO - 1
    return [pl.BlockSpec((1, ts, width), lambda b, i: (b, i, 0)),
            pl.BlockSpec((1, HALO, width), lambda b, i: (b, jnp.maximum(i * r - 1, 0), 0)),
            pl.BlockSpec((1, HALO, width), lambda b, i: (b, jnp.minimum((i + 1) * r, last), 0))]


def _fill_ext(ext_ref, cur, prev, nxt, i, n_tiles, ts):
    ext_ref[0:HALO, :] = jnp.where(i > 0, prev, jnp.zeros_like(prev))
    ext_ref[HALO:HALO + ts, :] = cur
    ext_ref[HALO + ts:2 * HALO + ts, :] = jnp.where(i < n_tiles - 1, nxt, jnp.zeros_like(nxt))


def _fill_shifted(sh_ref, ext_ref, ts):
    n = ts + 2 * HALO - 8
    for r in range(1, 8):
        sh_ref[r - 1, 0:n, :] = ext_ref[pl.ds(r, n), :]


def _window(sh_ref, ext_ref, off, rows, r0=0):
    q, r = divmod(off, 8)
    if r == 0:
        return ext_ref[pl.ds(r0 + off, rows), :]
    return sh_ref[r - 1, pl.ds(r0 + 8 * q, rows), :]


def _conv_forward(ga, gg, conv_w, conv_b, ln_w, ln_b, w_pw, b_pw, ts):
    bsz, s, dc = ga.shape
    n_tiles = s // ts

    def body(a_ref, ap_ref, an_ref, g_ref, gp_ref, gn_ref, cw_ref, cb_ref, lw_ref, lb_ref, wp_ref, bp_ref,
             y_ref, cv_ref, ext_ref, sh_ref):
        i = pl.program_id(1)
        glu = lambda a, g: a * _sigmoid(g)
        _fill_ext(ext_ref, glu(a_ref[0], g_ref[0]), glu(ap_ref[0], gp_ref[0]), glu(an_ref[0], gn_ref[0]), i, n_tiles, ts)
        _fill_shifted(sh_ref, ext_ref, ts)
        acc = jnp.broadcast_to(cb_ref[...], (ts, dc))
        for j in range(CONV_WIDTH):
            acc = acc + cw_ref[j:j + 1, :] * _window(sh_ref, ext_ref, HALO - CONV_PAD + j, ts)
        y_ref[0] = acc
        mu = jnp.mean(acc, axis=-1, keepdims=True)
        yc = acc - mu
        var = jnp.mean(yc * yc, axis=-1, keepdims=True)
        yn = yc * lax.rsqrt(var + EPS) * lw_ref[...] + lb_ref[...]
        ys = yn * _sigmoid(yn)
        cv_ref[0] = jnp.dot(ys.astype(BF16), wp_ref[...], preferred_element_type=F32) + bp_ref[...]

    const = lambda a: pl.BlockSpec(a.shape, lambda b, i: (0,) * a.ndim)
    return pl.pallas_call(
        body, name="conv_forward", grid=(bsz, n_tiles),
        in_specs=_halo_specs(dc, ts, s) + _halo_specs(dc, ts, s)
        + [const(conv_w), const(conv_b), const(ln_w), const(ln_b), const(w_pw), const(b_pw)],
        out_specs=[pl.BlockSpec((1, ts, dc), lambda b, i: (b, i, 0))] * 2,
        out_shape=[jax.ShapeDtypeStruct((bsz, s, dc), F32)] * 2,
        scratch_shapes=[pltpu.VMEM((ts + 2 * HALO, dc), F32), pltpu.VMEM((7, ts + 2 * HALO, dc), F32)],
        compiler_params=_params(2),
    )(ga, ga, ga, gg, gg, gg, conv_w, conv_b, ln_w, ln_b, w_pw, b_pw)


def _outproj_loss(x, target, gate, o, za, cv, zc, w_out, tm):
    bsz, s, d = x.shape

    def body(x_ref, t_ref, gate_ref, o_ref, za_ref, cv_ref, zc_ref, w_ref,
             loss_ref, dh_ref, do_ref, dza_ref, dcv_ref, dzc_ref, dgate_ref, gw_ref):
        b, i = pl.program_id(0), pl.program_id(1)
        ov, cvv = o_ref[0], cv_ref[0]
        silu_a, dsilu_a = _silu_and_grad(za_ref[0])
        silu_c, dsilu_c = _silu_and_grad(zc_ref[0])
        mix = jnp.concatenate([ov * silu_a, cvv * silu_c], axis=1).astype(BF16)
        out = jnp.dot(mix, w_ref[...], preferred_element_type=F32)
        gate_v = gate_ref[0]
        err = x_ref[0] + gate_v * out - t_ref[0]
        dh = err * (1.0 / d)
        dh_ref[0] = dh
        dout = (dh * gate_v).astype(BF16)
        dmix = lax.dot_general(dout, w_ref[...], NT, preferred_element_type=F32)
        gw = lax.dot_general(mix, dout, TN, preferred_element_type=F32)
        dg = jnp.sum(dh * out, axis=0, keepdims=True)
        sq = jnp.sum(err * err)

        @pl.when(jnp.logical_and(b == 0, i == 0))
        def _():
            gw_ref[...] = gw

        @pl.when(jnp.logical_or(b > 0, i > 0))
        def _():
            gw_ref[...] += gw

        @pl.when(i == 0)
        def _():
            dgate_ref[0] = dg
            loss_ref[...] = jnp.zeros(loss_ref.shape, F32) + sq

        @pl.when(i > 0)
        def _():
            dgate_ref[0] += dg
            loss_ref[...] += sq

        dma, dmc = dmix[:, :D_ATTN], dmix[:, D_ATTN:]
        do_ref[0] = dma * silu_a
        dza_ref[0] = (dma * ov * dsilu_a).astype(BF16)
        dcv_ref[0] = dmc * silu_c
        dzc_ref[0] = (dmc * cvv * dsilu_c).astype(BF16)

    tok = lambda w: pl.BlockSpec((1, tm, w), lambda b, i: (b, i, 0))
    return pl.pallas_call(
        body, name="outproj_loss", grid=(bsz, s // tm),
        in_specs=[tok(d), tok(d), _bcast_spec(gate), tok(512), tok(512), tok(512), tok(512),
                  pl.BlockSpec(w_out.shape, lambda b, i: (0, 0))],
        out_specs=[pl.BlockSpec((1, 8, 128), lambda b, i: (b, 0, 0)), tok(d), tok(512), tok(512), tok(512), tok(512),
                   pl.BlockSpec((1, 1, d), lambda b, i: (b, 0, 0)), pl.BlockSpec((d, d), lambda b, i: (0, 0))],
        out_shape=[jax.ShapeDtypeStruct((bsz, 8, 128), F32), jax.ShapeDtypeStruct((bsz, s, d), F32),
                   jax.ShapeDtypeStruct((bsz, s, 512), F32), jax.ShapeDtypeStruct((bsz, s, 512), BF16),
                   jax.ShapeDtypeStruct((bsz, s, 512), F32), jax.ShapeDtypeStruct((bsz, s, 512), BF16),
                   jax.ShapeDtypeStruct((bsz, 1, d), F32), jax.ShapeDtypeStruct((d, d), F32)],
        compiler_params=_params(2),
    )(x, target, gate, o, za, cv, zc, w_out)


def _conv_token_backward(dcv, y, ln_w, ln_b, w_pw, tm):
    bsz, s, dc = dcv.shape

    def body(dcv_ref, y_ref, lw_ref, lb_ref, wp_ref, dy_ref, gwp_ref, st_ref):
        b, i = pl.program_id(0), pl.program_id(1)
        yv, dcvv = y_ref[0], dcv_ref[0]
        mu = jnp.mean(yv, axis=-1, keepdims=True)
        yc = yv - mu
        rstd = lax.rsqrt(jnp.mean(yc * yc, axis=-1, keepdims=True) + EPS)
        yhat = yc * rstd
        yn = yhat * lw_ref[...] + lb_ref[...]
        ys, dsilu = _silu_and_grad(yn)
        dcvb = dcvv.astype(BF16)
        gwp = lax.dot_general(ys.astype(BF16), dcvb, TN, preferred_element_type=F32)
        dys = lax.dot_general(dcvb, wp_ref[...], NT, preferred_element_type=F32)
        dyn = dys * dsilu
        dyhat = dyn * lw_ref[...]
        dy = rstd * (dyhat - jnp.mean(dyhat, axis=-1, keepdims=True)
                     - yhat * jnp.mean(dyhat * yhat, axis=-1, keepdims=True))
        dy_ref[0] = dy
        red = lambda v: jnp.sum(v, axis=0, keepdims=True)
        stats = jnp.concatenate([red(dcvv), red(dyn * yhat), red(dyn), red(dy), jnp.zeros((4, dc), F32)], axis=0)
        first = jnp.logical_and(b == 0, i == 0)

        @pl.when(first)
        def _():
            gwp_ref[...] = gwp
            st_ref[...] = stats

        @pl.when(jnp.logical_not(first))
        def _():
            gwp_ref[...] += gwp
            st_ref[...] += stats

    tok = pl.BlockSpec((1, tm, dc), lambda b, i: (b, i, 0))
    const = lambda a: pl.BlockSpec(a.shape, lambda b, i: (0,) * a.ndim)
    return pl.pallas_call(
        body, name="conv_token_backward", grid=(bsz, s // tm),
        in_specs=[tok, tok, const(ln_w), const(ln_b), const(w_pw)],
        out_specs=[tok, pl.BlockSpec((dc, dc), lambda b, i: (0, 0)), pl.BlockSpec((8, dc), lambda b, i: (0, 0))],
        out_shape=[jax.ShapeDtypeStruct((bsz, s, dc), F32), jax.ShapeDtypeStruct((dc, dc), F32),
                   jax.ShapeDtypeStruct((8, dc), F32)],
        compiler_params=_params(2),
    )(dcv, y, ln_w, ln_b, w_pw)


def _conv_backward(dy, ga, gg, conv_w, ts):
    bsz, s, dc = dy.shape
    n_tiles = s // ts

    def body(dy_ref, dyp_ref, dyn_ref, a_ref, g_ref, cw_ref,
             da_ref, dg_ref, gcw_ref, dyext_ref, dysh_ref, ug_ref, dug_ref, gacc_ref):
        b, i = pl.program_id(0), pl.program_id(1)
        av, sg = a_ref[0], _sigmoid(g_ref[0])
        ug_ref[...] = av * sg
        _fill_ext(dyext_ref, dy_ref[0], dyp_ref[0], dyn_ref[0], i, n_tiles, ts)
        _fill_shifted(dysh_ref, dyext_ref, ts)
        gacc_ref[...] = jnp.zeros(gacc_ref.shape, F32)

        def row_block(r, carry):
            r0 = pl.multiple_of(r * CONV_ROWS, CONV_ROWS)
            ugb = ug_ref[pl.ds(r0, CONV_ROWS), :]
            acc = jnp.zeros((CONV_ROWS, dc), F32)
            for j in range(CONV_WIDTH):
                win = _window(dysh_ref, dyext_ref, HALO + CONV_PAD - j, CONV_ROWS, r0)
                acc = acc + cw_ref[j:j + 1, :] * win
                prod = ugb * win
                part = prod[0:8, :]
                for k in range(8, CONV_ROWS, 8):
                    part = part + prod[k:k + 8, :]
                gacc_ref[j] += part
            dug_ref[pl.ds(r0, CONV_ROWS), :] = acc
            return carry

        lax.fori_loop(0, ts // CONV_ROWS, row_block, 0)
        dug = dug_ref[...]
        gcw = jnp.sum(gacc_ref[...], axis=1)
        first = jnp.logical_and(b == 0, i == 0)

        @pl.when(first)
        def _():
            gcw_ref[...] = gcw

        @pl.when(jnp.logical_not(first))
        def _():
            gcw_ref[...] += gcw

        da_ref[0] = (dug * sg).astype(BF16)
        dg_ref[0] = (dug * av * sg * (1.0 - sg)).astype(BF16)

    tok = pl.BlockSpec((1, ts, dc), lambda b, i: (b, i, 0))
    return pl.pallas_call(
        body, name="conv_backward", grid=(bsz, n_tiles),
        in_specs=_halo_specs(dc, ts, s) + [tok, tok, pl.BlockSpec(conv_w.shape, lambda b, i: (0, 0))],
        out_specs=[tok, tok, pl.BlockSpec((32, dc), lambda b, i: (0, 0))],
        out_shape=[jax.ShapeDtypeStruct((bsz, s, dc), BF16), jax.ShapeDtypeStruct((bsz, s, dc), BF16),
                   jax.ShapeDtypeStruct((32, dc), F32)],
        scratch_shapes=[pltpu.VMEM((ts + 2 * HALO, dc), F32), pltpu.VMEM((7, ts + 2 * HALO, dc), F32),
                        pltpu.VMEM((ts, dc), F32), pltpu.VMEM((ts, dc), F32), pltpu.VMEM((32, 8, dc), F32)],
        compiler_params=_params(2),
    )(dy, dy, dy, ga, gg, conv_w)


def _attn_backward(q, k2, v2, kt, o, do, lse, tq, fused):
    bsz, s, _ = q.shape
    sk = k2.shape[2]
    scale = 1.0 / math.sqrt(HEAD_DIM)
    nq = s // tq
    total = bsz * N_KV * nq
    at_steps = [(0, True), (total // 5, True), (total // 2, True), (total - 1, False)]

    def body(*refs):
        (q_ref, k_ref, v_ref, kt_ref, o_ref, do_ref, lse_ref), (dq_ref, dk_ref, dv_ref), _ = _split_fused(
            refs, 7, 3, 0, fused)
        g, i = pl.program_id(1), pl.program_id(2)
        step = (pl.program_id(0) * N_KV + g) * nq + i
        _run_phases(fused, step, at_steps, True)
        kk, vv = k_ref[0, 0], v_ref[0, 0]
        kgt = kt_ref[0, pl.ds(pl.multiple_of(g * HEAD_DIM, HEAD_DIM), HEAD_DIM), :]
        lo_mask = _lo_mask(tq)
        dk_acc = jnp.zeros((HEAD_DIM, sk), F32)
        dv_acc = jnp.zeros((HEAD_DIM, sk), F32)
        for j in range(2):
            cols = slice(128 * j, 128 * (j + 1))
            qp, dop, lsep = q_ref[0, :, cols], do_ref[0, :, cols], lse_ref[0, :, cols]
            dprod = dop * o_ref[0, :, cols]
            q_t = qp.astype(F32).T.astype(BF16)
            do_t = dop.T.astype(BF16)
            dq_t = []
            for half in range(2):
                sel = lo_mask if half == 0 else jnp.logical_not(lo_mask)
                rows = slice(HEAD_DIM * half, HEAD_DIM * (half + 1))
                qs = jnp.where(sel, qp, jnp.zeros_like(qp))
                dos = jnp.where(sel, dop, 0.0).astype(BF16)
                lse_h = jnp.max(jnp.where(sel, lsep, -jnp.inf), axis=-1, keepdims=True)
                delta = jnp.sum(jnp.where(sel, dprod, 0.0), axis=-1, keepdims=True)
                sc = lax.dot_general(qs, kk, NT, preferred_element_type=F32)
                p = jnp.exp(sc - lse_h)
                dp = lax.dot_general(dos, vv, NT, preferred_element_type=F32)
                ds = (p * (dp - delta)).astype(BF16)
                dv_acc = dv_acc + jnp.dot(do_t[rows, :], p.astype(BF16), preferred_element_type=F32)
                dk_acc = dk_acc + jnp.dot(q_t[rows, :], ds, preferred_element_type=F32)
                dq_t.append(lax.dot_general(kgt, ds, NT, preferred_element_type=F32))
            dq_ref[0, :, cols] = (jnp.concatenate(dq_t, axis=0) * scale).T

        @pl.when(i == 0)
        def _():
            dk_ref[0, 0] = dk_acc
            dv_ref[0, 0] = dv_acc

        @pl.when(i > 0)
        def _():
            dk_ref[0, 0] += dk_acc
            dv_ref[0, 0] += dv_acc

        _run_phases(fused, step, at_steps, False)

    q_spec = pl.BlockSpec((1, tq, 256), lambda b, g, i: (b, i, g))
    kv_spec = pl.BlockSpec((1, 1, sk, 128), lambda b, g, i: (b, g, 0, 0))
    acc_spec = pl.BlockSpec((1, 1, HEAD_DIM, sk), lambda b, g, i: (b, g, 0, 0))
    return pl.pallas_call(
        body, name="attn_backward", grid=(bsz, N_KV, nq),
        in_specs=[q_spec, kv_spec, kv_spec, pl.BlockSpec((1, 128, sk), lambda b, g, i: (b, 0, 0)), q_spec, q_spec,
                  q_spec] + fused.in_specs,
        out_specs=[q_spec, acc_spec, acc_spec] + fused.out_specs,
        out_shape=[jax.ShapeDtypeStruct((bsz, s, 512), F32), jax.ShapeDtypeStruct((bsz, 2, HEAD_DIM, sk), F32),
                   jax.ShapeDtypeStruct((bsz, 2, HEAD_DIM, sk), F32)] + fused.out_shape,
        scratch_shapes=fused.scratch,
        compiler_params=_params(3),
    )(q, k2, v2, kt, o, do, lse, *fused.arrs)


def _heads_to_lanes(acc_ref):
    return jnp.concatenate([acc_ref[0, 0], acc_ref[0, 1]], axis=0).T


def _norm_backward(dn, pre, w, bd):
    rstd = lax.rsqrt(_seg_mean(pre * pre, bd) + EPS)
    xhat = pre * rstd
    dxhat = dn * w
    return rstd * (dxhat - xhat * _seg_mean(dxhat * xhat, bd)), dn * xhat


def _qkv_backward(qkv, dq, dk2, dv2, cos, sin, qnw, knw, bd512, bd128, ts):
    bsz, s, _ = qkv.shape

    def body(p_ref, dq_ref, dk_ref, dv_ref, cos_ref, sin_ref, qnw_ref, knw_ref, bd512_ref, bd128_ref, d_ref, gw_ref):
        b, i = pl.program_id(0), pl.program_id(1)
        lo_mask = _lo_mask(ts)
        cos_t, sin_t = cos_ref[...], sin_ref[...]
        dqr = dq_ref[0]
        dqn = dqr * _tile_lanes(cos_t, 4) + _partner(dqr * _tile_lanes(sin_t, 4))
        dqp, gq = _norm_backward(dqn, p_ref[0, :, 0:512], qnw_ref[...], bd512_ref[...])
        dkr = _heads_to_lanes(dk_ref)
        dkn = dkr * cos_t + _partner(dkr * sin_t)
        dkp, gk = _norm_backward(dkn, p_ref[0, :, 512:640], knw_ref[...], bd128_ref[...])
        dvp = _heads_to_lanes(dv_ref)
        d_ref[0] = jnp.concatenate([dqp, dkp, dvp], axis=1).astype(BF16)
        gk512 = jnp.concatenate([jnp.sum(gk, axis=0, keepdims=True), jnp.zeros((1, 384), F32)], axis=1)
        rows = jnp.concatenate([jnp.sum(gq, axis=0, keepdims=True), gk512, jnp.zeros((6, 512), F32)], axis=0)
        first = jnp.logical_and(b == 0, i == 0)

        @pl.when(first)
        def _():
            gw_ref[...] = rows

        @pl.when(jnp.logical_not(first))
        def _():
            gw_ref[...] += rows

    const = lambda a: pl.BlockSpec(a.shape, lambda b, i: (0,) * a.ndim)
    kv_spec = pl.BlockSpec((1, 2, HEAD_DIM, ts), lambda b, i: (b, 0, 0, i))
    return pl.pallas_call(
        body, name="qkv_backward", grid=(bsz, s // ts),
        in_specs=[pl.BlockSpec((1, ts, 768), lambda b, i: (b, i, 0)), pl.BlockSpec((1, ts, 512), lambda b, i: (b, i, 0)),
                  kv_spec, kv_spec, pl.BlockSpec((ts, 128), lambda b, i: (i, 0)),
                  pl.BlockSpec((ts, 128), lambda b, i: (i, 0)), const(qnw), const(knw), const(bd512), const(bd128)],
        out_specs=[pl.BlockSpec((1, ts, 768), lambda b, i: (b, i, 0)), pl.BlockSpec((8, 512), lambda b, i: (0, 0))],
        out_shape=[jax.ShapeDtypeStruct((bsz, s, 768), BF16), jax.ShapeDtypeStruct((8, 512), F32)],
        compiler_params=_params(2),
    )(qkv, dq, dk2, dv2, cos, sin, qnw, knw, bd512, bd128)


def _ctx_kv_backward(pc, dk2, dv2, knw, bd128):
    bsz, cl, _ = pc.shape

    def body(p_ref, dk_ref, dv_ref, knw_ref, bd128_ref, d_ref, gw_ref):
        b = pl.program_id(0)
        lo_mask = _lo_mask(cl)
        dkn = _heads_to_lanes(dk_ref)
        dkp, gk = _norm_backward(dkn, p_ref[0, :, 0:128], knw_ref[...], bd128_ref[...])
        dvp = _heads_to_lanes(dv_ref)
        d_ref[0] = jnp.concatenate([dkp, dvp], axis=1).astype(BF16)
        rows = jnp.concatenate([jnp.sum(gk, axis=0, keepdims=True), jnp.zeros((7, 128), F32)], axis=0)

        @pl.when(b == 0)
        def _():
            gw_ref[...] = rows

        @pl.when(b > 0)
        def _():
            gw_ref[...] += rows

    const = lambda a: pl.BlockSpec(a.shape, lambda b: (0,) * a.ndim)
    blk = dk2.shape[3] // cl - 1
    kv_spec = pl.BlockSpec((1, 2, HEAD_DIM, cl), lambda b: (b, 0, 0, blk))
    return pl.pallas_call(
        body, name="ctx_kv_backward", grid=(bsz,),
        in_specs=[pl.BlockSpec((1, cl, 256), lambda b: (b, 0, 0)), kv_spec, kv_spec, const(knw), const(bd128)],
        out_specs=[pl.BlockSpec((1, cl, 256), lambda b: (b, 0, 0)), pl.BlockSpec((8, 128), lambda b: (0, 0))],
        out_shape=[jax.ShapeDtypeStruct((bsz, cl, 256), BF16), jax.ShapeDtypeStruct((8, 128), F32)],
        compiler_params=_params(1),
    )(pc, dk2, dv2, knw, bd128)


def _weight_grad(parts, u, init, tm, name):
    bsz, s, d = u.shape
    n_p = len(parts)
    nrows = sum(hi - lo for _, lo, hi in parts)

    def body(*refs):
        p_refs, u_ref = refs[:n_p], refs[n_p]
        gi_ref = refs[n_p + 1] if init is not None else None
        gw_ref = refs[-1]
        first = jnp.logical_and(pl.program_id(0) == 0, pl.program_id(1) == 0)
        dp = jnp.concatenate([r[0, :, lo:hi] for r, (_, lo, hi) in zip(p_refs, parts)], axis=1)
        gw = lax.dot_general(dp, u_ref[0], TN, preferred_element_type=F32)

        @pl.when(first)
        def _():
            gw_ref[...] = gw
            if init is not None:
                gw_ref[KV_LO:KV_HI, :] += gi_ref[...]

        @pl.when(jnp.logical_not(first))
        def _():
            gw_ref[...] += gw

    tok = lambda w: pl.BlockSpec((1, tm, w), lambda b, i: (b, i, 0))
    in_specs = [tok(a.shape[2]) for a, _, _ in parts] + [tok(d)]
    args = [a for a, _, _ in parts] + [u]
    if init is not None:
        in_specs.append(pl.BlockSpec(init.shape, lambda b, i: (0, 0)))
        args.append(init)
    return pl.pallas_call(
        body, name=name, grid=(bsz, s // tm), in_specs=in_specs,
        out_specs=pl.BlockSpec((nrows, d), lambda b, i: (0, 0)), out_shape=jax.ShapeDtypeStruct((nrows, d), F32),
        compiler_params=_params(2),
    )(*args)


def _inproj_backward(dps, x, dh, scale1p, norm_w, w_t, tm, name, fused=None):
    bsz, s, d = x.shape
    n_p = len(dps)
    shared = scale1p.shape[0] == 1
    with_dx = dh is not None
    n_in = n_p + (2 if with_dx else 1) + 3
    n_out = 3 if with_dx else 2
    total = bsz * (s // tm)
    at_steps = [(0, True), (total // 8, True), ((3 * total) // 4, True), (total - 1, False)]

    def body(*refs):
        ins, outs, _ = _split_fused(refs, n_in, n_out, 0, fused)
        dp_refs, x_ref = ins[:n_p], ins[n_p]
        dh_ref = ins[n_p + 1] if with_dx else None
        sc_ref, nw_ref, w_ref = ins[-3:]
        mod_ref, gnw_ref = outs[-2:]
        b, i = pl.program_id(0), pl.program_id(1)
        step = b * (s // tm) + i
        _run_phases(fused, step, at_steps, True)
        first = jnp.logical_and(b == 0, i == 0)
        dp = dp_refs[0][0] if n_p == 1 else jnp.concatenate([r[0] for r in dp_refs], axis=1)
        du = jnp.dot(dp, w_ref[...], preferred_element_type=F32)
        xv = x_ref[0]
        rstd = lax.rsqrt(jnp.mean(xv * xv, axis=-1, keepdims=True) + EPS)
        xhat = xv * rstd
        nw, sc = nw_ref[...], sc_ref[0]
        red = lambda v: jnp.sum(v, axis=0, keepdims=True)
        mod_rows = jnp.concatenate([red(du), red(du * (xhat * nw)), jnp.zeros((6, d), F32)], axis=0)
        gnw_rows = jnp.concatenate([red(du * sc * xhat), jnp.zeros((7, d), F32)], axis=0)
        mod_first = first if shared else i == 0

        @pl.when(mod_first)
        def _():
            mod_ref[0] = mod_rows

        @pl.when(jnp.logical_not(mod_first))
        def _():
            mod_ref[0] += mod_rows

        @pl.when(first)
        def _():
            gnw_ref[...] = gnw_rows

        @pl.when(jnp.logical_not(first))
        def _():
            gnw_ref[...] += gnw_rows

        if with_dx:
            dxhat = du * (nw * sc)
            outs[0][0] = dh_ref[0] + rstd * (dxhat - xhat * jnp.mean(dxhat * xhat, axis=-1, keepdims=True))
        _run_phases(fused, step, at_steps, False)

    tok = lambda w: pl.BlockSpec((1, tm, w), lambda b, i: (b, i, 0))
    in_specs = [tok(p.shape[2]) for p in dps] + [tok(d)]
    args = list(dps) + [x]
    if with_dx:
        in_specs.append(tok(d))
        args.append(dh)
    in_specs += [_bcast_spec(scale1p), pl.BlockSpec((1, d), lambda b, i: (0, 0)),
                 pl.BlockSpec(w_t.shape, lambda b, i: (0, 0))]
    args += [scale1p, norm_w, w_t]
    bm = scale1p.shape[0]
    mod_spec = pl.BlockSpec((1, 8, d), (lambda b, i: (0, 0, 0)) if shared else (lambda b, i: (b, 0, 0)))
    out_specs = [mod_spec, pl.BlockSpec((8, d), lambda b, i: (0, 0))]
    out_shape = [jax.ShapeDtypeStruct((bm, 8, d), F32), jax.ShapeDtypeStruct((8, d), F32)]
    if with_dx:
        out_specs.insert(0, tok(d))
        out_shape.insert(0, jax.ShapeDtypeStruct((bsz, s, d), F32))
    scratch = []
    if fused is not None:
        in_specs += fused.in_specs
        args += fused.arrs
        out_specs += fused.out_specs
        out_shape += fused.out_shape
        scratch = fused.scratch
    res = pl.pallas_call(
        body, name=name, grid=(bsz, s // tm), in_specs=in_specs, out_specs=out_specs, out_shape=out_shape,
        scratch_shapes=scratch, compiler_params=_params(2),
    )(*args)
    return list(res) if with_dx else [None] + list(res)


def _adamw_update(w_ref, g_ref, m_ref, v_ref, d_ref, nm_ref, nv_ref):
    gv = g_ref[...]
    mn = ADAM_B1 * m_ref[...] + (1.0 - ADAM_B1) * gv
    vn = ADAM_B2 * v_ref[...] + (1.0 - ADAM_B2) * (gv * gv)
    m_hat = mn / (1.0 - ADAM_B1 ** ADAM_STEP)
    v_hat = vn / (1.0 - ADAM_B2 ** ADAM_STEP)
    d_ref[...] = -ADAM_LR * (m_hat / (jnp.sqrt(v_hat) + ADAM_EPS) + ADAM_WD * w_ref[...])
    nm_ref[...] = mn
    nv_ref[...] = vn


def _adamw_small(ws, gs, ms, vs):
    n = len(ws)

    def body(*refs):
        ins, outs = refs[:4 * n], refs[4 * n:]
        for k in range(n):
            _adamw_update(ins[k], ins[n + k], ins[2 * n + k], ins[3 * n + k], outs[3 * k], outs[3 * k + 1],
                          outs[3 * k + 2])

    res = pl.pallas_call(
        body, name="adamw_small",
        out_shape=[jax.ShapeDtypeStruct(w.shape, F32) for w in ws for _ in range(3)], compiler_params=_params(),
    )(*ws, *gs, *ms, *vs)
    return [tuple(res[3 * k:3 * k + 3]) for k in range(n)]


def _adamw(w, g, m, v, name):
    r, cdim = w.shape
    tr = next((t for t in (512, 352) if r % t == 0 and r > t), r)

    def body(*refs):
        _adamw_update(*refs)

    spec = pl.BlockSpec((tr, cdim), lambda i: (i, 0))
    return pl.pallas_call(
        body, name=name, grid=(r // tr,), in_specs=[spec] * 4, out_specs=[spec] * 3,
        out_shape=[jax.ShapeDtypeStruct((r, cdim), F32)] * 3, compiler_params=_params(1),
    )(w, g, m, v)


def _rope_tables(s):
    rows = s // GRID_W
    freqs = np.float32(ROPE_THETA) ** (-np.arange(0, ROPE_AXIS_DIM, 2, dtype=np.float32) / np.float32(ROPE_AXIS_DIM))
    ang_r = np.arange(rows, dtype=np.float32)[:, None] * freqs[None, :]
    ang_c = np.arange(GRID_W, dtype=np.float32)[:, None] * freqs[None, :]
    zr, zc = np.zeros_like(ang_r), np.zeros_like(ang_c)

    def table(by_row, by_col):
        r = jnp.asarray(np.tile(np.concatenate(by_row + [zr, zr], axis=1), (1, 2)), dtype=F32)
        c = jnp.asarray(np.tile(np.concatenate([zc, zc] + by_col, axis=1), (1, 2)), dtype=F32)
        return jnp.repeat(r, GRID_W, axis=0) + jnp.tile(c, (rows, 1))

    return (table([np.cos(ang_r)] * 2, [np.cos(ang_c)] * 2),
            table([-np.sin(ang_r), np.sin(ang_r)], [-np.sin(ang_c), np.sin(ang_c)]))


def _pack_rows(parts, rows):
    flat = jnp.concatenate([p.reshape(-1) for p in parts])
    return jnp.pad(flat, (0, rows * D_MODEL - flat.shape[0])).reshape(rows, D_MODEL)


def kernel(x, c, ctx, c_ctx, w_mod, b_mod, norm_w, w_in, q_norm_w, k_norm_w, conv_w, conv_b, conv_ln_w, conv_ln_b, w_pw, b_pw, w_out, loss_target, m_c_ctx, m_w_mod, m_b_mod, m_norm_w, m_w_in, m_q_norm_w, m_k_norm_w, m_conv_w, m_conv_b, m_conv_ln_w, m_conv_ln_b, m_w_pw, m_b_pw, m_w_out, v_c_ctx, v_w_mod, v_b_mod, v_norm_w, v_w_in, v_q_norm_w, v_k_norm_w, v_conv_w, v_conv_b, v_conv_ln_w, v_conv_ln_b, v_w_pw, v_b_pw, v_w_out):
    bsz, s, d = x.shape
    cl = ctx.shape[1]
    xi, yi, ci = lax.axis_index("x"), lax.axis_index("y"), lax.axis_index("c")
    chip = 2 * xi + yi
    dev = 2 * chip + ci
    ncol_mod = w_mod.shape[2]

    w_in_t_loc = w_in[0].T.astype(BF16)
    b_cols = lax.dynamic_slice(b_mod, (0, chip * ncol_mod), (1, ncol_mod))
    sc_rows, mod_g, g_in = _front(jnp.pad(c, ((0, 8 - bsz), (0, 0))), jnp.pad(c_ctx[None, :], ((0, 15), (0, 0))),
                                  w_mod[0], b_cols, w_in_t_loc)
    w_in_t = g_in.reshape(D_IN, d)
    mod_all = mod_g.transpose(1, 0, 2).reshape(80, 3 * d)
    mod_loc = lax.dynamic_slice(mod_all, (8 * dev, 0), (bsz, 3 * d))
    shift, scale1p, gate = mod_loc[:, None, :d], 1.0 + mod_loc[:, None, d:2 * d], mod_loc[:, None, 2 * d:]
    shift_c, scale1p_c = mod_all[64:65, :d][None], 1.0 + mod_all[64:65, d:2 * d][None]

    cos, sin = _rope_tables(s)
    qnw512 = jnp.tile(q_norm_w, (1, 8))
    knw128 = jnp.tile(k_norm_w, (1, 2))
    bd512 = jnp.kron(jnp.eye(8, dtype=F32), jnp.ones((HEAD_DIM, HEAD_DIM), F32)).astype(BF16)
    bd128 = bd512[:128, :128]

    tt, ti = min(TILE_TOKENS, s), min(TILE_INPROJ, s)
    u, p_qkv, p_za, p_ga, p_gg, p_zc = _norm_inproj(x, shift, scale1p, norm_w, w_in_t, SPLITS, ti, "norm_inproj")
    uc, pc_kv = _norm_inproj(ctx, shift_c, scale1p_c, norm_w, w_in_t[KV_LO:KV_HI], ((0, 256),), cl, "ctx_norm_inproj")
    q, k2x, v2x, ktx, vtx = _qkv_prep(p_qkv, cos, sin, qnw512, knw128, bd512, bd128, tt, cl)
    k2, v2, kt, vt = _ctx_kv_prep(pc_kv, knw128, bd128, k2x, v2x, ktx, vtx)
    conv_w_loc = jnp.pad(conv_w[0], ((0, 1), (0, 0)))
    o, lse, g_out, g_pw, g_cw = _attn_forward(
        q, k2, vt, min(TILE_ATTN_FWD, s), _ChipGather([w_out[0].astype(BF16), w_pw[0].astype(BF16), conv_w_loc]))
    w_out_f = g_out.reshape(d, d)
    w_pw_f = g_pw.reshape(D_CONV, D_CONV)
    conv_w_f = g_cw.transpose(1, 0, 2).reshape(32, D_CONV)
    y, cv = _conv_forward(p_ga, p_gg, conv_w_f, conv_b, conv_ln_w, conv_ln_b, w_pw_f, b_pw, tt)
    loss_part, dh, do, dza, dcv, dzc, dgate, gw_out = _outproj_loss(
        x, loss_target, gate, o, p_za, cv, p_zc, w_out_f, tt)

    all_chips, half_rows = (0, 1, 2, 3), D_IN // 2
    dy, gw_pw, conv_stats = _conv_token_backward(dcv, y, conv_ln_w, conv_ln_b, w_pw_f, tt)
    da, dg, gcw = _conv_backward(dy, p_ga, p_gg, conv_w_f, tt)
    gw_hi = _weight_grad([(da, half_rows - SPLITS[2][0], 512), (dg, 0, 512), (dzc, 0, 512)], u, None, ti,
                         "grad_in_rows_hi")
    dq, dkt, dvt, r_out, r_pw, r_hi = _attn_backward(
        q, k2, v2, kt, o, do, lse, min(TILE_ATTN_BWD, s),
        _FusedReduce([(gw_out, all_chips), (gw_pw, all_chips), (gw_hi, (2, 3))]))
    dqkv, qk_stats = _qkv_backward(p_qkv, dq, dkt, dvt, cos, sin, qnw512, knw128, bd512, bd128, tt)
    dpc, kc_stats = _ctx_kv_backward(pc_kv, dkt, dvt, knw128, bd128)
    gw_ctx = _weight_grad([(dpc, 0, 256)], uc, None, cl, "grad_in_rows_ctx")
    gw_lo = _weight_grad([(dqkv, 0, 768), (dza, 0, 512), (da, 0, half_rows - SPLITS[2][0])], u, gw_ctx, ti,
                         "grad_in_rows_lo")
    _, modc, gnw_c = _inproj_backward([dpc], ctx, None, scale1p_c, norm_w, w_in_t[KV_LO:KV_HI], cl,
                                      "ctx_inproj_backward")
    grad_x, modx, gnw_x, r_lo = _inproj_backward(
        [dqkv, dza, da, dg, dzc], x, dh, scale1p, norm_w, w_in_t, tt, "inproj_backward",
        _FusedReduce([(gw_lo, (0, 1))]))
    g_w_out, g_w_pw = r_out.reshape(d // 4, d), r_pw.reshape(D_CONV // 4, D_CONV)
    g_w_in_t = jnp.where(chip < 2, r_lo, r_hi).reshape(D_IN // 4, d)

    dmod_loc = jnp.concatenate([modx[:, 0, :], modx[:, 1, :], dgate[:, 0, :]], axis=1)
    gq = qk_stats[0].reshape(8, HEAD_DIM).sum(axis=0)
    gk = (qk_stats[1, :128] + kc_stats[0]).reshape(2, HEAD_DIM).sum(axis=0)
    packed = _pack_rows([dmod_loc, dmod_loc.sum(axis=0), gnw_x[0] + gnw_c[0], modc[0, 0], modc[0, 1], gq, gk,
                         conv_stats[0], conv_stats[1], conv_stats[2], conv_stats[3], gcw,
                         jnp.sum(loss_part[:, 0, 0])[None]], 32)
    total, g_w_mod, dsilu_ctx = _tail_exchange(packed, sc_rows, w_mod[0], bsz, 3 * bsz + 4)
    flat = total.reshape(-1)
    offs = [0]

    def take(nelem):
        lo = offs[0]
        offs[0] = lo + nelem
        return flat[lo:lo + nelem]

    take(bsz * 3 * d)
    g_b_mod_x = take(3 * d)
    g_norm_w = take(d)
    dshift_c, dscale_c = take(d), take(d)
    g_qnw, g_knw = take(HEAD_DIM), take(HEAD_DIM)
    g_b_pw, g_ln_w, g_ln_b, g_conv_b = take(D_CONV), take(D_CONV), take(D_CONV), take(D_CONV)
    g_conv_w_full = take(32 * D_CONV).reshape(32, D_CONV)
    loss = take(1)[0] * (0.5 / d)

    dmod_c = jnp.concatenate([dshift_c, dscale_c, jnp.zeros((d,), F32)])
    g_b_mod = (g_b_mod_x + dmod_c)[None, :]
    sg = _sigmoid(c_ctx)
    g_c_ctx = dsilu_ctx[0] * (sg * (1.0 + c_ctx * (1.0 - sg)))

    g_w_in = g_w_in_t.T
    g_conv_w = lax.dynamic_slice(g_conv_w_full, (0, chip * 128), (CONV_WIDTH, 128))

    grads = {
        "c_ctx": g_c_ctx, "w_mod": g_w_mod[None], "b_mod": g_b_mod, "norm_w": g_norm_w[None], "w_in": g_w_in[None],
        "q_norm_w": g_qnw[None], "k_norm_w": g_knw[None], "conv_w": g_conv_w[None], "conv_b": g_conv_b[None],
        "conv_ln_w": g_ln_w[None], "conv_ln_b": g_ln_b[None], "w_pw": g_w_pw[None], "b_pw": g_b_pw[None],
        "w_out": g_w_out[None],
    }
    weights = {
        "c_ctx": (c_ctx, m_c_ctx, v_c_ctx), "w_mod": (w_mod, m_w_mod, v_w_mod), "b_mod": (b_mod, m_b_mod, v_b_mod),
        "norm_w": (norm_w, m_norm_w, v_norm_w), "w_in": (w_in, m_w_in, v_w_in),
        "q_norm_w": (q_norm_w, m_q_norm_w, v_q_norm_w), "k_norm_w": (k_norm_w, m_k_norm_w, v_k_norm_w),
        "conv_w": (conv_w, m_conv_w, v_conv_w), "conv_b": (conv_b, m_conv_b, v_conv_b),
        "conv_ln_w": (conv_ln_w, m_conv_ln_w, v_conv_ln_w), "conv_ln_b": (conv_ln_b, m_conv_ln_b, v_conv_ln_b),
        "w_pw": (w_pw, m_w_pw, v_w_pw), "b_pw": (b_pw, m_b_pw, v_b_pw), "w_out": (w_out, m_w_out, v_w_out),
    }
    names = list(weights)
    big = ("w_mod", "w_in")
    as_2d = lambda a: a.reshape((1, a.shape[0]) if a.ndim == 1 else (a.shape[-2] if a.ndim == 3 else 1, a.shape[-1]))
    small = [n for n in names if n not in big]
    w_g_m_v = zip(*[[as_2d(a) for a in (weights[n][0], grads[n], weights[n][1], weights[n][2])] for n in small])
    updates = dict(zip(small, _adamw_small(*[list(col) for col in w_g_m_v])))
    w, m, v = weights["w_mod"]
    updates["w_mod"] = _adamw(as_2d(w), as_2d(grads["w_mod"]), as_2d(m), as_2d(v), "adamw_w_mod")
    w, m, v = weights["w_in"]
    updates["w_in"] = tuple(r.T for r in _adamw(w[0].T, g_w_in_t, m[0].T, v[0].T, "adamw_w_in"))
    deltas, new_ms, new_vs = ([updates[n][k].reshape(weights[n][0].shape) for n in names] for k in range(3))
    grads = {n: grads[n].reshape(weights[n][0].shape) for n in names}

    return (loss, grad_x, *[grads[n] for n in names], *deltas, *new_ms, *new_vs)
```

```python
import functools
import math

import jax
import jax.numpy as jnp
import numpy as np
from jax import lax
from jax.experimental import pallas as pl
from jax.experimental.pallas import tpu as pltpu

F32 = jnp.float32
BF16 = jnp.bfloat16
MESH = pl.DeviceIdType.MESH

D_MODEL = 1024
D_ATTN = 512
D_CONV = 512
HEAD_DIM = 64
N_KV = 2
GRID_W = 64
ROPE_AXIS_DIM = 32
ROPE_THETA = 10000.0
CONV_WIDTH = 31
CONV_PAD = 15
HALO = 16
CONV_ROWS = 32
EPS = 1e-6
SPLITS = ((0, 768), (768, 1280), (1280, 1792), (1792, 2304), (2304, 2816))
D_IN = 2816
KV_LO, KV_HI = 512, 768

ADAM_LR = 0.001
ADAM_B1 = 0.9
ADAM_B2 = 0.999
ADAM_EPS = 1e-08
ADAM_WD = 0.01
ADAM_STEP = 10

VMEM_LIMIT = 56 * 1024 * 1024

TILE_TOKENS = 512
TILE_INPROJ = 1024
TILE_ATTN_FWD = 1024
TILE_ATTN_BWD = 512

NT = (((1,), (1,)), ((), ()))
TN = (((0,), (0,)), ((), ()))


def _params(n_axes=0, **kw):
    if n_axes:
        kw["dimension_semantics"] = ("arbitrary",) * n_axes
    return pltpu.CompilerParams(vmem_limit_bytes=VMEM_LIMIT, **kw)


def _sigmoid(x):
    return 1.0 / (1.0 + jnp.exp(-x))


def _silu_and_grad(z):
    s = _sigmoid(z)
    return z * s, s * (1.0 + z * (1.0 - s))


def _seg_mean(v, ones_bd):
    hi = v.astype(BF16)
    lo = (v - hi.astype(F32)).astype(BF16)
    s = jnp.dot(hi, ones_bd, preferred_element_type=F32) + jnp.dot(lo, ones_bd, preferred_element_type=F32)
    return s * (1.0 / HEAD_DIM)


def _partner(v):
    n = v.shape[1]
    lane = lax.broadcasted_iota(jnp.int32, (v.shape[0], 128), 1)
    first = (lane % 32) < 16
    parts = []
    for k in range(n // 128):
        ch = v[:, 128 * k:128 * (k + 1)]
        parts.append(jnp.where(first, pltpu.roll(ch, 112, 1), pltpu.roll(ch, 16, 1)))
    return parts[0] if len(parts) == 1 else jnp.concatenate(parts, axis=1)


def _tile_lanes(t, reps):
    return t if reps == 1 else jnp.concatenate([t] * reps, axis=1)


def _lo_mask(rows):
    return lax.broadcasted_iota(jnp.int32, (rows, 128), 1) < HEAD_DIM


def _gather8_in_vmem(x_ref, out_ref, send_sems, recv_sems, local_sem):
    x, y, c = lax.axis_index("x"), lax.axis_index("y"), lax.axis_index("c")
    me, sibling = (x, y, c), (x, y, 1 - c)
    chips = [(1 - x, y), (x, 1 - y), (1 - x, 1 - y)]

    def slot(px, py, pc):
        return out_ref.at[4 * px + 2 * py + pc]

    def copy(k, block, to, src=None):
        return pltpu.make_async_remote_copy(
            src_ref=slot(*block) if src is None else src, dst_ref=slot(*block),
            send_sem=send_sems.at[k], recv_sem=recv_sems.at[k], device_id=to, device_id_type=MESH)

    mine = pltpu.make_async_copy(x_ref, slot(*me), local_sem)
    mine.start()
    first = [copy(0, me, sibling, src=x_ref)]
    first += [copy(1 + j, me, (*chip, c), src=x_ref) for j, chip in enumerate(chips)]
    for cp in first:
        cp.start()
    passed = [copy(4 + j, (*chip, c), sibling) for j, chip in enumerate(chips)]
    for j, chip in enumerate(chips):
        copy(1 + j, (*chip, c), me).wait_recv()
        passed[j].start()
    copy(0, sibling, me).wait_recv()
    for j, chip in enumerate(chips):
        copy(4 + j, (*chip, 1 - c), me).wait_recv()
    for cp in first + passed:
        cp.wait_send()
    mine.wait()


class _ChipGather:
    def __init__(self, arrs):
        self.arrs = list(arrs)
        n = self.n = len(self.arrs)
        self.in_specs = [pl.BlockSpec(memory_space=pl.ANY)] * n
        self.out_shape = [jax.ShapeDtypeStruct((4,) + a.shape, a.dtype) for a in self.arrs]
        self.out_specs = [pl.BlockSpec(memory_space=pl.ANY)] * n
        self.scratch = [pltpu.SemaphoreType.DMA((6 * n,)), pltpu.SemaphoreType.DMA((6 * n,)),
                        pltpu.SemaphoreType.DMA((n,))]
        self.phases = [self.start, self.forward, self.finish]

    def bind(self, ins, outs, scratch):
        self.ins, self.outs = ins, outs
        self.send_sems, self.recv_sems, self.local_sems = scratch
        self.x, self.y, self.c = lax.axis_index("x"), lax.axis_index("y"), lax.axis_index("c")
        self.chips = [(1 - self.x, self.y), (self.x, 1 - self.y), (1 - self.x, 1 - self.y)]
        self.mychip = 2 * self.x + self.y

    def _copy(self, a, k, chip_idx, cc, to, src=None):
        h = self.arrs[a].shape[0] // 2
        dst = self.outs[a].at[chip_idx, pl.ds(cc * h, h)]
        return pltpu.make_async_remote_copy(
            src_ref=dst if src is None else src, dst_ref=dst, send_sem=self.send_sems.at[6 * a + k],
            recv_sem=self.recv_sems.at[6 * a + k], device_id=to, device_id_type=MESH)

    def _local(self, a):
        return pltpu.make_async_copy(self.ins[a], self.outs[a].at[self.mychip], self.local_sems.at[a])

    def _first(self, a, j):
        h = self.arrs[a].shape[0] // 2
        return self._copy(a, j, self.mychip, self.c, (*self.chips[j], self.c), src=self.ins[a].at[pl.ds(self.c * h, h)])

    def _passed(self, a, j):
        cx, cy = self.chips[j]
        return self._copy(a, 3 + j, 2 * cx + cy, self.c, (self.x, self.y, 1 - self.c))

    def start(self):
        for a in range(self.n):
            self._local(a).start()
            for j in range(3):
                self._first(a, j).start()

    def forward(self):
        for a in range(self.n):
            for j, (cx, cy) in enumerate(self.chips):
                self._copy(a, j, 2 * cx + cy, self.c, (self.x, self.y, self.c)).wait_recv()
                self._passed(a, j).start()

    def finish(self):
        for a in range(self.n):
            for j, (cx, cy) in enumerate(self.chips):
                self._copy(a, 3 + j, 2 * cx + cy, 1 - self.c, (self.x, self.y, self.c)).wait_recv()
        for a in range(self.n):
            for j in range(3):
                self._first(a, j).wait_send()
                self._passed(a, j).wait_send()
            self._local(a).wait()


class _FusedReduce:
    def __init__(self, pieces, into=None):
        self.owners = [tuple(o) for _, o in pieces]
        self.arrs = [g.reshape(len(o), 2, g.shape[0] // (2 * len(o)), g.shape[1]) for g, o in pieces]
        n = self.n = len(pieces)
        hc = self.hc = [(v.shape[2], v.shape[3]) for v in self.arrs]
        nts = [len(o) for o in self.owners]
        self.base = [sum(nts[:p]) for p in range(n)]
        anyspec = pl.BlockSpec(memory_space=pl.ANY)
        self.into = dict(into or {})
        self.arrs = self.arrs + [self.into[p] for p in sorted(self.into)]
        self.in_specs = [anyspec] * len(self.arrs)
        self.out_shape = [jax.ShapeDtypeStruct((2,) + s, F32) for s in hc]
        self.out_specs = [anyspec] * n
        self.scratch = [pltpu.VMEM((nt,) + s, F32) for nt, s in zip(nts, hc)]
        self.scratch += [pltpu.VMEM((nt,) + s, F32) for nt, s in zip(nts, hc)]
        self.scratch += [pltpu.VMEM(s, F32) for s in hc]
        self.scratch += [pltpu.VMEM((nt,) + s, BF16) for nt, s in zip(nts, hc)]
        self.scratch += [pltpu.VMEM((3,) + s, BF16) for s in hc]
        self.scratch += [pltpu.VMEM(s, F32) for s in hc]
        tot = sum(nts)
        self.scratch += [pltpu.SemaphoreType.DMA((tot,)), pltpu.SemaphoreType.DMA((tot,)),
                         pltpu.SemaphoreType.DMA((tot,)), pltpu.SemaphoreType.DMA((3 * n,)),
                         pltpu.SemaphoreType.DMA((n,)), pltpu.SemaphoreType.DMA((n,)), pltpu.SemaphoreType.DMA((n,)),
                         pltpu.SemaphoreType.DMA((tot,))]
        self.phases = [self.start, self.exchange, self.combine, self.finish]

    def aliases(self, first_in, first_out):
        return {first_in + self.n + k: first_out + p for k, p in enumerate(sorted(self.into))}

    def bind(self, ins, outs, scratch):
        n = self.n
        self.g, self.out = ins[:n], outs
        self.va, self.recv_a, self.own = scratch[:n], scratch[n:2 * n], scratch[2 * n:3 * n]
        self.tsend, self.recv_b, self.fin = scratch[3 * n:4 * n], scratch[4 * n:5 * n], scratch[5 * n:6 * n]
        self.sa, self.ra, self.sb, self.rb, self.sc, self.rc, self.lc, self.la = scratch[6 * n:]
        self.x, self.y, self.c = lax.axis_index("x"), lax.axis_index("y"), lax.axis_index("c")
        self.mychip = 2 * self.x + self.y
        self.sibling = (self.x, self.y, 1 - self.c)

    def _copy_a(self, p, t):
        k = self.base[p] + t
        return pltpu.make_async_remote_copy(
            src_ref=self.g[p].at[t, 1 - self.c], dst_ref=self.recv_a[p].at[t], send_sem=self.sa.at[k],
            recv_sem=self.ra.at[k], device_id=self.sibling, device_id_type=MESH)

    def _fetch(self, p, t):
        return pltpu.make_async_copy(self.g[p].at[t, self.c], self.va[p].at[t], self.la.at[self.base[p] + t])

    def _slot(self, owner):
        rel = jnp.bitwise_xor(self.mychip, owner)
        return jnp.where(rel == 2, 0, jnp.where(rel == 1, 1, 2))

    def _copy_b(self, p, t, slot):
        owner = self.owners[p][t]
        return pltpu.make_async_remote_copy(
            src_ref=self.tsend[p].at[t], dst_ref=self.recv_b[p].at[slot], send_sem=self.sb.at[self.base[p] + t],
            recv_sem=self.rb.at[3 * p + slot], device_id=(owner // 2, owner % 2, self.c), device_id_type=MESH)

    def _copy_c(self, p, half):
        return pltpu.make_async_remote_copy(
            src_ref=self.fin[p], dst_ref=self.out[p].at[half], send_sem=self.sc.at[p], recv_sem=self.rc.at[p],
            device_id=self.sibling, device_id_type=MESH)

    def _local_c(self, p):
        return pltpu.make_async_copy(self.fin[p], self.out[p].at[self.c], self.lc.at[p])

    def start(self):
        for p in range(self.n):
            for t in range(len(self.owners[p])):
                self._copy_a(p, t).start()
                self._fetch(p, t).start()

    def exchange(self):
        for p in range(self.n):
            for t, owner in enumerate(self.owners[p]):
                self._copy_a(p, t).wait_recv()
                self._fetch(p, t).wait()
                mine = self.mychip == owner

                @pl.when(mine)
                def _():
                    self.own[p][...] = self.va[p][t] + self.recv_a[p][t]

                @pl.when(jnp.logical_not(mine))
                def _():
                    self.tsend[p][t] = (self.va[p][t] + self.recv_a[p][t]).astype(BF16)
                    self._copy_b(p, t, self._slot(owner)).start()

    def combine(self):
        for p in range(self.n):
            for t, owner in enumerate(self.owners[p]):
                @pl.when(self.mychip == owner)
                def _():
                    acc = self.own[p][...]
                    for j in range(3):
                        self._copy_b(p, t, j).wait_recv()
                        acc = acc + self.recv_b[p][j].astype(F32)
                    self.fin[p][...] = acc
                    self._local_c(p).start()
                    self._copy_c(p, self.c).start()

    def finish(self):
        for p in range(self.n):
            for t, owner in enumerate(self.owners[p]):
                self._copy_a(p, t).wait_send()
                mine = self.mychip == owner

                @pl.when(mine)
                def _():
                    self._copy_c(p, 1 - self.c).wait_recv()
                    self._copy_c(p, self.c).wait_send()
                    self._local_c(p).wait()

                @pl.when(jnp.logical_not(mine))
                def _():
                    self._copy_b(p, t, self._slot(owner)).wait_send()


def _split_fused(refs, n_in, n_out, n_scr, fused):
    if fused is None:
        return refs[:n_in], refs[n_in:n_in + n_out], refs[n_in + n_out:]
    fi, fo = len(fused.in_specs), len(fused.out_specs)
    ins, rest = refs[:n_in], refs[n_in:]
    f_ins, rest = rest[:fi], rest[fi:]
    outs, rest = rest[:n_out], rest[n_out:]
    f_outs, rest = rest[:fo], rest[fo:]
    scr, f_scr = rest[:n_scr], rest[n_scr:]
    fused.bind(f_ins, f_outs, f_scr)
    return ins, outs, scr


def _run_phases(fused, step, at_steps, before):
    if fused is None:
        return
    for phase, (at, first) in zip(fused.phases, at_steps):
        if first == before:
            pl.when(step == at)(phase)


def _front(c_pad, c_ctx_rows, w_mod, b_cols, w_in_t_loc):
    ncol = w_mod.shape[1]
    gather = _ChipGather([w_in_t_loc])

    def body(c_ref, cctx_ref, w_ref, b_ref, win_ref, sc_ref, modg_ref, wing_ref,
             call_ref, ag_send, ag_recv, ag_local, m_send, m_recv, *g_scr):
        gather.bind([win_ref], [wing_ref], g_scr)
        _gather8_in_vmem(c_ref, call_ref, ag_send, ag_recv, ag_local)
        gather.start()
        x, y, c = lax.axis_index("x"), lax.axis_index("y"), lax.axis_index("c")
        chips = [(1 - x, y), (x, 1 - y), (1 - x, 1 - y)]
        mychip = 2 * x + y
        rows = jnp.concatenate([call_ref[dv] for dv in range(8)] + [cctx_ref[...]], axis=0)
        sc = rows * _sigmoid(rows)
        sc_ref[...] = sc
        modg_ref[mychip] = jnp.dot(sc, w_ref[...], preferred_element_type=F32,
                                   precision=lax.Precision.HIGHEST) + b_ref[...]

        def mcopy(j, chip_idx, to):
            return pltpu.make_async_remote_copy(
                src_ref=modg_ref.at[chip_idx], dst_ref=modg_ref.at[chip_idx], send_sem=m_send.at[j],
                recv_sem=m_recv.at[j], device_id=to, device_id_type=MESH)

        sends = [mcopy(j, mychip, (*chip, c)) for j, chip in enumerate(chips)]
        for cp in sends:
            cp.start()
        for j, (cx, cy) in enumerate(chips):
            mcopy(j, 2 * cx + cy, (x, y, c)).wait_recv()
        gather.forward()
        gather.finish()
        for cp in sends:
            cp.wait_send()

    vm = pl.BlockSpec(memory_space=pltpu.VMEM)
    return pl.pallas_call(
        body, name="front_exchange",
        out_shape=[jax.ShapeDtypeStruct((80, D_MODEL), F32), jax.ShapeDtypeStruct((4, 80, ncol), F32)] + gather.out_shape,
        in_specs=[vm, vm, vm, vm] + gather.in_specs, out_specs=[vm, vm] + gather.out_specs,
        scratch_shapes=[pltpu.VMEM((8, 8, D_MODEL), F32), pltpu.SemaphoreType.DMA((7,)), pltpu.SemaphoreType.DMA((7,)),
                        pltpu.SemaphoreType.DMA, pltpu.SemaphoreType.DMA((3,)), pltpu.SemaphoreType.DMA((3,))]
        + gather.scratch,
        compiler_params=_params(),
    )(c_pad, c_ctx_rows, w_mod, b_cols, w_in_t_loc)


def _tail_exchange(packed, sc_rows, w_mod, bsz, ctx_row):
    d = D_MODEL
    ncol = w_mod.shape[1]

    def body(p_ref, sc_ref, w_ref, total_ref, gw_ref, gcc_ref, gat_ref, dm_ref, part_ref,
             ag_send, ag_recv, ag_local, g_send, g_recv):
        _gather8_in_vmem(p_ref, gat_ref, ag_send, ag_recv, ag_local)
        acc = gat_ref[0]
        for dv in range(1, 8):
            acc = acc + gat_ref[dv]
        total_ref[...] = acc
        x, y, c = lax.axis_index("x"), lax.axis_index("y"), lax.axis_index("c")
        chips = [(1 - x, y), (x, 1 - y), (1 - x, 1 - y)]
        mychip = 2 * x + y
        dm_ref[...] = jnp.zeros(dm_ref.shape, F32)
        for k in range(4):
            @pl.when(mychip == k)
            def _():
                spans = [(seg, max(k * ncol, seg * d) - seg * d, min((k + 1) * ncol, (seg + 1) * d) - seg * d)
                         for seg in range(3) if k * ncol < (seg + 1) * d and (k + 1) * ncol > seg * d]
                for dv in range(8):
                    for b in range(bsz):
                        dm_ref[8 * dv + b:8 * dv + b + 1, :] = jnp.concatenate(
                            [gat_ref[dv, 3 * b + seg:3 * b + seg + 1, lo:hi] for seg, lo, hi in spans], axis=1)
                dm_ref[64:65, :] = jnp.concatenate(
                    [total_ref[ctx_row + seg:ctx_row + seg + 1, lo:hi] if seg < 2 else jnp.zeros((1, hi - lo), F32)
                     for seg, lo, hi in spans], axis=1)

        dm = dm_ref[...]
        gw_ref[...] = lax.dot_general(sc_ref[...], dm, TN, preferred_element_type=F32,
                                      precision=lax.Precision.HIGHEST)
        part_ref[mychip] = lax.dot_general(dm[64:72, :], w_ref[...], NT, preferred_element_type=F32,
                                           precision=lax.Precision.HIGHEST)

        def gcopy(j, chip_idx, to):
            return pltpu.make_async_remote_copy(
                src_ref=part_ref.at[chip_idx], dst_ref=part_ref.at[chip_idx], send_sem=g_send.at[j],
                recv_sem=g_recv.at[j], device_id=to, device_id_type=MESH)

        sends = [gcopy(j, mychip, (*chip, c)) for j, chip in enumerate(chips)]
        for cp in sends:
            cp.start()
        for j, (cx, cy) in enumerate(chips):
            gcopy(j, 2 * cx + cy, (x, y, c)).wait_recv()
        for cp in sends:
            cp.wait_send()
        gcc_ref[...] = (part_ref[0] + part_ref[1]) + (part_ref[2] + part_ref[3])

    return pl.pallas_call(
        body, name="tail_exchange",
        out_shape=[jax.ShapeDtypeStruct(packed.shape, F32), jax.ShapeDtypeStruct((d, ncol), F32),
                   jax.ShapeDtypeStruct((8, d), F32)],
        scratch_shapes=[pltpu.VMEM((8,) + packed.shape, F32), pltpu.VMEM((80, ncol), F32), pltpu.VMEM((4, 8, d), F32),
                        pltpu.SemaphoreType.DMA((7,)), pltpu.SemaphoreType.DMA((7,)), pltpu.SemaphoreType.DMA,
                        pltpu.SemaphoreType.DMA((3,)), pltpu.SemaphoreType.DMA((3,))],
        compiler_params=_params(),
    )(packed, sc_rows, w_mod)


def _bcast_spec(arr):
    if arr.shape[0] == 1:
        return pl.BlockSpec((1, 1, arr.shape[2]), lambda b, i: (0, 0, 0))
    return pl.BlockSpec((1, 1, arr.shape[2]), lambda b, i: (b, 0, 0))


def _norm_inproj(x, shift, scale1p, norm_w, w_t, splits, tm, name):
    bsz, s, d = x.shape

    def body(x_ref, sh_ref, sc_ref, nw_ref, w_ref, u_ref, *out_refs):
        xv = x_ref[0]
        rstd = lax.rsqrt(jnp.mean(xv * xv, axis=-1, keepdims=True) + EPS)
        u = (xv * rstd * nw_ref[...]) * sc_ref[0] + sh_ref[0]
        ub = u.astype(BF16)
        u_ref[0] = ub
        for (lo, hi), o_ref in zip(splits, out_refs):
            o_ref[0] = lax.dot_general(ub, w_ref[lo:hi, :], NT, preferred_element_type=F32)

    tok = lambda w: pl.BlockSpec((1, tm, w), lambda b, i: (b, i, 0))
    return pl.pallas_call(
        body, name=name, grid=(bsz, s // tm),
        in_specs=[tok(d), _bcast_spec(shift), _bcast_spec(scale1p), pl.BlockSpec((1, d), lambda b, i: (0, 0)),
                  pl.BlockSpec(w_t.shape, lambda b, i: (0, 0))],
        out_specs=[tok(d)] + [tok(hi - lo) for lo, hi in splits],
        out_shape=[jax.ShapeDtypeStruct((bsz, s, d), BF16)]
        + [jax.ShapeDtypeStruct((bsz, s, hi - lo), F32) for lo, hi in splits],
        compiler_params=_params(2),
    )(x, shift, scale1p, norm_w, w_t)


def _dup_heads(kv, lo_mask):
    r = pltpu.roll(kv, HEAD_DIM, 1)
    return jnp.where(lo_mask, kv, r), jnp.where(lo_mask, r, kv)


def _qkv_prep(qkv, cos, sin, qnw, knw, bd512, bd128, ts, extra):
    bsz, s, _ = qkv.shape

    def body(p_ref, cos_ref, sin_ref, qnw_ref, knw_ref, bd512_ref, bd128_ref, q_ref, k_ref, v_ref, kt_ref, vt_ref):
        lo_mask = _lo_mask(ts)
        cos_t, sin_t = cos_ref[...], sin_ref[...]
        qp = p_ref[0, :, 0:512]
        qn = qp * lax.rsqrt(_seg_mean(qp * qp, bd512_ref[...]) + EPS) * qnw_ref[...]
        qr = qn * _tile_lanes(cos_t, 4) + _partner(qn) * _tile_lanes(sin_t, 4)
        q_ref[0] = (qr * (1.0 / math.sqrt(HEAD_DIM))).astype(BF16)
        kp = p_ref[0, :, 512:640]
        kn = kp * lax.rsqrt(_seg_mean(kp * kp, bd128_ref[...]) + EPS) * knw_ref[...]
        kr = kn * cos_t + _partner(kn) * sin_t
        k0, k1 = _dup_heads(kr, lo_mask)
        k_ref[0, 0] = k0.astype(BF16)
        k_ref[0, 1] = k1.astype(BF16)
        vp = p_ref[0, :, 640:768]
        v0, v1 = _dup_heads(vp, lo_mask)
        v_ref[0, 0] = v0.astype(BF16)
        v_ref[0, 1] = v1.astype(BF16)
        kt_ref[0] = kr.T.astype(BF16)
        vt_ref[0] = vp.T.astype(BF16)

    const = lambda a: pl.BlockSpec(a.shape, lambda b, i: (0,) * a.ndim)
    kv_spec = pl.BlockSpec((1, 2, ts, 128), lambda b, i: (b, 0, i, 0))
    t_spec = pl.BlockSpec((1, 128, ts), lambda b, i: (b, 0, i))
    return pl.pallas_call(
        body, name="qkv_prep", grid=(bsz, s // ts),
        in_specs=[pl.BlockSpec((1, ts, 768), lambda b, i: (b, i, 0)),
                  pl.BlockSpec((ts, 128), lambda b, i: (i, 0)), pl.BlockSpec((ts, 128), lambda b, i: (i, 0)),
                  const(qnw), const(knw), const(bd512), const(bd128)],
        out_specs=[pl.BlockSpec((1, ts, 512), lambda b, i: (b, i, 0)), kv_spec, kv_spec, t_spec, t_spec],
        out_shape=[jax.ShapeDtypeStruct((bsz, s, 512), BF16), jax.ShapeDtypeStruct((bsz, 2, s + extra, 128), BF16),
                   jax.ShapeDtypeStruct((bsz, 2, s + extra, 128), BF16),
                   jax.ShapeDtypeStruct((bsz, 128, s + extra), BF16), jax.ShapeDtypeStruct((bsz, 128, s + extra), BF16)],
        compiler_params=_params(2),
    )(qkv, cos, sin, qnw, knw, bd512, bd128)


def _ctx_kv_prep(pc, knw, bd128, k2, v2, kt, vt):
    bsz, cl, _ = pc.shape
    blk = k2.shape[2] // cl - 1

    def body(p_ref, knw_ref, bd128_ref, k_in, v_in, kt_in, vt_in, k_ref, v_ref, kt_ref, vt_ref):
        lo_mask = _lo_mask(cl)
        kp = p_ref[0, :, 0:128]
        kn = kp * lax.rsqrt(_seg_mean(kp * kp, bd128_ref[...]) + EPS) * knw_ref[...]
        k0, k1 = _dup_heads(kn, lo_mask)
        k_ref[0, 0] = k0.astype(BF16)
        k_ref[0, 1] = k1.astype(BF16)
        vp = p_ref[0, :, 128:256]
        v0, v1 = _dup_heads(vp, lo_mask)
        v_ref[0, 0] = v0.astype(BF16)
        v_ref[0, 1] = v1.astype(BF16)
        kt_ref[0] = kn.T.astype(BF16)
        vt_ref[0] = vp.T.astype(BF16)

    const = lambda a: pl.BlockSpec(a.shape, lambda b: (0,) * a.ndim)
    kv_spec = pl.BlockSpec((1, 2, cl, 128), lambda b: (b, 0, blk, 0))
    t_spec = pl.BlockSpec((1, 128, cl), lambda b: (b, 0, blk))
    anyspec = pl.BlockSpec(memory_space=pl.ANY)
    return pl.pallas_call(
        body, name="ctx_kv_prep", grid=(bsz,),
        in_specs=[pl.BlockSpec((1, cl, 256), lambda b: (b, 0, 0)), const(knw), const(bd128)] + [anyspec] * 4,
        out_specs=[kv_spec, kv_spec, t_spec, t_spec],
        out_shape=[jax.ShapeDtypeStruct(a.shape, BF16) for a in (k2, v2, kt, vt)],
        input_output_aliases={3: 0, 4: 1, 5: 2, 6: 3},
        compiler_params=_params(1),
    )(pc, knw, bd128, k2, v2, kt, vt)


def _attn_forward(q, k2, vt, tq, fused):
    bsz, s, _ = q.shape
    sk = k2.shape[2]
    nq = s // tq
    total = bsz * N_KV * nq
    at_steps = [(0, True), (total // 4, True), (total - 1, False)]

    def body(*refs):
        (q_ref, k_ref, vt_ref), (o_ref, lse_ref), _ = _split_fused(refs, 3, 2, 0, fused)
        g = pl.program_id(1)
        step = (pl.program_id(0) * N_KV + g) * nq + pl.program_id(2)
        _run_phases(fused, step, at_steps, True)
        kk = k_ref[0, 0]
        lo_mask = _lo_mask(tq)
        vt_aug = jnp.concatenate([vt_ref[0, pl.ds(pl.multiple_of(g * HEAD_DIM, HEAD_DIM), HEAD_DIM), :],
                                  jnp.ones((16, sk), BF16)], axis=0)
        ps, ms = [], []
        for j in range(2):
            qp = q_ref[0, :, 128 * j:128 * (j + 1)]
            for half in range(2):
                sel = lo_mask if half == 0 else jnp.logical_not(lo_mask)
                qs = jnp.where(sel, qp, jnp.zeros_like(qp))
                sc = lax.dot_general(qs, kk, NT, preferred_element_type=F32)
                m = jnp.max(sc, axis=-1, keepdims=True)
                ps.append(jnp.exp(sc - m).astype(BF16))
                ms.append(m)
        ots = [lax.dot_general(vt_aug, p, NT, preferred_element_type=F32) for p in ps]
        for j in range(2):
            o_t, l_t = [], []
            for half in range(2):
                ot = ots[2 * j + half]
                l = ot[HEAD_DIM:HEAD_DIM + 1, :]
                o_t.append(ot[0:HEAD_DIM, :] / l)
                l_t.append(jnp.broadcast_to(l, (HEAD_DIM, tq)))
            o_ref[0, :, 128 * j:128 * (j + 1)] = jnp.concatenate(o_t, axis=0).T
            lse_ref[0, :, 128 * j:128 * (j + 1)] = (jnp.where(lo_mask, ms[2 * j], ms[2 * j + 1])
                                                    + jnp.log(jnp.concatenate(l_t, axis=0).T))
        _run_phases(fused, step, at_steps, False)

    q_spec = pl.BlockSpec((1, tq, 256), lambda b, g, i: (b, i, g))
    kv_spec = pl.BlockSpec((1, 1, sk, 128), lambda b, g, i: (b, g, 0, 0))
    return pl.pallas_call(
        body, name="attn_forward", grid=(bsz, N_KV, nq),
        in_specs=[q_spec, kv_spec, pl.BlockSpec((1, 128, sk), lambda b, g, i: (b, 0, 0))] + fused.in_specs,
        out_specs=[q_spec, q_spec] + fused.out_specs,
        out_shape=[jax.ShapeDtypeStruct((bsz, s, 512), F32)] * 2 + fused.out_shape,
        scratch_shapes=fused.scratch,
        compiler_params=_params(3),
    )(q, k2, vt, *fused.arrs)


def _halo_specs(width, ts, s):
    r = ts // HALO
    last = s // HALO - 1
    return [pl.BlockSpec((1, ts, width), lambda b, i: (b, i, 0)),
            pl.BlockSpec((1, HALO, width), lambda b, i: (b, jnp.maximum(i * r - 1, 0), 0)),
            pl.BlockSpec((1, HALO, width), lambda b, i: (b, jnp.minimum((i + 1) * r, last), 0))]


def _fill_ext(ext_ref, cur, prev, nxt, i, n_tiles, ts):
    ext_ref[0:HALO, :] = jnp.where(i > 0, prev, jnp.zeros_like(prev))
    ext_ref[HALO:HALO + ts, :] = cur
    ext_ref[HALO + ts:2 * HALO + ts, :] = jnp.where(i < n_tiles - 1, nxt, jnp.zeros_like(nxt))


def _fill_shifted(sh_ref, ext_ref, ts):
    n = ts + 2 * HALO - 8
    for r in range(1, 8):
        sh_ref[r - 1, 0:n, :] = ext_ref[pl.ds(r, n), :]


def _window(sh_ref, ext_ref, off, rows, r0=0):
    q, r = divmod(off, 8)
    if r == 0:
        return ext_ref[pl.ds(r0 + off, rows), :]
    return sh_ref[r - 1, pl.ds(r0 + 8 * q, rows), :]


def _conv_forward(ga, gg, conv_w, conv_b, ln_w, ln_b, w_pw, b_pw, ts):
    bsz, s, dc = ga.shape
    n_tiles = s // ts

    def body(a_ref, ap_ref, an_ref, g_ref, gp_ref, gn_ref, cw_ref, cb_ref, lw_ref, lb_ref, wp_ref, bp_ref,
             y_ref, cv_ref, ext_ref, sh_ref):
        i = pl.program_id(1)
        glu = lambda a, g: a * _sigmoid(g)
        _fill_ext(ext_ref, glu(a_ref[0], g_ref[0]), glu(ap_ref[0], gp_ref[0]), glu(an_ref[0], gn_ref[0]), i, n_tiles, ts)
        _fill_shifted(sh_ref, ext_ref, ts)
        acc = jnp.broadcast_to(cb_ref[...], (ts, dc))
        for j in range(CONV_WIDTH):
            acc = acc + cw_ref[j:j + 1, :] * _window(sh_ref, ext_ref, HALO - CONV_PAD + j, ts)
        y_ref[0] = acc
        mu = jnp.mean(acc, axis=-1, keepdims=True)
        yc = acc - mu
        var = jnp.mean(yc * yc, axis=-1, keepdims=True)
        yn = yc * lax.rsqrt(var + EPS) * lw_ref[...] + lb_ref[...]
        ys = yn * _sigmoid(yn)
        cv_ref[0] = jnp.dot(ys.astype(BF16), wp_ref[...], preferred_element_type=F32) + bp_ref[...]

    const = lambda a: pl.BlockSpec(a.shape, lambda b, i: (0,) * a.ndim)
    return pl.pallas_call(
        body, name="conv_forward", grid=(bsz, n_tiles),
        in_specs=_halo_specs(dc, ts, s) + _halo_specs(dc, ts, s)
        + [const(conv_w), const(conv_b), const(ln_w), const(ln_b), const(w_pw), const(b_pw)],
        out_specs=[pl.BlockSpec((1, ts, dc), lambda b, i: (b, i, 0))] * 2,
        out_shape=[jax.ShapeDtypeStruct((bsz, s, dc), F32)] * 2,
        scratch_shapes=[pltpu.VMEM((ts + 2 * HALO, dc), F32), pltpu.VMEM((7, ts + 2 * HALO, dc), F32)],
        compiler_params=_params(2),
    )(ga, ga, ga, gg, gg, gg, conv_w, conv_b, ln_w, ln_b, w_pw, b_pw)


def _outproj_loss(x, target, gate, o, za, cv, zc, w_out, tm):
    bsz, s, d = x.shape

    def body(x_ref, t_ref, gate_ref, o_ref, za_ref, cv_ref, zc_ref, w_ref,
             loss_ref, dh_ref, do_ref, dza_ref, dcv_ref, dzc_ref, dgate_ref, gw_ref):
        b, i = pl.program_id(0), pl.program_id(1)
        ov, cvv = o_ref[0], cv_ref[0]
        silu_a, dsilu_a = _silu_and_grad(za_ref[0])
        silu_c, dsilu_c = _silu_and_grad(zc_ref[0])
        mix = jnp.concatenate([ov * silu_a, cvv * silu_c], axis=1).astype(BF16)
        out = jnp.dot(mix, w_ref[...], preferred_element_type=F32)
        gate_v = gate_ref[0]
        err = x_ref[0] + gate_v * out - t_ref[0]
        dh = err * (1.0 / d)
        dh_ref[0] = dh
        dout = (dh * gate_v).astype(BF16)
        dmix = lax.dot_general(dout, w_ref[...], NT, preferred_element_type=F32)
        gw = lax.dot_general(mix, dout, TN, preferred_element_type=F32)
        dg = jnp.sum(dh * out, axis=0, keepdims=True)
        sq = jnp.sum(err * err)

        @pl.when(jnp.logical_and(b == 0, i == 0))
        def _():
            gw_ref[...] = gw

        @pl.when(jnp.logical_or(b > 0, i > 0))
        def _():
            gw_ref[...] += gw

        @pl.when(i == 0)
        def _():
            dgate_ref[0] = dg
            loss_ref[...] = jnp.zeros(loss_ref.shape, F32) + sq

        @pl.when(i > 0)
        def _():
            dgate_ref[0] += dg
            loss_ref[...] += sq

        dma, dmc = dmix[:, :D_ATTN], dmix[:, D_ATTN:]
        do_ref[0] = dma * silu_a
        dza_ref[0] = (dma * ov * dsilu_a).astype(BF16)
        dcv_ref[0] = dmc * silu_c
        dzc_ref[0] = (dmc * cvv * dsilu_c).astype(BF16)

    tok = lambda w: pl.BlockSpec((1, tm, w), lambda b, i: (b, i, 0))
    return pl.pallas_call(
        body, name="outproj_loss", grid=(bsz, s // tm),
        in_specs=[tok(d), tok(d), _bcast_spec(gate), tok(512), tok(512), tok(512), tok(512),
                  pl.BlockSpec(w_out.shape, lambda b, i: (0, 0))],
        out_specs=[pl.BlockSpec((1, 8, 128), lambda b, i: (b, 0, 0)), tok(d), tok(512), tok(512), tok(512), tok(512),
                   pl.BlockSpec((1, 1, d), lambda b, i: (b, 0, 0)), pl.BlockSpec((d, d), lambda b, i: (0, 0))],
        out_shape=[jax.ShapeDtypeStruct((bsz, 8, 128), F32), jax.ShapeDtypeStruct((bsz, s, d), F32),
                   jax.ShapeDtypeStruct((bsz, s, 512), F32), jax.ShapeDtypeStruct((bsz, s, 512), BF16),
                   jax.ShapeDtypeStruct((bsz, s, 512), F32), jax.ShapeDtypeStruct((bsz, s, 512), BF16),
                   jax.ShapeDtypeStruct((bsz, 1, d), F32), jax.ShapeDtypeStruct((d, d), F32)],
        compiler_params=_params(2),
    )(x, target, gate, o, za, cv, zc, w_out)


def _conv_token_backward(dcv, y, ln_w, ln_b, w_pw, tm):
    bsz, s, dc = dcv.shape

    def body(dcv_ref, y_ref, lw_ref, lb_ref, wp_ref, dy_ref, gwp_ref, st_ref):
        b, i = pl.program_id(0), pl.program_id(1)
        yv, dcvv = y_ref[0], dcv_ref[0]
        mu = jnp.mean(yv, axis=-1, keepdims=True)
        yc = yv - mu
        rstd = lax.rsqrt(jnp.mean(yc * yc, axis=-1, keepdims=True) + EPS)
        yhat = yc * rstd
        yn = yhat * lw_ref[...] + lb_ref[...]
        ys, dsilu = _silu_and_grad(yn)
        dcvb = dcvv.astype(BF16)
        gwp = lax.dot_general(ys.astype(BF16), dcvb, TN, preferred_element_type=F32)
        dys = lax.dot_general(dcvb, wp_ref[...], NT, preferred_element_type=F32)
        dyn = dys * dsilu
        dyhat = dyn * lw_ref[...]
        dy = rstd * (dyhat - jnp.mean(dyhat, axis=-1, keepdims=True)
                     - yhat * jnp.mean(dyhat * yhat, axis=-1, keepdims=True))
        dy_ref[0] = dy
        red = lambda v: jnp.sum(v, axis=0, keepdims=True)
        stats = jnp.concatenate([red(dcvv), red(dyn * yhat), red(dyn), red(dy), jnp.zeros((4, dc), F32)], axis=0)
        first = jnp.logical_and(b == 0, i == 0)

        @pl.when(first)
        def _():
            gwp_ref[...] = gwp
            st_ref[...] = stats

        @pl.when(jnp.logical_not(first))
        def _():
            gwp_ref[...] += gwp
            st_ref[...] += stats

    tok = pl.BlockSpec((1, tm, dc), lambda b, i: (b, i, 0))
    const = lambda a: pl.BlockSpec(a.shape, lambda b, i: (0,) * a.ndim)
    return pl.pallas_call(
        body, name="conv_token_backward", grid=(bsz, s // tm),
        in_specs=[tok, tok, const(ln_w), const(ln_b), const(w_pw)],
        out_specs=[tok, pl.BlockSpec((dc, dc), lambda b, i: (0, 0)), pl.BlockSpec((8, dc), lambda b, i: (0, 0))],
        out_shape=[jax.ShapeDtypeStruct((bsz, s, dc), F32), jax.ShapeDtypeStruct((dc, dc), F32),
                   jax.ShapeDtypeStruct((8, dc), F32)],
        compiler_params=_params(2),
    )(dcv, y, ln_w, ln_b, w_pw)


def _conv_backward(dy, ga, gg, conv_w, ts):
    bsz, s, dc = dy.shape
    n_tiles = s // ts

    def body(dy_ref, dyp_ref, dyn_ref, a_ref, g_ref, cw_ref,
             da_ref, dg_ref, gcw_ref, dyext_ref, dysh_ref, ug_ref, dug_ref, gacc_ref):
        b, i = pl.program_id(0), pl.program_id(1)
        av, sg = a_ref[0], _sigmoid(g_ref[0])
        ug_ref[...] = av * sg
        _fill_ext(dyext_ref, dy_ref[0], dyp_ref[0], dyn_ref[0], i, n_tiles, ts)
        _fill_shifted(dysh_ref, dyext_ref, ts)
        gacc_ref[...] = jnp.zeros(gacc_ref.shape, F32)

        def row_block(r, carry):
            r0 = pl.multiple_of(r * CONV_ROWS, CONV_ROWS)
            ugb = ug_ref[pl.ds(r0, CONV_ROWS), :]
            acc = jnp.zeros((CONV_ROWS, dc), F32)
            for j in range(CONV_WIDTH):
                win = _window(dysh_ref, dyext_ref, HALO + CONV_PAD - j, CONV_ROWS, r0)
                acc = acc + cw_ref[j:j + 1, :] * win
                prod = ugb * win
                part = prod[0:8, :]
                for k in range(8, CONV_ROWS, 8):
                    part = part + prod[k:k + 8, :]
                gacc_ref[j] += part
            dug_ref[pl.ds(r0, CONV_ROWS), :] = acc
            return carry

        lax.fori_loop(0, ts // CONV_ROWS, row_block, 0)
        dug = dug_ref[...]
        gcw = jnp.sum(gacc_ref[...], axis=1)
        first = jnp.logical_and(b == 0, i == 0)

        @pl.when(first)
        def _():
            gcw_ref[...] = gcw

        @pl.when(jnp.logical_not(first))
        def _():
            gcw_ref[...] += gcw

        da_ref[0] = (dug * sg).astype(BF16)
        dg_ref[0] = (dug * av * sg * (1.0 - sg)).astype(BF16)

    tok = pl.BlockSpec((1, ts, dc), lambda b, i: (b, i, 0))
    return pl.pallas_call(
        body, name="conv_backward", grid=(bsz, n_tiles),
        in_specs=_halo_specs(dc, ts, s) + [tok, tok, pl.BlockSpec(conv_w.shape, lambda b, i: (0, 0))],
        out_specs=[tok, tok, pl.BlockSpec((32, dc), lambda b, i: (0, 0))],
        out_shape=[jax.ShapeDtypeStruct((bsz, s, dc), BF16), jax.ShapeDtypeStruct((bsz, s, dc), BF16),
                   jax.ShapeDtypeStruct((32, dc), F32)],
        scratch_shapes=[pltpu.VMEM((ts + 2 * HALO, dc), F32), pltpu.VMEM((7, ts + 2 * HALO, dc), F32),
                        pltpu.VMEM((ts, dc), F32), pltpu.VMEM((ts, dc), F32), pltpu.VMEM((32, 8, dc), F32)],
        compiler_params=_params(2),
    )(dy, dy, dy, ga, gg, conv_w)


def _attn_backward(q, k2, v2, kt, o, do, lse, tq, fused):
    bsz, s, _ = q.shape
    sk = k2.shape[2]
    scale = 1.0 / math.sqrt(HEAD_DIM)
    nq = s // tq
    total = bsz * N_KV * nq
    at_steps = [(0, True), (total // 5, True), (total // 2, True), (total - 1, False)]

    def body(*refs):
        (q_ref, k_ref, v_ref, kt_ref, o_ref, do_ref, lse_ref), (dq_ref, dk_ref, dv_ref), _ = _split_fused(
            refs, 7, 3, 0, fused)
        g, i = pl.program_id(1), pl.program_id(2)
        step = (pl.program_id(0) * N_KV + g) * nq + i
        _run_phases(fused, step, at_steps, True)
        kk, vv = k_ref[0, 0], v_ref[0, 0]
        kgt = kt_ref[0, pl.ds(pl.multiple_of(g * HEAD_DIM, HEAD_DIM), HEAD_DIM), :]
        lo_mask = _lo_mask(tq)
        dk_acc = jnp.zeros((HEAD_DIM, sk), F32)
        dv_acc = jnp.zeros((HEAD_DIM, sk), F32)
        for j in range(2):
            cols = slice(128 * j, 128 * (j + 1))
            qp, dop, lsep = q_ref[0, :, cols], do_ref[0, :, cols], lse_ref[0, :, cols]
            dprod = dop * o_ref[0, :, cols]
            q_t = qp.astype(F32).T.astype(BF16)
            do_t = dop.T.astype(BF16)
            dq_t = []
            for half in range(2):
                sel = lo_mask if half == 0 else jnp.logical_not(lo_mask)
                rows = slice(HEAD_DIM * half, HEAD_DIM * (half + 1))
                qs = jnp.where(sel, qp, jnp.zeros_like(qp))
                dos = jnp.where(sel, dop, 0.0).astype(BF16)
                lse_h = jnp.max(jnp.where(sel, lsep, -jnp.inf), axis=-1, keepdims=True)
                delta = jnp.sum(jnp.where(sel, dprod, 0.0), axis=-1, keepdims=True)
                sc = lax.dot_general(qs, kk, NT, preferred_element_type=F32)
                p = jnp.exp(sc - lse_h)
                dp = lax.dot_general(dos, vv, NT, preferred_element_type=F32)
                ds = (p * (dp - delta)).astype(BF16)
                dv_acc = dv_acc + jnp.dot(do_t[rows, :], p.astype(BF16), preferred_element_type=F32)
                dk_acc = dk_acc + jnp.dot(q_t[rows, :], ds, preferred_element_type=F32)
                dq_t.append(lax.dot_general(kgt, ds, NT, preferred_element_type=F32))
            dq_ref[0, :, cols] = (jnp.concatenate(dq_t, axis=0) * scale).T

        @pl.when(i == 0)
        def _():
            dk_ref[0, 0] = dk_acc
            dv_ref[0, 0] = dv_acc

        @pl.when(i > 0)
        def _():
            dk_ref[0, 0] += dk_acc
            dv_ref[0, 0] += dv_acc

        _run_phases(fused, step, at_steps, False)

    q_spec = pl.BlockSpec((1, tq, 256), lambda b, g, i: (b, i, g))
    kv_spec = pl.BlockSpec((1, 1, sk, 128), lambda b, g, i: (b, g, 0, 0))
    acc_spec = pl.BlockSpec((1, 1, HEAD_DIM, sk), lambda b, g, i: (b, g, 0, 0))
    return pl.pallas_call(
        body, name="attn_backward", grid=(bsz, N_KV, nq),
        in_specs=[q_spec, kv_spec, kv_spec, pl.BlockSpec((1, 128, sk), lambda b, g, i: (b, 0, 0)), q_spec, q_spec,
                  q_spec] + fused.in_specs,
        out_specs=[q_spec, acc_spec, acc_spec] + fused.out_specs,
        out_shape=[jax.ShapeDtypeStruct((bsz, s, 512), F32), jax.ShapeDtypeStruct((bsz, 2, HEAD_DIM, sk), F32),
                   jax.ShapeDtypeStruct((bsz, 2, HEAD_DIM, sk), F32)] + fused.out_shape,
        scratch_shapes=fused.scratch,
        compiler_params=_params(3),
    )(q, k2, v2, kt, o, do, lse, *fused.arrs)


def _heads_to_lanes(acc_ref):
    return jnp.concatenate([acc_ref[0, 0], acc_ref[0, 1]], axis=0).T


def _norm_backward(dn, pre, w, bd):
    rstd = lax.rsqrt(_seg_mean(pre * pre, bd) + EPS)
    xhat = pre * rstd
    dxhat = dn * w
    return rstd * (dxhat - xhat * _seg_mean(dxhat * xhat, bd)), dn * xhat


def _qkv_backward(qkv, dq, dk2, dv2, cos, sin, qnw, knw, bd512, bd128, ts):
    bsz, s, _ = qkv.shape

    def body(p_ref, dq_ref, dk_ref, dv_ref, cos_ref, sin_ref, qnw_ref, knw_ref, bd512_ref, bd128_ref, d_ref, gw_ref):
        b, i = pl.program_id(0), pl.program_id(1)
        lo_mask = _lo_mask(ts)
        cos_t, sin_t = cos_ref[...], sin_ref[...]
        dqr = dq_ref[0]
        dqn = dqr * _tile_lanes(cos_t, 4) + _partner(dqr * _tile_lanes(sin_t, 4))
        dqp, gq = _norm_backward(dqn, p_ref[0, :, 0:512], qnw_ref[...], bd512_ref[...])
        dkr = _heads_to_lanes(dk_ref)
        dkn = dkr * cos_t + _partner(dkr * sin_t)
        dkp, gk = _norm_backward(dkn, p_ref[0, :, 512:640], knw_ref[...], bd128_ref[...])
        dvp = _heads_to_lanes(dv_ref)
        d_ref[0] = jnp.concatenate([dqp, dkp, dvp], axis=1).astype(BF16)
        gk512 = jnp.concatenate([jnp.sum(gk, axis=0, keepdims=True), jnp.zeros((1, 384), F32)], axis=1)
        rows = jnp.concatenate([jnp.sum(gq, axis=0, keepdims=True), gk512, jnp.zeros((6, 512), F32)], axis=0)
        first = jnp.logical_and(b == 0, i == 0)

        @pl.when(first)
        def _():
            gw_ref[...] = rows

        @pl.when(jnp.logical_not(first))
        def _():
            gw_ref[...] += rows

    const = lambda a: pl.BlockSpec(a.shape, lambda b, i: (0,) * a.ndim)
    kv_spec = pl.BlockSpec((1, 2, HEAD_DIM, ts), lambda b, i: (b, 0, 0, i))
    return pl.pallas_call(
        body, name="qkv_backward", grid=(bsz, s // ts),
        in_specs=[pl.BlockSpec((1, ts, 768), lambda b, i: (b, i, 0)), pl.BlockSpec((1, ts, 512), lambda b, i: (b, i, 0)),
                  kv_spec, kv_spec, pl.BlockSpec((ts, 128), lambda b, i: (i, 0)),
                  pl.BlockSpec((ts, 128), lambda b, i: (i, 0)), const(qnw), const(knw), const(bd512), const(bd128)],
        out_specs=[pl.BlockSpec((1, ts, 768), lambda b, i: (b, i, 0)), pl.BlockSpec((8, 512), lambda b, i: (0, 0))],
        out_shape=[jax.ShapeDtypeStruct((bsz, s, 768), BF16), jax.ShapeDtypeStruct((8, 512), F32)],
        compiler_params=_params(2),
    )(qkv, dq, dk2, dv2, cos, sin, qnw, knw, bd512, bd128)


def _ctx_kv_backward(pc, dk2, dv2, knw, bd128):
    bsz, cl, _ = pc.shape

    def body(p_ref, dk_ref, dv_ref, knw_ref, bd128_ref, d_ref, gw_ref):
        b = pl.program_id(0)
        lo_mask = _lo_mask(cl)
        dkn = _heads_to_lanes(dk_ref)
        dkp, gk = _norm_backward(dkn, p_ref[0, :, 0:128], knw_ref[...], bd128_ref[...])
        dvp = _heads_to_lanes(dv_ref)
        d_ref[0] = jnp.concatenate([dkp, dvp], axis=1).astype(BF16)
        rows = jnp.concatenate([jnp.sum(gk, axis=0, keepdims=True), jnp.zeros((7, 128), F32)], axis=0)

        @pl.when(b == 0)
        def _():
            gw_ref[...] = rows

        @pl.when(b > 0)
        def _():
            gw_ref[...] += rows

    const = lambda a: pl.BlockSpec(a.shape, lambda b: (0,) * a.ndim)
    blk = dk2.shape[3] // cl - 1
    kv_spec = pl.BlockSpec((1, 2, HEAD_DIM, cl), lambda b: (b, 0, 0, blk))
    return pl.pallas_call(
        body, name="ctx_kv_backward", grid=(bsz,),
        in_specs=[pl.BlockSpec((1, cl, 256), lambda b: (b, 0, 0)), kv_spec, kv_spec, const(knw), const(bd128)],
        out_specs=[pl.BlockSpec((1, cl, 256), lambda b: (b, 0, 0)), pl.BlockSpec((8, 128), lambda b: (0, 0))],
        out_shape=[jax.ShapeDtypeStruct((bsz, cl, 256), BF16), jax.ShapeDtypeStruct((8, 128), F32)],
        compiler_params=_params(1),
    )(pc, dk2, dv2, knw, bd128)


def _weight_grad(parts, u, init, tm, name):
    bsz, s, d = u.shape
    n_p = len(parts)
    nrows = sum(hi - lo for _, lo, hi in parts)

    def body(*refs):
        p_refs, u_ref = refs[:n_p], refs[n_p]
        gi_ref = refs[n_p + 1] if init is not None else None
        gw_ref = refs[-1]
        first = jnp.logical_and(pl.program_id(0) == 0, pl.program_id(1) == 0)
        dp = jnp.concatenate([r[0, :, lo:hi] for r, (_, lo, hi) in zip(p_refs, parts)], axis=1)
        gw = lax.dot_general(dp, u_ref[0], TN, preferred_element_type=F32)

        @pl.when(first)
        def _():
            gw_ref[...] = gw
            if init is not None:
                gw_ref[KV_LO:KV_HI, :] += gi_ref[...]

        @pl.when(jnp.logical_not(first))
        def _():
            gw_ref[...] += gw

    tok = lambda w: pl.BlockSpec((1, tm, w), lambda b, i: (b, i, 0))
    in_specs = [tok(a.shape[2]) for a, _, _ in parts] + [tok(d)]
    args = [a for a, _, _ in parts] + [u]
    if init is not None:
        in_specs.append(pl.BlockSpec(init.shape, lambda b, i: (0, 0)))
        args.append(init)
    return pl.pallas_call(
        body, name=name, grid=(bsz, s // tm), in_specs=in_specs,
        out_specs=pl.BlockSpec((nrows, d), lambda b, i: (0, 0)), out_shape=jax.ShapeDtypeStruct((nrows, d), F32),
        compiler_params=_params(2),
    )(*args)


def _inproj_backward(dps, x, dh, scale1p, norm_w, w_t, tm, name, fused=None):
    bsz, s, d = x.shape
    n_p = len(dps)
    shared = scale1p.shape[0] == 1
    with_dx = dh is not None
    n_in = n_p + (2 if with_dx else 1) + 3
    n_out = 3 if with_dx else 2
    total = bsz * (s // tm)
    at_steps = [(0, True), (total // 8, True), ((3 * total) // 4, True), (total - 1, False)]

    def body(*refs):
        ins, outs, _ = _split_fused(refs, n_in, n_out, 0, fused)
        dp_refs, x_ref = ins[:n_p], ins[n_p]
        dh_ref = ins[n_p + 1] if with_dx else None
        sc_ref, nw_ref, w_ref = ins[-3:]
        mod_ref, gnw_ref = outs[-2:]
        b, i = pl.program_id(0), pl.program_id(1)
        step = b * (s // tm) + i
        _run_phases(fused, step, at_steps, True)
        first = jnp.logical_and(b == 0, i == 0)
        dp = dp_refs[0][0] if n_p == 1 else jnp.concatenate([r[0] for r in dp_refs], axis=1)
        du = jnp.dot(dp, w_ref[...], preferred_element_type=F32)
        xv = x_ref[0]
        rstd = lax.rsqrt(jnp.mean(xv * xv, axis=-1, keepdims=True) + EPS)
        xhat = xv * rstd
        nw, sc = nw_ref[...], sc_ref[0]
        red = lambda v: jnp.sum(v, axis=0, keepdims=True)
        mod_rows = jnp.concatenate([red(du), red(du * (xhat * nw)), jnp.zeros((6, d), F32)], axis=0)
        gnw_rows = jnp.concatenate([red(du * sc * xhat), jnp.zeros((7, d), F32)], axis=0)
        mod_first = first if shared else i == 0

        @pl.when(mod_first)
        def _():
            mod_ref[0] = mod_rows

        @pl.when(jnp.logical_not(mod_first))
        def _():
            mod_ref[0] += mod_rows

        @pl.when(first)
        def _():
            gnw_ref[...] = gnw_rows

        @pl.when(jnp.logical_not(first))
        def _():
            gnw_ref[...] += gnw_rows

        if with_dx:
            dxhat = du * (nw * sc)
            outs[0][0] = dh_ref[0] + rstd * (dxhat - xhat * jnp.mean(dxhat * xhat, axis=-1, keepdims=True))
        _run_phases(fused, step, at_steps, False)

    tok = lambda w: pl.BlockSpec((1, tm, w), lambda b, i: (b, i, 0))
    in_specs = [tok(p.shape[2]) for p in dps] + [tok(d)]
    args = list(dps) + [x]
    if with_dx:
        in_specs.append(tok(d))
        args.append(dh)
    in_specs += [_bcast_spec(scale1p), pl.BlockSpec((1, d), lambda b, i: (0, 0)),
                 pl.BlockSpec(w_t.shape, lambda b, i: (0, 0))]
    args += [scale1p, norm_w, w_t]
    bm = scale1p.shape[0]
    mod_spec = pl.BlockSpec((1, 8, d), (lambda b, i: (0, 0, 0)) if shared else (lambda b, i: (b, 0, 0)))
    out_specs = [mod_spec, pl.BlockSpec((8, d), lambda b, i: (0, 0))]
    out_shape = [jax.ShapeDtypeStruct((bm, 8, d), F32), jax.ShapeDtypeStruct((8, d), F32)]
    if with_dx:
        out_specs.insert(0, tok(d))
        out_shape.insert(0, jax.ShapeDtypeStruct((bsz, s, d), F32))
    scratch, aliases = [], {}
    if fused is not None:
        aliases = fused.aliases(len(args), len(out_specs))
        in_specs += fused.in_specs
        args += fused.arrs
        out_specs += fused.out_specs
        out_shape += fused.out_shape
        scratch = fused.scratch
    res = pl.pallas_call(
        body, name=name, grid=(bsz, s // tm), in_specs=in_specs, out_specs=out_specs, out_shape=out_shape,
        scratch_shapes=scratch, input_output_aliases=aliases, compiler_params=_params(2),
    )(*args)
    return list(res) if with_dx else [None] + list(res)


def _adamw_update(w_ref, g_ref, m_ref, v_ref, d_ref, nm_ref, nv_ref):
    gv = g_ref[...]
    mn = ADAM_B1 * m_ref[...] + (1.0 - ADAM_B1) * gv
    vn = ADAM_B2 * v_ref[...] + (1.0 - ADAM_B2) * (gv * gv)
    m_hat = mn / (1.0 - ADAM_B1 ** ADAM_STEP)
    v_hat = vn / (1.0 - ADAM_B2 ** ADAM_STEP)
    d_ref[...] = -ADAM_LR * (m_hat / (jnp.sqrt(v_hat) + ADAM_EPS) + ADAM_WD * w_ref[...])
    nm_ref[...] = mn
    nv_ref[...] = vn


def _adamw_small(ws, gs, ms, vs):
    n = len(ws)

    def body(*refs):
        ins, outs = refs[:4 * n], refs[4 * n:]
        for k in range(n):
            _adamw_update(ins[k], ins[n + k], ins[2 * n + k], ins[3 * n + k], outs[3 * k], outs[3 * k + 1],
                          outs[3 * k + 2])

    res = pl.pallas_call(
        body, name="adamw_small",
        out_shape=[jax.ShapeDtypeStruct(w.shape, F32) for w in ws for _ in range(3)], compiler_params=_params(),
    )(*ws, *gs, *ms, *vs)
    return [tuple(res[3 * k:3 * k + 3]) for k in range(n)]


def _adamw(w, g, m, v, name):
    r, cdim = w.shape
    tr = next((t for t in (512, 352) if r % t == 0 and r > t), r)

    def body(*refs):
        _adamw_update(*refs)

    spec = pl.BlockSpec((tr, cdim), lambda i: (i, 0))
    return pl.pallas_call(
        body, name=name, grid=(r // tr,), in_specs=[spec] * 4, out_specs=[spec] * 3,
        out_shape=[jax.ShapeDtypeStruct((r, cdim), F32)] * 3, compiler_params=_params(1),
    )(w, g, m, v)


def _rope_tables(s):
    rows = s // GRID_W
    freqs = np.float32(ROPE_THETA) ** (-np.arange(0, ROPE_AXIS_DIM, 2, dtype=np.float32) / np.float32(ROPE_AXIS_DIM))
    ang_r = np.arange(rows, dtype=np.float32)[:, None] * freqs[None, :]
    ang_c = np.arange(GRID_W, dtype=np.float32)[:, None] * freqs[None, :]
    zr, zc = np.zeros_like(ang_r), np.zeros_like(ang_c)

    def table(by_row, by_col):
        r = jnp.asarray(np.tile(np.concatenate(by_row + [zr, zr], axis=1), (1, 2)), dtype=F32)
        c = jnp.asarray(np.tile(np.concatenate([zc, zc] + by_col, axis=1), (1, 2)), dtype=F32)
        return jnp.repeat(r, GRID_W, axis=0) + jnp.tile(c, (rows, 1))

    return (table([np.cos(ang_r)] * 2, [np.cos(ang_c)] * 2),
            table([-np.sin(ang_r), np.sin(ang_r)], [-np.sin(ang_c), np.sin(ang_c)]))


def _pack_rows(parts, rows):
    flat = jnp.concatenate([p.reshape(-1) for p in parts])
    return jnp.pad(flat, (0, rows * D_MODEL - flat.shape[0])).reshape(rows, D_MODEL)


def kernel(x, c, ctx, c_ctx, w_mod, b_mod, norm_w, w_in, q_norm_w, k_norm_w, conv_w, conv_b, conv_ln_w, conv_ln_b, w_pw, b_pw, w_out, loss_target, m_c_ctx, m_w_mod, m_b_mod, m_norm_w, m_w_in, m_q_norm_w, m_k_norm_w, m_conv_w, m_conv_b, m_conv_ln_w, m_conv_ln_b, m_w_pw, m_b_pw, m_w_out, v_c_ctx, v_w_mod, v_b_mod, v_norm_w, v_w_in, v_q_norm_w, v_k_norm_w, v_conv_w, v_conv_b, v_conv_ln_w, v_conv_ln_b, v_w_pw, v_b_pw, v_w_out):
    bsz, s, d = x.shape
    cl = ctx.shape[1]
    xi, yi, ci = lax.axis_index("x"), lax.axis_index("y"), lax.axis_index("c")
    chip = 2 * xi + yi
    dev = 2 * chip + ci
    ncol_mod = w_mod.shape[2]

    w_in_t_loc = w_in[0].T.astype(BF16)
    b_cols = lax.dynamic_slice(b_mod, (0, chip * ncol_mod), (1, ncol_mod))
    sc_rows, mod_g, g_in = _front(jnp.pad(c, ((0, 8 - bsz), (0, 0))), jnp.pad(c_ctx[None, :], ((0, 15), (0, 0))),
                                  w_mod[0], b_cols, w_in_t_loc)
    w_in_t = g_in.reshape(D_IN, d)
    mod_all = mod_g.transpose(1, 0, 2).reshape(80, 3 * d)
    mod_loc = lax.dynamic_slice(mod_all, (8 * dev, 0), (bsz, 3 * d))
    shift, scale1p, gate = mod_loc[:, None, :d], 1.0 + mod_loc[:, None, d:2 * d], mod_loc[:, None, 2 * d:]
    shift_c, scale1p_c = mod_all[64:65, :d][None], 1.0 + mod_all[64:65, d:2 * d][None]

    cos, sin = _rope_tables(s)
    qnw512 = jnp.tile(q_norm_w, (1, 8))
    knw128 = jnp.tile(k_norm_w, (1, 2))
    bd512 = jnp.kron(jnp.eye(8, dtype=F32), jnp.ones((HEAD_DIM, HEAD_DIM), F32)).astype(BF16)
    bd128 = bd512[:128, :128]

    tt, ti = min(TILE_TOKENS, s), min(TILE_INPROJ, s)
    u, p_qkv, p_za, p_ga, p_gg, p_zc = _norm_inproj(x, shift, scale1p, norm_w, w_in_t, SPLITS, ti, "norm_inproj")
    uc, pc_kv = _norm_inproj(ctx, shift_c, scale1p_c, norm_w, w_in_t[KV_LO:KV_HI], ((0, 256),), cl, "ctx_norm_inproj")
    q, k2x, v2x, ktx, vtx = _qkv_prep(p_qkv, cos, sin, qnw512, knw128, bd512, bd128, tt, cl)
    k2, v2, kt, vt = _ctx_kv_prep(pc_kv, knw128, bd128, k2x, v2x, ktx, vtx)
    conv_w_loc = jnp.pad(conv_w[0], ((0, 1), (0, 0)))
    o, lse, g_out, g_pw, g_cw = _attn_forward(
        q, k2, vt, min(TILE_ATTN_FWD, s), _ChipGather([w_out[0].astype(BF16), w_pw[0].astype(BF16), conv_w_loc]))
    w_out_f = g_out.reshape(d, d)
    w_pw_f = g_pw.reshape(D_CONV, D_CONV)
    conv_w_f = g_cw.transpose(1, 0, 2).reshape(32, D_CONV)
    y, cv = _conv_forward(p_ga, p_gg, conv_w_f, conv_b, conv_ln_w, conv_ln_b, w_pw_f, b_pw, tt)
    loss_part, dh, do, dza, dcv, dzc, dgate, gw_out = _outproj_loss(
        x, loss_target, gate, o, p_za, cv, p_zc, w_out_f, tt)

    all_chips, half_rows = (0, 1, 2, 3), D_IN // 2
    dy, gw_pw, conv_stats = _conv_token_backward(dcv, y, conv_ln_w, conv_ln_b, w_pw_f, tt)
    da, dg, gcw = _conv_backward(dy, p_ga, p_gg, conv_w_f, tt)
    gw_hi = _weight_grad([(da, half_rows - SPLITS[2][0], 512), (dg, 0, 512), (dzc, 0, 512)], u, None, ti,
                         "grad_in_rows_hi")
    dq, dkt, dvt, r_out, r_pw, r_hi = _attn_backward(
        q, k2, v2, kt, o, do, lse, min(TILE_ATTN_BWD, s),
        _FusedReduce([(gw_out, all_chips), (gw_pw, all_chips), (gw_hi, (2, 3))]))
    dqkv, qk_stats = _qkv_backward(p_qkv, dq, dkt, dvt, cos, sin, qnw512, knw128, bd512, bd128, tt)
    dpc, kc_stats = _ctx_kv_backward(pc_kv, dkt, dvt, knw128, bd128)
    gw_ctx = _weight_grad([(dpc, 0, 256)], uc, None, cl, "grad_in_rows_ctx")
    gw_lo = _weight_grad([(dqkv, 0, 768), (dza, 0, 512), (da, 0, half_rows - SPLITS[2][0])], u, gw_ctx, ti,
                         "grad_in_rows_lo")
    _, modc, gnw_c = _inproj_backward([dpc], ctx, None, scale1p_c, norm_w, w_in_t[KV_LO:KV_HI], cl,
                                      "ctx_inproj_backward")
    grad_x, modx, gnw_x, r_lo = _inproj_backward(
        [dqkv, dza, da, dg, dzc], x, dh, scale1p, norm_w, w_in_t, tt, "inproj_backward",
        _FusedReduce([(gw_lo, (0, 1))], into={0: r_hi}))
    g_w_out, g_w_pw = r_out.reshape(d // 4, d), r_pw.reshape(D_CONV // 4, D_CONV)
    g_w_in_t = r_lo.reshape(D_IN // 4, d)

    dmod_loc = jnp.concatenate([modx[:, 0, :], modx[:, 1, :], dgate[:, 0, :]], axis=1)
    gq = qk_stats[0].reshape(8, HEAD_DIM).sum(axis=0)
    gk = (qk_stats[1, :128] + kc_stats[0]).reshape(2, HEAD_DIM).sum(axis=0)
    packed = _pack_rows([dmod_loc, dmod_loc.sum(axis=0), gnw_x[0] + gnw_c[0], modc[0, 0], modc[0, 1], gq, gk,
                         conv_stats[0], conv_stats[1], conv_stats[2], conv_stats[3], gcw,
                         jnp.sum(loss_part[:, 0, 0])[None]], 32)
    total, g_w_mod, dsilu_ctx = _tail_exchange(packed, sc_rows, w_mod[0], bsz, 3 * bsz + 4)
    flat = total.reshape(-1)
    offs = [0]

    def take(nelem):
        lo = offs[0]
        offs[0] = lo + nelem
        return flat[lo:lo + nelem]

    take(bsz * 3 * d)
    g_b_mod_x = take(3 * d)
    g_norm_w = take(d)
    dshift_c, dscale_c = take(d), take(d)
    g_qnw, g_knw = take(HEAD_DIM), take(HEAD_DIM)
    g_b_pw, g_ln_w, g_ln_b, g_conv_b = take(D_CONV), take(D_CONV), take(D_CONV), take(D_CONV)
    g_conv_w_full = take(32 * D_CONV).reshape(32, D_CONV)
    loss = take(1)[0] * (0.5 / d)

    dmod_c = jnp.concatenate([dshift_c, dscale_c, jnp.zeros((d,), F32)])
    g_b_mod = (g_b_mod_x + dmod_c)[None, :]
    sg = _sigmoid(c_ctx)
    g_c_ctx = dsilu_ctx[0] * (sg * (1.0 + c_ctx * (1.0 - sg)))

    g_w_in = g_w_in_t.T
    g_conv_w = lax.dynamic_slice(g_conv_w_full, (0, chip * 128), (CONV_WIDTH, 128))

    grads = {
        "c_ctx": g_c_ctx, "w_mod": g_w_mod[None], "b_mod": g_b_mod, "norm_w": g_norm_w[None], "w_in": g_w_in[None],
        "q_norm_w": g_qnw[None], "k_norm_w": g_knw[None], "conv_w": g_conv_w[None], "conv_b": g_conv_b[None],
        "conv_ln_w": g_ln_w[None], "conv_ln_b": g_ln_b[None], "w_pw": g_w_pw[None], "b_pw": g_b_pw[None],
        "w_out": g_w_out[None],
    }
    weights = {
        "c_ctx": (c_ctx, m_c_ctx, v_c_ctx), "w_mod": (w_mod, m_w_mod, v_w_mod), "b_mod": (b_mod, m_b_mod, v_b_mod),
        "norm_w": (norm_w, m_norm_w, v_norm_w), "w_in": (w_in, m_w_in, v_w_in),
        "q_norm_w": (q_norm_w, m_q_norm_w, v_q_norm_w), "k_norm_w": (k_norm_w, m_k_norm_w, v_k_norm_w),
        "conv_w": (conv_w, m_conv_w, v_conv_w), "conv_b": (conv_b, m_conv_b, v_conv_b),
        "conv_ln_w": (conv_ln_w, m_conv_ln_w, v_conv_ln_w), "conv_ln_b": (conv_ln_b, m_conv_ln_b, v_conv_ln_b),
        "w_pw": (w_pw, m_w_pw, v_w_pw), "b_pw": (b_pw, m_b_pw, v_b_pw), "w_out": (w_out, m_w_out, v_w_out),
    }
    names = list(weights)
    big = ("w_mod", "w_in")
    as_2d = lambda a: a.reshape((1, a.shape[0]) if a.ndim == 1 else (a.shape[-2] if a.ndim == 3 else 1, a.shape[-1]))
    small = [n for n in names if n not in big]
    w_g_m_v = zip(*[[as_2d(a) for a in (weights[n][0], grads[n], weights[n][1], weights[n][2])] for n in small])
    updates = dict(zip(small, _adamw_small(*[list(col) for col in w_g_m_v])))
    w, m, v = weights["w_mod"]
    updates["w_mod"] = _adamw(as_2d(w), as_2d(grads["w_mod"]), as_2d(m), as_2d(v), "adamw_w_mod")
    w, m, v = weights["w_in"]
    updates["w_in"] = tuple(r.T for r in _adamw(w[0].T, g_w_in_t, m[0].T, v[0].T, "adamw_w_in"))
    deltas, new_ms, new_vs = ([updates[n][k].reshape(weights[n][0].shape) for n in names] for k in range(3))
    grads = {n: grads[n].reshape(weights[n][0].shape) for n in names}

    return (loss, grad_x, *[grads[n] for n in names], *deltas, *new_ms, *new_vs)
```

```python
import functools
import math

import jax
import jax.numpy as jnp
import numpy as np
from jax import lax
from jax.experimental import pallas as pl
from jax.experimental.pallas import tpu as pltpu

F32 = jnp.float32
BF16 = jnp.bfloat16
MESH = pl.DeviceIdType.MESH

D_MODEL = 1024
D_ATTN = 512
D_CONV = 512
HEAD_DIM = 64
N_KV = 2
GRID_W = 64
ROPE_AXIS_DIM = 32
ROPE_THETA = 10000.0
CONV_WIDTH = 31
CONV_PAD = 15
HALO = 16
CONV_ROWS = 32
EPS = 1e-6
SPLITS = ((0, 768), (768, 1280), (1280, 1792), (1792, 2304), (2304, 2816))
D_IN = 2816
KV_LO, KV_HI = 512, 768

ADAM_LR = 0.001
ADAM_B1 = 0.9
ADAM_B2 = 0.999
ADAM_EPS = 1e-08
ADAM_WD = 0.01
ADAM_STEP = 10

VMEM_LIMIT = 56 * 1024 * 1024

TILE_TOKENS = 512
TILE_INPROJ = 1024
TILE_ATTN_FWD = 1024
TILE_ATTN_BWD = 512

NT = (((1,), (1,)), ((), ()))
TN = (((0,), (0,)), ((), ()))


def _params(n_axes=0, **kw):
    if n_axes:
        kw["dimension_semantics"] = ("arbitrary",) * n_axes
    return pltpu.CompilerParams(vmem_limit_bytes=VMEM_LIMIT, **kw)


def _sigmoid(x):
    return 1.0 / (1.0 + jnp.exp(-x))


def _silu_and_grad(z):
    s = _sigmoid(z)
    return z * s, s * (1.0 + z * (1.0 - s))


def _seg_mean(v, ones_bd):
    hi = v.astype(BF16)
    lo = (v - hi.astype(F32)).astype(BF16)
    s = jnp.dot(hi, ones_bd, preferred_element_type=F32) + jnp.dot(lo, ones_bd, preferred_element_type=F32)
    return s * (1.0 / HEAD_DIM)


def _partner(v):
    n = v.shape[1]
    lane = lax.broadcasted_iota(jnp.int32, (v.shape[0], 128), 1)
    first = (lane % 32) < 16
    parts = []
    for k in range(n // 128):
        ch = v[:, 128 * k:128 * (k + 1)]
        parts.append(jnp.where(first, pltpu.roll(ch, 112, 1), pltpu.roll(ch, 16, 1)))
    return parts[0] if len(parts) == 1 else jnp.concatenate(parts, axis=1)


def _tile_lanes(t, reps):
    return t if reps == 1 else jnp.concatenate([t] * reps, axis=1)


def _lo_mask(rows):
    return lax.broadcasted_iota(jnp.int32, (rows, 128), 1) < HEAD_DIM


def _gather8_in_vmem(x_ref, out_ref, send_sems, recv_sems, local_sem):
    x, y, c = lax.axis_index("x"), lax.axis_index("y"), lax.axis_index("c")
    me, sibling = (x, y, c), (x, y, 1 - c)
    chips = [(1 - x, y), (x, 1 - y), (1 - x, 1 - y)]

    def slot(px, py, pc):
        return out_ref.at[4 * px + 2 * py + pc]

    def copy(k, block, to, src=None):
        return pltpu.make_async_remote_copy(
            src_ref=slot(*block) if src is None else src, dst_ref=slot(*block),
            send_sem=send_sems.at[k], recv_sem=recv_sems.at[k], device_id=to, device_id_type=MESH)

    mine = pltpu.make_async_copy(x_ref, slot(*me), local_sem)
    mine.start()
    first = [copy(0, me, sibling, src=x_ref)]
    first += [copy(1 + j, me, (*chip, c), src=x_ref) for j, chip in enumerate(chips)]
    for cp in first:
        cp.start()
    passed = [copy(4 + j, (*chip, c), sibling) for j, chip in enumerate(chips)]
    for j, chip in enumerate(chips):
        copy(1 + j, (*chip, c), me).wait_recv()
        passed[j].start()
    copy(0, sibling, me).wait_recv()
    for j, chip in enumerate(chips):
        copy(4 + j, (*chip, 1 - c), me).wait_recv()
    for cp in first + passed:
        cp.wait_send()
    mine.wait()


class _ChipGather:
    def __init__(self, arrs):
        self.arrs = list(arrs)
        n = self.n = len(self.arrs)
        self.in_specs = [pl.BlockSpec(memory_space=pl.ANY)] * n
        self.out_shape = [jax.ShapeDtypeStruct((4,) + a.shape, a.dtype) for a in self.arrs]
        self.out_specs = [pl.BlockSpec(memory_space=pl.ANY)] * n
        self.scratch = [pltpu.SemaphoreType.DMA((6 * n,)), pltpu.SemaphoreType.DMA((6 * n,)),
                        pltpu.SemaphoreType.DMA((n,))]
        self.phases = [self.start, self.forward, self.finish]

    def bind(self, ins, outs, scratch):
        self.ins, self.outs = ins, outs
        self.send_sems, self.recv_sems, self.local_sems = scratch
        self.x, self.y, self.c = lax.axis_index("x"), lax.axis_index("y"), lax.axis_index("c")
        self.chips = [(1 - self.x, self.y), (self.x, 1 - self.y), (1 - self.x, 1 - self.y)]
        self.mychip = 2 * self.x + self.y

    def _copy(self, a, k, chip_idx, cc, to, src=None):
        h = self.arrs[a].shape[0] // 2
        dst = self.outs[a].at[chip_idx, pl.ds(cc * h, h)]
        return pltpu.make_async_remote_copy(
            src_ref=dst if src is None else src, dst_ref=dst, send_sem=self.send_sems.at[6 * a + k],
            recv_sem=self.recv_sems.at[6 * a + k], device_id=to, device_id_type=MESH)

    def _local(self, a):
        return pltpu.make_async_copy(self.ins[a], self.outs[a].at[self.mychip], self.local_sems.at[a])

    def _first(self, a, j):
        h = self.arrs[a].shape[0] // 2
        return self._copy(a, j, self.mychip, self.c, (*self.chips[j], self.c), src=self.ins[a].at[pl.ds(self.c * h, h)])

    def _passed(self, a, j):
        cx, cy = self.chips[j]
        return self._copy(a, 3 + j, 2 * cx + cy, self.c, (self.x, self.y, 1 - self.c))

    def start(self):
        for a in range(self.n):
            self._local(a).start()
            for j in range(3):
                self._first(a, j).start()

    def forward(self):
        for a in range(self.n):
            for j, (cx, cy) in enumerate(self.chips):
                self._copy(a, j, 2 * cx + cy, self.c, (self.x, self.y, self.c)).wait_recv()
                self._passed(a, j).start()

    def finish(self):
        for a in range(self.n):
            for j, (cx, cy) in enumerate(self.chips):
                self._copy(a, 3 + j, 2 * cx + cy, 1 - self.c, (self.x, self.y, self.c)).wait_recv()
        for a in range(self.n):
            for j in range(3):
                self._first(a, j).wait_send()
                self._passed(a, j).wait_send()
            self._local(a).wait()


class _FusedReduce:
    def __init__(self, pieces):
        self.owners = [tuple(o) for _, o in pieces]
        self.arrs = [g.reshape(len(o), 2, g.shape[0] // (2 * len(o)), g.shape[1]) for g, o in pieces]
        n = self.n = len(pieces)
        hc = self.hc = [(v.shape[2], v.shape[3]) for v in self.arrs]
        nts = [len(o) for o in self.owners]
        self.base = [sum(nts[:p]) for p in range(n)]
        anyspec = pl.BlockSpec(memory_space=pl.ANY)
        self.in_specs = [anyspec] * n
        self.out_shape = [jax.ShapeDtypeStruct((2,) + s, F32) for s in hc]
        self.out_specs = [anyspec] * n
        self.scratch = [pltpu.VMEM((nt,) + s, F32) for nt, s in zip(nts, hc)]
        self.scratch += [pltpu.VMEM((nt,) + s, F32) for nt, s in zip(nts, hc)]
        self.scratch += [pltpu.VMEM(s, F32) for s in hc]
        self.scratch += [pltpu.VMEM((nt,) + s, BF16) for nt, s in zip(nts, hc)]
        self.scratch += [pltpu.VMEM((3,) + s, BF16) for s in hc]
        self.scratch += [pltpu.VMEM(s, F32) for s in hc]
        tot = sum(nts)
        self.scratch += [pltpu.SemaphoreType.DMA((tot,)), pltpu.SemaphoreType.DMA((tot,)),
                         pltpu.SemaphoreType.DMA((tot,)), pltpu.SemaphoreType.DMA((3 * n,)),
                         pltpu.SemaphoreType.DMA((n,)), pltpu.SemaphoreType.DMA((n,)), pltpu.SemaphoreType.DMA((n,)),
                         pltpu.SemaphoreType.DMA((tot,))]
        self.phases = [self.start, self.exchange, self.combine, self.finish]

    def bind(self, ins, outs, scratch):
        n = self.n
        self.g, self.out = ins, outs
        self.va, self.recv_a, self.own = scratch[:n], scratch[n:2 * n], scratch[2 * n:3 * n]
        self.tsend, self.recv_b, self.fin = scratch[3 * n:4 * n], scratch[4 * n:5 * n], scratch[5 * n:6 * n]
        self.sa, self.ra, self.sb, self.rb, self.sc, self.rc, self.lc, self.la = scratch[6 * n:]
        self.x, self.y, self.c = lax.axis_index("x"), lax.axis_index("y"), lax.axis_index("c")
        self.mychip = 2 * self.x + self.y
        self.sibling = (self.x, self.y, 1 - self.c)

    def _copy_a(self, p, t):
        k = self.base[p] + t
        return pltpu.make_async_remote_copy(
            src_ref=self.g[p].at[t, 1 - self.c], dst_ref=self.recv_a[p].at[t], send_sem=self.sa.at[k],
            recv_sem=self.ra.at[k], device_id=self.sibling, device_id_type=MESH)

    def _fetch(self, p, t):
        return pltpu.make_async_copy(self.g[p].at[t, self.c], self.va[p].at[t], self.la.at[self.base[p] + t])

    def _slot(self, owner):
        rel = jnp.bitwise_xor(self.mychip, owner)
        return jnp.where(rel == 2, 0, jnp.where(rel == 1, 1, 2))

    def _copy_b(self, p, t, slot):
        owner = self.owners[p][t]
        return pltpu.make_async_remote_copy(
            src_ref=self.tsend[p].at[t], dst_ref=self.recv_b[p].at[slot], send_sem=self.sb.at[self.base[p] + t],
            recv_sem=self.rb.at[3 * p + slot], device_id=(owner // 2, owner % 2, self.c), device_id_type=MESH)

    def _copy_c(self, p, half):
        return pltpu.make_async_remote_copy(
            src_ref=self.fin[p], dst_ref=self.out[p].at[half], send_sem=self.sc.at[p], recv_sem=self.rc.at[p],
            device_id=self.sibling, device_id_type=MESH)

    def _local_c(self, p):
        return pltpu.make_async_copy(self.fin[p], self.out[p].at[self.c], self.lc.at[p])

    def start(self):
        for p in range(self.n):
            for t in range(len(self.owners[p])):
                self._copy_a(p, t).start()
                self._fetch(p, t).start()

    def exchange(self):
        for p in range(self.n):
            for t, owner in enumerate(self.owners[p]):
                self._copy_a(p, t).wait_recv()
                self._fetch(p, t).wait()
                mine = self.mychip == owner

                @pl.when(mine)
                def _():
                    self.own[p][...] = self.va[p][t] + self.recv_a[p][t]

                @pl.when(jnp.logical_not(mine))
                def _():
                    self.tsend[p][t] = (self.va[p][t] + self.recv_a[p][t]).astype(BF16)
                    self._copy_b(p, t, self._slot(owner)).start()

    def combine(self):
        for p in range(self.n):
            for t, owner in enumerate(self.owners[p]):
                @pl.when(self.mychip == owner)
                def _():
                    acc = self.own[p][...]
                    for j in range(3):
                        self._copy_b(p, t, j).wait_recv()
                        acc = acc + self.recv_b[p][j].astype(F32)
                    self.fin[p][...] = acc
                    self._local_c(p).start()
                    self._copy_c(p, self.c).start()

    def finish(self):
        for p in range(self.n):
            for t, owner in enumerate(self.owners[p]):
                self._copy_a(p, t).wait_send()
                mine = self.mychip == owner

                @pl.when(mine)
                def _():
                    self._copy_c(p, 1 - self.c).wait_recv()
                    self._copy_c(p, self.c).wait_send()
                    self._local_c(p).wait()

                @pl.when(jnp.logical_not(mine))
                def _():
                    self._copy_b(p, t, self._slot(owner)).wait_send()


def _split_fused(refs, n_in, n_out, n_scr, fused):
    if fused is None:
        return refs[:n_in], refs[n_in:n_in + n_out], refs[n_in + n_out:]
    fi, fo = len(fused.in_specs), len(fused.out_specs)
    ins, rest = refs[:n_in], refs[n_in:]
    f_ins, rest = rest[:fi], rest[fi:]
    outs, rest = rest[:n_out], rest[n_out:]
    f_outs, rest = rest[:fo], rest[fo:]
    scr, f_scr = rest[:n_scr], rest[n_scr:]
    fused.bind(f_ins, f_outs, f_scr)
    return ins, outs, scr


def _run_phases(fused, step, at_steps, before):
    if fused is None:
        return
    for phase, (at, first) in zip(fused.phases, at_steps):
        if first == before:
            pl.when(step == at)(phase)


def _front(c_pad, c_ctx_rows, w_mod, b_cols, w_in_t_loc):
    ncol = w_mod.shape[1]
    gather = _ChipGather([w_in_t_loc])

    def body(c_ref, cctx_ref, w_ref, b_ref, win_ref, sc_ref, modg_ref, wing_ref,
             call_ref, ag_send, ag_recv, ag_local, m_send, m_recv, *g_scr):
        gather.bind([win_ref], [wing_ref], g_scr)
        _gather8_in_vmem(c_ref, call_ref, ag_send, ag_recv, ag_local)
        gather.start()
        x, y, c = lax.axis_index("x"), lax.axis_index("y"), lax.axis_index("c")
        chips = [(1 - x, y), (x, 1 - y), (1 - x, 1 - y)]
        mychip = 2 * x + y
        rows = jnp.concatenate([call_ref[dv] for dv in range(8)] + [cctx_ref[...]], axis=0)
        sc = rows * _sigmoid(rows)
        sc_ref[...] = sc
        modg_ref[mychip] = jnp.dot(sc, w_ref[...], preferred_element_type=F32,
                                   precision=lax.Precision.HIGHEST) + b_ref[...]

        def mcopy(j, chip_idx, to):
            return pltpu.make_async_remote_copy(
                src_ref=modg_ref.at[chip_idx], dst_ref=modg_ref.at[chip_idx], send_sem=m_send.at[j],
                recv_sem=m_recv.at[j], device_id=to, device_id_type=MESH)

        sends = [mcopy(j, mychip, (*chip, c)) for j, chip in enumerate(chips)]
        for cp in sends:
            cp.start()
        for j, (cx, cy) in enumerate(chips):
            mcopy(j, 2 * cx + cy, (x, y, c)).wait_recv()
        gather.forward()
        gather.finish()
        for cp in sends:
            cp.wait_send()

    vm = pl.BlockSpec(memory_space=pltpu.VMEM)
    return pl.pallas_call(
        body, name="front_exchange",
        out_shape=[jax.ShapeDtypeStruct((80, D_MODEL), F32), jax.ShapeDtypeStruct((4, 80, ncol), F32)] + gather.out_shape,
        in_specs=[vm, vm, vm, vm] + gather.in_specs, out_specs=[vm, vm] + gather.out_specs,
        scratch_shapes=[pltpu.VMEM((8, 8, D_MODEL), F32), pltpu.SemaphoreType.DMA((7,)), pltpu.SemaphoreType.DMA((7,)),
                        pltpu.SemaphoreType.DMA, pltpu.SemaphoreType.DMA((3,)), pltpu.SemaphoreType.DMA((3,))]
        + gather.scratch,
        compiler_params=_params(),
    )(c_pad, c_ctx_rows, w_mod, b_cols, w_in_t_loc)


def _tail_exchange(packed, sc_rows, w_mod, bsz, ctx_row):
    d = D_MODEL
    ncol = w_mod.shape[1]

    def body(p_ref, sc_ref, w_ref, total_ref, gw_ref, gcc_ref, gat_ref, dm_ref, part_ref,
             ag_send, ag_recv, ag_local, g_send, g_recv):
        _gather8_in_vmem(p_ref, gat_ref, ag_send, ag_recv, ag_local)
        acc = gat_ref[0]
        for dv in range(1, 8):
            acc = acc + gat_ref[dv]
        total_ref[...] = acc
        x, y, c = lax.axis_index("x"), lax.axis_index("y"), lax.axis_index("c")
        chips = [(1 - x, y), (x, 1 - y), (1 - x, 1 - y)]
        mychip = 2 * x + y
        dm_ref[...] = jnp.zeros(dm_ref.shape, F32)
        for k in range(4):
            @pl.when(mychip == k)
            def _():
                spans = [(seg, max(k * ncol, seg * d) - seg * d, min((k + 1) * ncol, (seg + 1) * d) - seg * d)
                         for seg in range(3) if k * ncol < (seg + 1) * d and (k + 1) * ncol > seg * d]
                for dv in range(8):
                    for b in range(bsz):
                        dm_ref[8 * dv + b:8 * dv + b + 1, :] = jnp.concatenate(
                            [gat_ref[dv, 3 * b + seg:3 * b + seg + 1, lo:hi] for seg, lo, hi in spans], axis=1)
                dm_ref[64:65, :] = jnp.concatenate(
                    [total_ref[ctx_row + seg:ctx_row + seg + 1, lo:hi] if seg < 2 else jnp.zeros((1, hi - lo), F32)
                     for seg, lo, hi in spans], axis=1)

        dm = dm_ref[...]
        gw_ref[...] = lax.dot_general(sc_ref[...], dm, TN, preferred_element_type=F32,
                                      precision=lax.Precision.HIGHEST)
        part_ref[mychip] = lax.dot_general(dm[64:72, :], w_ref[...], NT, preferred_element_type=F32,
                                           precision=lax.Precision.HIGHEST)

        def gcopy(j, chip_idx, to):
            return pltpu.make_async_remote_copy(
                src_ref=part_ref.at[chip_idx], dst_ref=part_ref.at[chip_idx], send_sem=g_send.at[j],
                recv_sem=g_recv.at[j], device_id=to, device_id_type=MESH)

        sends = [gcopy(j, mychip, (*chip, c)) for j, chip in enumerate(chips)]
        for cp in sends:
            cp.start()
        for j, (cx, cy) in enumerate(chips):
            gcopy(j, 2 * cx + cy, (x, y, c)).wait_recv()
        for cp in sends:
            cp.wait_send()
        gcc_ref[...] = (part_ref[0] + part_ref[1]) + (part_ref[2] + part_ref[3])

    return pl.pallas_call(
        body, name="tail_exchange",
        out_shape=[jax.ShapeDtypeStruct(packed.shape, F32), jax.ShapeDtypeStruct((d, ncol), F32),
                   jax.ShapeDtypeStruct((8, d), F32)],
        scratch_shapes=[pltpu.VMEM((8,) + packed.shape, F32), pltpu.VMEM((80, ncol), F32), pltpu.VMEM((4, 8, d), F32),
                        pltpu.SemaphoreType.DMA((7,)), pltpu.SemaphoreType.DMA((7,)), pltpu.SemaphoreType.DMA,
                        pltpu.SemaphoreType.DMA((3,)), pltpu.SemaphoreType.DMA((3,))],
        compiler_params=_params(),
    )(packed, sc_rows, w_mod)


def _bcast_spec(arr):
    if arr.shape[0] == 1:
        return pl.BlockSpec((1, 1, arr.shape[2]), lambda b, i: (0, 0, 0))
    return pl.BlockSpec((1, 1, arr.shape[2]), lambda b, i: (b, 0, 0))


def _norm_inproj(x, shift, scale1p, norm_w, w_t, splits, tm, name):
    bsz, s, d = x.shape

    def body(x_ref, sh_ref, sc_ref, nw_ref, w_ref, u_ref, *out_refs):
        xv = x_ref[0]
        rstd = lax.rsqrt(jnp.mean(xv * xv, axis=-1, keepdims=True) + EPS)
        u = (xv * rstd * nw_ref[...]) * sc_ref[0] + sh_ref[0]
        ub = u.astype(BF16)
        u_ref[0] = ub
        for (lo, hi), o_ref in zip(splits, out_refs):
            o_ref[0] = lax.dot_general(ub, w_ref[lo:hi, :], NT, preferred_element_type=F32)

    tok = lambda w: pl.BlockSpec((1, tm, w), lambda b, i: (b, i, 0))
    return pl.pallas_call(
        body, name=name, grid=(bsz, s // tm),
        in_specs=[tok(d), _bcast_spec(shift), _bcast_spec(scale1p), pl.BlockSpec((1, d), lambda b, i: (0, 0)),
                  pl.BlockSpec(w_t.shape, lambda b, i: (0, 0))],
        out_specs=[tok(d)] + [tok(hi - lo) for lo, hi in splits],
        out_shape=[jax.ShapeDtypeStruct((bsz, s, d), BF16)]
        + [jax.ShapeDtypeStruct((bsz, s, hi - lo), F32) for lo, hi in splits],
        compiler_params=_params(2),
    )(x, shift, scale1p, norm_w, w_t)


def _dup_heads(kv, lo_mask):
    r = pltpu.roll(kv, HEAD_DIM, 1)
    return jnp.where(lo_mask, kv, r), jnp.where(lo_mask, r, kv)


def _qkv_prep(qkv, cos, sin, qnw, knw, bd512, bd128, ts, extra):
    bsz, s, _ = qkv.shape

    def body(p_ref, cos_ref, sin_ref, qnw_ref, knw_ref, bd512_ref, bd128_ref, q_ref, k_ref, v_ref, kt_ref, vt_ref):
        lo_mask = _lo_mask(ts)
        cos_t, sin_t = cos_ref[...], sin_ref[...]
        qp = p_ref[0, :, 0:512]
        qn = qp * lax.rsqrt(_seg_mean(qp * qp, bd512_ref[...]) + EPS) * qnw_ref[...]
        qr = qn * _tile_lanes(cos_t, 4) + _partner(qn) * _tile_lanes(sin_t, 4)
        q_ref[0] = (qr * (1.0 / math.sqrt(HEAD_DIM))).astype(BF16)
        kp = p_ref[0, :, 512:640]
        kn = kp * lax.rsqrt(_seg_mean(kp * kp, bd128_ref[...]) + EPS) * knw_ref[...]
        kr = kn * cos_t + _partner(kn) * sin_t
        k0, k1 = _dup_heads(kr, lo_mask)
        k_ref[0, 0] = k0.astype(BF16)
        k_ref[0, 1] = k1.astype(BF16)
        vp = p_ref[0, :, 640:768]
        v0, v1 = _dup_heads(vp, lo_mask)
        v_ref[0, 0] = v0.astype(BF16)
        v_ref[0, 1] = v1.astype(BF16)
        kt_ref[0] = kr.T.astype(BF16)
        vt_ref[0] = vp.T.astype(BF16)

    const = lambda a: pl.BlockSpec(a.shape, lambda b, i: (0,) * a.ndim)
    kv_spec = pl.BlockSpec((1, 2, ts, 128), lambda b, i: (b, 0, i, 0))
    t_spec = pl.BlockSpec((1, 128, ts), lambda b, i: (b, 0, i))
    return pl.pallas_call(
        body, name="qkv_prep", grid=(bsz, s // ts),
        in_specs=[pl.BlockSpec((1, ts, 768), lambda b, i: (b, i, 0)),
                  pl.BlockSpec((ts, 128), lambda b, i: (i, 0)), pl.BlockSpec((ts, 128), lambda b, i: (i, 0)),
                  const(qnw), const(knw), const(bd512), const(bd128)],
        out_specs=[pl.BlockSpec((1, ts, 512), lambda b, i: (b, i, 0)), kv_spec, kv_spec, t_spec, t_spec],
        out_shape=[jax.ShapeDtypeStruct((bsz, s, 512), BF16), jax.ShapeDtypeStruct((bsz, 2, s + extra, 128), BF16),
                   jax.ShapeDtypeStruct((bsz, 2, s + extra, 128), BF16),
                   jax.ShapeDtypeStruct((bsz, 128, s + extra), BF16), jax.ShapeDtypeStruct((bsz, 128, s + extra), BF16)],
        compiler_params=_params(2),
    )(qkv, cos, sin, qnw, knw, bd512, bd128)


def _ctx_kv_prep(pc, knw, bd128, k2, v2, kt, vt):
    bsz, cl, _ = pc.shape
    blk = k2.shape[2] // cl - 1

    def body(p_ref, knw_ref, bd128_ref, k_in, v_in, kt_in, vt_in, k_ref, v_ref, kt_ref, vt_ref):
        lo_mask = _lo_mask(cl)
        kp = p_ref[0, :, 0:128]
        kn = kp * lax.rsqrt(_seg_mean(kp * kp, bd128_ref[...]) + EPS) * knw_ref[...]
        k0, k1 = _dup_heads(kn, lo_mask)
        k_ref[0, 0] = k0.astype(BF16)
        k_ref[0, 1] = k1.astype(BF16)
        vp = p_ref[0, :, 128:256]
        v0, v1 = _dup_heads(vp, lo_mask)
        v_ref[0, 0] = v0.astype(BF16)
        v_ref[0, 1] = v1.astype(BF16)
        kt_ref[0] = kn.T.astype(BF16)
        vt_ref[0] = vp.T.astype(BF16)

    const = lambda a: pl.BlockSpec(a.shape, lambda b: (0,) * a.ndim)
    kv_spec = pl.BlockSpec((1, 2, cl, 128), lambda b: (b, 0, blk, 0))
    t_spec = pl.BlockSpec((1, 128, cl), lambda b: (b, 0, blk))
    anyspec = pl.BlockSpec(memory_space=pl.ANY)
    return pl.pallas_call(
        body, name="ctx_kv_prep", grid=(bsz,),
        in_specs=[pl.BlockSpec((1, cl, 256), lambda b: (b, 0, 0)), const(knw), const(bd128)] + [anyspec] * 4,
        out_specs=[kv_spec, kv_spec, t_spec, t_spec],
        out_shape=[jax.ShapeDtypeStruct(a.shape, BF16) for a in (k2, v2, kt, vt)],
        input_output_aliases={3: 0, 4: 1, 5: 2, 6: 3},
        compiler_params=_params(1),
    )(pc, knw, bd128, k2, v2, kt, vt)


def _attn_forward(q, k2, vt, tq, fused):
    bsz, s, _ = q.shape
    sk = k2.shape[2]
    nq = s // tq
    total = bsz * N_KV * nq
    at_steps = [(0, True), (total // 4, True), (total - 1, False)]

    def body(*refs):
        (q_ref, k_ref, vt_ref), (o_ref, lse_ref), _ = _split_fused(refs, 3, 2, 0, fused)
        g = pl.program_id(1)
        step = (pl.program_id(0) * N_KV + g) * nq + pl.program_id(2)
        _run_phases(fused, step, at_steps, True)
        kk = k_ref[0, 0]
        lo_mask = _lo_mask(tq)
        vt_aug = jnp.concatenate([vt_ref[0, pl.ds(pl.multiple_of(g * HEAD_DIM, HEAD_DIM), HEAD_DIM), :],
                                  jnp.ones((16, sk), BF16)], axis=0)
        ps, ms = [], []
        for j in range(2):
            qp = q_ref[0, :, 128 * j:128 * (j + 1)]
            for half in range(2):
                sel = lo_mask if half == 0 else jnp.logical_not(lo_mask)
                qs = jnp.where(sel, qp, jnp.zeros_like(qp))
                sc = lax.dot_general(qs, kk, NT, preferred_element_type=F32)
                m = jnp.max(sc, axis=-1, keepdims=True)
                ps.append(jnp.exp(sc - m).astype(BF16))
                ms.append(m)
        ots = [lax.dot_general(vt_aug, p, NT, preferred_element_type=F32) for p in ps]
        for j in range(2):
            o_t, l_t = [], []
            for half in range(2):
                ot = ots[2 * j + half]
                l = ot[HEAD_DIM:HEAD_DIM + 1, :]
                o_t.append(ot[0:HEAD_DIM, :] / l)
                l_t.append(jnp.broadcast_to(l, (HEAD_DIM, tq)))
            o_ref[0, :, 128 * j:128 * (j + 1)] = jnp.concatenate(o_t, axis=0).T
            lse_ref[0, :, 128 * j:128 * (j + 1)] = (jnp.where(lo_mask, ms[2 * j], ms[2 * j + 1])
                                                    + jnp.log(jnp.concatenate(l_t, axis=0).T))
        _run_phases(fused, step, at_steps, False)

    q_spec = pl.BlockSpec((1, tq, 256), lambda b, g, i: (b, i, g))
    kv_spec = pl.BlockSpec((1, 1, sk, 128), lambda b, g, i: (b, g, 0, 0))
    return pl.pallas_call(
        body, name="attn_forward", grid=(bsz, N_KV, nq),
        in_specs=[q_spec, kv_spec, pl.BlockSpec((1, 128, sk), lambda b, g, i: (b, 0, 0))] + fused.in_specs,
        out_specs=[q_spec, q_spec] + fused.out_specs,
        out_shape=[jax.ShapeDtypeStruct((bsz, s, 512), F32)] * 2 + fused.out_shape,
        scratch_shapes=fused.scratch,
        compiler_params=_params(3),
    )(q, k2, vt, *fused.arrs)


def _halo_specs(width, ts, s):
    r = ts // HALO
    last = s // HALO - 1
    return [pl.BlockSpec((1, ts, width), lambda b, i: (b, i, 0)),
            pl.BlockSpec((1, HALO, width), lambda b, i: (b, jnp.maximum(i * r - 1, 0), 0)),
            pl.BlockSpec((1, HALO, width), lambda b, i: (b, jnp.minimum((i + 1) * r, last), 0))]


def _fill_ext(ext_ref, cur, prev, nxt, i, n_tiles, ts):
    ext_ref[0:HALO, :] = jnp.where(i > 0, prev, jnp.zeros_like(prev))
    ext_ref[HALO:HALO + ts, :] = cur
    ext_ref[HALO + ts:2 * HALO + ts, :] = jnp.where(i < n_tiles - 1, nxt, jnp.zeros_like(nxt))


def _fill_shifted(sh_ref, ext_ref, ts):
    n = ts + 2 * HALO - 8
    for r in range(1, 8):
        sh_ref[r - 1, 0:n, :] = ext_ref[pl.ds(r, n), :]


def _window(sh_ref, ext_ref, off, rows, r0=0):
    q, r = divmod(off, 8)
    if r == 0:
        return ext_ref[pl.ds(r0 + off, rows), :]
    return sh_ref[r - 1, pl.ds(r0 + 8 * q, rows), :]


def _conv_forward(ga, gg, conv_w, conv_b, ln_w, ln_b, w_pw, b_pw, ts):
    bsz, s, dc = ga.shape
    n_tiles = s // ts

    def body(a_ref, ap_ref, an_ref, g_ref, gp_ref, gn_ref, cw_ref, cb_ref, lw_ref, lb_ref, wp_ref, bp_ref,
             y_ref, cv_ref, ext_ref, sh_ref):
        i = pl.program_id(1)
        glu = lambda a, g: a * _sigmoid(g)
        _fill_ext(ext_ref, glu(a_ref[0], g_ref[0]), glu(ap_ref[0], gp_ref[0]), glu(an_ref[0], gn_ref[0]), i, n_tiles, ts)
        _fill_shifted(sh_ref, ext_ref, ts)
        acc = jnp.broadcast_to(cb_ref[...], (ts, dc))
        for j in range(CONV_WIDTH):
            acc = acc + cw_ref[j:j + 1, :] * _window(sh_ref, ext_ref, HALO - CONV_PAD + j, ts)
        y_ref[0] = acc
        mu = jnp.mean(acc, axis=-1, keepdims=True)
        yc = acc - mu
        var = jnp.mean(yc * yc, axis=-1, keepdims=True)
        yn = yc * lax.rsqrt(var + EPS) * lw_ref[...] + lb_ref[...]
        ys = yn * _sigmoid(yn)
        cv_ref[0] = jnp.dot(ys.astype(BF16), wp_ref[...], preferred_element_type=F32) + bp_ref[...]

    const = lambda a: pl.BlockSpec(a.shape, lambda b, i: (0,) * a.ndim)
    return pl.pallas_call(
        body, name="conv_forward", grid=(bsz, n_tiles),
        in_specs=_halo_specs(dc, ts, s) + _halo_specs(dc, ts, s)
        + [const(conv_w), const(conv_b), const(ln_w), const(ln_b), const(w_pw), const(b_pw)],
        out_specs=[pl.BlockSpec((1, ts, dc), lambda b, i: (b, i, 0))] * 2,
        out_shape=[jax.ShapeDtypeStruct((bsz, s, dc), F32)] * 2,
        scratch_shapes=[pltpu.VMEM((ts + 2 * HALO, dc), F32), pltpu.VMEM((7, ts + 2 * HALO, dc), F32)],
        compiler_params=_params(2),
    )(ga, ga, ga, gg, gg, gg, conv_w, conv_b, ln_w, ln_b, w_pw, b_pw)


def _outproj_loss(x, target, gate, o, za, cv, zc, w_out, y, ln_w, ln_b, w_pw, tm):
    bsz, s, d = x.shape

    def body(x_ref, t_ref, gate_ref, o_ref, za_ref, cv_ref, zc_ref, w_ref, y_ref, lw_ref, lb_ref, wp_ref,
             loss_ref, dh_ref, do_ref, dza_ref, dcv_ref, dzc_ref, dgate_ref, gw_ref, gwp_ref, st_ref):
        b, i = pl.program_id(0), pl.program_id(1)
        ov, cvv = o_ref[0], cv_ref[0]
        silu_a, dsilu_a = _silu_and_grad(za_ref[0])
        silu_c, dsilu_c = _silu_and_grad(zc_ref[0])
        mix = jnp.concatenate([ov * silu_a, cvv * silu_c], axis=1).astype(BF16)
        out = jnp.dot(mix, w_ref[...], preferred_element_type=F32)
        gate_v = gate_ref[0]
        err = x_ref[0] + gate_v * out - t_ref[0]
        dh = err * (1.0 / d)
        dh_ref[0] = dh
        dout = (dh * gate_v).astype(BF16)
        dmix = lax.dot_general(dout, w_ref[...], NT, preferred_element_type=F32)
        gw = lax.dot_general(mix, dout, TN, preferred_element_type=F32)
        dg = jnp.sum(dh * out, axis=0, keepdims=True)
        sq = jnp.sum(err * err)

        @pl.when(jnp.logical_and(b == 0, i == 0))
        def _():
            gw_ref[...] = gw

        @pl.when(jnp.logical_or(b > 0, i > 0))
        def _():
            gw_ref[...] += gw

        @pl.when(i == 0)
        def _():
            dgate_ref[0] = dg
            loss_ref[...] = jnp.zeros(loss_ref.shape, F32) + sq

        @pl.when(i > 0)
        def _():
            dgate_ref[0] += dg
            loss_ref[...] += sq

        dma, dmc = dmix[:, :D_ATTN], dmix[:, D_ATTN:]
        do_ref[0] = dma * silu_a
        dza_ref[0] = (dma * ov * dsilu_a).astype(BF16)
        dzc_ref[0] = (dmc * cvv * dsilu_c).astype(BF16)

        dcvv = dmc * silu_c
        yv = y_ref[0]
        yc = yv - jnp.mean(yv, axis=-1, keepdims=True)
        rstd = lax.rsqrt(jnp.mean(yc * yc, axis=-1, keepdims=True) + EPS)
        yhat = yc * rstd
        ys, dsilu = _silu_and_grad(yhat * lw_ref[...] + lb_ref[...])
        dcvb = dcvv.astype(BF16)
        gwp = lax.dot_general(ys.astype(BF16), dcvb, TN, preferred_element_type=F32)
        dyn = lax.dot_general(dcvb, wp_ref[...], NT, preferred_element_type=F32) * dsilu
        dyhat = dyn * lw_ref[...]
        dy = rstd * (dyhat - jnp.mean(dyhat, axis=-1, keepdims=True)
                     - yhat * jnp.mean(dyhat * yhat, axis=-1, keepdims=True))
        dcv_ref[0] = dy
        red = lambda v: jnp.sum(v, axis=0, keepdims=True)
        stats = jnp.concatenate([red(dcvv), red(dyn * yhat), red(dyn), red(dy), jnp.zeros((4, D_CONV), F32)], axis=0)

        @pl.when(jnp.logical_and(b == 0, i == 0))
        def _():
            gwp_ref[...] = gwp
            st_ref[...] = stats

        @pl.when(jnp.logical_or(b > 0, i > 0))
        def _():
            gwp_ref[...] += gwp
            st_ref[...] += stats

    tok = lambda w: pl.BlockSpec((1, tm, w), lambda b, i: (b, i, 0))
    const = lambda a: pl.BlockSpec(a.shape, lambda b, i: (0,) * a.ndim)
    dc = D_CONV
    return pl.pallas_call(
        body, name="outproj_loss", grid=(bsz, s // tm),
        in_specs=[tok(d), tok(d), _bcast_spec(gate), tok(512), tok(512), tok(512), tok(512), const(w_out),
                  tok(dc), const(ln_w), const(ln_b), const(w_pw)],
        out_specs=[pl.BlockSpec((1, 8, 128), lambda b, i: (b, 0, 0)), tok(d), tok(512), tok(512), tok(512), tok(512),
                   pl.BlockSpec((1, 1, d), lambda b, i: (b, 0, 0)), pl.BlockSpec((d, d), lambda b, i: (0, 0)),
                   pl.BlockSpec((dc, dc), lambda b, i: (0, 0)), pl.BlockSpec((8, dc), lambda b, i: (0, 0))],
        out_shape=[jax.ShapeDtypeStruct((bsz, 8, 128), F32), jax.ShapeDtypeStruct((bsz, s, d), F32),
                   jax.ShapeDtypeStruct((bsz, s, 512), F32), jax.ShapeDtypeStruct((bsz, s, 512), BF16),
                   jax.ShapeDtypeStruct((bsz, s, 512), F32), jax.ShapeDtypeStruct((bsz, s, 512), BF16),
                   jax.ShapeDtypeStruct((bsz, 1, d), F32), jax.ShapeDtypeStruct((d, d), F32),
                   jax.ShapeDtypeStruct((dc, dc), F32), jax.ShapeDtypeStruct((8, dc), F32)],
        compiler_params=_params(2),
    )(x, target, gate, o, za, cv, zc, w_out, y, ln_w, ln_b, w_pw)


def _conv_token_backward(dcv, y, ln_w, ln_b, w_pw, tm):
    bsz, s, dc = dcv.shape

    def body(dcv_ref, y_ref, lw_ref, lb_ref, wp_ref, dy_ref, gwp_ref, st_ref):
        b, i = pl.program_id(0), pl.program_id(1)
        yv, dcvv = y_ref[0], dcv_ref[0]
        mu = jnp.mean(yv, axis=-1, keepdims=True)
        yc = yv - mu
        rstd = lax.rsqrt(jnp.mean(yc * yc, axis=-1, keepdims=True) + EPS)
        yhat = yc * rstd
        yn = yhat * lw_ref[...] + lb_ref[...]
        ys, dsilu = _silu_and_grad(yn)
        dcvb = dcvv.astype(BF16)
        gwp = lax.dot_general(ys.astype(BF16), dcvb, TN, preferred_element_type=F32)
        dys = lax.dot_general(dcvb, wp_ref[...], NT, preferred_element_type=F32)
        dyn = dys * dsilu
        dyhat = dyn * lw_ref[...]
        dy = rstd * (dyhat - jnp.mean(dyhat, axis=-1, keepdims=True)
                     - yhat * jnp.mean(dyhat * yhat, axis=-1, keepdims=True))
        dy_ref[0] = dy
        red = lambda v: jnp.sum(v, axis=0, keepdims=True)
        stats = jnp.concatenate([red(dcvv), red(dyn * yhat), red(dyn), red(dy), jnp.zeros((4, dc), F32)], axis=0)
        first = jnp.logical_and(b == 0, i == 0)

        @pl.when(first)
        def _():
            gwp_ref[...] = gwp
            st_ref[...] = stats

        @pl.when(jnp.logical_not(first))
        def _():
            gwp_ref[...] += gwp
            st_ref[...] += stats

    tok = pl.BlockSpec((1, tm, dc), lambda b, i: (b, i, 0))
    const = lambda a: pl.BlockSpec(a.shape, lambda b, i: (0,) * a.ndim)
    return pl.pallas_call(
        body, name="conv_token_backward", grid=(bsz, s // tm),
        in_specs=[tok, tok, const(ln_w), const(ln_b), const(w_pw)],
        out_specs=[tok, pl.BlockSpec((dc, dc), lambda b, i: (0, 0)), pl.BlockSpec((8, dc), lambda b, i: (0, 0))],
        out_shape=[jax.ShapeDtypeStruct((bsz, s, dc), F32), jax.ShapeDtypeStruct((dc, dc), F32),
                   jax.ShapeDtypeStruct((8, dc), F32)],
        compiler_params=_params(2),
    )(dcv, y, ln_w, ln_b, w_pw)


def _conv_backward(dy, ga, gg, conv_w, ts):
    bsz, s, dc = dy.shape
    n_tiles = s // ts

    def body(dy_ref, dyp_ref, dyn_ref, a_ref, g_ref, cw_ref,
             da_ref, dg_ref, gcw_ref, dyext_ref, dysh_ref, ug_ref, dug_ref, gacc_ref):
        b, i = pl.program_id(0), pl.program_id(1)
        av, sg = a_ref[0], _sigmoid(g_ref[0])
        ug_ref[...] = av * sg
        _fill_ext(dyext_ref, dy_ref[0], dyp_ref[0], dyn_ref[0], i, n_tiles, ts)
        _fill_shifted(dysh_ref, dyext_ref, ts)
        gacc_ref[...] = jnp.zeros(gacc_ref.shape, F32)

        def row_block(r, carry):
            r0 = pl.multiple_of(r * CONV_ROWS, CONV_ROWS)
            ugb = ug_ref[pl.ds(r0, CONV_ROWS), :]
            acc = jnp.zeros((CONV_ROWS, dc), F32)
            for j in range(CONV_WIDTH):
                win = _window(dysh_ref, dyext_ref, HALO + CONV_PAD - j, CONV_ROWS, r0)
                acc = acc + cw_ref[j:j + 1, :] * win
                prod = ugb * win
                part = prod[0:8, :]
                for k in range(8, CONV_ROWS, 8):
                    part = part + prod[k:k + 8, :]
                gacc_ref[j] += part
            dug_ref[pl.ds(r0, CONV_ROWS), :] = acc
            return carry

        lax.fori_loop(0, ts // CONV_ROWS, row_block, 0)
        dug = dug_ref[...]
        gcw = jnp.sum(gacc_ref[...], axis=1)
        first = jnp.logical_and(b == 0, i == 0)

        @pl.when(first)
        def _():
            gcw_ref[...] = gcw

        @pl.when(jnp.logical_not(first))
        def _():
            gcw_ref[...] += gcw

        da_ref[0] = (dug * sg).astype(BF16)
        dg_ref[0] = (dug * av * sg * (1.0 - sg)).astype(BF16)

    tok = pl.BlockSpec((1, ts, dc), lambda b, i: (b, i, 0))
    return pl.pallas_call(
        body, name="conv_backward", grid=(bsz, n_tiles),
        in_specs=_halo_specs(dc, ts, s) + [tok, tok, pl.BlockSpec(conv_w.shape, lambda b, i: (0, 0))],
        out_specs=[tok, tok, pl.BlockSpec((32, dc), lambda b, i: (0, 0))],
        out_shape=[jax.ShapeDtypeStruct((bsz, s, dc), BF16), jax.ShapeDtypeStruct((bsz, s, dc), BF16),
                   jax.ShapeDtypeStruct((32, dc), F32)],
        scratch_shapes=[pltpu.VMEM((ts + 2 * HALO, dc), F32), pltpu.VMEM((7, ts + 2 * HALO, dc), F32),
                        pltpu.VMEM((ts, dc), F32), pltpu.VMEM((ts, dc), F32), pltpu.VMEM((32, 8, dc), F32)],
        compiler_params=_params(2),
    )(dy, dy, dy, ga, gg, conv_w)


def _attn_backward(q, k2, v2, kt, o, do, lse, tq, fused):
    bsz, s, _ = q.shape
    sk = k2.shape[2]
    scale = 1.0 / math.sqrt(HEAD_DIM)
    nq = s // tq
    total = bsz * N_KV * nq
    at_steps = [(0, True), (total // 5, True), (total // 2, True), (total - 1, False)]

    def body(*refs):
        (q_ref, k_ref, v_ref, kt_ref, o_ref, do_ref, lse_ref), (dq_ref, dk_ref, dv_ref), _ = _split_fused(
            refs, 7, 3, 0, fused)
        g, i = pl.program_id(1), pl.program_id(2)
        step = (pl.program_id(0) * N_KV + g) * nq + i
        _run_phases(fused, step, at_steps, True)
        kk, vv = k_ref[0, 0], v_ref[0, 0]
        kgt = kt_ref[0, pl.ds(pl.multiple_of(g * HEAD_DIM, HEAD_DIM), HEAD_DIM), :]
        lo_mask = _lo_mask(tq)
        dk_acc = jnp.zeros((HEAD_DIM, sk), F32)
        dv_acc = jnp.zeros((HEAD_DIM, sk), F32)
        for j in range(2):
            cols = slice(128 * j, 128 * (j + 1))
            qp, dop, lsep = q_ref[0, :, cols], do_ref[0, :, cols], lse_ref[0, :, cols]
            dprod = dop * o_ref[0, :, cols]
            q_t = qp.astype(F32).T.astype(BF16)
            do_t = dop.T.astype(BF16)
            dq_t = []
            for half in range(2):
                sel = lo_mask if half == 0 else jnp.logical_not(lo_mask)
                rows = slice(HEAD_DIM * half, HEAD_DIM * (half + 1))
                qs = jnp.where(sel, qp, jnp.zeros_like(qp))
                dos = jnp.where(sel, dop, 0.0).astype(BF16)
                lse_h = jnp.max(jnp.where(sel, lsep, -jnp.inf), axis=-1, keepdims=True)
                delta = jnp.sum(jnp.where(sel, dprod, 0.0), axis=-1, keepdims=True)
                sc = lax.dot_general(qs, kk, NT, preferred_element_type=F32)
                p = jnp.exp(sc - lse_h)
                dp = lax.dot_general(dos, vv, NT, preferred_element_type=F32)
                ds = (p * (dp - delta)).astype(BF16)
                dv_acc = dv_acc + jnp.dot(do_t[rows, :], p.astype(BF16), preferred_element_type=F32)
                dk_acc = dk_acc + jnp.dot(q_t[rows, :], ds, preferred_element_type=F32)
                dq_t.append(lax.dot_general(kgt, ds, NT, preferred_element_type=F32))
            dq_ref[0, :, cols] = (jnp.concatenate(dq_t, axis=0) * scale).T

        @pl.when(i == 0)
        def _():
            dk_ref[0, 0] = dk_acc
            dv_ref[0, 0] = dv_acc

        @pl.when(i > 0)
        def _():
            dk_ref[0, 0] += dk_acc
            dv_ref[0, 0] += dv_acc

        _run_phases(fused, step, at_steps, False)

    q_spec = pl.BlockSpec((1, tq, 256), lambda b, g, i: (b, i, g))
    kv_spec = pl.BlockSpec((1, 1, sk, 128), lambda b, g, i: (b, g, 0, 0))
    acc_spec = pl.BlockSpec((1, 1, HEAD_DIM, sk), lambda b, g, i: (b, g, 0, 0))
    return pl.pallas_call(
        body, name="attn_backward", grid=(bsz, N_KV, nq),
        in_specs=[q_spec, kv_spec, kv_spec, pl.BlockSpec((1, 128, sk), lambda b, g, i: (b, 0, 0)), q_spec, q_spec,
                  q_spec] + fused.in_specs,
        out_specs=[q_spec, acc_spec, acc_spec] + fused.out_specs,
        out_shape=[jax.ShapeDtypeStruct((bsz, s, 512), F32), jax.ShapeDtypeStruct((bsz, 2, HEAD_DIM, sk), F32),
                   jax.ShapeDtypeStruct((bsz, 2, HEAD_DIM, sk), F32)] + fused.out_shape,
        scratch_shapes=fused.scratch,
        compiler_params=_params(3),
    )(q, k2, v2, kt, o, do, lse, *fused.arrs)


def _heads_to_lanes(acc_ref):
    return jnp.concatenate([acc_ref[0, 0], acc_ref[0, 1]], axis=0).T


def _norm_backward(dn, pre, w, bd):
    rstd = lax.rsqrt(_seg_mean(pre * pre, bd) + EPS)
    xhat = pre * rstd
    dxhat = dn * w
    return rstd * (dxhat - xhat * _seg_mean(dxhat * xhat, bd)), dn * xhat


def _qkv_backward(qkv, dq, dk2, dv2, cos, sin, qnw, knw, bd512, bd128, ts):
    bsz, s, _ = qkv.shape

    def body(p_ref, dq_ref, dk_ref, dv_ref, cos_ref, sin_ref, qnw_ref, knw_ref, bd512_ref, bd128_ref, d_ref, gw_ref):
        b, i = pl.program_id(0), pl.program_id(1)
        lo_mask = _lo_mask(ts)
        cos_t, sin_t = cos_ref[...], sin_ref[...]
        dqr = dq_ref[0]
        dqn = dqr * _tile_lanes(cos_t, 4) + _partner(dqr * _tile_lanes(sin_t, 4))
        dqp, gq = _norm_backward(dqn, p_ref[0, :, 0:512], qnw_ref[...], bd512_ref[...])
        dkr = _heads_to_lanes(dk_ref)
        dkn = dkr * cos_t + _partner(dkr * sin_t)
        dkp, gk = _norm_backward(dkn, p_ref[0, :, 512:640], knw_ref[...], bd128_ref[...])
        dvp = _heads_to_lanes(dv_ref)
        d_ref[0] = jnp.concatenate([dqp, dkp, dvp], axis=1).astype(BF16)
        gk512 = jnp.concatenate([jnp.sum(gk, axis=0, keepdims=True), jnp.zeros((1, 384), F32)], axis=1)
        rows = jnp.concatenate([jnp.sum(gq, axis=0, keepdims=True), gk512, jnp.zeros((6, 512), F32)], axis=0)
        first = jnp.logical_and(b == 0, i == 0)

        @pl.when(first)
        def _():
            gw_ref[...] = rows

        @pl.when(jnp.logical_not(first))
        def _():
            gw_ref[...] += rows

    const = lambda a: pl.BlockSpec(a.shape, lambda b, i: (0,) * a.ndim)
    kv_spec = pl.BlockSpec((1, 2, HEAD_DIM, ts), lambda b, i: (b, 0, 0, i))
    return pl.pallas_call(
        body, name="qkv_backward", grid=(bsz, s // ts),
        in_specs=[pl.BlockSpec((1, ts, 768), lambda b, i: (b, i, 0)), pl.BlockSpec((1, ts, 512), lambda b, i: (b, i, 0)),
                  kv_spec, kv_spec, pl.BlockSpec((ts, 128), lambda b, i: (i, 0)),
                  pl.BlockSpec((ts, 128), lambda b, i: (i, 0)), const(qnw), const(knw), const(bd512), const(bd128)],
        out_specs=[pl.BlockSpec((1, ts, 768), lambda b, i: (b, i, 0)), pl.BlockSpec((8, 512), lambda b, i: (0, 0))],
        out_shape=[jax.ShapeDtypeStruct((bsz, s, 768), BF16), jax.ShapeDtypeStruct((8, 512), F32)],
        compiler_params=_params(2),
    )(qkv, dq, dk2, dv2, cos, sin, qnw, knw, bd512, bd128)


def _ctx_kv_backward(pc, dk2, dv2, knw, bd128):
    bsz, cl, _ = pc.shape

    def body(p_ref, dk_ref, dv_ref, knw_ref, bd128_ref, d_ref, gw_ref):
        b = pl.program_id(0)
        lo_mask = _lo_mask(cl)
        dkn = _heads_to_lanes(dk_ref)
        dkp, gk = _norm_backward(dkn, p_ref[0, :, 0:128], knw_ref[...], bd128_ref[...])
        dvp = _heads_to_lanes(dv_ref)
        d_ref[0] = jnp.concatenate([dkp, dvp], axis=1).astype(BF16)
        rows = jnp.concatenate([jnp.sum(gk, axis=0, keepdims=True), jnp.zeros((7, 128), F32)], axis=0)

        @pl.when(b == 0)
        def _():
            gw_ref[...] = rows

        @pl.when(b > 0)
        def _():
            gw_ref[...] += rows

    const = lambda a: pl.BlockSpec(a.shape, lambda b: (0,) * a.ndim)
    blk = dk2.shape[3] // cl - 1
    kv_spec = pl.BlockSpec((1, 2, HEAD_DIM, cl), lambda b: (b, 0, 0, blk))
    return pl.pallas_call(
        body, name="ctx_kv_backward", grid=(bsz,),
        in_specs=[pl.BlockSpec((1, cl, 256), lambda b: (b, 0, 0)), kv_spec, kv_spec, const(knw), const(bd128)],
        out_specs=[pl.BlockSpec((1, cl, 256), lambda b: (b, 0, 0)), pl.BlockSpec((8, 128), lambda b: (0, 0))],
        out_shape=[jax.ShapeDtypeStruct((bsz, cl, 256), BF16), jax.ShapeDtypeStruct((8, 128), F32)],
        compiler_params=_params(1),
    )(pc, dk2, dv2, knw, bd128)


def _weight_grad(parts, u, init, tm, name):
    bsz, s, d = u.shape
    n_p = len(parts)
    nrows = sum(hi - lo for _, lo, hi in parts)

    def body(*refs):
        p_refs, u_ref = refs[:n_p], refs[n_p]
        gi_ref = refs[n_p + 1] if init is not None else None
        gw_ref = refs[-1]
        first = jnp.logical_and(pl.program_id(0) == 0, pl.program_id(1) == 0)
        dp = jnp.concatenate([r[0, :, lo:hi] for r, (_, lo, hi) in zip(p_refs, parts)], axis=1)
        gw = lax.dot_general(dp, u_ref[0], TN, preferred_element_type=F32)

        @pl.when(first)
        def _():
            gw_ref[...] = gw
            if init is not None:
                gw_ref[KV_LO:KV_HI, :] += gi_ref[...]

        @pl.when(jnp.logical_not(first))
        def _():
            gw_ref[...] += gw

    tok = lambda w: pl.BlockSpec((1, tm, w), lambda b, i: (b, i, 0))
    in_specs = [tok(a.shape[2]) for a, _, _ in parts] + [tok(d)]
    args = [a for a, _, _ in parts] + [u]
    if init is not None:
        in_specs.append(pl.BlockSpec(init.shape, lambda b, i: (0, 0)))
        args.append(init)
    return pl.pallas_call(
        body, name=name, grid=(bsz, s // tm), in_specs=in_specs,
        out_specs=pl.BlockSpec((nrows, d), lambda b, i: (0, 0)), out_shape=jax.ShapeDtypeStruct((nrows, d), F32),
        compiler_params=_params(2),
    )(*args)


def _inproj_backward(dps, x, dh, scale1p, norm_w, w_t, tm, name, fused=None):
    bsz, s, d = x.shape
    n_p = len(dps)
    shared = scale1p.shape[0] == 1
    with_dx = dh is not None
    n_in = n_p + (2 if with_dx else 1) + 3
    n_out = 3 if with_dx else 2
    total = bsz * (s // tm)
    at_steps = [(0, True), (total // 8, True), ((3 * total) // 4, True), (total - 1, False)]

    def body(*refs):
        ins, outs, _ = _split_fused(refs, n_in, n_out, 0, fused)
        dp_refs, x_ref = ins[:n_p], ins[n_p]
        dh_ref = ins[n_p + 1] if with_dx else None
        sc_ref, nw_ref, w_ref = ins[-3:]
        mod_ref, gnw_ref = outs[-2:]
        b, i = pl.program_id(0), pl.program_id(1)
        step = b * (s // tm) + i
        _run_phases(fused, step, at_steps, True)
        first = jnp.logical_and(b == 0, i == 0)
        dp = dp_refs[0][0] if n_p == 1 else jnp.concatenate([r[0] for r in dp_refs], axis=1)
        du = jnp.dot(dp, w_ref[...], preferred_element_type=F32)
        xv = x_ref[0]
        rstd = lax.rsqrt(jnp.mean(xv * xv, axis=-1, keepdims=True) + EPS)
        xhat = xv * rstd
        nw, sc = nw_ref[...], sc_ref[0]
        red = lambda v: jnp.sum(v, axis=0, keepdims=True)
        mod_rows = jnp.concatenate([red(du), red(du * (xhat * nw)), jnp.zeros((6, d), F32)], axis=0)
        gnw_rows = jnp.concatenate([red(du * sc * xhat), jnp.zeros((7, d), F32)], axis=0)
        mod_first = first if shared else i == 0

        @pl.when(mod_first)
        def _():
            mod_ref[0] = mod_rows

        @pl.when(jnp.logical_not(mod_first))
        def _():
            mod_ref[0] += mod_rows

        @pl.when(first)
        def _():
            gnw_ref[...] = gnw_rows

        @pl.when(jnp.logical_not(first))
        def _():
            gnw_ref[...] += gnw_rows

        if with_dx:
            dxhat = du * (nw * sc)
            outs[0][0] = dh_ref[0] + rstd * (dxhat - xhat * jnp.mean(dxhat * xhat, axis=-1, keepdims=True))
        _run_phases(fused, step, at_steps, False)

    tok = lambda w: pl.BlockSpec((1, tm, w), lambda b, i: (b, i, 0))
    in_specs = [tok(p.shape[2]) for p in dps] + [tok(d)]
    args = list(dps) + [x]
    if with_dx:
        in_specs.append(tok(d))
        args.append(dh)
    in_specs += [_bcast_spec(scale1p), pl.BlockSpec((1, d), lambda b, i: (0, 0)),
                 pl.BlockSpec(w_t.shape, lambda b, i: (0, 0))]
    args += [scale1p, norm_w, w_t]
    bm = scale1p.shape[0]
    mod_spec = pl.BlockSpec((1, 8, d), (lambda b, i: (0, 0, 0)) if shared else (lambda b, i: (b, 0, 0)))
    out_specs = [mod_spec, pl.BlockSpec((8, d), lambda b, i: (0, 0))]
    out_shape = [jax.ShapeDtypeStruct((bm, 8, d), F32), jax.ShapeDtypeStruct((8, d), F32)]
    if with_dx:
        out_specs.insert(0, tok(d))
        out_shape.insert(0, jax.ShapeDtypeStruct((bsz, s, d), F32))
    scratch = []
    if fused is not None:
        in_specs += fused.in_specs
        args += fused.arrs
        out_specs += fused.out_specs
        out_shape += fused.out_shape
        scratch = fused.scratch
    res = pl.pallas_call(
        body, name=name, grid=(bsz, s // tm), in_specs=in_specs, out_specs=out_specs, out_shape=out_shape,
        scratch_shapes=scratch, compiler_params=_params(2),
    )(*args)
    return list(res) if with_dx else [None] + list(res)


def _adamw_update(w_ref, g_ref, m_ref, v_ref, d_ref, nm_ref, nv_ref):
    gv = g_ref[...]
    mn = ADAM_B1 * m_ref[...] + (1.0 - ADAM_B1) * gv
    vn = ADAM_B2 * v_ref[...] + (1.0 - ADAM_B2) * (gv * gv)
    m_hat = mn / (1.0 - ADAM_B1 ** ADAM_STEP)
    v_hat = vn / (1.0 - ADAM_B2 ** ADAM_STEP)
    d_ref[...] = -ADAM_LR * (m_hat / (jnp.sqrt(v_hat) + ADAM_EPS) + ADAM_WD * w_ref[...])
    nm_ref[...] = mn
    nv_ref[...] = vn


def _adamw_small(ws, gs, ms, vs):
    n = len(ws)

    def body(*refs):
        ins, outs = refs[:4 * n], refs[4 * n:]
        for k in range(n):
            _adamw_update(ins[k], ins[n + k], ins[2 * n + k], ins[3 * n + k], outs[3 * k], outs[3 * k + 1],
                          outs[3 * k + 2])

    res = pl.pallas_call(
        body, name="adamw_small",
        out_shape=[jax.ShapeDtypeStruct(w.shape, F32) for w in ws for _ in range(3)], compiler_params=_params(),
    )(*ws, *gs, *ms, *vs)
    return [tuple(res[3 * k:3 * k + 3]) for k in range(n)]


def _adamw(w, g, m, v, name):
    r, cdim = w.shape
    tr = next((t for t in (512, 352) if r % t == 0 and r > t), r)

    def body(*refs):
        _adamw_update(*refs)

    spec = pl.BlockSpec((tr, cdim), lambda i: (i, 0))
    return pl.pallas_call(
        body, name=name, grid=(r // tr,), in_specs=[spec] * 4, out_specs=[spec] * 3,
        out_shape=[jax.ShapeDtypeStruct((r, cdim), F32)] * 3, compiler_params=_params(1),
    )(w, g, m, v)


def _rope_tables(s):
    rows = s // GRID_W
    freqs = np.float32(ROPE_THETA) ** (-np.arange(0, ROPE_AXIS_DIM, 2, dtype=np.float32) / np.float32(ROPE_AXIS_DIM))
    ang_r = np.arange(rows, dtype=np.float32)[:, None] * freqs[None, :]
    ang_c = np.arange(GRID_W, dtype=np.float32)[:, None] * freqs[None, :]
    zr, zc = np.zeros_like(ang_r), np.zeros_like(ang_c)

    def table(by_row, by_col):
        r = jnp.asarray(np.tile(np.concatenate(by_row + [zr, zr], axis=1), (1, 2)), dtype=F32)
        c = jnp.asarray(np.tile(np.concatenate([zc, zc] + by_col, axis=1), (1, 2)), dtype=F32)
        return jnp.repeat(r, GRID_W, axis=0) + jnp.tile(c, (rows, 1))

    return (table([np.cos(ang_r)] * 2, [np.cos(ang_c)] * 2),
            table([-np.sin(ang_r), np.sin(ang_r)], [-np.sin(ang_c), np.sin(ang_c)]))


def _pack_rows(parts, rows):
    flat = jnp.concatenate([p.reshape(-1) for p in parts])
    return jnp.pad(flat, (0, rows * D_MODEL - flat.shape[0])).reshape(rows, D_MODEL)


def kernel(x, c, ctx, c_ctx, w_mod, b_mod, norm_w, w_in, q_norm_w, k_norm_w, conv_w, conv_b, conv_ln_w, conv_ln_b, w_pw, b_pw, w_out, loss_target, m_c_ctx, m_w_mod, m_b_mod, m_norm_w, m_w_in, m_q_norm_w, m_k_norm_w, m_conv_w, m_conv_b, m_conv_ln_w, m_conv_ln_b, m_w_pw, m_b_pw, m_w_out, v_c_ctx, v_w_mod, v_b_mod, v_norm_w, v_w_in, v_q_norm_w, v_k_norm_w, v_conv_w, v_conv_b, v_conv_ln_w, v_conv_ln_b, v_w_pw, v_b_pw, v_w_out):
    bsz, s, d = x.shape
    cl = ctx.shape[1]
    xi, yi, ci = lax.axis_index("x"), lax.axis_index("y"), lax.axis_index("c")
    chip = 2 * xi + yi
    dev = 2 * chip + ci
    ncol_mod = w_mod.shape[2]

    w_in_t_loc = w_in[0].T.astype(BF16)
    b_cols = lax.dynamic_slice(b_mod, (0, chip * ncol_mod), (1, ncol_mod))
    sc_rows, mod_g, g_in = _front(jnp.pad(c, ((0, 8 - bsz), (0, 0))), jnp.pad(c_ctx[None, :], ((0, 15), (0, 0))),
                                  w_mod[0], b_cols, w_in_t_loc)
    w_in_t = g_in.reshape(D_IN, d)
    mod_all = mod_g.transpose(1, 0, 2).reshape(80, 3 * d)
    mod_loc = lax.dynamic_slice(mod_all, (8 * dev, 0), (bsz, 3 * d))
    shift, scale1p, gate = mod_loc[:, None, :d], 1.0 + mod_loc[:, None, d:2 * d], mod_loc[:, None, 2 * d:]
    shift_c, scale1p_c = mod_all[64:65, :d][None], 1.0 + mod_all[64:65, d:2 * d][None]

    cos, sin = _rope_tables(s)
    qnw512 = jnp.tile(q_norm_w, (1, 8))
    knw128 = jnp.tile(k_norm_w, (1, 2))
    bd512 = jnp.kron(jnp.eye(8, dtype=F32), jnp.ones((HEAD_DIM, HEAD_DIM), F32)).astype(BF16)
    bd128 = bd512[:128, :128]

    tt, ti = min(TILE_TOKENS, s), min(TILE_INPROJ, s)
    u, p_qkv, p_za, p_ga, p_gg, p_zc = _norm_inproj(x, shift, scale1p, norm_w, w_in_t, SPLITS, ti, "norm_inproj")
    uc, pc_kv = _norm_inproj(ctx, shift_c, scale1p_c, norm_w, w_in_t[KV_LO:KV_HI], ((0, 256),), cl, "ctx_norm_inproj")
    q, k2x, v2x, ktx, vtx = _qkv_prep(p_qkv, cos, sin, qnw512, knw128, bd512, bd128, tt, cl)
    k2, v2, kt, vt = _ctx_kv_prep(pc_kv, knw128, bd128, k2x, v2x, ktx, vtx)
    conv_w_loc = jnp.pad(conv_w[0], ((0, 1), (0, 0)))
    o, lse, g_out, g_pw, g_cw = _attn_forward(
        q, k2, vt, min(TILE_ATTN_FWD, s), _ChipGather([w_out[0].astype(BF16), w_pw[0].astype(BF16), conv_w_loc]))
    w_out_f = g_out.reshape(d, d)
    w_pw_f = g_pw.reshape(D_CONV, D_CONV)
    conv_w_f = g_cw.transpose(1, 0, 2).reshape(32, D_CONV)
    y, cv = _conv_forward(p_ga, p_gg, conv_w_f, conv_b, conv_ln_w, conv_ln_b, w_pw_f, b_pw, tt)
    loss_part, dh, do, dza, dy, dzc, dgate, gw_out, gw_pw, conv_stats = _outproj_loss(
        x, loss_target, gate, o, p_za, cv, p_zc, w_out_f, y, conv_ln_w, conv_ln_b, w_pw_f, tt)

    all_chips, half_rows = (0, 1, 2, 3), D_IN // 2
    da, dg, gcw = _conv_backward(dy, p_ga, p_gg, conv_w_f, tt)
    gw_hi = _weight_grad([(da, half_rows - SPLITS[2][0], 512), (dg, 0, 512), (dzc, 0, 512)], u, None, ti,
                         "grad_in_rows_hi")
    dq, dkt, dvt, r_out, r_pw, r_hi = _attn_backward(
        q, k2, v2, kt, o, do, lse, min(TILE_ATTN_BWD, s),
        _FusedReduce([(gw_out, all_chips), (gw_pw, all_chips), (gw_hi, (2, 3))]))
    dqkv, qk_stats = _qkv_backward(p_qkv, dq, dkt, dvt, cos, sin, qnw512, knw128, bd512, bd128, tt)
    dpc, kc_stats = _ctx_kv_backward(pc_kv, dkt, dvt, knw128, bd128)
    gw_ctx = _weight_grad([(dpc, 0, 256)], uc, None, cl, "grad_in_rows_ctx")
    gw_lo = _weight_grad([(dqkv, 0, 768), (dza, 0, 512), (da, 0, half_rows - SPLITS[2][0])], u, gw_ctx, ti,
                         "grad_in_rows_lo")
    _, modc, gnw_c = _inproj_backward([dpc], ctx, None, scale1p_c, norm_w, w_in_t[KV_LO:KV_HI], cl,
                                      "ctx_inproj_backward")
    grad_x, modx, gnw_x, r_lo = _inproj_backward(
        [dqkv, dza, da, dg, dzc], x, dh, scale1p, norm_w, w_in_t, tt, "inproj_backward",
        _FusedReduce([(gw_lo, (0, 1))]))
    g_w_out, g_w_pw = r_out.reshape(d // 4, d), r_pw.reshape(D_CONV // 4, D_CONV)
    g_w_in_t = jnp.where(chip < 2, r_lo, r_hi).reshape(D_IN // 4, d)

    dmod_loc = jnp.concatenate([modx[:, 0, :], modx[:, 1, :], dgate[:, 0, :]], axis=1)
    gq = qk_stats[0].reshape(8, HEAD_DIM).sum(axis=0)
    gk = (qk_stats[1, :128] + kc_stats[0]).reshape(2, HEAD_DIM).sum(axis=0)
    packed = _pack_rows([dmod_loc, dmod_loc.sum(axis=0), gnw_x[0] + gnw_c[0], modc[0, 0], modc[0, 1], gq, gk,
                         conv_stats[0], conv_stats[1], conv_stats[2], conv_stats[3], gcw,
                         jnp.sum(loss_part[:, 0, 0])[None]], 32)
    total, g_w_mod, dsilu_ctx = _tail_exchange(packed, sc_rows, w_mod[0], bsz, 3 * bsz + 4)
    flat = total.reshape(-1)
    offs = [0]

    def take(nelem):
        lo = offs[0]
        offs[0] = lo + nelem
        return flat[lo:lo + nelem]

    take(bsz * 3 * d)
    g_b_mod_x = take(3 * d)
    g_norm_w = take(d)
    dshift_c, dscale_c = take(d), take(d)
    g_qnw, g_knw = take(HEAD_DIM), take(HEAD_DIM)
    g_b_pw, g_ln_w, g_ln_b, g_conv_b = take(D_CONV), take(D_CONV), take(D_CONV), take(D_CONV)
    g_conv_w_full = take(32 * D_CONV).reshape(32, D_CONV)
    loss = take(1)[0] * (0.5 / d)

    dmod_c = jnp.concatenate([dshift_c, dscale_c, jnp.zeros((d,), F32)])
    g_b_mod = (g_b_mod_x + dmod_c)[None, :]
    sg = _sigmoid(c_ctx)
    g_c_ctx = dsilu_ctx[0] * (sg * (1.0 + c_ctx * (1.0 - sg)))

    g_w_in = g_w_in_t.T
    g_conv_w = lax.dynamic_slice(g_conv_w_full, (0, chip * 128), (CONV_WIDTH, 128))

    grads = {
        "c_ctx": g_c_ctx, "w_mod": g_w_mod[None], "b_mod": g_b_mod, "norm_w": g_norm_w[None], "w_in": g_w_in[None],
        "q_norm_w": g_qnw[None], "k_norm_w": g_knw[None], "conv_w": g_conv_w[None], "conv_b": g_conv_b[None],
        "conv_ln_w": g_ln_w[None], "conv_ln_b": g_ln_b[None], "w_pw": g_w_pw[None], "b_pw": g_b_pw[None],
        "w_out": g_w_out[None],
    }
    weights = {
        "c_ctx": (c_ctx, m_c_ctx, v_c_ctx), "w_mod": (w_mod, m_w_mod, v_w_mod), "b_mod": (b_mod, m_b_mod, v_b_mod),
        "norm_w": (norm_w, m_norm_w, v_norm_w), "w_in": (w_in, m_w_in, v_w_in),
        "q_norm_w": (q_norm_w, m_q_norm_w, v_q_norm_w), "k_norm_w": (k_norm_w, m_k_norm_w, v_k_norm_w),
        "conv_w": (conv_w, m_conv_w, v_conv_w), "conv_b": (conv_b, m_conv_b, v_conv_b),
        "conv_ln_w": (conv_ln_w, m_conv_ln_w, v_conv_ln_w), "conv_ln_b": (conv_ln_b, m_conv_ln_b, v_conv_ln_b),
        "w_pw": (w_pw, m_w_pw, v_w_pw), "b_pw": (b_pw, m_b_pw, v_b_pw), "w_out": (w_out, m_w_out, v_w_out),
    }
    names = list(weights)
    big = ("w_mod", "w_in")
    as_2d = lambda a: a.reshape((1, a.shape[0]) if a.ndim == 1 else (a.shape[-2] if a.ndim == 3 else 1, a.shape[-1]))
    small = [n for n in names if n not in big]
    w_g_m_v = zip(*[[as_2d(a) for a in (weights[n][0], grads[n], weights[n][1], weights[n][2])] for n in small])
    updates = dict(zip(small, _adamw_small(*[list(col) for col in w_g_m_v])))
    w, m, v = weights["w_mod"]
    updates["w_mod"] = _adamw(as_2d(w), as_2d(grads["w_mod"]), as_2d(m), as_2d(v), "adamw_w_mod")
    w, m, v = weights["w_in"]
    updates["w_in"] = tuple(r.T for r in _adamw(w[0].T, g_w_in_t, m[0].T, v[0].T, "adamw_w_in"))
    deltas, new_ms, new_vs = ([updates[n][k].reshape(weights[n][0].shape) for n in names] for k in range(3))
    grads = {n: grads[n].reshape(weights[n][0].shape) for n in names}

    return (loss, grad_x, *[grads[n] for n in names], *deltas, *new_ms, *new_vs)
```
